```python
import jax, jax.numpy as jnp
from jax import lax
import numpy as np

D_MODEL = 1024
BATCH = 8
SEQ = 8192
DEPTH = 2

POOL_WINDOWS = (2, 4, 8, 16)
POOL_GROUPS = 4
POOL_WIDTH = D_MODEL // 2
POOL_GROUP_DIM = POOL_WIDTH // POOL_GROUPS
CONV_WIDTH = D_MODEL // 2
CONV_KERNEL = 31
N_HEADS = 8
HEAD_DIM = 64
ATTN_WIDTH = N_HEADS * HEAD_DIM
Q_BLOCK = 128
N_BRANCHES = 3
IN_SIZES = (POOL_WIDTH, POOL_WIDTH, 2 * CONV_WIDTH, CONV_WIDTH, 3 * ATTN_WIDTH, ATTN_WIDTH, N_BRANCHES * D_MODEL)
IN_WIDTH = sum(IN_SIZES)
RMS_EPS = 1e-6
LN_EPS = 1e-5

kernel_name = "hybrid_pool_conv_stickbreak_gated_block"


def rms_norm(x, g):
    xf = x.astype(jnp.float32)
    y = xf * lax.rsqrt(jnp.mean(xf * xf, axis=-1, keepdims=True) + RMS_EPS)
    return (y * g.astype(jnp.float32)).astype(x.dtype)


def layer_norm(x, g, b):
    xf = x.astype(jnp.float32)
    mu = jnp.mean(xf, axis=-1, keepdims=True)
    var = jnp.mean(jnp.square(xf - mu), axis=-1, keepdims=True)
    y = (xf - mu) * lax.rsqrt(var + LN_EPS)
    return (y * g.astype(jnp.float32) + b.astype(jnp.float32)).astype(x.dtype)


def multiscale_pool(u, pool_w, pool_b, pool_scale):
    B, T, C = u.shape
    uf = u.astype(jnp.float32)
    cs = jnp.cumsum(uf, axis=1)
    pos = jnp.arange(T)
    diffs = []
    for g, w in enumerate(POOL_WINDOWS):
        sl = slice(g * POOL_GROUP_DIM, (g + 1) * POOL_GROUP_DIM)
        c = cs[..., sl]
        lagged = jnp.pad(c, ((0, 0), (w, 0), (0, 0)))[:, :T]
        count = jnp.minimum(pos + 1, w).astype(jnp.float32)[None, :, None]
        diffs.append((c - lagged) / count - uf[..., sl])
    d = jnp.stack(diffs, axis=2).astype(u.dtype)
    y = jnp.einsum('btgc,gcd->btgd', d, pool_w) + pool_b
    return y.reshape(B, T, C) * pool_scale


def conformer_conv(c2, conv_w, conv_b, ln_g, ln_b):
    a, b = jnp.split(c2, 2, axis=-1)
    u = a * jax.nn.sigmoid(b)
    u = lax.conv_general_dilated(
        u, conv_w[:, None, :].astype(u.dtype), window_strides=(1,),
        padding=[(CONV_KERNEL - 1, 0)], dimension_numbers=('NWC', 'WIO', 'NWC'),
        feature_group_count=CONV_WIDTH) + conv_b
    u = layer_norm(u, ln_g, ln_b)
    return jax.nn.silu(u)


def stick_breaking_attention(q, k, v):
    B, H, T, Dh = q.shape
    nb = T // Q_BLOCK
    scale = 1.0 / np.sqrt(Dh).astype(np.float32)
    qb = q.reshape(B, H, nb, Q_BLOCK, Dh).transpose(2, 0, 1, 3, 4)
    kf = k.astype(jnp.float32)
    vf = v.astype(jnp.float32)
    key_pos = jnp.arange(T)

    def block(args):
        q_blk, i = args
        z = jnp.einsum('bhqd,bhkd->bhqk', q_blk.astype(jnp.float32), kf) * scale
        q_pos = i * Q_BLOCK + jnp.arange(Q_BLOCK)
        mask = key_pos[None, :] < q_pos[:, None]
        log_beta = jax.nn.log_sigmoid(z)
        log_1m = jnp.where(mask, jax.nn.log_sigmoid(-z), 0.0)
        between = lax.cumsum(log_1m, axis=3, reverse=True) - log_1m
        w = jnp.where(mask, jnp.exp(log_beta + between), 0.0)
        return jnp.einsum('bhqk,bhkd->bhqd', w, vf)

    out = lax.map(block, (qb, jnp.arange(nb)))
    return out.transpose(1, 2, 0, 3, 4).reshape(B, H, T, Dh).astype(v.dtype)


def _fwd_setup_inputs(seed: int = 0) -> dict:
    key = jax.random.key(seed)
    ks = jax.random.split(key, 16)
    f32 = jnp.float32
    L, D = DEPTH, D_MODEL
    nrm = lambda k, shape, fan_in: jax.random.normal(k, shape, f32) * (fan_in ** -0.5)
    return {
        "x": jax.random.normal(ks[0], (BATCH, SEQ, D), f32),
        "norm_pre": 1.0 + 0.02 * jax.random.normal(ks[1], (L, D), f32),
        "w_in": nrm(ks[2], (L, D, IN_WIDTH), D),
        "pool_w": nrm(ks[3], (L, POOL_GROUPS, POOL_GROUP_DIM, POOL_GROUP_DIM), POOL_GROUP_DIM),
        "pool_b": 0.02 * jax.random.normal(ks[4], (L, POOL_GROUPS, POOL_GROUP_DIM), f32),
        "pool_scale": 1.0 + 0.02 * jax.random.normal(ks[5], (L, POOL_WIDTH), f32),
        "w_pool_out": nrm(ks[6], (L, POOL_WIDTH, D), POOL_WIDTH),
        "conv_w": nrm(ks[7], (L, CONV_KERNEL, CONV_WIDTH), CONV_KERNEL),
        "conv_b": 0.02 * jax.random.normal(ks[8], (L, CONV_WIDTH), f32),
        "conv_ln_g": 1.0 + 0.02 * jax.random.normal(ks[9], (L, CONV_WIDTH), f32),
        "conv_ln_b": 0.02 * jax.random.normal(ks[10], (L, CONV_WIDTH), f32),
        "w_conv_out": nrm(ks[11], (L, CONV_WIDTH, D), CONV_WIDTH),
        "w_attn_out": nrm(ks[12], (L, ATTN_WIDTH, D), ATTN_WIDTH),
        "w_o": nrm(ks[13], (L, D, D), D),
        "norm_post": 1.0 + 0.02 * jax.random.normal(ks[14], (L, D), f32),
    }


def _fwd_reference(x, norm_pre, w_in, pool_w, pool_b, pool_scale, w_pool_out, conv_w, conv_b,
              conv_ln_g, conv_ln_b, w_conv_out, w_attn_out, w_o, norm_post):
    B, T, D = x.shape
    split_at = list(np.cumsum(IN_SIZES)[:-1])
    for l in range(DEPTH):
        h = rms_norm(x, norm_pre[l])
        proj = jnp.einsum('btd,de->bte', h, w_in[l])
        p, gp, c2, gc, qkv, ga, gm = jnp.split(proj, split_at, axis=-1)

        ya = multiscale_pool(p, pool_w[l], pool_b[l], pool_scale[l]) * jax.nn.silu(gp)
        ya = ya @ w_pool_out[l]

        yb = conformer_conv(c2, conv_w[l], conv_b[l], conv_ln_g[l], conv_ln_b[l]) * jax.nn.silu(gc)
        yb = yb @ w_conv_out[l]

        q, k, v = jnp.split(qkv.reshape(B, T, 3, N_HEADS, HEAD_DIM).transpose(2, 0, 3, 1, 4), 3, axis=0)
        o = stick_breaking_attention(q[0], k[0], v[0])
        o = o.transpose(0, 2, 1, 3).reshape(B, T, ATTN_WIDTH) * jax.nn.silu(ga)
        yc = o @ w_attn_out[l]

        g = jax.nn.sigmoid(gm.reshape(B, T, N_BRANCHES, D))
        m = g[:, :, 0] * ya + g[:, :, 1] * yb + g[:, :, 2] * yc
        out = m @ w_o[l]
        x = x + rms_norm(out, norm_post[l])
    return x


import jax as _jax
import jax.numpy as _jnp

TWIN_FORMAT = 'train_step'
FWD_PARAMS = ['x', 'norm_pre', 'w_in', 'pool_w', 'pool_b', 'pool_scale', 'w_pool_out', 'conv_w', 'conv_b', 'conv_ln_g', 'conv_ln_b', 'w_conv_out', 'w_attn_out', 'w_o', 'norm_post']
TWIN_WEIGHTS = ['norm_pre', 'w_in', 'pool_w', 'pool_b', 'pool_scale', 'w_pool_out', 'conv_w', 'conv_b', 'conv_ln_g', 'conv_ln_b', 'w_conv_out', 'w_attn_out', 'w_o', 'norm_post']
TWIN_DIFF_INPUT = 'x'
TWIN_INPUTS = ['x', 'norm_pre', 'w_in', 'pool_w', 'pool_b', 'pool_scale', 'w_pool_out', 'conv_w', 'conv_b', 'conv_ln_g', 'conv_ln_b', 'w_conv_out', 'w_attn_out', 'w_o', 'norm_post', 'loss_target', 'm_norm_pre', 'm_w_in', 'm_pool_w', 'm_pool_b', 'm_pool_scale', 'm_w_pool_out', 'm_conv_w', 'm_conv_b', 'm_conv_ln_g', 'm_conv_ln_b', 'm_w_conv_out', 'm_w_attn_out', 'm_w_o', 'm_norm_post', 'v_norm_pre', 'v_w_in', 'v_pool_w', 'v_pool_b', 'v_pool_scale', 'v_w_pool_out', 'v_conv_w', 'v_conv_b', 'v_conv_ln_g', 'v_conv_ln_b', 'v_w_conv_out', 'v_w_attn_out', 'v_w_o', 'v_norm_post']
TWIN_OUTPUTS = ['loss', 'grad_x', 'grad_norm_pre', 'grad_w_in', 'grad_pool_w', 'grad_pool_b', 'grad_pool_scale', 'grad_w_pool_out', 'grad_conv_w', 'grad_conv_b', 'grad_conv_ln_g', 'grad_conv_ln_b', 'grad_w_conv_out', 'grad_w_attn_out', 'grad_w_o', 'grad_norm_post', 'delta_norm_pre', 'delta_w_in', 'delta_pool_w', 'delta_pool_b', 'delta_pool_scale', 'delta_w_pool_out', 'delta_conv_w', 'delta_conv_b', 'delta_conv_ln_g', 'delta_conv_ln_b', 'delta_w_conv_out', 'delta_w_attn_out', 'delta_w_o', 'delta_norm_post', 'new_m_norm_pre', 'new_m_w_in', 'new_m_pool_w', 'new_m_pool_b', 'new_m_pool_scale', 'new_m_w_pool_out', 'new_m_conv_w', 'new_m_conv_b', 'new_m_conv_ln_g', 'new_m_conv_ln_b', 'new_m_w_conv_out', 'new_m_w_attn_out', 'new_m_w_o', 'new_m_norm_post', 'new_v_norm_pre', 'new_v_w_in', 'new_v_pool_w', 'new_v_pool_b', 'new_v_pool_scale', 'new_v_w_pool_out', 'new_v_conv_w', 'new_v_conv_b', 'new_v_conv_ln_g', 'new_v_conv_ln_b', 'new_v_w_conv_out', 'new_v_w_attn_out', 'new_v_w_o', 'new_v_norm_post']
TWIN_LEAF_KINDS = {'loss': 'loss', 'grad_x': 'grad_x', 'grad_norm_pre': 'grad_w', 'grad_w_in': 'grad_w', 'grad_pool_w': 'grad_w', 'grad_pool_b': 'grad_w', 'grad_pool_scale': 'grad_w', 'grad_w_pool_out': 'grad_w', 'grad_conv_w': 'grad_w', 'grad_conv_b': 'grad_w', 'grad_conv_ln_g': 'grad_w', 'grad_conv_ln_b': 'grad_w', 'grad_w_conv_out': 'grad_w', 'grad_w_attn_out': 'grad_w', 'grad_w_o': 'grad_w', 'grad_norm_post': 'grad_w', 'delta_norm_pre': 'delta_w', 'delta_w_in': 'delta_w', 'delta_pool_w': 'delta_w', 'delta_pool_b': 'delta_w', 'delta_pool_scale': 'delta_w', 'delta_w_pool_out': 'delta_w', 'delta_conv_w': 'delta_w', 'delta_conv_b': 'delta_w', 'delta_conv_ln_g': 'delta_w', 'delta_conv_ln_b': 'delta_w', 'delta_w_conv_out': 'delta_w', 'delta_w_attn_out': 'delta_w', 'delta_w_o': 'delta_w', 'delta_norm_post': 'delta_w', 'new_m_norm_pre': 'new_m', 'new_m_w_in': 'new_m', 'new_m_pool_w': 'new_m', 'new_m_pool_b': 'new_m', 'new_m_pool_scale': 'new_m', 'new_m_w_pool_out': 'new_m', 'new_m_conv_w': 'new_m', 'new_m_conv_b': 'new_m', 'new_m_conv_ln_g': 'new_m', 'new_m_conv_ln_b': 'new_m', 'new_m_w_conv_out': 'new_m', 'new_m_w_attn_out': 'new_m', 'new_m_w_o': 'new_m', 'new_m_norm_post': 'new_m', 'new_v_norm_pre': 'new_v', 'new_v_w_in': 'new_v', 'new_v_pool_w': 'new_v', 'new_v_pool_b': 'new_v', 'new_v_pool_scale': 'new_v', 'new_v_w_pool_out': 'new_v', 'new_v_conv_w': 'new_v', 'new_v_conv_b': 'new_v', 'new_v_conv_ln_g': 'new_v', 'new_v_conv_ln_b': 'new_v', 'new_v_w_conv_out': 'new_v', 'new_v_w_attn_out': 'new_v', 'new_v_w_o': 'new_v', 'new_v_norm_post': 'new_v'}


def _forward(args):
    return _fwd_reference(*[args[k] for k in FWD_PARAMS])


def _output_shape():
    def fwd():
        inp = _fwd_setup_inputs(0)
        return _fwd_reference(*[inp[k] for k in FWD_PARAMS])
    out = _jax.eval_shape(fwd)
    return out.shape, out.dtype

N_MICROBATCH = 1
ADAM_LR = 0.001
ADAM_B1 = 0.9
ADAM_B2 = 0.999
ADAM_EPS = 1e-08
ADAM_WD = 0.01
ADAM_STEP = 10
PER_EXAMPLE_BATCH_AXIS = {'x': 0, 'loss_target': 0}
SHARED_INPUTS = []
_WEIGHT_DTYPES = {'norm_pre': _jnp.float32, 'w_in': _jnp.float32, 'pool_w': _jnp.float32, 'pool_b': _jnp.float32, 'pool_scale': _jnp.float32, 'w_pool_out': _jnp.float32, 'conv_w': _jnp.float32, 'conv_b': _jnp.float32, 'conv_ln_g': _jnp.float32, 'conv_ln_b': _jnp.float32, 'w_conv_out': _jnp.float32, 'w_attn_out': _jnp.float32, 'w_o': _jnp.float32, 'norm_post': _jnp.float32}
MOMENT_SCALE = {'norm_pre': 9.327641e-01, 'w_in': 3.375526e-01, 'pool_w': 6.962177e-01, 'pool_b': 2.151964e+00, 'pool_scale': 7.194253e-01, 'w_pool_out': 5.141188e-01, 'conv_w': 3.997602e-01, 'conv_b': 1.387204e+00, 'conv_ln_g': 8.123390e-01, 'conv_ln_b': 1.031817e+00, 'w_conv_out': 3.591783e-01, 'w_attn_out': 2.947581e-01, 'w_o': 7.196959e-01, 'norm_post': 6.399735e+01}


def _to_microbatches(a, axis):
    t = _jnp.moveaxis(a, axis, 0)
    t = t.reshape((N_MICROBATCH, t.shape[0] // N_MICROBATCH) + t.shape[1:])
    return _jnp.moveaxis(t, 1, axis + 1)


def setup_inputs(seed: int = 0) -> dict:
    inp = _fwd_setup_inputs(seed)
    key = _jax.random.fold_in(_jax.random.key(seed), 7919)
    shape, _ = _output_shape()
    out = dict(inp)
    out["loss_target"] = _jax.random.normal(_jax.random.fold_in(key, 0), shape, _jnp.float32)
    for i, name in enumerate(TWIN_WEIGHTS):
        w = inp[name].astype(_jnp.float32)
        if MOMENT_SCALE is None:
            s = _jnp.sqrt(_jnp.mean(_jnp.square(w)) + 1e-30)
        else:
            s = MOMENT_SCALE[name]
        km, kv = _jax.random.split(_jax.random.fold_in(key, i + 1))
        out[name] = w
        out["m_" + name] = s * _jax.random.normal(km, w.shape, _jnp.float32)
        out["v_" + name] = (s * s) * _jax.random.uniform(kv, w.shape, _jnp.float32, 0.5, 1.5)
    if N_MICROBATCH > 1:
        for name, axis in PER_EXAMPLE_BATCH_AXIS.items():
            out[name] = _to_microbatches(out[name], axis)
    return {'x': out['x'], 'norm_pre': out['norm_pre'], 'w_in': out['w_in'], 'pool_w': out['pool_w'], 'pool_b': out['pool_b'], 'pool_scale': out['pool_scale'], 'w_pool_out': out['w_pool_out'], 'conv_w': out['conv_w'], 'conv_b': out['conv_b'], 'conv_ln_g': out['conv_ln_g'], 'conv_ln_b': out['conv_ln_b'], 'w_conv_out': out['w_conv_out'], 'w_attn_out': out['w_attn_out'], 'w_o': out['w_o'], 'norm_post': out['norm_post'], 'loss_target': out['loss_target'], 'm_norm_pre': out['m_norm_pre'], 'm_w_in': out['m_w_in'], 'm_pool_w': out['m_pool_w'], 'm_pool_b': out['m_pool_b'], 'm_pool_scale': out['m_pool_scale'], 'm_w_pool_out': out['m_w_pool_out'], 'm_conv_w': out['m_conv_w'], 'm_conv_b': out['m_conv_b'], 'm_conv_ln_g': out['m_conv_ln_g'], 'm_conv_ln_b': out['m_conv_ln_b'], 'm_w_conv_out': out['m_w_conv_out'], 'm_w_attn_out': out['m_w_attn_out'], 'm_w_o': out['m_w_o'], 'm_norm_post': out['m_norm_post'], 'v_norm_pre': out['v_norm_pre'], 'v_w_in': out['v_w_in'], 'v_pool_w': out['v_pool_w'], 'v_pool_b': out['v_pool_b'], 'v_pool_scale': out['v_pool_scale'], 'v_w_pool_out': out['v_w_pool_out'], 'v_conv_w': out['v_conv_w'], 'v_conv_b': out['v_conv_b'], 'v_conv_ln_g': out['v_conv_ln_g'], 'v_conv_ln_b': out['v_conv_ln_b'], 'v_w_conv_out': out['v_w_conv_out'], 'v_w_attn_out': out['v_w_attn_out'], 'v_w_o': out['v_w_o'], 'v_norm_post': out['v_norm_post']}


def _loss(weights, diff, rest, loss_target):
    with _jax.named_scope("forward"):
        args = {**rest, TWIN_DIFF_INPUT: diff, **{k: w.astype(_WEIGHT_DTYPES[k]) for k, w in weights.items()}}
        y = _forward(args)
    with _jax.named_scope("loss_head"):
        err = _jnp.square(y.astype(_jnp.float32) - loss_target)
        return 0.5 * _jnp.sum(_jnp.mean(err, axis=-1)) if err.ndim else 0.5 * err


def _adamw(w, g, m, v):
    m = ADAM_B1 * m + (1.0 - ADAM_B1) * g
    v = ADAM_B2 * v + (1.0 - ADAM_B2) * _jnp.square(g)
    m_hat = m / (1.0 - ADAM_B1 ** ADAM_STEP)
    v_hat = v / (1.0 - ADAM_B2 ** ADAM_STEP)
    delta = -ADAM_LR * (m_hat / (_jnp.sqrt(v_hat) + ADAM_EPS) + ADAM_WD * w)
    return delta, m, v


def reference(x, norm_pre, w_in, pool_w, pool_b, pool_scale, w_pool_out, conv_w, conv_b, conv_ln_g, conv_ln_b, w_conv_out, w_attn_out, w_o, norm_post, loss_target, m_norm_pre, m_w_in, m_pool_w, m_pool_b, m_pool_scale, m_w_pool_out, m_conv_w, m_conv_b, m_conv_ln_g, m_conv_ln_b, m_w_conv_out, m_w_attn_out, m_w_o, m_norm_post, v_norm_pre, v_w_in, v_pool_w, v_pool_b, v_pool_scale, v_w_pool_out, v_conv_w, v_conv_b, v_conv_ln_g, v_conv_ln_b, v_w_conv_out, v_w_attn_out, v_w_o, v_norm_post):
    given = dict(x=x, norm_pre=norm_pre, w_in=w_in, pool_w=pool_w, pool_b=pool_b, pool_scale=pool_scale, w_pool_out=w_pool_out, conv_w=conv_w, conv_b=conv_b, conv_ln_g=conv_ln_g, conv_ln_b=conv_ln_b, w_conv_out=w_conv_out, w_attn_out=w_attn_out, w_o=w_o, norm_post=norm_post, loss_target=loss_target, m_norm_pre=m_norm_pre, m_w_in=m_w_in, m_pool_w=m_pool_w, m_pool_b=m_pool_b, m_pool_scale=m_pool_scale, m_w_pool_out=m_w_pool_out, m_conv_w=m_conv_w, m_conv_b=m_conv_b, m_conv_ln_g=m_conv_ln_g, m_conv_ln_b=m_conv_ln_b, m_w_conv_out=m_w_conv_out, m_w_attn_out=m_w_attn_out, m_w_o=m_w_o, m_norm_post=m_norm_post, v_norm_pre=v_norm_pre, v_w_in=v_w_in, v_pool_w=v_pool_w, v_pool_b=v_pool_b, v_pool_scale=v_pool_scale, v_w_pool_out=v_w_pool_out, v_conv_w=v_conv_w, v_conv_b=v_conv_b, v_conv_ln_g=v_conv_ln_g, v_conv_ln_b=v_conv_ln_b, v_w_conv_out=v_w_conv_out, v_w_attn_out=v_w_attn_out, v_w_o=v_w_o, v_norm_post=v_norm_post)
    weights = {n: given[n] for n in TWIN_WEIGHTS}
    shared = {n: given[n] for n in SHARED_INPUTS}
    per_example = {n: given[n] for n in ['x']}
    grad_fn = _jax.value_and_grad(_loss, argnums=(0, 1))

    def one_microbatch(ex, loss_target):
        ex = dict(ex)
        diff = ex.pop(TWIN_DIFF_INPUT)
        return grad_fn(weights, diff, {**shared, **ex}, loss_target)

    if N_MICROBATCH == 1:
        loss, (grad_w, grad_x) = one_microbatch(per_example, given["loss_target"])
    else:
        def body(carry, xs):
            loss_sum, grad_sum = carry
            l_k, (gw_k, gx_k) = one_microbatch(xs[0], xs[1])
            with _jax.named_scope("update"):
                return (loss_sum + l_k, _jax.tree.map(_jnp.add, grad_sum, gw_k)), gx_k

        init = (_jnp.zeros((), _jnp.float32), _jax.tree.map(_jnp.zeros_like, weights))
        (loss, grad_w), grad_x = _jax.lax.scan(body, init, (per_example, given["loss_target"]))
    with _jax.named_scope("update"):
        delta_w, new_m, new_v = {}, {}, {}
        for n in TWIN_WEIGHTS:
            delta_w[n], new_m[n], new_v[n] = _adamw(weights[n], grad_w[n], given["m_" + n], given["v_" + n])
    return (loss, grad_x, *[grad_w[n] for n in TWIN_WEIGHTS], *[delta_w[n] for n in TWIN_WEIGHTS],
            *[new_m[n] for n in TWIN_WEIGHTS], *[new_v[n] for n in TWIN_WEIGHTS])
```

```python
import functools

import jax
import jax.numpy as jnp
from jax import lax
from jax.experimental import pallas as pl
from jax.experimental.pallas import tpu as pltpu

F32 = jnp.float32
BF16 = jnp.bfloat16

D_MODEL = 1024
UNIT = 512
N_UNITS = 15
IN_WIDTH = UNIT * N_UNITS
N_HEADS = 8
HEAD_DIM = 64
HEADS_PER_BLOCK = 2
N_HEAD_BLOCKS = N_HEADS // HEADS_PER_BLOCK
LANES = 128
CONV_K = 31
CONV_K_PAD = 32
HALO = 32
POOL_WINDOWS = (2, 4, 8, 16)
GROUP = 128
N_BRANCH = 3
RMS_EPS = 1e-6
LN_EPS = 1e-5
ATTN_SCALE = 0.125
EXP_ZERO_BELOW = -104.0

ADAM_LR = 0.001
ADAM_B1 = 0.9
ADAM_B2 = 0.999
ADAM_EPS = 1e-08
ADAM_WD = 0.01
ADAM_STEP = 10

U_P, U_GP, U_CA, U_CB, U_GC, U_Q, U_K, U_V, U_GA, U_GM = 0, 1, 2, 3, 4, 5, 6, 7, 8, 9

V7X_VMEM_LIMIT = 60 * 1024 * 1024
N_DEV = 8
MESH_AXES = ("x", "y", "c")
MESH = pl.DeviceIdType.MESH

S = jax.ShapeDtypeStruct


def _params(n_grid):
    return pltpu.CompilerParams(dimension_semantics=("arbitrary",) * n_grid, vmem_limit_bytes=V7X_VMEM_LIMIT)


def _sigmoid(x):
    return 1.0 / (1.0 + jnp.exp(-x))


def _dsilu(x, s):
    return s * (1.0 + x * (1.0 - s))


def _dot(a, b):
    return jnp.dot(a, b, preferred_element_type=F32)


def _dot_nt(a, b):
    return lax.dot_general(a, b, (((1,), (1,)), ((), ())), preferred_element_type=F32)


def _dot_tn(a, b):
    return lax.dot_general(a, b, (((0,), (0,)), ((), ())), preferred_element_type=F32)


def _split_bf16(x):
    hi = x.astype(BF16)
    lo = (x - hi.astype(F32)).astype(BF16)
    return hi, lo


def _const_spec(shape):
    n = len(shape)
    return pl.BlockSpec(shape, lambda *_: (0,) * n, pipeline_mode=pl.Buffered(1))


def _inproj_fwd(x, g_pre, w_in):
    t = x.shape[0]
    tb = min(1024, t)
    ug = 3
    nb = ug * UNIT

    def body(x_ref, g_ref, w_ref, o_ref, h_ref):
        @pl.when(pl.program_id(1) == 0)
        def _():
            xf = x_ref[...]
            r = lax.rsqrt(jnp.mean(xf * xf, axis=-1, keepdims=True) + RMS_EPS)
            h_ref[...] = (xf * r * g_ref[...]).astype(BF16)

        acc = _dot(h_ref[...], w_ref[...])
        for u in range(ug):
            o_ref[u] = acc[:, u * UNIT:(u + 1) * UNIT]

    return pl.pallas_call(
        body, name="inproj_fwd",
        grid=(t // tb, N_UNITS // ug),
        in_specs=[pl.BlockSpec((tb, D_MODEL), lambda i, j: (i, 0)),
                  pl.BlockSpec((1, D_MODEL), lambda i, j: (0, 0)),
                  pl.BlockSpec((D_MODEL, nb), lambda i, j: (0, j))],
        out_specs=pl.BlockSpec((ug, tb, UNIT), lambda i, j: (j, i, 0)),
        out_shape=S((N_UNITS, t, UNIT), F32),
        scratch_shapes=[pltpu.VMEM((tb, D_MODEL), BF16)],
        compiler_params=_params(2),
    )(x, g_pre, w_in)


def _inproj_bwd_x(x, g_pre, w_in, dproj, dxn):
    t = x.shape[0]
    tb = min(512, t)
    ug = 3
    nb = ug * UNIT
    nj = N_UNITS // ug

    def body(x_ref, g_ref, w_ref, dp_ref, dxn_ref, dx_ref, dg_ref, acc_ref):
        i, j = pl.program_id(0), pl.program_id(1)

        @pl.when(jnp.logical_and(i == 0, j == 0))
        def _():
            dg_ref[...] = jnp.zeros_like(dg_ref)

        part = _dot_nt(dp_ref[0].astype(BF16), w_ref[:, 0:UNIT])
        for u in range(1, ug):
            part += _dot_nt(dp_ref[u].astype(BF16), w_ref[:, u * UNIT:(u + 1) * UNIT])

        @pl.when(j == 0)
        def _():
            acc_ref[...] = part

        @pl.when(j > 0)
        def _():
            acc_ref[...] += part

        @pl.when(j == nj - 1)
        def _():
            xf = x_ref[...]
            r = lax.rsqrt(jnp.mean(xf * xf, axis=-1, keepdims=True) + RMS_EPS)
            xhat = xf * r
            dh = acc_ref[...]
            dg_ref[...] += jnp.sum(dh * xhat, axis=0, keepdims=True)
            dxhat = dh * g_ref[...]
            dx_ref[...] = dxn_ref[...] + r * (dxhat - xhat * jnp.mean(dxhat * xhat, axis=-1, keepdims=True))

    return pl.pallas_call(
        body, name="inproj_bwd_x",
        grid=(t // tb, nj),
        in_specs=[pl.BlockSpec((tb, D_MODEL), lambda i, j: (i, 0)),
                  pl.BlockSpec((1, D_MODEL), lambda i, j: (0, 0)),
                  pl.BlockSpec((D_MODEL, nb), lambda i, j: (0, j)),
                  pl.BlockSpec((ug, tb, UNIT), lambda i, j: (j, i, 0)),
                  pl.BlockSpec((tb, D_MODEL), lambda i, j: (i, 0))],
        out_specs=[pl.BlockSpec((tb, D_MODEL), lambda i, j: (i, 0)),
                   pl.BlockSpec((1, D_MODEL), lambda i, j: (0, 0))],
        out_shape=[S((t, D_MODEL), F32), S((1, D_MODEL), F32)],
        scratch_shapes=[pltpu.VMEM((tb, D_MODEL), F32)],
        compiler_params=_params(2),
    )(x, g_pre, w_in, dproj, dxn)


def _inproj_bwd_w(x, g_pre, dproj):
    t = x.shape[0]
    tb = min(512, t)
    ug = 3
    nb = ug * UNIT
    ni = t // tb

    def body(x_ref, g_ref, dp_ref, dw_ref):
        i = pl.program_id(1)
        xf = x_ref[...]
        r = lax.rsqrt(jnp.mean(xf * xf, axis=-1, keepdims=True) + RMS_EPS)
        h = (xf * r * g_ref[...]).astype(BF16)

        @pl.when(i == 0)
        def _():
            dw_ref[...] = jnp.zeros_like(dw_ref)

        for u in range(ug):
            dw_ref[:, u * UNIT:(u + 1) * UNIT] += _dot_tn(h, dp_ref[u].astype(BF16))

    return pl.pallas_call(
        body, name="inproj_bwd_w",
        grid=(N_UNITS // ug, ni),
        in_specs=[pl.BlockSpec((tb, D_MODEL), lambda j, i: (i, 0)),
                  pl.BlockSpec((1, D_MODEL), lambda j, i: (0, 0)),
                  pl.BlockSpec((ug, tb, UNIT), lambda j, i: (j, i, 0))],
        out_specs=pl.BlockSpec((D_MODEL, nb), lambda j, i: (0, j)),
        out_shape=S((D_MODEL, IN_WIDTH), F32),
        compiler_params=_params(2),
    )(x, g_pre, dproj)


def _attn_tile_size(t):
    return min(256, t)


def _softplus_parts(z):
    sp = jnp.maximum(z, 0.0) + jnp.log(1.0 + jnp.exp(-jnp.abs(z)))
    return -sp, z - sp


def _attn_fwd(proj):
    t = proj.shape[1]
    tq = _attn_tile_size(t)

    def body(q_ref, k_ref, v_ref, o_ref, bs_ref):
        i = pl.program_id(1)
        row = lax.broadcasted_iota(jnp.int32, (tq, tq), 0)
        col = lax.broadcasted_iota(jnp.int32, (tq, tq), 1)
        tri = (row > col).astype(BF16)
        causal = col < row
        lane = lax.broadcasted_iota(jnp.int32, (1, LANES), 1)
        q_all = q_ref[0] * ATTN_SCALE
        o_out = jnp.zeros((tq, LANES), F32)
        b_out = jnp.zeros((tq, LANES), F32)

        for hh in range(HEADS_PER_BLOCK):
            in_head = (lane >= hh * HEAD_DIM) & (lane < (hh + 1) * HEAD_DIM)
            qh = jnp.where(in_head, q_all, 0.0).astype(BF16)

            def tile(kj, carry, acc, masked, qh=qh):
                ks = pl.multiple_of(kj * tq, tq)
                kh = k_ref[0, pl.ds(ks, tq), :].astype(BF16)
                vh = v_ref[0, pl.ds(ks, tq), :].astype(BF16)
                z = _dot_nt(qh, kh)
                lm, lb = _softplus_parts(z)
                if masked:
                    lm = jnp.where(causal, lm, 0.0)
                hi, lo = _split_bf16(lm)
                between = _dot(hi, tri) + _dot(lo, tri) + carry
                w = jnp.exp(lb + between)
                if masked:
                    w = jnp.where(causal, w, 0.0)
                acc = acc + _dot(w.astype(BF16), vh)
                carry = carry + jnp.sum(lm, axis=1, keepdims=True)
                return carry, acc

            carry, acc = tile(i, jnp.zeros((tq, 1), F32), jnp.zeros((tq, LANES), F32), True)
            carry, acc = lax.fori_loop(
                0, i, lambda n, c: tile(i - 1 - n, c[0], c[1], False), (carry, acc))
            o_out = jnp.where(in_head, acc, o_out)
            b_out = jnp.where(in_head, carry, b_out)

        o_ref[...] = o_out
        bs_ref[0] = b_out

    return pl.pallas_call(
        body, name="attn_fwd",
        grid=(N_HEAD_BLOCKS, t // tq),
        in_specs=[pl.BlockSpec((1, tq, LANES), lambda p, i: (U_Q, i, p)),
                  pl.BlockSpec((1, t, LANES), lambda p, i: (U_K, 0, p)),
                  pl.BlockSpec((1, t, LANES), lambda p, i: (U_V, 0, p))],
        out_specs=[pl.BlockSpec((tq, LANES), lambda p, i: (i, p)),
                   pl.BlockSpec((1, tq, LANES), lambda p, i: (p, i, 0))],
        out_shape=[S((t, UNIT), F32), S((N_HEAD_BLOCKS, t, LANES), F32)],
        compiler_params=_params(2),
    )(proj, proj, proj)


def _attn_bwd(proj, d_o, bsum):
    t = proj.shape[1]
    tq = _attn_tile_size(t)
    nq = t // tq

    def body(q_ref, k_ref, v_ref, do_ref, bs_ref, dq_ref, dk_ref, dv_ref, dkt_ref, dvt_ref):
        i = pl.program_id(1)

        @pl.when(i == 0)
        def _():
            dkt_ref[...] = jnp.zeros_like(dkt_ref)
            dvt_ref[...] = jnp.zeros_like(dvt_ref)

        row = lax.broadcasted_iota(jnp.int32, (tq, tq), 0)
        col = lax.broadcasted_iota(jnp.int32, (tq, tq), 1)
        upto = (row <= col).astype(BF16)
        before = (row < col).astype(BF16)
        causal = col < row
        lane = lax.broadcasted_iota(jnp.int32, (1, LANES), 1)
        q_all = q_ref[0] * ATTN_SCALE
        do_all = do_ref[...]
        bs_all = bs_ref[0]
        dq_out = jnp.zeros((tq, LANES), F32)

        for hh in range(HEADS_PER_BLOCK):
            in_head = (lane >= hh * HEAD_DIM) & (lane < (hh + 1) * HEAD_DIM)
            q_m = jnp.where(in_head, q_all, 0.0)
            do_m = jnp.where(in_head, do_all, 0.0)
            qh = q_m.astype(BF16)
            doh = do_m.astype(BF16)
            qt = q_m.T.astype(BF16)
            dot_ = do_m.T.astype(BF16)
            btot = jnp.max(jnp.where(in_head, bs_all, -jnp.inf), axis=1, keepdims=True)

            def tile(kj, c_b, c_p, dq, masked, qh=qh, doh=doh, qt=qt, dot_=dot_, btot=btot):
                ks = pl.multiple_of(kj * tq, tq)
                kh = k_ref[0, pl.ds(ks, tq), :].astype(BF16)
                vh = v_ref[0, pl.ds(ks, tq), :].astype(BF16)
                z = _dot_nt(qh, kh)
                lm, lb = _softplus_parts(z)
                if masked:
                    lm = jnp.where(causal, lm, 0.0)
                hi, lo = _split_bf16(lm)
                between = btot - (c_b + _dot(hi, upto) + _dot(lo, upto))
                w = jnp.exp(lb + between)
                if masked:
                    w = jnp.where(causal, w, 0.0)
                e = w * _dot_nt(doh, vh)
                ehi, elo = _split_bf16(e)
                p_sum = c_p + _dot(ehi, before) + _dot(elo, before)
                beta = jnp.exp(lb)
                dz = e * (1.0 - beta) - p_sum * beta
                if masked:
                    dz = jnp.where(causal, dz, 0.0)
                dzb = dz.astype(BF16)
                dq = dq + _dot(dzb, kh)
                dkt_ref[kj] += _dot(qt, dzb)
                dvt_ref[kj] += _dot(dot_, w.astype(BF16))
                c_b = c_b + jnp.sum(lm, axis=1, keepdims=True)
                c_p = c_p + jnp.sum(e, axis=1, keepdims=True)
                return c_b, c_p, dq

            init = (jnp.zeros((tq, 1), F32), jnp.zeros((tq, 1), F32), jnp.zeros((tq, LANES), F32))
            c_b, c_p, dq = lax.fori_loop(0, i, lambda kj, c: tile(kj, c[0], c[1], c[2], False), init)
            _, _, dq = tile(i, c_b, c_p, dq, True)
            dq_out = jnp.where(in_head, dq * ATTN_SCALE, dq_out)

        dq_ref[...] = dq_out

        @pl.when(i == nq - 1)
        def _():
            for kj in range(nq):
                dk_ref[kj * tq:(kj + 1) * tq, :] = dkt_ref[kj].T
                dv_ref[kj * tq:(kj + 1) * tq, :] = dvt_ref[kj].T

    return pl.pallas_call(
        body, name="attn_bwd",
        grid=(N_HEAD_BLOCKS, nq),
        in_specs=[pl.BlockSpec((1, tq, LANES), lambda p, i: (U_Q, i, p)),
                  pl.BlockSpec((1, t, LANES), lambda p, i: (U_K, 0, p)),
                  pl.BlockSpec((1, t, LANES), lambda p, i: (U_V, 0, p)),
                  pl.BlockSpec((tq, LANES), lambda p, i: (i, p)),
                  pl.BlockSpec((1, tq, LANES), lambda p, i: (p, i, 0))],
        out_specs=[pl.BlockSpec((tq, LANES), lambda p, i: (i, p)),
                   pl.BlockSpec((t, LANES), lambda p, i: (0, p)),
                   pl.BlockSpec((t, LANES), lambda p, i: (0, p))],
        out_shape=[S((t, UNIT), F32)] * 3,
        scratch_shapes=[pltpu.VMEM((nq, LANES, tq), F32), pltpu.VMEM((nq, LANES, tq), F32)],
        compiler_params=_params(2),
    )(proj, proj, proj, d_o, bsum)


def _mix_in_specs(tb):
    hb = tb // HALO

    def unit(u):
        return pl.BlockSpec((1, tb, UNIT), lambda i: (u, i, 0))

    def halo(u):
        return pl.BlockSpec((1, HALO, UNIT), lambda i: (u, jnp.maximum(i * hb - 1, 0), 0))

    return [unit(U_P), halo(U_P), unit(U_GP), unit(U_CA), halo(U_CA), unit(U_CB), halo(U_CB), unit(U_GC),
            unit(U_GA), pl.BlockSpec((3, tb, UNIT), lambda i: (3, i, 0)), pl.BlockSpec((3, tb, UNIT), lambda i: (4, i, 0))]


def _mix_weight_specs():
    return [_const_spec((4, GROUP, GROUP)), _const_spec((1, UNIT)), _const_spec((1, UNIT)),
            _const_spec((UNIT, D_MODEL)), _const_spec((CONV_K_PAD, UNIT)), _const_spec((1, UNIT)),
            _const_spec((1, UNIT)), _const_spec((1, UNIT)), _const_spec((UNIT, D_MODEL)),
            _const_spec((UNIT, D_MODEL)), _const_spec((D_MODEL, D_MODEL)), _const_spec((1, D_MODEL))]


def _mix_forward(i, tb, proj_refs, o_ref, w_refs, pbuf, ubuf):
    p_ref, ph_ref, gp_ref, ca_ref, cah_ref, cb_ref, cbh_ref, gc_ref, ga_ref, gm0_ref, gm1_ref = proj_refs
    (poolw_ref, poolb_ref, pscale_ref, wpo_ref, convw_ref, convb_ref, lng_ref, lnb_ref,
     wco_ref, wao_ref, wo_ref, gpost_ref) = w_refs
    first = i == 0
    r = {}

    pbuf[0:HALO, :] = jnp.where(first, 0.0, ph_ref[0])
    pbuf[HALO:HALO + tb, :] = p_ref[0]
    tpos = i * tb + lax.broadcasted_iota(jnp.int32, (tb, 1), 0)
    d_parts, y_parts = [], []
    for g, win in enumerate(POOL_WINDOWS):
        cs = slice(g * GROUP, (g + 1) * GROUP)
        cur = pbuf[HALO:HALO + tb, cs]
        s = cur
        for j in range(1, win):
            s = s + pbuf[HALO - j:HALO - j + tb, cs]
        cnt = jnp.minimum(tpos + 1, win).astype(F32)
        d_g = s / cnt - cur
        d_parts.append(d_g)
        y_parts.append(_dot(d_g.astype(BF16), poolw_ref[g]))
    r["d"] = d_parts
    y = jnp.concatenate(y_parts, axis=1) + poolb_ref[...]
    r["y"] = y
    mp = y * pscale_ref[...]
    gp = gp_ref[0]
    sgp = _sigmoid(gp)
    r["mp"], r["gp"], r["sgp"] = mp, gp, sgp
    ua = mp * (gp * sgp)

    ah, bh = cah_ref[0], cbh_ref[0]
    ubuf[0:HALO, :] = jnp.where(first, 0.0, ah * _sigmoid(bh))
    a, b = ca_ref[0], cb_ref[0]
    sb = _sigmoid(b)
    ubuf[HALO:HALO + tb, :] = a * sb
    r["a"], r["sb"] = a, sb
    cv = jnp.zeros((tb, UNIT), F32) + convb_ref[...]
    off = HALO - (CONV_K - 1)
    for k in range(CONV_K):
        cv = cv + convw_ref[k:k + 1, :] * ubuf[off + k:off + k + tb, :]
    mu = jnp.mean(cv, axis=-1, keepdims=True)
    cc = cv - mu
    rs = lax.rsqrt(jnp.mean(cc * cc, axis=-1, keepdims=True) + LN_EPS)
    nrm = cc * rs
    ln = nrm * lng_ref[...] + lnb_ref[...]
    sln = _sigmoid(ln)
    sc = ln * sln
    gc = gc_ref[0]
    sgc = _sigmoid(gc)
    r["rs"], r["nrm"], r["ln"], r["sln"], r["sc"], r["gc"], r["sgc"] = rs, nrm, ln, sln, sc, gc, sgc
    ub = sc * (gc * sgc)

    o = o_ref[...]
    ga = ga_ref[0]
    sga = _sigmoid(ga)
    r["o"], r["ga"], r["sga"] = o, ga, sga
    uc = o * (ga * sga)

    r["ua"], r["ub"], r["uc"] = ua.astype(BF16), ub.astype(BF16), uc.astype(BF16)
    ya = _dot(r["ua"], wpo_ref[...])
    yb = _dot(r["ub"], wco_ref[...])
    yc = _dot(r["uc"], wao_ref[...])
    g0 = _sigmoid(jnp.concatenate([gm0_ref[0], gm0_ref[1]], axis=1))
    g1 = _sigmoid(jnp.concatenate([gm0_ref[2], gm1_ref[0]], axis=1))
    g2 = _sigmoid(jnp.concatenate([gm1_ref[1], gm1_ref[2]], axis=1))
    r["ya"], r["yb"], r["yc"], r["g0"], r["g1"], r["g2"] = ya, yb, yc, g0, g1, g2
    m = (g0 * ya + g1 * yb + g2 * yc).astype(BF16)
    r["m"] = m
    out = _dot(m, wo_ref[...])
    r2 = lax.rsqrt(jnp.mean(out * out, axis=-1, keepdims=True) + RMS_EPS)
    r["n2"], r["r2"] = out * r2, r2
    return r


def _mix_fwd(proj, o, x, weights):
    t = x.shape[0]
    tb = min(256, t)

    def body(*refs):
        proj_refs, o_ref, x_ref = refs[0:11], refs[11], refs[12]
        w_refs = refs[13:25]
        xn_ref, pbuf, ubuf = refs[25], refs[26], refs[27]
        r = _mix_forward(pl.program_id(0), tb, proj_refs, o_ref, w_refs, pbuf, ubuf)
        xn_ref[...] = x_ref[...] + r["n2"] * w_refs[11][...]

    return pl.pallas_call(
        body, name="mix_fwd",
        grid=(t // tb,),
        in_specs=_mix_in_specs(tb) + [pl.BlockSpec((tb, UNIT), lambda i: (i, 0)),
                                      pl.BlockSpec((tb, D_MODEL), lambda i: (i, 0))] + _mix_weight_specs(),
        out_specs=pl.BlockSpec((tb, D_MODEL), lambda i: (i, 0)),
        out_shape=S((t, D_MODEL), F32),
        scratch_shapes=[pltpu.VMEM((HALO + tb, UNIT), F32), pltpu.VMEM((HALO + tb, UNIT), F32)],
        compiler_params=_params(1),
    )(*([proj] * 11), o, x, *weights)


def _mix_bwd(proj, o, x, dxn, weights):
    t = x.shape[0]
    tb = min(128, t)

    def body(*refs):
        proj_refs, o_ref, x_ref, dxn_ref = refs[0:11], refs[11], refs[12], refs[13]
        w_refs = refs[14:26]
        dg_ref, dd_ref, dcv_ref, do_ref, dwo_ref, dwp_ref, dwc_ref, dwa_ref, dpw_ref, dvec_ref = refs[26:36]
        pbuf, ubuf = refs[36], refs[37]
        (poolw_ref, _, pscale_ref, wpo_ref, _, _, lng_ref, _, wco_ref, wao_ref, wo_ref, gpost_ref) = w_refs
        i = pl.program_id(0)

        @pl.when(i == 0)
        def _():
            for ref in (dwo_ref, dwp_ref, dwc_ref, dwa_ref, dpw_ref, dvec_ref):
                ref[...] = jnp.zeros_like(ref)

        r = _mix_forward(i, tb, proj_refs, o_ref, w_refs, pbuf, ubuf)

        def colsum(v):
            return jnp.sum(v, axis=0, keepdims=True)

        dxn = dxn_ref[...]
        n2 = r["n2"]
        dvec_ref[0:1, :] += colsum(dxn * n2)
        dn2 = dxn * gpost_ref[...]
        dout = (r["r2"] * (dn2 - n2 * jnp.mean(dn2 * n2, axis=-1, keepdims=True))).astype(BF16)
        dm = _dot_nt(dout, wo_ref[...])
        dwo_ref[...] += _dot_tn(r["m"], dout)

        g0, g1, g2 = r["g0"], r["g1"], r["g2"]
        dgm = [dm * r["ya"] * g0 * (1.0 - g0), dm * r["yb"] * g1 * (1.0 - g1), dm * r["yc"] * g2 * (1.0 - g2)]
        for bidx in range(N_BRANCH):
            dg_ref[3 + 2 * bidx] = dgm[bidx][:, 0:UNIT]
            dg_ref[4 + 2 * bidx] = dgm[bidx][:, UNIT:2 * UNIT]
        dya = (dm * g0).astype(BF16)
        dyb = (dm * g1).astype(BF16)
        dyc = (dm * g2).astype(BF16)
        dua = _dot_nt(dya, wpo_ref[...])
        dub = _dot_nt(dyb, wco_ref[...])
        duc = _dot_nt(dyc, wao_ref[...])
        dwp_ref[...] += _dot_tn(r["ua"], dya)
        dwc_ref[...] += _dot_tn(r["ub"], dyb)
        dwa_ref[...] += _dot_tn(r["uc"], dyc)

        gp, sgp = r["gp"], r["sgp"]
        dmp = dua * (gp * sgp)
        dg_ref[0] = dua * r["mp"] * _dsilu(gp, sgp)
        dvec_ref[2:3, 0:UNIT] += colsum(dmp * r["y"])
        dy = dmp * pscale_ref[...]
        dvec_ref[1:2, 0:UNIT] += colsum(dy)
        dd_parts = []
        for g in range(len(POOL_WINDOWS)):
            dy_g = dy[:, g * GROUP:(g + 1) * GROUP].astype(BF16)
            dd_parts.append(_dot_nt(dy_g, poolw_ref[g]))
            dpw_ref[g] += _dot_tn(r["d"][g].astype(BF16), dy_g)
        dd_ref[...] = jnp.concatenate(dd_parts, axis=1)

        gc, sgc = r["gc"], r["sgc"]
        dsc = dub * (gc * sgc)
        dg_ref[1] = dub * r["sc"] * _dsilu(gc, sgc)
        dln = dsc * _dsilu(r["ln"], r["sln"])
        nrm = r["nrm"]
        dvec_ref[4:5, 0:UNIT] += colsum(dln * nrm)
        dvec_ref[5:6, 0:UNIT] += colsum(dln)
        dnrm = dln * lng_ref[...]
        dcv = r["rs"] * (dnrm - jnp.mean(dnrm, axis=-1, keepdims=True)
                         - nrm * jnp.mean(dnrm * nrm, axis=-1, keepdims=True))
        dvec_ref[3:4, 0:UNIT] += colsum(dcv)
        dcv_ref[...] = dcv

        ga, sga = r["ga"], r["sga"]
        do_ref[...] = duc * (ga * sga)
        dg_ref[2] = duc * r["o"] * _dsilu(ga, sga)

    def acc_spec(shape):
        n = len(shape)
        return pl.BlockSpec(shape, lambda i: (0,) * n)

    tok = lambda w: pl.BlockSpec((tb, w), lambda i: (i, 0))
    return pl.pallas_call(
        body, name="mix_bwd",
        grid=(t // tb,),
        in_specs=_mix_in_specs(tb) + [tok(UNIT), tok(D_MODEL), tok(D_MODEL)] + _mix_weight_specs(),
        out_specs=[pl.BlockSpec((9, tb, UNIT), lambda i: (0, i, 0)), tok(UNIT), tok(UNIT), tok(UNIT),
                   acc_spec((D_MODEL, D_MODEL)), acc_spec((UNIT, D_MODEL)), acc_spec((UNIT, D_MODEL)),
                   acc_spec((UNIT, D_MODEL)), acc_spec((4, GROUP, GROUP)), acc_spec((8, D_MODEL))],
        out_shape=[S((9, t, UNIT), F32), S((t, UNIT), F32), S((t, UNIT), F32), S((t, UNIT), F32),
                   S((D_MODEL, D_MODEL), F32), S((UNIT, D_MODEL), F32), S((UNIT, D_MODEL), F32),
                   S((UNIT, D_MODEL), F32), S((4, GROUP, GROUP), F32), S((8, D_MODEL), F32)],
        scratch_shapes=[pltpu.VMEM((HALO + tb, UNIT), F32), pltpu.VMEM((HALO + tb, UNIT), F32)],
        compiler_params=_params(1),
    )(*([proj] * 11), o, x, dxn, *weights)


def _halo_bwd(proj, dcv, dd, conv_w):
    t = dcv.shape[0]
    tb = min(256, t)
    hb = tb // HALO
    n_halo_blocks = t // HALO
    nt = t // tb

    def body(ca_ref, cah_ref, cb_ref, cbh_ref, dcv_ref, dcvh_ref, dd_ref, ddh_ref, cw_ref,
             dpre_ref, dcw_ref, ubuf, gbuf, nbuf):
        i = pl.program_id(0)
        first = i == 0
        last = i == nt - 1

        @pl.when(first)
        def _():
            dcw_ref[...] = jnp.zeros_like(dcw_ref)

        a, b = ca_ref[0], cb_ref[0]
        sb = _sigmoid(b)
        ubuf[0:HALO, :] = jnp.where(first, 0.0, cah_ref[0] * _sigmoid(cbh_ref[0]))
        ubuf[HALO:HALO + tb, :] = a * sb
        dcv_main = dcv_ref[...]
        gbuf[0:tb, :] = dcv_main
        gbuf[tb:tb + HALO, :] = jnp.where(last, 0.0, dcvh_ref[...])

        off = HALO - (CONV_K - 1)
        du = jnp.zeros((tb, UNIT), F32)
        for k in range(CONV_K):
            du = du + cw_ref[k:k + 1, :] * gbuf[CONV_K - 1 - k:CONV_K - 1 - k + tb, :]
            dcw_ref[k:k + 1, :] += jnp.sum(dcv_main * ubuf[off + k:off + k + tb, :], axis=0, keepdims=True)
        dpre_ref[1] = du * sb
        dpre_ref[2] = du * a * sb * (1.0 - sb)

        tpos = i * tb + lax.broadcasted_iota(jnp.int32, (tb + HALO, 1), 0)
        dd_main = dd_ref[...]
        dd_ext = jnp.concatenate([dd_main, jnp.where(last, 0.0, ddh_ref[...])], axis=0)
        dp_parts = []
        for g, win in enumerate(POOL_WINDOWS):
            cs = slice(g * GROUP, (g + 1) * GROUP)
            cnt = jnp.minimum(tpos + 1, win).astype(F32)
            nbuf[:, cs] = dd_ext[:, cs] / cnt
        for g, win in enumerate(POOL_WINDOWS):
            cs = slice(g * GROUP, (g + 1) * GROUP)
            s = nbuf[0:tb, cs]
            for j in range(1, win):
                s = s + nbuf[j:j + tb, cs]
            dp_parts.append(s - dd_main[:, cs])
        dpre_ref[0] = jnp.concatenate(dp_parts, axis=1)

    def unit(u):
        return pl.BlockSpec((1, tb, UNIT), lambda i: (u, i, 0))

    def past(u):
        return pl.BlockSpec((1, HALO, UNIT), lambda i: (u, jnp.maximum(i * hb - 1, 0), 0))

    tok = pl.BlockSpec((tb, UNIT), lambda i: (i, 0))
    future = pl.BlockSpec((HALO, UNIT), lambda i: (jnp.minimum((i + 1) * hb, n_halo_blocks - 1), 0))
    return pl.pallas_call(
        body, name="halo_bwd",
        grid=(nt,),
        in_specs=[unit(U_CA), past(U_CA), unit(U_CB), past(U_CB), tok, future, tok, future,
                  _const_spec((CONV_K_PAD, UNIT))],
        out_specs=[pl.BlockSpec((3, tb, UNIT), lambda i: (0, i, 0)),
                   pl.BlockSpec((CONV_K_PAD, UNIT), lambda i: (0, 0))],
        out_shape=[S((3, t, UNIT), F32), S((CONV_K_PAD, UNIT), F32)],
        scratch_shapes=[pltpu.VMEM((HALO + tb, UNIT), F32), pltpu.VMEM((HALO + tb, UNIT), F32),
                        pltpu.VMEM((HALO + tb, UNIT), F32)],
        compiler_params=_params(1),
    )(proj, proj, proj, proj, dcv, dcv, dd, dd, conv_w)


def _loss_head(y, target):
    t = y.shape[0]
    tb = min(512, t)

    def body(y_ref, t_ref, loss_ref, dy_ref):
        @pl.when(pl.program_id(0) == 0)
        def _():
            loss_ref[...] = jnp.zeros_like(loss_ref)

        err = y_ref[...] - t_ref[...]
        dy_ref[...] = err * (1.0 / D_MODEL)
        part = 0.5 * jnp.sum(jnp.mean(err * err, axis=-1, keepdims=True), axis=0, keepdims=True)
        r8 = lax.broadcasted_iota(jnp.int32, (8, LANES), 0)
        c8 = lax.broadcasted_iota(jnp.int32, (8, LANES), 1)
        loss_ref[...] += jnp.where((r8 == 0) & (c8 == 0), part, 0.0)

    return pl.pallas_call(
        body, name="loss_head",
        grid=(t // tb,),
        in_specs=[pl.BlockSpec((tb, D_MODEL), lambda i: (i, 0))] * 2,
        out_specs=[pl.BlockSpec((8, LANES), lambda i: (0, 0)), pl.BlockSpec((tb, D_MODEL), lambda i: (i, 0))],
        out_shape=[S((8, LANES), F32), S((t, D_MODEL), F32)],
        compiler_params=_params(1),
    )(y, target)


def _adamw(name, parts, w, m, v):
    n, rows, cols = parts.shape
    rb = rows
    while rb * cols * 4 * (n + 7) * 2 > 24 * 1024 * 1024 and rb % 16 == 0:
        rb //= 2

    def body(p_ref, w_ref, m_ref, v_ref, g_ref, d_ref, nm_ref, nv_ref):
        g = p_ref[0]
        for k in range(1, n):
            g = g + p_ref[k]
        nm = ADAM_B1 * m_ref[...] + (1.0 - ADAM_B1) * g
        nv = ADAM_B2 * v_ref[...] + (1.0 - ADAM_B2) * (g * g)
        m_hat = nm / (1.0 - ADAM_B1 ** ADAM_STEP)
        v_hat = nv / (1.0 - ADAM_B2 ** ADAM_STEP)
        g_ref[...] = g
        d_ref[...] = -ADAM_LR * (m_hat / (jnp.sqrt(v_hat) + ADAM_EPS) + ADAM_WD * w_ref[...])
        nm_ref[...] = nm
        nv_ref[...] = nv

    blk = pl.BlockSpec((rb, cols), lambda i: (i, 0))
    return pl.pallas_call(
        body, name=name,
        grid=(rows // rb,),
        in_specs=[pl.BlockSpec((n, rb, cols), lambda i: (0, i, 0)), blk, blk, blk],
        out_specs=[blk] * 4,
        out_shape=[S((rows, cols), F32)] * 4,
        compiler_params=_params(1),
    )(parts, w, m, v)


def _chip_index():
    return 2 * lax.axis_index("x") + lax.axis_index("y")


def _gather_weights(shards, axes):
    n = len(shards)
    fulls = [S(tuple(4 * d if k == ax else d for k, d in enumerate(s.shape)), s.dtype) for s, ax in zip(shards, axes)]

    def body(*refs):
        src, dst = refs[0:n], refs[n:2 * n]
        send_sems, recv_sems, local_sems = refs[2 * n:2 * n + 3]
        x, y, c = lax.axis_index("x"), lax.axis_index("y"), lax.axis_index("c")
        chips = [(1 - x, y), (x, 1 - y), (1 - x, 1 - y)]

        def place(a, chip):
            ax, size = axes[a], src[a].shape[axes[a]]
            idx = [slice(None)] * len(src[a].shape)
            idx[ax] = pl.ds(pl.multiple_of((2 * chip[0] + chip[1]) * size, size), size)
            return dst[a].at[tuple(idx)]

        def remote(a, j, chip_of_block, to):
            return pltpu.make_async_remote_copy(
                src_ref=src[a], dst_ref=place(a, chip_of_block), send_sem=send_sems.at[a, j],
                recv_sem=recv_sems.at[a, j], device_id=(to[0], to[1], c), device_id_type=MESH)

        local = [pltpu.make_async_copy(src[a], place(a, (x, y)), local_sems.at[a]) for a in range(n)]
        for cp in local:
            cp.start()
        sends = [remote(a, j, (x, y), chip) for a in range(n) for j, chip in enumerate(chips)]
        for cp in sends:
            cp.start()
        for a in range(n):
            for j, chip in enumerate(chips):
                remote(a, j, chip, chip).wait_recv()
        for cp in sends:
            cp.wait_send()
        for cp in local:
            cp.wait()

    any_spec = pl.BlockSpec(memory_space=pl.ANY)
    return pl.pallas_call(
        body, name="gather_weights",
        in_specs=[any_spec] * n, out_specs=[any_spec] * n, out_shape=fulls,
        scratch_shapes=[pltpu.SemaphoreType.DMA((n, 3)), pltpu.SemaphoreType.DMA((n, 3)),
                        pltpu.SemaphoreType.DMA((n,))],
    )(*shards)


def _exchange_partials(partials, axes):
    n = len(partials)

    def owned_shape(a):
        s, ax = partials[a].shape, axes[a]
        return tuple(d // 4 if k == ax else d for k, d in enumerate(s))

    outs = [S((N_DEV,) + owned_shape(a), partials[a].dtype) for a in range(n)]
    flips = [(dx, dy, dc) for dx in (0, 1) for dy in (0, 1) for dc in (0, 1)][1:]

    def body(*refs):
        src, dst = refs[0:n], refs[n:2 * n]
        send_sems, recv_sems, local_sems = refs[2 * n:2 * n + 3]
        x, y, c = lax.axis_index("x"), lax.axis_index("y"), lax.axis_index("c")
        me = 4 * x + 2 * y + c

        def owned(a, chip):
            ax = axes[a]
            if ax is None:
                return src[a]
            size = src[a].shape[ax] // 4
            idx = [slice(None)] * len(src[a].shape)
            idx[ax] = pl.ds(pl.multiple_of((2 * chip[0] + chip[1]) * size, size), size)
            return src[a].at[tuple(idx)]

        def peer(f):
            return (1 - x if f[0] else x, 1 - y if f[1] else y, 1 - c if f[2] else c)

        def remote(a, j, to, slot):
            return pltpu.make_async_remote_copy(
                src_ref=owned(a, (to[0], to[1])), dst_ref=dst[a].at[slot], send_sem=send_sems.at[a, j],
                recv_sem=recv_sems.at[a, j], device_id=to, device_id_type=MESH)

        local = [pltpu.make_async_copy(owned(a, (x, y)), dst[a].at[me], local_sems.at[a]) for a in range(n)]
        for cp in local:
            cp.start()
        sends = [remote(a, j, peer(f), me) for a in range(n) for j, f in enumerate(flips)]
        for cp in sends:
            cp.start()
        for a in range(n):
            for j, f in enumerate(flips):
                px, py, pc = peer(f)
                remote(a, j, (x, y, c), 4 * px + 2 * py + pc).wait_recv()
        for cp in sends:
            cp.wait_send()
        for cp in local:
            cp.wait()

    any_spec = pl.BlockSpec(memory_space=pl.ANY)
    return pl.pallas_call(
        body, name="exchange_partials",
        in_specs=[any_spec] * n, out_specs=[any_spec] * n, out_shape=outs,
        scratch_shapes=[pltpu.SemaphoreType.DMA((n, 7)), pltpu.SemaphoreType.DMA((n, 7)),
                        pltpu.SemaphoreType.DMA((n,))],
    )(*partials)


SMALL_NAMES = ("norm_pre", "pool_w", "pool_b", "pool_scale", "conv_b", "conv_ln_g", "conv_ln_b", "norm_post")
BIG_NAMES = ("w_in", "w_pool_out", "conv_w", "w_conv_out", "w_attn_out", "w_o")
SHARD_AXIS = {"w_in": 2, "w_pool_out": 2, "conv_w": 2, "w_conv_out": 2, "w_attn_out": 2, "w_o": 1}
WEIGHT_ORDER = ("norm_pre", "w_in", "pool_w", "pool_b", "pool_scale", "w_pool_out", "conv_w", "conv_b",
                "conv_ln_g", "conv_ln_b", "w_conv_out", "w_attn_out", "w_o", "norm_post")


def _pack_small(parts):
    return jnp.concatenate([parts[n].reshape(-1, LANES) for n in SMALL_NAMES], axis=0)


def _unpack_small(packed, shapes):
    out, r0 = {}, 0
    for n in SMALL_NAMES:
        size = 1
        for d in shapes[n]:
            size *= d
        rows = size // LANES
        out[n] = packed[r0:r0 + rows].reshape(shapes[n])
        r0 += rows
    return out


def _layer_weights(full, small, l):
    row = lambda a: a[l][None, :]
    return (small["pool_w"][l].astype(BF16), small["pool_b"][l].reshape(1, UNIT), row(small["pool_scale"]),
            full["w_pool_out"][l], full["conv_w"][l], row(small["conv_b"]), row(small["conv_ln_g"]),
            row(small["conv_ln_b"]), full["w_conv_out"][l], full["w_attn_out"][l], full["w_o"][l],
            row(small["norm_post"]))


def _forward_backward(x, target, full, small):
    depth = full["w_in"].shape[0]
    acts = []
    for l in range(depth):
        g_pre = small["norm_pre"][l][None, :]
        proj = _inproj_fwd(x, g_pre, full["w_in"][l])
        o, bsum = _attn_fwd(proj)
        weights = _layer_weights(full, small, l)
        x_next = _mix_fwd(proj, o, x, weights)
        acts.append((x, proj, o, bsum, weights, g_pre))
        x = x_next
    loss_tile, dx = _loss_head(x, target)

    big = {n: [None] * depth for n in BIG_NAMES}
    sm = {n: [None] * depth for n in SMALL_NAMES}
    for l in reversed(range(depth)):
        x_in, proj, o, bsum, weights, g_pre = acts[l]
        dgates, dd, dcv, d_o, dwo, dwp, dwc, dwa, dpw, dvec = _mix_bwd(proj, o, x_in, dx, weights)
        dpre, dcw = _halo_bwd(proj, dcv, dd, full["conv_w"][l])
        dq, dk, dv = _attn_bwd(proj, d_o, bsum)
        dproj = jnp.concatenate([dpre[0:1], dgates[0:1], dpre[1:3], dgates[1:2], dq[None], dk[None], dv[None],
                                 dgates[2:9]], axis=0)
        dx, dg_pre = _inproj_bwd_x(x_in, g_pre, full["w_in"][l], dproj, dx)
        big["w_in"][l] = _inproj_bwd_w(x_in, g_pre, dproj)
        big["w_pool_out"][l], big["w_conv_out"][l], big["w_attn_out"][l], big["w_o"][l] = dwp, dwc, dwa, dwo
        big["conv_w"][l] = dcw
        sm["norm_pre"][l] = dg_pre[0]
        sm["pool_w"][l] = dpw
        sm["norm_post"][l] = dvec[0]
        sm["pool_b"][l] = dvec[1, 0:UNIT].reshape(4, GROUP)
        sm["pool_scale"][l] = dvec[2, 0:UNIT]
        sm["conv_b"][l] = dvec[3, 0:UNIT]
        sm["conv_ln_g"][l] = dvec[4, 0:UNIT]
        sm["conv_ln_b"][l] = dvec[5, 0:UNIT]
    big = {n: jnp.stack(v) for n, v in big.items()}
    sm = {n: jnp.stack(v) for n, v in sm.items()}
    return loss_tile, dx, big, sm


def kernel(x, norm_pre, w_in, pool_w, pool_b, pool_scale, w_pool_out, conv_w, conv_b, conv_ln_g, conv_ln_b, w_conv_out, w_attn_out, w_o, norm_post, loss_target, m_norm_pre, m_w_in, m_pool_w, m_pool_b, m_pool_scale, m_w_pool_out, m_conv_w, m_conv_b, m_conv_ln_g, m_conv_ln_b, m_w_conv_out, m_w_attn_out, m_w_o, m_norm_post, v_norm_pre, v_w_in, v_pool_w, v_pool_b, v_pool_scale, v_w_pool_out, v_conv_w, v_conv_b, v_conv_ln_g, v_conv_ln_b, v_w_conv_out, v_w_attn_out, v_w_o, v_norm_post):
    w = dict(norm_pre=norm_pre, w_in=w_in, pool_w=pool_w, pool_b=pool_b, pool_scale=pool_scale,
             w_pool_out=w_pool_out, conv_w=conv_w, conv_b=conv_b, conv_ln_g=conv_ln_g, conv_ln_b=conv_ln_b,
             w_conv_out=w_conv_out, w_attn_out=w_attn_out, w_o=w_o, norm_post=norm_post)
    m = dict(norm_pre=m_norm_pre, w_in=m_w_in, pool_w=m_pool_w, pool_b=m_pool_b, pool_scale=m_pool_scale,
             w_pool_out=m_w_pool_out, conv_w=m_conv_w, conv_b=m_conv_b, conv_ln_g=m_conv_ln_g,
             conv_ln_b=m_conv_ln_b, w_conv_out=m_w_conv_out, w_attn_out=m_w_attn_out, w_o=m_w_o,
             norm_post=m_norm_post)
    v = dict(norm_pre=v_norm_pre, w_in=v_w_in, pool_w=v_pool_w, pool_b=v_pool_b, pool_scale=v_pool_scale,
             w_pool_out=v_w_pool_out, conv_w=v_conv_w, conv_b=v_conv_b, conv_ln_g=v_conv_ln_g,
             conv_ln_b=v_conv_ln_b, w_conv_out=v_w_conv_out, w_attn_out=v_w_attn_out, w_o=v_w_o,
             norm_post=v_norm_post)
    pad_taps = lambda a: jnp.pad(a, ((0, 0), (0, CONV_K_PAD - CONV_K), (0, 0)))

    shards = [pad_taps(w[n]) if n == "conv_w" else w[n].astype(BF16) for n in BIG_NAMES]
    full = dict(zip(BIG_NAMES, _gather_weights(shards, [SHARD_AXIS[n] for n in BIG_NAMES])))
    small = {n: w[n] for n in SMALL_NAMES}

    loss_tile, dx, big, sm = _forward_backward(x[0], loss_target[0], full, small)

    packed = jnp.concatenate([_pack_small(sm), loss_tile], axis=0)
    exchanged = _exchange_partials([big[n] for n in BIG_NAMES] + [packed],
                                   [SHARD_AXIS[n] for n in BIG_NAMES] + [None])
    out = {}
    for n, parts in zip(BIG_NAMES, exchanged[:-1]):
        wn, mn, vn = (pad_taps(a[n]) if n == "conv_w" else a[n] for a in (w, m, v))
        shape = wn.shape
        flat = lambda a: a.reshape((-1, shape[-1]))
        res = _adamw("adamw_" + n, parts.reshape((N_DEV, -1, shape[-1])), flat(wn), flat(mn), flat(vn))
        res = [r.reshape(shape) for r in res]
        out[n] = [r[:, :CONV_K] for r in res] if n == "conv_w" else res
    parts = exchanged[-1]
    n_small_rows = parts.shape[1] - 8
    zeros_tile = jnp.zeros((8, LANES), F32)
    packs = [jnp.concatenate([_pack_small(a), zeros_tile], axis=0) for a in (w, m, v)]
    res = _adamw("adamw_small", parts, *packs)
    loss = res[0][n_small_rows, 0]
    shapes = {n: w[n].shape for n in SMALL_NAMES}
    unpacked = [_unpack_small(r[:n_small_rows], shapes) for r in res]
    for n in SMALL_NAMES:
        out[n] = [u[n] for u in unpacked]

    grads = [out[n][0] for n in WEIGHT_ORDER]
    deltas = [out[n][1] for n in WEIGHT_ORDER]
    new_m = [out[n][2] for n in WEIGHT_ORDER]
    new_v = [out[n][3] for n in WEIGHT_ORDER]
    return (loss, dx[None], *grads, *deltas, *new_m, *new_v)
```

```python
import jax
import jax.numpy as jnp
from jax import lax
from jax.experimental import pallas as pl
from jax.experimental.pallas import tpu as pltpu

F32 = jnp.float32
BF16 = jnp.bfloat16

D_MODEL = 1024
UNIT = 512
N_UNITS = 15
IN_WIDTH = UNIT * N_UNITS
N_HEADS = 8
HEAD_DIM = 64
HEADS_PER_BLOCK = 2
N_HEAD_BLOCKS = N_HEADS // HEADS_PER_BLOCK
LANES = 128
CONV_K = 31
CONV_K_PAD = 32
HALO = 32
POOL_WINDOWS = (2, 4, 8, 16)
GROUP = 128
N_BRANCH = 3
RMS_EPS = 1e-6
LN_EPS = 1e-5
ATTN_SCALE = 0.125
EXP_ZERO_BELOW = -104.0

ADAM_LR = 0.001
ADAM_B1 = 0.9
ADAM_B2 = 0.999
ADAM_EPS = 1e-08
ADAM_WD = 0.01
ADAM_STEP = 10

U_P, U_GP, U_CA, U_CB, U_GC, U_Q, U_K, U_V, U_GA, U_GM = 0, 1, 2, 3, 4, 5, 6, 7, 8, 9

V7X_VMEM_LIMIT = 60 * 1024 * 1024
N_DEV = 8
MESH = pl.DeviceIdType.MESH

S = jax.ShapeDtypeStruct


def _params(n_grid):
    return pltpu.CompilerParams(dimension_semantics=("arbitrary",) * n_grid, vmem_limit_bytes=V7X_VMEM_LIMIT)


def _sigmoid(x):
    return 1.0 / (1.0 + jnp.exp(-x))


def _dsilu(x, s):
    return s * (1.0 + x * (1.0 - s))


def _dot(a, b):
    return jnp.dot(a, b, preferred_element_type=F32)


def _dot_nt(a, b):
    return lax.dot_general(a, b, (((1,), (1,)), ((), ())), preferred_element_type=F32)


def _dot_tn(a, b):
    return lax.dot_general(a, b, (((0,), (0,)), ((), ())), preferred_element_type=F32)


def _split_bf16(x):
    hi = x.astype(BF16)
    lo = (x - hi.astype(F32)).astype(BF16)
    return hi, lo


def _const_spec(shape):
    n = len(shape)
    return pl.BlockSpec(shape, lambda *_: (0,) * n, pipeline_mode=pl.Buffered(1))


def _inproj_fwd(x, g_pre, w_in):
    t = x.shape[0]
    tb = min(1024, t)
    ug = 3
    nb = ug * UNIT

    def body(x_ref, g_ref, w_ref, o_ref, h_ref):
        @pl.when(pl.program_id(1) == 0)
        def _():
            xf = x_ref[...]
            r = lax.rsqrt(jnp.mean(xf * xf, axis=-1, keepdims=True) + RMS_EPS)
            h_ref[...] = (xf * r * g_ref[...]).astype(BF16)

        acc = _dot(h_ref[...], w_ref[...])
        for u in range(ug):
            o_ref[u] = acc[:, u * UNIT:(u + 1) * UNIT]

    return pl.pallas_call(
        body, name="inproj_fwd",
        grid=(t // tb, N_UNITS // ug),
        in_specs=[pl.BlockSpec((tb, D_MODEL), lambda i, j: (i, 0)),
                  pl.BlockSpec((1, D_MODEL), lambda i, j: (0, 0)),
                  pl.BlockSpec((D_MODEL, nb), lambda i, j: (0, j))],
        out_specs=pl.BlockSpec((ug, tb, UNIT), lambda i, j: (j, i, 0)),
        out_shape=S((N_UNITS, t, UNIT), F32),
        scratch_shapes=[pltpu.VMEM((tb, D_MODEL), BF16)],
        compiler_params=_params(2),
    )(x, g_pre, w_in)


def _dproj_specs(tb):
    tok = pl.BlockSpec((tb, UNIT), lambda i: (i, 0))
    return [pl.BlockSpec((3, tb, UNIT), lambda i: (0, i, 0)), pl.BlockSpec((9, tb, UNIT), lambda i: (0, i, 0)),
            tok, tok, tok]


def _dproj_unit(u, dpre_ref, dgates_ref, dq_ref, dk_ref, dv_ref):
    pre = {U_P: 0, U_CA: 1, U_CB: 2}
    gate = {U_GP: 0, U_GC: 1, U_GA: 2}
    if u in pre:
        val = dpre_ref[pre[u]]
    elif u in gate:
        val = dgates_ref[gate[u]]
    elif u >= U_GM:
        val = dgates_ref[3 + u - U_GM]
    else:
        val = {U_Q: dq_ref, U_K: dk_ref, U_V: dv_ref}[u][...]
    return val.astype(BF16)


def _inproj_bwd_x(x, g_pre, w_in, dproj_parts, dxn):
    t = x.shape[0]
    tb = min(256, t)

    def body(x_ref, g_ref, w_ref, dpre_ref, dgates_ref, dq_ref, dk_ref, dv_ref, dxn_ref, dx_ref, dg_ref):
        @pl.when(pl.program_id(0) == 0)
        def _():
            dg_ref[...] = jnp.zeros_like(dg_ref)

        dh = jnp.zeros((tb, D_MODEL), F32)
        for u in range(N_UNITS):
            dh = dh + _dot_nt(_dproj_unit(u, dpre_ref, dgates_ref, dq_ref, dk_ref, dv_ref),
                              w_ref[:, u * UNIT:(u + 1) * UNIT])
        xf = x_ref[...]
        r = lax.rsqrt(jnp.mean(xf * xf, axis=-1, keepdims=True) + RMS_EPS)
        xhat = xf * r
        dg_ref[...] += jnp.sum(dh * xhat, axis=0, keepdims=True)
        dxhat = dh * g_ref[...]
        dx_ref[...] = dxn_ref[...] + r * (dxhat - xhat * jnp.mean(dxhat * xhat, axis=-1, keepdims=True))

    tokd = pl.BlockSpec((tb, D_MODEL), lambda i: (i, 0))
    return pl.pallas_call(
        body, name="inproj_bwd_x",
        grid=(t // tb,),
        in_specs=[tokd, _const_spec((1, D_MODEL)), _const_spec((D_MODEL, IN_WIDTH))] + _dproj_specs(tb) + [tokd],
        out_specs=[tokd, pl.BlockSpec((1, D_MODEL), lambda i: (0, 0))],
        out_shape=[S((t, D_MODEL), F32), S((1, D_MODEL), F32)],
        compiler_params=_params(1),
    )(x, g_pre, w_in, *dproj_parts, dxn)


def _inproj_bwd_w(x, g_pre, dproj_parts):
    t = x.shape[0]
    tb = min(256, t)
    nt = t // tb

    def body(x_ref, g_ref, dpre_ref, dgates_ref, dq_ref, dk_ref, dv_ref, dw_hbm, acc_ref):
        i = pl.program_id(0)
        xf = x_ref[...]
        r = lax.rsqrt(jnp.mean(xf * xf, axis=-1, keepdims=True) + RMS_EPS)
        ht = (xf * r * g_ref[...]).T.astype(BF16)

        @pl.when(i == 0)
        def _():
            acc_ref[...] = jnp.zeros_like(acc_ref)

        for u in range(N_UNITS):
            acc_ref[:, u * UNIT:(u + 1) * UNIT] += _dot(
                ht, _dproj_unit(u, dpre_ref, dgates_ref, dq_ref, dk_ref, dv_ref))

        @pl.when(i == nt - 1)
        def _():
            pltpu.sync_copy(acc_ref, dw_hbm)

    return pl.pallas_call(
        body, name="inproj_bwd_w",
        grid=(nt,),
        in_specs=[pl.BlockSpec((tb, D_MODEL), lambda i: (i, 0)), _const_spec((1, D_MODEL))] + _dproj_specs(tb),
        out_specs=pl.BlockSpec(memory_space=pl.ANY),
        out_shape=S((D_MODEL, IN_WIDTH), F32),
        scratch_shapes=[pltpu.VMEM((D_MODEL, IN_WIDTH), F32)],
        compiler_params=_params(1),
    )(x, g_pre, *dproj_parts)


def _attn_tile_size(t):
    return min(256, t)


def _softplus_parts(z):
    sp = jnp.maximum(z, 0.0) + jnp.log(1.0 + jnp.exp(-jnp.abs(z)))
    return -sp, z - sp


def _attn_fwd(proj):
    t = proj.shape[1]
    tq = _attn_tile_size(t)

    def body(q_ref, k_ref, v_ref, o_ref, bs_ref, kf_ref):
        i = pl.program_id(1)
        row = lax.broadcasted_iota(jnp.int32, (tq, tq), 0)
        col = lax.broadcasted_iota(jnp.int32, (tq, tq), 1)
        tri = (row > col).astype(BF16)
        causal = col < row
        lane = lax.broadcasted_iota(jnp.int32, (1, LANES), 1)
        lane8 = lax.broadcasted_iota(jnp.int32, (8, LANES), 1)
        q_all = q_ref[0] * ATTN_SCALE
        o_out = jnp.zeros((tq, LANES), F32)
        b_out = jnp.zeros((tq, LANES), F32)
        kf_out = jnp.zeros((8, LANES), F32)

        for hh in range(HEADS_PER_BLOCK):
            in_head = (lane >= hh * HEAD_DIM) & (lane < (hh + 1) * HEAD_DIM)
            qh = jnp.where(in_head, q_all, 0.0).astype(BF16)

            def tile(kj, carry, acc, masked, qh=qh):
                ks = pl.multiple_of(kj * tq, tq)
                kh = k_ref[0, pl.ds(ks, tq), :].astype(BF16)
                vh = v_ref[0, pl.ds(ks, tq), :].astype(BF16)
                z = _dot_nt(qh, kh)
                lm, lb = _softplus_parts(z)
                if masked:
                    lm = jnp.where(causal, lm, 0.0)
                hi, lo = _split_bf16(lm)
                between = _dot(hi, tri) + _dot(lo, tri) + carry
                w = jnp.exp(lb + between)
                if masked:
                    w = jnp.where(causal, w, 0.0)
                acc = acc + _dot(w.astype(BF16), vh)
                carry = carry + jnp.sum(lm, axis=1, keepdims=True)
                return carry, acc

            carry, acc = tile(i, jnp.zeros((tq, 1), F32), jnp.zeros((tq, LANES), F32), True)

            def more(c):
                return jnp.logical_and(c[0] >= 0, jnp.max(c[1]) >= EXP_ZERO_BELOW)

            def step(c):
                carry, acc = tile(c[0], c[1], c[2], False)
                return c[0] - 1, carry, acc

            kj_end, carry, acc = lax.while_loop(more, step, (i - 1, carry, acc))
            o_out = jnp.where(in_head, acc, o_out)
            b_out = jnp.where(in_head, carry, b_out)
            in_head8 = (lane8 >= hh * HEAD_DIM) & (lane8 < (hh + 1) * HEAD_DIM)
            kf_out = jnp.where(in_head8, (kj_end + 1).astype(F32), kf_out)

        o_ref[...] = o_out
        bs_ref[0] = b_out
        kf_ref[0, 0] = kf_out

    nq = t // tq
    return pl.pallas_call(
        body, name="attn_fwd",
        grid=(N_HEAD_BLOCKS, nq),
        in_specs=[pl.BlockSpec((1, tq, LANES), lambda p, i: (U_Q, i, p)),
                  pl.BlockSpec((1, t, LANES), lambda p, i: (U_K, 0, p)),
                  pl.BlockSpec((1, t, LANES), lambda p, i: (U_V, 0, p))],
        out_specs=[pl.BlockSpec((tq, LANES), lambda p, i: (i, p)),
                   pl.BlockSpec((1, tq, LANES), lambda p, i: (p, i, 0)),
                   pl.BlockSpec((1, 1, 8, LANES), lambda p, i: (p, i, 0, 0))],
        out_shape=[S((t, UNIT), F32), S((N_HEAD_BLOCKS, t, LANES), F32), S((N_HEAD_BLOCKS, nq, 8, LANES), F32)],
        compiler_params=_params(2),
    )(proj, proj, proj)


def _attn_bwd(proj, d_o, bsum, kfirst):
    t = proj.shape[1]
    tq = _attn_tile_size(t)
    nq = t // tq

    def body(q_ref, k_ref, v_ref, do_ref, bs_ref, kf_ref, dq_ref, dk_ref, dv_ref, dkt_ref, dvt_ref):
        i = pl.program_id(1)

        @pl.when(i == 0)
        def _():
            dkt_ref[...] = jnp.zeros_like(dkt_ref)
            dvt_ref[...] = jnp.zeros_like(dvt_ref)

        row = lax.broadcasted_iota(jnp.int32, (tq, tq), 0)
        col = lax.broadcasted_iota(jnp.int32, (tq, tq), 1)
        upto = (row <= col).astype(BF16)
        before = (row < col).astype(BF16)
        causal = col < row
        lane = lax.broadcasted_iota(jnp.int32, (1, LANES), 1)
        lane8 = lax.broadcasted_iota(jnp.int32, (8, LANES), 1)
        q_all = q_ref[0] * ATTN_SCALE
        do_all = do_ref[...]
        bs_all = bs_ref[0]
        dq_out = jnp.zeros((tq, LANES), F32)

        for hh in range(HEADS_PER_BLOCK):
            in_head = (lane >= hh * HEAD_DIM) & (lane < (hh + 1) * HEAD_DIM)
            q_m = jnp.where(in_head, q_all, 0.0)
            do_m = jnp.where(in_head, do_all, 0.0)
            qh = q_m.astype(BF16)
            doh = do_m.astype(BF16)
            qt = q_m.T.astype(BF16)
            dot_ = do_m.T.astype(BF16)
            btot = jnp.max(jnp.where(in_head, bs_all, -jnp.inf), axis=1, keepdims=True)
            in_head8 = (lane8 >= hh * HEAD_DIM) & (lane8 < (hh + 1) * HEAD_DIM)
            k_first = jnp.clip(jnp.max(jnp.where(in_head8, kf_ref[0, 0], -jnp.inf)).astype(jnp.int32), 0, i)

            def tile(kj, c_b, c_p, dq, masked, qh=qh, doh=doh, qt=qt, dot_=dot_, btot=btot):
                ks = pl.multiple_of(kj * tq, tq)
                kh = k_ref[0, pl.ds(ks, tq), :].astype(BF16)
                vh = v_ref[0, pl.ds(ks, tq), :].astype(BF16)
                z = _dot_nt(qh, kh)
                lm, lb = _softplus_parts(z)
                if masked:
                    lm = jnp.where(causal, lm, 0.0)
                hi, lo = _split_bf16(lm)
                between = btot - (c_b + _dot(hi, upto) + _dot(lo, upto))
                w = jnp.exp(lb + between)
                if masked:
                    w = jnp.where(causal, w, 0.0)
                e = w * _dot_nt(doh, vh)
                ehi, elo = _split_bf16(e)
                p_sum = c_p + _dot(ehi, before) + _dot(elo, before)
                beta = jnp.exp(lb)
                dz = e * (1.0 - beta) - p_sum * beta
                if masked:
                    dz = jnp.where(causal, dz, 0.0)
                dzb = dz.astype(BF16)
                dq = dq + _dot(dzb, kh)
                dkt_ref[kj] += _dot(qt, dzb)
                dvt_ref[kj] += _dot(dot_, w.astype(BF16))
                c_b = c_b + jnp.sum(lm, axis=1, keepdims=True)
                c_p = c_p + jnp.sum(e, axis=1, keepdims=True)
                return c_b, c_p, dq

            init = (jnp.zeros((tq, 1), F32), jnp.zeros((tq, 1), F32), jnp.zeros((tq, LANES), F32))
            c_b, c_p, dq = lax.fori_loop(k_first, i, lambda kj, c: tile(kj, c[0], c[1], c[2], False), init)
            _, _, dq = tile(i, c_b, c_p, dq, True)
            dq_out = jnp.where(in_head, dq * ATTN_SCALE, dq_out)

        dq_ref[...] = dq_out

        @pl.when(i == nq - 1)
        def _():
            for kj in range(nq):
                dk_ref[kj * tq:(kj + 1) * tq, :] = dkt_ref[kj].T
                dv_ref[kj * tq:(kj + 1) * tq, :] = dvt_ref[kj].T

    return pl.pallas_call(
        body, name="attn_bwd",
        grid=(N_HEAD_BLOCKS, nq),
        in_specs=[pl.BlockSpec((1, tq, LANES), lambda p, i: (U_Q, i, p)),
                  pl.BlockSpec((1, t, LANES), lambda p, i: (U_K, 0, p)),
                  pl.BlockSpec((1, t, LANES), lambda p, i: (U_V, 0, p)),
                  pl.BlockSpec((tq, LANES), lambda p, i: (i, p)),
                  pl.BlockSpec((1, tq, LANES), lambda p, i: (p, i, 0)),
                  pl.BlockSpec((1, 1, 8, LANES), lambda p, i: (p, i, 0, 0))],
        out_specs=[pl.BlockSpec((tq, LANES), lambda p, i: (i, p)),
                   pl.BlockSpec((t, LANES), lambda p, i: (0, p)),
                   pl.BlockSpec((t, LANES), lambda p, i: (0, p))],
        out_shape=[S((t, UNIT), F32)] * 3,
        scratch_shapes=[pltpu.VMEM((nq, LANES, tq), F32), pltpu.VMEM((nq, LANES, tq), F32)],
        compiler_params=_params(2),
    )(proj, proj, proj, d_o, bsum, kfirst)


def _mix_in_specs(tb):
    hb = tb // HALO

    def unit(u):
        return pl.BlockSpec((1, tb, UNIT), lambda i: (u, i, 0))

    def halo(u):
        return pl.BlockSpec((1, HALO, UNIT), lambda i: (u, jnp.maximum(i * hb - 1, 0), 0))

    return [unit(U_P), halo(U_P), unit(U_GP), unit(U_CA), halo(U_CA), unit(U_CB), halo(U_CB), unit(U_GC),
            unit(U_GA), pl.BlockSpec((3, tb, UNIT), lambda i: (3, i, 0)), pl.BlockSpec((3, tb, UNIT), lambda i: (4, i, 0))]


def _mix_weight_specs():
    return [_const_spec((4, GROUP, GROUP)), _const_spec((1, UNIT)), _const_spec((1, UNIT)),
            _const_spec((UNIT, D_MODEL)), _const_spec((CONV_K_PAD, UNIT)), _const_spec((1, UNIT)),
            _const_spec((1, UNIT)), _const_spec((1, UNIT)), _const_spec((UNIT, D_MODEL)),
            _const_spec((UNIT, D_MODEL)), _const_spec((D_MODEL, D_MODEL)), _const_spec((1, D_MODEL))]


def _mix_forward(i, tb, proj_refs, o_ref, w_refs, pbuf, ubuf):
    p_ref, ph_ref, gp_ref, ca_ref, cah_ref, cb_ref, cbh_ref, gc_ref, ga_ref, gm0_ref, gm1_ref = proj_refs
    (poolw_ref, poolb_ref, pscale_ref, wpo_ref, convw_ref, convb_ref, lng_ref, lnb_ref,
     wco_ref, wao_ref, wo_ref, gpost_ref) = w_refs
    first = i == 0
    r = {}

    pbuf[0:HALO, :] = jnp.where(first, 0.0, ph_ref[0])
    pbuf[HALO:HALO + tb, :] = p_ref[0]
    tpos = i * tb + lax.broadcasted_iota(jnp.int32, (tb, 1), 0)
    d_parts, y_parts = [], []
    for g, win in enumerate(POOL_WINDOWS):
        cs = slice(g * GROUP, (g + 1) * GROUP)
        cur = pbuf[HALO:HALO + tb, cs]
        s = cur
        for j in range(1, win):
            s = s + pbuf[HALO - j:HALO - j + tb, cs]
        cnt = jnp.minimum(tpos + 1, win).astype(F32)
        d_g = s / cnt - cur
        d_parts.append(d_g)
        y_parts.append(_dot(d_g.astype(BF16), poolw_ref[g]))
    r["d"] = d_parts
    y = jnp.concatenate(y_parts, axis=1) + poolb_ref[...]
    r["y"] = y
    mp = y * pscale_ref[...]
    gp = gp_ref[0]
    sgp = _sigmoid(gp)
    r["mp"], r["gp"], r["sgp"] = mp, gp, sgp
    ua = mp * (gp * sgp)

    ah, bh = cah_ref[0], cbh_ref[0]
    ubuf[0:HALO, :] = jnp.where(first, 0.0, ah * _sigmoid(bh))
    a, b = ca_ref[0], cb_ref[0]
    sb = _sigmoid(b)
    ubuf[HALO:HALO + tb, :] = a * sb
    r["a"], r["sb"] = a, sb
    cv = jnp.zeros((tb, UNIT), F32) + convb_ref[...]
    off = HALO - (CONV_K - 1)
    for k in range(CONV_K):
        cv = cv + convw_ref[k:k + 1, :] * ubuf[off + k:off + k + tb, :]
    mu = jnp.mean(cv, axis=-1, keepdims=True)
    cc = cv - mu
    rs = lax.rsqrt(jnp.mean(cc * cc, axis=-1, keepdims=True) + LN_EPS)
    nrm = cc * rs
    ln = nrm * lng_ref[...] + lnb_ref[...]
    sln = _sigmoid(ln)
    sc = ln * sln
    gc = gc_ref[0]
    sgc = _sigmoid(gc)
    r["rs"], r["nrm"], r["ln"], r["sln"], r["sc"], r["gc"], r["sgc"] = rs, nrm, ln, sln, sc, gc, sgc
    ub = sc * (gc * sgc)

    o = o_ref[...]
    ga = ga_ref[0]
    sga = _sigmoid(ga)
    r["o"], r["ga"], r["sga"] = o, ga, sga
    uc = o * (ga * sga)

    r["ua"], r["ub"], r["uc"] = ua.astype(BF16), ub.astype(BF16), uc.astype(BF16)
    ya = _dot(r["ua"], wpo_ref[...])
    yb = _dot(r["ub"], wco_ref[...])
    yc = _dot(r["uc"], wao_ref[...])
    g0 = _sigmoid(jnp.concatenate([gm0_ref[0], gm0_ref[1]], axis=1))
    g1 = _sigmoid(jnp.concatenate([gm0_ref[2], gm1_ref[0]], axis=1))
    g2 = _sigmoid(jnp.concatenate([gm1_ref[1], gm1_ref[2]], axis=1))
    r["ya"], r["yb"], r["yc"], r["g0"], r["g1"], r["g2"] = ya, yb, yc, g0, g1, g2
    m = (g0 * ya + g1 * yb + g2 * yc).astype(BF16)
    r["m"] = m
    out = _dot(m, wo_ref[...])
    r2 = lax.rsqrt(jnp.mean(out * out, axis=-1, keepdims=True) + RMS_EPS)
    r["n2"], r["r2"] = out * r2, r2
    return r


def _mix_fwd(proj, o, x, weights):
    t = x.shape[0]
    tb = min(256, t)

    def body(*refs):
        proj_refs, o_ref, x_ref = refs[0:11], refs[11], refs[12]
        w_refs = refs[13:25]
        xn_ref, pbuf, ubuf = refs[25], refs[26], refs[27]
        r = _mix_forward(pl.program_id(0), tb, proj_refs, o_ref, w_refs, pbuf, ubuf)
        xn_ref[...] = x_ref[...] + r["n2"] * w_refs[11][...]

    return pl.pallas_call(
        body, name="mix_fwd",
        grid=(t // tb,),
        in_specs=_mix_in_specs(tb) + [pl.BlockSpec((tb, UNIT), lambda i: (i, 0)),
                                      pl.BlockSpec((tb, D_MODEL), lambda i: (i, 0))] + _mix_weight_specs(),
        out_specs=pl.BlockSpec((tb, D_MODEL), lambda i: (i, 0)),
        out_shape=S((t, D_MODEL), F32),
        scratch_shapes=[pltpu.VMEM((HALO + tb, UNIT), F32), pltpu.VMEM((HALO + tb, UNIT), F32)],
        compiler_params=_params(1),
    )(*([proj] * 11), o, x, *weights)


def _mix_bwd(proj, o, x, dxn, weights):
    t = x.shape[0]
    tb = min(128, t)

    def body(*refs):
        proj_refs, o_ref, x_ref, dxn_ref = refs[0:11], refs[11], refs[12], refs[13]
        w_refs = refs[14:26]
        dg_ref, dd_ref, dcv_ref, do_ref, dwo_ref, dwp_ref, dwc_ref, dwa_ref, dpw_ref, dvec_ref = refs[26:36]
        pbuf, ubuf = refs[36], refs[37]
        (poolw_ref, _, pscale_ref, wpo_ref, _, _, lng_ref, _, wco_ref, wao_ref, wo_ref, gpost_ref) = w_refs
        i = pl.program_id(0)

        @pl.when(i == 0)
        def _():
            for ref in (dwo_ref, dwp_ref, dwc_ref, dwa_ref, dpw_ref, dvec_ref):
                ref[...] = jnp.zeros_like(ref)

        r = _mix_forward(i, tb, proj_refs, o_ref, w_refs, pbuf, ubuf)

        def colsum(v):
            return jnp.sum(v, axis=0, keepdims=True)

        dxn = dxn_ref[...]
        n2 = r["n2"]
        dvec_ref[0:1, :] += colsum(dxn * n2)
        dn2 = dxn * gpost_ref[...]
        dout = (r["r2"] * (dn2 - n2 * jnp.mean(dn2 * n2, axis=-1, keepdims=True))).astype(BF16)
        dm = _dot_nt(dout, wo_ref[...])
        dwo_ref[...] += _dot_tn(r["m"], dout)

        g0, g1, g2 = r["g0"], r["g1"], r["g2"]
        dgm = [dm * r["ya"] * g0 * (1.0 - g0), dm * r["yb"] * g1 * (1.0 - g1), dm * r["yc"] * g2 * (1.0 - g2)]
        for bidx in range(N_BRANCH):
            dg_ref[3 + 2 * bidx] = dgm[bidx][:, 0:UNIT]
            dg_ref[4 + 2 * bidx] = dgm[bidx][:, UNIT:2 * UNIT]
        dya = (dm * g0).astype(BF16)
        dyb = (dm * g1).astype(BF16)
        dyc = (dm * g2).astype(BF16)
        dua = _dot_nt(dya, wpo_ref[...])
        dub = _dot_nt(dyb, wco_ref[...])
        duc = _dot_nt(dyc, wao_ref[...])
        dwp_ref[...] += _dot_tn(r["ua"], dya)
        dwc_ref[...] += _dot_tn(r["ub"], dyb)
        dwa_ref[...] += _dot_tn(r["uc"], dyc)

        gp, sgp = r["gp"], r["sgp"]
        dmp = dua * (gp * sgp)
        dg_ref[0] = dua * r["mp"] * _dsilu(gp, sgp)
        dvec_ref[2:3, 0:UNIT] += colsum(dmp * r["y"])
        dy = dmp * pscale_ref[...]
        dvec_ref[1:2, 0:UNIT] += colsum(dy)
        dd_parts = []
        for g in range(len(POOL_WINDOWS)):
            dy_g = dy[:, g * GROUP:(g + 1) * GROUP].astype(BF16)
            dd_parts.append(_dot_nt(dy_g, poolw_ref[g]))
            dpw_ref[g] += _dot_tn(r["d"][g].astype(BF16), dy_g)
        dd_ref[...] = jnp.concatenate(dd_parts, axis=1)

        gc, sgc = r["gc"], r["sgc"]
        dsc = dub * (gc * sgc)
        dg_ref[1] = dub * r["sc"] * _dsilu(gc, sgc)
        dln = dsc * _dsilu(r["ln"], r["sln"])
        nrm = r["nrm"]
        dvec_ref[4:5, 0:UNIT] += colsum(dln * nrm)
        dvec_ref[5:6, 0:UNIT] += colsum(dln)
        dnrm = dln * lng_ref[...]
        dcv = r["rs"] * (dnrm - jnp.mean(dnrm, axis=-1, keepdims=True)
                         - nrm * jnp.mean(dnrm * nrm, axis=-1, keepdims=True))
        dvec_ref[3:4, 0:UNIT] += colsum(dcv)
        dcv_ref[...] = dcv

        ga, sga = r["ga"], r["sga"]
        do_ref[...] = duc * (ga * sga)
        dg_ref[2] = duc * r["o"] * _dsilu(ga, sga)

    def acc_spec(shape):
        n = len(shape)
        return pl.BlockSpec(shape, lambda i: (0,) * n)

    tok = lambda w: pl.BlockSpec((tb, w), lambda i: (i, 0))
    return pl.pallas_call(
        body, name="mix_bwd",
        grid=(t // tb,),
        in_specs=_mix_in_specs(tb) + [tok(UNIT), tok(D_MODEL), tok(D_MODEL)] + _mix_weight_specs(),
        out_specs=[pl.BlockSpec((9, tb, UNIT), lambda i: (0, i, 0)), tok(UNIT), tok(UNIT), tok(UNIT),
                   acc_spec((D_MODEL, D_MODEL)), acc_spec((UNIT, D_MODEL)), acc_spec((UNIT, D_MODEL)),
                   acc_spec((UNIT, D_MODEL)), acc_spec((4, GROUP, GROUP)), acc_spec((8, D_MODEL))],
        out_shape=[S((9, t, UNIT), F32), S((t, UNIT), F32), S((t, UNIT), F32), S((t, UNIT), F32),
                   S((D_MODEL, D_MODEL), F32), S((UNIT, D_MODEL), F32), S((UNIT, D_MODEL), F32),
                   S((UNIT, D_MODEL), F32), S((4, GROUP, GROUP), F32), S((8, D_MODEL), F32)],
        scratch_shapes=[pltpu.VMEM((HALO + tb, UNIT), F32), pltpu.VMEM((HALO + tb, UNIT), F32)],
        compiler_params=_params(1),
    )(*([proj] * 11), o, x, dxn, *weights)


def _halo_bwd(proj, dcv, dd, conv_w):
    t = dcv.shape[0]
    tb = min(256, t)
    hb = tb // HALO
    n_halo_blocks = t // HALO
    nt = t // tb

    def body(ca_ref, cah_ref, cb_ref, cbh_ref, dcv_ref, dcvh_ref, dd_ref, ddh_ref, cw_ref,
             dpre_ref, dcw_ref, ubuf, gbuf, nbuf):
        i = pl.program_id(0)
        first = i == 0
        last = i == nt - 1

        @pl.when(first)
        def _():
            dcw_ref[...] = jnp.zeros_like(dcw_ref)

        a, b = ca_ref[0], cb_ref[0]
        sb = _sigmoid(b)
        ubuf[0:HALO, :] = jnp.where(first, 0.0, cah_ref[0] * _sigmoid(cbh_ref[0]))
        ubuf[HALO:HALO + tb, :] = a * sb
        dcv_main = dcv_ref[...]
        gbuf[0:tb, :] = dcv_main
        gbuf[tb:tb + HALO, :] = jnp.where(last, 0.0, dcvh_ref[...])

        off = HALO - (CONV_K - 1)
        du = jnp.zeros((tb, UNIT), F32)
        for k in range(CONV_K):
            du = du + cw_ref[k:k + 1, :] * gbuf[CONV_K - 1 - k:CONV_K - 1 - k + tb, :]
            dcw_ref[k:k + 1, :] += jnp.sum(dcv_main * ubuf[off + k:off + k + tb, :], axis=0, keepdims=True)
        dpre_ref[1] = du * sb
        dpre_ref[2] = du * a * sb * (1.0 - sb)

        tpos = i * tb + lax.broadcasted_iota(jnp.int32, (tb + HALO, 1), 0)
        dd_main = dd_ref[...]
        dd_ext = jnp.concatenate([dd_main, jnp.where(last, 0.0, ddh_ref[...])], axis=0)
        dp_parts = []
        for g, win in enumerate(POOL_WINDOWS):
            cs = slice(g * GROUP, (g + 1) * GROUP)
            cnt = jnp.minimum(tpos + 1, win).astype(F32)
            nbuf[:, cs] = dd_ext[:, cs] / cnt
        for g, win in enumerate(POOL_WINDOWS):
            cs = slice(g * GROUP, (g + 1) * GROUP)
            s = nbuf[0:tb, cs]
            for j in range(1, win):
                s = s + nbuf[j:j + tb, cs]
            dp_parts.append(s - dd_main[:, cs])
        dpre_ref[0] = jnp.concatenate(dp_parts, axis=1)

    def unit(u):
        return pl.BlockSpec((1, tb, UNIT), lambda i: (u, i, 0))

    def past(u):
        return pl.BlockSpec((1, HALO, UNIT), lambda i: (u, jnp.maximum(i * hb - 1, 0), 0))

    tok = pl.BlockSpec((tb, UNIT), lambda i: (i, 0))
    future = pl.BlockSpec((HALO, UNIT), lambda i: (jnp.minimum((i + 1) * hb, n_halo_blocks - 1), 0))
    return pl.pallas_call(
        body, name="halo_bwd",
        grid=(nt,),
        in_specs=[unit(U_CA), past(U_CA), unit(U_CB), past(U_CB), tok, future, tok, future,
                  _const_spec((CONV_K_PAD, UNIT))],
        out_specs=[pl.BlockSpec((3, tb, UNIT), lambda i: (0, i, 0)),
                   pl.BlockSpec((CONV_K_PAD, UNIT), lambda i: (0, 0))],
        out_shape=[S((3, t, UNIT), F32), S((CONV_K_PAD, UNIT), F32)],
        scratch_shapes=[pltpu.VMEM((HALO + tb, UNIT), F32), pltpu.VMEM((HALO + tb, UNIT), F32),
                        pltpu.VMEM((HALO + tb, UNIT), F32)],
        compiler_params=_params(1),
    )(proj, proj, proj, proj, dcv, dcv, dd, dd, conv_w)


def _loss_head(y, target):
    t = y.shape[0]
    tb = min(512, t)

    def body(y_ref, t_ref, loss_ref, dy_ref):
        @pl.when(pl.program_id(0) == 0)
        def _():
            loss_ref[...] = jnp.zeros_like(loss_ref)

        err = y_ref[...] - t_ref[...]
        dy_ref[...] = err * (1.0 / D_MODEL)
        part = 0.5 * jnp.sum(jnp.mean(err * err, axis=-1, keepdims=True), axis=0, keepdims=True)
        r8 = lax.broadcasted_iota(jnp.int32, (8, LANES), 0)
        c8 = lax.broadcasted_iota(jnp.int32, (8, LANES), 1)
        loss_ref[...] += jnp.where((r8 == 0) & (c8 == 0), part, 0.0)

    return pl.pallas_call(
        body, name="loss_head",
        grid=(t // tb,),
        in_specs=[pl.BlockSpec((tb, D_MODEL), lambda i: (i, 0))] * 2,
        out_specs=[pl.BlockSpec((8, LANES), lambda i: (0, 0)), pl.BlockSpec((tb, D_MODEL), lambda i: (i, 0))],
        out_shape=[S((8, LANES), F32), S((t, D_MODEL), F32)],
        compiler_params=_params(1),
    )(y, target)


def _adamw(name, parts, w, m, v):
    n, rows, cols = parts.shape
    rb = rows
    while rb * cols * 4 * (n + 7) * 2 > 24 * 1024 * 1024 and rb % 16 == 0:
        rb //= 2

    def body(p_ref, w_ref, m_ref, v_ref, g_ref, d_ref, nm_ref, nv_ref):
        g = p_ref[0]
        for k in range(1, n):
            g = g + p_ref[k]
        nm = ADAM_B1 * m_ref[...] + (1.0 - ADAM_B1) * g
        nv = ADAM_B2 * v_ref[...] + (1.0 - ADAM_B2) * (g * g)
        m_hat = nm / (1.0 - ADAM_B1 ** ADAM_STEP)
        v_hat = nv / (1.0 - ADAM_B2 ** ADAM_STEP)
        g_ref[...] = g
        d_ref[...] = -ADAM_LR * (m_hat / (jnp.sqrt(v_hat) + ADAM_EPS) + ADAM_WD * w_ref[...])
        nm_ref[...] = nm
        nv_ref[...] = nv

    blk = pl.BlockSpec((rb, cols), lambda i: (i, 0))
    return pl.pallas_call(
        body, name=name,
        grid=(rows // rb,),
        in_specs=[pl.BlockSpec((n, rb, cols), lambda i: (0, i, 0)), blk, blk, blk],
        out_specs=[blk] * 4,
        out_shape=[S((rows, cols), F32)] * 4,
        compiler_params=_params(1),
    )(parts, w, m, v)


def _axis_slice(ref, axis, block, size):
    idx = [slice(None)] * len(ref.shape)
    idx[axis] = pl.ds(pl.multiple_of(block * size, size), size)
    return ref.at[tuple(idx)]


def _axis_index(ref, axis, index):
    return ref.at[tuple([slice(None)] * axis + [index])]


def _run_copies(local, sends, recvs):
    for cp in local + sends:
        cp.start()
    for cp in recvs:
        cp.wait_recv()
    for cp in sends:
        cp.wait_send()
    for cp in local:
        cp.wait()


def _gather_weights(shards, axes):
    n = len(shards)
    fulls = [S(tuple(4 * d if k == ax else d for k, d in enumerate(s.shape)), s.dtype) for s, ax in zip(shards, axes)]

    def body(*refs):
        src, dst = refs[0:n], refs[n:2 * n]
        send_sems, recv_sems, local_sems = refs[2 * n:2 * n + 3]
        x, y, c = lax.axis_index("x"), lax.axis_index("y"), lax.axis_index("c")
        chips = [(1 - x, y), (x, 1 - y), (1 - x, 1 - y)]

        def place(a, chip):
            ax, size = axes[a], src[a].shape[axes[a]]
            idx = [slice(None)] * len(src[a].shape)
            idx[ax] = pl.ds(pl.multiple_of((2 * chip[0] + chip[1]) * size, size), size)
            return dst[a].at[tuple(idx)]

        def remote(a, j, chip_of_block, to):
            return pltpu.make_async_remote_copy(
                src_ref=src[a], dst_ref=place(a, chip_of_block), send_sem=send_sems.at[a, j],
                recv_sem=recv_sems.at[a, j], device_id=(to[0], to[1], c), device_id_type=MESH)

        local = [pltpu.make_async_copy(src[a], place(a, (x, y)), local_sems.at[a]) for a in range(n)]
        for cp in local:
            cp.start()
        sends = [remote(a, j, (x, y), chip) for a in range(n) for j, chip in enumerate(chips)]
        for cp in sends:
            cp.start()
        for a in range(n):
            for j, chip in enumerate(chips):
                remote(a, j, chip, chip).wait_recv()
        for cp in sends:
            cp.wait_send()
        for cp in local:
            cp.wait()

    any_spec = pl.BlockSpec(memory_space=pl.ANY)
    return pl.pallas_call(
        body, name="gather_weights",
        in_specs=[any_spec] * n, out_specs=[any_spec] * n, out_shape=fulls,
        scratch_shapes=[pltpu.SemaphoreType.DMA((n, 3)), pltpu.SemaphoreType.DMA((n, 3)),
                        pltpu.SemaphoreType.DMA((n,))],
    )(*shards)


def _swap_halves(partials, half_axes):
    n = len(partials)

    def half_shape(a):
        return tuple(d // 2 if k == half_axes[a] else d for k, d in enumerate(partials[a].shape))

    def body(*refs):
        src, dst = refs[0:n], refs[n:2 * n]
        send_sems, recv_sems, local_sems = refs[2 * n:2 * n + 3]
        x, y, c = lax.axis_index("x"), lax.axis_index("y"), lax.axis_index("c")

        def half(a, h):
            return _axis_slice(src[a], half_axes[a], h, src[a].shape[half_axes[a]] // 2)

        def remote(a, h):
            return pltpu.make_async_remote_copy(
                src_ref=half(a, h), dst_ref=dst[a].at[1], send_sem=send_sems.at[a], recv_sem=recv_sems.at[a],
                device_id=(x, y, 1 - c), device_id_type=MESH)

        local = [pltpu.make_async_copy(half(a, c), dst[a].at[0], local_sems.at[a]) for a in range(n)]
        sends = [remote(a, 1 - c) for a in range(n)]
        _run_copies(local, sends, [remote(a, c) for a in range(n)])

    any_spec = pl.BlockSpec(memory_space=pl.ANY)
    return pl.pallas_call(
        body, name="swap_halves",
        in_specs=[any_spec] * n, out_specs=[any_spec] * n,
        out_shape=[S((2,) + half_shape(a), partials[a].dtype) for a in range(n)],
        scratch_shapes=[pltpu.SemaphoreType.DMA((n,))] * 3,
    )(*partials)


def _send_to_owners(chip_halves, shard_axes, packed):
    n = len(chip_halves)

    def owned_shape(a):
        return tuple(d // 4 if k == shard_axes[a] else d for k, d in enumerate(chip_halves[a].shape))

    flips = [(dx, dy, dc) for dx in (0, 1) for dy in (0, 1) for dc in (0, 1)][1:]

    def body(*refs):
        src, psrc, dst, pdst = refs[0:n], refs[n], refs[n + 1:2 * n + 1], refs[2 * n + 1]
        send_sems, recv_sems, local_sems, psend_sems, precv_sems = refs[2 * n + 2:2 * n + 7]
        x, y, c = lax.axis_index("x"), lax.axis_index("y"), lax.axis_index("c")
        my_chip = 2 * x + y
        chips = [(1 - x, y), (x, 1 - y), (1 - x, 1 - y)]

        def owned(a, chip):
            return _axis_slice(src[a], shard_axes[a], 2 * chip[0] + chip[1], src[a].shape[shard_axes[a]] // 4)

        def remote(a, j, to, from_chip):
            return pltpu.make_async_remote_copy(
                src_ref=owned(a, to), dst_ref=dst[a].at[from_chip], send_sem=send_sems.at[a, j],
                recv_sem=recv_sems.at[a, j], device_id=(to[0], to[1], c), device_id_type=MESH)

        def peer(f):
            return (1 - x if f[0] else x, 1 - y if f[1] else y, 1 - c if f[2] else c)

        def premote(j, to, from_dev):
            return pltpu.make_async_remote_copy(
                src_ref=psrc, dst_ref=pdst.at[from_dev], send_sem=psend_sems.at[j], recv_sem=precv_sems.at[j],
                device_id=to, device_id_type=MESH)

        local = [pltpu.make_async_copy(owned(a, (x, y)), dst[a].at[my_chip], local_sems.at[a]) for a in range(n)]
        local.append(pltpu.make_async_copy(psrc, pdst.at[4 * x + 2 * y + c], local_sems.at[n]))
        sends = [remote(a, j, chip, my_chip) for a in range(n) for j, chip in enumerate(chips)]
        sends += [premote(j, peer(f), 4 * x + 2 * y + c) for j, f in enumerate(flips)]
        recvs = [remote(a, j, (x, y), 2 * chip[0] + chip[1]) for a in range(n) for j, chip in enumerate(chips)]
        for j, f in enumerate(flips):
            px, py, pc = peer(f)
            recvs.append(premote(j, (x, y, c), 4 * px + 2 * py + pc))
        _run_copies(local, sends, recvs)

    any_spec = pl.BlockSpec(memory_space=pl.ANY)
    return pl.pallas_call(
        body, name="send_to_owners",
        in_specs=[any_spec] * (n + 1), out_specs=[any_spec] * (n + 1),
        out_shape=[S((4,) + owned_shape(a), chip_halves[a].dtype) for a in range(n)]
        + [S((N_DEV,) + packed.shape, packed.dtype)],
        scratch_shapes=[pltpu.SemaphoreType.DMA((n, 3)), pltpu.SemaphoreType.DMA((n, 3)),
                        pltpu.SemaphoreType.DMA((n + 1,)), pltpu.SemaphoreType.DMA((7,)),
                        pltpu.SemaphoreType.DMA((7,))],
    )(*chip_halves, packed)


def _join_halves(reduced, half_axes):
    n = len(reduced)

    def joined_shape(a):
        s, ax = reduced[a].shape, half_axes[a]
        return s[:ax] + (2,) + s[ax:]

    def body(*refs):
        src, dst = refs[0:n], refs[n:2 * n]
        send_sems, recv_sems, local_sems = refs[2 * n:2 * n + 3]
        x, y, c = lax.axis_index("x"), lax.axis_index("y"), lax.axis_index("c")

        def remote(a, h):
            return pltpu.make_async_remote_copy(
                src_ref=src[a], dst_ref=_axis_index(dst[a], half_axes[a], h), send_sem=send_sems.at[a],
                recv_sem=recv_sems.at[a], device_id=(x, y, 1 - c), device_id_type=MESH)

        local = [pltpu.make_async_copy(src[a], _axis_index(dst[a], half_axes[a], c), local_sems.at[a])
                 for a in range(n)]
        _run_copies(local, [remote(a, c) for a in range(n)], [remote(a, 1 - c) for a in range(n)])

    any_spec = pl.BlockSpec(memory_space=pl.ANY)
    return pl.pallas_call(
        body, name="join_halves",
        in_specs=[any_spec] * n, out_specs=[any_spec] * n,
        out_shape=[S(joined_shape(a), reduced[a].dtype) for a in range(n)],
        scratch_shapes=[pltpu.SemaphoreType.DMA((n,))] * 3,
    )(*reduced)


def _sum_slots(name, parts):
    n = parts.shape[0]
    shape = parts.shape[1:]
    flat = parts.reshape((n, -1, shape[-1]))
    rows, cols = flat.shape[1:]
    rb = rows
    while rb * cols * 4 * (n + 1) * 2 > 24 * 1024 * 1024 and rb % 16 == 0:
        rb //= 2

    def body(p_ref, o_ref):
        acc = p_ref[0]
        for k in range(1, n):
            acc = acc + p_ref[k]
        o_ref[...] = acc

    out = pl.pallas_call(
        body, name=name,
        grid=(rows // rb,),
        in_specs=[pl.BlockSpec((n, rb, cols), lambda i: (0, i, 0))],
        out_specs=pl.BlockSpec((rb, cols), lambda i: (i, 0)),
        out_shape=S((rows, cols), F32),
        compiler_params=_params(1),
    )(flat)
    return out.reshape(shape)


SMALL_NAMES = ("norm_pre", "pool_w", "pool_b", "pool_scale", "conv_b", "conv_ln_g", "conv_ln_b", "norm_post")
BIG_NAMES = ("w_in", "w_pool_out", "conv_w", "w_conv_out", "w_attn_out", "w_o")
SHARD_AXIS = {"w_in": 2, "w_pool_out": 2, "conv_w": 2, "w_conv_out": 2, "w_attn_out": 2, "w_o": 1}
WEIGHT_ORDER = ("norm_pre", "w_in", "pool_w", "pool_b", "pool_scale", "w_pool_out", "conv_w", "conv_b",
                "conv_ln_g", "conv_ln_b", "w_conv_out", "w_attn_out", "w_o", "norm_post")


def _pack_small(parts):
    return jnp.concatenate([parts[n].reshape(-1, LANES) for n in SMALL_NAMES], axis=0)


def _unpack_small(packed, shapes):
    out, r0 = {}, 0
    for n in SMALL_NAMES:
        size = 1
        for d in shapes[n]:
            size *= d
        rows = size // LANES
        out[n] = packed[r0:r0 + rows].reshape(shapes[n])
        r0 += rows
    return out


def _layer_weights(full, small, l):
    row = lambda a: a[l][None, :]
    return (small["pool_w"][l].astype(BF16), small["pool_b"][l].reshape(1, UNIT), row(small["pool_scale"]),
            full["w_pool_out"][l], full["conv_w"][l], row(small["conv_b"]), row(small["conv_ln_g"]),
            row(small["conv_ln_b"]), full["w_conv_out"][l], full["w_attn_out"][l], full["w_o"][l],
            row(small["norm_post"]))


def _forward_backward(x, target, full, small):
    depth = full["w_in"].shape[0]
    acts = []
    for l in range(depth):
        g_pre = small["norm_pre"][l][None, :]
        proj = _inproj_fwd(x, g_pre, full["w_in"][l])
        o, bsum, kfirst = _attn_fwd(proj)
        weights = _layer_weights(full, small, l)
        x_next = _mix_fwd(proj, o, x, weights)
        acts.append((x, proj, o, bsum, kfirst, weights, g_pre))
        x = x_next
    loss_tile, dx = _loss_head(x, target)

    big = {n: [None] * depth for n in BIG_NAMES}
    sm = {n: [None] * depth for n in SMALL_NAMES}
    for l in reversed(range(depth)):
        x_in, proj, o, bsum, kfirst, weights, g_pre = acts[l]
        dgates, dd, dcv, d_o, dwo, dwp, dwc, dwa, dpw, dvec = _mix_bwd(proj, o, x_in, dx, weights)
        dpre, dcw = _halo_bwd(proj, dcv, dd, full["conv_w"][l])
        dq, dk, dv = _attn_bwd(proj, d_o, bsum, kfirst)
        dproj_parts = (dpre, dgates, dq, dk, dv)
        dx, dg_pre = _inproj_bwd_x(x_in, g_pre, full["w_in"][l], dproj_parts, dx)
        big["w_in"][l] = _inproj_bwd_w(x_in, g_pre, dproj_parts)
        big["w_pool_out"][l], big["w_conv_out"][l], big["w_attn_out"][l], big["w_o"][l] = dwp, dwc, dwa, dwo
        big["conv_w"][l] = dcw
        sm["norm_pre"][l] = dg_pre[0]
        sm["pool_w"][l] = dpw
        sm["norm_post"][l] = dvec[0]
        sm["pool_b"][l] = dvec[1, 0:UNIT].reshape(4, GROUP)
        sm["pool_scale"][l] = dvec[2, 0:UNIT]
        sm["conv_b"][l] = dvec[3, 0:UNIT]
        sm["conv_ln_g"][l] = dvec[4, 0:UNIT]
        sm["conv_ln_b"][l] = dvec[5, 0:UNIT]
    big = {n: jnp.stack(v) for n, v in big.items()}
    sm = {n: jnp.stack(v) for n, v in sm.items()}
    return loss_tile, dx, big, sm


def kernel(x, norm_pre, w_in, pool_w, pool_b, pool_scale, w_pool_out, conv_w, conv_b, conv_ln_g, conv_ln_b, w_conv_out, w_attn_out, w_o, norm_post, loss_target, m_norm_pre, m_w_in, m_pool_w, m_pool_b, m_pool_scale, m_w_pool_out, m_conv_w, m_conv_b, m_conv_ln_g, m_conv_ln_b, m_w_conv_out, m_w_attn_out, m_w_o, m_norm_post, v_norm_pre, v_w_in, v_pool_w, v_pool_b, v_pool_scale, v_w_pool_out, v_conv_w, v_conv_b, v_conv_ln_g, v_conv_ln_b, v_w_conv_out, v_w_attn_out, v_w_o, v_norm_post):
    w = dict(norm_pre=norm_pre, w_in=w_in, pool_w=pool_w, pool_b=pool_b, pool_scale=pool_scale,
             w_pool_out=w_pool_out, conv_w=conv_w, conv_b=conv_b, conv_ln_g=conv_ln_g, conv_ln_b=conv_ln_b,
             w_conv_out=w_conv_out, w_attn_out=w_attn_out, w_o=w_o, norm_post=norm_post)
    m = dict(norm_pre=m_norm_pre, w_in=m_w_in, pool_w=m_pool_w, pool_b=m_pool_b, pool_scale=m_pool_scale,
             w_pool_out=m_w_pool_out, conv_w=m_conv_w, conv_b=m_conv_b, conv_ln_g=m_conv_ln_g,
             conv_ln_b=m_conv_ln_b, w_conv_out=m_w_conv_out, w_attn_out=m_w_attn_out, w_o=m_w_o,
             norm_post=m_norm_post)
    v = dict(norm_pre=v_norm_pre, w_in=v_w_in, pool_w=v_pool_w, pool_b=v_pool_b, pool_scale=v_pool_scale,
             w_pool_out=v_w_pool_out, conv_w=v_conv_w, conv_b=v_conv_b, conv_ln_g=v_conv_ln_g,
             conv_ln_b=v_conv_ln_b, w_conv_out=v_w_conv_out, w_attn_out=v_w_attn_out, w_o=v_w_o,
             norm_post=v_norm_post)
    pad_taps = lambda a: jnp.pad(a, ((0, 0), (0, CONV_K_PAD - CONV_K), (0, 0)))

    shards = [pad_taps(w[n]) if n == "conv_w" else w[n].astype(BF16) for n in BIG_NAMES]
    full = dict(zip(BIG_NAMES, _gather_weights(shards, [SHARD_AXIS[n] for n in BIG_NAMES])))
    small = {n: w[n] for n in SMALL_NAMES}

    loss_tile, dx, big, sm = _forward_backward(x[0], loss_target[0], full, small)

    packed = jnp.concatenate([_pack_small(sm), loss_tile], axis=0)
    layer_axis = [0] * len(BIG_NAMES)
    pairs = _swap_halves([big[n] for n in BIG_NAMES], layer_axis)
    chip_sums = [_sum_slots("sum_cores_" + n, p) for n, p in zip(BIG_NAMES, pairs)]
    exchanged = _send_to_owners(chip_sums, [SHARD_AXIS[n] for n in BIG_NAMES], packed)
    reduced = [_sum_slots("sum_chips_" + n, p) for n, p in zip(BIG_NAMES, exchanged[:-1])]
    joined = _join_halves(reduced, layer_axis)
    out = {}
    for n, g in zip(BIG_NAMES, joined):
        wn, mn, vn = (pad_taps(a[n]) if n == "conv_w" else a[n] for a in (w, m, v))
        shape = wn.shape
        flat = lambda a: a.reshape((-1, shape[-1]))
        res = _adamw("adamw_" + n, g.reshape((1, -1, shape[-1])), flat(wn), flat(mn), flat(vn))
        res = [r.reshape(shape) for r in res]
        out[n] = [r[:, :CONV_K] for r in res] if n == "conv_w" else res
    parts = exchanged[-1]
    n_small_rows = parts.shape[1] - 8
    zeros_tile = jnp.zeros((8, LANES), F32)
    packs = [jnp.concatenate([_pack_small(a), zeros_tile], axis=0) for a in (w, m, v)]
    res = _adamw("adamw_small", parts, *packs)
    loss = res[0][n_small_rows, 0]
    shapes = {n: w[n].shape for n in SMALL_NAMES}
    unpacked = [_unpack_small(r[:n_small_rows], shapes) for r in res]
    for n in SMALL_NAMES:
        out[n] = [u[n] for u in unpacked]

    grads = [out[n][0] for n in WEIGHT_ORDER]
    deltas = [out[n][1] for n in WEIGHT_ORDER]
    new_m = [out[n][2] for n in WEIGHT_ORDER]
    new_v = [out[n][3] for n in WEIGHT_ORDER]
    return (loss, dx[None], *grads, *deltas, *new_m, *new_v)
```

```python
import jax
import jax.numpy as jnp
from jax import lax
from jax.experimental import pallas as pl
from jax.experimental.pallas import tpu as pltpu

F32 = jnp.float32
BF16 = jnp.bfloat16

D_MODEL = 1024
UNIT = 512
N_UNITS = 15
IN_WIDTH = UNIT * N_UNITS
N_HEADS = 8
HEAD_DIM = 64
HEADS_PER_BLOCK = 2
N_HEAD_BLOCKS = N_HEADS // HEADS_PER_BLOCK
LANES = 128
CONV_K = 31
CONV_K_PAD = 32
HALO = 32
POOL_WINDOWS = (2, 4, 8, 16)
GROUP = 128
N_BRANCH = 3
RMS_EPS = 1e-6
LN_EPS = 1e-5
ATTN_SCALE = 0.125
EXP_ZERO_BELOW = -104.0

ADAM_LR = 0.001
ADAM_B1 = 0.9
ADAM_B2 = 0.999
ADAM_EPS = 1e-08
ADAM_WD = 0.01
ADAM_STEP = 10

U_P, U_GP, U_CA, U_CB, U_GC, U_Q, U_K, U_V, U_GA, U_GM = 0, 1, 2, 3, 4, 5, 6, 7, 8, 9

V7X_VMEM_LIMIT = 62 * 1024 * 1024
N_DEV = 8
MESH = pl.DeviceIdType.MESH

S = jax.ShapeDtypeStruct


def _params(n_grid):
    return pltpu.CompilerParams(dimension_semantics=("arbitrary",) * n_grid, vmem_limit_bytes=V7X_VMEM_LIMIT)


def _sigmoid(x):
    return 1.0 / (1.0 + jnp.exp(-x))


def _dsilu(x, s):
    return s * (1.0 + x * (1.0 - s))


def _dot(a, b):
    return jnp.dot(a, b, preferred_element_type=F32)


def _dot_nt(a, b):
    return lax.dot_general(a, b, (((1,), (1,)), ((), ())), preferred_element_type=F32)


def _dot_tn(a, b):
    return lax.dot_general(a, b, (((0,), (0,)), ((), ())), preferred_element_type=F32)


def _split_bf16(x):
    hi = x.astype(BF16)
    lo = (x - hi.astype(F32)).astype(BF16)
    return hi, lo


def _const_spec(shape):
    n = len(shape)
    return pl.BlockSpec(shape, lambda *_: (0,) * n, pipeline_mode=pl.Buffered(1))


def _inproj_fwd(x, g_pre, w_in):
    t = x.shape[0]
    tb = min(1024, t)
    ug = 3
    nb = ug * UNIT

    def body(x_ref, g_ref, w_ref, o_ref, h_ref):
        @pl.when(pl.program_id(1) == 0)
        def _():
            xf = x_ref[...]
            r = lax.rsqrt(jnp.mean(xf * xf, axis=-1, keepdims=True) + RMS_EPS)
            h_ref[...] = (xf * r * g_ref[...]).astype(BF16)

        acc = _dot(h_ref[...], w_ref[...])
        for u in range(ug):
            o_ref[u] = acc[:, u * UNIT:(u + 1) * UNIT]

    return pl.pallas_call(
        body, name="inproj_fwd",
        grid=(t // tb, N_UNITS // ug),
        in_specs=[pl.BlockSpec((tb, D_MODEL), lambda i, j: (i, 0)),
                  pl.BlockSpec((1, D_MODEL), lambda i, j: (0, 0)),
                  pl.BlockSpec((D_MODEL, nb), lambda i, j: (0, j))],
        out_specs=pl.BlockSpec((ug, tb, UNIT), lambda i, j: (j, i, 0)),
        out_shape=S((N_UNITS, t, UNIT), F32),
        scratch_shapes=[pltpu.VMEM((tb, D_MODEL), BF16)],
        compiler_params=_params(2),
    )(x, g_pre, w_in)


def _dproj_specs(tb):
    tok = pl.BlockSpec((tb, UNIT), lambda i: (i, 0))
    return [pl.BlockSpec((3, tb, UNIT), lambda i: (0, i, 0)), pl.BlockSpec((9, tb, UNIT), lambda i: (0, i, 0)),
            tok, tok, tok]


def _dproj_unit(u, dpre_ref, dgates_ref, dq_ref, dk_ref, dv_ref):
    pre = {U_P: 0, U_CA: 1, U_CB: 2}
    gate = {U_GP: 0, U_GC: 1, U_GA: 2}
    if u in pre:
        val = dpre_ref[pre[u]]
    elif u in gate:
        val = dgates_ref[gate[u]]
    elif u >= U_GM:
        val = dgates_ref[3 + u - U_GM]
    else:
        val = {U_Q: dq_ref, U_K: dk_ref, U_V: dv_ref}[u][...]
    return val.astype(BF16)


def _inproj_bwd_x(x, g_pre, w_in, dproj_parts, dxn):
    t = x.shape[0]
    tb = min(256, t)

    def body(x_ref, g_ref, w_ref, dpre_ref, dgates_ref, dq_ref, dk_ref, dv_ref, dxn_ref, dx_ref, dg_ref):
        @pl.when(pl.program_id(0) == 0)
        def _():
            dg_ref[...] = jnp.zeros_like(dg_ref)

        dh = jnp.zeros((tb, D_MODEL), F32)
        for u in range(N_UNITS):
            dh = dh + _dot_nt(_dproj_unit(u, dpre_ref, dgates_ref, dq_ref, dk_ref, dv_ref),
                              w_ref[:, u * UNIT:(u + 1) * UNIT])
        xf = x_ref[...]
        r = lax.rsqrt(jnp.mean(xf * xf, axis=-1, keepdims=True) + RMS_EPS)
        xhat = xf * r
        dg_ref[...] += jnp.sum(dh * xhat, axis=0, keepdims=True)
        dxhat = dh * g_ref[...]
        dx_ref[...] = dxn_ref[...] + r * (dxhat - xhat * jnp.mean(dxhat * xhat, axis=-1, keepdims=True))

    tokd = pl.BlockSpec((tb, D_MODEL), lambda i: (i, 0))
    return pl.pallas_call(
        body, name="inproj_bwd_x",
        grid=(t // tb,),
        in_specs=[tokd, _const_spec((1, D_MODEL)), _const_spec((D_MODEL, IN_WIDTH))] + _dproj_specs(tb) + [tokd],
        out_specs=[tokd, pl.BlockSpec((1, D_MODEL), lambda i: (0, 0))],
        out_shape=[S((t, D_MODEL), F32), S((1, D_MODEL), F32)],
        compiler_params=_params(1),
    )(x, g_pre, w_in, *dproj_parts, dxn)


def _inproj_bwd_w(x, g_pre, dproj_parts):
    t = x.shape[0]
    tb = min(256, t)
    nt = t // tb

    def body(x_ref, g_ref, dpre_ref, dgates_ref, dq_ref, dk_ref, dv_ref, dw_hbm, acc_ref):
        i = pl.program_id(0)
        xf = x_ref[...]
        r = lax.rsqrt(jnp.mean(xf * xf, axis=-1, keepdims=True) + RMS_EPS)
        ht = (xf * r * g_ref[...]).T.astype(BF16)

        @pl.when(i == 0)
        def _():
            acc_ref[...] = jnp.zeros_like(acc_ref)

        for u in range(N_UNITS):
            acc_ref[:, u * UNIT:(u + 1) * UNIT] += _dot(
                ht, _dproj_unit(u, dpre_ref, dgates_ref, dq_ref, dk_ref, dv_ref))

        @pl.when(i == nt - 1)
        def _():
            pltpu.sync_copy(acc_ref, dw_hbm)

    return pl.pallas_call(
        body, name="inproj_bwd_w",
        grid=(nt,),
        in_specs=[pl.BlockSpec((tb, D_MODEL), lambda i: (i, 0)), _const_spec((1, D_MODEL))] + _dproj_specs(tb),
        out_specs=pl.BlockSpec(memory_space=pl.ANY),
        out_shape=S((D_MODEL, IN_WIDTH), F32),
        scratch_shapes=[pltpu.VMEM((D_MODEL, IN_WIDTH), F32)],
        compiler_params=_params(1),
    )(x, g_pre, *dproj_parts)


def _attn_tile_size(t):
    return min(256, t)


def _softplus_parts(z):
    sp = jnp.maximum(z, 0.0) + jnp.log(1.0 + jnp.exp(-jnp.abs(z)))
    return -sp, z - sp


def _attn_fwd(proj):
    t = proj.shape[1]
    tq = _attn_tile_size(t)

    def body(q_ref, k_ref, v_ref, o_ref, bs_ref, kf_ref):
        i = pl.program_id(1)
        row = lax.broadcasted_iota(jnp.int32, (tq, tq), 0)
        col = lax.broadcasted_iota(jnp.int32, (tq, tq), 1)
        tri = (row > col).astype(BF16)
        causal = col < row
        lane = lax.broadcasted_iota(jnp.int32, (1, LANES), 1)
        lane8 = lax.broadcasted_iota(jnp.int32, (8, LANES), 1)
        q_all = q_ref[0] * ATTN_SCALE
        o_out = jnp.zeros((tq, LANES), F32)
        b_out = jnp.zeros((tq, LANES), F32)
        kf_out = jnp.zeros((8, LANES), F32)

        for hh in range(HEADS_PER_BLOCK):
            in_head = (lane >= hh * HEAD_DIM) & (lane < (hh + 1) * HEAD_DIM)
            qh = jnp.where(in_head, q_all, 0.0).astype(BF16)

            def tile(kj, carry, acc, masked, qh=qh):
                ks = pl.multiple_of(kj * tq, tq)
                kh = k_ref[0, pl.ds(ks, tq), :].astype(BF16)
                vh = v_ref[0, pl.ds(ks, tq), :].astype(BF16)
                z = _dot_nt(qh, kh)
                lm, lb = _softplus_parts(z)
                if masked:
                    lm = jnp.where(causal, lm, 0.0)
                hi, lo = _split_bf16(lm)
                between = _dot(hi, tri) + _dot(lo, tri) + carry
                w = jnp.exp(lb + between)
                if masked:
                    w = jnp.where(causal, w, 0.0)
                acc = acc + _dot(w.astype(BF16), vh)
                carry = carry + jnp.sum(lm, axis=1, keepdims=True)
                return carry, acc

            carry, acc = tile(i, jnp.zeros((tq, 1), F32), jnp.zeros((tq, LANES), F32), True)

            def more(c):
                return jnp.logical_and(c[0] >= 0, jnp.max(c[1]) >= EXP_ZERO_BELOW)

            def step(c):
                carry, acc = tile(c[0], c[1], c[2], False)
                return c[0] - 1, carry, acc

            kj_end, carry, acc = lax.while_loop(more, step, (i - 1, carry, acc))
            o_out = jnp.where(in_head, acc, o_out)
            b_out = jnp.where(in_head, carry, b_out)
            in_head8 = (lane8 >= hh * HEAD_DIM) & (lane8 < (hh + 1) * HEAD_DIM)
            kf_out = jnp.where(in_head8, (kj_end + 1).astype(F32), kf_out)

        o_ref[...] = o_out
        bs_ref[0] = b_out
        kf_ref[0, 0] = kf_out

    nq = t // tq
    return pl.pallas_call(
        body, name="attn_fwd",
        grid=(N_HEAD_BLOCKS, nq),
        in_specs=[pl.BlockSpec((1, tq, LANES), lambda p, i: (U_Q, i, p)),
                  pl.BlockSpec((1, t, LANES), lambda p, i: (U_K, 0, p)),
                  pl.BlockSpec((1, t, LANES), lambda p, i: (U_V, 0, p))],
        out_specs=[pl.BlockSpec((tq, LANES), lambda p, i: (i, p)),
                   pl.BlockSpec((1, tq, LANES), lambda p, i: (p, i, 0)),
                   pl.BlockSpec((1, 1, 8, LANES), lambda p, i: (p, i, 0, 0))],
        out_shape=[S((t, UNIT), F32), S((N_HEAD_BLOCKS, t, LANES), F32), S((N_HEAD_BLOCKS, nq, 8, LANES), F32)],
        compiler_params=_params(2),
    )(proj, proj, proj)


def _attn_bwd(proj, d_o, bsum, kfirst):
    t = proj.shape[1]
    tq = _attn_tile_size(t)
    nq = t // tq

    def body(q_ref, k_ref, v_ref, do_ref, bs_ref, kf_ref, dq_ref, dk_ref, dv_ref, dkt_ref, dvt_ref):
        i = pl.program_id(1)

        @pl.when(i == 0)
        def _():
            dkt_ref[...] = jnp.zeros_like(dkt_ref)
            dvt_ref[...] = jnp.zeros_like(dvt_ref)

        row = lax.broadcasted_iota(jnp.int32, (tq, tq), 0)
        col = lax.broadcasted_iota(jnp.int32, (tq, tq), 1)
        upto = (row <= col).astype(BF16)
        before = (row < col).astype(BF16)
        causal = col < row
        lane = lax.broadcasted_iota(jnp.int32, (1, LANES), 1)
        lane8 = lax.broadcasted_iota(jnp.int32, (8, LANES), 1)
        q_all = q_ref[0] * ATTN_SCALE
        do_all = do_ref[...]
        bs_all = bs_ref[0]
        dq_out = jnp.zeros((tq, LANES), F32)

        for hh in range(HEADS_PER_BLOCK):
            in_head = (lane >= hh * HEAD_DIM) & (lane < (hh + 1) * HEAD_DIM)
            q_m = jnp.where(in_head, q_all, 0.0)
            do_m = jnp.where(in_head, do_all, 0.0)
            qh = q_m.astype(BF16)
            doh = do_m.astype(BF16)
            qt = q_m.T.astype(BF16)
            dot_ = do_m.T.astype(BF16)
            btot = jnp.max(jnp.where(in_head, bs_all, -jnp.inf), axis=1, keepdims=True)
            in_head8 = (lane8 >= hh * HEAD_DIM) & (lane8 < (hh + 1) * HEAD_DIM)
            k_first = jnp.clip(jnp.max(jnp.where(in_head8, kf_ref[0, 0], -jnp.inf)).astype(jnp.int32), 0, i)

            def tile(kj, c_b, c_p, dq, masked, qh=qh, doh=doh, qt=qt, dot_=dot_, btot=btot):
                ks = pl.multiple_of(kj * tq, tq)
                kh = k_ref[0, pl.ds(ks, tq), :].astype(BF16)
                vh = v_ref[0, pl.ds(ks, tq), :].astype(BF16)
                z = _dot_nt(qh, kh)
                lm, lb = _softplus_parts(z)
                if masked:
                    lm = jnp.where(causal, lm, 0.0)
                hi, lo = _split_bf16(lm)
                between = btot - (c_b + _dot(hi, upto) + _dot(lo, upto))
                w = jnp.exp(lb + between)
                if masked:
                    w = jnp.where(causal, w, 0.0)
                e = w * _dot_nt(doh, vh)
                ehi, elo = _split_bf16(e)
                p_sum = c_p + _dot(ehi, before) + _dot(elo, before)
                beta = jnp.exp(lb)
                dz = e * (1.0 - beta) - p_sum * beta
                if masked:
                    dz = jnp.where(causal, dz, 0.0)
                dzb = dz.astype(BF16)
                dq = dq + _dot(dzb, kh)
                dkt_ref[kj] += _dot(qt, dzb)
                dvt_ref[kj] += _dot(dot_, w.astype(BF16))
                c_b = c_b + jnp.sum(lm, axis=1, keepdims=True)
                c_p = c_p + jnp.sum(e, axis=1, keepdims=True)
                return c_b, c_p, dq

            init = (jnp.zeros((tq, 1), F32), jnp.zeros((tq, 1), F32), jnp.zeros((tq, LANES), F32))
            c_b, c_p, dq = lax.fori_loop(k_first, i, lambda kj, c: tile(kj, c[0], c[1], c[2], False), init)
            _, _, dq = tile(i, c_b, c_p, dq, True)
            dq_out = jnp.where(in_head, dq * ATTN_SCALE, dq_out)

        dq_ref[...] = dq_out

        @pl.when(i == nq - 1)
        def _():
            for kj in range(nq):
                dk_ref[kj * tq:(kj + 1) * tq, :] = dkt_ref[kj].T
                dv_ref[kj * tq:(kj + 1) * tq, :] = dvt_ref[kj].T

    return pl.pallas_call(
        body, name="attn_bwd",
        grid=(N_HEAD_BLOCKS, nq),
        in_specs=[pl.BlockSpec((1, tq, LANES), lambda p, i: (U_Q, i, p)),
                  pl.BlockSpec((1, t, LANES), lambda p, i: (U_K, 0, p)),
                  pl.BlockSpec((1, t, LANES), lambda p, i: (U_V, 0, p)),
                  pl.BlockSpec((tq, LANES), lambda p, i: (i, p)),
                  pl.BlockSpec((1, tq, LANES), lambda p, i: (p, i, 0)),
                  pl.BlockSpec((1, 1, 8, LANES), lambda p, i: (p, i, 0, 0))],
        out_specs=[pl.BlockSpec((tq, LANES), lambda p, i: (i, p)),
                   pl.BlockSpec((t, LANES), lambda p, i: (0, p)),
                   pl.BlockSpec((t, LANES), lambda p, i: (0, p))],
        out_shape=[S((t, UNIT), F32)] * 3,
        scratch_shapes=[pltpu.VMEM((nq, LANES, tq), F32), pltpu.VMEM((nq, LANES, tq), F32)],
        compiler_params=_params(2),
    )(proj, proj, proj, d_o, bsum, kfirst)


def _mix_in_specs(tb):
    hb = tb // HALO

    def unit(u):
        return pl.BlockSpec((1, tb, UNIT), lambda i: (u, i, 0))

    def halo(u):
        return pl.BlockSpec((1, HALO, UNIT), lambda i: (u, jnp.maximum(i * hb - 1, 0), 0))

    return [unit(U_P), halo(U_P), unit(U_GP), unit(U_CA), halo(U_CA), unit(U_CB), halo(U_CB), unit(U_GC),
            unit(U_GA), pl.BlockSpec((3, tb, UNIT), lambda i: (3, i, 0)), pl.BlockSpec((3, tb, UNIT), lambda i: (4, i, 0))]


def _mix_weight_specs():
    return [_const_spec((4, GROUP, GROUP)), _const_spec((1, UNIT)), _const_spec((1, UNIT)),
            _const_spec((UNIT, D_MODEL)), _const_spec((CONV_K_PAD, UNIT)), _const_spec((1, UNIT)),
            _const_spec((1, UNIT)), _const_spec((1, UNIT)), _const_spec((UNIT, D_MODEL)),
            _const_spec((UNIT, D_MODEL)), _const_spec((D_MODEL, D_MODEL)), _const_spec((1, D_MODEL))]


def _mix_forward(i, tb, proj_refs, o_ref, w_refs, pbuf, ubuf):
    p_ref, ph_ref, gp_ref, ca_ref, cah_ref, cb_ref, cbh_ref, gc_ref, ga_ref, gm0_ref, gm1_ref = proj_refs
    (poolw_ref, poolb_ref, pscale_ref, wpo_ref, convw_ref, convb_ref, lng_ref, lnb_ref,
     wco_ref, wao_ref, wo_ref, gpost_ref) = w_refs
    first = i == 0
    r = {}

    pbuf[0:HALO, :] = jnp.where(first, 0.0, ph_ref[0])
    pbuf[HALO:HALO + tb, :] = p_ref[0]
    tpos = i * tb + lax.broadcasted_iota(jnp.int32, (tb, 1), 0)
    d_parts, y_parts = [], []
    for g, win in enumerate(POOL_WINDOWS):
        cs = slice(g * GROUP, (g + 1) * GROUP)
        cur = pbuf[HALO:HALO + tb, cs]
        s = cur
        for j in range(1, win):
            s = s + pbuf[HALO - j:HALO - j + tb, cs]
        cnt = jnp.minimum(tpos + 1, win).astype(F32)
        d_g = s / cnt - cur
        d_parts.append(d_g)
        y_parts.append(_dot(d_g.astype(BF16), poolw_ref[g]))
    r["d"] = d_parts
    y = jnp.concatenate(y_parts, axis=1) + poolb_ref[...]
    r["y"] = y
    mp = y * pscale_ref[...]
    gp = gp_ref[0]
    sgp = _sigmoid(gp)
    r["mp"], r["gp"], r["sgp"] = mp, gp, sgp
    ua = mp * (gp * sgp)

    ah, bh = cah_ref[0], cbh_ref[0]
    ubuf[0:HALO, :] = jnp.where(first, 0.0, ah * _sigmoid(bh))
    a, b = ca_ref[0], cb_ref[0]
    sb = _sigmoid(b)
    ubuf[HALO:HALO + tb, :] = a * sb
    r["a"], r["sb"] = a, sb
    cv = jnp.zeros((tb, UNIT), F32) + convb_ref[...]
    off = HALO - (CONV_K - 1)
    for k in range(CONV_K):
        cv = cv + convw_ref[k:k + 1, :] * ubuf[off + k:off + k + tb, :]
    mu = jnp.mean(cv, axis=-1, keepdims=True)
    cc = cv - mu
    rs = lax.rsqrt(jnp.mean(cc * cc, axis=-1, keepdims=True) + LN_EPS)
    nrm = cc * rs
    ln = nrm * lng_ref[...] + lnb_ref[...]
    sln = _sigmoid(ln)
    sc = ln * sln
    gc = gc_ref[0]
    sgc = _sigmoid(gc)
    r["rs"], r["nrm"], r["ln"], r["sln"], r["sc"], r["gc"], r["sgc"] = rs, nrm, ln, sln, sc, gc, sgc
    ub = sc * (gc * sgc)

    o = o_ref[...]
    ga = ga_ref[0]
    sga = _sigmoid(ga)
    r["o"], r["ga"], r["sga"] = o, ga, sga
    uc = o * (ga * sga)

    r["ua"], r["ub"], r["uc"] = ua.astype(BF16), ub.astype(BF16), uc.astype(BF16)
    ya = _dot(r["ua"], wpo_ref[...])
    yb = _dot(r["ub"], wco_ref[...])
    yc = _dot(r["uc"], wao_ref[...])
    g0 = _sigmoid(jnp.concatenate([gm0_ref[0], gm0_ref[1]], axis=1))
    g1 = _sigmoid(jnp.concatenate([gm0_ref[2], gm1_ref[0]], axis=1))
    g2 = _sigmoid(jnp.concatenate([gm1_ref[1], gm1_ref[2]], axis=1))
    r["ya"], r["yb"], r["yc"], r["g0"], r["g1"], r["g2"] = ya, yb, yc, g0, g1, g2
    m = (g0 * ya + g1 * yb + g2 * yc).astype(BF16)
    r["m"] = m
    out = _dot(m, wo_ref[...])
    r2 = lax.rsqrt(jnp.mean(out * out, axis=-1, keepdims=True) + RMS_EPS)
    r["n2"], r["r2"] = out * r2, r2
    return r


def _mix_fwd(proj, o, x, weights):
    t = x.shape[0]
    tb = min(256, t)

    def body(*refs):
        proj_refs, o_ref, x_ref = refs[0:11], refs[11], refs[12]
        w_refs = refs[13:25]
        xn_ref, pbuf, ubuf = refs[25], refs[26], refs[27]
        r = _mix_forward(pl.program_id(0), tb, proj_refs, o_ref, w_refs, pbuf, ubuf)
        xn_ref[...] = x_ref[...] + r["n2"] * w_refs[11][...]

    return pl.pallas_call(
        body, name="mix_fwd",
        grid=(t // tb,),
        in_specs=_mix_in_specs(tb) + [pl.BlockSpec((tb, UNIT), lambda i: (i, 0)),
                                      pl.BlockSpec((tb, D_MODEL), lambda i: (i, 0))] + _mix_weight_specs(),
        out_specs=pl.BlockSpec((tb, D_MODEL), lambda i: (i, 0)),
        out_shape=S((t, D_MODEL), F32),
        scratch_shapes=[pltpu.VMEM((HALO + tb, UNIT), F32), pltpu.VMEM((HALO + tb, UNIT), F32)],
        compiler_params=_params(1),
    )(*([proj] * 11), o, x, *weights)


def _mix_bwd(proj, o, x, dxn, weights):
    t = x.shape[0]
    tb = min(256, t)
    nt = t // tb

    def body(*refs):
        proj_refs, o_ref, x_ref, dxn_ref = refs[0:11], refs[11], refs[12], refs[13]
        w_refs = refs[14:26]
        dg_ref, dd_ref, dcv_ref, do_ref, dwo_hbm, dwp_hbm, dwc_hbm, dwa_hbm, dpw_ref, dvec_ref = refs[26:36]
        pbuf, ubuf, dwo_ref, dwp_ref, dwc_ref, dwa_ref = refs[36:42]
        (poolw_ref, _, pscale_ref, wpo_ref, _, _, lng_ref, _, wco_ref, wao_ref, wo_ref, gpost_ref) = w_refs
        i = pl.program_id(0)

        @pl.when(i == 0)
        def _():
            for ref in (dwo_ref, dwp_ref, dwc_ref, dwa_ref, dpw_ref, dvec_ref):
                ref[...] = jnp.zeros_like(ref)

        r = _mix_forward(i, tb, proj_refs, o_ref, w_refs, pbuf, ubuf)

        def colsum(v):
            return jnp.sum(v, axis=0, keepdims=True)

        dxn = dxn_ref[...]
        n2 = r["n2"]
        dvec_ref[0:1, :] += colsum(dxn * n2)
        dn2 = dxn * gpost_ref[...]
        dout = (r["r2"] * (dn2 - n2 * jnp.mean(dn2 * n2, axis=-1, keepdims=True))).astype(BF16)
        dm = _dot_nt(dout, wo_ref[...])
        dwo_ref[...] += _dot_tn(r["m"], dout)

        g0, g1, g2 = r["g0"], r["g1"], r["g2"]
        dgm = [dm * r["ya"] * g0 * (1.0 - g0), dm * r["yb"] * g1 * (1.0 - g1), dm * r["yc"] * g2 * (1.0 - g2)]
        for bidx in range(N_BRANCH):
            dg_ref[3 + 2 * bidx] = dgm[bidx][:, 0:UNIT]
            dg_ref[4 + 2 * bidx] = dgm[bidx][:, UNIT:2 * UNIT]
        dya = (dm * g0).astype(BF16)
        dyb = (dm * g1).astype(BF16)
        dyc = (dm * g2).astype(BF16)
        dua = _dot_nt(dya, wpo_ref[...])
        dub = _dot_nt(dyb, wco_ref[...])
        duc = _dot_nt(dyc, wao_ref[...])
        dwp_ref[...] += _dot_tn(r["ua"], dya)
        dwc_ref[...] += _dot_tn(r["ub"], dyb)
        dwa_ref[...] += _dot_tn(r["uc"], dyc)

        gp, sgp = r["gp"], r["sgp"]
        dmp = dua * (gp * sgp)
        dg_ref[0] = dua * r["mp"] * _dsilu(gp, sgp)
        dvec_ref[2:3, 0:UNIT] += colsum(dmp * r["y"])
        dy = dmp * pscale_ref[...]
        dvec_ref[1:2, 0:UNIT] += colsum(dy)
        dd_parts = []
        for g in range(len(POOL_WINDOWS)):
            dy_g = dy[:, g * GROUP:(g + 1) * GROUP].astype(BF16)
            dd_parts.append(_dot_nt(dy_g, poolw_ref[g]))
            dpw_ref[g] += _dot_tn(r["d"][g].astype(BF16), dy_g)
        dd_ref[...] = jnp.concatenate(dd_parts, axis=1)

        gc, sgc = r["gc"], r["sgc"]
        dsc = dub * (gc * sgc)
        dg_ref[1] = dub * r["sc"] * _dsilu(gc, sgc)
        dln = dsc * _dsilu(r["ln"], r["sln"])
        nrm = r["nrm"]
        dvec_ref[4:5, 0:UNIT] += colsum(dln * nrm)
        dvec_ref[5:6, 0:UNIT] += colsum(dln)
        dnrm = dln * lng_ref[...]
        dcv = r["rs"] * (dnrm - jnp.mean(dnrm, axis=-1, keepdims=True)
                         - nrm * jnp.mean(dnrm * nrm, axis=-1, keepdims=True))
        dvec_ref[3:4, 0:UNIT] += colsum(dcv)
        dcv_ref[...] = dcv

        ga, sga = r["ga"], r["sga"]
        do_ref[...] = duc * (ga * sga)
        dg_ref[2] = duc * r["o"] * _dsilu(ga, sga)

        @pl.when(i == nt - 1)
        def _():
            for acc, hbm in ((dwo_ref, dwo_hbm), (dwp_ref, dwp_hbm), (dwc_ref, dwc_hbm), (dwa_ref, dwa_hbm)):
                pltpu.sync_copy(acc, hbm)

    def acc_spec(shape):
        n = len(shape)
        return pl.BlockSpec(shape, lambda i: (0,) * n)

    tok = lambda w: pl.BlockSpec((tb, w), lambda i: (i, 0))
    any_spec = pl.BlockSpec(memory_space=pl.ANY)
    w_shapes = [(D_MODEL, D_MODEL), (UNIT, D_MODEL), (UNIT, D_MODEL), (UNIT, D_MODEL)]
    return pl.pallas_call(
        body, name="mix_bwd",
        grid=(nt,),
        in_specs=_mix_in_specs(tb) + [tok(UNIT), tok(D_MODEL), tok(D_MODEL)] + _mix_weight_specs(),
        out_specs=[pl.BlockSpec((9, tb, UNIT), lambda i: (0, i, 0)), tok(UNIT), tok(UNIT), tok(UNIT)]
        + [any_spec] * 4 + [acc_spec((4, GROUP, GROUP)), acc_spec((8, D_MODEL))],
        out_shape=[S((9, t, UNIT), F32), S((t, UNIT), F32), S((t, UNIT), F32), S((t, UNIT), F32)]
        + [S(s, F32) for s in w_shapes] + [S((4, GROUP, GROUP), F32), S((8, D_MODEL), F32)],
        scratch_shapes=[pltpu.VMEM((HALO + tb, UNIT), F32), pltpu.VMEM((HALO + tb, UNIT), F32)]
        + [pltpu.VMEM(s, F32) for s in w_shapes],
        compiler_params=_params(1),
    )(*([proj] * 11), o, x, dxn, *weights)


def _halo_bwd(proj, dcv, dd, conv_w):
    t = dcv.shape[0]
    tb = min(256, t)
    hb = tb // HALO
    n_halo_blocks = t // HALO
    nt = t // tb

    def body(ca_ref, cah_ref, cb_ref, cbh_ref, dcv_ref, dcvh_ref, dd_ref, ddh_ref, cw_ref,
             dpre_ref, dcw_ref, ubuf, gbuf, nbuf):
        i = pl.program_id(0)
        first = i == 0
        last = i == nt - 1

        @pl.when(first)
        def _():
            dcw_ref[...] = jnp.zeros_like(dcw_ref)

        a, b = ca_ref[0], cb_ref[0]
        sb = _sigmoid(b)
        ubuf[0:HALO, :] = jnp.where(first, 0.0, cah_ref[0] * _sigmoid(cbh_ref[0]))
        ubuf[HALO:HALO + tb, :] = a * sb
        dcv_main = dcv_ref[...]
        gbuf[0:tb, :] = dcv_main
        gbuf[tb:tb + HALO, :] = jnp.where(last, 0.0, dcvh_ref[...])

        off = HALO - (CONV_K - 1)
        du = jnp.zeros((tb, UNIT), F32)
        for k in range(CONV_K):
            du = du + cw_ref[k:k + 1, :] * gbuf[CONV_K - 1 - k:CONV_K - 1 - k + tb, :]
            dcw_ref[k:k + 1, :] += jnp.sum(dcv_main * ubuf[off + k:off + k + tb, :], axis=0, keepdims=True)
        dpre_ref[1] = du * sb
        dpre_ref[2] = du * a * sb * (1.0 - sb)

        tpos = i * tb + lax.broadcasted_iota(jnp.int32, (tb + HALO, 1), 0)
        dd_main = dd_ref[...]
        dd_ext = jnp.concatenate([dd_main, jnp.where(last, 0.0, ddh_ref[...])], axis=0)
        dp_parts = []
        for g, win in enumerate(POOL_WINDOWS):
            cs = slice(g * GROUP, (g + 1) * GROUP)
            cnt = jnp.minimum(tpos + 1, win).astype(F32)
            nbuf[:, cs] = dd_ext[:, cs] / cnt
        for g, win in enumerate(POOL_WINDOWS):
            cs = slice(g * GROUP, (g + 1) * GROUP)
            s = nbuf[0:tb, cs]
            for j in range(1, win):
                s = s + nbuf[j:j + tb, cs]
            dp_parts.append(s - dd_main[:, cs])
        dpre_ref[0] = jnp.concatenate(dp_parts, axis=1)

    def unit(u):
        return pl.BlockSpec((1, tb, UNIT), lambda i: (u, i, 0))

    def past(u):
        return pl.BlockSpec((1, HALO, UNIT), lambda i: (u, jnp.maximum(i * hb - 1, 0), 0))

    tok = pl.BlockSpec((tb, UNIT), lambda i: (i, 0))
    future = pl.BlockSpec((HALO, UNIT), lambda i: (jnp.minimum((i + 1) * hb, n_halo_blocks - 1), 0))
    return pl.pallas_call(
        body, name="halo_bwd",
        grid=(nt,),
        in_specs=[unit(U_CA), past(U_CA), unit(U_CB), past(U_CB), tok, future, tok, future,
                  _const_spec((CONV_K_PAD, UNIT))],
        out_specs=[pl.BlockSpec((3, tb, UNIT), lambda i: (0, i, 0)),
                   pl.BlockSpec((CONV_K_PAD, UNIT), lambda i: (0, 0))],
        out_shape=[S((3, t, UNIT), F32), S((CONV_K_PAD, UNIT), F32)],
        scratch_shapes=[pltpu.VMEM((HALO + tb, UNIT), F32), pltpu.VMEM((HALO + tb, UNIT), F32),
                        pltpu.VMEM((HALO + tb, UNIT), F32)],
        compiler_params=_params(1),
    )(proj, proj, proj, proj, dcv, dcv, dd, dd, conv_w)


def _loss_head(y, target):
    t = y.shape[0]
    tb = min(512, t)

    def body(y_ref, t_ref, loss_ref, dy_ref):
        @pl.when(pl.program_id(0) == 0)
        def _():
            loss_ref[...] = jnp.zeros_like(loss_ref)

        err = y_ref[...] - t_ref[...]
        dy_ref[...] = err * (1.0 / D_MODEL)
        part = 0.5 * jnp.sum(jnp.mean(err * err, axis=-1, keepdims=True), axis=0, keepdims=True)
        r8 = lax.broadcasted_iota(jnp.int32, (8, LANES), 0)
        c8 = lax.broadcasted_iota(jnp.int32, (8, LANES), 1)
        loss_ref[...] += jnp.where((r8 == 0) & (c8 == 0), part, 0.0)

    return pl.pallas_call(
        body, name="loss_head",
        grid=(t // tb,),
        in_specs=[pl.BlockSpec((tb, D_MODEL), lambda i: (i, 0))] * 2,
        out_specs=[pl.BlockSpec((8, LANES), lambda i: (0, 0)), pl.BlockSpec((tb, D_MODEL), lambda i: (i, 0))],
        out_shape=[S((8, LANES), F32), S((t, D_MODEL), F32)],
        compiler_params=_params(1),
    )(y, target)


def _adamw(name, parts, w, m, v):
    n, rows, cols = parts.shape
    rb = rows
    while rb * cols * 4 * (n + 7) * 2 > 24 * 1024 * 1024 and rb % 16 == 0:
        rb //= 2

    def body(p_ref, w_ref, m_ref, v_ref, g_ref, d_ref, nm_ref, nv_ref):
        g = p_ref[0]
        for k in range(1, n):
            g = g + p_ref[k]
        nm = ADAM_B1 * m_ref[...] + (1.0 - ADAM_B1) * g
        nv = ADAM_B2 * v_ref[...] + (1.0 - ADAM_B2) * (g * g)
        m_hat = nm / (1.0 - ADAM_B1 ** ADAM_STEP)
        v_hat = nv / (1.0 - ADAM_B2 ** ADAM_STEP)
        g_ref[...] = g
        d_ref[...] = -ADAM_LR * (m_hat / (jnp.sqrt(v_hat) + ADAM_EPS) + ADAM_WD * w_ref[...])
        nm_ref[...] = nm
        nv_ref[...] = nv

    blk = pl.BlockSpec((rb, cols), lambda i: (i, 0))
    return pl.pallas_call(
        body, name=name,
        grid=(rows // rb,),
        in_specs=[pl.BlockSpec((n, rb, cols), lambda i: (0, i, 0)), blk, blk, blk],
        out_specs=[blk] * 4,
        out_shape=[S((rows, cols), F32)] * 4,
        compiler_params=_params(1),
    )(parts, w, m, v)


def _axis_slice(ref, axis, block, size):
    idx = [slice(None)] * len(ref.shape)
    idx[axis] = pl.ds(pl.multiple_of(block * size, size), size)
    return ref.at[tuple(idx)]


def _axis_index(ref, axis, index):
    return ref.at[tuple([slice(None)] * axis + [index])]


COPY_PIECE_BYTES = 2 * 1024 * 1024


def _row_pieces(shape):
    rows = shape[-2]
    size = 4 * rows * shape[-1]
    for d in shape[:-2]:
        size *= d
    n = 1
    while size // n > COPY_PIECE_BYTES and rows % (2 * n * 8) == 0:
        n *= 2
    return [(k * (rows // n), rows // n) for k in range(n)]


def _rows(ref, r0, rows):
    idx = [slice(None)] * len(ref.shape)
    idx[-2] = pl.ds(r0, rows)
    return ref.at[tuple(idx)]


def _run_copies(local, sends, recvs):
    for cp in local + sends:
        cp.start()
    for cp in recvs:
        cp.wait_recv()
    for cp in sends:
        cp.wait_send()
    for cp in local:
        cp.wait()


def _gather_weights(shards, axes):
    n = len(shards)
    fulls = [S(tuple(4 * d if k == ax else d for k, d in enumerate(s.shape)), s.dtype) for s, ax in zip(shards, axes)]

    def body(*refs):
        src, dst = refs[0:n], refs[n:2 * n]
        send_sems, recv_sems, pass_send_sems, pass_recv_sems, local_sems = refs[2 * n:2 * n + 5]
        x, y, c = lax.axis_index("x"), lax.axis_index("y"), lax.axis_index("c")
        chips = [(1 - x, y), (x, 1 - y), (1 - x, 1 - y)]

        def place(a, chip, layer=None):
            ref = _axis_slice(dst[a], axes[a], 2 * chip[0] + chip[1], src[a].shape[axes[a]])
            return ref if layer is None else ref.at[pl.ds(layer, 1)]

        def over_ici(a, j, chip_of_block, to):
            return pltpu.make_async_remote_copy(
                src_ref=src[a].at[pl.ds(c, 1)], dst_ref=place(a, chip_of_block, c), send_sem=send_sems.at[a, j],
                recv_sem=recv_sems.at[a, j], device_id=(to[0], to[1], c), device_id_type=MESH)

        def to_sibling(k, chip_of_block, layer):
            a, j, r0, rows = pieces[k]
            block = _rows(place(a, chip_of_block, layer), r0, rows)
            return pltpu.make_async_remote_copy(
                src_ref=block, dst_ref=block, send_sem=pass_send_sems.at[k], recv_sem=pass_recv_sems.at[k],
                device_id=(x, y, 1 - c), device_id_type=MESH)

        local = [pltpu.make_async_copy(src[a], place(a, (x, y)), local_sems.at[a]) for a in range(n)]
        first = [over_ici(a, j, (x, y), chip) for j, chip in enumerate(chips) for a in range(n)]
        for cp in local + first:
            cp.start()
        passed = []
        for k, (a, j, r0, rows) in enumerate(pieces):
            if r0 == 0:
                over_ici(a, j, chips[j], chips[j]).wait_recv()
            passed.append(to_sibling(k, chips[j], c))
            passed[-1].start()
        for k, (a, j, r0, rows) in enumerate(pieces):
            to_sibling(k, chips[j], 1 - c).wait_recv()
        for cp in first + passed:
            cp.wait_send()
        for cp in local:
            cp.wait()

    pieces = [(a, j, r0, rows) for j in range(3) for a in range(n)
              for r0, rows in _row_pieces((1,) + shards[a].shape[1:])]
    any_spec = pl.BlockSpec(memory_space=pl.ANY)
    return pl.pallas_call(
        body, name="gather_weights",
        in_specs=[any_spec] * n, out_specs=[any_spec] * n, out_shape=fulls,
        scratch_shapes=[pltpu.SemaphoreType.DMA((n, 3)), pltpu.SemaphoreType.DMA((n, 3)),
                        pltpu.SemaphoreType.DMA((len(pieces),)), pltpu.SemaphoreType.DMA((len(pieces),)),
                        pltpu.SemaphoreType.DMA((n,))],
    )(*shards)


def _swap_halves(partials, half_axes):
    n = len(partials)

    def half_shape(a):
        return tuple(d // 2 if k == half_axes[a] else d for k, d in enumerate(partials[a].shape))

    pieces = [(a, r0, rows) for a in range(n) for r0, rows in _row_pieces(half_shape(a))]

    def body(*refs):
        src, dst = refs[0:n], refs[n:2 * n]
        send_sems, recv_sems, local_sems = refs[2 * n:2 * n + 3]
        x, y, c = lax.axis_index("x"), lax.axis_index("y"), lax.axis_index("c")

        def half(a, h):
            return _axis_slice(src[a], half_axes[a], h, src[a].shape[half_axes[a]] // 2)

        def remote(k, h):
            a, r0, rows = pieces[k]
            return pltpu.make_async_remote_copy(
                src_ref=_rows(half(a, h), r0, rows), dst_ref=_rows(dst[a].at[1], r0, rows),
                send_sem=send_sems.at[k], recv_sem=recv_sems.at[k], device_id=(x, y, 1 - c), device_id_type=MESH)

        local = [pltpu.make_async_copy(half(a, c), dst[a].at[0], local_sems.at[a]) for a in range(n)]
        sends = [remote(k, 1 - c) for k in range(len(pieces))]
        _run_copies(local, sends, [remote(k, c) for k in range(len(pieces))])

    any_spec = pl.BlockSpec(memory_space=pl.ANY)
    return pl.pallas_call(
        body, name="swap_halves",
        in_specs=[any_spec] * n, out_specs=[any_spec] * n,
        out_shape=[S((2,) + half_shape(a), partials[a].dtype) for a in range(n)],
        scratch_shapes=[pltpu.SemaphoreType.DMA((len(pieces),)), pltpu.SemaphoreType.DMA((len(pieces),)),
                        pltpu.SemaphoreType.DMA((n,))],
    )(*partials)


def _send_to_owners(chip_halves, shard_axes, packed):
    n = len(chip_halves)

    def owned_shape(a):
        return tuple(d // 4 if k == shard_axes[a] else d for k, d in enumerate(chip_halves[a].shape))

    flips = [(dx, dy, dc) for dx in (0, 1) for dy in (0, 1) for dc in (0, 1)][1:]

    def body(*refs):
        src, psrc, dst, pdst = refs[0:n], refs[n], refs[n + 1:2 * n + 1], refs[2 * n + 1]
        send_sems, recv_sems, local_sems, psend_sems, precv_sems = refs[2 * n + 2:2 * n + 7]
        x, y, c = lax.axis_index("x"), lax.axis_index("y"), lax.axis_index("c")
        my_chip = 2 * x + y
        chips = [(1 - x, y), (x, 1 - y), (1 - x, 1 - y)]

        def owned(a, chip):
            return _axis_slice(src[a], shard_axes[a], 2 * chip[0] + chip[1], src[a].shape[shard_axes[a]] // 4)

        def remote(a, j, to, from_chip):
            return pltpu.make_async_remote_copy(
                src_ref=owned(a, to), dst_ref=dst[a].at[from_chip], send_sem=send_sems.at[a, j],
                recv_sem=recv_sems.at[a, j], device_id=(to[0], to[1], c), device_id_type=MESH)

        def peer(f):
            return (1 - x if f[0] else x, 1 - y if f[1] else y, 1 - c if f[2] else c)

        def premote(j, to, from_dev):
            return pltpu.make_async_remote_copy(
                src_ref=psrc, dst_ref=pdst.at[from_dev], send_sem=psend_sems.at[j], recv_sem=precv_sems.at[j],
                device_id=to, device_id_type=MESH)

        local = [pltpu.make_async_copy(owned(a, (x, y)), dst[a].at[my_chip], local_sems.at[a]) for a in range(n)]
        local.append(pltpu.make_async_copy(psrc, pdst.at[4 * x + 2 * y + c], local_sems.at[n]))
        sends = [remote(a, j, chip, my_chip) for a in range(n) for j, chip in enumerate(chips)]
        sends += [premote(j, peer(f), 4 * x + 2 * y + c) for j, f in enumerate(flips)]
        recvs = [remote(a, j, (x, y), 2 * chip[0] + chip[1]) for a in range(n) for j, chip in enumerate(chips)]
        for j, f in enumerate(flips):
            px, py, pc = peer(f)
            recvs.append(premote(j, (x, y, c), 4 * px + 2 * py + pc))
        _run_copies(local, sends, recvs)

    any_spec = pl.BlockSpec(memory_space=pl.ANY)
    return pl.pallas_call(
        body, name="send_to_owners",
        in_specs=[any_spec] * (n + 1), out_specs=[any_spec] * (n + 1),
        out_shape=[S((4,) + owned_shape(a), chip_halves[a].dtype) for a in range(n)]
        + [S((N_DEV,) + packed.shape, packed.dtype)],
        scratch_shapes=[pltpu.SemaphoreType.DMA((n, 3)), pltpu.SemaphoreType.DMA((n, 3)),
                        pltpu.SemaphoreType.DMA((n + 1,)), pltpu.SemaphoreType.DMA((7,)),
                        pltpu.SemaphoreType.DMA((7,))],
    )(*chip_halves, packed)


def _join_halves(reduced, half_axes):
    n = len(reduced)

    def joined_shape(a):
        s, ax = reduced[a].shape, half_axes[a]
        return s[:ax] + (2,) + s[ax:]

    pieces = [(a, r0, rows) for a in range(n) for r0, rows in _row_pieces(reduced[a].shape)]

    def body(*refs):
        src, dst = refs[0:n], refs[n:2 * n]
        send_sems, recv_sems, local_sems = refs[2 * n:2 * n + 3]
        x, y, c = lax.axis_index("x"), lax.axis_index("y"), lax.axis_index("c")

        def remote(k, h):
            a, r0, rows = pieces[k]
            return pltpu.make_async_remote_copy(
                src_ref=_rows(src[a], r0, rows), dst_ref=_rows(_axis_index(dst[a], half_axes[a], h), r0, rows),
                send_sem=send_sems.at[k], recv_sem=recv_sems.at[k], device_id=(x, y, 1 - c), device_id_type=MESH)

        local = [pltpu.make_async_copy(src[a], _axis_index(dst[a], half_axes[a], c), local_sems.at[a])
                 for a in range(n)]
        _run_copies(local, [remote(k, c) for k in range(len(pieces))], [remote(k, 1 - c) for k in range(len(pieces))])

    any_spec = pl.BlockSpec(memory_space=pl.ANY)
    return pl.pallas_call(
        body, name="join_halves",
        in_specs=[any_spec] * n, out_specs=[any_spec] * n,
        out_shape=[S(joined_shape(a), reduced[a].dtype) for a in range(n)],
        scratch_shapes=[pltpu.SemaphoreType.DMA((len(pieces),)), pltpu.SemaphoreType.DMA((len(pieces),)),
                        pltpu.SemaphoreType.DMA((n,))],
    )(*reduced)


def _sum_slots(name, parts):
    n = parts.shape[0]
    shape = parts.shape[1:]
    flat = parts.reshape((n, -1, shape[-1]))
    rows, cols = flat.shape[1:]
    rb = rows
    while rb * cols * 4 * (n + 1) * 2 > 24 * 1024 * 1024 and rb % 16 == 0:
        rb //= 2

    def body(p_ref, o_ref):
        acc = p_ref[0]
        for k in range(1, n):
            acc = acc + p_ref[k]
        o_ref[...] = acc

    out = pl.pallas_call(
        body, name=name,
        grid=(rows // rb,),
        in_specs=[pl.BlockSpec((n, rb, cols), lambda i: (0, i, 0))],
        out_specs=pl.BlockSpec((rb, cols), lambda i: (i, 0)),
        out_shape=S((rows, cols), F32),
        compiler_params=_params(1),
    )(flat)
    return out.reshape(shape)


SMALL_NAMES = ("norm_pre", "pool_w", "pool_b", "pool_scale", "conv_b", "conv_ln_g", "conv_ln_b", "norm_post")
BIG_NAMES = ("w_in", "w_pool_out", "conv_w", "w_conv_out", "w_attn_out", "w_o")
SHARD_AXIS = {"w_in": 2, "w_pool_out": 2, "conv_w": 2, "w_conv_out": 2, "w_attn_out": 2, "w_o": 1}
WEIGHT_ORDER = ("norm_pre", "w_in", "pool_w", "pool_b", "pool_scale", "w_pool_out", "conv_w", "conv_b",
                "conv_ln_g", "conv_ln_b", "w_conv_out", "w_attn_out", "w_o", "norm_post")


def _pack_small(parts):
    return jnp.concatenate([parts[n].reshape(-1, LANES) for n in SMALL_NAMES], axis=0)


def _unpack_small(packed, shapes):
    out, r0 = {}, 0
    for n in SMALL_NAMES:
        size = 1
        for d in shapes[n]:
            size *= d
        rows = size // LANES
        out[n] = packed[r0:r0 + rows].reshape(shapes[n])
        r0 += rows
    return out


def _layer_weights(full, small, l):
    row = lambda a: a[l][None, :]
    return (small["pool_w"][l].astype(BF16), small["pool_b"][l].reshape(1, UNIT), row(small["pool_scale"]),
            full["w_pool_out"][l], full["conv_w"][l], row(small["conv_b"]), row(small["conv_ln_g"]),
            row(small["conv_ln_b"]), full["w_conv_out"][l], full["w_attn_out"][l], full["w_o"][l],
            row(small["norm_post"]))


def _forward_backward(x, target, full, small):
    depth = full["w_in"].shape[0]
    acts = []
    for l in range(depth):
        g_pre = small["norm_pre"][l][None, :]
        proj = _inproj_fwd(x, g_pre, full["w_in"][l])
        o, bsum, kfirst = _attn_fwd(proj)
        weights = _layer_weights(full, small, l)
        x_next = _mix_fwd(proj, o, x, weights)
        acts.append((x, proj, o, bsum, kfirst, weights, g_pre))
        x = x_next
    loss_tile, dx = _loss_head(x, target)

    big = {n: [None] * depth for n in BIG_NAMES}
    sm = {n: [None] * depth for n in SMALL_NAMES}
    for l in reversed(range(depth)):
        x_in, proj, o, bsum, kfirst, weights, g_pre = acts[l]
        dgates, dd, dcv, d_o, dwo, dwp, dwc, dwa, dpw, dvec = _mix_bwd(proj, o, x_in, dx, weights)
        dpre, dcw = _halo_bwd(proj, dcv, dd, full["conv_w"][l])
        dq, dk, dv = _attn_bwd(proj, d_o, bsum, kfirst)
        dproj_parts = (dpre, dgates, dq, dk, dv)
        dx, dg_pre = _inproj_bwd_x(x_in, g_pre, full["w_in"][l], dproj_parts, dx)
        big["w_in"][l] = _inproj_bwd_w(x_in, g_pre, dproj_parts)
        big["w_pool_out"][l], big["w_conv_out"][l], big["w_attn_out"][l], big["w_o"][l] = dwp, dwc, dwa, dwo
        big["conv_w"][l] = dcw
        sm["norm_pre"][l] = dg_pre[0]
        sm["pool_w"][l] = dpw
        sm["norm_post"][l] = dvec[0]
        sm["pool_b"][l] = dvec[1, 0:UNIT].reshape(4, GROUP)
        sm["pool_scale"][l] = dvec[2, 0:UNIT]
        sm["conv_b"][l] = dvec[3, 0:UNIT]
        sm["conv_ln_g"][l] = dvec[4, 0:UNIT]
        sm["conv_ln_b"][l] = dvec[5, 0:UNIT]
    big = {n: jnp.stack(v) for n, v in big.items()}
    sm = {n: jnp.stack(v) for n, v in sm.items()}
    return loss_tile, dx, big, sm


def kernel(x, norm_pre, w_in, pool_w, pool_b, pool_scale, w_pool_out, conv_w, conv_b, conv_ln_g, conv_ln_b, w_conv_out, w_attn_out, w_o, norm_post, loss_target, m_norm_pre, m_w_in, m_pool_w, m_pool_b, m_pool_scale, m_w_pool_out, m_conv_w, m_conv_b, m_conv_ln_g, m_conv_ln_b, m_w_conv_out, m_w_attn_out, m_w_o, m_norm_post, v_norm_pre, v_w_in, v_pool_w, v_pool_b, v_pool_scale, v_w_pool_out, v_conv_w, v_conv_b, v_conv_ln_g, v_conv_ln_b, v_w_conv_out, v_w_attn_out, v_w_o, v_norm_post):
    w = dict(norm_pre=norm_pre, w_in=w_in, pool_w=pool_w, pool_b=pool_b, pool_scale=pool_scale,
             w_pool_out=w_pool_out, conv_w=conv_w, conv_b=conv_b, conv_ln_g=conv_ln_g, conv_ln_b=conv_ln_b,
             w_conv_out=w_conv_out, w_attn_out=w_attn_out, w_o=w_o, norm_post=norm_post)
    m = dict(norm_pre=m_norm_pre, w_in=m_w_in, pool_w=m_pool_w, pool_b=m_pool_b, pool_scale=m_pool_scale,
             w_pool_out=m_w_pool_out, conv_w=m_conv_w, conv_b=m_conv_b, conv_ln_g=m_conv_ln_g,
             conv_ln_b=m_conv_ln_b, w_conv_out=m_w_conv_out, w_attn_out=m_w_attn_out, w_o=m_w_o,
             norm_post=m_norm_post)
    v = dict(norm_pre=v_norm_pre, w_in=v_w_in, pool_w=v_pool_w, pool_b=v_pool_b, pool_scale=v_pool_scale,
             w_pool_out=v_w_pool_out, conv_w=v_conv_w, conv_b=v_conv_b, conv_ln_g=v_conv_ln_g,
             conv_ln_b=v_conv_ln_b, w_conv_out=v_w_conv_out, w_attn_out=v_w_attn_out, w_o=v_w_o,
             norm_post=v_norm_post)
    pad_taps = lambda a: jnp.pad(a, ((0, 0), (0, CONV_K_PAD - CONV_K), (0, 0)))

    shards = [pad_taps(w[n]) if n == "conv_w" else w[n].astype(BF16) for n in BIG_NAMES]
    full = dict(zip(BIG_NAMES, _gather_weights(shards, [SHARD_AXIS[n] for n in BIG_NAMES])))
    small = {n: w[n] for n in SMALL_NAMES}

    loss_tile, dx, big, sm = _forward_backward(x[0], loss_target[0], full, small)

    packed = jnp.concatenate([_pack_small(sm), loss_tile], axis=0)
    layer_axis = [0] * len(BIG_NAMES)
    pairs = _swap_halves([big[n] for n in BIG_NAMES], layer_axis)
    chip_sums = [_sum_slots("sum_cores_" + n, p) for n, p in zip(BIG_NAMES, pairs)]
    exchanged = _send_to_owners(chip_sums, [SHARD_AXIS[n] for n in BIG_NAMES], packed)
    reduced = [_sum_slots("sum_chips_" + n, p) for n, p in zip(BIG_NAMES, exchanged[:-1])]
    joined = _join_halves(reduced, layer_axis)
    out = {}
    for n, g in zip(BIG_NAMES, joined):
        wn, mn, vn = (pad_taps(a[n]) if n == "conv_w" else a[n] for a in (w, m, v))
        shape = wn.shape
        flat = lambda a: a.reshape((-1, shape[-1]))
        res = _adamw("adamw_" + n, g.reshape((1, -1, shape[-1])), flat(wn), flat(mn), flat(vn))
        res = [r.reshape(shape) for r in res]
        out[n] = [r[:, :CONV_K] for r in res] if n == "conv_w" else res
    parts = exchanged[-1]
    n_small_rows = parts.shape[1] - 8
    zeros_tile = jnp.zeros((8, LANES), F32)
    packs = [jnp.concatenate([_pack_small(a), zeros_tile], axis=0) for a in (w, m, v)]
    res = _adamw("adamw_small", parts, *packs)
    loss = res[0][n_small_rows, 0]
    shapes = {n: w[n].shape for n in SMALL_NAMES}
    unpacked = [_unpack_small(r[:n_small_rows], shapes) for r in res]
    for n in SMALL_NAMES:
        out[n] = [u[n] for u in unpacked]

    grads = [out[n][0] for n in WEIGHT_ORDER]
    deltas = [out[n][1] for n in WEIGHT_ORDER]
    new_m = [out[n][2] for n in WEIGHT_ORDER]
    new_v = [out[n][3] for n in WEIGHT_ORDER]
    return (loss, dx[None], *grads, *deltas, *new_m, *new_v)
```

```python
import jax
import jax.numpy as jnp
from jax import lax
from jax.experimental import pallas as pl
from jax.experimental.pallas import tpu as pltpu

F32 = jnp.float32
BF16 = jnp.bfloat16

D_MODEL = 1024
UNIT = 512
N_UNITS = 15
IN_WIDTH = UNIT * N_UNITS
N_HEADS = 8
HEAD_DIM = 64
HEADS_PER_BLOCK = 2
N_HEAD_BLOCKS = N_HEADS // HEADS_PER_BLOCK
LANES = 128
CONV_K = 31
CONV_K_PAD = 32
HALO = 32
POOL_WINDOWS = (2, 4, 8, 16)
GROUP = 128
N_BRANCH = 3
RMS_EPS = 1e-6
LN_EPS = 1e-5
ATTN_SCALE = 0.125
EXP_ZERO_BELOW = -104.0

ADAM_LR = 0.001
ADAM_B1 = 0.9
ADAM_B2 = 0.999
ADAM_EPS = 1e-08
ADAM_WD = 0.01
ADAM_STEP = 10

U_P, U_GP, U_CA, U_CB, U_GC, U_Q, U_K, U_V, U_GA, U_GM = 0, 1, 2, 3, 4, 5, 6, 7, 8, 9

V7X_VMEM_LIMIT = 62 * 1024 * 1024
N_DEV = 8
MESH = pl.DeviceIdType.MESH

S = jax.ShapeDtypeStruct


def _params(n_grid):
    return pltpu.CompilerParams(dimension_semantics=("arbitrary",) * n_grid, vmem_limit_bytes=V7X_VMEM_LIMIT)


def _sigmoid(x):
    return 1.0 / (1.0 + jnp.exp(-x))


def _dsilu(x, s):
    return s * (1.0 + x * (1.0 - s))


def _dot(a, b):
    return jnp.dot(a, b, preferred_element_type=F32)


def _dot_nt(a, b):
    return lax.dot_general(a, b, (((1,), (1,)), ((), ())), preferred_element_type=F32)


def _dot_tn(a, b):
    return lax.dot_general(a, b, (((0,), (0,)), ((), ())), preferred_element_type=F32)


def _split_bf16(x):
    hi = x.astype(BF16)
    lo = (x - hi.astype(F32)).astype(BF16)
    return hi, lo


def _const_spec(shape):
    n = len(shape)
    return pl.BlockSpec(shape, lambda *_: (0,) * n, pipeline_mode=pl.Buffered(1))


def _inproj_fwd(x, g_pre, w_in):
    t = x.shape[0]
    tb = min(1024, t)
    ug = 3
    nb = ug * UNIT

    def body(x_ref, g_ref, w_ref, o_ref, h_ref):
        @pl.when(pl.program_id(1) == 0)
        def _():
            xf = x_ref[...]
            r = lax.rsqrt(jnp.mean(xf * xf, axis=-1, keepdims=True) + RMS_EPS)
            h_ref[...] = (xf * r * g_ref[...]).astype(BF16)

        acc = _dot(h_ref[...], w_ref[...])
        for u in range(ug):
            o_ref[u] = acc[:, u * UNIT:(u + 1) * UNIT]

    return pl.pallas_call(
        body, name="inproj_fwd",
        grid=(t // tb, N_UNITS // ug),
        in_specs=[pl.BlockSpec((tb, D_MODEL), lambda i, j: (i, 0)),
                  pl.BlockSpec((1, D_MODEL), lambda i, j: (0, 0)),
                  pl.BlockSpec((D_MODEL, nb), lambda i, j: (0, j))],
        out_specs=pl.BlockSpec((ug, tb, UNIT), lambda i, j: (j, i, 0)),
        out_shape=S((N_UNITS, t, UNIT), F32),
        scratch_shapes=[pltpu.VMEM((tb, D_MODEL), BF16)],
        compiler_params=_params(2),
    )(x, g_pre, w_in)


def _dproj_specs(tb):
    tok = pl.BlockSpec((tb, UNIT), lambda i: (i, 0))
    return [pl.BlockSpec((3, tb, UNIT), lambda i: (0, i, 0)), pl.BlockSpec((9, tb, UNIT), lambda i: (0, i, 0)),
            tok, tok, tok]


def _dproj_unit(u, dpre_ref, dgates_ref, dq_ref, dk_ref, dv_ref):
    pre = {U_P: 0, U_CA: 1, U_CB: 2}
    gate = {U_GP: 0, U_GC: 1, U_GA: 2}
    if u in pre:
        val = dpre_ref[pre[u]]
    elif u in gate:
        val = dgates_ref[gate[u]]
    elif u >= U_GM:
        val = dgates_ref[3 + u - U_GM]
    else:
        val = {U_Q: dq_ref, U_K: dk_ref, U_V: dv_ref}[u][...]
    return val.astype(BF16)


def _inproj_bwd_x(x, g_pre, w_in, dproj_parts, dxn):
    t = x.shape[0]
    tb = min(256, t)

    def body(x_ref, g_ref, w_ref, dpre_ref, dgates_ref, dq_ref, dk_ref, dv_ref, dxn_ref, dx_ref, dg_ref):
        @pl.when(pl.program_id(0) == 0)
        def _():
            dg_ref[...] = jnp.zeros_like(dg_ref)

        dh = jnp.zeros((tb, D_MODEL), F32)
        for u in range(N_UNITS):
            dh = dh + _dot_nt(_dproj_unit(u, dpre_ref, dgates_ref, dq_ref, dk_ref, dv_ref),
                              w_ref[:, u * UNIT:(u + 1) * UNIT])
        xf = x_ref[...]
        r = lax.rsqrt(jnp.mean(xf * xf, axis=-1, keepdims=True) + RMS_EPS)
        xhat = xf * r
        dg_ref[...] += jnp.sum(dh * xhat, axis=0, keepdims=True)
        dxhat = dh * g_ref[...]
        dx_ref[...] = dxn_ref[...] + r * (dxhat - xhat * jnp.mean(dxhat * xhat, axis=-1, keepdims=True))

    tokd = pl.BlockSpec((tb, D_MODEL), lambda i: (i, 0))
    return pl.pallas_call(
        body, name="inproj_bwd_x",
        grid=(t // tb,),
        in_specs=[tokd, _const_spec((1, D_MODEL)), _const_spec((D_MODEL, IN_WIDTH))] + _dproj_specs(tb) + [tokd],
        out_specs=[tokd, pl.BlockSpec((1, D_MODEL), lambda i: (0, 0))],
        out_shape=[S((t, D_MODEL), F32), S((1, D_MODEL), F32)],
        compiler_params=_params(1),
    )(x, g_pre, w_in, *dproj_parts, dxn)


def _inproj_bwd_w(x, g_pre, dproj_parts):
    t = x.shape[0]
    tb = min(256, t)
    nt = t // tb

    def body(x_ref, g_ref, dpre_ref, dgates_ref, dq_ref, dk_ref, dv_ref, dw_hbm, acc_ref):
        i = pl.program_id(0)
        xf = x_ref[...]
        r = lax.rsqrt(jnp.mean(xf * xf, axis=-1, keepdims=True) + RMS_EPS)
        ht = (xf * r * g_ref[...]).T.astype(BF16)

        @pl.when(i == 0)
        def _():
            acc_ref[...] = jnp.zeros_like(acc_ref)

        for u in range(N_UNITS):
            acc_ref[:, u * UNIT:(u + 1) * UNIT] += _dot(
                ht, _dproj_unit(u, dpre_ref, dgates_ref, dq_ref, dk_ref, dv_ref))

        @pl.when(i == nt - 1)
        def _():
            pltpu.sync_copy(acc_ref, dw_hbm)

    return pl.pallas_call(
        body, name="inproj_bwd_w",
        grid=(nt,),
        in_specs=[pl.BlockSpec((tb, D_MODEL), lambda i: (i, 0)), _const_spec((1, D_MODEL))] + _dproj_specs(tb),
        out_specs=pl.BlockSpec(memory_space=pl.ANY),
        out_shape=S((D_MODEL, IN_WIDTH), F32),
        scratch_shapes=[pltpu.VMEM((D_MODEL, IN_WIDTH), F32)],
        compiler_params=_params(1),
    )(x, g_pre, *dproj_parts)


def _attn_tile_size(t):
    return min(256, t)


def _softplus_parts(z):
    sp = jnp.maximum(z, 0.0) + jnp.log(1.0 + jnp.exp(-jnp.abs(z)))
    return -sp, z - sp


def _attn_fwd(proj):
    t = proj.shape[1]
    tq = _attn_tile_size(t)

    def body(q_ref, k_ref, v_ref, o_ref, bs_ref, kf_ref):
        i = pl.program_id(1)
        row = lax.broadcasted_iota(jnp.int32, (tq, tq), 0)
        col = lax.broadcasted_iota(jnp.int32, (tq, tq), 1)
        tri = (row > col).astype(BF16)
        causal = col < row
        lane = lax.broadcasted_iota(jnp.int32, (1, LANES), 1)
        lane8 = lax.broadcasted_iota(jnp.int32, (8, LANES), 1)
        q_all = q_ref[0] * ATTN_SCALE
        o_out = jnp.zeros((tq, LANES), F32)
        b_out = jnp.zeros((tq, LANES), F32)
        kf_out = jnp.zeros((8, LANES), F32)

        for hh in range(HEADS_PER_BLOCK):
            in_head = (lane >= hh * HEAD_DIM) & (lane < (hh + 1) * HEAD_DIM)
            qh = jnp.where(in_head, q_all, 0.0).astype(BF16)

            def tile(kj, carry, acc, masked, qh=qh):
                ks = pl.multiple_of(kj * tq, tq)
                kh = k_ref[0, pl.ds(ks, tq), :].astype(BF16)
                vh = v_ref[0, pl.ds(ks, tq), :].astype(BF16)
                z = _dot_nt(qh, kh)
                lm, lb = _softplus_parts(z)
                if masked:
                    lm = jnp.where(causal, lm, 0.0)
                hi, lo = _split_bf16(lm)
                between = _dot(hi, tri) + _dot(lo, tri) + carry
                w = jnp.exp(lb + between)
                if masked:
                    w = jnp.where(causal, w, 0.0)
                acc = acc + _dot(w.astype(BF16), vh)
                carry = carry + jnp.sum(lm, axis=1, keepdims=True)
                return carry, acc

            carry, acc = tile(i, jnp.zeros((tq, 1), F32), jnp.zeros((tq, LANES), F32), True)

            def more(c):
                return jnp.logical_and(c[0] >= 0, jnp.max(c[1]) >= EXP_ZERO_BELOW)

            def step(c):
                carry, acc = tile(c[0], c[1], c[2], False)
                return c[0] - 1, carry, acc

            kj_end, carry, acc = lax.while_loop(more, step, (i - 1, carry, acc))
            o_out = jnp.where(in_head, acc, o_out)
            b_out = jnp.where(in_head, carry, b_out)
            in_head8 = (lane8 >= hh * HEAD_DIM) & (lane8 < (hh + 1) * HEAD_DIM)
            kf_out = jnp.where(in_head8, (kj_end + 1).astype(F32), kf_out)

        o_ref[...] = o_out
        bs_ref[0] = b_out
        kf_ref[0, 0] = kf_out

    nq = t // tq
    return pl.pallas_call(
        body, name="attn_fwd",
        grid=(N_HEAD_BLOCKS, nq),
        in_specs=[pl.BlockSpec((1, tq, LANES), lambda p, i: (U_Q, i, p)),
                  pl.BlockSpec((1, t, LANES), lambda p, i: (U_K, 0, p)),
                  pl.BlockSpec((1, t, LANES), lambda p, i: (U_V, 0, p))],
        out_specs=[pl.BlockSpec((tq, LANES), lambda p, i: (i, p)),
                   pl.BlockSpec((1, tq, LANES), lambda p, i: (p, i, 0)),
                   pl.BlockSpec((1, 1, 8, LANES), lambda p, i: (p, i, 0, 0))],
        out_shape=[S((t, UNIT), F32), S((N_HEAD_BLOCKS, t, LANES), F32), S((N_HEAD_BLOCKS, nq, 8, LANES), F32)],
        compiler_params=_params(2),
    )(proj, proj, proj)


def _attn_bwd(proj, d_o, bsum, kfirst):
    t = proj.shape[1]
    tq = _attn_tile_size(t)
    nq = t // tq

    def body(q_ref, k_ref, v_ref, do_ref, bs_ref, kf_ref, dq_ref, dk_ref, dv_ref, dkt_ref, dvt_ref):
        i = pl.program_id(1)

        @pl.when(i == 0)
        def _():
            dkt_ref[...] = jnp.zeros_like(dkt_ref)
            dvt_ref[...] = jnp.zeros_like(dvt_ref)

        row = lax.broadcasted_iota(jnp.int32, (tq, tq), 0)
        col = lax.broadcasted_iota(jnp.int32, (tq, tq), 1)
        upto = (row <= col).astype(BF16)
        before = (row < col).astype(BF16)
        causal = col < row
        lane = lax.broadcasted_iota(jnp.int32, (1, LANES), 1)
        lane8 = lax.broadcasted_iota(jnp.int32, (8, LANES), 1)
        q_all = q_ref[0] * ATTN_SCALE
        do_all = do_ref[...]
        bs_all = bs_ref[0]
        dq_out = jnp.zeros((tq, LANES), F32)

        for hh in range(HEADS_PER_BLOCK):
            in_head = (lane >= hh * HEAD_DIM) & (lane < (hh + 1) * HEAD_DIM)
            q_m = jnp.where(in_head, q_all, 0.0)
            do_m = jnp.where(in_head, do_all, 0.0)
            qh = q_m.astype(BF16)
            doh = do_m.astype(BF16)
            qt = q_m.T.astype(BF16)
            dot_ = do_m.T.astype(BF16)
            btot = jnp.max(jnp.where(in_head, bs_all, -jnp.inf), axis=1, keepdims=True)
            in_head8 = (lane8 >= hh * HEAD_DIM) & (lane8 < (hh + 1) * HEAD_DIM)
            k_first = jnp.clip(jnp.max(jnp.where(in_head8, kf_ref[0, 0], -jnp.inf)).astype(jnp.int32), 0, i)

            def tile(kj, c_b, c_p, dq, masked, qh=qh, doh=doh, qt=qt, dot_=dot_, btot=btot):
                ks = pl.multiple_of(kj * tq, tq)
                kh = k_ref[0, pl.ds(ks, tq), :].astype(BF16)
                vh = v_ref[0, pl.ds(ks, tq), :].astype(BF16)
                z = _dot_nt(qh, kh)
                lm, lb = _softplus_parts(z)
                if masked:
                    lm = jnp.where(causal, lm, 0.0)
                hi, lo = _split_bf16(lm)
                between = btot - (c_b + _dot(hi, upto) + _dot(lo, upto))
                w = jnp.exp(lb + between)
                if masked:
                    w = jnp.where(causal, w, 0.0)
                e = w * _dot_nt(doh, vh)
                ehi, elo = _split_bf16(e)
                p_sum = c_p + _dot(ehi, before) + _dot(elo, before)
                beta = jnp.exp(lb)
                dz = e * (1.0 - beta) - p_sum * beta
                if masked:
                    dz = jnp.where(causal, dz, 0.0)
                dzb = dz.astype(BF16)
                dq = dq + _dot(dzb, kh)
                dkt_ref[kj] += _dot(qt, dzb)
                dvt_ref[kj] += _dot(dot_, w.astype(BF16))
                c_b = c_b + jnp.sum(lm, axis=1, keepdims=True)
                c_p = c_p + jnp.sum(e, axis=1, keepdims=True)
                return c_b, c_p, dq

            init = (jnp.zeros((tq, 1), F32), jnp.zeros((tq, 1), F32), jnp.zeros((tq, LANES), F32))
            c_b, c_p, dq = lax.fori_loop(k_first, i, lambda kj, c: tile(kj, c[0], c[1], c[2], False), init)
            _, _, dq = tile(i, c_b, c_p, dq, True)
            dq_out = jnp.where(in_head, dq * ATTN_SCALE, dq_out)

        dq_ref[...] = dq_out

        @pl.when(i == nq - 1)
        def _():
            for kj in range(nq):
                dk_ref[kj * tq:(kj + 1) * tq, :] = dkt_ref[kj].T
                dv_ref[kj * tq:(kj + 1) * tq, :] = dvt_ref[kj].T

    return pl.pallas_call(
        body, name="attn_bwd",
        grid=(N_HEAD_BLOCKS, nq),
        in_specs=[pl.BlockSpec((1, tq, LANES), lambda p, i: (U_Q, i, p)),
                  pl.BlockSpec((1, t, LANES), lambda p, i: (U_K, 0, p)),
                  pl.BlockSpec((1, t, LANES), lambda p, i: (U_V, 0, p)),
                  pl.BlockSpec((tq, LANES), lambda p, i: (i, p)),
                  pl.BlockSpec((1, tq, LANES), lambda p, i: (p, i, 0)),
                  pl.BlockSpec((1, 1, 8, LANES), lambda p, i: (p, i, 0, 0))],
        out_specs=[pl.BlockSpec((tq, LANES), lambda p, i: (i, p)),
                   pl.BlockSpec((t, LANES), lambda p, i: (0, p)),
                   pl.BlockSpec((t, LANES), lambda p, i: (0, p))],
        out_shape=[S((t, UNIT), F32)] * 3,
        scratch_shapes=[pltpu.VMEM((nq, LANES, tq), F32), pltpu.VMEM((nq, LANES, tq), F32)],
        compiler_params=_params(2),
    )(proj, proj, proj, d_o, bsum, kfirst)


def _mix_in_specs(tb):
    hb = tb // HALO

    def unit(u):
        return pl.BlockSpec((1, tb, UNIT), lambda i: (u, i, 0))

    def halo(u):
        return pl.BlockSpec((1, HALO, UNIT), lambda i: (u, jnp.maximum(i * hb - 1, 0), 0))

    return [unit(U_P), halo(U_P), unit(U_GP), unit(U_CA), halo(U_CA), unit(U_CB), halo(U_CB), unit(U_GC),
            unit(U_GA), pl.BlockSpec((3, tb, UNIT), lambda i: (3, i, 0)), pl.BlockSpec((3, tb, UNIT), lambda i: (4, i, 0))]


def _mix_weight_specs():
    return [_const_spec((4, GROUP, GROUP)), _const_spec((1, UNIT)), _const_spec((1, UNIT)),
            _const_spec((UNIT, D_MODEL)), _const_spec((CONV_K_PAD, UNIT)), _const_spec((1, UNIT)),
            _const_spec((1, UNIT)), _const_spec((1, UNIT)), _const_spec((UNIT, D_MODEL)),
            _const_spec((UNIT, D_MODEL)), _const_spec((D_MODEL, D_MODEL)), _const_spec((1, D_MODEL))]


def _shifted_reader(buf, rbuf, tb):
    if rbuf is None:
        return lambda s: buf[s:s + tb, :]
    length = tb + HALO - 8
    for b in range(1, 8):
        rbuf[b, :, :] = buf[b:b + length, :]

    def read(s):
        a, b = divmod(s, 8)
        return buf[s:s + tb, :] if b == 0 else rbuf[b, 8 * a:8 * a + tb, :]

    return read


def _mix_forward(i, tb, proj_refs, o_ref, w_refs, pbuf, ubuf, rbuf=None):
    p_ref, ph_ref, gp_ref, ca_ref, cah_ref, cb_ref, cbh_ref, gc_ref, ga_ref, gm0_ref, gm1_ref = proj_refs
    (poolw_ref, poolb_ref, pscale_ref, wpo_ref, convw_ref, convb_ref, lng_ref, lnb_ref,
     wco_ref, wao_ref, wo_ref, gpost_ref) = w_refs
    first = i == 0
    r = {}

    pbuf[0:HALO, :] = jnp.where(first, 0.0, ph_ref[0])
    pbuf[HALO:HALO + tb, :] = p_ref[0]
    tpos = i * tb + lax.broadcasted_iota(jnp.int32, (tb, 1), 0)
    d_parts, y_parts = [], []
    for g, win in enumerate(POOL_WINDOWS):
        cs = slice(g * GROUP, (g + 1) * GROUP)
        cur = pbuf[HALO:HALO + tb, cs]
        s = cur
        for j in range(1, win):
            s = s + pbuf[HALO - j:HALO - j + tb, cs]
        cnt = jnp.minimum(tpos + 1, win).astype(F32)
        d_g = s / cnt - cur
        d_parts.append(d_g)
        y_parts.append(_dot(d_g.astype(BF16), poolw_ref[g]))
    r["d"] = d_parts
    y = jnp.concatenate(y_parts, axis=1) + poolb_ref[...]
    r["y"] = y
    mp = y * pscale_ref[...]
    gp = gp_ref[0]
    sgp = _sigmoid(gp)
    r["mp"], r["gp"], r["sgp"] = mp, gp, sgp
    ua = mp * (gp * sgp)

    ah, bh = cah_ref[0], cbh_ref[0]
    ubuf[0:HALO, :] = jnp.where(first, 0.0, ah * _sigmoid(bh))
    a, b = ca_ref[0], cb_ref[0]
    sb = _sigmoid(b)
    ubuf[HALO:HALO + tb, :] = a * sb
    r["a"], r["sb"] = a, sb
    cv = jnp.zeros((tb, UNIT), F32) + convb_ref[...]
    off = HALO - (CONV_K - 1)
    u_at = _shifted_reader(ubuf, rbuf, tb)
    for k in range(CONV_K):
        cv = cv + convw_ref[k:k + 1, :] * u_at(off + k)
    mu = jnp.mean(cv, axis=-1, keepdims=True)
    cc = cv - mu
    rs = lax.rsqrt(jnp.mean(cc * cc, axis=-1, keepdims=True) + LN_EPS)
    nrm = cc * rs
    ln = nrm * lng_ref[...] + lnb_ref[...]
    sln = _sigmoid(ln)
    sc = ln * sln
    gc = gc_ref[0]
    sgc = _sigmoid(gc)
    r["rs"], r["nrm"], r["ln"], r["sln"], r["sc"], r["gc"], r["sgc"] = rs, nrm, ln, sln, sc, gc, sgc
    ub = sc * (gc * sgc)

    o = o_ref[...]
    ga = ga_ref[0]
    sga = _sigmoid(ga)
    r["o"], r["ga"], r["sga"] = o, ga, sga
    uc = o * (ga * sga)

    r["ua"], r["ub"], r["uc"] = ua.astype(BF16), ub.astype(BF16), uc.astype(BF16)
    ya = _dot(r["ua"], wpo_ref[...])
    yb = _dot(r["ub"], wco_ref[...])
    yc = _dot(r["uc"], wao_ref[...])
    g0 = _sigmoid(jnp.concatenate([gm0_ref[0], gm0_ref[1]], axis=1))
    g1 = _sigmoid(jnp.concatenate([gm0_ref[2], gm1_ref[0]], axis=1))
    g2 = _sigmoid(jnp.concatenate([gm1_ref[1], gm1_ref[2]], axis=1))
    r["ya"], r["yb"], r["yc"], r["g0"], r["g1"], r["g2"] = ya, yb, yc, g0, g1, g2
    m = (g0 * ya + g1 * yb + g2 * yc).astype(BF16)
    r["m"] = m
    out = _dot(m, wo_ref[...])
    r2 = lax.rsqrt(jnp.mean(out * out, axis=-1, keepdims=True) + RMS_EPS)
    r["n2"], r["r2"] = out * r2, r2
    return r


def _mix_fwd(proj, o, x, weights):
    t = x.shape[0]
    tb = min(256, t)

    def body(*refs):
        proj_refs, o_ref, x_ref = refs[0:11], refs[11], refs[12]
        w_refs = refs[13:25]
        xn_ref, pbuf, ubuf, rbuf = refs[25:29]
        r = _mix_forward(pl.program_id(0), tb, proj_refs, o_ref, w_refs, pbuf, ubuf, rbuf)
        xn_ref[...] = x_ref[...] + r["n2"] * w_refs[11][...]

    return pl.pallas_call(
        body, name="mix_fwd",
        grid=(t // tb,),
        in_specs=_mix_in_specs(tb) + [pl.BlockSpec((tb, UNIT), lambda i: (i, 0)),
                                      pl.BlockSpec((tb, D_MODEL), lambda i: (i, 0))] + _mix_weight_specs(),
        out_specs=pl.BlockSpec((tb, D_MODEL), lambda i: (i, 0)),
        out_shape=S((t, D_MODEL), F32),
        scratch_shapes=[pltpu.VMEM((HALO + tb, UNIT), F32), pltpu.VMEM((HALO + tb, UNIT), F32),
                        pltpu.VMEM((8, HALO + tb - 8, UNIT), F32)],
        compiler_params=_params(1),
    )(*([proj] * 11), o, x, *weights)


def _mix_bwd(proj, o, x, dxn, weights):
    t = x.shape[0]
    tb = min(256, t)
    nt = t // tb

    def body(*refs):
        proj_refs, o_ref, x_ref, dxn_ref = refs[0:11], refs[11], refs[12], refs[13]
        w_refs = refs[14:26]
        dg_ref, dd_ref, dcv_ref, do_ref, dwo_hbm, dwp_hbm, dwc_hbm, dwa_hbm, dpw_ref, dvec_ref = refs[26:36]
        pbuf, ubuf, dwo_ref, dwp_ref, dwc_ref, dwa_ref = refs[36:42]
        (poolw_ref, _, pscale_ref, wpo_ref, _, _, lng_ref, _, wco_ref, wao_ref, wo_ref, gpost_ref) = w_refs
        i = pl.program_id(0)

        @pl.when(i == 0)
        def _():
            for ref in (dwo_ref, dwp_ref, dwc_ref, dwa_ref, dpw_ref, dvec_ref):
                ref[...] = jnp.zeros_like(ref)

        r = _mix_forward(i, tb, proj_refs, o_ref, w_refs, pbuf, ubuf)

        def colsum(v):
            return jnp.sum(v, axis=0, keepdims=True)

        dxn = dxn_ref[...]
        n2 = r["n2"]
        dvec_ref[0:1, :] += colsum(dxn * n2)
        dn2 = dxn * gpost_ref[...]
        dout = (r["r2"] * (dn2 - n2 * jnp.mean(dn2 * n2, axis=-1, keepdims=True))).astype(BF16)
        dm = _dot_nt(dout, wo_ref[...])
        dwo_ref[...] += _dot_tn(r["m"], dout)

        g0, g1, g2 = r["g0"], r["g1"], r["g2"]
        dgm = [dm * r["ya"] * g0 * (1.0 - g0), dm * r["yb"] * g1 * (1.0 - g1), dm * r["yc"] * g2 * (1.0 - g2)]
        for bidx in range(N_BRANCH):
            dg_ref[3 + 2 * bidx] = dgm[bidx][:, 0:UNIT]
            dg_ref[4 + 2 * bidx] = dgm[bidx][:, UNIT:2 * UNIT]
        dya = (dm * g0).astype(BF16)
        dyb = (dm * g1).astype(BF16)
        dyc = (dm * g2).astype(BF16)
        dua = _dot_nt(dya, wpo_ref[...])
        dub = _dot_nt(dyb, wco_ref[...])
        duc = _dot_nt(dyc, wao_ref[...])
        dwp_ref[...] += _dot_tn(r["ua"], dya)
        dwc_ref[...] += _dot_tn(r["ub"], dyb)
        dwa_ref[...] += _dot_tn(r["uc"], dyc)

        gp, sgp = r["gp"], r["sgp"]
        dmp = dua * (gp * sgp)
        dg_ref[0] = dua * r["mp"] * _dsilu(gp, sgp)
        dvec_ref[2:3, 0:UNIT] += colsum(dmp * r["y"])
        dy = dmp * pscale_ref[...]
        dvec_ref[1:2, 0:UNIT] += colsum(dy)
        dd_parts = []
        for g in range(len(POOL_WINDOWS)):
            dy_g = dy[:, g * GROUP:(g + 1) * GROUP].astype(BF16)
            dd_parts.append(_dot_nt(dy_g, poolw_ref[g]))
            dpw_ref[g] += _dot_tn(r["d"][g].astype(BF16), dy_g)
        dd_ref[...] = jnp.concatenate(dd_parts, axis=1)

        gc, sgc = r["gc"], r["sgc"]
        dsc = dub * (gc * sgc)
        dg_ref[1] = dub * r["sc"] * _dsilu(gc, sgc)
        dln = dsc * _dsilu(r["ln"], r["sln"])
        nrm = r["nrm"]
        dvec_ref[4:5, 0:UNIT] += colsum(dln * nrm)
        dvec_ref[5:6, 0:UNIT] += colsum(dln)
        dnrm = dln * lng_ref[...]
        dcv = r["rs"] * (dnrm - jnp.mean(dnrm, axis=-1, keepdims=True)
                         - nrm * jnp.mean(dnrm * nrm, axis=-1, keepdims=True))
        dvec_ref[3:4, 0:UNIT] += colsum(dcv)
        dcv_ref[...] = dcv

        ga, sga = r["ga"], r["sga"]
        do_ref[...] = duc * (ga * sga)
        dg_ref[2] = duc * r["o"] * _dsilu(ga, sga)

        @pl.when(i == nt - 1)
        def _():
            for acc, hbm in ((dwo_ref, dwo_hbm), (dwp_ref, dwp_hbm), (dwc_ref, dwc_hbm), (dwa_ref, dwa_hbm)):
                pltpu.sync_copy(acc, hbm)

    def acc_spec(shape):
        n = len(shape)
        return pl.BlockSpec(shape, lambda i: (0,) * n)

    tok = lambda w: pl.BlockSpec((tb, w), lambda i: (i, 0))
    any_spec = pl.BlockSpec(memory_space=pl.ANY)
    w_shapes = [(D_MODEL, D_MODEL), (UNIT, D_MODEL), (UNIT, D_MODEL), (UNIT, D_MODEL)]
    return pl.pallas_call(
        body, name="mix_bwd",
        grid=(nt,),
        in_specs=_mix_in_specs(tb) + [tok(UNIT), tok(D_MODEL), tok(D_MODEL)] + _mix_weight_specs(),
        out_specs=[pl.BlockSpec((9, tb, UNIT), lambda i: (0, i, 0)), tok(UNIT), tok(UNIT), tok(UNIT)]
        + [any_spec] * 4 + [acc_spec((4, GROUP, GROUP)), acc_spec((8, D_MODEL))],
        out_shape=[S((9, t, UNIT), F32), S((t, UNIT), F32), S((t, UNIT), F32), S((t, UNIT), F32)]
        + [S(s, F32) for s in w_shapes] + [S((4, GROUP, GROUP), F32), S((8, D_MODEL), F32)],
        scratch_shapes=[pltpu.VMEM((HALO + tb, UNIT), F32), pltpu.VMEM((HALO + tb, UNIT), F32)]
        + [pltpu.VMEM(s, F32) for s in w_shapes],
        compiler_params=_params(1),
    )(*([proj] * 11), o, x, dxn, *weights)


def _halo_bwd(proj, dcv, dd, conv_w):
    t = dcv.shape[0]
    tb = min(256, t)
    hb = tb // HALO
    n_halo_blocks = t // HALO
    nt = t // tb

    def body(ca_ref, cah_ref, cb_ref, cbh_ref, dcv_ref, dcvh_ref, dd_ref, ddh_ref, cw_ref,
             dpre_ref, dcw_ref, ubuf, gbuf, nbuf, ru_buf, rg_buf, acc_ref):
        i = pl.program_id(0)
        first = i == 0
        last = i == nt - 1

        @pl.when(first)
        def _():
            acc_ref[...] = jnp.zeros_like(acc_ref)

        a, b = ca_ref[0], cb_ref[0]
        sb = _sigmoid(b)
        ubuf[0:HALO, :] = jnp.where(first, 0.0, cah_ref[0] * _sigmoid(cbh_ref[0]))
        ubuf[HALO:HALO + tb, :] = a * sb
        dcv_main = dcv_ref[...]
        gbuf[0:tb, :] = dcv_main
        gbuf[tb:tb + HALO, :] = jnp.where(last, 0.0, dcvh_ref[...])

        off = HALO - (CONV_K - 1)
        u_at = _shifted_reader(ubuf, ru_buf, tb)
        g_at = _shifted_reader(gbuf, rg_buf, tb)
        du = jnp.zeros((tb, UNIT), F32)
        for k in range(CONV_K):
            du = du + cw_ref[k:k + 1, :] * g_at(CONV_K - 1 - k)
            acc_ref[k] += jnp.sum((dcv_main * u_at(off + k)).reshape(tb // 8, 8, UNIT), axis=0)
        dpre_ref[1] = du * sb
        dpre_ref[2] = du * a * sb * (1.0 - sb)

        @pl.when(last)
        def _():
            dcw_ref[...] = jnp.sum(acc_ref[...], axis=1)

        tpos = i * tb + lax.broadcasted_iota(jnp.int32, (tb + HALO, 1), 0)
        dd_main = dd_ref[...]
        dd_ext = jnp.concatenate([dd_main, jnp.where(last, 0.0, ddh_ref[...])], axis=0)
        dp_parts = []
        for g, win in enumerate(POOL_WINDOWS):
            cs = slice(g * GROUP, (g + 1) * GROUP)
            cnt = jnp.minimum(tpos + 1, win).astype(F32)
            nbuf[:, cs] = dd_ext[:, cs] / cnt
        for g, win in enumerate(POOL_WINDOWS):
            cs = slice(g * GROUP, (g + 1) * GROUP)
            s = nbuf[0:tb, cs]
            for j in range(1, win):
                s = s + nbuf[j:j + tb, cs]
            dp_parts.append(s - dd_main[:, cs])
        dpre_ref[0] = jnp.concatenate(dp_parts, axis=1)

    def unit(u):
        return pl.BlockSpec((1, tb, UNIT), lambda i: (u, i, 0))

    def past(u):
        return pl.BlockSpec((1, HALO, UNIT), lambda i: (u, jnp.maximum(i * hb - 1, 0), 0))

    tok = pl.BlockSpec((tb, UNIT), lambda i: (i, 0))
    future = pl.BlockSpec((HALO, UNIT), lambda i: (jnp.minimum((i + 1) * hb, n_halo_blocks - 1), 0))
    return pl.pallas_call(
        body, name="halo_bwd",
        grid=(nt,),
        in_specs=[unit(U_CA), past(U_CA), unit(U_CB), past(U_CB), tok, future, tok, future,
                  _const_spec((CONV_K_PAD, UNIT))],
        out_specs=[pl.BlockSpec((3, tb, UNIT), lambda i: (0, i, 0)),
                   pl.BlockSpec((CONV_K_PAD, UNIT), lambda i: (0, 0))],
        out_shape=[S((3, t, UNIT), F32), S((CONV_K_PAD, UNIT), F32)],
        scratch_shapes=[pltpu.VMEM((HALO + tb, UNIT), F32)] * 3
        + [pltpu.VMEM((8, HALO + tb - 8, UNIT), F32)] * 2 + [pltpu.VMEM((CONV_K_PAD, 8, UNIT), F32)],
        compiler_params=_params(1),
    )(proj, proj, proj, proj, dcv, dcv, dd, dd, conv_w)


def _loss_head(y, target):
    t = y.shape[0]
    tb = min(512, t)

    def body(y_ref, t_ref, loss_ref, dy_ref):
        @pl.when(pl.program_id(0) == 0)
        def _():
            loss_ref[...] = jnp.zeros_like(loss_ref)

        err = y_ref[...] - t_ref[...]
        dy_ref[...] = err * (1.0 / D_MODEL)
        part = 0.5 * jnp.sum(jnp.mean(err * err, axis=-1, keepdims=True), axis=0, keepdims=True)
        r8 = lax.broadcasted_iota(jnp.int32, (8, LANES), 0)
        c8 = lax.broadcasted_iota(jnp.int32, (8, LANES), 1)
        loss_ref[...] += jnp.where((r8 == 0) & (c8 == 0), part, 0.0)

    return pl.pallas_call(
        body, name="loss_head",
        grid=(t // tb,),
        in_specs=[pl.BlockSpec((tb, D_MODEL), lambda i: (i, 0))] * 2,
        out_specs=[pl.BlockSpec((8, LANES), lambda i: (0, 0)), pl.BlockSpec((tb, D_MODEL), lambda i: (i, 0))],
        out_shape=[S((8, LANES), F32), S((t, D_MODEL), F32)],
        compiler_params=_params(1),
    )(y, target)


def _adamw(name, parts, w, m, v):
    n, rows, cols = parts.shape
    rb = rows
    while rb * cols * 4 * (n + 7) * 2 > 24 * 1024 * 1024 and rb % 16 == 0:
        rb //= 2

    def body(p_ref, w_ref, m_ref, v_ref, g_ref, d_ref, nm_ref, nv_ref):
        g = p_ref[0]
        for k in range(1, n):
            g = g + p_ref[k]
        nm = ADAM_B1 * m_ref[...] + (1.0 - ADAM_B1) * g
        nv = ADAM_B2 * v_ref[...] + (1.0 - ADAM_B2) * (g * g)
        m_hat = nm / (1.0 - ADAM_B1 ** ADAM_STEP)
        v_hat = nv / (1.0 - ADAM_B2 ** ADAM_STEP)
        g_ref[...] = g
        d_ref[...] = -ADAM_LR * (m_hat / (jnp.sqrt(v_hat) + ADAM_EPS) + ADAM_WD * w_ref[...])
        nm_ref[...] = nm
        nv_ref[...] = nv

    blk = pl.BlockSpec((rb, cols), lambda i: (i, 0))
    return pl.pallas_call(
        body, name=name,
        grid=(rows // rb,),
        in_specs=[pl.BlockSpec((n, rb, cols), lambda i: (0, i, 0)), blk, blk, blk],
        out_specs=[blk] * 4,
        out_shape=[S((rows, cols), F32)] * 4,
        compiler_params=_params(1),
    )(parts, w, m, v)


def _axis_slice(ref, axis, block, size):
    idx = [slice(None)] * len(ref.shape)
    idx[axis] = pl.ds(pl.multiple_of(block * size, size), size)
    return ref.at[tuple(idx)]


def _axis_index(ref, axis, index):
    return ref.at[tuple([slice(None)] * axis + [index])]


COPY_PIECE_BYTES = 2 * 1024 * 1024


def _row_pieces(shape):
    rows = shape[-2]
    size = 4 * rows * shape[-1]
    for d in shape[:-2]:
        size *= d
    n = 1
    while size // n > COPY_PIECE_BYTES and rows % (2 * n * 8) == 0:
        n *= 2
    return [(k * (rows // n), rows // n) for k in range(n)]


def _rows(ref, r0, rows):
    idx = [slice(None)] * len(ref.shape)
    idx[-2] = pl.ds(r0, rows)
    return ref.at[tuple(idx)]


def _run_copies(local, sends, recvs):
    for cp in local + sends:
        cp.start()
    for cp in recvs:
        cp.wait_recv()
    for cp in sends:
        cp.wait_send()
    for cp in local:
        cp.wait()


def _gather_weights(shards, axes):
    n = len(shards)
    fulls = [S(tuple(4 * d if k == ax else d for k, d in enumerate(s.shape)), s.dtype) for s, ax in zip(shards, axes)]

    def body(*refs):
        src, dst = refs[0:n], refs[n:2 * n]
        send_sems, recv_sems, pass_send_sems, pass_recv_sems, local_sems = refs[2 * n:2 * n + 5]
        x, y, c = lax.axis_index("x"), lax.axis_index("y"), lax.axis_index("c")
        chips = [(1 - x, y), (x, 1 - y), (1 - x, 1 - y)]

        def place(a, chip, layer=None):
            ref = _axis_slice(dst[a], axes[a], 2 * chip[0] + chip[1], src[a].shape[axes[a]])
            return ref if layer is None else ref.at[pl.ds(layer, 1)]

        def over_ici(k, chip_of_block, to):
            a, j, r0, rows = pieces[k]
            return pltpu.make_async_remote_copy(
                src_ref=_rows(src[a].at[pl.ds(c, 1)], r0, rows), dst_ref=_rows(place(a, chip_of_block, c), r0, rows),
                send_sem=send_sems.at[k], recv_sem=recv_sems.at[k], device_id=(to[0], to[1], c), device_id_type=MESH)

        def to_sibling(k, chip_of_block, layer):
            a, j, r0, rows = pieces[k]
            block = _rows(place(a, chip_of_block, layer), r0, rows)
            return pltpu.make_async_remote_copy(
                src_ref=block, dst_ref=block, send_sem=pass_send_sems.at[k], recv_sem=pass_recv_sems.at[k],
                device_id=(x, y, 1 - c), device_id_type=MESH)

        local = [pltpu.make_async_copy(_rows(src[a], r0, rows), _rows(place(a, (x, y)), r0, rows), local_sems.at[k])
                 for k, (a, r0, rows) in enumerate(own_pieces)]
        first = [over_ici(k, (x, y), chips[j]) for k, (a, j, r0, rows) in enumerate(pieces)]
        for cp in local + first:
            cp.start()
        passed = []
        for k, (a, j, r0, rows) in enumerate(pieces):
            over_ici(k, chips[j], chips[j]).wait_recv()
            passed.append(to_sibling(k, chips[j], c))
            passed[-1].start()
        for k, (a, j, r0, rows) in enumerate(pieces):
            to_sibling(k, chips[j], 1 - c).wait_recv()
        for cp in first + passed:
            cp.wait_send()
        for cp in local:
            cp.wait()

    per_array = [_row_pieces((1,) + shards[a].shape[1:]) for a in range(n)]
    pieces = [(a, j, r0, rows) for a in range(n) for r0, rows in per_array[a] for j in range(3)]
    own_pieces = [(a, r0, rows) for a in range(n) for r0, rows in _row_pieces(shards[a].shape)]
    any_spec = pl.BlockSpec(memory_space=pl.ANY)
    return pl.pallas_call(
        body, name="gather_weights",
        in_specs=[any_spec] * n, out_specs=[any_spec] * n, out_shape=fulls,
        scratch_shapes=[pltpu.SemaphoreType.DMA((len(pieces),))] * 4 + [pltpu.SemaphoreType.DMA((len(own_pieces),))],
    )(*shards)


def _swap_halves(partials, half_axes):
    n = len(partials)

    def half_shape(a):
        return tuple(d // 2 if k == half_axes[a] else d for k, d in enumerate(partials[a].shape))

    pieces = [(a, r0, rows) for a in range(n) for r0, rows in _row_pieces(half_shape(a))]

    def body(*refs):
        src, dst = refs[0:n], refs[n:2 * n]
        send_sems, recv_sems, local_sems = refs[2 * n:2 * n + 3]
        x, y, c = lax.axis_index("x"), lax.axis_index("y"), lax.axis_index("c")

        def half(a, h):
            return _axis_slice(src[a], half_axes[a], h, src[a].shape[half_axes[a]] // 2)

        def remote(k, h):
            a, r0, rows = pieces[k]
            return pltpu.make_async_remote_copy(
                src_ref=_rows(half(a, h), r0, rows), dst_ref=_rows(dst[a].at[1], r0, rows),
                send_sem=send_sems.at[k], recv_sem=recv_sems.at[k], device_id=(x, y, 1 - c), device_id_type=MESH)

        local = [pltpu.make_async_copy(_rows(half(a, c), r0, rows), _rows(dst[a].at[0], r0, rows), local_sems.at[k])
                 for k, (a, r0, rows) in enumerate(pieces)]
        sends = [remote(k, 1 - c) for k in range(len(pieces))]
        _run_copies(local, sends, [remote(k, c) for k in range(len(pieces))])

    any_spec = pl.BlockSpec(memory_space=pl.ANY)
    return pl.pallas_call(
        body, name="swap_halves",
        in_specs=[any_spec] * n, out_specs=[any_spec] * n,
        out_shape=[S((2,) + half_shape(a), partials[a].dtype) for a in range(n)],
        scratch_shapes=[pltpu.SemaphoreType.DMA((len(pieces),))] * 3,
    )(*partials)


def _send_to_owners(chip_halves, shard_axes, packed):
    n = len(chip_halves)

    def owned_shape(a):
        return tuple(d // 4 if k == shard_axes[a] else d for k, d in enumerate(chip_halves[a].shape))

    flips = [(dx, dy, dc) for dx in (0, 1) for dy in (0, 1) for dc in (0, 1)][1:]

    def body(*refs):
        src, psrc, dst, pdst = refs[0:n], refs[n], refs[n + 1:2 * n + 1], refs[2 * n + 1]
        send_sems, recv_sems, local_sems, psend_sems, precv_sems = refs[2 * n + 2:2 * n + 7]
        x, y, c = lax.axis_index("x"), lax.axis_index("y"), lax.axis_index("c")
        my_chip = 2 * x + y
        chips = [(1 - x, y), (x, 1 - y), (1 - x, 1 - y)]

        def owned(a, chip):
            return _axis_slice(src[a], shard_axes[a], 2 * chip[0] + chip[1], src[a].shape[shard_axes[a]] // 4)

        def remote(a, j, to, from_chip):
            return pltpu.make_async_remote_copy(
                src_ref=owned(a, to), dst_ref=dst[a].at[from_chip], send_sem=send_sems.at[a, j],
                recv_sem=recv_sems.at[a, j], device_id=(to[0], to[1], c), device_id_type=MESH)

        def peer(f):
            return (1 - x if f[0] else x, 1 - y if f[1] else y, 1 - c if f[2] else c)

        def premote(j, to, from_dev):
            return pltpu.make_async_remote_copy(
                src_ref=psrc, dst_ref=pdst.at[from_dev], send_sem=psend_sems.at[j], recv_sem=precv_sems.at[j],
                device_id=to, device_id_type=MESH)

        local = [pltpu.make_async_copy(_rows(owned(a, (x, y)), r0, rows), _rows(dst[a].at[my_chip], r0, rows),
                                       local_sems.at[k]) for k, (a, r0, rows) in enumerate(own_pieces)]
        local.append(pltpu.make_async_copy(psrc, pdst.at[4 * x + 2 * y + c], local_sems.at[len(own_pieces)]))
        sends = [remote(a, j, chip, my_chip) for a in range(n) for j, chip in enumerate(chips)]
        sends += [premote(j, peer(f), 4 * x + 2 * y + c) for j, f in enumerate(flips)]
        recvs = [remote(a, j, (x, y), 2 * chip[0] + chip[1]) for a in range(n) for j, chip in enumerate(chips)]
        for j, f in enumerate(flips):
            px, py, pc = peer(f)
            recvs.append(premote(j, (x, y, c), 4 * px + 2 * py + pc))
        _run_copies(local, sends, recvs)

    own_pieces = [(a, r0, rows) for a in range(n) for r0, rows in _row_pieces(owned_shape(a))]
    any_spec = pl.BlockSpec(memory_space=pl.ANY)
    return pl.pallas_call(
        body, name="send_to_owners",
        in_specs=[any_spec] * (n + 1), out_specs=[any_spec] * (n + 1),
        out_shape=[S((4,) + owned_shape(a), chip_halves[a].dtype) for a in range(n)]
        + [S((N_DEV,) + packed.shape, packed.dtype)],
        scratch_shapes=[pltpu.SemaphoreType.DMA((n, 3)), pltpu.SemaphoreType.DMA((n, 3)),
                        pltpu.SemaphoreType.DMA((len(own_pieces) + 1,)), pltpu.SemaphoreType.DMA((7,)),
                        pltpu.SemaphoreType.DMA((7,))],
    )(*chip_halves, packed)


def _join_halves(reduced, half_axes):
    n = len(reduced)

    def joined_shape(a):
        s, ax = reduced[a].shape, half_axes[a]
        return s[:ax] + (2,) + s[ax:]

    pieces = [(a, r0, rows) for a in range(n) for r0, rows in _row_pieces(reduced[a].shape)]

    def body(*refs):
        src, dst = refs[0:n], refs[n:2 * n]
        send_sems, recv_sems, local_sems = refs[2 * n:2 * n + 3]
        x, y, c = lax.axis_index("x"), lax.axis_index("y"), lax.axis_index("c")

        def remote(k, h):
            a, r0, rows = pieces[k]
            return pltpu.make_async_remote_copy(
                src_ref=_rows(src[a], r0, rows), dst_ref=_rows(_axis_index(dst[a], half_axes[a], h), r0, rows),
                send_sem=send_sems.at[k], recv_sem=recv_sems.at[k], device_id=(x, y, 1 - c), device_id_type=MESH)

        local = [pltpu.make_async_copy(_rows(src[a], r0, rows),
                                       _rows(_axis_index(dst[a], half_axes[a], c), r0, rows), local_sems.at[k])
                 for k, (a, r0, rows) in enumerate(pieces)]
        _run_copies(local, [remote(k, c) for k in range(len(pieces))], [remote(k, 1 - c) for k in range(len(pieces))])

    any_spec = pl.BlockSpec(memory_space=pl.ANY)
    return pl.pallas_call(
        body, name="join_halves",
        in_specs=[any_spec] * n, out_specs=[any_spec] * n,
        out_shape=[S(joined_shape(a), reduced[a].dtype) for a in range(n)],
        scratch_shapes=[pltpu.SemaphoreType.DMA((len(pieces),))] * 3,
    )(*reduced)


def _sum_slots(name, parts):
    n = parts.shape[0]
    shape = parts.shape[1:]
    flat = parts.reshape((n, -1, shape[-1]))
    rows, cols = flat.shape[1:]
    rb = rows
    while rb * cols * 4 * (n + 1) * 2 > 24 * 1024 * 1024 and rb % 16 == 0:
        rb //= 2

    def body(p_ref, o_ref):
        acc = p_ref[0]
        for k in range(1, n):
            acc = acc + p_ref[k]
        o_ref[...] = acc

    out = pl.pallas_call(
        body, name=name,
        grid=(rows // rb,),
        in_specs=[pl.BlockSpec((n, rb, cols), lambda i: (0, i, 0))],
        out_specs=pl.BlockSpec((rb, cols), lambda i: (i, 0)),
        out_shape=S((rows, cols), F32),
        compiler_params=_params(1),
    )(flat)
    return out.reshape(shape)


SMALL_NAMES = ("norm_pre", "pool_w", "pool_b", "pool_scale", "conv_b", "conv_ln_g", "conv_ln_b", "norm_post")
BIG_NAMES = ("w_in", "w_pool_out", "conv_w", "w_conv_out", "w_attn_out", "w_o")
SHARD_AXIS = {"w_in": 2, "w_pool_out": 2, "conv_w": 2, "w_conv_out": 2, "w_attn_out": 2, "w_o": 1}
WEIGHT_ORDER = ("norm_pre", "w_in", "pool_w", "pool_b", "pool_scale", "w_pool_out", "conv_w", "conv_b",
                "conv_ln_g", "conv_ln_b", "w_conv_out", "w_attn_out", "w_o", "norm_post")


def _pack_small(parts):
    return jnp.concatenate([parts[n].reshape(-1, LANES) for n in SMALL_NAMES], axis=0)


def _unpack_small(packed, shapes):
    out, r0 = {}, 0
    for n in SMALL_NAMES:
        size = 1
        for d in shapes[n]:
            size *= d
        rows = size // LANES
        out[n] = packed[r0:r0 + rows].reshape(shapes[n])
        r0 += rows
    return out


def _layer_weights(full, small, l):
    row = lambda a: a[l][None, :]
    return (small["pool_w"][l].astype(BF16), small["pool_b"][l].reshape(1, UNIT), row(small["pool_scale"]),
            full["w_pool_out"][l], full["conv_w"][l], row(small["conv_b"]), row(small["conv_ln_g"]),
            row(small["conv_ln_b"]), full["w_conv_out"][l], full["w_attn_out"][l], full["w_o"][l],
            row(small["norm_post"]))


def _forward_backward(x, target, full, small):
    depth = full["w_in"].shape[0]
    acts = []
    for l in range(depth):
        g_pre = small["norm_pre"][l][None, :]
        proj = _inproj_fwd(x, g_pre, full["w_in"][l])
        o, bsum, kfirst = _attn_fwd(proj)
        weights = _layer_weights(full, small, l)
        x_next = _mix_fwd(proj, o, x, weights)
        acts.append((x, proj, o, bsum, kfirst, weights, g_pre))
        x = x_next
    loss_tile, dx = _loss_head(x, target)

    big = {n: [None] * depth for n in BIG_NAMES}
    sm = {n: [None] * depth for n in SMALL_NAMES}
    for l in reversed(range(depth)):
        x_in, proj, o, bsum, kfirst, weights, g_pre = acts[l]
        dgates, dd, dcv, d_o, dwo, dwp, dwc, dwa, dpw, dvec = _mix_bwd(proj, o, x_in, dx, weights)
        dpre, dcw = _halo_bwd(proj, dcv, dd, full["conv_w"][l])
        dq, dk, dv = _attn_bwd(proj, d_o, bsum, kfirst)
        dproj_parts = (dpre, dgates, dq, dk, dv)
        dx, dg_pre = _inproj_bwd_x(x_in, g_pre, full["w_in"][l], dproj_parts, dx)
        big["w_in"][l] = _inproj_bwd_w(x_in, g_pre, dproj_parts)
        big["w_pool_out"][l], big["w_conv_out"][l], big["w_attn_out"][l], big["w_o"][l] = dwp, dwc, dwa, dwo
        big["conv_w"][l] = dcw
        sm["norm_pre"][l] = dg_pre[0]
        sm["pool_w"][l] = dpw
        sm["norm_post"][l] = dvec[0]
        sm["pool_b"][l] = dvec[1, 0:UNIT].reshape(4, GROUP)
        sm["pool_scale"][l] = dvec[2, 0:UNIT]
        sm["conv_b"][l] = dvec[3, 0:UNIT]
        sm["conv_ln_g"][l] = dvec[4, 0:UNIT]
        sm["conv_ln_b"][l] = dvec[5, 0:UNIT]
    big = {n: jnp.stack(v) for n, v in big.items()}
    sm = {n: jnp.stack(v) for n, v in sm.items()}
    return loss_tile, dx, big, sm


def kernel(x, norm_pre, w_in, pool_w, pool_b, pool_scale, w_pool_out, conv_w, conv_b, conv_ln_g, conv_ln_b, w_conv_out, w_attn_out, w_o, norm_post, loss_target, m_norm_pre, m_w_in, m_pool_w, m_pool_b, m_pool_scale, m_w_pool_out, m_conv_w, m_conv_b, m_conv_ln_g, m_conv_ln_b, m_w_conv_out, m_w_attn_out, m_w_o, m_norm_post, v_norm_pre, v_w_in, v_pool_w, v_pool_b, v_pool_scale, v_w_pool_out, v_conv_w, v_conv_b, v_conv_ln_g, v_conv_ln_b, v_w_conv_out, v_w_attn_out, v_w_o, v_norm_post):
    w = dict(norm_pre=norm_pre, w_in=w_in, pool_w=pool_w, pool_b=pool_b, pool_scale=pool_scale,
             w_pool_out=w_pool_out, conv_w=conv_w, conv_b=conv_b, conv_ln_g=conv_ln_g, conv_ln_b=conv_ln_b,
             w_conv_out=w_conv_out, w_attn_out=w_attn_out, w_o=w_o, norm_post=norm_post)
    m = dict(norm_pre=m_norm_pre, w_in=m_w_in, pool_w=m_pool_w, pool_b=m_pool_b, pool_scale=m_pool_scale,
             w_pool_out=m_w_pool_out, conv_w=m_conv_w, conv_b=m_conv_b, conv_ln_g=m_conv_ln_g,
             conv_ln_b=m_conv_ln_b, w_conv_out=m_w_conv_out, w_attn_out=m_w_attn_out, w_o=m_w_o,
             norm_post=m_norm_post)
    v = dict(norm_pre=v_norm_pre, w_in=v_w_in, pool_w=v_pool_w, pool_b=v_pool_b, pool_scale=v_pool_scale,
             w_pool_out=v_w_pool_out, conv_w=v_conv_w, conv_b=v_conv_b, conv_ln_g=v_conv_ln_g,
             conv_ln_b=v_conv_ln_b, w_conv_out=v_w_conv_out, w_attn_out=v_w_attn_out, w_o=v_w_o,
             norm_post=v_norm_post)
    pad_taps = lambda a: jnp.pad(a, ((0, 0), (0, CONV_K_PAD - CONV_K), (0, 0)))

    shards = [pad_taps(w[n]) if n == "conv_w" else w[n].astype(BF16) for n in BIG_NAMES]
    full = dict(zip(BIG_NAMES, _gather_weights(shards, [SHARD_AXIS[n] for n in BIG_NAMES])))
    small = {n: w[n] for n in SMALL_NAMES}

    loss_tile, dx, big, sm = _forward_backward(x[0], loss_target[0], full, small)

    packed = jnp.concatenate([_pack_small(sm), loss_tile], axis=0)
    layer_axis = [0] * len(BIG_NAMES)
    pairs = _swap_halves([big[n] for n in BIG_NAMES], layer_axis)
    chip_sums = [_sum_slots("sum_cores_" + n, p) for n, p in zip(BIG_NAMES, pairs)]
    exchanged = _send_to_owners(chip_sums, [SHARD_AXIS[n] for n in BIG_NAMES], packed)
    reduced = [_sum_slots("sum_chips_" + n, p) for n, p in zip(BIG_NAMES, exchanged[:-1])]
    joined = _join_halves(reduced, layer_axis)
    out = {}
    for n, g in zip(BIG_NAMES, joined):
        wn, mn, vn = (pad_taps(a[n]) if n == "conv_w" else a[n] for a in (w, m, v))
        shape = wn.shape
        flat = lambda a: a.reshape((-1, shape[-1]))
        res = _adamw("adamw_" + n, g.reshape((1, -1, shape[-1])), flat(wn), flat(mn), flat(vn))
        res = [r.reshape(shape) for r in res]
        out[n] = [r[:, :CONV_K] for r in res] if n == "conv_w" else res
    parts = exchanged[-1]
    n_small_rows = parts.shape[1] - 8
    zeros_tile = jnp.zeros((8, LANES), F32)
    packs = [jnp.concatenate([_pack_small(a), zeros_tile], axis=0) for a in (w, m, v)]
    res = _adamw("adamw_small", parts, *packs)
    loss = res[0][n_small_rows, 0]
    shapes = {n: w[n].shape for n in SMALL_NAMES}
    unpacked = [_unpack_small(r[:n_small_rows], shapes) for r in res]
    for n in SMALL_NAMES:
        out[n] = [u[n] for u in unpacked]

    grads = [out[n][0] for n in WEIGHT_ORDER]
    deltas = [out[n][1] for n in WEIGHT_ORDER]
    new_m = [out[n][2] for n in WEIGHT_ORDER]
    new_v = [out[n][3] for n in WEIGHT_ORDER]
    return (loss, dx[None], *grads, *deltas, *new_m, *new_v)
```

```python
import jax
import jax.numpy as jnp
from jax import lax
from jax.experimental import pallas as pl
from jax.experimental.pallas import tpu as pltpu

F32 = jnp.float32
BF16 = jnp.bfloat16

D_MODEL = 1024
UNIT = 512
N_UNITS = 15
IN_WIDTH = UNIT * N_UNITS
N_HEADS = 8
HEAD_DIM = 64
HEADS_PER_BLOCK = 2
N_HEAD_BLOCKS = N_HEADS // HEADS_PER_BLOCK
LANES = 128
CONV_K = 31
CONV_K_PAD = 32
HALO = 32
POOL_WINDOWS = (2, 4, 8, 16)
GROUP = 128
N_BRANCH = 3
RMS_EPS = 1e-6
LN_EPS = 1e-5
ATTN_SCALE = 0.125
EXP_ZERO_BELOW = -104.0

ADAM_LR = 0.001
ADAM_B1 = 0.9
ADAM_B2 = 0.999
ADAM_EPS = 1e-08
ADAM_WD = 0.01
ADAM_STEP = 10

U_P, U_GP, U_CA, U_CB, U_GC, U_Q, U_K, U_V, U_GA, U_GM = 0, 1, 2, 3, 4, 5, 6, 7, 8, 9

V7X_VMEM_LIMIT = 62 * 1024 * 1024
N_DEV = 8
MESH = pl.DeviceIdType.MESH

S = jax.ShapeDtypeStruct


def _params(n_grid):
    return pltpu.CompilerParams(dimension_semantics=("arbitrary",) * n_grid, vmem_limit_bytes=V7X_VMEM_LIMIT)


def _sigmoid(x):
    return 1.0 / (1.0 + jnp.exp(-x))


def _dsilu(x, s):
    return s * (1.0 + x * (1.0 - s))


def _dot(a, b):
    return jnp.dot(a, b, preferred_element_type=F32)


def _dot_nt(a, b):
    return lax.dot_general(a, b, (((1,), (1,)), ((), ())), preferred_element_type=F32)


def _dot_tn(a, b):
    return lax.dot_general(a, b, (((0,), (0,)), ((), ())), preferred_element_type=F32)


def _split_bf16(x):
    hi = x.astype(BF16)
    lo = (x - hi.astype(F32)).astype(BF16)
    return hi, lo


def _const_spec(shape):
    n = len(shape)
    return pl.BlockSpec(shape, lambda *_: (0,) * n, pipeline_mode=pl.Buffered(1))


def _inproj_fwd(x, g_pre, w_in):
    t = x.shape[0]
    tb = min(1024, t)
    ug = 3
    nb = ug * UNIT

    def body(x_ref, g_ref, w_ref, o_ref, h_ref):
        @pl.when(pl.program_id(1) == 0)
        def _():
            xf = x_ref[...]
            r = lax.rsqrt(jnp.mean(xf * xf, axis=-1, keepdims=True) + RMS_EPS)
            h_ref[...] = (xf * r * g_ref[...]).astype(BF16)

        acc = _dot(h_ref[...], w_ref[...])
        for u in range(ug):
            o_ref[u] = acc[:, u * UNIT:(u + 1) * UNIT]

    return pl.pallas_call(
        body, name="inproj_fwd",
        grid=(t // tb, N_UNITS // ug),
        in_specs=[pl.BlockSpec((tb, D_MODEL), lambda i, j: (i, 0)),
                  pl.BlockSpec((1, D_MODEL), lambda i, j: (0, 0)),
                  pl.BlockSpec((D_MODEL, nb), lambda i, j: (0, j))],
        out_specs=pl.BlockSpec((ug, tb, UNIT), lambda i, j: (j, i, 0)),
        out_shape=S((N_UNITS, t, UNIT), F32),
        scratch_shapes=[pltpu.VMEM((tb, D_MODEL), BF16)],
        compiler_params=_params(2),
    )(x, g_pre, w_in)


def _dproj_specs(tb):
    tok = pl.BlockSpec((tb, UNIT), lambda i: (i, 0))
    return [pl.BlockSpec((3, tb, UNIT), lambda i: (0, i, 0)), pl.BlockSpec((9, tb, UNIT), lambda i: (0, i, 0)),
            tok, tok, tok]


def _dproj_unit(u, dpre_ref, dgates_ref, dq_ref, dk_ref, dv_ref):
    pre = {U_P: 0, U_CA: 1, U_CB: 2}
    gate = {U_GP: 0, U_GC: 1, U_GA: 2}
    if u in pre:
        val = dpre_ref[pre[u]]
    elif u in gate:
        val = dgates_ref[gate[u]]
    elif u >= U_GM:
        val = dgates_ref[3 + u - U_GM]
    else:
        val = {U_Q: dq_ref, U_K: dk_ref, U_V: dv_ref}[u][...]
    return val.astype(BF16)


def _inproj_bwd_x(x, g_pre, w_in, dproj_parts, dxn):
    t = x.shape[0]
    tb = min(256, t)

    def body(x_ref, g_ref, w_ref, dpre_ref, dgates_ref, dq_ref, dk_ref, dv_ref, dxn_ref, dx_ref, dg_ref):
        @pl.when(pl.program_id(0) == 0)
        def _():
            dg_ref[...] = jnp.zeros_like(dg_ref)

        dh = jnp.zeros((tb, D_MODEL), F32)
        for u in range(N_UNITS):
            dh = dh + _dot_nt(_dproj_unit(u, dpre_ref, dgates_ref, dq_ref, dk_ref, dv_ref),
                              w_ref[:, u * UNIT:(u + 1) * UNIT])
        xf = x_ref[...]
        r = lax.rsqrt(jnp.mean(xf * xf, axis=-1, keepdims=True) + RMS_EPS)
        xhat = xf * r
        dg_ref[...] += jnp.sum(dh * xhat, axis=0, keepdims=True)
        dxhat = dh * g_ref[...]
        dx_ref[...] = dxn_ref[...] + r * (dxhat - xhat * jnp.mean(dxhat * xhat, axis=-1, keepdims=True))

    tokd = pl.BlockSpec((tb, D_MODEL), lambda i: (i, 0))
    return pl.pallas_call(
        body, name="inproj_bwd_x",
        grid=(t // tb,),
        in_specs=[tokd, _const_spec((1, D_MODEL)), _const_spec((D_MODEL, IN_WIDTH))] + _dproj_specs(tb) + [tokd],
        out_specs=[tokd, pl.BlockSpec((1, D_MODEL), lambda i: (0, 0))],
        out_shape=[S((t, D_MODEL), F32), S((1, D_MODEL), F32)],
        compiler_params=_params(1),
    )(x, g_pre, w_in, *dproj_parts, dxn)


def _inproj_bwd_w(x, g_pre, dproj_parts):
    t = x.shape[0]
    tb = min(256, t)
    nt = t // tb

    def body(x_ref, g_ref, dpre_ref, dgates_ref, dq_ref, dk_ref, dv_ref, dw_hbm, acc_ref):
        i = pl.program_id(0)
        xf = x_ref[...]
        r = lax.rsqrt(jnp.mean(xf * xf, axis=-1, keepdims=True) + RMS_EPS)
        ht = (xf * r * g_ref[...]).T.astype(BF16)

        @pl.when(i == 0)
        def _():
            acc_ref[...] = jnp.zeros_like(acc_ref)

        for u in range(N_UNITS):
            acc_ref[:, u * UNIT:(u + 1) * UNIT] += _dot(
                ht, _dproj_unit(u, dpre_ref, dgates_ref, dq_ref, dk_ref, dv_ref))

        @pl.when(i == nt - 1)
        def _():
            pltpu.sync_copy(acc_ref, dw_hbm)

    return pl.pallas_call(
        body, name="inproj_bwd_w",
        grid=(nt,),
        in_specs=[pl.BlockSpec((tb, D_MODEL), lambda i: (i, 0)), _const_spec((1, D_MODEL))] + _dproj_specs(tb),
        out_specs=pl.BlockSpec(memory_space=pl.ANY),
        out_shape=S((D_MODEL, IN_WIDTH), F32),
        scratch_shapes=[pltpu.VMEM((D_MODEL, IN_WIDTH), F32)],
        compiler_params=_params(1),
    )(x, g_pre, *dproj_parts)


def _attn_tile_size(t):
    return min(256, t)


def _softplus_parts(z):
    sp = jnp.maximum(z, 0.0) + jnp.log(1.0 + jnp.exp(-jnp.abs(z)))
    return -sp, z - sp


def _attn_fwd(proj):
    t = proj.shape[1]
    tq = _attn_tile_size(t)

    def body(q_ref, k_ref, v_ref, o_ref, bs_ref, kf_ref):
        i = pl.program_id(1)
        row = lax.broadcasted_iota(jnp.int32, (tq, tq), 0)
        col = lax.broadcasted_iota(jnp.int32, (tq, tq), 1)
        tri = (row > col).astype(BF16)
        causal = col < row
        lane = lax.broadcasted_iota(jnp.int32, (1, LANES), 1)
        lane8 = lax.broadcasted_iota(jnp.int32, (8, LANES), 1)
        q_all = q_ref[0] * ATTN_SCALE
        o_out = jnp.zeros((tq, LANES), F32)
        b_out = jnp.zeros((tq, LANES), F32)
        kf_out = jnp.zeros((8, LANES), F32)

        for hh in range(HEADS_PER_BLOCK):
            in_head = (lane >= hh * HEAD_DIM) & (lane < (hh + 1) * HEAD_DIM)
            qh = jnp.where(in_head, q_all, 0.0).astype(BF16)

            def tile(kj, carry, acc, masked, qh=qh):
                ks = pl.multiple_of(kj * tq, tq)
                kh = k_ref[0, pl.ds(ks, tq), :].astype(BF16)
                vh = v_ref[0, pl.ds(ks, tq), :].astype(BF16)
                z = _dot_nt(qh, kh)
                lm, lb = _softplus_parts(z)
                if masked:
                    lm = jnp.where(causal, lm, 0.0)
                hi, lo = _split_bf16(lm)
                between = _dot(hi, tri) + _dot(lo, tri) + carry
                w = jnp.exp(lb + between)
                if masked:
                    w = jnp.where(causal, w, 0.0)
                acc = acc + _dot(w.astype(BF16), vh)
                carry = carry + jnp.sum(lm, axis=1, keepdims=True)
                return carry, acc

            carry, acc = tile(i, jnp.zeros((tq, 1), F32), jnp.zeros((tq, LANES), F32), True)

            def more(c):
                return jnp.logical_and(c[0] >= 0, jnp.max(c[1]) >= EXP_ZERO_BELOW)

            def step(c):
                carry, acc = tile(c[0], c[1], c[2], False)
                return c[0] - 1, carry, acc

            kj_end, carry, acc = lax.while_loop(more, step, (i - 1, carry, acc))
            o_out = jnp.where(in_head, acc, o_out)
            b_out = jnp.where(in_head, carry, b_out)
            in_head8 = (lane8 >= hh * HEAD_DIM) & (lane8 < (hh + 1) * HEAD_DIM)
            kf_out = jnp.where(in_head8, (kj_end + 1).astype(F32), kf_out)

        o_ref[...] = o_out
        bs_ref[0] = b_out
        kf_ref[0, 0] = kf_out

    nq = t // tq
    return pl.pallas_call(
        body, name="attn_fwd",
        grid=(N_HEAD_BLOCKS, nq),
        in_specs=[pl.BlockSpec((1, tq, LANES), lambda p, i: (U_Q, i, p)),
                  pl.BlockSpec((1, t, LANES), lambda p, i: (U_K, 0, p)),
                  pl.BlockSpec((1, t, LANES), lambda p, i: (U_V, 0, p))],
        out_specs=[pl.BlockSpec((tq, LANES), lambda p, i: (i, p)),
                   pl.BlockSpec((1, tq, LANES), lambda p, i: (p, i, 0)),
                   pl.BlockSpec((1, 1, 8, LANES), lambda p, i: (p, i, 0, 0))],
        out_shape=[S((t, UNIT), F32), S((N_HEAD_BLOCKS, t, LANES), F32), S((N_HEAD_BLOCKS, nq, 8, LANES), F32)],
        compiler_params=_params(2),
    )(proj, proj, proj)


def _attn_bwd(proj, d_o, bsum, kfirst):
    t = proj.shape[1]
    tq = _attn_tile_size(t)
    nq = t // tq

    def body(q_ref, k_ref, v_ref, do_ref, bs_ref, kf_ref, dq_ref, dk_ref, dv_ref, dkt_ref, dvt_ref):
        i = pl.program_id(1)

        @pl.when(i == 0)
        def _():
            dkt_ref[...] = jnp.zeros_like(dkt_ref)
            dvt_ref[...] = jnp.zeros_like(dvt_ref)

        row = lax.broadcasted_iota(jnp.int32, (tq, tq), 0)
        col = lax.broadcasted_iota(jnp.int32, (tq, tq), 1)
        upto = (row <= col).astype(BF16)
        before = (row < col).astype(BF16)
        causal = col < row
        lane = lax.broadcasted_iota(jnp.int32, (1, LANES), 1)
        lane8 = lax.broadcasted_iota(jnp.int32, (8, LANES), 1)
        q_all = q_ref[0] * ATTN_SCALE
        do_all = do_ref[...]
        bs_all = bs_ref[0]
        dq_out = jnp.zeros((tq, LANES), F32)

        for hh in range(HEADS_PER_BLOCK):
            in_head = (lane >= hh * HEAD_DIM) & (lane < (hh + 1) * HEAD_DIM)
            q_m = jnp.where(in_head, q_all, 0.0)
            do_m = jnp.where(in_head, do_all, 0.0)
            qh = q_m.astype(BF16)
            doh = do_m.astype(BF16)
            qt = q_m.T.astype(BF16)
            dot_ = do_m.T.astype(BF16)
            btot = jnp.max(jnp.where(in_head, bs_all, -jnp.inf), axis=1, keepdims=True)
            in_head8 = (lane8 >= hh * HEAD_DIM) & (lane8 < (hh + 1) * HEAD_DIM)
            k_first = jnp.clip(jnp.max(jnp.where(in_head8, kf_ref[0, 0], -jnp.inf)).astype(jnp.int32), 0, i)

            def tile(kj, c_b, c_p, dq, masked, qh=qh, doh=doh, qt=qt, dot_=dot_, btot=btot):
                ks = pl.multiple_of(kj * tq, tq)
                kh = k_ref[0, pl.ds(ks, tq), :].astype(BF16)
                vh = v_ref[0, pl.ds(ks, tq), :].astype(BF16)
                z = _dot_nt(qh, kh)
                lm, lb = _softplus_parts(z)
                if masked:
                    lm = jnp.where(causal, lm, 0.0)
                hi, lo = _split_bf16(lm)
                between = btot - (c_b + _dot(hi, upto) + _dot(lo, upto))
                w = jnp.exp(lb + between)
                if masked:
                    w = jnp.where(causal, w, 0.0)
                e = w * _dot_nt(doh, vh)
                ehi, elo = _split_bf16(e)
                p_sum = c_p + _dot(ehi, before) + _dot(elo, before)
                beta = jnp.exp(lb)
                dz = e * (1.0 - beta) - p_sum * beta
                if masked:
                    dz = jnp.where(causal, dz, 0.0)
                dzb = dz.astype(BF16)
                dq = dq + _dot(dzb, kh)
                dkt_ref[kj] += _dot(qt, dzb)
                dvt_ref[kj] += _dot(dot_, w.astype(BF16))
                c_b = c_b + jnp.sum(lm, axis=1, keepdims=True)
                c_p = c_p + jnp.sum(e, axis=1, keepdims=True)
                return c_b, c_p, dq

            init = (jnp.zeros((tq, 1), F32), jnp.zeros((tq, 1), F32), jnp.zeros((tq, LANES), F32))
            c_b, c_p, dq = lax.fori_loop(k_first, i, lambda kj, c: tile(kj, c[0], c[1], c[2], False), init)
            _, _, dq = tile(i, c_b, c_p, dq, True)
            dq_out = jnp.where(in_head, dq * ATTN_SCALE, dq_out)

        dq_ref[...] = dq_out

        @pl.when(i == nq - 1)
        def _():
            for kj in range(nq):
                dk_ref[kj * tq:(kj + 1) * tq, :] = dkt_ref[kj].T
                dv_ref[kj * tq:(kj + 1) * tq, :] = dvt_ref[kj].T

    return pl.pallas_call(
        body, name="attn_bwd",
        grid=(N_HEAD_BLOCKS, nq),
        in_specs=[pl.BlockSpec((1, tq, LANES), lambda p, i: (U_Q, i, p)),
                  pl.BlockSpec((1, t, LANES), lambda p, i: (U_K, 0, p)),
                  pl.BlockSpec((1, t, LANES), lambda p, i: (U_V, 0, p)),
                  pl.BlockSpec((tq, LANES), lambda p, i: (i, p)),
                  pl.BlockSpec((1, tq, LANES), lambda p, i: (p, i, 0)),
                  pl.BlockSpec((1, 1, 8, LANES), lambda p, i: (p, i, 0, 0))],
        out_specs=[pl.BlockSpec((tq, LANES), lambda p, i: (i, p)),
                   pl.BlockSpec((t, LANES), lambda p, i: (0, p)),
                   pl.BlockSpec((t, LANES), lambda p, i: (0, p))],
        out_shape=[S((t, UNIT), F32)] * 3,
        scratch_shapes=[pltpu.VMEM((nq, LANES, tq), F32), pltpu.VMEM((nq, LANES, tq), F32)],
        compiler_params=_params(2),
    )(proj, proj, proj, d_o, bsum, kfirst)


def _mix_in_specs(tb):
    hb = tb // HALO

    def unit(u):
        return pl.BlockSpec((1, tb, UNIT), lambda i: (u, i, 0))

    def halo(u):
        return pl.BlockSpec((1, HALO, UNIT), lambda i: (u, jnp.maximum(i * hb - 1, 0), 0))

    return [unit(U_P), halo(U_P), unit(U_GP), unit(U_CA), halo(U_CA), unit(U_CB), halo(U_CB), unit(U_GC),
            unit(U_GA), pl.BlockSpec((3, tb, UNIT), lambda i: (3, i, 0)), pl.BlockSpec((3, tb, UNIT), lambda i: (4, i, 0))]


def _mix_weight_specs():
    return [_const_spec((4, GROUP, GROUP)), _const_spec((1, UNIT)), _const_spec((1, UNIT)),
            _const_spec((UNIT, D_MODEL)), _const_spec((CONV_K_PAD, UNIT)), _const_spec((1, UNIT)),
            _const_spec((1, UNIT)), _const_spec((1, UNIT)), _const_spec((UNIT, D_MODEL)),
            _const_spec((UNIT, D_MODEL)), _const_spec((D_MODEL, D_MODEL)), _const_spec((1, D_MODEL))]


def _shifted_reader(buf, rbuf, tb):
    if rbuf is None:
        return lambda s: buf[s:s + tb, :]
    length = tb + HALO - 8
    for b in range(1, 8):
        rbuf[b, :, :] = buf[b:b + length, :]

    def read(s):
        a, b = divmod(s, 8)
        return buf[s:s + tb, :] if b == 0 else rbuf[b, 8 * a:8 * a + tb, :]

    return read


def _mix_forward(i, tb, proj_refs, o_ref, w_refs, pbuf, ubuf, rbuf=None):
    p_ref, ph_ref, gp_ref, ca_ref, cah_ref, cb_ref, cbh_ref, gc_ref, ga_ref, gm0_ref, gm1_ref = proj_refs
    (poolw_ref, poolb_ref, pscale_ref, wpo_ref, convw_ref, convb_ref, lng_ref, lnb_ref,
     wco_ref, wao_ref, wo_ref, gpost_ref) = w_refs
    first = i == 0
    r = {}

    pbuf[0:HALO, :] = jnp.where(first, 0.0, ph_ref[0])
    pbuf[HALO:HALO + tb, :] = p_ref[0]
    tpos = i * tb + lax.broadcasted_iota(jnp.int32, (tb, 1), 0)
    d_parts, y_parts = [], []
    for g, win in enumerate(POOL_WINDOWS):
        cs = slice(g * GROUP, (g + 1) * GROUP)
        cur = pbuf[HALO:HALO + tb, cs]
        s = cur
        for j in range(1, win):
            s = s + pbuf[HALO - j:HALO - j + tb, cs]
        cnt = jnp.minimum(tpos + 1, win).astype(F32)
        d_g = s / cnt - cur
        d_parts.append(d_g)
        y_parts.append(_dot(d_g.astype(BF16), poolw_ref[g]))
    r["d"] = d_parts
    y = jnp.concatenate(y_parts, axis=1) + poolb_ref[...]
    r["y"] = y
    mp = y * pscale_ref[...]
    gp = gp_ref[0]
    sgp = _sigmoid(gp)
    r["mp"], r["gp"], r["sgp"] = mp, gp, sgp
    ua = mp * (gp * sgp)

    ah, bh = cah_ref[0], cbh_ref[0]
    ubuf[0:HALO, :] = jnp.where(first, 0.0, ah * _sigmoid(bh))
    a, b = ca_ref[0], cb_ref[0]
    sb = _sigmoid(b)
    ubuf[HALO:HALO + tb, :] = a * sb
    r["a"], r["sb"] = a, sb
    cv = jnp.zeros((tb, UNIT), F32) + convb_ref[...]
    off = HALO - (CONV_K - 1)
    u_at = _shifted_reader(ubuf, rbuf, tb)
    for k in range(CONV_K):
        cv = cv + convw_ref[k:k + 1, :] * u_at(off + k)
    mu = jnp.mean(cv, axis=-1, keepdims=True)
    cc = cv - mu
    rs = lax.rsqrt(jnp.mean(cc * cc, axis=-1, keepdims=True) + LN_EPS)
    nrm = cc * rs
    ln = nrm * lng_ref[...] + lnb_ref[...]
    sln = _sigmoid(ln)
    sc = ln * sln
    gc = gc_ref[0]
    sgc = _sigmoid(gc)
    r["rs"], r["nrm"], r["ln"], r["sln"], r["sc"], r["gc"], r["sgc"] = rs, nrm, ln, sln, sc, gc, sgc
    ub = sc * (gc * sgc)

    o = o_ref[...]
    ga = ga_ref[0]
    sga = _sigmoid(ga)
    r["o"], r["ga"], r["sga"] = o, ga, sga
    uc = o * (ga * sga)

    r["ua"], r["ub"], r["uc"] = ua.astype(BF16), ub.astype(BF16), uc.astype(BF16)
    ya = _dot(r["ua"], wpo_ref[...])
    yb = _dot(r["ub"], wco_ref[...])
    yc = _dot(r["uc"], wao_ref[...])
    g0 = _sigmoid(jnp.concatenate([gm0_ref[0], gm0_ref[1]], axis=1))
    g1 = _sigmoid(jnp.concatenate([gm0_ref[2], gm1_ref[0]], axis=1))
    g2 = _sigmoid(jnp.concatenate([gm1_ref[1], gm1_ref[2]], axis=1))
    r["ya"], r["yb"], r["yc"], r["g0"], r["g1"], r["g2"] = ya, yb, yc, g0, g1, g2
    m = (g0 * ya + g1 * yb + g2 * yc).astype(BF16)
    r["m"] = m
    out = _dot(m, wo_ref[...])
    r2 = lax.rsqrt(jnp.mean(out * out, axis=-1, keepdims=True) + RMS_EPS)
    r["n2"], r["r2"] = out * r2, r2
    return r


def _mix_fwd(proj, o, x, weights):
    t = x.shape[0]
    tb = min(256, t)

    def body(*refs):
        proj_refs, o_ref, x_ref = refs[0:11], refs[11], refs[12]
        w_refs = refs[13:25]
        xn_ref, pbuf, ubuf, rbuf = refs[25:29]
        r = _mix_forward(pl.program_id(0), tb, proj_refs, o_ref, w_refs, pbuf, ubuf, rbuf)
        xn_ref[...] = x_ref[...] + r["n2"] * w_refs[11][...]

    return pl.pallas_call(
        body, name="mix_fwd",
        grid=(t // tb,),
        in_specs=_mix_in_specs(tb) + [pl.BlockSpec((tb, UNIT), lambda i: (i, 0)),
                                      pl.BlockSpec((tb, D_MODEL), lambda i: (i, 0))] + _mix_weight_specs(),
        out_specs=pl.BlockSpec((tb, D_MODEL), lambda i: (i, 0)),
        out_shape=S((t, D_MODEL), F32),
        scratch_shapes=[pltpu.VMEM((HALO + tb, UNIT), F32), pltpu.VMEM((HALO + tb, UNIT), F32),
                        pltpu.VMEM((8, HALO + tb - 8, UNIT), F32)],
        compiler_params=_params(1),
    )(*([proj] * 11), o, x, *weights)


def _mix_bwd(proj, o, x, dxn, weights):
    t = x.shape[0]
    tb = min(256, t)
    nt = t // tb

    def body(*refs):
        proj_refs, o_ref, x_ref, dxn_ref = refs[0:11], refs[11], refs[12], refs[13]
        w_refs = refs[14:26]
        dg_ref, dd_ref, dcv_ref, do_ref, dwo_hbm, dwp_hbm, dwc_hbm, dwa_hbm, dpw_ref, dvec_ref = refs[26:36]
        pbuf, ubuf, dwo_ref, dwp_ref, dwc_ref, dwa_ref = refs[36:42]
        (poolw_ref, _, pscale_ref, wpo_ref, _, _, lng_ref, _, wco_ref, wao_ref, wo_ref, gpost_ref) = w_refs
        i = pl.program_id(0)

        @pl.when(i == 0)
        def _():
            for ref in (dwo_ref, dwp_ref, dwc_ref, dwa_ref, dpw_ref, dvec_ref):
                ref[...] = jnp.zeros_like(ref)

        r = _mix_forward(i, tb, proj_refs, o_ref, w_refs, pbuf, ubuf)

        def colsum(v):
            return jnp.sum(v, axis=0, keepdims=True)

        dxn = dxn_ref[...]
        n2 = r["n2"]
        dvec_ref[0:1, :] += colsum(dxn * n2)
        dn2 = dxn * gpost_ref[...]
        dout = (r["r2"] * (dn2 - n2 * jnp.mean(dn2 * n2, axis=-1, keepdims=True))).astype(BF16)
        dm = _dot_nt(dout, wo_ref[...])
        dwo_ref[...] += _dot_tn(r["m"], dout)

        g0, g1, g2 = r["g0"], r["g1"], r["g2"]
        dgm = [dm * r["ya"] * g0 * (1.0 - g0), dm * r["yb"] * g1 * (1.0 - g1), dm * r["yc"] * g2 * (1.0 - g2)]
        for bidx in range(N_BRANCH):
            dg_ref[3 + 2 * bidx] = dgm[bidx][:, 0:UNIT]
            dg_ref[4 + 2 * bidx] = dgm[bidx][:, UNIT:2 * UNIT]
        dya = (dm * g0).astype(BF16)
        dyb = (dm * g1).astype(BF16)
        dyc = (dm * g2).astype(BF16)
        dua = _dot_nt(dya, wpo_ref[...])
        dub = _dot_nt(dyb, wco_ref[...])
        duc = _dot_nt(dyc, wao_ref[...])
        dwp_ref[...] += _dot_tn(r["ua"], dya)
        dwc_ref[...] += _dot_tn(r["ub"], dyb)
        dwa_ref[...] += _dot_tn(r["uc"], dyc)

        gp, sgp = r["gp"], r["sgp"]
        dmp = dua * (gp * sgp)
        dg_ref[0] = dua * r["mp"] * _dsilu(gp, sgp)
        dvec_ref[2:3, 0:UNIT] += colsum(dmp * r["y"])
        dy = dmp * pscale_ref[...]
        dvec_ref[1:2, 0:UNIT] += colsum(dy)
        dd_parts = []
        for g in range(len(POOL_WINDOWS)):
            dy_g = dy[:, g * GROUP:(g + 1) * GROUP].astype(BF16)
            dd_parts.append(_dot_nt(dy_g, poolw_ref[g]))
            dpw_ref[g] += _dot_tn(r["d"][g].astype(BF16), dy_g)
        dd_ref[...] = jnp.concatenate(dd_parts, axis=1)

        gc, sgc = r["gc"], r["sgc"]
        dsc = dub * (gc * sgc)
        dg_ref[1] = dub * r["sc"] * _dsilu(gc, sgc)
        dln = dsc * _dsilu(r["ln"], r["sln"])
        nrm = r["nrm"]
        dvec_ref[4:5, 0:UNIT] += colsum(dln * nrm)
        dvec_ref[5:6, 0:UNIT] += colsum(dln)
        dnrm = dln * lng_ref[...]
        dcv = r["rs"] * (dnrm - jnp.mean(dnrm, axis=-1, keepdims=True)
                         - nrm * jnp.mean(dnrm * nrm, axis=-1, keepdims=True))
        dvec_ref[3:4, 0:UNIT] += colsum(dcv)
        dcv_ref[...] = dcv

        ga, sga = r["ga"], r["sga"]
        do_ref[...] = duc * (ga * sga)
        dg_ref[2] = duc * r["o"] * _dsilu(ga, sga)

        @pl.when(i == nt - 1)
        def _():
            for acc, hbm in ((dwo_ref, dwo_hbm), (dwp_ref, dwp_hbm), (dwc_ref, dwc_hbm), (dwa_ref, dwa_hbm)):
                pltpu.sync_copy(acc, hbm)

    def acc_spec(shape):
        n = len(shape)
        return pl.BlockSpec(shape, lambda i: (0,) * n)

    tok = lambda w: pl.BlockSpec((tb, w), lambda i: (i, 0))
    any_spec = pl.BlockSpec(memory_space=pl.ANY)
    w_shapes = [(D_MODEL, D_MODEL), (UNIT, D_MODEL), (UNIT, D_MODEL), (UNIT, D_MODEL)]
    return pl.pallas_call(
        body, name="mix_bwd",
        grid=(nt,),
        in_specs=_mix_in_specs(tb) + [tok(UNIT), tok(D_MODEL), tok(D_MODEL)] + _mix_weight_specs(),
        out_specs=[pl.BlockSpec((9, tb, UNIT), lambda i: (0, i, 0)), tok(UNIT), tok(UNIT), tok(UNIT)]
        + [any_spec] * 4 + [acc_spec((4, GROUP, GROUP)), acc_spec((8, D_MODEL))],
        out_shape=[S((9, t, UNIT), F32), S((t, UNIT), F32), S((t, UNIT), F32), S((t, UNIT), F32)]
        + [S(s, F32) for s in w_shapes] + [S((4, GROUP, GROUP), F32), S((8, D_MODEL), F32)],
        scratch_shapes=[pltpu.VMEM((HALO + tb, UNIT), F32), pltpu.VMEM((HALO + tb, UNIT), F32)]
        + [pltpu.VMEM(s, F32) for s in w_shapes],
        compiler_params=_params(1),
    )(*([proj] * 11), o, x, dxn, *weights)


def _halo_bwd(proj, dcv, dd, conv_w):
    t = dcv.shape[0]
    tb = min(256, t)
    hb = tb // HALO
    n_halo_blocks = t // HALO
    nt = t // tb

    def body(ca_ref, cah_ref, cb_ref, cbh_ref, dcv_ref, dcvh_ref, dd_ref, ddh_ref, cw_ref,
             dpre_ref, dcw_ref, ubuf, gbuf, nbuf, ru_buf, rg_buf, acc_ref):
        i = pl.program_id(0)
        first = i == 0
        last = i == nt - 1

        @pl.when(first)
        def _():
            acc_ref[...] = jnp.zeros_like(acc_ref)

        a, b = ca_ref[0], cb_ref[0]
        sb = _sigmoid(b)
        ubuf[0:HALO, :] = jnp.where(first, 0.0, cah_ref[0] * _sigmoid(cbh_ref[0]))
        ubuf[HALO:HALO + tb, :] = a * sb
        dcv_main = dcv_ref[...]
        gbuf[0:tb, :] = dcv_main
        gbuf[tb:tb + HALO, :] = jnp.where(last, 0.0, dcvh_ref[...])

        off = HALO - (CONV_K - 1)
        u_at = _shifted_reader(ubuf, ru_buf, tb)
        g_at = _shifted_reader(gbuf, rg_buf, tb)
        du = jnp.zeros((tb, UNIT), F32)
        for k in range(CONV_K):
            du = du + cw_ref[k:k + 1, :] * g_at(CONV_K - 1 - k)
            acc_ref[k] += jnp.sum((dcv_main * u_at(off + k)).reshape(tb // 8, 8, UNIT), axis=0)
        dpre_ref[1] = du * sb
        dpre_ref[2] = du * a * sb * (1.0 - sb)

        @pl.when(last)
        def _():
            dcw_ref[...] = jnp.sum(acc_ref[...], axis=1)

        tpos = i * tb + lax.broadcasted_iota(jnp.int32, (tb + HALO, 1), 0)
        dd_main = dd_ref[...]
        dd_ext = jnp.concatenate([dd_main, jnp.where(last, 0.0, ddh_ref[...])], axis=0)
        dp_parts = []
        for g, win in enumerate(POOL_WINDOWS):
            cs = slice(g * GROUP, (g + 1) * GROUP)
            cnt = jnp.minimum(tpos + 1, win).astype(F32)
            nbuf[:, cs] = dd_ext[:, cs] / cnt
        for g, win in enumerate(POOL_WINDOWS):
            cs = slice(g * GROUP, (g + 1) * GROUP)
            s = nbuf[0:tb, cs]
            for j in range(1, win):
                s = s + nbuf[j:j + tb, cs]
            dp_parts.append(s - dd_main[:, cs])
        dpre_ref[0] = jnp.concatenate(dp_parts, axis=1)

    def unit(u):
        return pl.BlockSpec((1, tb, UNIT), lambda i: (u, i, 0))

    def past(u):
        return pl.BlockSpec((1, HALO, UNIT), lambda i: (u, jnp.maximum(i * hb - 1, 0), 0))

    tok = pl.BlockSpec((tb, UNIT), lambda i: (i, 0))
    future = pl.BlockSpec((HALO, UNIT), lambda i: (jnp.minimum((i + 1) * hb, n_halo_blocks - 1), 0))
    return pl.pallas_call(
        body, name="halo_bwd",
        grid=(nt,),
        in_specs=[unit(U_CA), past(U_CA), unit(U_CB), past(U_CB), tok, future, tok, future,
                  _const_spec((CONV_K_PAD, UNIT))],
        out_specs=[pl.BlockSpec((3, tb, UNIT), lambda i: (0, i, 0)),
                   pl.BlockSpec((CONV_K_PAD, UNIT), lambda i: (0, 0))],
        out_shape=[S((3, t, UNIT), F32), S((CONV_K_PAD, UNIT), F32)],
        scratch_shapes=[pltpu.VMEM((HALO + tb, UNIT), F32)] * 3
        + [pltpu.VMEM((8, HALO + tb - 8, UNIT), F32)] * 2 + [pltpu.VMEM((CONV_K_PAD, 8, UNIT), F32)],
        compiler_params=_params(1),
    )(proj, proj, proj, proj, dcv, dcv, dd, dd, conv_w)


def _loss_head(y, target):
    t = y.shape[0]
    tb = min(512, t)

    def body(y_ref, t_ref, loss_ref, dy_ref):
        @pl.when(pl.program_id(0) == 0)
        def _():
            loss_ref[...] = jnp.zeros_like(loss_ref)

        err = y_ref[...] - t_ref[...]
        dy_ref[...] = err * (1.0 / D_MODEL)
        part = 0.5 * jnp.sum(jnp.mean(err * err, axis=-1, keepdims=True), axis=0, keepdims=True)
        r8 = lax.broadcasted_iota(jnp.int32, (8, LANES), 0)
        c8 = lax.broadcasted_iota(jnp.int32, (8, LANES), 1)
        loss_ref[...] += jnp.where((r8 == 0) & (c8 == 0), part, 0.0)

    return pl.pallas_call(
        body, name="loss_head",
        grid=(t // tb,),
        in_specs=[pl.BlockSpec((tb, D_MODEL), lambda i: (i, 0))] * 2,
        out_specs=[pl.BlockSpec((8, LANES), lambda i: (0, 0)), pl.BlockSpec((tb, D_MODEL), lambda i: (i, 0))],
        out_shape=[S((8, LANES), F32), S((t, D_MODEL), F32)],
        compiler_params=_params(1),
    )(y, target)


def _adamw(name, parts, w, m, v):
    n, rows, cols = parts.shape
    rb = rows
    while rb * cols * 4 * (n + 7) * 2 > 24 * 1024 * 1024 and rb % 16 == 0:
        rb //= 2

    def body(p_ref, w_ref, m_ref, v_ref, g_ref, d_ref, nm_ref, nv_ref):
        g = p_ref[0]
        for k in range(1, n):
            g = g + p_ref[k]
        nm = ADAM_B1 * m_ref[...] + (1.0 - ADAM_B1) * g
        nv = ADAM_B2 * v_ref[...] + (1.0 - ADAM_B2) * (g * g)
        m_hat = nm / (1.0 - ADAM_B1 ** ADAM_STEP)
        v_hat = nv / (1.0 - ADAM_B2 ** ADAM_STEP)
        g_ref[...] = g
        d_ref[...] = -ADAM_LR * (m_hat / (jnp.sqrt(v_hat) + ADAM_EPS) + ADAM_WD * w_ref[...])
        nm_ref[...] = nm
        nv_ref[...] = nv

    blk = pl.BlockSpec((rb, cols), lambda i: (i, 0))
    return pl.pallas_call(
        body, name=name,
        grid=(rows // rb,),
        in_specs=[pl.BlockSpec((n, rb, cols), lambda i: (0, i, 0)), blk, blk, blk],
        out_specs=[blk] * 4,
        out_shape=[S((rows, cols), F32)] * 4,
        compiler_params=_params(1),
    )(parts, w, m, v)


def _axis_slice(ref, axis, block, size):
    idx = [slice(None)] * len(ref.shape)
    idx[axis] = pl.ds(pl.multiple_of(block * size, size), size)
    return ref.at[tuple(idx)]


def _axis_index(ref, axis, index):
    return ref.at[tuple([slice(None)] * axis + [index])]


COPY_PIECE_BYTES = 2 * 1024 * 1024


def _row_pieces(shape):
    rows = shape[-2]
    size = 4 * rows * shape[-1]
    for d in shape[:-2]:
        size *= d
    n = 1
    while size // n > COPY_PIECE_BYTES and rows % (2 * n * 8) == 0:
        n *= 2
    return [(k * (rows // n), rows // n) for k in range(n)]


def _rows(ref, r0, rows):
    idx = [slice(None)] * len(ref.shape)
    idx[-2] = pl.ds(r0, rows)
    return ref.at[tuple(idx)]


def _staged_local_copies(pairs, bufs, in_sems, out_sems):
    ins = [pltpu.make_async_copy(src, buf, in_sems.at[k]) for k, ((src, _), buf) in enumerate(zip(pairs, bufs))]
    for cp in ins:
        cp.start()
    for cp in ins:
        cp.wait()
    outs = [pltpu.make_async_copy(buf, dst, out_sems.at[k]) for k, ((_, dst), buf) in enumerate(zip(pairs, bufs))]
    for cp in outs:
        cp.start()
    return outs


def _run_copies(local, sends, recvs):
    for cp in local + sends:
        cp.start()
    for cp in recvs:
        cp.wait_recv()
    for cp in sends:
        cp.wait_send()
    for cp in local:
        cp.wait()


def _gather_weights(shards, axes):
    n = len(shards)
    fulls = [S(tuple(4 * d if k == ax else d for k, d in enumerate(s.shape)), s.dtype) for s, ax in zip(shards, axes)]

    def body(*refs):
        src, dst = refs[0:n], refs[n:2 * n]
        send_sems, recv_sems, pass_send_sems, pass_recv_sems, in_sems, out_sems = refs[2 * n:2 * n + 6]
        bufs = refs[2 * n + 6:3 * n + 6]
        x, y, c = lax.axis_index("x"), lax.axis_index("y"), lax.axis_index("c")
        chips = [(1 - x, y), (x, 1 - y), (1 - x, 1 - y)]

        def place(a, chip, layer=None):
            ref = _axis_slice(dst[a], axes[a], 2 * chip[0] + chip[1], src[a].shape[axes[a]])
            return ref if layer is None else ref.at[pl.ds(layer, 1)]

        def over_ici(k, chip_of_block, to):
            a, j, r0, rows = pieces[k]
            return pltpu.make_async_remote_copy(
                src_ref=_rows(src[a].at[pl.ds(c, 1)], r0, rows), dst_ref=_rows(place(a, chip_of_block, c), r0, rows),
                send_sem=send_sems.at[k], recv_sem=recv_sems.at[k], device_id=(to[0], to[1], c), device_id_type=MESH)

        def to_sibling(k, chip_of_block, layer):
            a, j, r0, rows = pieces[k]
            block = _rows(place(a, chip_of_block, layer), r0, rows)
            return pltpu.make_async_remote_copy(
                src_ref=block, dst_ref=block, send_sem=pass_send_sems.at[k], recv_sem=pass_recv_sems.at[k],
                device_id=(x, y, 1 - c), device_id_type=MESH)

        first = [over_ici(k, (x, y), chips[j]) for k, (a, j, r0, rows) in enumerate(pieces)]
        for cp in first:
            cp.start()
        local = _staged_local_copies([(src[a], place(a, (x, y))) for a in range(n)], bufs, in_sems, out_sems)
        passed = []
        for k, (a, j, r0, rows) in enumerate(pieces):
            over_ici(k, chips[j], chips[j]).wait_recv()
            passed.append(to_sibling(k, chips[j], c))
            passed[-1].start()
        for k, (a, j, r0, rows) in enumerate(pieces):
            to_sibling(k, chips[j], 1 - c).wait_recv()
        for cp in first + passed:
            cp.wait_send()
        for cp in local:
            cp.wait()

    per_array = [_row_pieces((1,) + shards[a].shape[1:]) for a in range(n)]
    pieces = [(a, j, r0, rows) for a in range(n) for r0, rows in per_array[a] for j in range(3)]
    any_spec = pl.BlockSpec(memory_space=pl.ANY)
    return pl.pallas_call(
        body, name="gather_weights",
        in_specs=[any_spec] * n, out_specs=[any_spec] * n, out_shape=fulls,
        scratch_shapes=[pltpu.SemaphoreType.DMA((len(pieces),))] * 4 + [pltpu.SemaphoreType.DMA((n,))] * 2
        + [pltpu.VMEM(s.shape, s.dtype) for s in shards],
        compiler_params=_params(0),
    )(*shards)


def _swap_halves(partials, half_axes):
    n = len(partials)

    def half_shape(a):
        return tuple(d // 2 if k == half_axes[a] else d for k, d in enumerate(partials[a].shape))

    pieces = [(a, r0, rows) for a in range(n) for r0, rows in _row_pieces(half_shape(a))]

    def body(*refs):
        src, dst = refs[0:n], refs[n:2 * n]
        send_sems, recv_sems = refs[2 * n:2 * n + 2]
        x, y, c = lax.axis_index("x"), lax.axis_index("y"), lax.axis_index("c")

        def remote(k, h):
            a, r0, rows = pieces[k]
            half = _axis_slice(src[a], half_axes[a], h, src[a].shape[half_axes[a]] // 2)
            return pltpu.make_async_remote_copy(
                src_ref=_rows(half, r0, rows), dst_ref=_rows(dst[a], r0, rows),
                send_sem=send_sems.at[k], recv_sem=recv_sems.at[k], device_id=(x, y, 1 - c), device_id_type=MESH)

        sends = [remote(k, 1 - c) for k in range(len(pieces))]
        _run_copies([], sends, [remote(k, c) for k in range(len(pieces))])

    any_spec = pl.BlockSpec(memory_space=pl.ANY)
    return pl.pallas_call(
        body, name="swap_halves",
        in_specs=[any_spec] * n, out_specs=[any_spec] * n,
        out_shape=[S(half_shape(a), partials[a].dtype) for a in range(n)],
        scratch_shapes=[pltpu.SemaphoreType.DMA((len(pieces),))] * 2,
    )(*partials)


def _add_own_layer(name, mine, theirs, layer):
    _, rows, cols = mine.shape
    rb = rows
    while rb * cols * 4 * 3 * 2 > 24 * 1024 * 1024 and rb % 16 == 0:
        rb //= 2

    def body(layer_ref, m_ref, t_ref, o_ref):
        o_ref[...] = m_ref[...] + t_ref[...]

    return pl.pallas_call(
        body, name=name,
        grid_spec=pltpu.PrefetchScalarGridSpec(
            num_scalar_prefetch=1, grid=(rows // rb,),
            in_specs=[pl.BlockSpec((1, rb, cols), lambda i, l: (l[0], i, 0)),
                      pl.BlockSpec((1, rb, cols), lambda i, l: (0, i, 0))],
            out_specs=pl.BlockSpec((1, rb, cols), lambda i, l: (0, i, 0))),
        out_shape=S((1, rows, cols), F32),
        compiler_params=_params(1),
    )(layer, mine, theirs)


def _send_to_owners(chip_halves, shard_axes, packed):
    n = len(chip_halves)

    def owned_shape(a):
        return tuple(d // 4 if k == shard_axes[a] else d for k, d in enumerate(chip_halves[a].shape))

    flips = [(dx, dy, dc) for dx in (0, 1) for dy in (0, 1) for dc in (0, 1)][1:]

    def body(*refs):
        src, psrc, dst, pdst = refs[0:n], refs[n], refs[n + 1:2 * n + 1], refs[2 * n + 1]
        send_sems, recv_sems, local_sems, psend_sems, precv_sems = refs[2 * n + 2:2 * n + 7]
        x, y, c = lax.axis_index("x"), lax.axis_index("y"), lax.axis_index("c")
        my_chip = 2 * x + y
        chips = [(1 - x, y), (x, 1 - y), (1 - x, 1 - y)]

        def owned(a, chip):
            return _axis_slice(src[a], shard_axes[a], 2 * chip[0] + chip[1], src[a].shape[shard_axes[a]] // 4)

        def remote(a, j, to, from_chip):
            return pltpu.make_async_remote_copy(
                src_ref=owned(a, to), dst_ref=dst[a].at[from_chip], send_sem=send_sems.at[a, j],
                recv_sem=recv_sems.at[a, j], device_id=(to[0], to[1], c), device_id_type=MESH)

        def peer(f):
            return (1 - x if f[0] else x, 1 - y if f[1] else y, 1 - c if f[2] else c)

        def premote(j, to, from_dev):
            return pltpu.make_async_remote_copy(
                src_ref=psrc, dst_ref=pdst.at[from_dev], send_sem=psend_sems.at[j], recv_sem=precv_sems.at[j],
                device_id=to, device_id_type=MESH)

        local = [pltpu.make_async_copy(_rows(owned(a, (x, y)), r0, rows), _rows(dst[a].at[my_chip], r0, rows),
                                       local_sems.at[k]) for k, (a, r0, rows) in enumerate(own_pieces)]
        local.append(pltpu.make_async_copy(psrc, pdst.at[4 * x + 2 * y + c], local_sems.at[len(own_pieces)]))
        sends = [remote(a, j, chip, my_chip) for a in range(n) for j, chip in enumerate(chips)]
        sends += [premote(j, peer(f), 4 * x + 2 * y + c) for j, f in enumerate(flips)]
        recvs = [remote(a, j, (x, y), 2 * chip[0] + chip[1]) for a in range(n) for j, chip in enumerate(chips)]
        for j, f in enumerate(flips):
            px, py, pc = peer(f)
            recvs.append(premote(j, (x, y, c), 4 * px + 2 * py + pc))
        _run_copies(local, sends, recvs)

    own_pieces = [(a, r0, rows) for a in range(n) for r0, rows in _row_pieces(owned_shape(a))]
    any_spec = pl.BlockSpec(memory_space=pl.ANY)
    return pl.pallas_call(
        body, name="send_to_owners",
        in_specs=[any_spec] * (n + 1), out_specs=[any_spec] * (n + 1),
        out_shape=[S((4,) + owned_shape(a), chip_halves[a].dtype) for a in range(n)]
        + [S((N_DEV,) + packed.shape, packed.dtype)],
        scratch_shapes=[pltpu.SemaphoreType.DMA((n, 3)), pltpu.SemaphoreType.DMA((n, 3)),
                        pltpu.SemaphoreType.DMA((len(own_pieces) + 1,)), pltpu.SemaphoreType.DMA((7,)),
                        pltpu.SemaphoreType.DMA((7,))],
    )(*chip_halves, packed)


def _join_halves(reduced, half_axes):
    n = len(reduced)

    def joined_shape(a):
        s, ax = reduced[a].shape, half_axes[a]
        return s[:ax] + (2,) + s[ax:]

    pieces = [(a, r0, rows) for a in range(n) for r0, rows in _row_pieces(reduced[a].shape)]

    def body(*refs):
        src, dst = refs[0:n], refs[n:2 * n]
        send_sems, recv_sems, in_sems, out_sems = refs[2 * n:2 * n + 4]
        bufs = refs[2 * n + 4:3 * n + 4]
        x, y, c = lax.axis_index("x"), lax.axis_index("y"), lax.axis_index("c")

        def remote(k, h):
            a, r0, rows = pieces[k]
            return pltpu.make_async_remote_copy(
                src_ref=_rows(src[a], r0, rows), dst_ref=_rows(_axis_index(dst[a], half_axes[a], h), r0, rows),
                send_sem=send_sems.at[k], recv_sem=recv_sems.at[k], device_id=(x, y, 1 - c), device_id_type=MESH)

        sends = [remote(k, c) for k in range(len(pieces))]
        for cp in sends:
            cp.start()
        own = [(src[a], _axis_index(dst[a], half_axes[a], c)) for a in range(n)]
        local = _staged_local_copies(own, bufs, in_sems, out_sems)
        for k in range(len(pieces)):
            remote(k, 1 - c).wait_recv()
        for cp in sends:
            cp.wait_send()
        for cp in local:
            cp.wait()

    any_spec = pl.BlockSpec(memory_space=pl.ANY)
    return pl.pallas_call(
        body, name="join_halves",
        in_specs=[any_spec] * n, out_specs=[any_spec] * n,
        out_shape=[S(joined_shape(a), reduced[a].dtype) for a in range(n)],
        scratch_shapes=[pltpu.SemaphoreType.DMA((len(pieces),))] * 2 + [pltpu.SemaphoreType.DMA((n,))] * 2
        + [pltpu.VMEM(r.shape, r.dtype) for r in reduced],
        compiler_params=_params(0),
    )(*reduced)


def _sum_slots(name, parts):
    n = parts.shape[0]
    shape = parts.shape[1:]
    flat = parts.reshape((n, -1, shape[-1]))
    rows, cols = flat.shape[1:]
    rb = rows
    while rb * cols * 4 * (n + 1) * 2 > 24 * 1024 * 1024 and rb % 16 == 0:
        rb //= 2

    def body(p_ref, o_ref):
        acc = p_ref[0]
        for k in range(1, n):
            acc = acc + p_ref[k]
        o_ref[...] = acc

    out = pl.pallas_call(
        body, name=name,
        grid=(rows // rb,),
        in_specs=[pl.BlockSpec((n, rb, cols), lambda i: (0, i, 0))],
        out_specs=pl.BlockSpec((rb, cols), lambda i: (i, 0)),
        out_shape=S((rows, cols), F32),
        compiler_params=_params(1),
    )(flat)
    return out.reshape(shape)


SMALL_NAMES = ("norm_pre", "pool_w", "pool_b", "pool_scale", "conv_b", "conv_ln_g", "conv_ln_b", "norm_post")
BIG_NAMES = ("w_in", "w_pool_out", "conv_w", "w_conv_out", "w_attn_out", "w_o")
SHARD_AXIS = {"w_in": 2, "w_pool_out": 2, "conv_w": 2, "w_conv_out": 2, "w_attn_out": 2, "w_o": 1}
WEIGHT_ORDER = ("norm_pre", "w_in", "pool_w", "pool_b", "pool_scale", "w_pool_out", "conv_w", "conv_b",
                "conv_ln_g", "conv_ln_b", "w_conv_out", "w_attn_out", "w_o", "norm_post")


def _pack_small(parts):
    return jnp.concatenate([parts[n].reshape(-1, LANES) for n in SMALL_NAMES], axis=0)


def _unpack_small(packed, shapes):
    out, r0 = {}, 0
    for n in SMALL_NAMES:
        size = 1
        for d in shapes[n]:
            size *= d
        rows = size // LANES
        out[n] = packed[r0:r0 + rows].reshape(shapes[n])
        r0 += rows
    return out


def _layer_weights(full, small, l):
    row = lambda a: a[l][None, :]
    return (small["pool_w"][l].astype(BF16), small["pool_b"][l].reshape(1, UNIT), row(small["pool_scale"]),
            full["w_pool_out"][l], full["conv_w"][l], row(small["conv_b"]), row(small["conv_ln_g"]),
            row(small["conv_ln_b"]), full["w_conv_out"][l], full["w_attn_out"][l], full["w_o"][l],
            row(small["norm_post"]))


def _forward_backward(x, target, full, small):
    depth = full["w_in"].shape[0]
    acts = []
    for l in range(depth):
        g_pre = small["norm_pre"][l][None, :]
        proj = _inproj_fwd(x, g_pre, full["w_in"][l])
        o, bsum, kfirst = _attn_fwd(proj)
        weights = _layer_weights(full, small, l)
        x_next = _mix_fwd(proj, o, x, weights)
        acts.append((x, proj, o, bsum, kfirst, weights, g_pre))
        x = x_next
    loss_tile, dx = _loss_head(x, target)

    big = {n: [None] * depth for n in BIG_NAMES}
    sm = {n: [None] * depth for n in SMALL_NAMES}
    for l in reversed(range(depth)):
        x_in, proj, o, bsum, kfirst, weights, g_pre = acts[l]
        dgates, dd, dcv, d_o, dwo, dwp, dwc, dwa, dpw, dvec = _mix_bwd(proj, o, x_in, dx, weights)
        dpre, dcw = _halo_bwd(proj, dcv, dd, full["conv_w"][l])
        dq, dk, dv = _attn_bwd(proj, d_o, bsum, kfirst)
        dproj_parts = (dpre, dgates, dq, dk, dv)
        dx, dg_pre = _inproj_bwd_x(x_in, g_pre, full["w_in"][l], dproj_parts, dx)
        big["w_in"][l] = _inproj_bwd_w(x_in, g_pre, dproj_parts)
        big["w_pool_out"][l], big["w_conv_out"][l], big["w_attn_out"][l], big["w_o"][l] = dwp, dwc, dwa, dwo
        big["conv_w"][l] = dcw
        sm["norm_pre"][l] = dg_pre[0]
        sm["pool_w"][l] = dpw
        sm["norm_post"][l] = dvec[0]
        sm["pool_b"][l] = dvec[1, 0:UNIT].reshape(4, GROUP)
        sm["pool_scale"][l] = dvec[2, 0:UNIT]
        sm["conv_b"][l] = dvec[3, 0:UNIT]
        sm["conv_ln_g"][l] = dvec[4, 0:UNIT]
        sm["conv_ln_b"][l] = dvec[5, 0:UNIT]
    big = {n: jnp.stack(v) for n, v in big.items()}
    sm = {n: jnp.stack(v) for n, v in sm.items()}
    return loss_tile, dx, big, sm


def kernel(x, norm_pre, w_in, pool_w, pool_b, pool_scale, w_pool_out, conv_w, conv_b, conv_ln_g, conv_ln_b, w_conv_out, w_attn_out, w_o, norm_post, loss_target, m_norm_pre, m_w_in, m_pool_w, m_pool_b, m_pool_scale, m_w_pool_out, m_conv_w, m_conv_b, m_conv_ln_g, m_conv_ln_b, m_w_conv_out, m_w_attn_out, m_w_o, m_norm_post, v_norm_pre, v_w_in, v_pool_w, v_pool_b, v_pool_scale, v_w_pool_out, v_conv_w, v_conv_b, v_conv_ln_g, v_conv_ln_b, v_w_conv_out, v_w_attn_out, v_w_o, v_norm_post):
    w = dict(norm_pre=norm_pre, w_in=w_in, pool_w=pool_w, pool_b=pool_b, pool_scale=pool_scale,
             w_pool_out=w_pool_out, conv_w=conv_w, conv_b=conv_b, conv_ln_g=conv_ln_g, conv_ln_b=conv_ln_b,
             w_conv_out=w_conv_out, w_attn_out=w_attn_out, w_o=w_o, norm_post=norm_post)
    m = dict(norm_pre=m_norm_pre, w_in=m_w_in, pool_w=m_pool_w, pool_b=m_pool_b, pool_scale=m_pool_scale,
             w_pool_out=m_w_pool_out, conv_w=m_conv_w, conv_b=m_conv_b, conv_ln_g=m_conv_ln_g,
             conv_ln_b=m_conv_ln_b, w_conv_out=m_w_conv_out, w_attn_out=m_w_attn_out, w_o=m_w_o,
             norm_post=m_norm_post)
    v = dict(norm_pre=v_norm_pre, w_in=v_w_in, pool_w=v_pool_w, pool_b=v_pool_b, pool_scale=v_pool_scale,
             w_pool_out=v_w_pool_out, conv_w=v_conv_w, conv_b=v_conv_b, conv_ln_g=v_conv_ln_g,
             conv_ln_b=v_conv_ln_b, w_conv_out=v_w_conv_out, w_attn_out=v_w_attn_out, w_o=v_w_o,
             norm_post=v_norm_post)
    pad_taps = lambda a: jnp.pad(a, ((0, 0), (0, CONV_K_PAD - CONV_K), (0, 0)))

    shards = [pad_taps(w[n]) if n == "conv_w" else w[n].astype(BF16) for n in BIG_NAMES]
    full = dict(zip(BIG_NAMES, _gather_weights(shards, [SHARD_AXIS[n] for n in BIG_NAMES])))
    small = {n: w[n] for n in SMALL_NAMES}

    loss_tile, dx, big, sm = _forward_backward(x[0], loss_target[0], full, small)

    packed = jnp.concatenate([_pack_small(sm), loss_tile], axis=0)
    layer_axis = [0] * len(BIG_NAMES)
    theirs = _swap_halves([big[n] for n in BIG_NAMES], layer_axis)
    my_layer = lax.axis_index("c").astype(jnp.int32).reshape(1)
    chip_sums = [_add_own_layer("sum_cores_" + n, big[n], t, my_layer) for n, t in zip(BIG_NAMES, theirs)]
    exchanged = _send_to_owners(chip_sums, [SHARD_AXIS[n] for n in BIG_NAMES], packed)
    reduced = [_sum_slots("sum_chips_" + n, p) for n, p in zip(BIG_NAMES, exchanged[:-1])]
    joined = _join_halves(reduced, layer_axis)
    out = {}
    for n, g in zip(BIG_NAMES, joined):
        wn, mn, vn = (pad_taps(a[n]) if n == "conv_w" else a[n] for a in (w, m, v))
        shape = wn.shape
        flat = lambda a: a.reshape((-1, shape[-1]))
        res = _adamw("adamw_" + n, g.reshape((1, -1, shape[-1])), flat(wn), flat(mn), flat(vn))
        res = [r.reshape(shape) for r in res]
        out[n] = [r[:, :CONV_K] for r in res] if n == "conv_w" else res
    parts = exchanged[-1]
    n_small_rows = parts.shape[1] - 8
    zeros_tile = jnp.zeros((8, LANES), F32)
    packs = [jnp.concatenate([_pack_small(a), zeros_tile], axis=0) for a in (w, m, v)]
    res = _adamw("adamw_small", parts, *packs)
    loss = res[0][n_small_rows, 0]
    shapes = {n: w[n].shape for n in SMALL_NAMES}
    unpacked = [_unpack_small(r[:n_small_rows], shapes) for r in res]
    for n in SMALL_NAMES:
        out[n] = [u[n] for u in unpacked]

    grads = [out[n][0] for n in WEIGHT_ORDER]
    deltas = [out[n][1] for n in WEIGHT_ORDER]
    new_m = [out[n][2] for n in WEIGHT_ORDER]
    new_v = [out[n][3] for n in WEIGHT_ORDER]
    return (loss, dx[None], *grads, *deltas, *new_m, *new_v)
```

```python
import jax
import jax.numpy as jnp
from jax import lax
from jax.experimental import pallas as pl
from jax.experimental.pallas import tpu as pltpu

F32 = jnp.float32
BF16 = jnp.bfloat16

D_MODEL = 1024
UNIT = 512
N_UNITS = 15
IN_WIDTH = UNIT * N_UNITS
N_HEADS = 8
HEAD_DIM = 64
HEADS_PER_BLOCK = 2
N_HEAD_BLOCKS = N_HEADS // HEADS_PER_BLOCK
LANES = 128
CONV_K = 31
CONV_K_PAD = 32
HALO = 32
POOL_WINDOWS = (2, 4, 8, 16)
GROUP = 128
N_BRANCH = 3
RMS_EPS = 1e-6
LN_EPS = 1e-5
ATTN_SCALE = 0.125
EXP_ZERO_BELOW = -104.0

ADAM_LR = 0.001
ADAM_B1 = 0.9
ADAM_B2 = 0.999
ADAM_EPS = 1e-08
ADAM_WD = 0.01
ADAM_STEP = 10

U_P, U_GP, U_CA, U_CB, U_GC, U_Q, U_K, U_V, U_GA, U_GM = 0, 1, 2, 3, 4, 5, 6, 7, 8, 9

V7X_VMEM_LIMIT = 62 * 1024 * 1024
N_DEV = 8
MESH = pl.DeviceIdType.MESH

S = jax.ShapeDtypeStruct


def _params(n_grid):
    return pltpu.CompilerParams(dimension_semantics=("arbitrary",) * n_grid, vmem_limit_bytes=V7X_VMEM_LIMIT)


def _sigmoid(x):
    return 1.0 / (1.0 + jnp.exp(-x))


def _dsilu(x, s):
    return s * (1.0 + x * (1.0 - s))


def _dot(a, b):
    return jnp.dot(a, b, preferred_element_type=F32)


def _dot_nt(a, b):
    return lax.dot_general(a, b, (((1,), (1,)), ((), ())), preferred_element_type=F32)


def _dot_tn(a, b):
    return lax.dot_general(a, b, (((0,), (0,)), ((), ())), preferred_element_type=F32)


def _split_bf16(x):
    hi = x.astype(BF16)
    lo = (x - hi.astype(F32)).astype(BF16)
    return hi, lo


def _const_spec(shape):
    n = len(shape)
    return pl.BlockSpec(shape, lambda *_: (0,) * n, pipeline_mode=pl.Buffered(1))


def _inproj_fwd(x, g_pre, w_in):
    t = x.shape[0]
    tb = min(1024, t)
    ug = 3
    nb = ug * UNIT

    def body(x_ref, g_ref, w_ref, o_ref, h_ref):
        @pl.when(pl.program_id(1) == 0)
        def _():
            xf = x_ref[...]
            r = lax.rsqrt(jnp.mean(xf * xf, axis=-1, keepdims=True) + RMS_EPS)
            h_ref[...] = (xf * r * g_ref[...]).astype(BF16)

        acc = _dot(h_ref[...], w_ref[...])
        for u in range(ug):
            o_ref[u] = acc[:, u * UNIT:(u + 1) * UNIT]

    return pl.pallas_call(
        body, name="inproj_fwd",
        grid=(t // tb, N_UNITS // ug),
        in_specs=[pl.BlockSpec((tb, D_MODEL), lambda i, j: (i, 0)),
                  pl.BlockSpec((1, D_MODEL), lambda i, j: (0, 0)),
                  pl.BlockSpec((D_MODEL, nb), lambda i, j: (0, j))],
        out_specs=pl.BlockSpec((ug, tb, UNIT), lambda i, j: (j, i, 0)),
        out_shape=S((N_UNITS, t, UNIT), F32),
        scratch_shapes=[pltpu.VMEM((tb, D_MODEL), BF16)],
        compiler_params=_params(2),
    )(x, g_pre, w_in)


def _dproj_specs(tb):
    tok = pl.BlockSpec((tb, UNIT), lambda i: (i, 0))
    return [pl.BlockSpec((3, tb, UNIT), lambda i: (0, i, 0)), pl.BlockSpec((9, tb, UNIT), lambda i: (0, i, 0)),
            tok, tok, tok]


def _dproj_unit(u, dpre_ref, dgates_ref, dq_ref, dk_ref, dv_ref):
    pre = {U_P: 0, U_CA: 1, U_CB: 2}
    gate = {U_GP: 0, U_GC: 1, U_GA: 2}
    if u in pre:
        val = dpre_ref[pre[u]]
    elif u in gate:
        val = dgates_ref[gate[u]]
    elif u >= U_GM:
        val = dgates_ref[3 + u - U_GM]
    else:
        val = {U_Q: dq_ref, U_K: dk_ref, U_V: dv_ref}[u][...]
    return val.astype(BF16)


def _inproj_bwd_x(x, g_pre, w_in, dproj_parts, dxn):
    t = x.shape[0]
    tb = min(256, t)

    def body(x_ref, g_ref, w_ref, dpre_ref, dgates_ref, dq_ref, dk_ref, dv_ref, dxn_ref, dx_ref, dg_ref):
        @pl.when(pl.program_id(0) == 0)
        def _():
            dg_ref[...] = jnp.zeros_like(dg_ref)

        dh = jnp.zeros((tb, D_MODEL), F32)
        for u in range(N_UNITS):
            dh = dh + _dot_nt(_dproj_unit(u, dpre_ref, dgates_ref, dq_ref, dk_ref, dv_ref),
                              w_ref[:, u * UNIT:(u + 1) * UNIT])
        xf = x_ref[...]
        r = lax.rsqrt(jnp.mean(xf * xf, axis=-1, keepdims=True) + RMS_EPS)
        xhat = xf * r
        dg_ref[...] += jnp.sum(dh * xhat, axis=0, keepdims=True)
        dxhat = dh * g_ref[...]
        dx_ref[...] = dxn_ref[...] + r * (dxhat - xhat * jnp.mean(dxhat * xhat, axis=-1, keepdims=True))

    tokd = pl.BlockSpec((tb, D_MODEL), lambda i: (i, 0))
    return pl.pallas_call(
        body, name="inproj_bwd_x",
        grid=(t // tb,),
        in_specs=[tokd, _const_spec((1, D_MODEL)), _const_spec((D_MODEL, IN_WIDTH))] + _dproj_specs(tb) + [tokd],
        out_specs=[tokd, pl.BlockSpec((1, D_MODEL), lambda i: (0, 0))],
        out_shape=[S((t, D_MODEL), F32), S((1, D_MODEL), F32)],
        compiler_params=_params(1),
    )(x, g_pre, w_in, *dproj_parts, dxn)


def _inproj_bwd_w(x, g_pre, dproj_parts):
    t = x.shape[0]
    tb = min(256, t)
    nt = t // tb

    def body(x_ref, g_ref, dpre_ref, dgates_ref, dq_ref, dk_ref, dv_ref, dw_hbm, acc_ref):
        i = pl.program_id(0)
        xf = x_ref[...]
        r = lax.rsqrt(jnp.mean(xf * xf, axis=-1, keepdims=True) + RMS_EPS)
        ht = (xf * r * g_ref[...]).T.astype(BF16)

        @pl.when(i == 0)
        def _():
            acc_ref[...] = jnp.zeros_like(acc_ref)

        for u in range(N_UNITS):
            acc_ref[:, u * UNIT:(u + 1) * UNIT] += _dot(
                ht, _dproj_unit(u, dpre_ref, dgates_ref, dq_ref, dk_ref, dv_ref))

        @pl.when(i == nt - 1)
        def _():
            pltpu.sync_copy(acc_ref, dw_hbm)

    return pl.pallas_call(
        body, name="inproj_bwd_w",
        grid=(nt,),
        in_specs=[pl.BlockSpec((tb, D_MODEL), lambda i: (i, 0)), _const_spec((1, D_MODEL))] + _dproj_specs(tb),
        out_specs=pl.BlockSpec(memory_space=pl.ANY),
        out_shape=S((D_MODEL, IN_WIDTH), F32),
        scratch_shapes=[pltpu.VMEM((D_MODEL, IN_WIDTH), F32)],
        compiler_params=_params(1),
    )(x, g_pre, *dproj_parts)


def _attn_tile_size(t):
    return min(256, t)


def _softplus_parts(z):
    sp = jnp.maximum(z, 0.0) + jnp.log(1.0 + jnp.exp(-jnp.abs(z)))
    return -sp, z - sp


def _attn_fwd(proj):
    t = proj.shape[1]
    tq = _attn_tile_size(t)

    def body(q_ref, k_ref, v_ref, o_ref, bs_ref, kf_ref):
        i = pl.program_id(1)
        row = lax.broadcasted_iota(jnp.int32, (tq, tq), 0)
        col = lax.broadcasted_iota(jnp.int32, (tq, tq), 1)
        tri = (row > col).astype(BF16)
        causal = col < row
        lane = lax.broadcasted_iota(jnp.int32, (1, LANES), 1)
        q_all = q_ref[0] * ATTN_SCALE
        in_heads = [(lane >= hh * HEAD_DIM) & (lane < (hh + 1) * HEAD_DIM) for hh in range(HEADS_PER_BLOCK)]
        q_heads = [jnp.where(in_head, q_all, 0.0).astype(BF16) for in_head in in_heads]

        def tiles(kj, state, masked):
            ks = pl.multiple_of(kj * tq, tq)
            kh = k_ref[0, pl.ds(ks, tq), :].astype(BF16)
            vh = v_ref[0, pl.ds(ks, tq), :].astype(BF16)
            new = []
            for qh, (carry, acc) in zip(q_heads, state):
                z = _dot_nt(qh, kh)
                lm, lb = _softplus_parts(z)
                if masked:
                    lm = jnp.where(causal, lm, 0.0)
                hi, lo = _split_bf16(lm)
                between = _dot(hi, tri) + _dot(lo, tri) + carry
                w = jnp.exp(lb + between)
                if masked:
                    w = jnp.where(causal, w, 0.0)
                new.append((carry + jnp.sum(lm, axis=1, keepdims=True), acc + _dot(w.astype(BF16), vh)))
            return tuple(new)

        zero = (jnp.zeros((tq, 1), F32), jnp.zeros((tq, LANES), F32))
        state = tiles(i, (zero,) * HEADS_PER_BLOCK, True)

        def more(c):
            top = jnp.max(c[1][0][0])
            for carry, _ in c[1][1:]:
                top = jnp.maximum(top, jnp.max(carry))
            return jnp.logical_and(c[0] >= 0, top >= EXP_ZERO_BELOW)

        kj_end, state = lax.while_loop(more, lambda c: (c[0] - 1, tiles(c[0], c[1], False)), (i - 1, state))
        o_out, b_out = state[0][1], jnp.broadcast_to(state[0][0], (tq, LANES))
        for in_head, (carry, acc) in zip(in_heads[1:], state[1:]):
            o_out = jnp.where(in_head, acc, o_out)
            b_out = jnp.where(in_head, carry, b_out)
        o_ref[...] = o_out
        bs_ref[0] = b_out
        kf_ref[0, 0] = jnp.zeros((8, LANES), F32) + (kj_end + 1).astype(F32)

    nq = t // tq
    return pl.pallas_call(
        body, name="attn_fwd",
        grid=(N_HEAD_BLOCKS, nq),
        in_specs=[pl.BlockSpec((1, tq, LANES), lambda p, i: (U_Q, i, p)),
                  pl.BlockSpec((1, t, LANES), lambda p, i: (U_K, 0, p)),
                  pl.BlockSpec((1, t, LANES), lambda p, i: (U_V, 0, p))],
        out_specs=[pl.BlockSpec((tq, LANES), lambda p, i: (i, p)),
                   pl.BlockSpec((1, tq, LANES), lambda p, i: (p, i, 0)),
                   pl.BlockSpec((1, 1, 8, LANES), lambda p, i: (p, i, 0, 0))],
        out_shape=[S((t, UNIT), F32), S((N_HEAD_BLOCKS, t, LANES), F32), S((N_HEAD_BLOCKS, nq, 8, LANES), F32)],
        compiler_params=_params(2),
    )(proj, proj, proj)


def _attn_bwd(proj, d_o, bsum, kfirst):
    t = proj.shape[1]
    tq = _attn_tile_size(t)
    nq = t // tq

    def body(q_ref, k_ref, v_ref, do_ref, bs_ref, kf_ref, dq_ref, dk_ref, dv_ref, dkt_ref, dvt_ref):
        i = pl.program_id(1)

        @pl.when(i == 0)
        def _():
            dkt_ref[...] = jnp.zeros_like(dkt_ref)
            dvt_ref[...] = jnp.zeros_like(dvt_ref)

        row = lax.broadcasted_iota(jnp.int32, (tq, tq), 0)
        col = lax.broadcasted_iota(jnp.int32, (tq, tq), 1)
        upto = (row <= col).astype(BF16)
        before = (row < col).astype(BF16)
        causal = col < row
        lane = lax.broadcasted_iota(jnp.int32, (1, LANES), 1)
        q_all = q_ref[0] * ATTN_SCALE
        do_all = do_ref[...]
        bs_all = bs_ref[0]
        k_first = jnp.clip(jnp.max(kf_ref[0, 0]).astype(jnp.int32), 0, i)

        in_heads = [(lane >= hh * HEAD_DIM) & (lane < (hh + 1) * HEAD_DIM) for hh in range(HEADS_PER_BLOCK)]
        heads = []
        for in_head in in_heads:
            q_m = jnp.where(in_head, q_all, 0.0)
            do_m = jnp.where(in_head, do_all, 0.0)
            btot = jnp.max(jnp.where(in_head, bs_all, -jnp.inf), axis=1, keepdims=True)
            heads.append((q_m.astype(BF16), do_m.astype(BF16), q_m.T.astype(BF16), do_m.T.astype(BF16), btot))

        def tiles(kj, state, masked):
            ks = pl.multiple_of(kj * tq, tq)
            kh = k_ref[0, pl.ds(ks, tq), :].astype(BF16)
            vh = v_ref[0, pl.ds(ks, tq), :].astype(BF16)
            new, dkt, dvt = [], None, None
            for (qh, doh, qt, dot_, btot), (c_b, c_p, dq) in zip(heads, state):
                z = _dot_nt(qh, kh)
                lm, lb = _softplus_parts(z)
                if masked:
                    lm = jnp.where(causal, lm, 0.0)
                hi, lo = _split_bf16(lm)
                between = btot - (c_b + _dot(hi, upto) + _dot(lo, upto))
                w = jnp.exp(lb + between)
                if masked:
                    w = jnp.where(causal, w, 0.0)
                e = w * _dot_nt(doh, vh)
                ehi, elo = _split_bf16(e)
                p_sum = c_p + _dot(ehi, before) + _dot(elo, before)
                beta = jnp.exp(lb)
                dz = e * (1.0 - beta) - p_sum * beta
                if masked:
                    dz = jnp.where(causal, dz, 0.0)
                dzb = dz.astype(BF16)
                dkt_h, dvt_h = _dot(qt, dzb), _dot(dot_, w.astype(BF16))
                dkt = dkt_h if dkt is None else dkt + dkt_h
                dvt = dvt_h if dvt is None else dvt + dvt_h
                new.append((c_b + jnp.sum(lm, axis=1, keepdims=True), c_p + jnp.sum(e, axis=1, keepdims=True),
                            dq + _dot(dzb, kh)))
            dkt_ref[kj] += dkt
            dvt_ref[kj] += dvt
            return tuple(new)

        zero = (jnp.zeros((tq, 1), F32), jnp.zeros((tq, 1), F32), jnp.zeros((tq, LANES), F32))
        state = lax.fori_loop(k_first, i, lambda kj, st: tiles(kj, st, False), (zero,) * HEADS_PER_BLOCK)
        state = tiles(i, state, True)
        dq_out = state[0][2]
        for in_head, (_, _, dq) in zip(in_heads[1:], state[1:]):
            dq_out = jnp.where(in_head, dq, dq_out)
        dq_ref[...] = dq_out * ATTN_SCALE

        @pl.when(i == nq - 1)
        def _():
            for kj in range(nq):
                dk_ref[kj * tq:(kj + 1) * tq, :] = dkt_ref[kj].T
                dv_ref[kj * tq:(kj + 1) * tq, :] = dvt_ref[kj].T

    return pl.pallas_call(
        body, name="attn_bwd",
        grid=(N_HEAD_BLOCKS, nq),
        in_specs=[pl.BlockSpec((1, tq, LANES), lambda p, i: (U_Q, i, p)),
                  pl.BlockSpec((1, t, LANES), lambda p, i: (U_K, 0, p)),
                  pl.BlockSpec((1, t, LANES), lambda p, i: (U_V, 0, p)),
                  pl.BlockSpec((tq, LANES), lambda p, i: (i, p)),
                  pl.BlockSpec((1, tq, LANES), lambda p, i: (p, i, 0)),
                  pl.BlockSpec((1, 1, 8, LANES), lambda p, i: (p, i, 0, 0))],
        out_specs=[pl.BlockSpec((tq, LANES), lambda p, i: (i, p)),
                   pl.BlockSpec((t, LANES), lambda p, i: (0, p)),
                   pl.BlockSpec((t, LANES), lambda p, i: (0, p))],
        out_shape=[S((t, UNIT), F32)] * 3,
        scratch_shapes=[pltpu.VMEM((nq, LANES, tq), F32), pltpu.VMEM((nq, LANES, tq), F32)],
        compiler_params=_params(2),
    )(proj, proj, proj, d_o, bsum, kfirst)


def _mix_in_specs(tb):
    hb = tb // HALO

    def unit(u):
        return pl.BlockSpec((1, tb, UNIT), lambda i: (u, i, 0))

    def halo(u):
        return pl.BlockSpec((1, HALO, UNIT), lambda i: (u, jnp.maximum(i * hb - 1, 0), 0))

    return [unit(U_P), halo(U_P), unit(U_GP), unit(U_CA), halo(U_CA), unit(U_CB), halo(U_CB), unit(U_GC),
            unit(U_GA), pl.BlockSpec((3, tb, UNIT), lambda i: (3, i, 0)), pl.BlockSpec((3, tb, UNIT), lambda i: (4, i, 0))]


def _mix_weight_specs():
    return [_const_spec((4, GROUP, GROUP)), _const_spec((1, UNIT)), _const_spec((1, UNIT)),
            _const_spec((UNIT, D_MODEL)), _const_spec((CONV_K_PAD, UNIT)), _const_spec((1, UNIT)),
            _const_spec((1, UNIT)), _const_spec((1, UNIT)), _const_spec((UNIT, D_MODEL)),
            _const_spec((UNIT, D_MODEL)), _const_spec((D_MODEL, D_MODEL)), _const_spec((1, D_MODEL))]


def _shifted_reader(buf, rbuf, tb):
    if rbuf is None:
        return lambda s: buf[s:s + tb, :]
    length = tb + HALO - 8
    for b in range(1, 8):
        rbuf[b, :, :] = buf[b:b + length, :]

    def read(s):
        a, b = divmod(s, 8)
        return buf[s:s + tb, :] if b == 0 else rbuf[b, 8 * a:8 * a + tb, :]

    return read


def _mix_forward(i, tb, proj_refs, o_ref, w_refs, pbuf, ubuf, rbuf=None):
    p_ref, ph_ref, gp_ref, ca_ref, cah_ref, cb_ref, cbh_ref, gc_ref, ga_ref, gm0_ref, gm1_ref = proj_refs
    (poolw_ref, poolb_ref, pscale_ref, wpo_ref, convw_ref, convb_ref, lng_ref, lnb_ref,
     wco_ref, wao_ref, wo_ref, gpost_ref) = w_refs
    first = i == 0
    r = {}

    pbuf[0:HALO, :] = jnp.where(first, 0.0, ph_ref[0])
    pbuf[HALO:HALO + tb, :] = p_ref[0]
    tpos = i * tb + lax.broadcasted_iota(jnp.int32, (tb, 1), 0)
    d_parts, y_parts = [], []
    for g, win in enumerate(POOL_WINDOWS):
        cs = slice(g * GROUP, (g + 1) * GROUP)
        cur = pbuf[HALO:HALO + tb, cs]
        s = cur
        for j in range(1, win):
            s = s + pbuf[HALO - j:HALO - j + tb, cs]
        cnt = jnp.minimum(tpos + 1, win).astype(F32)
        d_g = s / cnt - cur
        d_parts.append(d_g)
        y_parts.append(_dot(d_g.astype(BF16), poolw_ref[g]))
    r["d"] = d_parts
    y = jnp.concatenate(y_parts, axis=1) + poolb_ref[...]
    r["y"] = y
    mp = y * pscale_ref[...]
    gp = gp_ref[0]
    sgp = _sigmoid(gp)
    r["mp"], r["gp"], r["sgp"] = mp, gp, sgp
    ua = mp * (gp * sgp)

    ah, bh = cah_ref[0], cbh_ref[0]
    ubuf[0:HALO, :] = jnp.where(first, 0.0, ah * _sigmoid(bh))
    a, b = ca_ref[0], cb_ref[0]
    sb = _sigmoid(b)
    ubuf[HALO:HALO + tb, :] = a * sb
    r["a"], r["sb"] = a, sb
    cv = jnp.zeros((tb, UNIT), F32) + convb_ref[...]
    off = HALO - (CONV_K - 1)
    u_at = _shifted_reader(ubuf, rbuf, tb)
    for k in range(CONV_K):
        cv = cv + convw_ref[k:k + 1, :] * u_at(off + k)
    mu = jnp.mean(cv, axis=-1, keepdims=True)
    cc = cv - mu
    rs = lax.rsqrt(jnp.mean(cc * cc, axis=-1, keepdims=True) + LN_EPS)
    nrm = cc * rs
    ln = nrm * lng_ref[...] + lnb_ref[...]
    sln = _sigmoid(ln)
    sc = ln * sln
    gc = gc_ref[0]
    sgc = _sigmoid(gc)
    r["rs"], r["nrm"], r["ln"], r["sln"], r["sc"], r["gc"], r["sgc"] = rs, nrm, ln, sln, sc, gc, sgc
    ub = sc * (gc * sgc)

    o = o_ref[...]
    ga = ga_ref[0]
    sga = _sigmoid(ga)
    r["o"], r["ga"], r["sga"] = o, ga, sga
    uc = o * (ga * sga)

    r["ua"], r["ub"], r["uc"] = ua.astype(BF16), ub.astype(BF16), uc.astype(BF16)
    ya = _dot(r["ua"], wpo_ref[...])
    yb = _dot(r["ub"], wco_ref[...])
    yc = _dot(r["uc"], wao_ref[...])
    g0 = _sigmoid(jnp.concatenate([gm0_ref[0], gm0_ref[1]], axis=1))
    g1 = _sigmoid(jnp.concatenate([gm0_ref[2], gm1_ref[0]], axis=1))
    g2 = _sigmoid(jnp.concatenate([gm1_ref[1], gm1_ref[2]], axis=1))
    r["ya"], r["yb"], r["yc"], r["g0"], r["g1"], r["g2"] = ya, yb, yc, g0, g1, g2
    m = (g0 * ya + g1 * yb + g2 * yc).astype(BF16)
    r["m"] = m
    out = _dot(m, wo_ref[...])
    r2 = lax.rsqrt(jnp.mean(out * out, axis=-1, keepdims=True) + RMS_EPS)
    r["n2"], r["r2"] = out * r2, r2
    return r


def _mix_fwd(proj, o, x, weights):
    t = x.shape[0]
    tb = min(256, t)

    def body(*refs):
        proj_refs, o_ref, x_ref = refs[0:11], refs[11], refs[12]
        w_refs = refs[13:25]
        xn_ref, pbuf, ubuf, rbuf = refs[25:29]
        r = _mix_forward(pl.program_id(0), tb, proj_refs, o_ref, w_refs, pbuf, ubuf, rbuf)
        xn_ref[...] = x_ref[...] + r["n2"] * w_refs[11][...]

    return pl.pallas_call(
        body, name="mix_fwd",
        grid=(t // tb,),
        in_specs=_mix_in_specs(tb) + [pl.BlockSpec((tb, UNIT), lambda i: (i, 0)),
                                      pl.BlockSpec((tb, D_MODEL), lambda i: (i, 0))] + _mix_weight_specs(),
        out_specs=pl.BlockSpec((tb, D_MODEL), lambda i: (i, 0)),
        out_shape=S((t, D_MODEL), F32),
        scratch_shapes=[pltpu.VMEM((HALO + tb, UNIT), F32), pltpu.VMEM((HALO + tb, UNIT), F32),
                        pltpu.VMEM((8, HALO + tb - 8, UNIT), F32)],
        compiler_params=_params(1),
    )(*([proj] * 11), o, x, *weights)


def _mix_bwd(proj, o, x, dxn, weights):
    t = x.shape[0]
    tb = min(256, t)
    nt = t // tb

    def body(*refs):
        proj_refs, o_ref, x_ref, dxn_ref = refs[0:11], refs[11], refs[12], refs[13]
        w_refs = refs[14:26]
        dg_ref, dd_ref, dcv_ref, do_ref, dwo_hbm, dwp_hbm, dwc_hbm, dwa_hbm, dpw_ref, dvec_ref = refs[26:36]
        pbuf, ubuf, dwo_ref, dwp_ref, dwc_ref, dwa_ref = refs[36:42]
        (poolw_ref, _, pscale_ref, wpo_ref, _, _, lng_ref, _, wco_ref, wao_ref, wo_ref, gpost_ref) = w_refs
        i = pl.program_id(0)

        @pl.when(i == 0)
        def _():
            for ref in (dwo_ref, dwp_ref, dwc_ref, dwa_ref, dpw_ref, dvec_ref):
                ref[...] = jnp.zeros_like(ref)

        r = _mix_forward(i, tb, proj_refs, o_ref, w_refs, pbuf, ubuf)

        def colsum(v):
            return jnp.sum(v, axis=0, keepdims=True)

        dxn = dxn_ref[...]
        n2 = r["n2"]
        dvec_ref[0:1, :] += colsum(dxn * n2)
        dn2 = dxn * gpost_ref[...]
        dout = (r["r2"] * (dn2 - n2 * jnp.mean(dn2 * n2, axis=-1, keepdims=True))).astype(BF16)
        dm = _dot_nt(dout, wo_ref[...])
        dwo_ref[...] += _dot_tn(r["m"], dout)

        g0, g1, g2 = r["g0"], r["g1"], r["g2"]
        dgm = [dm * r["ya"] * g0 * (1.0 - g0), dm * r["yb"] * g1 * (1.0 - g1), dm * r["yc"] * g2 * (1.0 - g2)]
        for bidx in range(N_BRANCH):
            dg_ref[3 + 2 * bidx] = dgm[bidx][:, 0:UNIT]
            dg_ref[4 + 2 * bidx] = dgm[bidx][:, UNIT:2 * UNIT]
        dya = (dm * g0).astype(BF16)
        dyb = (dm * g1).astype(BF16)
        dyc = (dm * g2).astype(BF16)
        dua = _dot_nt(dya, wpo_ref[...])
        dub = _dot_nt(dyb, wco_ref[...])
        duc = _dot_nt(dyc, wao_ref[...])
        dwp_ref[...] += _dot_tn(r["ua"], dya)
        dwc_ref[...] += _dot_tn(r["ub"], dyb)
        dwa_ref[...] += _dot_tn(r["uc"], dyc)

        gp, sgp = r["gp"], r["sgp"]
        dmp = dua * (gp * sgp)
        dg_ref[0] = dua * r["mp"] * _dsilu(gp, sgp)
        dvec_ref[2:3, 0:UNIT] += colsum(dmp * r["y"])
        dy = dmp * pscale_ref[...]
        dvec_ref[1:2, 0:UNIT] += colsum(dy)
        dd_parts = []
        for g in range(len(POOL_WINDOWS)):
            dy_g = dy[:, g * GROUP:(g + 1) * GROUP].astype(BF16)
            dd_parts.append(_dot_nt(dy_g, poolw_ref[g]))
            dpw_ref[g] += _dot_tn(r["d"][g].astype(BF16), dy_g)
        dd_ref[...] = jnp.concatenate(dd_parts, axis=1)

        gc, sgc = r["gc"], r["sgc"]
        dsc = dub * (gc * sgc)
        dg_ref[1] = dub * r["sc"] * _dsilu(gc, sgc)
        dln = dsc * _dsilu(r["ln"], r["sln"])
        nrm = r["nrm"]
        dvec_ref[4:5, 0:UNIT] += colsum(dln * nrm)
        dvec_ref[5:6, 0:UNIT] += colsum(dln)
        dnrm = dln * lng_ref[...]
        dcv = r["rs"] * (dnrm - jnp.mean(dnrm, axis=-1, keepdims=True)
                         - nrm * jnp.mean(dnrm * nrm, axis=-1, keepdims=True))
        dvec_ref[3:4, 0:UNIT] += colsum(dcv)
        dcv_ref[...] = dcv

        ga, sga = r["ga"], r["sga"]
        do_ref[...] = duc * (ga * sga)
        dg_ref[2] = duc * r["o"] * _dsilu(ga, sga)

        @pl.when(i == nt - 1)
        def _():
            for acc, hbm in ((dwo_ref, dwo_hbm), (dwp_ref, dwp_hbm), (dwc_ref, dwc_hbm), (dwa_ref, dwa_hbm)):
                pltpu.sync_copy(acc, hbm)

    def acc_spec(shape):
        n = len(shape)
        return pl.BlockSpec(shape, lambda i: (0,) * n)

    tok = lambda w: pl.BlockSpec((tb, w), lambda i: (i, 0))
    any_spec = pl.BlockSpec(memory_space=pl.ANY)
    w_shapes = [(D_MODEL, D_MODEL), (UNIT, D_MODEL), (UNIT, D_MODEL), (UNIT, D_MODEL)]
    return pl.pallas_call(
        body, name="mix_bwd",
        grid=(nt,),
        in_specs=_mix_in_specs(tb) + [tok(UNIT), tok(D_MODEL), tok(D_MODEL)] + _mix_weight_specs(),
        out_specs=[pl.BlockSpec((9, tb, UNIT), lambda i: (0, i, 0)), tok(UNIT), tok(UNIT), tok(UNIT)]
        + [any_spec] * 4 + [acc_spec((4, GROUP, GROUP)), acc_spec((8, D_MODEL))],
        out_shape=[S((9, t, UNIT), F32), S((t, UNIT), F32), S((t, UNIT), F32), S((t, UNIT), F32)]
        + [S(s, F32) for s in w_shapes] + [S((4, GROUP, GROUP), F32), S((8, D_MODEL), F32)],
        scratch_shapes=[pltpu.VMEM((HALO + tb, UNIT), F32), pltpu.VMEM((HALO + tb, UNIT), F32)]
        + [pltpu.VMEM(s, F32) for s in w_shapes],
        compiler_params=_params(1),
    )(*([proj] * 11), o, x, dxn, *weights)


def _halo_bwd(proj, dcv, dd, conv_w):
    t = dcv.shape[0]
    tb = min(256, t)
    hb = tb // HALO
    n_halo_blocks = t // HALO
    nt = t // tb

    def body(ca_ref, cah_ref, cb_ref, cbh_ref, dcv_ref, dcvh_ref, dd_ref, ddh_ref, cw_ref,
             dpre_ref, dcw_ref, ubuf, gbuf, nbuf, ru_buf, rg_buf, acc_ref):
        i = pl.program_id(0)
        first = i == 0
        last = i == nt - 1

        @pl.when(first)
        def _():
            acc_ref[...] = jnp.zeros_like(acc_ref)

        a, b = ca_ref[0], cb_ref[0]
        sb = _sigmoid(b)
        ubuf[0:HALO, :] = jnp.where(first, 0.0, cah_ref[0] * _sigmoid(cbh_ref[0]))
        ubuf[HALO:HALO + tb, :] = a * sb
        dcv_main = dcv_ref[...]
        gbuf[0:tb, :] = dcv_main
        gbuf[tb:tb + HALO, :] = jnp.where(last, 0.0, dcvh_ref[...])

        off = HALO - (CONV_K - 1)
        u_at = _shifted_reader(ubuf, ru_buf, tb)
        g_at = _shifted_reader(gbuf, rg_buf, tb)
        du = jnp.zeros((tb, UNIT), F32)
        for k in range(CONV_K):
            du = du + cw_ref[k:k + 1, :] * g_at(CONV_K - 1 - k)
            acc_ref[k] += jnp.sum((dcv_main * u_at(off + k)).reshape(tb // 8, 8, UNIT), axis=0)
        dpre_ref[1] = du * sb
        dpre_ref[2] = du * a * sb * (1.0 - sb)

        @pl.when(last)
        def _():
            dcw_ref[...] = jnp.sum(acc_ref[...], axis=1)

        tpos = i * tb + lax.broadcasted_iota(jnp.int32, (tb + HALO, 1), 0)
        dd_main = dd_ref[...]
        dd_ext = jnp.concatenate([dd_main, jnp.where(last, 0.0, ddh_ref[...])], axis=0)
        dp_parts = []
        for g, win in enumerate(POOL_WINDOWS):
            cs = slice(g * GROUP, (g + 1) * GROUP)
            cnt = jnp.minimum(tpos + 1, win).astype(F32)
            nbuf[:, cs] = dd_ext[:, cs] / cnt
        for g, win in enumerate(POOL_WINDOWS):
            cs = slice(g * GROUP, (g + 1) * GROUP)
            s = nbuf[0:tb, cs]
            for j in range(1, win):
                s = s + nbuf[j:j + tb, cs]
            dp_parts.append(s - dd_main[:, cs])
        dpre_ref[0] = jnp.concatenate(dp_parts, axis=1)

    def unit(u):
        return pl.BlockSpec((1, tb, UNIT), lambda i: (u, i, 0))

    def past(u):
        return pl.BlockSpec((1, HALO, UNIT), lambda i: (u, jnp.maximum(i * hb - 1, 0), 0))

    tok = pl.BlockSpec((tb, UNIT), lambda i: (i, 0))
    future = pl.BlockSpec((HALO, UNIT), lambda i: (jnp.minimum((i + 1) * hb, n_halo_blocks - 1), 0))
    return pl.pallas_call(
        body, name="halo_bwd",
        grid=(nt,),
        in_specs=[unit(U_CA), past(U_CA), unit(U_CB), past(U_CB), tok, future, tok, future,
                  _const_spec((CONV_K_PAD, UNIT))],
        out_specs=[pl.BlockSpec((3, tb, UNIT), lambda i: (0, i, 0)),
                   pl.BlockSpec((CONV_K_PAD, UNIT), lambda i: (0, 0))],
        out_shape=[S((3, t, UNIT), F32), S((CONV_K_PAD, UNIT), F32)],
        scratch_shapes=[pltpu.VMEM((HALO + tb, UNIT), F32)] * 3
        + [pltpu.VMEM((8, HALO + tb - 8, UNIT), F32)] * 2 + [pltpu.VMEM((CONV_K_PAD, 8, UNIT), F32)],
        compiler_params=_params(1),
    )(proj, proj, proj, proj, dcv, dcv, dd, dd, conv_w)


def _loss_head(y, target):
    t = y.shape[0]
    tb = min(512, t)

    def body(y_ref, t_ref, loss_ref, dy_ref):
        @pl.when(pl.program_id(0) == 0)
        def _():
            loss_ref[...] = jnp.zeros_like(loss_ref)

        err = y_ref[...] - t_ref[...]
        dy_ref[...] = err * (1.0 / D_MODEL)
        part = 0.5 * jnp.sum(jnp.mean(err * err, axis=-1, keepdims=True), axis=0, keepdims=True)
        r8 = lax.broadcasted_iota(jnp.int32, (8, LANES), 0)
        c8 = lax.broadcasted_iota(jnp.int32, (8, LANES), 1)
        loss_ref[...] += jnp.where((r8 == 0) & (c8 == 0), part, 0.0)

    return pl.pallas_call(
        body, name="loss_head",
        grid=(t // tb,),
        in_specs=[pl.BlockSpec((tb, D_MODEL), lambda i: (i, 0))] * 2,
        out_specs=[pl.BlockSpec((8, LANES), lambda i: (0, 0)), pl.BlockSpec((tb, D_MODEL), lambda i: (i, 0))],
        out_shape=[S((8, LANES), F32), S((t, D_MODEL), F32)],
        compiler_params=_params(1),
    )(y, target)


def _adamw(name, parts, w, m, v):
    n, rows, cols = parts.shape
    rb = rows
    while rb * cols * 4 * (n + 7) * 2 > 24 * 1024 * 1024 and rb % 16 == 0:
        rb //= 2

    def body(p_ref, w_ref, m_ref, v_ref, g_ref, d_ref, nm_ref, nv_ref):
        g = p_ref[0]
        for k in range(1, n):
            g = g + p_ref[k]
        nm = ADAM_B1 * m_ref[...] + (1.0 - ADAM_B1) * g
        nv = ADAM_B2 * v_ref[...] + (1.0 - ADAM_B2) * (g * g)
        m_hat = nm / (1.0 - ADAM_B1 ** ADAM_STEP)
        v_hat = nv / (1.0 - ADAM_B2 ** ADAM_STEP)
        g_ref[...] = g
        d_ref[...] = -ADAM_LR * (m_hat / (jnp.sqrt(v_hat) + ADAM_EPS) + ADAM_WD * w_ref[...])
        nm_ref[...] = nm
        nv_ref[...] = nv

    blk = pl.BlockSpec((rb, cols), lambda i: (i, 0))
    return pl.pallas_call(
        body, name=name,
        grid=(rows // rb,),
        in_specs=[pl.BlockSpec((n, rb, cols), lambda i: (0, i, 0)), blk, blk, blk],
        out_specs=[blk] * 4,
        out_shape=[S((rows, cols), F32)] * 4,
        compiler_params=_params(1),
    )(parts, w, m, v)


def _axis_slice(ref, axis, block, size):
    idx = [slice(None)] * len(ref.shape)
    idx[axis] = pl.ds(pl.multiple_of(block * size, size), size)
    return ref.at[tuple(idx)]


def _axis_index(ref, axis, index):
    return ref.at[tuple([slice(None)] * axis + [index])]


COPY_PIECE_BYTES = 2 * 1024 * 1024


def _row_pieces(shape):
    rows = shape[-2]
    size = 4 * rows * shape[-1]
    for d in shape[:-2]:
        size *= d
    n = 1
    while size // n > COPY_PIECE_BYTES and rows % (2 * n * 8) == 0:
        n *= 2
    return [(k * (rows // n), rows // n) for k in range(n)]


def _rows(ref, r0, rows):
    idx = [slice(None)] * len(ref.shape)
    idx[-2] = pl.ds(r0, rows)
    return ref.at[tuple(idx)]


def _staged_local_copies(pairs, bufs, in_sems, out_sems):
    ins = [pltpu.make_async_copy(src, buf, in_sems.at[k]) for k, ((src, _), buf) in enumerate(zip(pairs, bufs))]
    for cp in ins:
        cp.start()
    for cp in ins:
        cp.wait()
    outs = [pltpu.make_async_copy(buf, dst, out_sems.at[k]) for k, ((_, dst), buf) in enumerate(zip(pairs, bufs))]
    for cp in outs:
        cp.start()
    return outs


def _run_copies(local, sends, recvs):
    for cp in local + sends:
        cp.start()
    for cp in recvs:
        cp.wait_recv()
    for cp in sends:
        cp.wait_send()
    for cp in local:
        cp.wait()


def _gather_weights(shards, axes):
    n = len(shards)
    fulls = [S(tuple(4 * d if k == ax else d for k, d in enumerate(s.shape)), s.dtype) for s, ax in zip(shards, axes)]

    def body(*refs):
        src, dst = refs[0:n], refs[n:2 * n]
        send_sems, recv_sems, pass_send_sems, pass_recv_sems, in_sems, out_sems = refs[2 * n:2 * n + 6]
        bufs = refs[2 * n + 6:3 * n + 6]
        x, y, c = lax.axis_index("x"), lax.axis_index("y"), lax.axis_index("c")
        chips = [(1 - x, y), (x, 1 - y), (1 - x, 1 - y)]

        def place(a, chip, layer=None):
            ref = _axis_slice(dst[a], axes[a], 2 * chip[0] + chip[1], src[a].shape[axes[a]])
            return ref if layer is None else ref.at[pl.ds(layer, 1)]

        def over_ici(k, chip_of_block, to):
            a, j, r0, rows = pieces[k]
            return pltpu.make_async_remote_copy(
                src_ref=_rows(src[a].at[pl.ds(c, 1)], r0, rows), dst_ref=_rows(place(a, chip_of_block, c), r0, rows),
                send_sem=send_sems.at[k], recv_sem=recv_sems.at[k], device_id=(to[0], to[1], c), device_id_type=MESH)

        def to_sibling(k, chip_of_block, layer):
            a, j, r0, rows = pieces[k]
            block = _rows(place(a, chip_of_block, layer), r0, rows)
            return pltpu.make_async_remote_copy(
                src_ref=block, dst_ref=block, send_sem=pass_send_sems.at[k], recv_sem=pass_recv_sems.at[k],
                device_id=(x, y, 1 - c), device_id_type=MESH)

        first = [over_ici(k, (x, y), chips[j]) for k, (a, j, r0, rows) in enumerate(pieces)]
        for cp in first:
            cp.start()
        local = _staged_local_copies([(src[a], place(a, (x, y))) for a in range(n)], bufs, in_sems, out_sems)
        passed = []
        for k, (a, j, r0, rows) in enumerate(pieces):
            over_ici(k, chips[j], chips[j]).wait_recv()
            passed.append(to_sibling(k, chips[j], c))
            passed[-1].start()
        for k, (a, j, r0, rows) in enumerate(pieces):
            to_sibling(k, chips[j], 1 - c).wait_recv()
        for cp in first + passed:
            cp.wait_send()
        for cp in local:
            cp.wait()

    per_array = [_row_pieces((1,) + shards[a].shape[1:]) for a in range(n)]
    pieces = [(a, j, r0, rows) for a in range(n) for r0, rows in per_array[a] for j in range(3)]
    any_spec = pl.BlockSpec(memory_space=pl.ANY)
    return pl.pallas_call(
        body, name="gather_weights",
        in_specs=[any_spec] * n, out_specs=[any_spec] * n, out_shape=fulls,
        scratch_shapes=[pltpu.SemaphoreType.DMA((len(pieces),))] * 4 + [pltpu.SemaphoreType.DMA((n,))] * 2
        + [pltpu.VMEM(s.shape, s.dtype) for s in shards],
        compiler_params=_params(0),
    )(*shards)


def _swap_halves(partials, half_axes):
    n = len(partials)

    def half_shape(a):
        return tuple(d // 2 if k == half_axes[a] else d for k, d in enumerate(partials[a].shape))

    pieces = [(a, r0, rows) for a in range(n) for r0, rows in _row_pieces(half_shape(a))]

    def body(*refs):
        src, dst = refs[0:n], refs[n:2 * n]
        send_sems, recv_sems = refs[2 * n:2 * n + 2]
        x, y, c = lax.axis_index("x"), lax.axis_index("y"), lax.axis_index("c")

        def remote(k, h):
            a, r0, rows = pieces[k]
            half = _axis_slice(src[a], half_axes[a], h, src[a].shape[half_axes[a]] // 2)
            return pltpu.make_async_remote_copy(
                src_ref=_rows(half, r0, rows), dst_ref=_rows(dst[a], r0, rows),
                send_sem=send_sems.at[k], recv_sem=recv_sems.at[k], device_id=(x, y, 1 - c), device_id_type=MESH)

        sends = [remote(k, 1 - c) for k in range(len(pieces))]
        _run_copies([], sends, [remote(k, c) for k in range(len(pieces))])

    any_spec = pl.BlockSpec(memory_space=pl.ANY)
    return pl.pallas_call(
        body, name="swap_halves",
        in_specs=[any_spec] * n, out_specs=[any_spec] * n,
        out_shape=[S(half_shape(a), partials[a].dtype) for a in range(n)],
        scratch_shapes=[pltpu.SemaphoreType.DMA((len(pieces),))] * 2,
    )(*partials)


def _add_own_layer(name, mine, theirs, layer):
    _, rows, cols = mine.shape
    rb = rows
    while rb * cols * 4 * 3 * 2 > 24 * 1024 * 1024 and rb % 32 == 0:
        rb //= 2

    def body(layer_ref, m_ref, t_ref, o_ref):
        o_ref[...] = (m_ref[...] + t_ref[...]).astype(BF16)

    return pl.pallas_call(
        body, name=name,
        grid_spec=pltpu.PrefetchScalarGridSpec(
            num_scalar_prefetch=1, grid=(rows // rb,),
            in_specs=[pl.BlockSpec((1, rb, cols), lambda i, l: (l[0], i, 0)),
                      pl.BlockSpec((1, rb, cols), lambda i, l: (0, i, 0))],
            out_specs=pl.BlockSpec((1, rb, cols), lambda i, l: (0, i, 0))),
        out_shape=S((1, rows, cols), BF16),
        compiler_params=_params(1),
    )(layer, mine, theirs)


def _send_to_owners(chip_halves, shard_axes, packed):
    n = len(chip_halves)

    def owned_shape(a):
        return tuple(d // 4 if k == shard_axes[a] else d for k, d in enumerate(chip_halves[a].shape))

    flips = [(dx, dy, dc) for dx in (0, 1) for dy in (0, 1) for dc in (0, 1)][1:]

    def body(*refs):
        src, psrc, dst, pdst = refs[0:n], refs[n], refs[n + 1:2 * n + 1], refs[2 * n + 1]
        send_sems, recv_sems, local_sems, psend_sems, precv_sems = refs[2 * n + 2:2 * n + 7]
        x, y, c = lax.axis_index("x"), lax.axis_index("y"), lax.axis_index("c")
        my_chip = 2 * x + y
        chips = [(1 - x, y), (x, 1 - y), (1 - x, 1 - y)]

        def owned(a, chip):
            return _axis_slice(src[a], shard_axes[a], 2 * chip[0] + chip[1], src[a].shape[shard_axes[a]] // 4)

        def remote(a, j, to, from_chip):
            return pltpu.make_async_remote_copy(
                src_ref=owned(a, to), dst_ref=dst[a].at[from_chip], send_sem=send_sems.at[a, j],
                recv_sem=recv_sems.at[a, j], device_id=(to[0], to[1], c), device_id_type=MESH)

        def peer(f):
            return (1 - x if f[0] else x, 1 - y if f[1] else y, 1 - c if f[2] else c)

        def premote(j, to, from_dev):
            return pltpu.make_async_remote_copy(
                src_ref=psrc, dst_ref=pdst.at[from_dev], send_sem=psend_sems.at[j], recv_sem=precv_sems.at[j],
                device_id=to, device_id_type=MESH)

        local = [pltpu.make_async_copy(_rows(owned(a, (x, y)), r0, rows), _rows(dst[a].at[my_chip], r0, rows),
                                       local_sems.at[k]) for k, (a, r0, rows) in enumerate(own_pieces)]
        local.append(pltpu.make_async_copy(psrc, pdst.at[4 * x + 2 * y + c], local_sems.at[len(own_pieces)]))
        sends = [remote(a, j, chip, my_chip) for a in range(n) for j, chip in enumerate(chips)]
        sends += [premote(j, peer(f), 4 * x + 2 * y + c) for j, f in enumerate(flips)]
        recvs = [remote(a, j, (x, y), 2 * chip[0] + chip[1]) for a in range(n) for j, chip in enumerate(chips)]
        for j, f in enumerate(flips):
            px, py, pc = peer(f)
            recvs.append(premote(j, (x, y, c), 4 * px + 2 * py + pc))
        _run_copies(local, sends, recvs)

    own_pieces = [(a, r0, rows) for a in range(n) for r0, rows in _row_pieces(owned_shape(a))]
    any_spec = pl.BlockSpec(memory_space=pl.ANY)
    return pl.pallas_call(
        body, name="send_to_owners",
        in_specs=[any_spec] * (n + 1), out_specs=[any_spec] * (n + 1),
        out_shape=[S((4,) + owned_shape(a), chip_halves[a].dtype) for a in range(n)]
        + [S((N_DEV,) + packed.shape, packed.dtype)],
        scratch_shapes=[pltpu.SemaphoreType.DMA((n, 3)), pltpu.SemaphoreType.DMA((n, 3)),
                        pltpu.SemaphoreType.DMA((len(own_pieces) + 1,)), pltpu.SemaphoreType.DMA((7,)),
                        pltpu.SemaphoreType.DMA((7,))],
    )(*chip_halves, packed)


def _join_halves(reduced, half_axes):
    n = len(reduced)

    def joined_shape(a):
        s, ax = reduced[a].shape, half_axes[a]
        return s[:ax] + (2,) + s[ax:]

    pieces = [(a, r0, rows) for a in range(n) for r0, rows in _row_pieces(reduced[a].shape)]

    def body(*refs):
        src, dst = refs[0:n], refs[n:2 * n]
        send_sems, recv_sems, in_sems, out_sems = refs[2 * n:2 * n + 4]
        bufs = refs[2 * n + 4:3 * n + 4]
        x, y, c = lax.axis_index("x"), lax.axis_index("y"), lax.axis_index("c")

        def remote(k, h):
            a, r0, rows = pieces[k]
            return pltpu.make_async_remote_copy(
                src_ref=_rows(src[a], r0, rows), dst_ref=_rows(_axis_index(dst[a], half_axes[a], h), r0, rows),
                send_sem=send_sems.at[k], recv_sem=recv_sems.at[k], device_id=(x, y, 1 - c), device_id_type=MESH)

        sends = [remote(k, c) for k in range(len(pieces))]
        for cp in sends:
            cp.start()
        own = [(src[a], _axis_index(dst[a], half_axes[a], c)) for a in range(n)]
        local = _staged_local_copies(own, bufs, in_sems, out_sems)
        for k in range(len(pieces)):
            remote(k, 1 - c).wait_recv()
        for cp in sends:
            cp.wait_send()
        for cp in local:
            cp.wait()

    any_spec = pl.BlockSpec(memory_space=pl.ANY)
    return pl.pallas_call(
        body, name="join_halves",
        in_specs=[any_spec] * n, out_specs=[any_spec] * n,
        out_shape=[S(joined_shape(a), reduced[a].dtype) for a in range(n)],
        scratch_shapes=[pltpu.SemaphoreType.DMA((len(pieces),))] * 2 + [pltpu.SemaphoreType.DMA((n,))] * 2
        + [pltpu.VMEM(r.shape, r.dtype) for r in reduced],
        compiler_params=_params(0),
    )(*reduced)


def _sum_slots(name, parts):
    n = parts.shape[0]
    shape = parts.shape[1:]
    flat = parts.reshape((n, -1, shape[-1]))
    rows, cols = flat.shape[1:]
    rb = rows
    while rb * cols * 4 * (n + 1) * 2 > 24 * 1024 * 1024 and rb % 32 == 0:
        rb //= 2

    def body(p_ref, o_ref):
        acc = p_ref[0].astype(F32)
        for k in range(1, n):
            acc = acc + p_ref[k].astype(F32)
        o_ref[...] = acc

    out = pl.pallas_call(
        body, name=name,
        grid=(rows // rb,),
        in_specs=[pl.BlockSpec((n, rb, cols), lambda i: (0, i, 0))],
        out_specs=pl.BlockSpec((rb, cols), lambda i: (i, 0)),
        out_shape=S((rows, cols), F32),
        compiler_params=_params(1),
    )(flat)
    return out.reshape(shape)


SMALL_NAMES = ("norm_pre", "pool_w", "pool_b", "pool_scale", "conv_b", "conv_ln_g", "conv_ln_b", "norm_post")
BIG_NAMES = ("w_in", "w_pool_out", "conv_w", "w_conv_out", "w_attn_out", "w_o")
SHARD_AXIS = {"w_in": 2, "w_pool_out": 2, "conv_w": 2, "w_conv_out": 2, "w_attn_out": 2, "w_o": 1}
WEIGHT_ORDER = ("norm_pre", "w_in", "pool_w", "pool_b", "pool_scale", "w_pool_out", "conv_w", "conv_b",
                "conv_ln_g", "conv_ln_b", "w_conv_out", "w_attn_out", "w_o", "norm_post")


def _pack_small(parts):
    return jnp.concatenate([parts[n].reshape(-1, LANES) for n in SMALL_NAMES], axis=0)


def _unpack_small(packed, shapes):
    out, r0 = {}, 0
    for n in SMALL_NAMES:
        size = 1
        for d in shapes[n]:
            size *= d
        rows = size // LANES
        out[n] = packed[r0:r0 + rows].reshape(shapes[n])
        r0 += rows
    return out


def _layer_weights(full, small, l):
    row = lambda a: a[l][None, :]
    return (small["pool_w"][l].astype(BF16), small["pool_b"][l].reshape(1, UNIT), row(small["pool_scale"]),
            full["w_pool_out"][l], full["conv_w"][l], row(small["conv_b"]), row(small["conv_ln_g"]),
            row(small["conv_ln_b"]), full["w_conv_out"][l], full["w_attn_out"][l], full["w_o"][l],
            row(small["norm_post"]))


def _forward_backward(x, target, full, small):
    depth = full["w_in"].shape[0]
    acts = []
    for l in range(depth):
        g_pre = small["norm_pre"][l][None, :]
        proj = _inproj_fwd(x, g_pre, full["w_in"][l])
        o, bsum, kfirst = _attn_fwd(proj)
        weights = _layer_weights(full, small, l)
        x_next = _mix_fwd(proj, o, x, weights)
        acts.append((x, proj, o, bsum, kfirst, weights, g_pre))
        x = x_next
    loss_tile, dx = _loss_head(x, target)

    big = {n: [None] * depth for n in BIG_NAMES}
    sm = {n: [None] * depth for n in SMALL_NAMES}
    for l in reversed(range(depth)):
        x_in, proj, o, bsum, kfirst, weights, g_pre = acts[l]
        dgates, dd, dcv, d_o, dwo, dwp, dwc, dwa, dpw, dvec = _mix_bwd(proj, o, x_in, dx, weights)
        dpre, dcw = _halo_bwd(proj, dcv, dd, full["conv_w"][l])
        dq, dk, dv = _attn_bwd(proj, d_o, bsum, kfirst)
        dproj_parts = (dpre, dgates, dq, dk, dv)
        dx, dg_pre = _inproj_bwd_x(x_in, g_pre, full["w_in"][l], dproj_parts, dx)
        big["w_in"][l] = _inproj_bwd_w(x_in, g_pre, dproj_parts)
        big["w_pool_out"][l], big["w_conv_out"][l], big["w_attn_out"][l], big["w_o"][l] = dwp, dwc, dwa, dwo
        big["conv_w"][l] = dcw
        sm["norm_pre"][l] = dg_pre[0]
        sm["pool_w"][l] = dpw
        sm["norm_post"][l] = dvec[0]
        sm["pool_b"][l] = dvec[1, 0:UNIT].reshape(4, GROUP)
        sm["pool_scale"][l] = dvec[2, 0:UNIT]
        sm["conv_b"][l] = dvec[3, 0:UNIT]
        sm["conv_ln_g"][l] = dvec[4, 0:UNIT]
        sm["conv_ln_b"][l] = dvec[5, 0:UNIT]
    big = {n: jnp.stack(v) for n, v in big.items()}
    sm = {n: jnp.stack(v) for n, v in sm.items()}
    return loss_tile, dx, big, sm


def kernel(x, norm_pre, w_in, pool_w, pool_b, pool_scale, w_pool_out, conv_w, conv_b, conv_ln_g, conv_ln_b, w_conv_out, w_attn_out, w_o, norm_post, loss_target, m_norm_pre, m_w_in, m_pool_w, m_pool_b, m_pool_scale, m_w_pool_out, m_conv_w, m_conv_b, m_conv_ln_g, m_conv_ln_b, m_w_conv_out, m_w_attn_out, m_w_o, m_norm_post, v_norm_pre, v_w_in, v_pool_w, v_pool_b, v_pool_scale, v_w_pool_out, v_conv_w, v_conv_b, v_conv_ln_g, v_conv_ln_b, v_w_conv_out, v_w_attn_out, v_w_o, v_norm_post):
    w = dict(norm_pre=norm_pre, w_in=w_in, pool_w=pool_w, pool_b=pool_b, pool_scale=pool_scale,
             w_pool_out=w_pool_out, conv_w=conv_w, conv_b=conv_b, conv_ln_g=conv_ln_g, conv_ln_b=conv_ln_b,
             w_conv_out=w_conv_out, w_attn_out=w_attn_out, w_o=w_o, norm_post=norm_post)
    m = dict(norm_pre=m_norm_pre, w_in=m_w_in, pool_w=m_pool_w, pool_b=m_pool_b, pool_scale=m_pool_scale,
             w_pool_out=m_w_pool_out, conv_w=m_conv_w, conv_b=m_conv_b, conv_ln_g=m_conv_ln_g,
             conv_ln_b=m_conv_ln_b, w_conv_out=m_w_conv_out, w_attn_out=m_w_attn_out, w_o=m_w_o,
             norm_post=m_norm_post)
    v = dict(norm_pre=v_norm_pre, w_in=v_w_in, pool_w=v_pool_w, pool_b=v_pool_b, pool_scale=v_pool_scale,
             w_pool_out=v_w_pool_out, conv_w=v_conv_w, conv_b=v_conv_b, conv_ln_g=v_conv_ln_g,
             conv_ln_b=v_conv_ln_b, w_conv_out=v_w_conv_out, w_attn_out=v_w_attn_out, w_o=v_w_o,
             norm_post=v_norm_post)
    pad_taps = lambda a: jnp.pad(a, ((0, 0), (0, CONV_K_PAD - CONV_K), (0, 0)))

    shards = [pad_taps(w[n]) if n == "conv_w" else w[n].astype(BF16) for n in BIG_NAMES]
    full = dict(zip(BIG_NAMES, _gather_weights(shards, [SHARD_AXIS[n] for n in BIG_NAMES])))
    small = {n: w[n] for n in SMALL_NAMES}

    loss_tile, dx, big, sm = _forward_backward(x[0], loss_target[0], full, small)

    packed = jnp.concatenate([_pack_small(sm), loss_tile], axis=0)
    layer_axis = [0] * len(BIG_NAMES)
    theirs = _swap_halves([big[n] for n in BIG_NAMES], layer_axis)
    my_layer = lax.axis_index("c").astype(jnp.int32).reshape(1)
    chip_sums = [_add_own_layer("sum_cores_" + n, big[n], t, my_layer) for n, t in zip(BIG_NAMES, theirs)]
    exchanged = _send_to_owners(chip_sums, [SHARD_AXIS[n] for n in BIG_NAMES], packed)
    reduced = [_sum_slots("sum_chips_" + n, p) for n, p in zip(BIG_NAMES, exchanged[:-1])]
    joined = _join_halves(reduced, layer_axis)
    out = {}
    for n, g in zip(BIG_NAMES, joined):
        wn, mn, vn = (pad_taps(a[n]) if n == "conv_w" else a[n] for a in (w, m, v))
        shape = wn.shape
        flat = lambda a: a.reshape((-1, shape[-1]))
        res = _adamw("adamw_" + n, g.reshape((1, -1, shape[-1])), flat(wn), flat(mn), flat(vn))
        res = [r.reshape(shape) for r in res]
        out[n] = [r[:, :CONV_K] for r in res] if n == "conv_w" else res
    parts = exchanged[-1]
    n_small_rows = parts.shape[1] - 8
    zeros_tile = jnp.zeros((8, LANES), F32)
    packs = [jnp.concatenate([_pack_small(a), zeros_tile], axis=0) for a in (w, m, v)]
    res = _adamw("adamw_small", parts, *packs)
    loss = res[0][n_small_rows, 0]
    shapes = {n: w[n].shape for n in SMALL_NAMES}
    unpacked = [_unpack_small(r[:n_small_rows], shapes) for r in res]
    for n in SMALL_NAMES:
        out[n] = [u[n] for u in unpacked]

    grads = [out[n][0] for n in WEIGHT_ORDER]
    deltas = [out[n][1] for n in WEIGHT_ORDER]
    new_m = [out[n][2] for n in WEIGHT_ORDER]
    new_v = [out[n][3] for n in WEIGHT_ORDER]
    return (loss, dx[None], *grads, *deltas, *new_m, *new_v)
```

```python
import jax
import jax.numpy as jnp
from jax import lax
from jax.experimental import pallas as pl
from jax.experimental.pallas import tpu as pltpu

F32 = jnp.float32
BF16 = jnp.bfloat16

D_MODEL = 1024
UNIT = 512
N_UNITS = 15
IN_WIDTH = UNIT * N_UNITS
N_HEADS = 8
HEAD_DIM = 64
HEADS_PER_BLOCK = 2
N_HEAD_BLOCKS = N_HEADS // HEADS_PER_BLOCK
LANES = 128
CONV_K = 31
CONV_K_PAD = 32
HALO = 32
POOL_WINDOWS = (2, 4, 8, 16)
GROUP = 128
N_BRANCH = 3
RMS_EPS = 1e-6
LN_EPS = 1e-5
ATTN_SCALE = 0.125
EXP_ZERO_BELOW = -104.0

ADAM_LR = 0.001
ADAM_B1 = 0.9
ADAM_B2 = 0.999
ADAM_EPS = 1e-08
ADAM_WD = 0.01
ADAM_STEP = 10

U_P, U_GP, U_CA, U_CB, U_GC, U_Q, U_K, U_V, U_GA, U_GM = 0, 1, 2, 3, 4, 5, 6, 7, 8, 9

V7X_VMEM_LIMIT = 62 * 1024 * 1024
N_DEV = 8
MESH = pl.DeviceIdType.MESH

S = jax.ShapeDtypeStruct


def _params(n_grid):
    return pltpu.CompilerParams(dimension_semantics=("arbitrary",) * n_grid, vmem_limit_bytes=V7X_VMEM_LIMIT)


def _sigmoid(x):
    return 1.0 / (1.0 + jnp.exp(-x))


def _dsilu(x, s):
    return s * (1.0 + x * (1.0 - s))


def _dot(a, b):
    return jnp.dot(a, b, preferred_element_type=F32)


def _dot_nt(a, b):
    return lax.dot_general(a, b, (((1,), (1,)), ((), ())), preferred_element_type=F32)


def _dot_tn(a, b):
    return lax.dot_general(a, b, (((0,), (0,)), ((), ())), preferred_element_type=F32)


def _split_bf16(x):
    hi = x.astype(BF16)
    lo = (x - hi.astype(F32)).astype(BF16)
    return hi, lo


def _const_spec(shape):
    n = len(shape)
    return pl.BlockSpec(shape, lambda *_: (0,) * n, pipeline_mode=pl.Buffered(1))


def _inproj_fwd(x, g_pre, w_in):
    t = x.shape[0]
    tb = min(1024, t)
    ug = 3
    nb = ug * UNIT

    def body(x_ref, g_ref, w_ref, o_ref, h_ref):
        @pl.when(pl.program_id(1) == 0)
        def _():
            xf = x_ref[...]
            r = lax.rsqrt(jnp.mean(xf * xf, axis=-1, keepdims=True) + RMS_EPS)
            h_ref[...] = (xf * r * g_ref[...]).astype(BF16)

        acc = _dot(h_ref[...], w_ref[...])
        for u in range(ug):
            o_ref[u] = acc[:, u * UNIT:(u + 1) * UNIT]

    return pl.pallas_call(
        body, name="inproj_fwd",
        grid=(t // tb, N_UNITS // ug),
        in_specs=[pl.BlockSpec((tb, D_MODEL), lambda i, j: (i, 0)),
                  pl.BlockSpec((1, D_MODEL), lambda i, j: (0, 0)),
                  pl.BlockSpec((D_MODEL, nb), lambda i, j: (0, j))],
        out_specs=pl.BlockSpec((ug, tb, UNIT), lambda i, j: (j, i, 0)),
        out_shape=S((N_UNITS, t, UNIT), F32),
        scratch_shapes=[pltpu.VMEM((tb, D_MODEL), BF16)],
        compiler_params=_params(2),
    )(x, g_pre, w_in)


def _dproj_specs(tb):
    tok = pl.BlockSpec((tb, UNIT), lambda i: (i, 0))
    return [pl.BlockSpec((3, tb, UNIT), lambda i: (0, i, 0)), pl.BlockSpec((9, tb, UNIT), lambda i: (0, i, 0)),
            tok, tok, tok]


def _dproj_unit(u, dpre_ref, dgates_ref, dq_ref, dk_ref, dv_ref):
    pre = {U_P: 0, U_CA: 1, U_CB: 2}
    gate = {U_GP: 0, U_GC: 1, U_GA: 2}
    if u in pre:
        val = dpre_ref[pre[u]]
    elif u in gate:
        val = dgates_ref[gate[u]]
    elif u >= U_GM:
        val = dgates_ref[3 + u - U_GM]
    else:
        val = {U_Q: dq_ref, U_K: dk_ref, U_V: dv_ref}[u][...]
    return val.astype(BF16)


def _inproj_bwd_x(x, g_pre, w_in, dproj_parts, dxn):
    t = x.shape[0]
    tb = min(256, t)

    def body(x_ref, g_ref, w_ref, dpre_ref, dgates_ref, dq_ref, dk_ref, dv_ref, dxn_ref, dx_ref, dg_ref):
        @pl.when(pl.program_id(0) == 0)
        def _():
            dg_ref[...] = jnp.zeros_like(dg_ref)

        dh = jnp.zeros((tb, D_MODEL), F32)
        for u in range(N_UNITS):
            dh = dh + _dot_nt(_dproj_unit(u, dpre_ref, dgates_ref, dq_ref, dk_ref, dv_ref),
                              w_ref[:, u * UNIT:(u + 1) * UNIT])
        xf = x_ref[...]
        r = lax.rsqrt(jnp.mean(xf * xf, axis=-1, keepdims=True) + RMS_EPS)
        xhat = xf * r
        dg_ref[...] += jnp.sum(dh * xhat, axis=0, keepdims=True)
        dxhat = dh * g_ref[...]
        dx_ref[...] = dxn_ref[...] + r * (dxhat - xhat * jnp.mean(dxhat * xhat, axis=-1, keepdims=True))

    tokd = pl.BlockSpec((tb, D_MODEL), lambda i: (i, 0))
    return pl.pallas_call(
        body, name="inproj_bwd_x",
        grid=(t // tb,),
        in_specs=[tokd, _const_spec((1, D_MODEL)), _const_spec((D_MODEL, IN_WIDTH))] + _dproj_specs(tb) + [tokd],
        out_specs=[tokd, pl.BlockSpec((1, D_MODEL), lambda i: (0, 0))],
        out_shape=[S((t, D_MODEL), F32), S((1, D_MODEL), F32)],
        compiler_params=_params(1),
    )(x, g_pre, w_in, *dproj_parts, dxn)


def _inproj_bwd_w(x, g_pre, dproj_parts):
    t = x.shape[0]
    tb = min(256, t)
    nt = t // tb

    def body(x_ref, g_ref, dpre_ref, dgates_ref, dq_ref, dk_ref, dv_ref, dw_hbm, acc_ref):
        i = pl.program_id(0)
        xf = x_ref[...]
        r = lax.rsqrt(jnp.mean(xf * xf, axis=-1, keepdims=True) + RMS_EPS)
        ht = (xf * r * g_ref[...]).T.astype(BF16)

        @pl.when(i == 0)
        def _():
            acc_ref[...] = jnp.zeros_like(acc_ref)

        for u in range(N_UNITS):
            acc_ref[:, u * UNIT:(u + 1) * UNIT] += _dot(
                ht, _dproj_unit(u, dpre_ref, dgates_ref, dq_ref, dk_ref, dv_ref))

        @pl.when(i == nt - 1)
        def _():
            pltpu.sync_copy(acc_ref, dw_hbm)

    return pl.pallas_call(
        body, name="inproj_bwd_w",
        grid=(nt,),
        in_specs=[pl.BlockSpec((tb, D_MODEL), lambda i: (i, 0)), _const_spec((1, D_MODEL))] + _dproj_specs(tb),
        out_specs=pl.BlockSpec(memory_space=pl.ANY),
        out_shape=S((D_MODEL, IN_WIDTH), F32),
        scratch_shapes=[pltpu.VMEM((D_MODEL, IN_WIDTH), F32)],
        compiler_params=_params(1),
    )(x, g_pre, *dproj_parts)


def _attn_tile_size(t):
    return min(256, t)


def _softplus_parts(z):
    sp = jnp.maximum(z, 0.0) + jnp.log(1.0 + jnp.exp(-jnp.abs(z)))
    return -sp, z - sp


def _attn_fwd(proj):
    t = proj.shape[1]
    tq = _attn_tile_size(t)

    def body(q_ref, k_ref, v_ref, o_ref, bs_ref, kf_ref):
        i = pl.program_id(1)
        row = lax.broadcasted_iota(jnp.int32, (tq, tq), 0)
        col = lax.broadcasted_iota(jnp.int32, (tq, tq), 1)
        tri = (row > col).astype(BF16)
        causal = col < row
        lane = lax.broadcasted_iota(jnp.int32, (1, LANES), 1)
        q_all = q_ref[0] * ATTN_SCALE
        in_heads = [(lane >= hh * HEAD_DIM) & (lane < (hh + 1) * HEAD_DIM) for hh in range(HEADS_PER_BLOCK)]
        q_heads = [jnp.where(in_head, q_all, 0.0).astype(BF16) for in_head in in_heads]

        def tiles(kj, state, masked):
            ks = pl.multiple_of(kj * tq, tq)
            kh = k_ref[0, pl.ds(ks, tq), :].astype(BF16)
            vh = v_ref[0, pl.ds(ks, tq), :].astype(BF16)
            new = []
            for qh, (carry, acc) in zip(q_heads, state):
                z = _dot_nt(qh, kh)
                lm, lb = _softplus_parts(z)
                if masked:
                    lm = jnp.where(causal, lm, 0.0)
                hi, lo = _split_bf16(lm)
                between = _dot(hi, tri) + _dot(lo, tri) + carry
                w = jnp.exp(lb + between)
                if masked:
                    w = jnp.where(causal, w, 0.0)
                new.append((carry + jnp.sum(lm, axis=1, keepdims=True), acc + _dot(w.astype(BF16), vh)))
            return tuple(new)

        zero = (jnp.zeros((tq, 1), F32), jnp.zeros((tq, LANES), F32))
        state = tiles(i, (zero,) * HEADS_PER_BLOCK, True)

        def more(c):
            top = jnp.max(c[1][0][0])
            for carry, _ in c[1][1:]:
                top = jnp.maximum(top, jnp.max(carry))
            return jnp.logical_and(c[0] >= 0, top >= EXP_ZERO_BELOW)

        kj_end, state = lax.while_loop(more, lambda c: (c[0] - 1, tiles(c[0], c[1], False)), (i - 1, state))
        o_out, b_out = state[0][1], jnp.broadcast_to(state[0][0], (tq, LANES))
        for in_head, (carry, acc) in zip(in_heads[1:], state[1:]):
            o_out = jnp.where(in_head, acc, o_out)
            b_out = jnp.where(in_head, carry, b_out)
        o_ref[...] = o_out
        bs_ref[0] = b_out
        kf_ref[0, 0] = jnp.zeros((8, LANES), F32) + (kj_end + 1).astype(F32)

    nq = t // tq
    return pl.pallas_call(
        body, name="attn_fwd",
        grid=(N_HEAD_BLOCKS, nq),
        in_specs=[pl.BlockSpec((1, tq, LANES), lambda p, i: (U_Q, i, p)),
                  pl.BlockSpec((1, t, LANES), lambda p, i: (U_K, 0, p)),
                  pl.BlockSpec((1, t, LANES), lambda p, i: (U_V, 0, p))],
        out_specs=[pl.BlockSpec((tq, LANES), lambda p, i: (i, p)),
                   pl.BlockSpec((1, tq, LANES), lambda p, i: (p, i, 0)),
                   pl.BlockSpec((1, 1, 8, LANES), lambda p, i: (p, i, 0, 0))],
        out_shape=[S((t, UNIT), F32), S((N_HEAD_BLOCKS, t, LANES), F32), S((N_HEAD_BLOCKS, nq, 8, LANES), F32)],
        compiler_params=_params(2),
    )(proj, proj, proj)


def _attn_bwd(proj, d_o, bsum, kfirst):
    t = proj.shape[1]
    tq = _attn_tile_size(t)
    nq = t // tq

    def body(q_ref, k_ref, v_ref, do_ref, bs_ref, kf_ref, dq_ref, dk_ref, dv_ref, dkt_ref, dvt_ref):
        i = pl.program_id(1)

        @pl.when(i == 0)
        def _():
            dkt_ref[...] = jnp.zeros_like(dkt_ref)
            dvt_ref[...] = jnp.zeros_like(dvt_ref)

        row = lax.broadcasted_iota(jnp.int32, (tq, tq), 0)
        col = lax.broadcasted_iota(jnp.int32, (tq, tq), 1)
        upto = (row <= col).astype(BF16)
        before = (row < col).astype(BF16)
        causal = col < row
        lane = lax.broadcasted_iota(jnp.int32, (1, LANES), 1)
        q_all = q_ref[0] * ATTN_SCALE
        do_all = do_ref[...]
        bs_all = bs_ref[0]
        k_first = jnp.clip(jnp.max(kf_ref[0, 0]).astype(jnp.int32), 0, i)

        in_heads = [(lane >= hh * HEAD_DIM) & (lane < (hh + 1) * HEAD_DIM) for hh in range(HEADS_PER_BLOCK)]
        heads = []
        for in_head in in_heads:
            q_m = jnp.where(in_head, q_all, 0.0)
            do_m = jnp.where(in_head, do_all, 0.0)
            btot = jnp.max(jnp.where(in_head, bs_all, -jnp.inf), axis=1, keepdims=True)
            heads.append((q_m.astype(BF16), do_m.astype(BF16), q_m.T.astype(BF16), do_m.T.astype(BF16), btot))

        def tiles(kj, state, masked):
            ks = pl.multiple_of(kj * tq, tq)
            kh = k_ref[0, pl.ds(ks, tq), :].astype(BF16)
            vh = v_ref[0, pl.ds(ks, tq), :].astype(BF16)
            new, dkt, dvt = [], None, None
            for (qh, doh, qt, dot_, btot), (c_b, c_p, dq) in zip(heads, state):
                z = _dot_nt(qh, kh)
                lm, lb = _softplus_parts(z)
                if masked:
                    lm = jnp.where(causal, lm, 0.0)
                hi, lo = _split_bf16(lm)
                between = btot - (c_b + _dot(hi, upto) + _dot(lo, upto))
                w = jnp.exp(lb + between)
                if masked:
                    w = jnp.where(causal, w, 0.0)
                e = w * _dot_nt(doh, vh)
                ehi, elo = _split_bf16(e)
                p_sum = c_p + _dot(ehi, before) + _dot(elo, before)
                beta = jnp.exp(lb)
                dz = e * (1.0 - beta) - p_sum * beta
                if masked:
                    dz = jnp.where(causal, dz, 0.0)
                dzb = dz.astype(BF16)
                dkt_h, dvt_h = _dot(qt, dzb), _dot(dot_, w.astype(BF16))
                dkt = dkt_h if dkt is None else dkt + dkt_h
                dvt = dvt_h if dvt is None else dvt + dvt_h
                new.append((c_b + jnp.sum(lm, axis=1, keepdims=True), c_p + jnp.sum(e, axis=1, keepdims=True),
                            dq + _dot(dzb, kh)))
            dkt_ref[kj] += dkt
            dvt_ref[kj] += dvt
            return tuple(new)

        zero = (jnp.zeros((tq, 1), F32), jnp.zeros((tq, 1), F32), jnp.zeros((tq, LANES), F32))
        state = lax.fori_loop(k_first, i, lambda kj, st: tiles(kj, st, False), (zero,) * HEADS_PER_BLOCK)
        state = tiles(i, state, True)
        dq_out = state[0][2]
        for in_head, (_, _, dq) in zip(in_heads[1:], state[1:]):
            dq_out = jnp.where(in_head, dq, dq_out)
        dq_ref[...] = dq_out * ATTN_SCALE

        @pl.when(i == nq - 1)
        def _():
            for kj in range(nq):
                dk_ref[kj * tq:(kj + 1) * tq, :] = dkt_ref[kj].T
                dv_ref[kj * tq:(kj + 1) * tq, :] = dvt_ref[kj].T

    return pl.pallas_call(
        body, name="attn_bwd",
        grid=(N_HEAD_BLOCKS, nq),
        in_specs=[pl.BlockSpec((1, tq, LANES), lambda p, i: (U_Q, i, p)),
                  pl.BlockSpec((1, t, LANES), lambda p, i: (U_K, 0, p)),
                  pl.BlockSpec((1, t, LANES), lambda p, i: (U_V, 0, p)),
                  pl.BlockSpec((tq, LANES), lambda p, i: (i, p)),
                  pl.BlockSpec((1, tq, LANES), lambda p, i: (p, i, 0)),
                  pl.BlockSpec((1, 1, 8, LANES), lambda p, i: (p, i, 0, 0))],
        out_specs=[pl.BlockSpec((tq, LANES), lambda p, i: (i, p)),
                   pl.BlockSpec((t, LANES), lambda p, i: (0, p)),
                   pl.BlockSpec((t, LANES), lambda p, i: (0, p))],
        out_shape=[S((t, UNIT), F32)] * 3,
        scratch_shapes=[pltpu.VMEM((nq, LANES, tq), F32), pltpu.VMEM((nq, LANES, tq), F32)],
        compiler_params=_params(2),
    )(proj, proj, proj, d_o, bsum, kfirst)


N_MIX_VIEWS = 11
N_MIX_VIEWS_NO_CONV = 7


def _mix_in_specs(tb, conv_inputs=True):
    hb = tb // HALO

    def unit(u):
        return pl.BlockSpec((1, tb, UNIT), lambda i: (u, i, 0))

    def halo(u):
        return pl.BlockSpec((1, HALO, UNIT), lambda i: (u, jnp.maximum(i * hb - 1, 0), 0))

    conv = [unit(U_CA), halo(U_CA), unit(U_CB), halo(U_CB)] if conv_inputs else []
    return [unit(U_P), halo(U_P), unit(U_GP)] + conv + [
        unit(U_GC), unit(U_GA), pl.BlockSpec((3, tb, UNIT), lambda i: (3, i, 0)),
        pl.BlockSpec((3, tb, UNIT), lambda i: (4, i, 0))]


def _mix_weight_specs():
    return [_const_spec((4, GROUP, GROUP)), _const_spec((1, UNIT)), _const_spec((1, UNIT)),
            _const_spec((UNIT, D_MODEL)), _const_spec((CONV_K_PAD, UNIT)), _const_spec((1, UNIT)),
            _const_spec((1, UNIT)), _const_spec((1, UNIT)), _const_spec((UNIT, D_MODEL)),
            _const_spec((UNIT, D_MODEL)), _const_spec((D_MODEL, D_MODEL)), _const_spec((1, D_MODEL))]


def _shifted_reader(buf, rbuf, tb):
    if rbuf is None:
        return lambda s: buf[s:s + tb, :]
    length = tb + HALO - 8
    for b in range(1, 8):
        rbuf[b, :, :] = buf[b:b + length, :]

    def read(s):
        a, b = divmod(s, 8)
        return buf[s:s + tb, :] if b == 0 else rbuf[b, 8 * a:8 * a + tb, :]

    return read


def _mix_forward(i, tb, proj_refs, o_ref, w_refs, pbuf, ubuf=None, rbuf=None, cv_ref=None):
    if cv_ref is None:
        p_ref, ph_ref, gp_ref, ca_ref, cah_ref, cb_ref, cbh_ref, gc_ref, ga_ref, gm0_ref, gm1_ref = proj_refs
    else:
        p_ref, ph_ref, gp_ref, gc_ref, ga_ref, gm0_ref, gm1_ref = proj_refs
    (poolw_ref, poolb_ref, pscale_ref, wpo_ref, convw_ref, convb_ref, lng_ref, lnb_ref,
     wco_ref, wao_ref, wo_ref, gpost_ref) = w_refs
    first = i == 0
    r = {}

    pbuf[0:HALO, :] = jnp.where(first, 0.0, ph_ref[0])
    pbuf[HALO:HALO + tb, :] = p_ref[0]
    tpos = i * tb + lax.broadcasted_iota(jnp.int32, (tb, 1), 0)
    d_parts, y_parts = [], []
    for g, win in enumerate(POOL_WINDOWS):
        cs = slice(g * GROUP, (g + 1) * GROUP)
        cur = pbuf[HALO:HALO + tb, cs]
        s = cur
        for j in range(1, win):
            s = s + pbuf[HALO - j:HALO - j + tb, cs]
        cnt = jnp.minimum(tpos + 1, win).astype(F32)
        d_g = s / cnt - cur
        d_parts.append(d_g)
        y_parts.append(_dot(d_g.astype(BF16), poolw_ref[g]))
    r["d"] = d_parts
    y = jnp.concatenate(y_parts, axis=1) + poolb_ref[...]
    r["y"] = y
    mp = y * pscale_ref[...]
    gp = gp_ref[0]
    sgp = _sigmoid(gp)
    r["mp"], r["gp"], r["sgp"] = mp, gp, sgp
    ua = mp * (gp * sgp)

    if cv_ref is None:
        ah, bh = cah_ref[0], cbh_ref[0]
        ubuf[0:HALO, :] = jnp.where(first, 0.0, ah * _sigmoid(bh))
        ubuf[HALO:HALO + tb, :] = ca_ref[0] * _sigmoid(cb_ref[0])
        cv = jnp.zeros((tb, UNIT), F32) + convb_ref[...]
        off = HALO - (CONV_K - 1)
        u_at = _shifted_reader(ubuf, rbuf, tb)
        for k in range(CONV_K):
            cv = cv + convw_ref[k:k + 1, :] * u_at(off + k)
    else:
        cv = cv_ref[...]
    r["cv"] = cv
    mu = jnp.mean(cv, axis=-1, keepdims=True)
    cc = cv - mu
    rs = lax.rsqrt(jnp.mean(cc * cc, axis=-1, keepdims=True) + LN_EPS)
    nrm = cc * rs
    ln = nrm * lng_ref[...] + lnb_ref[...]
    sln = _sigmoid(ln)
    sc = ln * sln
    gc = gc_ref[0]
    sgc = _sigmoid(gc)
    r["rs"], r["nrm"], r["ln"], r["sln"], r["sc"], r["gc"], r["sgc"] = rs, nrm, ln, sln, sc, gc, sgc
    ub = sc * (gc * sgc)

    o = o_ref[...]
    ga = ga_ref[0]
    sga = _sigmoid(ga)
    r["o"], r["ga"], r["sga"] = o, ga, sga
    uc = o * (ga * sga)

    r["ua"], r["ub"], r["uc"] = ua.astype(BF16), ub.astype(BF16), uc.astype(BF16)
    ya = _dot(r["ua"], wpo_ref[...])
    yb = _dot(r["ub"], wco_ref[...])
    yc = _dot(r["uc"], wao_ref[...])
    g0 = _sigmoid(jnp.concatenate([gm0_ref[0], gm0_ref[1]], axis=1))
    g1 = _sigmoid(jnp.concatenate([gm0_ref[2], gm1_ref[0]], axis=1))
    g2 = _sigmoid(jnp.concatenate([gm1_ref[1], gm1_ref[2]], axis=1))
    r["ya"], r["yb"], r["yc"], r["g0"], r["g1"], r["g2"] = ya, yb, yc, g0, g1, g2
    m = (g0 * ya + g1 * yb + g2 * yc).astype(BF16)
    r["m"] = m
    out = _dot(m, wo_ref[...])
    r2 = lax.rsqrt(jnp.mean(out * out, axis=-1, keepdims=True) + RMS_EPS)
    r["n2"], r["r2"] = out * r2, r2
    return r


def _mix_fwd(proj, o, x, weights):
    t = x.shape[0]
    tb = min(256, t)
    nv = N_MIX_VIEWS

    def body(*refs):
        proj_refs, o_ref, x_ref = refs[0:nv], refs[nv], refs[nv + 1]
        w_refs = refs[nv + 2:nv + 14]
        xn_ref, cv_ref, pbuf, ubuf, rbuf = refs[nv + 14:nv + 19]
        r = _mix_forward(pl.program_id(0), tb, proj_refs, o_ref, w_refs, pbuf, ubuf, rbuf)
        xn_ref[...] = x_ref[...] + r["n2"] * w_refs[11][...]
        cv_ref[...] = r["cv"]

    return pl.pallas_call(
        body, name="mix_fwd",
        grid=(t // tb,),
        in_specs=_mix_in_specs(tb) + [pl.BlockSpec((tb, UNIT), lambda i: (i, 0)),
                                      pl.BlockSpec((tb, D_MODEL), lambda i: (i, 0))] + _mix_weight_specs(),
        out_specs=[pl.BlockSpec((tb, D_MODEL), lambda i: (i, 0)), pl.BlockSpec((tb, UNIT), lambda i: (i, 0))],
        out_shape=[S((t, D_MODEL), F32), S((t, UNIT), F32)],
        scratch_shapes=[pltpu.VMEM((HALO + tb, UNIT), F32), pltpu.VMEM((HALO + tb, UNIT), F32),
                        pltpu.VMEM((8, HALO + tb - 8, UNIT), F32)],
        compiler_params=_params(1),
    )(*([proj] * nv), o, x, *weights)


def _mix_bwd(proj, o, cv, x, dxn, weights):
    t = x.shape[0]
    tb = min(256, t)
    nt = t // tb

    nv = N_MIX_VIEWS_NO_CONV

    def body(*refs):
        proj_refs, o_ref, cv_ref, x_ref, dxn_ref = refs[0:nv], refs[nv], refs[nv + 1], refs[nv + 2], refs[nv + 3]
        w_refs = refs[nv + 4:nv + 16]
        (dg_ref, dd_ref, dcv_ref, do_ref, dwo_hbm, dwp_hbm, dwc_hbm, dwa_hbm, dpw_ref,
         dvec_ref) = refs[nv + 16:nv + 26]
        pbuf, dwo_ref, dwp_ref, dwc_ref, dwa_ref = refs[nv + 26:nv + 31]
        (poolw_ref, _, pscale_ref, wpo_ref, _, _, lng_ref, _, wco_ref, wao_ref, wo_ref, gpost_ref) = w_refs
        i = pl.program_id(0)

        @pl.when(i == 0)
        def _():
            for ref in (dwo_ref, dwp_ref, dwc_ref, dwa_ref, dpw_ref, dvec_ref):
                ref[...] = jnp.zeros_like(ref)

        r = _mix_forward(i, tb, proj_refs, o_ref, w_refs, pbuf, cv_ref=cv_ref)

        def colsum(v):
            return jnp.sum(v, axis=0, keepdims=True)

        dxn = dxn_ref[...]
        n2 = r["n2"]
        dvec_ref[0:1, :] += colsum(dxn * n2)
        dn2 = dxn * gpost_ref[...]
        dout = (r["r2"] * (dn2 - n2 * jnp.mean(dn2 * n2, axis=-1, keepdims=True))).astype(BF16)
        dm = _dot_nt(dout, wo_ref[...])
        dwo_ref[...] += _dot_tn(r["m"], dout)

        g0, g1, g2 = r["g0"], r["g1"], r["g2"]
        dgm = [dm * r["ya"] * g0 * (1.0 - g0), dm * r["yb"] * g1 * (1.0 - g1), dm * r["yc"] * g2 * (1.0 - g2)]
        for bidx in range(N_BRANCH):
            dg_ref[3 + 2 * bidx] = dgm[bidx][:, 0:UNIT]
            dg_ref[4 + 2 * bidx] = dgm[bidx][:, UNIT:2 * UNIT]
        dya = (dm * g0).astype(BF16)
        dyb = (dm * g1).astype(BF16)
        dyc = (dm * g2).astype(BF16)
        dua = _dot_nt(dya, wpo_ref[...])
        dub = _dot_nt(dyb, wco_ref[...])
        duc = _dot_nt(dyc, wao_ref[...])
        dwp_ref[...] += _dot_tn(r["ua"], dya)
        dwc_ref[...] += _dot_tn(r["ub"], dyb)
        dwa_ref[...] += _dot_tn(r["uc"], dyc)

        gp, sgp = r["gp"], r["sgp"]
        dmp = dua * (gp * sgp)
        dg_ref[0] = dua * r["mp"] * _dsilu(gp, sgp)
        dvec_ref[2:3, 0:UNIT] += colsum(dmp * r["y"])
        dy = dmp * pscale_ref[...]
        dvec_ref[1:2, 0:UNIT] += colsum(dy)
        dd_parts = []
        for g in range(len(POOL_WINDOWS)):
            dy_g = dy[:, g * GROUP:(g + 1) * GROUP].astype(BF16)
            dd_parts.append(_dot_nt(dy_g, poolw_ref[g]))
            dpw_ref[g] += _dot_tn(r["d"][g].astype(BF16), dy_g)
        dd_ref[...] = jnp.concatenate(dd_parts, axis=1)

        gc, sgc = r["gc"], r["sgc"]
        dsc = dub * (gc * sgc)
        dg_ref[1] = dub * r["sc"] * _dsilu(gc, sgc)
        dln = dsc * _dsilu(r["ln"], r["sln"])
        nrm = r["nrm"]
        dvec_ref[4:5, 0:UNIT] += colsum(dln * nrm)
        dvec_ref[5:6, 0:UNIT] += colsum(dln)
        dnrm = dln * lng_ref[...]
        dcv = r["rs"] * (dnrm - jnp.mean(dnrm, axis=-1, keepdims=True)
                         - nrm * jnp.mean(dnrm * nrm, axis=-1, keepdims=True))
        dvec_ref[3:4, 0:UNIT] += colsum(dcv)
        dcv_ref[...] = dcv

        ga, sga = r["ga"], r["sga"]
        do_ref[...] = duc * (ga * sga)
        dg_ref[2] = duc * r["o"] * _dsilu(ga, sga)

        @pl.when(i == nt - 1)
        def _():
            for acc, hbm in ((dwo_ref, dwo_hbm), (dwp_ref, dwp_hbm), (dwc_ref, dwc_hbm), (dwa_ref, dwa_hbm)):
                pltpu.sync_copy(acc, hbm)

    def acc_spec(shape):
        n = len(shape)
        return pl.BlockSpec(shape, lambda i: (0,) * n)

    tok = lambda w: pl.BlockSpec((tb, w), lambda i: (i, 0))
    any_spec = pl.BlockSpec(memory_space=pl.ANY)
    w_shapes = [(D_MODEL, D_MODEL), (UNIT, D_MODEL), (UNIT, D_MODEL), (UNIT, D_MODEL)]
    return pl.pallas_call(
        body, name="mix_bwd",
        grid=(nt,),
        in_specs=_mix_in_specs(tb, conv_inputs=False) + [tok(UNIT), tok(UNIT), tok(D_MODEL), tok(D_MODEL)]
        + _mix_weight_specs(),
        out_specs=[pl.BlockSpec((9, tb, UNIT), lambda i: (0, i, 0)), tok(UNIT), tok(UNIT), tok(UNIT)]
        + [any_spec] * 4 + [acc_spec((4, GROUP, GROUP)), acc_spec((8, D_MODEL))],
        out_shape=[S((9, t, UNIT), F32), S((t, UNIT), F32), S((t, UNIT), F32), S((t, UNIT), F32)]
        + [S(s, F32) for s in w_shapes] + [S((4, GROUP, GROUP), F32), S((8, D_MODEL), F32)],
        scratch_shapes=[pltpu.VMEM((HALO + tb, UNIT), F32)] + [pltpu.VMEM(s, F32) for s in w_shapes],
        compiler_params=_params(1),
    )(*([proj] * nv), o, cv, x, dxn, *weights)


def _halo_bwd(proj, dcv, dd, conv_w):
    t = dcv.shape[0]
    tb = min(256, t)
    hb = tb // HALO
    n_halo_blocks = t // HALO
    nt = t // tb

    def body(ca_ref, cah_ref, cb_ref, cbh_ref, dcv_ref, dcvh_ref, dd_ref, ddh_ref, cw_ref,
             dpre_ref, dcw_ref, ubuf, gbuf, nbuf, ru_buf, rg_buf, acc_ref):
        i = pl.program_id(0)
        first = i == 0
        last = i == nt - 1

        @pl.when(first)
        def _():
            acc_ref[...] = jnp.zeros_like(acc_ref)

        a, b = ca_ref[0], cb_ref[0]
        sb = _sigmoid(b)
        ubuf[0:HALO, :] = jnp.where(first, 0.0, cah_ref[0] * _sigmoid(cbh_ref[0]))
        ubuf[HALO:HALO + tb, :] = a * sb
        dcv_main = dcv_ref[...]
        gbuf[0:tb, :] = dcv_main
        gbuf[tb:tb + HALO, :] = jnp.where(last, 0.0, dcvh_ref[...])

        off = HALO - (CONV_K - 1)
        u_at = _shifted_reader(ubuf, ru_buf, tb)
        g_at = _shifted_reader(gbuf, rg_buf, tb)
        du = jnp.zeros((tb, UNIT), F32)
        for k in range(CONV_K):
            du = du + cw_ref[k:k + 1, :] * g_at(CONV_K - 1 - k)
            acc_ref[k] += jnp.sum((dcv_main * u_at(off + k)).reshape(tb // 8, 8, UNIT), axis=0)
        dpre_ref[1] = du * sb
        dpre_ref[2] = du * a * sb * (1.0 - sb)

        @pl.when(last)
        def _():
            dcw_ref[...] = jnp.sum(acc_ref[...], axis=1)

        tpos = i * tb + lax.broadcasted_iota(jnp.int32, (tb + HALO, 1), 0)
        dd_main = dd_ref[...]
        dd_ext = jnp.concatenate([dd_main, jnp.where(last, 0.0, ddh_ref[...])], axis=0)
        dp_parts = []
        for g, win in enumerate(POOL_WINDOWS):
            cs = slice(g * GROUP, (g + 1) * GROUP)
            cnt = jnp.minimum(tpos + 1, win).astype(F32)
            nbuf[:, cs] = dd_ext[:, cs] / cnt
        for g, win in enumerate(POOL_WINDOWS):
            cs = slice(g * GROUP, (g + 1) * GROUP)
            s = nbuf[0:tb, cs]
            for j in range(1, win):
                s = s + nbuf[j:j + tb, cs]
            dp_parts.append(s - dd_main[:, cs])
        dpre_ref[0] = jnp.concatenate(dp_parts, axis=1)

    def unit(u):
        return pl.BlockSpec((1, tb, UNIT), lambda i: (u, i, 0))

    def past(u):
        return pl.BlockSpec((1, HALO, UNIT), lambda i: (u, jnp.maximum(i * hb - 1, 0), 0))

    tok = pl.BlockSpec((tb, UNIT), lambda i: (i, 0))
    future = pl.BlockSpec((HALO, UNIT), lambda i: (jnp.minimum((i + 1) * hb, n_halo_blocks - 1), 0))
    return pl.pallas_call(
        body, name="halo_bwd",
        grid=(nt,),
        in_specs=[unit(U_CA), past(U_CA), unit(U_CB), past(U_CB), tok, future, tok, future,
                  _const_spec((CONV_K_PAD, UNIT))],
        out_specs=[pl.BlockSpec((3, tb, UNIT), lambda i: (0, i, 0)),
                   pl.BlockSpec((CONV_K_PAD, UNIT), lambda i: (0, 0))],
        out_shape=[S((3, t, UNIT), F32), S((CONV_K_PAD, UNIT), F32)],
        scratch_shapes=[pltpu.VMEM((HALO + tb, UNIT), F32)] * 3
        + [pltpu.VMEM((8, HALO + tb - 8, UNIT), F32)] * 2 + [pltpu.VMEM((CONV_K_PAD, 8, UNIT), F32)],
        compiler_params=_params(1),
    )(proj, proj, proj, proj, dcv, dcv, dd, dd, conv_w)


def _loss_head(y, target):
    t = y.shape[0]
    tb = min(512, t)

    def body(y_ref, t_ref, loss_ref, dy_ref):
        @pl.when(pl.program_id(0) == 0)
        def _():
            loss_ref[...] = jnp.zeros_like(loss_ref)

        err = y_ref[...] - t_ref[...]
        dy_ref[...] = err * (1.0 / D_MODEL)
        part = 0.5 * jnp.sum(jnp.mean(err * err, axis=-1, keepdims=True), axis=0, keepdims=True)
        r8 = lax.broadcasted_iota(jnp.int32, (8, LANES), 0)
        c8 = lax.broadcasted_iota(jnp.int32, (8, LANES), 1)
        loss_ref[...] += jnp.where((r8 == 0) & (c8 == 0), part, 0.0)

    return pl.pallas_call(
        body, name="loss_head",
        grid=(t // tb,),
        in_specs=[pl.BlockSpec((tb, D_MODEL), lambda i: (i, 0))] * 2,
        out_specs=[pl.BlockSpec((8, LANES), lambda i: (0, 0)), pl.BlockSpec((tb, D_MODEL), lambda i: (i, 0))],
        out_shape=[S((8, LANES), F32), S((t, D_MODEL), F32)],
        compiler_params=_params(1),
    )(y, target)


def _adamw(name, parts, w, m, v):
    n, rows, cols = parts.shape
    rb = rows
    while rb * cols * 4 * (n + 7) * 2 > 24 * 1024 * 1024 and rb % 16 == 0:
        rb //= 2

    def body(p_ref, w_ref, m_ref, v_ref, g_ref, d_ref, nm_ref, nv_ref):
        g = p_ref[0]
        for k in range(1, n):
            g = g + p_ref[k]
        nm = ADAM_B1 * m_ref[...] + (1.0 - ADAM_B1) * g
        nv = ADAM_B2 * v_ref[...] + (1.0 - ADAM_B2) * (g * g)
        m_hat = nm / (1.0 - ADAM_B1 ** ADAM_STEP)
        v_hat = nv / (1.0 - ADAM_B2 ** ADAM_STEP)
        g_ref[...] = g
        d_ref[...] = -ADAM_LR * (m_hat / (jnp.sqrt(v_hat) + ADAM_EPS) + ADAM_WD * w_ref[...])
        nm_ref[...] = nm
        nv_ref[...] = nv

    blk = pl.BlockSpec((rb, cols), lambda i: (i, 0))
    return pl.pallas_call(
        body, name=name,
        grid=(rows // rb,),
        in_specs=[pl.BlockSpec((n, rb, cols), lambda i: (0, i, 0)), blk, blk, blk],
        out_specs=[blk] * 4,
        out_shape=[S((rows, cols), F32)] * 4,
        compiler_params=_params(1),
    )(parts, w, m, v)


def _axis_slice(ref, axis, block, size):
    idx = [slice(None)] * len(ref.shape)
    idx[axis] = pl.ds(pl.multiple_of(block * size, size), size)
    return ref.at[tuple(idx)]


def _axis_index(ref, axis, index):
    return ref.at[tuple([slice(None)] * axis + [index])]


COPY_PIECE_BYTES = 2 * 1024 * 1024


def _row_pieces(shape):
    rows = shape[-2]
    size = 4 * rows * shape[-1]
    for d in shape[:-2]:
        size *= d
    n = 1
    while size // n > COPY_PIECE_BYTES and rows % (2 * n * 8) == 0:
        n *= 2
    return [(k * (rows // n), rows // n) for k in range(n)]


def _rows(ref, r0, rows):
    idx = [slice(None)] * len(ref.shape)
    idx[-2] = pl.ds(r0, rows)
    return ref.at[tuple(idx)]


def _staged_local_copies(pairs, bufs, in_sems, out_sems):
    ins = [pltpu.make_async_copy(src, buf, in_sems.at[k]) for k, ((src, _), buf) in enumerate(zip(pairs, bufs))]
    for cp in ins:
        cp.start()
    for cp in ins:
        cp.wait()
    outs = [pltpu.make_async_copy(buf, dst, out_sems.at[k]) for k, ((_, dst), buf) in enumerate(zip(pairs, bufs))]
    for cp in outs:
        cp.start()
    return outs


def _run_copies(local, sends, recvs):
    for cp in local + sends:
        cp.start()
    for cp in recvs:
        cp.wait_recv()
    for cp in sends:
        cp.wait_send()
    for cp in local:
        cp.wait()


def _gather_weights(shards, axes):
    n = len(shards)
    fulls = [S(tuple(4 * d if k == ax else d for k, d in enumerate(s.shape)), s.dtype) for s, ax in zip(shards, axes)]

    def body(*refs):
        src, dst = refs[0:n], refs[n:2 * n]
        send_sems, recv_sems, pass_send_sems, pass_recv_sems, in_sems, out_sems = refs[2 * n:2 * n + 6]
        bufs = refs[2 * n + 6:3 * n + 6]
        x, y, c = lax.axis_index("x"), lax.axis_index("y"), lax.axis_index("c")
        chips = [(1 - x, y), (x, 1 - y), (1 - x, 1 - y)]

        def place(a, chip, layer=None):
            ref = _axis_slice(dst[a], axes[a], 2 * chip[0] + chip[1], src[a].shape[axes[a]])
            return ref if layer is None else ref.at[pl.ds(layer, 1)]

        def over_ici(k, chip_of_block, to):
            a, j, r0, rows = pieces[k]
            return pltpu.make_async_remote_copy(
                src_ref=_rows(src[a].at[pl.ds(c, 1)], r0, rows), dst_ref=_rows(place(a, chip_of_block, c), r0, rows),
                send_sem=send_sems.at[k], recv_sem=recv_sems.at[k], device_id=(to[0], to[1], c), device_id_type=MESH)

        def to_sibling(k, chip_of_block, layer):
            a, j, r0, rows = pieces[k]
            block = _rows(place(a, chip_of_block, layer), r0, rows)
            return pltpu.make_async_remote_copy(
                src_ref=block, dst_ref=block, send_sem=pass_send_sems.at[k], recv_sem=pass_recv_sems.at[k],
                device_id=(x, y, 1 - c), device_id_type=MESH)

        first = [over_ici(k, (x, y), chips[j]) for k, (a, j, r0, rows) in enumerate(pieces)]
        for cp in first:
            cp.start()
        local = _staged_local_copies([(src[a], place(a, (x, y))) for a in range(n)], bufs, in_sems, out_sems)
        passed = []
        for k, (a, j, r0, rows) in enumerate(pieces):
            over_ici(k, chips[j], chips[j]).wait_recv()
            passed.append(to_sibling(k, chips[j], c))
            passed[-1].start()
        for k, (a, j, r0, rows) in enumerate(pieces):
            to_sibling(k, chips[j], 1 - c).wait_recv()
        for cp in first + passed:
            cp.wait_send()
        for cp in local:
            cp.wait()

    per_array = [_row_pieces((1,) + shards[a].shape[1:]) for a in range(n)]
    pieces = [(a, j, r0, rows) for a in range(n) for r0, rows in per_array[a] for j in range(3)]
    any_spec = pl.BlockSpec(memory_space=pl.ANY)
    return pl.pallas_call(
        body, name="gather_weights",
        in_specs=[any_spec] * n, out_specs=[any_spec] * n, out_shape=fulls,
        scratch_shapes=[pltpu.SemaphoreType.DMA((len(pieces),))] * 4 + [pltpu.SemaphoreType.DMA((n,))] * 2
        + [pltpu.VMEM(s.shape, s.dtype) for s in shards],
        compiler_params=_params(0),
    )(*shards)


def _swap_halves(partials, half_axes):
    n = len(partials)

    def half_shape(a):
        return tuple(d // 2 if k == half_axes[a] else d for k, d in enumerate(partials[a].shape))

    pieces = [(a, r0, rows) for a in range(n) for r0, rows in _row_pieces(half_shape(a))]

    def body(*refs):
        src, dst = refs[0:n], refs[n:2 * n]
        send_sems, recv_sems = refs[2 * n:2 * n + 2]
        x, y, c = lax.axis_index("x"), lax.axis_index("y"), lax.axis_index("c")

        def remote(k, h):
            a, r0, rows = pieces[k]
            half = _axis_slice(src[a], half_axes[a], h, src[a].shape[half_axes[a]] // 2)
            return pltpu.make_async_remote_copy(
                src_ref=_rows(half, r0, rows), dst_ref=_rows(dst[a], r0, rows),
                send_sem=send_sems.at[k], recv_sem=recv_sems.at[k], device_id=(x, y, 1 - c), device_id_type=MESH)

        sends = [remote(k, 1 - c) for k in range(len(pieces))]
        _run_copies([], sends, [remote(k, c) for k in range(len(pieces))])

    any_spec = pl.BlockSpec(memory_space=pl.ANY)
    return pl.pallas_call(
        body, name="swap_halves",
        in_specs=[any_spec] * n, out_specs=[any_spec] * n,
        out_shape=[S(half_shape(a), partials[a].dtype) for a in range(n)],
        scratch_shapes=[pltpu.SemaphoreType.DMA((len(pieces),))] * 2,
    )(*partials)


def _add_own_layer(name, mine, theirs, layer):
    _, rows, cols = mine.shape
    rb = rows
    while rb * cols * 4 * 3 * 2 > 24 * 1024 * 1024 and rb % 32 == 0:
        rb //= 2

    def body(layer_ref, m_ref, t_ref, o_ref):
        o_ref[...] = (m_ref[...] + t_ref[...]).astype(BF16)

    return pl.pallas_call(
        body, name=name,
        grid_spec=pltpu.PrefetchScalarGridSpec(
            num_scalar_prefetch=1, grid=(rows // rb,),
            in_specs=[pl.BlockSpec((1, rb, cols), lambda i, l: (l[0], i, 0)),
                      pl.BlockSpec((1, rb, cols), lambda i, l: (0, i, 0))],
            out_specs=pl.BlockSpec((1, rb, cols), lambda i, l: (0, i, 0))),
        out_shape=S((1, rows, cols), BF16),
        compiler_params=_params(1),
    )(layer, mine, theirs)


def _send_to_owners(chip_halves, shard_axes, packed):
    n = len(chip_halves)

    def owned_shape(a):
        return tuple(d // 4 if k == shard_axes[a] else d for k, d in enumerate(chip_halves[a].shape))

    flips = [(dx, dy, dc) for dx in (0, 1) for dy in (0, 1) for dc in (0, 1)][1:]

    def body(*refs):
        src, psrc, dst, pdst = refs[0:n], refs[n], refs[n + 1:2 * n + 1], refs[2 * n + 1]
        send_sems, recv_sems, local_sems, psend_sems, precv_sems = refs[2 * n + 2:2 * n + 7]
        x, y, c = lax.axis_index("x"), lax.axis_index("y"), lax.axis_index("c")
        my_chip = 2 * x + y
        chips = [(1 - x, y), (x, 1 - y), (1 - x, 1 - y)]

        def owned(a, chip):
            return _axis_slice(src[a], shard_axes[a], 2 * chip[0] + chip[1], src[a].shape[shard_axes[a]] // 4)

        def remote(a, j, to, from_chip):
            return pltpu.make_async_remote_copy(
                src_ref=owned(a, to), dst_ref=dst[a].at[from_chip], send_sem=send_sems.at[a, j],
                recv_sem=recv_sems.at[a, j], device_id=(to[0], to[1], c), device_id_type=MESH)

        def peer(f):
            return (1 - x if f[0] else x, 1 - y if f[1] else y, 1 - c if f[2] else c)

        def premote(j, to, from_dev):
            return pltpu.make_async_remote_copy(
                src_ref=psrc, dst_ref=pdst.at[from_dev], send_sem=psend_sems.at[j], recv_sem=precv_sems.at[j],
                device_id=to, device_id_type=MESH)

        local = [pltpu.make_async_copy(_rows(owned(a, (x, y)), r0, rows), _rows(dst[a].at[my_chip], r0, rows),
                                       local_sems.at[k]) for k, (a, r0, rows) in enumerate(own_pieces)]
        local.append(pltpu.make_async_copy(psrc, pdst.at[4 * x + 2 * y + c], local_sems.at[len(own_pieces)]))
        sends = [remote(a, j, chip, my_chip) for a in range(n) for j, chip in enumerate(chips)]
        sends += [premote(j, peer(f), 4 * x + 2 * y + c) for j, f in enumerate(flips)]
        recvs = [remote(a, j, (x, y), 2 * chip[0] + chip[1]) for a in range(n) for j, chip in enumerate(chips)]
        for j, f in enumerate(flips):
            px, py, pc = peer(f)
            recvs.append(premote(j, (x, y, c), 4 * px + 2 * py + pc))
        _run_copies(local, sends, recvs)

    own_pieces = [(a, r0, rows) for a in range(n) for r0, rows in _row_pieces(owned_shape(a))]
    any_spec = pl.BlockSpec(memory_space=pl.ANY)
    return pl.pallas_call(
        body, name="send_to_owners",
        in_specs=[any_spec] * (n + 1), out_specs=[any_spec] * (n + 1),
        out_shape=[S((4,) + owned_shape(a), chip_halves[a].dtype) for a in range(n)]
        + [S((N_DEV,) + packed.shape, packed.dtype)],
        scratch_shapes=[pltpu.SemaphoreType.DMA((n, 3)), pltpu.SemaphoreType.DMA((n, 3)),
                        pltpu.SemaphoreType.DMA((len(own_pieces) + 1,)), pltpu.SemaphoreType.DMA((7,)),
                        pltpu.SemaphoreType.DMA((7,))],
    )(*chip_halves, packed)


def _join_halves(reduced, half_axes):
    n = len(reduced)

    def joined_shape(a):
        s, ax = reduced[a].shape, half_axes[a]
        return s[:ax] + (2,) + s[ax:]

    pieces = [(a, r0, rows) for a in range(n) for r0, rows in _row_pieces(reduced[a].shape)]

    def body(*refs):
        src, dst = refs[0:n], refs[n:2 * n]
        send_sems, recv_sems, in_sems, out_sems = refs[2 * n:2 * n + 4]
        bufs = refs[2 * n + 4:3 * n + 4]
        x, y, c = lax.axis_index("x"), lax.axis_index("y"), lax.axis_index("c")

        def remote(k, h):
            a, r0, rows = pieces[k]
            return pltpu.make_async_remote_copy(
                src_ref=_rows(src[a], r0, rows), dst_ref=_rows(_axis_index(dst[a], half_axes[a], h), r0, rows),
                send_sem=send_sems.at[k], recv_sem=recv_sems.at[k], device_id=(x, y, 1 - c), device_id_type=MESH)

        sends = [remote(k, c) for k in range(len(pieces))]
        for cp in sends:
            cp.start()
        own = [(src[a], _axis_index(dst[a], half_axes[a], c)) for a in range(n)]
        local = _staged_local_copies(own, bufs, in_sems, out_sems)
        for k in range(len(pieces)):
            remote(k, 1 - c).wait_recv()
        for cp in sends:
            cp.wait_send()
        for cp in local:
            cp.wait()

    any_spec = pl.BlockSpec(memory_space=pl.ANY)
    return pl.pallas_call(
        body, name="join_halves",
        in_specs=[any_spec] * n, out_specs=[any_spec] * n,
        out_shape=[S(joined_shape(a), reduced[a].dtype) for a in range(n)],
        scratch_shapes=[pltpu.SemaphoreType.DMA((len(pieces),))] * 2 + [pltpu.SemaphoreType.DMA((n,))] * 2
        + [pltpu.VMEM(r.shape, r.dtype) for r in reduced],
        compiler_params=_params(0),
    )(*reduced)


def _sum_slots(name, parts):
    n = parts.shape[0]
    shape = parts.shape[1:]
    flat = parts.reshape((n, -1, shape[-1]))
    rows, cols = flat.shape[1:]
    rb = rows
    while rb * cols * 4 * (n + 1) * 2 > 24 * 1024 * 1024 and rb % 32 == 0:
        rb //= 2

    def body(p_ref, o_ref):
        acc = p_ref[0].astype(F32)
        for k in range(1, n):
            acc = acc + p_ref[k].astype(F32)
        o_ref[...] = acc

    out = pl.pallas_call(
        body, name=name,
        grid=(rows // rb,),
        in_specs=[pl.BlockSpec((n, rb, cols), lambda i: (0, i, 0))],
        out_specs=pl.BlockSpec((rb, cols), lambda i: (i, 0)),
        out_shape=S((rows, cols), F32),
        compiler_params=_params(1),
    )(flat)
    return out.reshape(shape)


SMALL_NAMES = ("norm_pre", "pool_w", "pool_b", "pool_scale", "conv_b", "conv_ln_g", "conv_ln_b", "norm_post")
BIG_NAMES = ("w_in", "w_pool_out", "conv_w", "w_conv_out", "w_attn_out", "w_o")
SHARD_AXIS = {"w_in": 2, "w_pool_out": 2, "conv_w": 2, "w_conv_out": 2, "w_attn_out": 2, "w_o": 1}
WEIGHT_ORDER = ("norm_pre", "w_in", "pool_w", "pool_b", "pool_scale", "w_pool_out", "conv_w", "conv_b",
                "conv_ln_g", "conv_ln_b", "w_conv_out", "w_attn_out", "w_o", "norm_post")


def _pack_small(parts):
    return jnp.concatenate([parts[n].reshape(-1, LANES) for n in SMALL_NAMES], axis=0)


def _unpack_small(packed, shapes):
    out, r0 = {}, 0
    for n in SMALL_NAMES:
        size = 1
        for d in shapes[n]:
            size *= d
        rows = size // LANES
        out[n] = packed[r0:r0 + rows].reshape(shapes[n])
        r0 += rows
    return out


def _layer_weights(full, small, l):
    row = lambda a: a[l][None, :]
    return (small["pool_w"][l].astype(BF16), small["pool_b"][l].reshape(1, UNIT), row(small["pool_scale"]),
            full["w_pool_out"][l], full["conv_w"][l], row(small["conv_b"]), row(small["conv_ln_g"]),
            row(small["conv_ln_b"]), full["w_conv_out"][l], full["w_attn_out"][l], full["w_o"][l],
            row(small["norm_post"]))


def _forward_backward(x, target, full, small):
    depth = full["w_in"].shape[0]
    acts = []
    for l in range(depth):
        g_pre = small["norm_pre"][l][None, :]
        proj = _inproj_fwd(x, g_pre, full["w_in"][l])
        o, bsum, kfirst = _attn_fwd(proj)
        weights = _layer_weights(full, small, l)
        x_next, cv = _mix_fwd(proj, o, x, weights)
        acts.append((x, proj, o, cv, bsum, kfirst, weights, g_pre))
        x = x_next
    loss_tile, dx = _loss_head(x, target)

    big = {n: [None] * depth for n in BIG_NAMES}
    sm = {n: [None] * depth for n in SMALL_NAMES}
    for l in reversed(range(depth)):
        x_in, proj, o, cv, bsum, kfirst, weights, g_pre = acts[l]
        dgates, dd, dcv, d_o, dwo, dwp, dwc, dwa, dpw, dvec = _mix_bwd(proj, o, cv, x_in, dx, weights)
        dpre, dcw = _halo_bwd(proj, dcv, dd, full["conv_w"][l])
        dq, dk, dv = _attn_bwd(proj, d_o, bsum, kfirst)
        dproj_parts = (dpre, dgates, dq, dk, dv)
        dx, dg_pre = _inproj_bwd_x(x_in, g_pre, full["w_in"][l], dproj_parts, dx)
        big["w_in"][l] = _inproj_bwd_w(x_in, g_pre, dproj_parts)
        big["w_pool_out"][l], big["w_conv_out"][l], big["w_attn_out"][l], big["w_o"][l] = dwp, dwc, dwa, dwo
        big["conv_w"][l] = dcw
        sm["norm_pre"][l] = dg_pre[0]
        sm["pool_w"][l] = dpw
        sm["norm_post"][l] = dvec[0]
        sm["pool_b"][l] = dvec[1, 0:UNIT].reshape(4, GROUP)
        sm["pool_scale"][l] = dvec[2, 0:UNIT]
        sm["conv_b"][l] = dvec[3, 0:UNIT]
        sm["conv_ln_g"][l] = dvec[4, 0:UNIT]
        sm["conv_ln_b"][l] = dvec[5, 0:UNIT]
    big = {n: jnp.stack(v) for n, v in big.items()}
    sm = {n: jnp.stack(v) for n, v in sm.items()}
    return loss_tile, dx, big, sm


def kernel(x, norm_pre, w_in, pool_w, pool_b, pool_scale, w_pool_out, conv_w, conv_b, conv_ln_g, conv_ln_b, w_conv_out, w_attn_out, w_o, norm_post, loss_target, m_norm_pre, m_w_in, m_pool_w, m_pool_b, m_pool_scale, m_w_pool_out, m_conv_w, m_conv_b, m_conv_ln_g, m_conv_ln_b, m_w_conv_out, m_w_attn_out, m_w_o, m_norm_post, v_norm_pre, v_w_in, v_pool_w, v_pool_b, v_pool_scale, v_w_pool_out, v_conv_w, v_conv_b, v_conv_ln_g, v_conv_ln_b, v_w_conv_out, v_w_attn_out, v_w_o, v_norm_post):
    w = dict(norm_pre=norm_pre, w_in=w_in, pool_w=pool_w, pool_b=pool_b, pool_scale=pool_scale,
             w_pool_out=w_pool_out, conv_w=conv_w, conv_b=conv_b, conv_ln_g=conv_ln_g, conv_ln_b=conv_ln_b,
             w_conv_out=w_conv_out, w_attn_out=w_attn_out, w_o=w_o, norm_post=norm_post)
    m = dict(norm_pre=m_norm_pre, w_in=m_w_in, pool_w=m_pool_w, pool_b=m_pool_b, pool_scale=m_pool_scale,
             w_pool_out=m_w_pool_out, conv_w=m_conv_w, conv_b=m_conv_b, conv_ln_g=m_conv_ln_g,
             conv_ln_b=m_conv_ln_b, w_conv_out=m_w_conv_out, w_attn_out=m_w_attn_out, w_o=m_w_o,
             norm_post=m_norm_post)
    v = dict(norm_pre=v_norm_pre, w_in=v_w_in, pool_w=v_pool_w, pool_b=v_pool_b, pool_scale=v_pool_scale,
             w_pool_out=v_w_pool_out, conv_w=v_conv_w, conv_b=v_conv_b, conv_ln_g=v_conv_ln_g,
             conv_ln_b=v_conv_ln_b, w_conv_out=v_w_conv_out, w_attn_out=v_w_attn_out, w_o=v_w_o,
             norm_post=v_norm_post)
    pad_taps = lambda a: jnp.pad(a, ((0, 0), (0, CONV_K_PAD - CONV_K), (0, 0)))

    shards = [pad_taps(w[n]) if n == "conv_w" else w[n].astype(BF16) for n in BIG_NAMES]
    full = dict(zip(BIG_NAMES, _gather_weights(shards, [SHARD_AXIS[n] for n in BIG_NAMES])))
    small = {n: w[n] for n in SMALL_NAMES}

    loss_tile, dx, big, sm = _forward_backward(x[0], loss_target[0], full, small)

    packed = jnp.concatenate([_pack_small(sm), loss_tile], axis=0)
    layer_axis = [0] * len(BIG_NAMES)
    theirs = _swap_halves([big[n] for n in BIG_NAMES], layer_axis)
    my_layer = lax.axis_index("c").astype(jnp.int32).reshape(1)
    chip_sums = [_add_own_layer("sum_cores_" + n, big[n], t, my_layer) for n, t in zip(BIG_NAMES, theirs)]
    exchanged = _send_to_owners(chip_sums, [SHARD_AXIS[n] for n in BIG_NAMES], packed)
    reduced = [_sum_slots("sum_chips_" + n, p) for n, p in zip(BIG_NAMES, exchanged[:-1])]
    joined = _join_halves(reduced, layer_axis)
    out = {}
    for n, g in zip(BIG_NAMES, joined):
        wn, mn, vn = (pad_taps(a[n]) if n == "conv_w" else a[n] for a in (w, m, v))
        shape = wn.shape
        flat = lambda a: a.reshape((-1, shape[-1]))
        res = _adamw("adamw_" + n, g.reshape((1, -1, shape[-1])), flat(wn), flat(mn), flat(vn))
        res = [r.reshape(shape) for r in res]
        out[n] = [r[:, :CONV_K] for r in res] if n == "conv_w" else res
    parts = exchanged[-1]
    n_small_rows = parts.shape[1] - 8
    zeros_tile = jnp.zeros((8, LANES), F32)
    packs = [jnp.concatenate([_pack_small(a), zeros_tile], axis=0) for a in (w, m, v)]
    res = _adamw("adamw_small", parts, *packs)
    loss = res[0][n_small_rows, 0]
    shapes = {n: w[n].shape for n in SMALL_NAMES}
    unpacked = [_unpack_small(r[:n_small_rows], shapes) for r in res]
    for n in SMALL_NAMES:
        out[n] = [u[n] for u in unpacked]

    grads = [out[n][0] for n in WEIGHT_ORDER]
    deltas = [out[n][1] for n in WEIGHT_ORDER]
    new_m = [out[n][2] for n in WEIGHT_ORDER]
    new_v = [out[n][3] for n in WEIGHT_ORDER]
    return (loss, dx[None], *grads, *deltas, *new_m, *new_v)
```

```python
import jax
import jax.numpy as jnp
from jax import lax
from jax.experimental import pallas as pl
from jax.experimental.pallas import tpu as pltpu

F32 = jnp.float32
BF16 = jnp.bfloat16

D_MODEL = 1024
UNIT = 512
N_UNITS = 15
IN_WIDTH = UNIT * N_UNITS
N_HEADS = 8
HEAD_DIM = 64
HEADS_PER_BLOCK = 2
N_HEAD_BLOCKS = N_HEADS // HEADS_PER_BLOCK
LANES = 128
CONV_K = 31
CONV_K_PAD = 32
HALO = 32
POOL_WINDOWS = (2, 4, 8, 16)
GROUP = 128
N_BRANCH = 3
RMS_EPS = 1e-6
LN_EPS = 1e-5
ATTN_SCALE = 0.125
EXP_ZERO_BELOW = -104.0

ADAM_LR = 0.001
ADAM_B1 = 0.9
ADAM_B2 = 0.999
ADAM_EPS = 1e-08
ADAM_WD = 0.01
ADAM_STEP = 10

U_P, U_GP, U_CA, U_CB, U_GC, U_Q, U_K, U_V, U_GA, U_GM = 0, 1, 2, 3, 4, 5, 6, 7, 8, 9

V7X_VMEM_LIMIT = 62 * 1024 * 1024
N_DEV = 8
MESH = pl.DeviceIdType.MESH

S = jax.ShapeDtypeStruct


def _params(n_grid):
    return pltpu.CompilerParams(dimension_semantics=("arbitrary",) * n_grid, vmem_limit_bytes=V7X_VMEM_LIMIT)


def _sigmoid(x):
    return 1.0 / (1.0 + jnp.exp(-x))


def _dsilu(x, s):
    return s * (1.0 + x * (1.0 - s))


def _dot(a, b):
    return jnp.dot(a, b, preferred_element_type=F32)


def _dot_nt(a, b):
    return lax.dot_general(a, b, (((1,), (1,)), ((), ())), preferred_element_type=F32)


def _dot_tn(a, b):
    return lax.dot_general(a, b, (((0,), (0,)), ((), ())), preferred_element_type=F32)


def _split_bf16(x):
    hi = x.astype(BF16)
    lo = (x - hi.astype(F32)).astype(BF16)
    return hi, lo


def _const_spec(shape):
    n = len(shape)
    return pl.BlockSpec(shape, lambda *_: (0,) * n, pipeline_mode=pl.Buffered(1))


def _inproj_fwd(x, g_pre, w_in):
    t = x.shape[0]
    tb = min(1024, t)
    ug = 3
    nb = ug * UNIT

    def body(x_ref, g_ref, w_ref, o_ref, h_ref):
        @pl.when(pl.program_id(1) == 0)
        def _():
            xf = x_ref[...]
            r = lax.rsqrt(jnp.mean(xf * xf, axis=-1, keepdims=True) + RMS_EPS)
            h_ref[...] = (xf * r * g_ref[...]).astype(BF16)

        acc = _dot(h_ref[...], w_ref[...])
        for u in range(ug):
            o_ref[u] = acc[:, u * UNIT:(u + 1) * UNIT]

    return pl.pallas_call(
        body, name="inproj_fwd",
        grid=(t // tb, N_UNITS // ug),
        in_specs=[pl.BlockSpec((tb, D_MODEL), lambda i, j: (i, 0)),
                  pl.BlockSpec((1, D_MODEL), lambda i, j: (0, 0)),
                  pl.BlockSpec((D_MODEL, nb), lambda i, j: (0, j))],
        out_specs=pl.BlockSpec((ug, tb, UNIT), lambda i, j: (j, i, 0)),
        out_shape=S((N_UNITS, t, UNIT), F32),
        scratch_shapes=[pltpu.VMEM((tb, D_MODEL), BF16)],
        compiler_params=_params(2),
    )(x, g_pre, w_in)


def _dproj_specs(tb):
    tok = pl.BlockSpec((tb, UNIT), lambda i: (i, 0))
    return [pl.BlockSpec((3, tb, UNIT), lambda i: (0, i, 0)), pl.BlockSpec((9, tb, UNIT), lambda i: (0, i, 0)),
            tok, tok, tok]


def _dproj_unit(u, dpre_ref, dgates_ref, dq_ref, dk_ref, dv_ref):
    pre = {U_P: 0, U_CA: 1, U_CB: 2}
    gate = {U_GP: 0, U_GC: 1, U_GA: 2}
    if u in pre:
        val = dpre_ref[pre[u]]
    elif u in gate:
        val = dgates_ref[gate[u]]
    elif u >= U_GM:
        val = dgates_ref[3 + u - U_GM]
    else:
        val = {U_Q: dq_ref, U_K: dk_ref, U_V: dv_ref}[u][...]
    return val.astype(BF16)


def _inproj_bwd_x(x, g_pre, w_in, dproj_parts, dxn):
    t = x.shape[0]
    tb = min(256, t)

    def body(x_ref, g_ref, w_ref, dpre_ref, dgates_ref, dq_ref, dk_ref, dv_ref, dxn_ref, dx_ref, dg_ref):
        @pl.when(pl.program_id(0) == 0)
        def _():
            dg_ref[...] = jnp.zeros_like(dg_ref)

        dh = jnp.zeros((tb, D_MODEL), F32)
        for u in range(N_UNITS):
            dh = dh + _dot_nt(_dproj_unit(u, dpre_ref, dgates_ref, dq_ref, dk_ref, dv_ref),
                              w_ref[:, u * UNIT:(u + 1) * UNIT])
        xf = x_ref[...]
        r = lax.rsqrt(jnp.mean(xf * xf, axis=-1, keepdims=True) + RMS_EPS)
        xhat = xf * r
        dg_ref[...] += jnp.sum(dh * xhat, axis=0, keepdims=True)
        dxhat = dh * g_ref[...]
        dx_ref[...] = dxn_ref[...] + r * (dxhat - xhat * jnp.mean(dxhat * xhat, axis=-1, keepdims=True))

    tokd = pl.BlockSpec((tb, D_MODEL), lambda i: (i, 0))
    return pl.pallas_call(
        body, name="inproj_bwd_x",
        grid=(t // tb,),
        in_specs=[tokd, _const_spec((1, D_MODEL)), _const_spec((D_MODEL, IN_WIDTH))] + _dproj_specs(tb) + [tokd],
        out_specs=[tokd, pl.BlockSpec((1, D_MODEL), lambda i: (0, 0))],
        out_shape=[S((t, D_MODEL), F32), S((1, D_MODEL), F32)],
        compiler_params=_params(1),
    )(x, g_pre, w_in, *dproj_parts, dxn)


def _inproj_bwd_w(x, g_pre, dproj_parts):
    t = x.shape[0]
    tb = min(256, t)
    nt = t // tb

    def body(x_ref, g_ref, dpre_ref, dgates_ref, dq_ref, dk_ref, dv_ref, dw_hbm, acc_ref):
        i = pl.program_id(0)
        xf = x_ref[...]
        r = lax.rsqrt(jnp.mean(xf * xf, axis=-1, keepdims=True) + RMS_EPS)
        ht = (xf * r * g_ref[...]).T.astype(BF16)

        @pl.when(i == 0)
        def _():
            acc_ref[...] = jnp.zeros_like(acc_ref)

        for u in range(N_UNITS):
            acc_ref[:, u * UNIT:(u + 1) * UNIT] += _dot(
                ht, _dproj_unit(u, dpre_ref, dgates_ref, dq_ref, dk_ref, dv_ref))

        @pl.when(i == nt - 1)
        def _():
            pltpu.sync_copy(acc_ref, dw_hbm)

    return pl.pallas_call(
        body, name="inproj_bwd_w",
        grid=(nt,),
        in_specs=[pl.BlockSpec((tb, D_MODEL), lambda i: (i, 0)), _const_spec((1, D_MODEL))] + _dproj_specs(tb),
        out_specs=pl.BlockSpec(memory_space=pl.ANY),
        out_shape=S((D_MODEL, IN_WIDTH), F32),
        scratch_shapes=[pltpu.VMEM((D_MODEL, IN_WIDTH), F32)],
        compiler_params=_params(1),
    )(x, g_pre, *dproj_parts)


def _attn_tile_size(t):
    return min(256, t)


def _softplus_parts(z):
    sp = jnp.maximum(z, 0.0) + jnp.log(1.0 + jnp.exp(-jnp.abs(z)))
    return -sp, z - sp


def _attn_fwd(proj):
    t = proj.shape[1]
    tq = _attn_tile_size(t)

    def body(q_ref, k_ref, v_ref, o_ref, bs_ref, kf_ref):
        i = pl.program_id(1)
        row = lax.broadcasted_iota(jnp.int32, (tq, tq), 0)
        col = lax.broadcasted_iota(jnp.int32, (tq, tq), 1)
        tri = (row > col).astype(BF16)
        tri2 = jnp.concatenate([tri, tri], axis=0)
        causal = col < row
        lane = lax.broadcasted_iota(jnp.int32, (1, LANES), 1)
        q_all = q_ref[0] * ATTN_SCALE
        in_heads = [(lane >= hh * HEAD_DIM) & (lane < (hh + 1) * HEAD_DIM) for hh in range(HEADS_PER_BLOCK)]
        q_heads = [jnp.where(in_head, q_all, 0.0).astype(BF16) for in_head in in_heads]

        def tiles(kj, state, masked):
            ks = pl.multiple_of(kj * tq, tq)
            kh = k_ref[0, pl.ds(ks, tq), :].astype(BF16)
            vh = v_ref[0, pl.ds(ks, tq), :].astype(BF16)
            new = []
            for qh, (carry, acc) in zip(q_heads, state):
                z = _dot_nt(qh, kh)
                lm, lb = _softplus_parts(z)
                if masked:
                    lm = jnp.where(causal, lm, 0.0)
                between = _dot(jnp.concatenate(_split_bf16(lm), axis=1), tri2) + carry
                w = jnp.exp(lb + between)
                if masked:
                    w = jnp.where(causal, w, 0.0)
                new.append((carry + jnp.sum(lm, axis=1, keepdims=True), acc + _dot(w.astype(BF16), vh)))
            return tuple(new)

        zero = (jnp.zeros((tq, 1), F32), jnp.zeros((tq, LANES), F32))
        state = tiles(i, (zero,) * HEADS_PER_BLOCK, True)

        def more(c):
            top = jnp.max(c[1][0][0])
            for carry, _ in c[1][1:]:
                top = jnp.maximum(top, jnp.max(carry))
            return jnp.logical_and(c[0] >= 0, top >= EXP_ZERO_BELOW)

        kj_end, state = lax.while_loop(more, lambda c: (c[0] - 1, tiles(c[0], c[1], False)), (i - 1, state))
        o_out, b_out = state[0][1], jnp.broadcast_to(state[0][0], (tq, LANES))
        for in_head, (carry, acc) in zip(in_heads[1:], state[1:]):
            o_out = jnp.where(in_head, acc, o_out)
            b_out = jnp.where(in_head, carry, b_out)
        o_ref[...] = o_out
        bs_ref[0] = b_out
        kf_ref[0, 0] = jnp.zeros((8, LANES), F32) + (kj_end + 1).astype(F32)

    nq = t // tq
    return pl.pallas_call(
        body, name="attn_fwd",
        grid=(N_HEAD_BLOCKS, nq),
        in_specs=[pl.BlockSpec((1, tq, LANES), lambda p, i: (U_Q, i, p)),
                  pl.BlockSpec((1, t, LANES), lambda p, i: (U_K, 0, p)),
                  pl.BlockSpec((1, t, LANES), lambda p, i: (U_V, 0, p))],
        out_specs=[pl.BlockSpec((tq, LANES), lambda p, i: (i, p)),
                   pl.BlockSpec((1, tq, LANES), lambda p, i: (p, i, 0)),
                   pl.BlockSpec((1, 1, 8, LANES), lambda p, i: (p, i, 0, 0))],
        out_shape=[S((t, UNIT), F32), S((N_HEAD_BLOCKS, t, LANES), F32), S((N_HEAD_BLOCKS, nq, 8, LANES), F32)],
        compiler_params=_params(2),
    )(proj, proj, proj)


def _attn_bwd(proj, d_o, bsum, kfirst):
    t = proj.shape[1]
    tq = _attn_tile_size(t)
    nq = t // tq

    def body(q_ref, k_ref, v_ref, do_ref, bs_ref, kf_ref, dq_ref, dk_ref, dv_ref, dkt_ref, dvt_ref):
        i = pl.program_id(1)

        @pl.when(i == 0)
        def _():
            dkt_ref[...] = jnp.zeros_like(dkt_ref)
            dvt_ref[...] = jnp.zeros_like(dvt_ref)

        row = lax.broadcasted_iota(jnp.int32, (tq, tq), 0)
        col = lax.broadcasted_iota(jnp.int32, (tq, tq), 1)
        upto = (row <= col).astype(BF16)
        before = (row < col).astype(BF16)
        upto2 = jnp.concatenate([upto, upto], axis=0)
        before2 = jnp.concatenate([before, before], axis=0)
        causal = col < row
        lane = lax.broadcasted_iota(jnp.int32, (1, LANES), 1)
        q_all = q_ref[0] * ATTN_SCALE
        do_all = do_ref[...]
        bs_all = bs_ref[0]
        k_first = jnp.clip(jnp.max(kf_ref[0, 0]).astype(jnp.int32), 0, i)

        in_heads = [(lane >= hh * HEAD_DIM) & (lane < (hh + 1) * HEAD_DIM) for hh in range(HEADS_PER_BLOCK)]
        heads = []
        for in_head in in_heads:
            q_m = jnp.where(in_head, q_all, 0.0)
            do_m = jnp.where(in_head, do_all, 0.0)
            btot = jnp.max(jnp.where(in_head, bs_all, -jnp.inf), axis=1, keepdims=True)
            heads.append((q_m.astype(BF16), do_m.astype(BF16), q_m.T.astype(BF16), do_m.T.astype(BF16), btot))

        def tiles(kj, state, masked):
            ks = pl.multiple_of(kj * tq, tq)
            kh = k_ref[0, pl.ds(ks, tq), :].astype(BF16)
            vh = v_ref[0, pl.ds(ks, tq), :].astype(BF16)
            new, dkt, dvt = [], None, None
            for (qh, doh, qt, dot_, btot), (c_b, c_p, dq) in zip(heads, state):
                z = _dot_nt(qh, kh)
                lm, lb = _softplus_parts(z)
                if masked:
                    lm = jnp.where(causal, lm, 0.0)
                between = btot - (c_b + _dot(jnp.concatenate(_split_bf16(lm), axis=1), upto2))
                w = jnp.exp(lb + between)
                if masked:
                    w = jnp.where(causal, w, 0.0)
                e = w * _dot_nt(doh, vh)
                p_sum = c_p + _dot(jnp.concatenate(_split_bf16(e), axis=1), before2)
                beta = jnp.exp(lb)
                dz = e * (1.0 - beta) - p_sum * beta
                if masked:
                    dz = jnp.where(causal, dz, 0.0)
                dzb = dz.astype(BF16)
                dkt_h, dvt_h = _dot(qt, dzb), _dot(dot_, w.astype(BF16))
                dkt = dkt_h if dkt is None else dkt + dkt_h
                dvt = dvt_h if dvt is None else dvt + dvt_h
                new.append((c_b + jnp.sum(lm, axis=1, keepdims=True), c_p + jnp.sum(e, axis=1, keepdims=True),
                            dq + _dot(dzb, kh)))
            dkt_ref[kj] += dkt
            dvt_ref[kj] += dvt
            return tuple(new)

        zero = (jnp.zeros((tq, 1), F32), jnp.zeros((tq, 1), F32), jnp.zeros((tq, LANES), F32))
        state = lax.fori_loop(k_first, i, lambda kj, st: tiles(kj, st, False), (zero,) * HEADS_PER_BLOCK)
        state = tiles(i, state, True)
        dq_out = state[0][2]
        for in_head, (_, _, dq) in zip(in_heads[1:], state[1:]):
            dq_out = jnp.where(in_head, dq, dq_out)
        dq_ref[...] = dq_out * ATTN_SCALE

        @pl.when(i == nq - 1)
        def _():
            for kj in range(nq):
                dk_ref[kj * tq:(kj + 1) * tq, :] = dkt_ref[kj].T
                dv_ref[kj * tq:(kj + 1) * tq, :] = dvt_ref[kj].T

    return pl.pallas_call(
        body, name="attn_bwd",
        grid=(N_HEAD_BLOCKS, nq),
        in_specs=[pl.BlockSpec((1, tq, LANES), lambda p, i: (U_Q, i, p)),
                  pl.BlockSpec((1, t, LANES), lambda p, i: (U_K, 0, p)),
                  pl.BlockSpec((1, t, LANES), lambda p, i: (U_V, 0, p)),
                  pl.BlockSpec((tq, LANES), lambda p, i: (i, p)),
                  pl.BlockSpec((1, tq, LANES), lambda p, i: (p, i, 0)),
                  pl.BlockSpec((1, 1, 8, LANES), lambda p, i: (p, i, 0, 0))],
        out_specs=[pl.BlockSpec((tq, LANES), lambda p, i: (i, p)),
                   pl.BlockSpec((t, LANES), lambda p, i: (0, p)),
                   pl.BlockSpec((t, LANES), lambda p, i: (0, p))],
        out_shape=[S((t, UNIT), F32)] * 3,
        scratch_shapes=[pltpu.VMEM((nq, LANES, tq), F32), pltpu.VMEM((nq, LANES, tq), F32)],
        compiler_params=_params(2),
    )(proj, proj, proj, d_o, bsum, kfirst)


N_MIX_VIEWS = 11
N_MIX_VIEWS_NO_CONV = 7


def _mix_in_specs(tb, conv_inputs=True):
    hb = tb // HALO

    def unit(u):
        return pl.BlockSpec((1, tb, UNIT), lambda i: (u, i, 0))

    def halo(u):
        return pl.BlockSpec((1, HALO, UNIT), lambda i: (u, jnp.maximum(i * hb - 1, 0), 0))

    conv = [unit(U_CA), halo(U_CA), unit(U_CB), halo(U_CB)] if conv_inputs else []
    return [unit(U_P), halo(U_P), unit(U_GP)] + conv + [
        unit(U_GC), unit(U_GA), pl.BlockSpec((3, tb, UNIT), lambda i: (3, i, 0)),
        pl.BlockSpec((3, tb, UNIT), lambda i: (4, i, 0))]


def _mix_weight_specs():
    return [_const_spec((4, GROUP, GROUP)), _const_spec((1, UNIT)), _const_spec((1, UNIT)),
            _const_spec((UNIT, D_MODEL)), _const_spec((CONV_K_PAD, UNIT)), _const_spec((1, UNIT)),
            _const_spec((1, UNIT)), _const_spec((1, UNIT)), _const_spec((UNIT, D_MODEL)),
            _const_spec((UNIT, D_MODEL)), _const_spec((D_MODEL, D_MODEL)), _const_spec((1, D_MODEL))]


def _shifted_reader(buf, rbuf, tb):
    if rbuf is None:
        return lambda s: buf[s:s + tb, :]
    length = tb + HALO - 8
    for b in range(1, 8):
        rbuf[b, :, :] = buf[b:b + length, :]

    def read(s):
        a, b = divmod(s, 8)
        return buf[s:s + tb, :] if b == 0 else rbuf[b, 8 * a:8 * a + tb, :]

    return read


def _mix_forward(i, tb, proj_refs, o_ref, w_refs, pbuf, ubuf=None, rbuf=None, cv_ref=None):
    if cv_ref is None:
        p_ref, ph_ref, gp_ref, ca_ref, cah_ref, cb_ref, cbh_ref, gc_ref, ga_ref, gm0_ref, gm1_ref = proj_refs
    else:
        p_ref, ph_ref, gp_ref, gc_ref, ga_ref, gm0_ref, gm1_ref = proj_refs
    (poolw_ref, poolb_ref, pscale_ref, wpo_ref, convw_ref, convb_ref, lng_ref, lnb_ref,
     wco_ref, wao_ref, wo_ref, gpost_ref) = w_refs
    first = i == 0
    r = {}

    pbuf[0:HALO, :] = jnp.where(first, 0.0, ph_ref[0])
    pbuf[HALO:HALO + tb, :] = p_ref[0]
    tpos = i * tb + lax.broadcasted_iota(jnp.int32, (tb, 1), 0)
    d_parts, y_parts = [], []
    for g, win in enumerate(POOL_WINDOWS):
        cs = slice(g * GROUP, (g + 1) * GROUP)
        cur = pbuf[HALO:HALO + tb, cs]
        s = cur
        for j in range(1, win):
            s = s + pbuf[HALO - j:HALO - j + tb, cs]
        cnt = jnp.minimum(tpos + 1, win).astype(F32)
        d_g = s / cnt - cur
        d_parts.append(d_g)
        y_parts.append(_dot(d_g.astype(BF16), poolw_ref[g]))
    r["d"] = d_parts
    y = jnp.concatenate(y_parts, axis=1) + poolb_ref[...]
    r["y"] = y
    mp = y * pscale_ref[...]
    gp = gp_ref[0]
    sgp = _sigmoid(gp)
    r["mp"], r["gp"], r["sgp"] = mp, gp, sgp
    ua = mp * (gp * sgp)

    if cv_ref is None:
        ah, bh = cah_ref[0], cbh_ref[0]
        ubuf[0:HALO, :] = jnp.where(first, 0.0, ah * _sigmoid(bh))
        ubuf[HALO:HALO + tb, :] = ca_ref[0] * _sigmoid(cb_ref[0])
        cv = jnp.zeros((tb, UNIT), F32) + convb_ref[...]
        off = HALO - (CONV_K - 1)
        u_at = _shifted_reader(ubuf, rbuf, tb)
        for k in range(CONV_K):
            cv = cv + convw_ref[k:k + 1, :] * u_at(off + k)
    else:
        cv = cv_ref[...]
    r["cv"] = cv
    mu = jnp.mean(cv, axis=-1, keepdims=True)
    cc = cv - mu
    rs = lax.rsqrt(jnp.mean(cc * cc, axis=-1, keepdims=True) + LN_EPS)
    nrm = cc * rs
    ln = nrm * lng_ref[...] + lnb_ref[...]
    sln = _sigmoid(ln)
    sc = ln * sln
    gc = gc_ref[0]
    sgc = _sigmoid(gc)
    r["rs"], r["nrm"], r["ln"], r["sln"], r["sc"], r["gc"], r["sgc"] = rs, nrm, ln, sln, sc, gc, sgc
    ub = sc * (gc * sgc)

    o = o_ref[...]
    ga = ga_ref[0]
    sga = _sigmoid(ga)
    r["o"], r["ga"], r["sga"] = o, ga, sga
    uc = o * (ga * sga)

    r["ua"], r["ub"], r["uc"] = ua.astype(BF16), ub.astype(BF16), uc.astype(BF16)
    ya = _dot(r["ua"], wpo_ref[...])
    yb = _dot(r["ub"], wco_ref[...])
    yc = _dot(r["uc"], wao_ref[...])
    g0 = _sigmoid(jnp.concatenate([gm0_ref[0], gm0_ref[1]], axis=1))
    g1 = _sigmoid(jnp.concatenate([gm0_ref[2], gm1_ref[0]], axis=1))
    g2 = _sigmoid(jnp.concatenate([gm1_ref[1], gm1_ref[2]], axis=1))
    r["ya"], r["yb"], r["yc"], r["g0"], r["g1"], r["g2"] = ya, yb, yc, g0, g1, g2
    m = (g0 * ya + g1 * yb + g2 * yc).astype(BF16)
    r["m"] = m
    out = _dot(m, wo_ref[...])
    r2 = lax.rsqrt(jnp.mean(out * out, axis=-1, keepdims=True) + RMS_EPS)
    r["n2"], r["r2"] = out * r2, r2
    return r


def _mix_fwd(proj, o, x, weights):
    t = x.shape[0]
    tb = min(256, t)
    nv = N_MIX_VIEWS

    def body(*refs):
        proj_refs, o_ref, x_ref = refs[0:nv], refs[nv], refs[nv + 1]
        w_refs = refs[nv + 2:nv + 14]
        xn_ref, cv_ref, pbuf, ubuf, rbuf = refs[nv + 14:nv + 19]
        r = _mix_forward(pl.program_id(0), tb, proj_refs, o_ref, w_refs, pbuf, ubuf, rbuf)
        xn_ref[...] = x_ref[...] + r["n2"] * w_refs[11][...]
        cv_ref[...] = r["cv"]

    return pl.pallas_call(
        body, name="mix_fwd",
        grid=(t // tb,),
        in_specs=_mix_in_specs(tb) + [pl.BlockSpec((tb, UNIT), lambda i: (i, 0)),
                                      pl.BlockSpec((tb, D_MODEL), lambda i: (i, 0))] + _mix_weight_specs(),
        out_specs=[pl.BlockSpec((tb, D_MODEL), lambda i: (i, 0)), pl.BlockSpec((tb, UNIT), lambda i: (i, 0))],
        out_shape=[S((t, D_MODEL), F32), S((t, UNIT), F32)],
        scratch_shapes=[pltpu.VMEM((HALO + tb, UNIT), F32), pltpu.VMEM((HALO + tb, UNIT), F32),
                        pltpu.VMEM((8, HALO + tb - 8, UNIT), F32)],
        compiler_params=_params(1),
    )(*([proj] * nv), o, x, *weights)


def _mix_bwd(proj, o, cv, x, dxn, weights):
    t = x.shape[0]
    tb = min(256, t)
    nt = t // tb

    nv = N_MIX_VIEWS_NO_CONV

    def body(*refs):
        proj_refs, o_ref, cv_ref, x_ref, dxn_ref = refs[0:nv], refs[nv], refs[nv + 1], refs[nv + 2], refs[nv + 3]
        w_refs = refs[nv + 4:nv + 16]
        (dg_ref, dd_ref, dcv_ref, do_ref, dwo_hbm, dwp_hbm, dwc_hbm, dwa_hbm, dpw_ref,
         dvec_ref) = refs[nv + 16:nv + 26]
        pbuf, dwo_ref, dwp_ref, dwc_ref, dwa_ref = refs[nv + 26:nv + 31]
        (poolw_ref, _, pscale_ref, wpo_ref, _, _, lng_ref, _, wco_ref, wao_ref, wo_ref, gpost_ref) = w_refs
        i = pl.program_id(0)

        @pl.when(i == 0)
        def _():
            for ref in (dwo_ref, dwp_ref, dwc_ref, dwa_ref, dpw_ref, dvec_ref):
                ref[...] = jnp.zeros_like(ref)

        r = _mix_forward(i, tb, proj_refs, o_ref, w_refs, pbuf, cv_ref=cv_ref)

        def colsum(v):
            return jnp.sum(v, axis=0, keepdims=True)

        dxn = dxn_ref[...]
        n2 = r["n2"]
        dvec_ref[0:1, :] += colsum(dxn * n2)
        dn2 = dxn * gpost_ref[...]
        dout = (r["r2"] * (dn2 - n2 * jnp.mean(dn2 * n2, axis=-1, keepdims=True))).astype(BF16)
        dm = _dot_nt(dout, wo_ref[...])
        dwo_ref[...] += _dot_tn(r["m"], dout)

        g0, g1, g2 = r["g0"], r["g1"], r["g2"]
        dgm = [dm * r["ya"] * g0 * (1.0 - g0), dm * r["yb"] * g1 * (1.0 - g1), dm * r["yc"] * g2 * (1.0 - g2)]
        for bidx in range(N_BRANCH):
            dg_ref[3 + 2 * bidx] = dgm[bidx][:, 0:UNIT]
            dg_ref[4 + 2 * bidx] = dgm[bidx][:, UNIT:2 * UNIT]
        dya = (dm * g0).astype(BF16)
        dyb = (dm * g1).astype(BF16)
        dyc = (dm * g2).astype(BF16)
        dua = _dot_nt(dya, wpo_ref[...])
        dub = _dot_nt(dyb, wco_ref[...])
        duc = _dot_nt(dyc, wao_ref[...])
        dwp_ref[...] += _dot_tn(r["ua"], dya)
        dwc_ref[...] += _dot_tn(r["ub"], dyb)
        dwa_ref[...] += _dot_tn(r["uc"], dyc)

        gp, sgp = r["gp"], r["sgp"]
        dmp = dua * (gp * sgp)
        dg_ref[0] = dua * r["mp"] * _dsilu(gp, sgp)
        dvec_ref[2:3, 0:UNIT] += colsum(dmp * r["y"])
        dy = dmp * pscale_ref[...]
        dvec_ref[1:2, 0:UNIT] += colsum(dy)
        dd_parts = []
        for g in range(len(POOL_WINDOWS)):
            dy_g = dy[:, g * GROUP:(g + 1) * GROUP].astype(BF16)
            dd_parts.append(_dot_nt(dy_g, poolw_ref[g]))
            dpw_ref[g] += _dot_tn(r["d"][g].astype(BF16), dy_g)
        dd_ref[...] = jnp.concatenate(dd_parts, axis=1)

        gc, sgc = r["gc"], r["sgc"]
        dsc = dub * (gc * sgc)
        dg_ref[1] = dub * r["sc"] * _dsilu(gc, sgc)
        dln = dsc * _dsilu(r["ln"], r["sln"])
        nrm = r["nrm"]
        dvec_ref[4:5, 0:UNIT] += colsum(dln * nrm)
        dvec_ref[5:6, 0:UNIT] += colsum(dln)
        dnrm = dln * lng_ref[...]
        dcv = r["rs"] * (dnrm - jnp.mean(dnrm, axis=-1, keepdims=True)
                         - nrm * jnp.mean(dnrm * nrm, axis=-1, keepdims=True))
        dvec_ref[3:4, 0:UNIT] += colsum(dcv)
        dcv_ref[...] = dcv

        ga, sga = r["ga"], r["sga"]
        do_ref[...] = duc * (ga * sga)
        dg_ref[2] = duc * r["o"] * _dsilu(ga, sga)

        @pl.when(i == nt - 1)
        def _():
            for acc, hbm in ((dwo_ref, dwo_hbm), (dwp_ref, dwp_hbm), (dwc_ref, dwc_hbm), (dwa_ref, dwa_hbm)):
                pltpu.sync_copy(acc, hbm)

    def acc_spec(shape):
        n = len(shape)
        return pl.BlockSpec(shape, lambda i: (0,) * n)

    tok = lambda w: pl.BlockSpec((tb, w), lambda i: (i, 0))
    any_spec = pl.BlockSpec(memory_space=pl.ANY)
    w_shapes = [(D_MODEL, D_MODEL), (UNIT, D_MODEL), (UNIT, D_MODEL), (UNIT, D_MODEL)]
    return pl.pallas_call(
        body, name="mix_bwd",
        grid=(nt,),
        in_specs=_mix_in_specs(tb, conv_inputs=False) + [tok(UNIT), tok(UNIT), tok(D_MODEL), tok(D_MODEL)]
        + _mix_weight_specs(),
        out_specs=[pl.BlockSpec((9, tb, UNIT), lambda i: (0, i, 0)), tok(UNIT), tok(UNIT), tok(UNIT)]
        + [any_spec] * 4 + [acc_spec((4, GROUP, GROUP)), acc_spec((8, D_MODEL))],
        out_shape=[S((9, t, UNIT), F32), S((t, UNIT), F32), S((t, UNIT), F32), S((t, UNIT), F32)]
        + [S(s, F32) for s in w_shapes] + [S((4, GROUP, GROUP), F32), S((8, D_MODEL), F32)],
        scratch_shapes=[pltpu.VMEM((HALO + tb, UNIT), F32)] + [pltpu.VMEM(s, F32) for s in w_shapes],
        compiler_params=_params(1),
    )(*([proj] * nv), o, cv, x, dxn, *weights)


def _halo_bwd(proj, dcv, dd, conv_w):
    t = dcv.shape[0]
    tb = min(256, t)
    hb = tb // HALO
    n_halo_blocks = t // HALO
    nt = t // tb

    def body(ca_ref, cah_ref, cb_ref, cbh_ref, dcv_ref, dcvh_ref, dd_ref, ddh_ref, cw_ref,
             dpre_ref, dcw_ref, ubuf, gbuf, nbuf, ru_buf, rg_buf, acc_ref):
        i = pl.program_id(0)
        first = i == 0
        last = i == nt - 1

        @pl.when(first)
        def _():
            acc_ref[...] = jnp.zeros_like(acc_ref)

        a, b = ca_ref[0], cb_ref[0]
        sb = _sigmoid(b)
        ubuf[0:HALO, :] = jnp.where(first, 0.0, cah_ref[0] * _sigmoid(cbh_ref[0]))
        ubuf[HALO:HALO + tb, :] = a * sb
        dcv_main = dcv_ref[...]
        gbuf[0:tb, :] = dcv_main
        gbuf[tb:tb + HALO, :] = jnp.where(last, 0.0, dcvh_ref[...])

        off = HALO - (CONV_K - 1)
        u_at = _shifted_reader(ubuf, ru_buf, tb)
        g_at = _shifted_reader(gbuf, rg_buf, tb)
        du = jnp.zeros((tb, UNIT), F32)
        for k in range(CONV_K):
            du = du + cw_ref[k:k + 1, :] * g_at(CONV_K - 1 - k)
            acc_ref[k] += jnp.sum((dcv_main * u_at(off + k)).reshape(tb // 8, 8, UNIT), axis=0)
        dpre_ref[1] = du * sb
        dpre_ref[2] = du * a * sb * (1.0 - sb)

        @pl.when(last)
        def _():
            dcw_ref[...] = jnp.sum(acc_ref[...], axis=1)

        tpos = i * tb + lax.broadcasted_iota(jnp.int32, (tb + HALO, 1), 0)
        dd_main = dd_ref[...]
        dd_ext = jnp.concatenate([dd_main, jnp.where(last, 0.0, ddh_ref[...])], axis=0)
        dp_parts = []
        for g, win in enumerate(POOL_WINDOWS):
            cs = slice(g * GROUP, (g + 1) * GROUP)
            cnt = jnp.minimum(tpos + 1, win).astype(F32)
            nbuf[:, cs] = dd_ext[:, cs] / cnt
        for g, win in enumerate(POOL_WINDOWS):
            cs = slice(g * GROUP, (g + 1) * GROUP)
            s = nbuf[0:tb, cs]
            for j in range(1, win):
                s = s + nbuf[j:j + tb, cs]
            dp_parts.append(s - dd_main[:, cs])
        dpre_ref[0] = jnp.concatenate(dp_parts, axis=1)

    def unit(u):
        return pl.BlockSpec((1, tb, UNIT), lambda i: (u, i, 0))

    def past(u):
        return pl.BlockSpec((1, HALO, UNIT), lambda i: (u, jnp.maximum(i * hb - 1, 0), 0))

    tok = pl.BlockSpec((tb, UNIT), lambda i: (i, 0))
    future = pl.BlockSpec((HALO, UNIT), lambda i: (jnp.minimum((i + 1) * hb, n_halo_blocks - 1), 0))
    return pl.pallas_call(
        body, name="halo_bwd",
        grid=(nt,),
        in_specs=[unit(U_CA), past(U_CA), unit(U_CB), past(U_CB), tok, future, tok, future,
                  _const_spec((CONV_K_PAD, UNIT))],
        out_specs=[pl.BlockSpec((3, tb, UNIT), lambda i: (0, i, 0)),
                   pl.BlockSpec((CONV_K_PAD, UNIT), lambda i: (0, 0))],
        out_shape=[S((3, t, UNIT), F32), S((CONV_K_PAD, UNIT), F32)],
        scratch_shapes=[pltpu.VMEM((HALO + tb, UNIT), F32)] * 3
        + [pltpu.VMEM((8, HALO + tb - 8, UNIT), F32)] * 2 + [pltpu.VMEM((CONV_K_PAD, 8, UNIT), F32)],
        compiler_params=_params(1),
    )(proj, proj, proj, proj, dcv, dcv, dd, dd, conv_w)


def _loss_head(y, target):
    t = y.shape[0]
    tb = min(512, t)

    def body(y_ref, t_ref, loss_ref, dy_ref):
        @pl.when(pl.program_id(0) == 0)
        def _():
            loss_ref[...] = jnp.zeros_like(loss_ref)

        err = y_ref[...] - t_ref[...]
        dy_ref[...] = err * (1.0 / D_MODEL)
        part = 0.5 * jnp.sum(jnp.mean(err * err, axis=-1, keepdims=True), axis=0, keepdims=True)
        r8 = lax.broadcasted_iota(jnp.int32, (8, LANES), 0)
        c8 = lax.broadcasted_iota(jnp.int32, (8, LANES), 1)
        loss_ref[...] += jnp.where((r8 == 0) & (c8 == 0), part, 0.0)

    return pl.pallas_call(
        body, name="loss_head",
        grid=(t // tb,),
        in_specs=[pl.BlockSpec((tb, D_MODEL), lambda i: (i, 0))] * 2,
        out_specs=[pl.BlockSpec((8, LANES), lambda i: (0, 0)), pl.BlockSpec((tb, D_MODEL), lambda i: (i, 0))],
        out_shape=[S((8, LANES), F32), S((t, D_MODEL), F32)],
        compiler_params=_params(1),
    )(y, target)


def _adamw(name, parts, w, m, v):
    n, rows, cols = parts.shape
    rb = rows
    while rb * cols * 4 * (n + 7) * 2 > 24 * 1024 * 1024 and rb % 16 == 0:
        rb //= 2

    def body(p_ref, w_ref, m_ref, v_ref, g_ref, d_ref, nm_ref, nv_ref):
        g = p_ref[0]
        for k in range(1, n):
            g = g + p_ref[k]
        nm = ADAM_B1 * m_ref[...] + (1.0 - ADAM_B1) * g
        nv = ADAM_B2 * v_ref[...] + (1.0 - ADAM_B2) * (g * g)
        m_hat = nm / (1.0 - ADAM_B1 ** ADAM_STEP)
        v_hat = nv / (1.0 - ADAM_B2 ** ADAM_STEP)
        g_ref[...] = g
        d_ref[...] = -ADAM_LR * (m_hat / (jnp.sqrt(v_hat) + ADAM_EPS) + ADAM_WD * w_ref[...])
        nm_ref[...] = nm
        nv_ref[...] = nv

    blk = pl.BlockSpec((rb, cols), lambda i: (i, 0))
    return pl.pallas_call(
        body, name=name,
        grid=(rows // rb,),
        in_specs=[pl.BlockSpec((n, rb, cols), lambda i: (0, i, 0)), blk, blk, blk],
        out_specs=[blk] * 4,
        out_shape=[S((rows, cols), F32)] * 4,
        compiler_params=_params(1),
    )(parts, w, m, v)


def _axis_slice(ref, axis, block, size):
    idx = [slice(None)] * len(ref.shape)
    idx[axis] = pl.ds(pl.multiple_of(block * size, size), size)
    return ref.at[tuple(idx)]


def _axis_index(ref, axis, index):
    return ref.at[tuple([slice(None)] * axis + [index])]


COPY_PIECE_BYTES = 2 * 1024 * 1024


def _row_pieces(shape):
    rows = shape[-2]
    size = 4 * rows * shape[-1]
    for d in shape[:-2]:
        size *= d
    n = 1
    while size // n > COPY_PIECE_BYTES and rows % (2 * n * 8) == 0:
        n *= 2
    return [(k * (rows // n), rows // n) for k in range(n)]


def _rows(ref, r0, rows):
    idx = [slice(None)] * len(ref.shape)
    idx[-2] = pl.ds(r0, rows)
    return ref.at[tuple(idx)]


def _staged_local_copies(pairs, bufs, in_sems, out_sems):
    ins = [pltpu.make_async_copy(src, buf, in_sems.at[k]) for k, ((src, _), buf) in enumerate(zip(pairs, bufs))]
    for cp in ins:
        cp.start()
    for cp in ins:
        cp.wait()
    outs = [pltpu.make_async_copy(buf, dst, out_sems.at[k]) for k, ((_, dst), buf) in enumerate(zip(pairs, bufs))]
    for cp in outs:
        cp.start()
    return outs


def _run_copies(local, sends, recvs):
    for cp in local + sends:
        cp.start()
    for cp in recvs:
        cp.wait_recv()
    for cp in sends:
        cp.wait_send()
    for cp in local:
        cp.wait()


def _gather_weights(shards, axes):
    n = len(shards)
    fulls = [S(tuple(4 * d if k == ax else d for k, d in enumerate(s.shape)), s.dtype) for s, ax in zip(shards, axes)]

    def body(*refs):
        src, dst = refs[0:n], refs[n:2 * n]
        send_sems, recv_sems, pass_send_sems, pass_recv_sems, in_sems, out_sems = refs[2 * n:2 * n + 6]
        bufs = refs[2 * n + 6:3 * n + 6]
        x, y, c = lax.axis_index("x"), lax.axis_index("y"), lax.axis_index("c")
        chips = [(1 - x, y), (x, 1 - y), (1 - x, 1 - y)]

        def place(a, chip, layer=None):
            ref = _axis_slice(dst[a], axes[a], 2 * chip[0] + chip[1], src[a].shape[axes[a]])
            return ref if layer is None else ref.at[pl.ds(layer, 1)]

        def over_ici(k, chip_of_block, to):
            a, j, r0, rows = pieces[k]
            return pltpu.make_async_remote_copy(
                src_ref=_rows(src[a].at[pl.ds(c, 1)], r0, rows), dst_ref=_rows(place(a, chip_of_block, c), r0, rows),
                send_sem=send_sems.at[k], recv_sem=recv_sems.at[k], device_id=(to[0], to[1], c), device_id_type=MESH)

        def to_sibling(k, chip_of_block, layer):
            a, j, r0, rows = pieces[k]
            block = _rows(place(a, chip_of_block, layer), r0, rows)
            return pltpu.make_async_remote_copy(
                src_ref=block, dst_ref=block, send_sem=pass_send_sems.at[k], recv_sem=pass_recv_sems.at[k],
                device_id=(x, y, 1 - c), device_id_type=MESH)

        first = [over_ici(k, (x, y), chips[j]) for k, (a, j, r0, rows) in enumerate(pieces)]
        for cp in first:
            cp.start()
        local = _staged_local_copies([(src[a], place(a, (x, y))) for a in range(n)], bufs, in_sems, out_sems)
        passed = []
        for k, (a, j, r0, rows) in enumerate(pieces):
            over_ici(k, chips[j], chips[j]).wait_recv()
            passed.append(to_sibling(k, chips[j], c))
            passed[-1].start()
        for k, (a, j, r0, rows) in enumerate(pieces):
            to_sibling(k, chips[j], 1 - c).wait_recv()
        for cp in first + passed:
            cp.wait_send()
        for cp in local:
            cp.wait()

    per_array = [_row_pieces((1,) + shards[a].shape[1:]) for a in range(n)]
    pieces = [(a, j, r0, rows) for a in range(n) for r0, rows in per_array[a] for j in range(3)]
    any_spec = pl.BlockSpec(memory_space=pl.ANY)
    return pl.pallas_call(
        body, name="gather_weights",
        in_specs=[any_spec] * n, out_specs=[any_spec] * n, out_shape=fulls,
        scratch_shapes=[pltpu.SemaphoreType.DMA((len(pieces),))] * 4 + [pltpu.SemaphoreType.DMA((n,))] * 2
        + [pltpu.VMEM(s.shape, s.dtype) for s in shards],
        compiler_params=_params(0),
    )(*shards)


def _swap_layers(layers):
    n = len(layers)
    pieces = [(a, r0, rows) for a in range(n) for r0, rows in _row_pieces(layers[a][0].shape)]

    def body(*refs):
        src = [refs[2 * a:2 * a + 2] for a in range(n)]
        dst = refs[2 * n:3 * n]
        send_sems, recv_sems = refs[3 * n:3 * n + 2]
        x, y, c = lax.axis_index("x"), lax.axis_index("y"), lax.axis_index("c")

        def remote(k, layer):
            a, r0, rows = pieces[k]
            return pltpu.make_async_remote_copy(
                src_ref=_rows(src[a][layer], r0, rows), dst_ref=_rows(dst[a], r0, rows),
                send_sem=send_sems.at[k], recv_sem=recv_sems.at[k], device_id=(x, y, 1 - c), device_id_type=MESH)

        for layer in range(2):
            @pl.when(c == 1 - layer)
            def _(layer=layer):
                for k in range(len(pieces)):
                    remote(k, layer).start()

        for k in range(len(pieces)):
            remote(k, 0).wait_recv()
        for k in range(len(pieces)):
            remote(k, 0).wait_send()

    any_spec = pl.BlockSpec(memory_space=pl.ANY)
    return pl.pallas_call(
        body, name="swap_layers",
        in_specs=[any_spec] * (2 * n), out_specs=[any_spec] * n,
        out_shape=[S(layers[a][0].shape, layers[a][0].dtype) for a in range(n)],
        scratch_shapes=[pltpu.SemaphoreType.DMA((len(pieces),))] * 2,
    )(*[arr for pair in layers for arr in pair])


def _add_own_layer(name, mine, theirs, layer):
    rows, cols = theirs.shape
    rb = rows
    while rb * cols * 4 * 4 * 2 > 24 * 1024 * 1024 and rb % 32 == 0:
        rb //= 2

    def body(layer_ref, m0_ref, m1_ref, t_ref, o_ref):
        own = jnp.where(layer_ref[0] == 0, m0_ref[...], m1_ref[...])
        o_ref[0] = (own + t_ref[...]).astype(BF16)

    def own_spec(which):
        return pl.BlockSpec((rb, cols), lambda i, l: (jnp.where(l[0] == which, i, 0), 0))

    return pl.pallas_call(
        body, name=name,
        grid_spec=pltpu.PrefetchScalarGridSpec(
            num_scalar_prefetch=1, grid=(rows // rb,),
            in_specs=[own_spec(0), own_spec(1), pl.BlockSpec((rb, cols), lambda i, l: (i, 0))],
            out_specs=pl.BlockSpec((1, rb, cols), lambda i, l: (0, i, 0))),
        out_shape=S((1, rows, cols), BF16),
        compiler_params=_params(1),
    )(layer, *mine, theirs)


def _send_to_owners(chip_halves, shard_axes, packed):
    n = len(chip_halves)

    def owned_shape(a):
        return tuple(d // 4 if k == shard_axes[a] else d for k, d in enumerate(chip_halves[a].shape))

    flips = [(dx, dy, dc) for dx in (0, 1) for dy in (0, 1) for dc in (0, 1)][1:]

    def body(*refs):
        src, psrc, dst, pdst = refs[0:n], refs[n], refs[n + 1:2 * n + 1], refs[2 * n + 1]
        send_sems, recv_sems, local_sems, psend_sems, precv_sems = refs[2 * n + 2:2 * n + 7]
        x, y, c = lax.axis_index("x"), lax.axis_index("y"), lax.axis_index("c")
        my_chip = 2 * x + y
        chips = [(1 - x, y), (x, 1 - y), (1 - x, 1 - y)]

        def owned(a, chip):
            return _axis_slice(src[a], shard_axes[a], 2 * chip[0] + chip[1], src[a].shape[shard_axes[a]] // 4)

        def remote(a, j, to, from_chip):
            return pltpu.make_async_remote_copy(
                src_ref=owned(a, to), dst_ref=dst[a].at[from_chip], send_sem=send_sems.at[a, j],
                recv_sem=recv_sems.at[a, j], device_id=(to[0], to[1], c), device_id_type=MESH)

        def peer(f):
            return (1 - x if f[0] else x, 1 - y if f[1] else y, 1 - c if f[2] else c)

        def premote(j, to, from_dev):
            return pltpu.make_async_remote_copy(
                src_ref=psrc, dst_ref=pdst.at[from_dev], send_sem=psend_sems.at[j], recv_sem=precv_sems.at[j],
                device_id=to, device_id_type=MESH)

        local = [pltpu.make_async_copy(_rows(owned(a, (x, y)), r0, rows), _rows(dst[a].at[my_chip], r0, rows),
                                       local_sems.at[k]) for k, (a, r0, rows) in enumerate(own_pieces)]
        local.append(pltpu.make_async_copy(psrc, pdst.at[4 * x + 2 * y + c], local_sems.at[len(own_pieces)]))
        sends = [remote(a, j, chip, my_chip) for a in range(n) for j, chip in enumerate(chips)]
        sends += [premote(j, peer(f), 4 * x + 2 * y + c) for j, f in enumerate(flips)]
        recvs = [remote(a, j, (x, y), 2 * chip[0] + chip[1]) for a in range(n) for j, chip in enumerate(chips)]
        for j, f in enumerate(flips):
            px, py, pc = peer(f)
            recvs.append(premote(j, (x, y, c), 4 * px + 2 * py + pc))
        _run_copies(local, sends, recvs)

    own_pieces = [(a, r0, rows) for a in range(n) for r0, rows in _row_pieces(owned_shape(a))]
    any_spec = pl.BlockSpec(memory_space=pl.ANY)
    return pl.pallas_call(
        body, name="send_to_owners",
        in_specs=[any_spec] * (n + 1), out_specs=[any_spec] * (n + 1),
        out_shape=[S((4,) + owned_shape(a), chip_halves[a].dtype) for a in range(n)]
        + [S((N_DEV,) + packed.shape, packed.dtype)],
        scratch_shapes=[pltpu.SemaphoreType.DMA((n, 3)), pltpu.SemaphoreType.DMA((n, 3)),
                        pltpu.SemaphoreType.DMA((len(own_pieces) + 1,)), pltpu.SemaphoreType.DMA((7,)),
                        pltpu.SemaphoreType.DMA((7,))],
    )(*chip_halves, packed)


def _join_halves(reduced, half_axes):
    n = len(reduced)

    def joined_shape(a):
        s, ax = reduced[a].shape, half_axes[a]
        return s[:ax] + (2,) + s[ax:]

    pieces = [(a, r0, rows) for a in range(n) for r0, rows in _row_pieces(reduced[a].shape)]

    def body(*refs):
        src, dst = refs[0:n], refs[n:2 * n]
        send_sems, recv_sems, in_sems, out_sems = refs[2 * n:2 * n + 4]
        bufs = refs[2 * n + 4:3 * n + 4]
        x, y, c = lax.axis_index("x"), lax.axis_index("y"), lax.axis_index("c")

        def remote(k, h):
            a, r0, rows = pieces[k]
            return pltpu.make_async_remote_copy(
                src_ref=_rows(src[a], r0, rows), dst_ref=_rows(_axis_index(dst[a], half_axes[a], h), r0, rows),
                send_sem=send_sems.at[k], recv_sem=recv_sems.at[k], device_id=(x, y, 1 - c), device_id_type=MESH)

        sends = [remote(k, c) for k in range(len(pieces))]
        for cp in sends:
            cp.start()
        own = [(src[a], _axis_index(dst[a], half_axes[a], c)) for a in range(n)]
        local = _staged_local_copies(own, bufs, in_sems, out_sems)
        for k in range(len(pieces)):
            remote(k, 1 - c).wait_recv()
        for cp in sends:
            cp.wait_send()
        for cp in local:
            cp.wait()

    any_spec = pl.BlockSpec(memory_space=pl.ANY)
    return pl.pallas_call(
        body, name="join_halves",
        in_specs=[any_spec] * n, out_specs=[any_spec] * n,
        out_shape=[S(joined_shape(a), reduced[a].dtype) for a in range(n)],
        scratch_shapes=[pltpu.SemaphoreType.DMA((len(pieces),))] * 2 + [pltpu.SemaphoreType.DMA((n,))] * 2
        + [pltpu.VMEM(r.shape, r.dtype) for r in reduced],
        compiler_params=_params(0),
    )(*reduced)


def _sum_slots(name, parts):
    n = parts.shape[0]
    shape = parts.shape[1:]
    flat = parts.reshape((n, -1, shape[-1]))
    rows, cols = flat.shape[1:]
    rb = rows
    while rb * cols * 4 * (n + 1) * 2 > 24 * 1024 * 1024 and rb % 32 == 0:
        rb //= 2

    def body(p_ref, o_ref):
        acc = p_ref[0].astype(F32)
        for k in range(1, n):
            acc = acc + p_ref[k].astype(F32)
        o_ref[...] = acc

    out = pl.pallas_call(
        body, name=name,
        grid=(rows // rb,),
        in_specs=[pl.BlockSpec((n, rb, cols), lambda i: (0, i, 0))],
        out_specs=pl.BlockSpec((rb, cols), lambda i: (i, 0)),
        out_shape=S((rows, cols), F32),
        compiler_params=_params(1),
    )(flat)
    return out.reshape(shape)


SMALL_NAMES = ("norm_pre", "pool_w", "pool_b", "pool_scale", "conv_b", "conv_ln_g", "conv_ln_b", "norm_post")
BIG_NAMES = ("w_in", "w_pool_out", "conv_w", "w_conv_out", "w_attn_out", "w_o")
SHARD_AXIS = {"w_in": 2, "w_pool_out": 2, "conv_w": 2, "w_conv_out": 2, "w_attn_out": 2, "w_o": 1}
WEIGHT_ORDER = ("norm_pre", "w_in", "pool_w", "pool_b", "pool_scale", "w_pool_out", "conv_w", "conv_b",
                "conv_ln_g", "conv_ln_b", "w_conv_out", "w_attn_out", "w_o", "norm_post")


def _pack_small(parts):
    return jnp.concatenate([parts[n].reshape(-1, LANES) for n in SMALL_NAMES], axis=0)


def _unpack_small(packed, shapes):
    out, r0 = {}, 0
    for n in SMALL_NAMES:
        size = 1
        for d in shapes[n]:
            size *= d
        rows = size // LANES
        out[n] = packed[r0:r0 + rows].reshape(shapes[n])
        r0 += rows
    return out


def _layer_weights(full, small, l):
    row = lambda a: a[l][None, :]
    return (small["pool_w"][l].astype(BF16), small["pool_b"][l].reshape(1, UNIT), row(small["pool_scale"]),
            full["w_pool_out"][l], full["conv_w"][l], row(small["conv_b"]), row(small["conv_ln_g"]),
            row(small["conv_ln_b"]), full["w_conv_out"][l], full["w_attn_out"][l], full["w_o"][l],
            row(small["norm_post"]))


def _forward_backward(x, target, full, small):
    depth = full["w_in"].shape[0]
    acts = []
    for l in range(depth):
        g_pre = small["norm_pre"][l][None, :]
        proj = _inproj_fwd(x, g_pre, full["w_in"][l])
        o, bsum, kfirst = _attn_fwd(proj)
        weights = _layer_weights(full, small, l)
        x_next, cv = _mix_fwd(proj, o, x, weights)
        acts.append((x, proj, o, cv, bsum, kfirst, weights, g_pre))
        x = x_next
    loss_tile, dx = _loss_head(x, target)

    big = {n: [None] * depth for n in BIG_NAMES}
    sm = {n: [None] * depth for n in SMALL_NAMES}
    for l in reversed(range(depth)):
        x_in, proj, o, cv, bsum, kfirst, weights, g_pre = acts[l]
        dgates, dd, dcv, d_o, dwo, dwp, dwc, dwa, dpw, dvec = _mix_bwd(proj, o, cv, x_in, dx, weights)
        dpre, dcw = _halo_bwd(proj, dcv, dd, full["conv_w"][l])
        dq, dk, dv = _attn_bwd(proj, d_o, bsum, kfirst)
        dproj_parts = (dpre, dgates, dq, dk, dv)
        dx, dg_pre = _inproj_bwd_x(x_in, g_pre, full["w_in"][l], dproj_parts, dx)
        big["w_in"][l] = _inproj_bwd_w(x_in, g_pre, dproj_parts)
        big["w_pool_out"][l], big["w_conv_out"][l], big["w_attn_out"][l], big["w_o"][l] = dwp, dwc, dwa, dwo
        big["conv_w"][l] = dcw
        sm["norm_pre"][l] = dg_pre[0]
        sm["pool_w"][l] = dpw
        sm["norm_post"][l] = dvec[0]
        sm["pool_b"][l] = dvec[1, 0:UNIT].reshape(4, GROUP)
        sm["pool_scale"][l] = dvec[2, 0:UNIT]
        sm["conv_b"][l] = dvec[3, 0:UNIT]
        sm["conv_ln_g"][l] = dvec[4, 0:UNIT]
        sm["conv_ln_b"][l] = dvec[5, 0:UNIT]
    sm = {n: jnp.stack(v) for n, v in sm.items()}
    return loss_tile, dx, big, sm


def kernel(x, norm_pre, w_in, pool_w, pool_b, pool_scale, w_pool_out, conv_w, conv_b, conv_ln_g, conv_ln_b, w_conv_out, w_attn_out, w_o, norm_post, loss_target, m_norm_pre, m_w_in, m_pool_w, m_pool_b, m_pool_scale, m_w_pool_out, m_conv_w, m_conv_b, m_conv_ln_g, m_conv_ln_b, m_w_conv_out, m_w_attn_out, m_w_o, m_norm_post, v_norm_pre, v_w_in, v_pool_w, v_pool_b, v_pool_scale, v_w_pool_out, v_conv_w, v_conv_b, v_conv_ln_g, v_conv_ln_b, v_w_conv_out, v_w_attn_out, v_w_o, v_norm_post):
    w = dict(norm_pre=norm_pre, w_in=w_in, pool_w=pool_w, pool_b=pool_b, pool_scale=pool_scale,
             w_pool_out=w_pool_out, conv_w=conv_w, conv_b=conv_b, conv_ln_g=conv_ln_g, conv_ln_b=conv_ln_b,
             w_conv_out=w_conv_out, w_attn_out=w_attn_out, w_o=w_o, norm_post=norm_post)
    m = dict(norm_pre=m_norm_pre, w_in=m_w_in, pool_w=m_pool_w, pool_b=m_pool_b, pool_scale=m_pool_scale,
             w_pool_out=m_w_pool_out, conv_w=m_conv_w, conv_b=m_conv_b, conv_ln_g=m_conv_ln_g,
             conv_ln_b=m_conv_ln_b, w_conv_out=m_w_conv_out, w_attn_out=m_w_attn_out, w_o=m_w_o,
             norm_post=m_norm_post)
    v = dict(norm_pre=v_norm_pre, w_in=v_w_in, pool_w=v_pool_w, pool_b=v_pool_b, pool_scale=v_pool_scale,
             w_pool_out=v_w_pool_out, conv_w=v_conv_w, conv_b=v_conv_b, conv_ln_g=v_conv_ln_g,
             conv_ln_b=v_conv_ln_b, w_conv_out=v_w_conv_out, w_attn_out=v_w_attn_out, w_o=v_w_o,
             norm_post=v_norm_post)
    pad_taps = lambda a: jnp.pad(a, ((0, 0), (0, CONV_K_PAD - CONV_K), (0, 0)))

    shards = [pad_taps(w[n]) if n == "conv_w" else w[n].astype(BF16) for n in BIG_NAMES]
    full = dict(zip(BIG_NAMES, _gather_weights(shards, [SHARD_AXIS[n] for n in BIG_NAMES])))
    small = {n: w[n] for n in SMALL_NAMES}

    loss_tile, dx, big, sm = _forward_backward(x[0], loss_target[0], full, small)

    packed = jnp.concatenate([_pack_small(sm), loss_tile], axis=0)
    layer_axis = [0] * len(BIG_NAMES)
    theirs = _swap_layers([big[n] for n in BIG_NAMES])
    my_layer = lax.axis_index("c").astype(jnp.int32).reshape(1)
    chip_sums = [_add_own_layer("sum_cores_" + n, big[n], t, my_layer) for n, t in zip(BIG_NAMES, theirs)]
    exchanged = _send_to_owners(chip_sums, [SHARD_AXIS[n] for n in BIG_NAMES], packed)
    reduced = [_sum_slots("sum_chips_" + n, p) for n, p in zip(BIG_NAMES, exchanged[:-1])]
    joined = _join_halves(reduced, layer_axis)
    out = {}
    for n, g in zip(BIG_NAMES, joined):
        wn, mn, vn = (pad_taps(a[n]) if n == "conv_w" else a[n] for a in (w, m, v))
        shape = wn.shape
        flat = lambda a: a.reshape((-1, shape[-1]))
        res = _adamw("adamw_" + n, g.reshape((1, -1, shape[-1])), flat(wn), flat(mn), flat(vn))
        res = [r.reshape(shape) for r in res]
        out[n] = [r[:, :CONV_K] for r in res] if n == "conv_w" else res
    parts = exchanged[-1]
    n_small_rows = parts.shape[1] - 8
    zeros_tile = jnp.zeros((8, LANES), F32)
    packs = [jnp.concatenate([_pack_small(a), zeros_tile], axis=0) for a in (w, m, v)]
    res = _adamw("adamw_small", parts, *packs)
    loss = res[0][n_small_rows, 0]
    shapes = {n: w[n].shape for n in SMALL_NAMES}
    unpacked = [_unpack_small(r[:n_small_rows], shapes) for r in res]
    for n in SMALL_NAMES:
        out[n] = [u[n] for u in unpacked]

    grads = [out[n][0] for n in WEIGHT_ORDER]
    deltas = [out[n][1] for n in WEIGHT_ORDER]
    new_m = [out[n][2] for n in WEIGHT_ORDER]
    new_v = [out[n][3] for n in WEIGHT_ORDER]
    return (loss, dx[None], *grads, *deltas, *new_m, *new_v)
```

```python
import jax
import jax.numpy as jnp
from jax import lax
from jax.experimental import pallas as pl
from jax.experimental.pallas import tpu as pltpu

F32 = jnp.float32
BF16 = jnp.bfloat16

D_MODEL = 1024
UNIT = 512
N_UNITS = 15
IN_WIDTH = UNIT * N_UNITS
N_HEADS = 8
HEAD_DIM = 64
HEADS_PER_BLOCK = 2
N_HEAD_BLOCKS = N_HEADS // HEADS_PER_BLOCK
LANES = 128
CONV_K = 31
CONV_K_PAD = 32
HALO = 32
POOL_WINDOWS = (2, 4, 8, 16)
GROUP = 128
N_BRANCH = 3
RMS_EPS = 1e-6
LN_EPS = 1e-5
ATTN_SCALE = 0.125
EXP_ZERO_BELOW = -104.0

ADAM_LR = 0.001
ADAM_B1 = 0.9
ADAM_B2 = 0.999
ADAM_EPS = 1e-08
ADAM_WD = 0.01
ADAM_STEP = 10

U_P, U_GP, U_CA, U_CB, U_GC, U_Q, U_K, U_V, U_GA, U_GM = 0, 1, 2, 3, 4, 5, 6, 7, 8, 9

V7X_VMEM_LIMIT = 62 * 1024 * 1024
N_DEV = 8
MESH = pl.DeviceIdType.MESH

S = jax.ShapeDtypeStruct


def _params(n_grid):
    return pltpu.CompilerParams(dimension_semantics=("arbitrary",) * n_grid, vmem_limit_bytes=V7X_VMEM_LIMIT)


def _sigmoid(x):
    return 1.0 / (1.0 + jnp.exp(-x))


def _dsilu(x, s):
    return s * (1.0 + x * (1.0 - s))


def _dot(a, b):
    return jnp.dot(a, b, preferred_element_type=F32)


def _dot_nt(a, b):
    return lax.dot_general(a, b, (((1,), (1,)), ((), ())), preferred_element_type=F32)


def _dot_tn(a, b):
    return lax.dot_general(a, b, (((0,), (0,)), ((), ())), preferred_element_type=F32)


def _split_bf16(x):
    hi = x.astype(BF16)
    lo = (x - hi.astype(F32)).astype(BF16)
    return hi, lo


def _const_spec(shape):
    n = len(shape)
    return pl.BlockSpec(shape, lambda *_: (0,) * n, pipeline_mode=pl.Buffered(1))


def _inproj_fwd(x, g_pre, w_in):
    t = x.shape[0]
    tb = min(1024, t)
    ug = 3
    nb = ug * UNIT

    def body(x_ref, g_ref, w_ref, o_ref, h_ref):
        @pl.when(pl.program_id(1) == 0)
        def _():
            xf = x_ref[...]
            r = lax.rsqrt(jnp.mean(xf * xf, axis=-1, keepdims=True) + RMS_EPS)
            h_ref[...] = (xf * r * g_ref[...]).astype(BF16)

        acc = _dot(h_ref[...], w_ref[...])
        for u in range(ug):
            o_ref[u] = acc[:, u * UNIT:(u + 1) * UNIT]

    return pl.pallas_call(
        body, name="inproj_fwd",
        grid=(t // tb, N_UNITS // ug),
        in_specs=[pl.BlockSpec((tb, D_MODEL), lambda i, j: (i, 0)),
                  pl.BlockSpec((1, D_MODEL), lambda i, j: (0, 0)),
                  pl.BlockSpec((D_MODEL, nb), lambda i, j: (0, j))],
        out_specs=pl.BlockSpec((ug, tb, UNIT), lambda i, j: (j, i, 0)),
        out_shape=S((N_UNITS, t, UNIT), F32),
        scratch_shapes=[pltpu.VMEM((tb, D_MODEL), BF16)],
        compiler_params=_params(2),
    )(x, g_pre, w_in)


def _dproj_specs(tb):
    tok = pl.BlockSpec((tb, UNIT), lambda i: (i, 0))
    return [pl.BlockSpec((3, tb, UNIT), lambda i: (0, i, 0)), pl.BlockSpec((9, tb, UNIT), lambda i: (0, i, 0)),
            tok, tok, tok]


def _dproj_unit(u, dpre_ref, dgates_ref, dq_ref, dk_ref, dv_ref):
    pre = {U_P: 0, U_CA: 1, U_CB: 2}
    gate = {U_GP: 0, U_GC: 1, U_GA: 2}
    if u in pre:
        val = dpre_ref[pre[u]]
    elif u in gate:
        val = dgates_ref[gate[u]]
    elif u >= U_GM:
        val = dgates_ref[3 + u - U_GM]
    else:
        val = {U_Q: dq_ref, U_K: dk_ref, U_V: dv_ref}[u][...]
    return val.astype(BF16)


def _inproj_bwd_x(x, g_pre, w_in, dproj_parts, dxn):
    t = x.shape[0]
    tb = min(256, t)

    def body(x_ref, g_ref, w_ref, dpre_ref, dgates_ref, dq_ref, dk_ref, dv_ref, dxn_ref, dx_ref, dg_ref):
        @pl.when(pl.program_id(0) == 0)
        def _():
            dg_ref[...] = jnp.zeros_like(dg_ref)

        dh = jnp.zeros((tb, D_MODEL), F32)
        for u in range(N_UNITS):
            dh = dh + _dot_nt(_dproj_unit(u, dpre_ref, dgates_ref, dq_ref, dk_ref, dv_ref),
                              w_ref[:, u * UNIT:(u + 1) * UNIT])
        xf = x_ref[...]
        r = lax.rsqrt(jnp.mean(xf * xf, axis=-1, keepdims=True) + RMS_EPS)
        xhat = xf * r
        dg_ref[...] += jnp.sum(dh * xhat, axis=0, keepdims=True)
        dxhat = dh * g_ref[...]
        dx_ref[...] = dxn_ref[...] + r * (dxhat - xhat * jnp.mean(dxhat * xhat, axis=-1, keepdims=True))

    tokd = pl.BlockSpec((tb, D_MODEL), lambda i: (i, 0))
    return pl.pallas_call(
        body, name="inproj_bwd_x",
        grid=(t // tb,),
        in_specs=[tokd, _const_spec((1, D_MODEL)), _const_spec((D_MODEL, IN_WIDTH))] + _dproj_specs(tb) + [tokd],
        out_specs=[tokd, pl.BlockSpec((1, D_MODEL), lambda i: (0, 0))],
        out_shape=[S((t, D_MODEL), F32), S((1, D_MODEL), F32)],
        compiler_params=_params(1),
    )(x, g_pre, w_in, *dproj_parts, dxn)


def _inproj_bwd_w(x, g_pre, dproj_parts):
    t = x.shape[0]
    tb = min(256, t)
    nt = t // tb

    def body(x_ref, g_ref, dpre_ref, dgates_ref, dq_ref, dk_ref, dv_ref, dw_hbm, acc_ref):
        i = pl.program_id(0)
        xf = x_ref[...]
        r = lax.rsqrt(jnp.mean(xf * xf, axis=-1, keepdims=True) + RMS_EPS)
        ht = (xf * r * g_ref[...]).T.astype(BF16)

        @pl.when(i == 0)
        def _():
            acc_ref[...] = jnp.zeros_like(acc_ref)

        for u in range(N_UNITS):
            acc_ref[:, u * UNIT:(u + 1) * UNIT] += _dot(
                ht, _dproj_unit(u, dpre_ref, dgates_ref, dq_ref, dk_ref, dv_ref))

        @pl.when(i == nt - 1)
        def _():
            pltpu.sync_copy(acc_ref, dw_hbm)

    return pl.pallas_call(
        body, name="inproj_bwd_w",
        grid=(nt,),
        in_specs=[pl.BlockSpec((tb, D_MODEL), lambda i: (i, 0)), _const_spec((1, D_MODEL))] + _dproj_specs(tb),
        out_specs=pl.BlockSpec(memory_space=pl.ANY),
        out_shape=S((D_MODEL, IN_WIDTH), F32),
        scratch_shapes=[pltpu.VMEM((D_MODEL, IN_WIDTH), F32)],
        compiler_params=_params(1),
    )(x, g_pre, *dproj_parts)


def _attn_tile_size(t):
    return min(256, t)


def _softplus_parts(z):
    sp = jnp.maximum(z, 0.0) + jnp.log(1.0 + jnp.exp(-jnp.abs(z)))
    return -sp, z - sp


def _attn_fwd(proj):
    t = proj.shape[1]
    tq = _attn_tile_size(t)

    def body(q_ref, k_ref, v_ref, o_ref, bs_ref, kf_ref):
        i = pl.program_id(1)
        row = lax.broadcasted_iota(jnp.int32, (tq, tq), 0)
        col = lax.broadcasted_iota(jnp.int32, (tq, tq), 1)
        tri = (row > col).astype(BF16)
        tri2 = jnp.concatenate([tri, tri], axis=0)
        causal = col < row
        lane = lax.broadcasted_iota(jnp.int32, (1, LANES), 1)
        q_all = q_ref[0] * ATTN_SCALE
        in_heads = [(lane >= hh * HEAD_DIM) & (lane < (hh + 1) * HEAD_DIM) for hh in range(HEADS_PER_BLOCK)]
        q_heads = [jnp.where(in_head, q_all, 0.0).astype(BF16) for in_head in in_heads]

        def tiles(kj, state, masked):
            ks = pl.multiple_of(kj * tq, tq)
            kh = k_ref[0, pl.ds(ks, tq), :].astype(BF16)
            vh = v_ref[0, pl.ds(ks, tq), :].astype(BF16)
            new = []
            for qh, (carry, acc) in zip(q_heads, state):
                z = _dot_nt(qh, kh)
                lm, lb = _softplus_parts(z)
                if masked:
                    lm = jnp.where(causal, lm, 0.0)
                between = _dot(jnp.concatenate(_split_bf16(lm), axis=1), tri2) + carry
                w = jnp.exp(lb + between)
                if masked:
                    w = jnp.where(causal, w, 0.0)
                new.append((carry + jnp.sum(lm, axis=1, keepdims=True), acc + _dot(w.astype(BF16), vh)))
            return tuple(new)

        zero = (jnp.zeros((tq, 1), F32), jnp.zeros((tq, LANES), F32))
        state = tiles(i, (zero,) * HEADS_PER_BLOCK, True)

        def more(c):
            top = jnp.max(c[1][0][0])
            for carry, _ in c[1][1:]:
                top = jnp.maximum(top, jnp.max(carry))
            return jnp.logical_and(c[0] >= 0, top >= EXP_ZERO_BELOW)

        kj_end, state = lax.while_loop(more, lambda c: (c[0] - 1, tiles(c[0], c[1], False)), (i - 1, state))
        o_out, b_out = state[0][1], jnp.broadcast_to(state[0][0], (tq, LANES))
        for in_head, (carry, acc) in zip(in_heads[1:], state[1:]):
            o_out = jnp.where(in_head, acc, o_out)
            b_out = jnp.where(in_head, carry, b_out)
        o_ref[...] = o_out
        bs_ref[0] = b_out
        kf_ref[0, 0] = jnp.zeros((8, LANES), F32) + (kj_end + 1).astype(F32)

    nq = t // tq
    return pl.pallas_call(
        body, name="attn_fwd",
        grid=(N_HEAD_BLOCKS, nq),
        in_specs=[pl.BlockSpec((1, tq, LANES), lambda p, i: (U_Q, i, p)),
                  pl.BlockSpec((1, t, LANES), lambda p, i: (U_K, 0, p)),
                  pl.BlockSpec((1, t, LANES), lambda p, i: (U_V, 0, p))],
        out_specs=[pl.BlockSpec((tq, LANES), lambda p, i: (i, p)),
                   pl.BlockSpec((1, tq, LANES), lambda p, i: (p, i, 0)),
                   pl.BlockSpec((1, 1, 8, LANES), lambda p, i: (p, i, 0, 0))],
        out_shape=[S((t, UNIT), F32), S((N_HEAD_BLOCKS, t, LANES), F32), S((N_HEAD_BLOCKS, nq, 8, LANES), F32)],
        compiler_params=_params(2),
    )(proj, proj, proj)


def _attn_bwd(proj, d_o, bsum, kfirst):
    t = proj.shape[1]
    tq = _attn_tile_size(t)
    nq = t // tq

    def body(q_ref, k_ref, v_ref, do_ref, bs_ref, kf_ref, dq_ref, dk_ref, dv_ref, dkt_ref, dvt_ref):
        i = pl.program_id(1)

        @pl.when(i == 0)
        def _():
            dkt_ref[...] = jnp.zeros_like(dkt_ref)
            dvt_ref[...] = jnp.zeros_like(dvt_ref)

        row = lax.broadcasted_iota(jnp.int32, (tq, tq), 0)
        col = lax.broadcasted_iota(jnp.int32, (tq, tq), 1)
        upto = (row <= col).astype(BF16)
        before = (row < col).astype(BF16)
        upto2 = jnp.concatenate([upto, upto], axis=0)
        before2 = jnp.concatenate([before, before], axis=0)
        causal = col < row
        lane = lax.broadcasted_iota(jnp.int32, (1, LANES), 1)
        q_all = q_ref[0] * ATTN_SCALE
        do_all = do_ref[...].astype(F32)
        bs_all = bs_ref[0]
        k_first = jnp.clip(jnp.max(kf_ref[0, 0]).astype(jnp.int32), 0, i)

        in_heads = [(lane >= hh * HEAD_DIM) & (lane < (hh + 1) * HEAD_DIM) for hh in range(HEADS_PER_BLOCK)]
        heads = []
        for in_head in in_heads:
            q_m = jnp.where(in_head, q_all, 0.0)
            do_m = jnp.where(in_head, do_all, 0.0)
            btot = jnp.max(jnp.where(in_head, bs_all, -jnp.inf), axis=1, keepdims=True)
            heads.append((q_m.astype(BF16), do_m.astype(BF16), q_m.T.astype(BF16), do_m.T.astype(BF16), btot))

        def tiles(kj, state, masked):
            ks = pl.multiple_of(kj * tq, tq)
            kh = k_ref[0, pl.ds(ks, tq), :].astype(BF16)
            vh = v_ref[0, pl.ds(ks, tq), :].astype(BF16)
            new, dkt, dvt = [], None, None
            for (qh, doh, qt, dot_, btot), (c_b, c_p, dq) in zip(heads, state):
                z = _dot_nt(qh, kh)
                lm, lb = _softplus_parts(z)
                if masked:
                    lm = jnp.where(causal, lm, 0.0)
                between = btot - (c_b + _dot(jnp.concatenate(_split_bf16(lm), axis=1), upto2))
                w = jnp.exp(lb + between)
                if masked:
                    w = jnp.where(causal, w, 0.0)
                e = w * _dot_nt(doh, vh)
                p_sum = c_p + _dot(jnp.concatenate(_split_bf16(e), axis=1), before2)
                beta = jnp.exp(lb)
                dz = e * (1.0 - beta) - p_sum * beta
                if masked:
                    dz = jnp.where(causal, dz, 0.0)
                dzb = dz.astype(BF16)
                dkt_h, dvt_h = _dot(qt, dzb), _dot(dot_, w.astype(BF16))
                dkt = dkt_h if dkt is None else dkt + dkt_h
                dvt = dvt_h if dvt is None else dvt + dvt_h
                new.append((c_b + jnp.sum(lm, axis=1, keepdims=True), c_p + jnp.sum(e, axis=1, keepdims=True),
                            dq + _dot(dzb, kh)))
            dkt_ref[kj] += dkt
            dvt_ref[kj] += dvt
            return tuple(new)

        zero = (jnp.zeros((tq, 1), F32), jnp.zeros((tq, 1), F32), jnp.zeros((tq, LANES), F32))
        state = lax.fori_loop(k_first, i, lambda kj, st: tiles(kj, st, False), (zero,) * HEADS_PER_BLOCK)
        state = tiles(i, state, True)
        dq_out = state[0][2]
        for in_head, (_, _, dq) in zip(in_heads[1:], state[1:]):
            dq_out = jnp.where(in_head, dq, dq_out)
        dq_ref[...] = (dq_out * ATTN_SCALE).astype(BF16)

        @pl.when(i == nq - 1)
        def _():
            for kj in range(nq):
                dk_ref[kj * tq:(kj + 1) * tq, :] = dkt_ref[kj].T.astype(BF16)
                dv_ref[kj * tq:(kj + 1) * tq, :] = dvt_ref[kj].T.astype(BF16)

    return pl.pallas_call(
        body, name="attn_bwd",
        grid=(N_HEAD_BLOCKS, nq),
        in_specs=[pl.BlockSpec((1, tq, LANES), lambda p, i: (U_Q, i, p)),
                  pl.BlockSpec((1, t, LANES), lambda p, i: (U_K, 0, p)),
                  pl.BlockSpec((1, t, LANES), lambda p, i: (U_V, 0, p)),
                  pl.BlockSpec((tq, LANES), lambda p, i: (i, p)),
                  pl.BlockSpec((1, tq, LANES), lambda p, i: (p, i, 0)),
                  pl.BlockSpec((1, 1, 8, LANES), lambda p, i: (p, i, 0, 0))],
        out_specs=[pl.BlockSpec((tq, LANES), lambda p, i: (i, p)),
                   pl.BlockSpec((t, LANES), lambda p, i: (0, p)),
                   pl.BlockSpec((t, LANES), lambda p, i: (0, p))],
        out_shape=[S((t, UNIT), BF16)] * 3,
        scratch_shapes=[pltpu.VMEM((nq, LANES, tq), F32), pltpu.VMEM((nq, LANES, tq), F32)],
        compiler_params=_params(2),
    )(proj, proj, proj, d_o, bsum, kfirst)


N_MIX_VIEWS = 11
N_MIX_VIEWS_NO_CONV = 7


def _mix_in_specs(tb, conv_inputs=True):
    hb = tb // HALO

    def unit(u):
        return pl.BlockSpec((1, tb, UNIT), lambda i: (u, i, 0))

    def halo(u):
        return pl.BlockSpec((1, HALO, UNIT), lambda i: (u, jnp.maximum(i * hb - 1, 0), 0))

    conv = [unit(U_CA), halo(U_CA), unit(U_CB), halo(U_CB)] if conv_inputs else []
    return [unit(U_P), halo(U_P), unit(U_GP)] + conv + [
        unit(U_GC), unit(U_GA), pl.BlockSpec((3, tb, UNIT), lambda i: (3, i, 0)),
        pl.BlockSpec((3, tb, UNIT), lambda i: (4, i, 0))]


def _mix_weight_specs():
    return [_const_spec((4, GROUP, GROUP)), _const_spec((1, UNIT)), _const_spec((1, UNIT)),
            _const_spec((UNIT, D_MODEL)), _const_spec((CONV_K_PAD, UNIT)), _const_spec((1, UNIT)),
            _const_spec((1, UNIT)), _const_spec((1, UNIT)), _const_spec((UNIT, D_MODEL)),
            _const_spec((UNIT, D_MODEL)), _const_spec((D_MODEL, D_MODEL)), _const_spec((1, D_MODEL))]


def _shifted_reader(buf, rbuf, tb):
    if rbuf is None:
        return lambda s: buf[s:s + tb, :]
    length = tb + HALO - 8
    for b in range(1, 8):
        rbuf[b, :, :] = buf[b:b + length, :]

    def read(s):
        a, b = divmod(s, 8)
        return buf[s:s + tb, :] if b == 0 else rbuf[b, 8 * a:8 * a + tb, :]

    return read


def _mix_forward(i, tb, proj_refs, o_ref, w_refs, pbuf, ubuf=None, rbuf=None, cv_ref=None):
    if cv_ref is None:
        p_ref, ph_ref, gp_ref, ca_ref, cah_ref, cb_ref, cbh_ref, gc_ref, ga_ref, gm0_ref, gm1_ref = proj_refs
    else:
        p_ref, ph_ref, gp_ref, gc_ref, ga_ref, gm0_ref, gm1_ref = proj_refs
    (poolw_ref, poolb_ref, pscale_ref, wpo_ref, convw_ref, convb_ref, lng_ref, lnb_ref,
     wco_ref, wao_ref, wo_ref, gpost_ref) = w_refs
    first = i == 0
    r = {}

    pbuf[0:HALO, :] = jnp.where(first, 0.0, ph_ref[0])
    pbuf[HALO:HALO + tb, :] = p_ref[0]
    tpos = i * tb + lax.broadcasted_iota(jnp.int32, (tb, 1), 0)
    d_parts, y_parts = [], []
    for g, win in enumerate(POOL_WINDOWS):
        cs = slice(g * GROUP, (g + 1) * GROUP)
        cur = pbuf[HALO:HALO + tb, cs]
        s = cur
        for j in range(1, win):
            s = s + pbuf[HALO - j:HALO - j + tb, cs]
        cnt = jnp.minimum(tpos + 1, win).astype(F32)
        d_g = s / cnt - cur
        d_parts.append(d_g)
        y_parts.append(_dot(d_g.astype(BF16), poolw_ref[g]))
    r["d"] = d_parts
    y = jnp.concatenate(y_parts, axis=1) + poolb_ref[...]
    r["y"] = y
    mp = y * pscale_ref[...]
    gp = gp_ref[0]
    sgp = _sigmoid(gp)
    r["mp"], r["gp"], r["sgp"] = mp, gp, sgp
    ua = mp * (gp * sgp)

    if cv_ref is None:
        ah, bh = cah_ref[0], cbh_ref[0]
        ubuf[0:HALO, :] = jnp.where(first, 0.0, ah * _sigmoid(bh))
        ubuf[HALO:HALO + tb, :] = ca_ref[0] * _sigmoid(cb_ref[0])
        cv = jnp.zeros((tb, UNIT), F32) + convb_ref[...]
        off = HALO - (CONV_K - 1)
        u_at = _shifted_reader(ubuf, rbuf, tb)
        for k in range(CONV_K):
            cv = cv + convw_ref[k:k + 1, :] * u_at(off + k)
    else:
        cv = cv_ref[...]
    r["cv"] = cv
    mu = jnp.mean(cv, axis=-1, keepdims=True)
    cc = cv - mu
    rs = lax.rsqrt(jnp.mean(cc * cc, axis=-1, keepdims=True) + LN_EPS)
    nrm = cc * rs
    ln = nrm * lng_ref[...] + lnb_ref[...]
    sln = _sigmoid(ln)
    sc = ln * sln
    gc = gc_ref[0]
    sgc = _sigmoid(gc)
    r["rs"], r["nrm"], r["ln"], r["sln"], r["sc"], r["gc"], r["sgc"] = rs, nrm, ln, sln, sc, gc, sgc
    ub = sc * (gc * sgc)

    o = o_ref[...]
    ga = ga_ref[0]
    sga = _sigmoid(ga)
    r["o"], r["ga"], r["sga"] = o, ga, sga
    uc = o * (ga * sga)

    r["ua"], r["ub"], r["uc"] = ua.astype(BF16), ub.astype(BF16), uc.astype(BF16)
    ya = _dot(r["ua"], wpo_ref[...])
    yb = _dot(r["ub"], wco_ref[...])
    yc = _dot(r["uc"], wao_ref[...])
    g0 = _sigmoid(jnp.concatenate([gm0_ref[0], gm0_ref[1]], axis=1))
    g1 = _sigmoid(jnp.concatenate([gm0_ref[2], gm1_ref[0]], axis=1))
    g2 = _sigmoid(jnp.concatenate([gm1_ref[1], gm1_ref[2]], axis=1))
    r["ya"], r["yb"], r["yc"], r["g0"], r["g1"], r["g2"] = ya, yb, yc, g0, g1, g2
    m = (g0 * ya + g1 * yb + g2 * yc).astype(BF16)
    r["m"] = m
    out = _dot(m, wo_ref[...])
    r2 = lax.rsqrt(jnp.mean(out * out, axis=-1, keepdims=True) + RMS_EPS)
    r["n2"], r["r2"] = out * r2, r2
    return r


def _mix_fwd(proj, o, x, weights):
    t = x.shape[0]
    tb = min(256, t)
    nv = N_MIX_VIEWS

    def body(*refs):
        proj_refs, o_ref, x_ref = refs[0:nv], refs[nv], refs[nv + 1]
        w_refs = refs[nv + 2:nv + 14]
        xn_ref, cv_ref, pbuf, ubuf, rbuf = refs[nv + 14:nv + 19]
        r = _mix_forward(pl.program_id(0), tb, proj_refs, o_ref, w_refs, pbuf, ubuf, rbuf)
        xn_ref[...] = x_ref[...] + r["n2"] * w_refs[11][...]
        cv_ref[...] = r["cv"]

    return pl.pallas_call(
        body, name="mix_fwd",
        grid=(t // tb,),
        in_specs=_mix_in_specs(tb) + [pl.BlockSpec((tb, UNIT), lambda i: (i, 0)),
                                      pl.BlockSpec((tb, D_MODEL), lambda i: (i, 0))] + _mix_weight_specs(),
        out_specs=[pl.BlockSpec((tb, D_MODEL), lambda i: (i, 0)), pl.BlockSpec((tb, UNIT), lambda i: (i, 0))],
        out_shape=[S((t, D_MODEL), F32), S((t, UNIT), F32)],
        scratch_shapes=[pltpu.VMEM((HALO + tb, UNIT), F32), pltpu.VMEM((HALO + tb, UNIT), F32),
                        pltpu.VMEM((8, HALO + tb - 8, UNIT), F32)],
        compiler_params=_params(1),
    )(*([proj] * nv), o, x, *weights)


def _mix_bwd(proj, o, cv, x, dxn, weights):
    t = x.shape[0]
    tb = min(256, t)
    nt = t // tb

    nv = N_MIX_VIEWS_NO_CONV

    def body(*refs):
        proj_refs, o_ref, cv_ref, x_ref, dxn_ref = refs[0:nv], refs[nv], refs[nv + 1], refs[nv + 2], refs[nv + 3]
        w_refs = refs[nv + 4:nv + 16]
        (dg_ref, dd_ref, dcv_ref, do_ref, dwo_hbm, dwp_hbm, dwc_hbm, dwa_hbm, dpw_ref,
         dvec_ref) = refs[nv + 16:nv + 26]
        pbuf, dwo_ref, dwp_ref, dwc_ref, dwa_ref = refs[nv + 26:nv + 31]
        (poolw_ref, _, pscale_ref, wpo_ref, _, _, lng_ref, _, wco_ref, wao_ref, wo_ref, gpost_ref) = w_refs
        i = pl.program_id(0)

        @pl.when(i == 0)
        def _():
            for ref in (dwo_ref, dwp_ref, dwc_ref, dwa_ref, dpw_ref, dvec_ref):
                ref[...] = jnp.zeros_like(ref)

        r = _mix_forward(i, tb, proj_refs, o_ref, w_refs, pbuf, cv_ref=cv_ref)

        def colsum(v):
            return jnp.sum(v, axis=0, keepdims=True)

        dxn = dxn_ref[...]
        n2 = r["n2"]
        dvec_ref[0:1, :] += colsum(dxn * n2)
        dn2 = dxn * gpost_ref[...]
        dout = (r["r2"] * (dn2 - n2 * jnp.mean(dn2 * n2, axis=-1, keepdims=True))).astype(BF16)
        dm = _dot_nt(dout, wo_ref[...])
        dwo_ref[...] += _dot_tn(r["m"], dout)

        g0, g1, g2 = r["g0"], r["g1"], r["g2"]
        dgm = [dm * r["ya"] * g0 * (1.0 - g0), dm * r["yb"] * g1 * (1.0 - g1), dm * r["yc"] * g2 * (1.0 - g2)]
        for bidx in range(N_BRANCH):
            dg_ref[3 + 2 * bidx] = dgm[bidx][:, 0:UNIT].astype(BF16)
            dg_ref[4 + 2 * bidx] = dgm[bidx][:, UNIT:2 * UNIT].astype(BF16)
        dya = (dm * g0).astype(BF16)
        dyb = (dm * g1).astype(BF16)
        dyc = (dm * g2).astype(BF16)
        dua = _dot_nt(dya, wpo_ref[...])
        dub = _dot_nt(dyb, wco_ref[...])
        duc = _dot_nt(dyc, wao_ref[...])
        dwp_ref[...] += _dot_tn(r["ua"], dya)
        dwc_ref[...] += _dot_tn(r["ub"], dyb)
        dwa_ref[...] += _dot_tn(r["uc"], dyc)

        gp, sgp = r["gp"], r["sgp"]
        dmp = dua * (gp * sgp)
        dg_ref[0] = (dua * r["mp"] * _dsilu(gp, sgp)).astype(BF16)
        dvec_ref[2:3, 0:UNIT] += colsum(dmp * r["y"])
        dy = dmp * pscale_ref[...]
        dvec_ref[1:2, 0:UNIT] += colsum(dy)
        dd_parts = []
        for g in range(len(POOL_WINDOWS)):
            dy_g = dy[:, g * GROUP:(g + 1) * GROUP].astype(BF16)
            dd_parts.append(_dot_nt(dy_g, poolw_ref[g]))
            dpw_ref[g] += _dot_tn(r["d"][g].astype(BF16), dy_g)
        dd_ref[...] = jnp.concatenate(dd_parts, axis=1)

        gc, sgc = r["gc"], r["sgc"]
        dsc = dub * (gc * sgc)
        dg_ref[1] = (dub * r["sc"] * _dsilu(gc, sgc)).astype(BF16)
        dln = dsc * _dsilu(r["ln"], r["sln"])
        nrm = r["nrm"]
        dvec_ref[4:5, 0:UNIT] += colsum(dln * nrm)
        dvec_ref[5:6, 0:UNIT] += colsum(dln)
        dnrm = dln * lng_ref[...]
        dcv = r["rs"] * (dnrm - jnp.mean(dnrm, axis=-1, keepdims=True)
                         - nrm * jnp.mean(dnrm * nrm, axis=-1, keepdims=True))
        dvec_ref[3:4, 0:UNIT] += colsum(dcv)
        dcv_ref[...] = dcv

        ga, sga = r["ga"], r["sga"]
        do_ref[...] = (duc * (ga * sga)).astype(BF16)
        dg_ref[2] = (duc * r["o"] * _dsilu(ga, sga)).astype(BF16)

        @pl.when(i == nt - 1)
        def _():
            for acc, hbm in ((dwo_ref, dwo_hbm), (dwp_ref, dwp_hbm), (dwc_ref, dwc_hbm), (dwa_ref, dwa_hbm)):
                pltpu.sync_copy(acc, hbm)

    def acc_spec(shape):
        n = len(shape)
        return pl.BlockSpec(shape, lambda i: (0,) * n)

    tok = lambda w: pl.BlockSpec((tb, w), lambda i: (i, 0))
    any_spec = pl.BlockSpec(memory_space=pl.ANY)
    w_shapes = [(D_MODEL, D_MODEL), (UNIT, D_MODEL), (UNIT, D_MODEL), (UNIT, D_MODEL)]
    return pl.pallas_call(
        body, name="mix_bwd",
        grid=(nt,),
        in_specs=_mix_in_specs(tb, conv_inputs=False) + [tok(UNIT), tok(UNIT), tok(D_MODEL), tok(D_MODEL)]
        + _mix_weight_specs(),
        out_specs=[pl.BlockSpec((9, tb, UNIT), lambda i: (0, i, 0)), tok(UNIT), tok(UNIT), tok(UNIT)]
        + [any_spec] * 4 + [acc_spec((4, GROUP, GROUP)), acc_spec((8, D_MODEL))],
        out_shape=[S((9, t, UNIT), BF16), S((t, UNIT), F32), S((t, UNIT), F32), S((t, UNIT), BF16)]
        + [S(s, F32) for s in w_shapes] + [S((4, GROUP, GROUP), F32), S((8, D_MODEL), F32)],
        scratch_shapes=[pltpu.VMEM((HALO + tb, UNIT), F32)] + [pltpu.VMEM(s, F32) for s in w_shapes],
        compiler_params=_params(1),
    )(*([proj] * nv), o, cv, x, dxn, *weights)


def _halo_bwd(proj, dcv, dd, conv_w):
    t = dcv.shape[0]
    tb = min(256, t)
    hb = tb // HALO
    n_halo_blocks = t // HALO
    nt = t // tb

    def body(ca_ref, cah_ref, cb_ref, cbh_ref, dcv_ref, dcvh_ref, dd_ref, ddh_ref, cw_ref,
             dpre_ref, dcw_ref, ubuf, gbuf, nbuf, ru_buf, rg_buf, acc_ref):
        i = pl.program_id(0)
        first = i == 0
        last = i == nt - 1

        @pl.when(first)
        def _():
            acc_ref[...] = jnp.zeros_like(acc_ref)

        a, b = ca_ref[0], cb_ref[0]
        sb = _sigmoid(b)
        ubuf[0:HALO, :] = jnp.where(first, 0.0, cah_ref[0] * _sigmoid(cbh_ref[0]))
        ubuf[HALO:HALO + tb, :] = a * sb
        dcv_main = dcv_ref[...]
        gbuf[0:tb, :] = dcv_main
        gbuf[tb:tb + HALO, :] = jnp.where(last, 0.0, dcvh_ref[...])

        off = HALO - (CONV_K - 1)
        u_at = _shifted_reader(ubuf, ru_buf, tb)
        g_at = _shifted_reader(gbuf, rg_buf, tb)
        du = jnp.zeros((tb, UNIT), F32)
        for k in range(CONV_K):
            du = du + cw_ref[k:k + 1, :] * g_at(CONV_K - 1 - k)
            acc_ref[k] += jnp.sum((dcv_main * u_at(off + k)).reshape(tb // 8, 8, UNIT), axis=0)
        dpre_ref[1] = (du * sb).astype(BF16)
        dpre_ref[2] = (du * a * sb * (1.0 - sb)).astype(BF16)

        @pl.when(last)
        def _():
            dcw_ref[...] = jnp.sum(acc_ref[...], axis=1)

        tpos = i * tb + lax.broadcasted_iota(jnp.int32, (tb + HALO, 1), 0)
        dd_main = dd_ref[...]
        dd_ext = jnp.concatenate([dd_main, jnp.where(last, 0.0, ddh_ref[...])], axis=0)
        dp_parts = []
        for g, win in enumerate(POOL_WINDOWS):
            cs = slice(g * GROUP, (g + 1) * GROUP)
            cnt = jnp.minimum(tpos + 1, win).astype(F32)
            nbuf[:, cs] = dd_ext[:, cs] / cnt
        for g, win in enumerate(POOL_WINDOWS):
            cs = slice(g * GROUP, (g + 1) * GROUP)
            s = nbuf[0:tb, cs]
            for j in range(1, win):
                s = s + nbuf[j:j + tb, cs]
            dp_parts.append(s - dd_main[:, cs])
        dpre_ref[0] = jnp.concatenate(dp_parts, axis=1).astype(BF16)

    def unit(u):
        return pl.BlockSpec((1, tb, UNIT), lambda i: (u, i, 0))

    def past(u):
        return pl.BlockSpec((1, HALO, UNIT), lambda i: (u, jnp.maximum(i * hb - 1, 0), 0))

    tok = pl.BlockSpec((tb, UNIT), lambda i: (i, 0))
    future = pl.BlockSpec((HALO, UNIT), lambda i: (jnp.minimum((i + 1) * hb, n_halo_blocks - 1), 0))
    return pl.pallas_call(
        body, name="halo_bwd",
        grid=(nt,),
        in_specs=[unit(U_CA), past(U_CA), unit(U_CB), past(U_CB), tok, future, tok, future,
                  _const_spec((CONV_K_PAD, UNIT))],
        out_specs=[pl.BlockSpec((3, tb, UNIT), lambda i: (0, i, 0)),
                   pl.BlockSpec((CONV_K_PAD, UNIT), lambda i: (0, 0))],
        out_shape=[S((3, t, UNIT), BF16), S((CONV_K_PAD, UNIT), F32)],
        scratch_shapes=[pltpu.VMEM((HALO + tb, UNIT), F32)] * 3
        + [pltpu.VMEM((8, HALO + tb - 8, UNIT), F32)] * 2 + [pltpu.VMEM((CONV_K_PAD, 8, UNIT), F32)],
        compiler_params=_params(1),
    )(proj, proj, proj, proj, dcv, dcv, dd, dd, conv_w)


def _loss_head(y, target):
    t = y.shape[0]
    tb = min(512, t)

    def body(y_ref, t_ref, loss_ref, dy_ref):
        @pl.when(pl.program_id(0) == 0)
        def _():
            loss_ref[...] = jnp.zeros_like(loss_ref)

        err = y_ref[...] - t_ref[...]
        dy_ref[...] = err * (1.0 / D_MODEL)
        part = 0.5 * jnp.sum(jnp.mean(err * err, axis=-1, keepdims=True), axis=0, keepdims=True)
        r8 = lax.broadcasted_iota(jnp.int32, (8, LANES), 0)
        c8 = lax.broadcasted_iota(jnp.int32, (8, LANES), 1)
        loss_ref[...] += jnp.where((r8 == 0) & (c8 == 0), part, 0.0)

    return pl.pallas_call(
        body, name="loss_head",
        grid=(t // tb,),
        in_specs=[pl.BlockSpec((tb, D_MODEL), lambda i: (i, 0))] * 2,
        out_specs=[pl.BlockSpec((8, LANES), lambda i: (0, 0)), pl.BlockSpec((tb, D_MODEL), lambda i: (i, 0))],
        out_shape=[S((8, LANES), F32), S((t, D_MODEL), F32)],
        compiler_params=_params(1),
    )(y, target)


def _adamw(name, parts, w, m, v):
    n, rows, cols = parts.shape
    rb = rows
    while rb * cols * 4 * (n + 7) * 2 > 24 * 1024 * 1024 and rb % 16 == 0:
        rb //= 2

    def body(p_ref, w_ref, m_ref, v_ref, g_ref, d_ref, nm_ref, nv_ref):
        g = p_ref[0]
        for k in range(1, n):
            g = g + p_ref[k]
        nm = ADAM_B1 * m_ref[...] + (1.0 - ADAM_B1) * g
        nv = ADAM_B2 * v_ref[...] + (1.0 - ADAM_B2) * (g * g)
        m_hat = nm / (1.0 - ADAM_B1 ** ADAM_STEP)
        v_hat = nv / (1.0 - ADAM_B2 ** ADAM_STEP)
        g_ref[...] = g
        d_ref[...] = -ADAM_LR * (m_hat / (jnp.sqrt(v_hat) + ADAM_EPS) + ADAM_WD * w_ref[...])
        nm_ref[...] = nm
        nv_ref[...] = nv

    blk = pl.BlockSpec((rb, cols), lambda i: (i, 0))
    return pl.pallas_call(
        body, name=name,
        grid=(rows // rb,),
        in_specs=[pl.BlockSpec((n, rb, cols), lambda i: (0, i, 0)), blk, blk, blk],
        out_specs=[blk] * 4,
        out_shape=[S((rows, cols), F32)] * 4,
        compiler_params=_params(1),
    )(parts, w, m, v)


def _axis_slice(ref, axis, block, size):
    idx = [slice(None)] * len(ref.shape)
    idx[axis] = pl.ds(pl.multiple_of(block * size, size), size)
    return ref.at[tuple(idx)]


def _axis_index(ref, axis, index):
    return ref.at[tuple([slice(None)] * axis + [index])]


COPY_PIECE_BYTES = 2 * 1024 * 1024


def _row_pieces(shape):
    rows = shape[-2]
    size = 4 * rows * shape[-1]
    for d in shape[:-2]:
        size *= d
    n = 1
    while size // n > COPY_PIECE_BYTES and rows % (2 * n * 8) == 0:
        n *= 2
    return [(k * (rows // n), rows // n) for k in range(n)]


def _rows(ref, r0, rows):
    idx = [slice(None)] * len(ref.shape)
    idx[-2] = pl.ds(r0, rows)
    return ref.at[tuple(idx)]


def _staged_local_copies(pairs, bufs, in_sems, out_sems):
    ins = [pltpu.make_async_copy(src, buf, in_sems.at[k]) for k, ((src, _), buf) in enumerate(zip(pairs, bufs))]
    for cp in ins:
        cp.start()
    for cp in ins:
        cp.wait()
    outs = [pltpu.make_async_copy(buf, dst, out_sems.at[k]) for k, ((_, dst), buf) in enumerate(zip(pairs, bufs))]
    for cp in outs:
        cp.start()
    return outs


def _run_copies(local, sends, recvs):
    for cp in local + sends:
        cp.start()
    for cp in recvs:
        cp.wait_recv()
    for cp in sends:
        cp.wait_send()
    for cp in local:
        cp.wait()


def _gather_weights(shards, axes):
    n = len(shards)
    fulls = [S(tuple(4 * d if k == ax else d for k, d in enumerate(s.shape)), s.dtype) for s, ax in zip(shards, axes)]

    def body(*refs):
        src, dst = refs[0:n], refs[n:2 * n]
        send_sems, recv_sems, pass_send_sems, pass_recv_sems, in_sems, out_sems = refs[2 * n:2 * n + 6]
        bufs = refs[2 * n + 6:3 * n + 6]
        x, y, c = lax.axis_index("x"), lax.axis_index("y"), lax.axis_index("c")
        chips = [(1 - x, y), (x, 1 - y), (1 - x, 1 - y)]

        def place(a, chip, layer=None):
            ref = _axis_slice(dst[a], axes[a], 2 * chip[0] + chip[1], src[a].shape[axes[a]])
            return ref if layer is None else ref.at[pl.ds(layer, 1)]

        def over_ici(k, chip_of_block, to):
            a, j, r0, rows = pieces[k]
            return pltpu.make_async_remote_copy(
                src_ref=_rows(src[a].at[pl.ds(c, 1)], r0, rows), dst_ref=_rows(place(a, chip_of_block, c), r0, rows),
                send_sem=send_sems.at[k], recv_sem=recv_sems.at[k], device_id=(to[0], to[1], c), device_id_type=MESH)

        def to_sibling(k, chip_of_block, layer):
            a, j, r0, rows = pieces[k]
            block = _rows(place(a, chip_of_block, layer), r0, rows)
            return pltpu.make_async_remote_copy(
                src_ref=block, dst_ref=block, send_sem=pass_send_sems.at[k], recv_sem=pass_recv_sems.at[k],
                device_id=(x, y, 1 - c), device_id_type=MESH)

        first = [over_ici(k, (x, y), chips[j]) for k, (a, j, r0, rows) in enumerate(pieces)]
        for cp in first:
            cp.start()
        local = _staged_local_copies([(src[a], place(a, (x, y))) for a in range(n)], bufs, in_sems, out_sems)
        passed = []
        for k, (a, j, r0, rows) in enumerate(pieces):
            over_ici(k, chips[j], chips[j]).wait_recv()
            passed.append(to_sibling(k, chips[j], c))
            passed[-1].start()
        for k, (a, j, r0, rows) in enumerate(pieces):
            to_sibling(k, chips[j], 1 - c).wait_recv()
        for cp in first + passed:
            cp.wait_send()
        for cp in local:
            cp.wait()

    per_array = [_row_pieces((1,) + shards[a].shape[1:]) for a in range(n)]
    pieces = [(a, j, r0, rows) for a in range(n) for r0, rows in per_array[a] for j in range(3)]
    any_spec = pl.BlockSpec(memory_space=pl.ANY)
    return pl.pallas_call(
        body, name="gather_weights",
        in_specs=[any_spec] * n, out_specs=[any_spec] * n, out_shape=fulls,
        scratch_shapes=[pltpu.SemaphoreType.DMA((len(pieces),))] * 4 + [pltpu.SemaphoreType.DMA((n,))] * 2
        + [pltpu.VMEM(s.shape, s.dtype) for s in shards],
        compiler_params=_params(0),
    )(*shards)


def _swap_layers(layers):
    n = len(layers)
    pieces = [(a, r0, rows) for a in range(n) for r0, rows in _row_pieces(layers[a][0].shape)]

    def body(*refs):
        src = [refs[2 * a:2 * a + 2] for a in range(n)]
        dst = refs[2 * n:3 * n]
        send_sems, recv_sems = refs[3 * n:3 * n + 2]
        x, y, c = lax.axis_index("x"), lax.axis_index("y"), lax.axis_index("c")

        def remote(k, layer):
            a, r0, rows = pieces[k]
            return pltpu.make_async_remote_copy(
                src_ref=_rows(src[a][layer], r0, rows), dst_ref=_rows(dst[a], r0, rows),
                send_sem=send_sems.at[k], recv_sem=recv_sems.at[k], device_id=(x, y, 1 - c), device_id_type=MESH)

        for layer in range(2):
            @pl.when(c == 1 - layer)
            def _(layer=layer):
                for k in range(len(pieces)):
                    remote(k, layer).start()

        for k in range(len(pieces)):
            remote(k, 0).wait_recv()
        for k in range(len(pieces)):
            remote(k, 0).wait_send()

    any_spec = pl.BlockSpec(memory_space=pl.ANY)
    return pl.pallas_call(
        body, name="swap_layers",
        in_specs=[any_spec] * (2 * n), out_specs=[any_spec] * n,
        out_shape=[S(layers[a][0].shape, layers[a][0].dtype) for a in range(n)],
        scratch_shapes=[pltpu.SemaphoreType.DMA((len(pieces),))] * 2,
    )(*[arr for pair in layers for arr in pair])


def _add_own_layer(name, mine, theirs, layer):
    rows, cols = theirs.shape
    rb = rows
    while rb * cols * 4 * 4 * 2 > 24 * 1024 * 1024 and rb % 32 == 0:
        rb //= 2

    def body(layer_ref, m0_ref, m1_ref, t_ref, o_ref):
        own = jnp.where(layer_ref[0] == 0, m0_ref[...], m1_ref[...])
        o_ref[0] = (own + t_ref[...]).astype(BF16)

    def own_spec(which):
        return pl.BlockSpec((rb, cols), lambda i, l: (jnp.where(l[0] == which, i, 0), 0))

    return pl.pallas_call(
        body, name=name,
        grid_spec=pltpu.PrefetchScalarGridSpec(
            num_scalar_prefetch=1, grid=(rows // rb,),
            in_specs=[own_spec(0), own_spec(1), pl.BlockSpec((rb, cols), lambda i, l: (i, 0))],
            out_specs=pl.BlockSpec((1, rb, cols), lambda i, l: (0, i, 0))),
        out_shape=S((1, rows, cols), BF16),
        compiler_params=_params(1),
    )(layer, *mine, theirs)


def _send_to_owners(chip_halves, shard_axes, packed):
    n = len(chip_halves)

    def owned_shape(a):
        return tuple(d // 4 if k == shard_axes[a] else d for k, d in enumerate(chip_halves[a].shape))

    flips = [(dx, dy, dc) for dx in (0, 1) for dy in (0, 1) for dc in (0, 1)][1:]

    def body(*refs):
        src, psrc, dst, pdst = refs[0:n], refs[n], refs[n + 1:2 * n + 1], refs[2 * n + 1]
        send_sems, recv_sems, local_sems, psend_sems, precv_sems = refs[2 * n + 2:2 * n + 7]
        x, y, c = lax.axis_index("x"), lax.axis_index("y"), lax.axis_index("c")
        my_chip = 2 * x + y
        chips = [(1 - x, y), (x, 1 - y), (1 - x, 1 - y)]

        def owned(a, chip):
            return _axis_slice(src[a], shard_axes[a], 2 * chip[0] + chip[1], src[a].shape[shard_axes[a]] // 4)

        def remote(a, j, to, from_chip):
            return pltpu.make_async_remote_copy(
                src_ref=owned(a, to), dst_ref=dst[a].at[from_chip], send_sem=send_sems.at[a, j],
                recv_sem=recv_sems.at[a, j], device_id=(to[0], to[1], c), device_id_type=MESH)

        def peer(f):
            return (1 - x if f[0] else x, 1 - y if f[1] else y, 1 - c if f[2] else c)

        def premote(j, to, from_dev):
            return pltpu.make_async_remote_copy(
                src_ref=psrc, dst_ref=pdst.at[from_dev], send_sem=psend_sems.at[j], recv_sem=precv_sems.at[j],
                device_id=to, device_id_type=MESH)

        local = [pltpu.make_async_copy(_rows(owned(a, (x, y)), r0, rows), _rows(dst[a].at[my_chip], r0, rows),
                                       local_sems.at[k]) for k, (a, r0, rows) in enumerate(own_pieces)]
        local.append(pltpu.make_async_copy(psrc, pdst.at[4 * x + 2 * y + c], local_sems.at[len(own_pieces)]))
        sends = [remote(a, j, chip, my_chip) for a in range(n) for j, chip in enumerate(chips)]
        sends += [premote(j, peer(f), 4 * x + 2 * y + c) for j, f in enumerate(flips)]
        recvs = [remote(a, j, (x, y), 2 * chip[0] + chip[1]) for a in range(n) for j, chip in enumerate(chips)]
        for j, f in enumerate(flips):
            px, py, pc = peer(f)
            recvs.append(premote(j, (x, y, c), 4 * px + 2 * py + pc))
        _run_copies(local, sends, recvs)

    own_pieces = [(a, r0, rows) for a in range(n) for r0, rows in _row_pieces(owned_shape(a))]
    any_spec = pl.BlockSpec(memory_space=pl.ANY)
    return pl.pallas_call(
        body, name="send_to_owners",
        in_specs=[any_spec] * (n + 1), out_specs=[any_spec] * (n + 1),
        out_shape=[S((4,) + owned_shape(a), chip_halves[a].dtype) for a in range(n)]
        + [S((N_DEV,) + packed.shape, packed.dtype)],
        scratch_shapes=[pltpu.SemaphoreType.DMA((n, 3)), pltpu.SemaphoreType.DMA((n, 3)),
                        pltpu.SemaphoreType.DMA((len(own_pieces) + 1,)), pltpu.SemaphoreType.DMA((7,)),
                        pltpu.SemaphoreType.DMA((7,))],
    )(*chip_halves, packed)


def _join_halves(reduced, half_axes):
    n = len(reduced)

    def joined_shape(a):
        s, ax = reduced[a].shape, half_axes[a]
        return s[:ax] + (2,) + s[ax:]

    pieces = [(a, r0, rows) for a in range(n) for r0, rows in _row_pieces(reduced[a].shape)]

    def body(*refs):
        src, dst = refs[0:n], refs[n:2 * n]
        send_sems, recv_sems, in_sems, out_sems = refs[2 * n:2 * n + 4]
        bufs = refs[2 * n + 4:3 * n + 4]
        x, y, c = lax.axis_index("x"), lax.axis_index("y"), lax.axis_index("c")

        def remote(k, h):
            a, r0, rows = pieces[k]
            return pltpu.make_async_remote_copy(
                src_ref=_rows(src[a], r0, rows), dst_ref=_rows(_axis_index(dst[a], half_axes[a], h), r0, rows),
                send_sem=send_sems.at[k], recv_sem=recv_sems.at[k], device_id=(x, y, 1 - c), device_id_type=MESH)

        sends = [remote(k, c) for k in range(len(pieces))]
        for cp in sends:
            cp.start()
        own = [(src[a], _axis_index(dst[a], half_axes[a], c)) for a in range(n)]
        local = _staged_local_copies(own, bufs, in_sems, out_sems)
        for k in range(len(pieces)):
            remote(k, 1 - c).wait_recv()
        for cp in sends:
            cp.wait_send()
        for cp in local:
            cp.wait()

    any_spec = pl.BlockSpec(memory_space=pl.ANY)
    return pl.pallas_call(
        body, name="join_halves",
        in_specs=[any_spec] * n, out_specs=[any_spec] * n,
        out_shape=[S(joined_shape(a), reduced[a].dtype) for a in range(n)],
        scratch_shapes=[pltpu.SemaphoreType.DMA((len(pieces),))] * 2 + [pltpu.SemaphoreType.DMA((n,))] * 2
        + [pltpu.VMEM(r.shape, r.dtype) for r in reduced],
        compiler_params=_params(0),
    )(*reduced)


def _sum_slots(name, parts):
    n = parts.shape[0]
    shape = parts.shape[1:]
    flat = parts.reshape((n, -1, shape[-1]))
    rows, cols = flat.shape[1:]
    rb = rows
    while rb * cols * 4 * (n + 1) * 2 > 24 * 1024 * 1024 and rb % 32 == 0:
        rb //= 2

    def body(p_ref, o_ref):
        acc = p_ref[0].astype(F32)
        for k in range(1, n):
            acc = acc + p_ref[k].astype(F32)
        o_ref[...] = acc

    out = pl.pallas_call(
        body, name=name,
        grid=(rows // rb,),
        in_specs=[pl.BlockSpec((n, rb, cols), lambda i: (0, i, 0))],
        out_specs=pl.BlockSpec((rb, cols), lambda i: (i, 0)),
        out_shape=S((rows, cols), F32),
        compiler_params=_params(1),
    )(flat)
    return out.reshape(shape)


SMALL_NAMES = ("norm_pre", "pool_w", "pool_b", "pool_scale", "conv_b", "conv_ln_g", "conv_ln_b", "norm_post")
BIG_NAMES = ("w_in", "w_pool_out", "conv_w", "w_conv_out", "w_attn_out", "w_o")
SHARD_AXIS = {"w_in": 2, "w_pool_out": 2, "conv_w": 2, "w_conv_out": 2, "w_attn_out": 2, "w_o": 1}
WEIGHT_ORDER = ("norm_pre", "w_in", "pool_w", "pool_b", "pool_scale", "w_pool_out", "conv_w", "conv_b",
                "conv_ln_g", "conv_ln_b", "w_conv_out", "w_attn_out", "w_o", "norm_post")


def _pack_small(parts):
    return jnp.concatenate([parts[n].reshape(-1, LANES) for n in SMALL_NAMES], axis=0)


def _unpack_small(packed, shapes):
    out, r0 = {}, 0
    for n in SMALL_NAMES:
        size = 1
        for d in shapes[n]:
            size *= d
        rows = size // LANES
        out[n] = packed[r0:r0 + rows].reshape(shapes[n])
        r0 += rows
    return out


def _layer_weights(full, small, l):
    row = lambda a: a[l][None, :]
    return (small["pool_w"][l].astype(BF16), small["pool_b"][l].reshape(1, UNIT), row(small["pool_scale"]),
            full["w_pool_out"][l], full["conv_w"][l], row(small["conv_b"]), row(small["conv_ln_g"]),
            row(small["conv_ln_b"]), full["w_conv_out"][l], full["w_attn_out"][l], full["w_o"][l],
            row(small["norm_post"]))


def _forward_backward(x, target, full, small):
    depth = full["w_in"].shape[0]
    acts = []
    for l in range(depth):
        g_pre = small["norm_pre"][l][None, :]
        proj = _inproj_fwd(x, g_pre, full["w_in"][l])
        o, bsum, kfirst = _attn_fwd(proj)
        weights = _layer_weights(full, small, l)
        x_next, cv = _mix_fwd(proj, o, x, weights)
        acts.append((x, proj, o, cv, bsum, kfirst, weights, g_pre))
        x = x_next
    loss_tile, dx = _loss_head(x, target)

    big = {n: [None] * depth for n in BIG_NAMES}
    sm = {n: [None] * depth for n in SMALL_NAMES}
    for l in reversed(range(depth)):
        x_in, proj, o, cv, bsum, kfirst, weights, g_pre = acts[l]
        dgates, dd, dcv, d_o, dwo, dwp, dwc, dwa, dpw, dvec = _mix_bwd(proj, o, cv, x_in, dx, weights)
        dpre, dcw = _halo_bwd(proj, dcv, dd, full["conv_w"][l])
        dq, dk, dv = _attn_bwd(proj, d_o, bsum, kfirst)
        dproj_parts = (dpre, dgates, dq, dk, dv)
        dx, dg_pre = _inproj_bwd_x(x_in, g_pre, full["w_in"][l], dproj_parts, dx)
        big["w_in"][l] = _inproj_bwd_w(x_in, g_pre, dproj_parts)
        big["w_pool_out"][l], big["w_conv_out"][l], big["w_attn_out"][l], big["w_o"][l] = dwp, dwc, dwa, dwo
        big["conv_w"][l] = dcw
        sm["norm_pre"][l] = dg_pre[0]
        sm["pool_w"][l] = dpw
        sm["norm_post"][l] = dvec[0]
        sm["pool_b"][l] = dvec[1, 0:UNIT].reshape(4, GROUP)
        sm["pool_scale"][l] = dvec[2, 0:UNIT]
        sm["conv_b"][l] = dvec[3, 0:UNIT]
        sm["conv_ln_g"][l] = dvec[4, 0:UNIT]
        sm["conv_ln_b"][l] = dvec[5, 0:UNIT]
    sm = {n: jnp.stack(v) for n, v in sm.items()}
    return loss_tile, dx, big, sm


def kernel(x, norm_pre, w_in, pool_w, pool_b, pool_scale, w_pool_out, conv_w, conv_b, conv_ln_g, conv_ln_b, w_conv_out, w_attn_out, w_o, norm_post, loss_target, m_norm_pre, m_w_in, m_pool_w, m_pool_b, m_pool_scale, m_w_pool_out, m_conv_w, m_conv_b, m_conv_ln_g, m_conv_ln_b, m_w_conv_out, m_w_attn_out, m_w_o, m_norm_post, v_norm_pre, v_w_in, v_pool_w, v_pool_b, v_pool_scale, v_w_pool_out, v_conv_w, v_conv_b, v_conv_ln_g, v_conv_ln_b, v_w_conv_out, v_w_attn_out, v_w_o, v_norm_post):
    w = dict(norm_pre=norm_pre, w_in=w_in, pool_w=pool_w, pool_b=pool_b, pool_scale=pool_scale,
             w_pool_out=w_pool_out, conv_w=conv_w, conv_b=conv_b, conv_ln_g=conv_ln_g, conv_ln_b=conv_ln_b,
             w_conv_out=w_conv_out, w_attn_out=w_attn_out, w_o=w_o, norm_post=norm_post)
    m = dict(norm_pre=m_norm_pre, w_in=m_w_in, pool_w=m_pool_w, pool_b=m_pool_b, pool_scale=m_pool_scale,
             w_pool_out=m_w_pool_out, conv_w=m_conv_w, conv_b=m_conv_b, conv_ln_g=m_conv_ln_g,
             conv_ln_b=m_conv_ln_b, w_conv_out=m_w_conv_out, w_attn_out=m_w_attn_out, w_o=m_w_o,
             norm_post=m_norm_post)
    v = dict(norm_pre=v_norm_pre, w_in=v_w_in, pool_w=v_pool_w, pool_b=v_pool_b, pool_scale=v_pool_scale,
             w_pool_out=v_w_pool_out, conv_w=v_conv_w, conv_b=v_conv_b, conv_ln_g=v_conv_ln_g,
             conv_ln_b=v_conv_ln_b, w_conv_out=v_w_conv_out, w_attn_out=v_w_attn_out, w_o=v_w_o,
             norm_post=v_norm_post)
    pad_taps = lambda a: jnp.pad(a, ((0, 0), (0, CONV_K_PAD - CONV_K), (0, 0)))

    shards = [pad_taps(w[n]) if n == "conv_w" else w[n].astype(BF16) for n in BIG_NAMES]
    full = dict(zip(BIG_NAMES, _gather_weights(shards, [SHARD_AXIS[n] for n in BIG_NAMES])))
    small = {n: w[n] for n in SMALL_NAMES}

    loss_tile, dx, big, sm = _forward_backward(x[0], loss_target[0], full, small)

    packed = jnp.concatenate([_pack_small(sm), loss_tile], axis=0)
    layer_axis = [0] * len(BIG_NAMES)
    theirs = _swap_layers([big[n] for n in BIG_NAMES])
    my_layer = lax.axis_index("c").astype(jnp.int32).reshape(1)
    chip_sums = [_add_own_layer("sum_cores_" + n, big[n], t, my_layer) for n, t in zip(BIG_NAMES, theirs)]
    exchanged = _send_to_owners(chip_sums, [SHARD_AXIS[n] for n in BIG_NAMES], packed)
    reduced = [_sum_slots("sum_chips_" + n, p) for n, p in zip(BIG_NAMES, exchanged[:-1])]
    joined = _join_halves(reduced, layer_axis)
    out = {}
    for n, g in zip(BIG_NAMES, joined):
        wn, mn, vn = (pad_taps(a[n]) if n == "conv_w" else a[n] for a in (w, m, v))
        shape = wn.shape
        flat = lambda a: a.reshape((-1, shape[-1]))
        res = _adamw("adamw_" + n, g.reshape((1, -1, shape[-1])), flat(wn), flat(mn), flat(vn))
        res = [r.reshape(shape) for r in res]
        out[n] = [r[:, :CONV_K] for r in res] if n == "conv_w" else res
    parts = exchanged[-1]
    n_small_rows = parts.shape[1] - 8
    zeros_tile = jnp.zeros((8, LANES), F32)
    packs = [jnp.concatenate([_pack_small(a), zeros_tile], axis=0) for a in (w, m, v)]
    res = _adamw("adamw_small", parts, *packs)
    loss = res[0][n_small_rows, 0]
    shapes = {n: w[n].shape for n in SMALL_NAMES}
    unpacked = [_unpack_small(r[:n_small_rows], shapes) for r in res]
    for n in SMALL_NAMES:
        out[n] = [u[n] for u in unpacked]

    grads = [out[n][0] for n in WEIGHT_ORDER]
    deltas = [out[n][1] for n in WEIGHT_ORDER]
    new_m = [out[n][2] for n in WEIGHT_ORDER]
    new_v = [out[n][3] for n in WEIGHT_ORDER]
    return (loss, dx[None], *grads, *deltas, *new_m, *new_v)
```

```python
import jax
import jax.numpy as jnp
from jax import lax
from jax.experimental import pallas as pl
from jax.experimental.pallas import tpu as pltpu

F32 = jnp.float32
BF16 = jnp.bfloat16

D_MODEL = 1024
UNIT = 512
N_UNITS = 15
IN_WIDTH = UNIT * N_UNITS
N_HEADS = 8
HEAD_DIM = 64
HEADS_PER_BLOCK = 2
N_HEAD_BLOCKS = N_HEADS // HEADS_PER_BLOCK
LANES = 128
CONV_K = 31
CONV_K_PAD = 32
HALO = 32
POOL_WINDOWS = (2, 4, 8, 16)
GROUP = 128
N_BRANCH = 3
RMS_EPS = 1e-6
LN_EPS = 1e-5
ATTN_SCALE = 0.125
EXP_ZERO_BELOW = -104.0

ADAM_LR = 0.001
ADAM_B1 = 0.9
ADAM_B2 = 0.999
ADAM_EPS = 1e-08
ADAM_WD = 0.01
ADAM_STEP = 10

U_P, U_GP, U_CA, U_CB, U_GC, U_Q, U_K, U_V, U_GA, U_GM = 0, 1, 2, 3, 4, 5, 6, 7, 8, 9

V7X_VMEM_LIMIT = 62 * 1024 * 1024
N_DEV = 8
MESH = pl.DeviceIdType.MESH

S = jax.ShapeDtypeStruct


def _params(n_grid):
    return pltpu.CompilerParams(dimension_semantics=("arbitrary",) * n_grid, vmem_limit_bytes=V7X_VMEM_LIMIT)


def _sigmoid(x):
    return 0.5 * jnp.tanh(0.5 * x) + 0.5


def _dsilu(x, s):
    return s * (1.0 + x * (1.0 - s))


def _dot(a, b):
    return jnp.dot(a, b, preferred_element_type=F32)


def _dot_nt(a, b):
    return lax.dot_general(a, b, (((1,), (1,)), ((), ())), preferred_element_type=F32)


def _dot_tn(a, b):
    return lax.dot_general(a, b, (((0,), (0,)), ((), ())), preferred_element_type=F32)


def _split_bf16(x):
    hi = x.astype(BF16)
    lo = (x - hi.astype(F32)).astype(BF16)
    return hi, lo


def _const_spec(shape):
    n = len(shape)
    return pl.BlockSpec(shape, lambda *_: (0,) * n, pipeline_mode=pl.Buffered(1))


def _inproj_fwd(x, g_pre, w_in):
    t = x.shape[0]
    tb = min(1024, t)
    ug = 3
    nb = ug * UNIT

    def body(x_ref, g_ref, w_ref, o_ref, h_ref):
        @pl.when(pl.program_id(1) == 0)
        def _():
            xf = x_ref[...]
            r = lax.rsqrt(jnp.mean(xf * xf, axis=-1, keepdims=True) + RMS_EPS)
            h_ref[...] = (xf * r * g_ref[...]).astype(BF16)

        acc = _dot(h_ref[...], w_ref[...])
        for u in range(ug):
            o_ref[u] = acc[:, u * UNIT:(u + 1) * UNIT]

    return pl.pallas_call(
        body, name="inproj_fwd",
        grid=(t // tb, N_UNITS // ug),
        in_specs=[pl.BlockSpec((tb, D_MODEL), lambda i, j: (i, 0)),
                  pl.BlockSpec((1, D_MODEL), lambda i, j: (0, 0)),
                  pl.BlockSpec((D_MODEL, nb), lambda i, j: (0, j))],
        out_specs=pl.BlockSpec((ug, tb, UNIT), lambda i, j: (j, i, 0)),
        out_shape=S((N_UNITS, t, UNIT), F32),
        scratch_shapes=[pltpu.VMEM((tb, D_MODEL), BF16)],
        compiler_params=_params(2),
    )(x, g_pre, w_in)


def _dproj_specs(tb):
    tok = pl.BlockSpec((tb, UNIT), lambda i: (i, 0))
    return [pl.BlockSpec((3, tb, UNIT), lambda i: (0, i, 0)), pl.BlockSpec((9, tb, UNIT), lambda i: (0, i, 0)),
            tok, tok, tok]


def _dproj_unit(u, dpre_ref, dgates_ref, dq_ref, dk_ref, dv_ref):
    pre = {U_P: 0, U_CA: 1, U_CB: 2}
    gate = {U_GP: 0, U_GC: 1, U_GA: 2}
    if u in pre:
        val = dpre_ref[pre[u]]
    elif u in gate:
        val = dgates_ref[gate[u]]
    elif u >= U_GM:
        val = dgates_ref[3 + u - U_GM]
    else:
        val = {U_Q: dq_ref, U_K: dk_ref, U_V: dv_ref}[u][...]
    return val.astype(BF16)


def _inproj_bwd_x(x, g_pre, w_in, dproj_parts, dxn):
    t = x.shape[0]
    tb = min(256, t)

    def body(x_ref, g_ref, w_ref, dpre_ref, dgates_ref, dq_ref, dk_ref, dv_ref, dxn_ref, dx_ref, dg_ref):
        @pl.when(pl.program_id(0) == 0)
        def _():
            dg_ref[...] = jnp.zeros_like(dg_ref)

        dh = jnp.zeros((tb, D_MODEL), F32)
        for u in range(N_UNITS):
            dh = dh + _dot_nt(_dproj_unit(u, dpre_ref, dgates_ref, dq_ref, dk_ref, dv_ref),
                              w_ref[:, u * UNIT:(u + 1) * UNIT])
        xf = x_ref[...]
        r = lax.rsqrt(jnp.mean(xf * xf, axis=-1, keepdims=True) + RMS_EPS)
        xhat = xf * r
        dg_ref[...] += jnp.sum(dh * xhat, axis=0, keepdims=True)
        dxhat = dh * g_ref[...]
        dx_ref[...] = dxn_ref[...] + r * (dxhat - xhat * jnp.mean(dxhat * xhat, axis=-1, keepdims=True))

    tokd = pl.BlockSpec((tb, D_MODEL), lambda i: (i, 0))
    return pl.pallas_call(
        body, name="inproj_bwd_x",
        grid=(t // tb,),
        in_specs=[tokd, _const_spec((1, D_MODEL)), _const_spec((D_MODEL, IN_WIDTH))] + _dproj_specs(tb) + [tokd],
        out_specs=[tokd, pl.BlockSpec((1, D_MODEL), lambda i: (0, 0))],
        out_shape=[S((t, D_MODEL), F32), S((1, D_MODEL), F32)],
        compiler_params=_params(1),
    )(x, g_pre, w_in, *dproj_parts, dxn)


def _inproj_bwd_w(x, g_pre, dproj_parts):
    t = x.shape[0]
    tb = min(256, t)
    nt = t // tb

    def body(x_ref, g_ref, dpre_ref, dgates_ref, dq_ref, dk_ref, dv_ref, dw_hbm, acc_ref):
        i = pl.program_id(0)
        xf = x_ref[...]
        r = lax.rsqrt(jnp.mean(xf * xf, axis=-1, keepdims=True) + RMS_EPS)
        ht = (xf * r * g_ref[...]).T.astype(BF16)

        @pl.when(i == 0)
        def _():
            acc_ref[...] = jnp.zeros_like(acc_ref)

        for u in range(N_UNITS):
            acc_ref[:, u * UNIT:(u + 1) * UNIT] += _dot(
                ht, _dproj_unit(u, dpre_ref, dgates_ref, dq_ref, dk_ref, dv_ref))

        @pl.when(i == nt - 1)
        def _():
            pltpu.sync_copy(acc_ref, dw_hbm)

    return pl.pallas_call(
        body, name="inproj_bwd_w",
        grid=(nt,),
        in_specs=[pl.BlockSpec((tb, D_MODEL), lambda i: (i, 0)), _const_spec((1, D_MODEL))] + _dproj_specs(tb),
        out_specs=pl.BlockSpec(memory_space=pl.ANY),
        out_shape=S((D_MODEL, IN_WIDTH), F32),
        scratch_shapes=[pltpu.VMEM((D_MODEL, IN_WIDTH), F32)],
        compiler_params=_params(1),
    )(x, g_pre, *dproj_parts)


def _attn_tile_size(t):
    return min(256, t)


def _softplus_parts(z):
    sp = jnp.maximum(z, 0.0) + jnp.log(1.0 + jnp.exp(-jnp.abs(z)))
    return -sp, z - sp


def _attn_fwd(proj):
    t = proj.shape[1]
    tq = _attn_tile_size(t)

    def body(q_ref, k_ref, v_ref, o_ref, bs_ref, kf_ref):
        i = pl.program_id(1)
        row = lax.broadcasted_iota(jnp.int32, (tq, tq), 0)
        col = lax.broadcasted_iota(jnp.int32, (tq, tq), 1)
        tri = (row > col).astype(BF16)
        tri2 = jnp.concatenate([tri, tri], axis=0)
        causal = col < row
        lane = lax.broadcasted_iota(jnp.int32, (1, LANES), 1)
        q_all = q_ref[0] * ATTN_SCALE
        in_heads = [(lane >= hh * HEAD_DIM) & (lane < (hh + 1) * HEAD_DIM) for hh in range(HEADS_PER_BLOCK)]
        q_heads = [jnp.where(in_head, q_all, 0.0).astype(BF16) for in_head in in_heads]

        def tiles(kj, state, masked):
            ks = pl.multiple_of(kj * tq, tq)
            kh = k_ref[0, pl.ds(ks, tq), :].astype(BF16)
            vh = v_ref[0, pl.ds(ks, tq), :].astype(BF16)
            new = []
            for qh, (carry, acc) in zip(q_heads, state):
                z = _dot_nt(qh, kh)
                lm, lb = _softplus_parts(z)
                if masked:
                    lm = jnp.where(causal, lm, 0.0)
                between = _dot(jnp.concatenate(_split_bf16(lm), axis=1), tri2) + carry
                w = jnp.exp(lb + between)
                if masked:
                    w = jnp.where(causal, w, 0.0)
                new.append((carry + jnp.sum(lm, axis=1, keepdims=True), acc + _dot(w.astype(BF16), vh)))
            return tuple(new)

        zero = (jnp.zeros((tq, 1), F32), jnp.zeros((tq, LANES), F32))
        state = tiles(i, (zero,) * HEADS_PER_BLOCK, True)

        def more(c):
            top = jnp.max(c[1][0][0])
            for carry, _ in c[1][1:]:
                top = jnp.maximum(top, jnp.max(carry))
            return jnp.logical_and(c[0] >= 0, top >= EXP_ZERO_BELOW)

        kj_end, state = lax.while_loop(more, lambda c: (c[0] - 1, tiles(c[0], c[1], False)), (i - 1, state))
        o_out, b_out = state[0][1], jnp.broadcast_to(state[0][0], (tq, LANES))
        for in_head, (carry, acc) in zip(in_heads[1:], state[1:]):
            o_out = jnp.where(in_head, acc, o_out)
            b_out = jnp.where(in_head, carry, b_out)
        o_ref[...] = o_out
        bs_ref[0] = b_out
        kf_ref[0, 0] = jnp.zeros((8, LANES), F32) + (kj_end + 1).astype(F32)

    nq = t // tq
    return pl.pallas_call(
        body, name="attn_fwd",
        grid=(N_HEAD_BLOCKS, nq),
        in_specs=[pl.BlockSpec((1, tq, LANES), lambda p, i: (U_Q, i, p)),
                  pl.BlockSpec((1, t, LANES), lambda p, i: (U_K, 0, p)),
                  pl.BlockSpec((1, t, LANES), lambda p, i: (U_V, 0, p))],
        out_specs=[pl.BlockSpec((tq, LANES), lambda p, i: (i, p)),
                   pl.BlockSpec((1, tq, LANES), lambda p, i: (p, i, 0)),
                   pl.BlockSpec((1, 1, 8, LANES), lambda p, i: (p, i, 0, 0))],
        out_shape=[S((t, UNIT), F32), S((N_HEAD_BLOCKS, t, LANES), F32), S((N_HEAD_BLOCKS, nq, 8, LANES), F32)],
        compiler_params=_params(2),
    )(proj, proj, proj)


def _attn_bwd(proj, d_o, bsum, kfirst):
    t = proj.shape[1]
    tq = _attn_tile_size(t)
    nq = t // tq

    def body(q_ref, k_ref, v_ref, do_ref, bs_ref, kf_ref, dq_ref, dk_ref, dv_ref, dkt_ref, dvt_ref):
        i = pl.program_id(1)

        @pl.when(i == 0)
        def _():
            dkt_ref[...] = jnp.zeros_like(dkt_ref)
            dvt_ref[...] = jnp.zeros_like(dvt_ref)

        row = lax.broadcasted_iota(jnp.int32, (tq, tq), 0)
        col = lax.broadcasted_iota(jnp.int32, (tq, tq), 1)
        upto = (row <= col).astype(BF16)
        before = (row < col).astype(BF16)
        upto2 = jnp.concatenate([upto, upto], axis=0)
        before2 = jnp.concatenate([before, before], axis=0)
        causal = col < row
        lane = lax.broadcasted_iota(jnp.int32, (1, LANES), 1)
        q_all = q_ref[0] * ATTN_SCALE
        do_all = do_ref[...].astype(F32)
        bs_all = bs_ref[0]
        k_first = jnp.clip(jnp.max(kf_ref[0, 0]).astype(jnp.int32), 0, i)

        in_heads = [(lane >= hh * HEAD_DIM) & (lane < (hh + 1) * HEAD_DIM) for hh in range(HEADS_PER_BLOCK)]
        heads = []
        for in_head in in_heads:
            q_m = jnp.where(in_head, q_all, 0.0)
            do_m = jnp.where(in_head, do_all, 0.0)
            btot = jnp.max(jnp.where(in_head, bs_all, -jnp.inf), axis=1, keepdims=True)
            heads.append((q_m.astype(BF16), do_m.astype(BF16), q_m.T.astype(BF16), do_m.T.astype(BF16), btot))

        def tiles(kj, state, masked):
            ks = pl.multiple_of(kj * tq, tq)
            kh = k_ref[0, pl.ds(ks, tq), :].astype(BF16)
            vh = v_ref[0, pl.ds(ks, tq), :].astype(BF16)
            new, dkt, dvt = [], None, None
            for (qh, doh, qt, dot_, btot), (c_b, c_p, dq) in zip(heads, state):
                z = _dot_nt(qh, kh)
                lm, lb = _softplus_parts(z)
                if masked:
                    lm = jnp.where(causal, lm, 0.0)
                between = btot - (c_b + _dot(jnp.concatenate(_split_bf16(lm), axis=1), upto2))
                w = jnp.exp(lb + between)
                if masked:
                    w = jnp.where(causal, w, 0.0)
                e = w * _dot_nt(doh, vh)
                p_sum = c_p + _dot(jnp.concatenate(_split_bf16(e), axis=1), before2)
                beta = jnp.exp(lb)
                dz = e * (1.0 - beta) - p_sum * beta
                if masked:
                    dz = jnp.where(causal, dz, 0.0)
                dzb = dz.astype(BF16)
                dkt_h, dvt_h = _dot(qt, dzb), _dot(dot_, w.astype(BF16))
                dkt = dkt_h if dkt is None else dkt + dkt_h
                dvt = dvt_h if dvt is None else dvt + dvt_h
                new.append((c_b + jnp.sum(lm, axis=1, keepdims=True), c_p + jnp.sum(e, axis=1, keepdims=True),
                            dq + _dot(dzb, kh)))
            dkt_ref[kj] += dkt
            dvt_ref[kj] += dvt
            return tuple(new)

        zero = (jnp.zeros((tq, 1), F32), jnp.zeros((tq, 1), F32), jnp.zeros((tq, LANES), F32))
        state = lax.fori_loop(k_first, i, lambda kj, st: tiles(kj, st, False), (zero,) * HEADS_PER_BLOCK)
        state = tiles(i, state, True)
        dq_out = state[0][2]
        for in_head, (_, _, dq) in zip(in_heads[1:], state[1:]):
            dq_out = jnp.where(in_head, dq, dq_out)
        dq_ref[...] = (dq_out * ATTN_SCALE).astype(BF16)

        @pl.when(i == nq - 1)
        def _():
            for kj in range(nq):
                dk_ref[kj * tq:(kj + 1) * tq, :] = dkt_ref[kj].T.astype(BF16)
                dv_ref[kj * tq:(kj + 1) * tq, :] = dvt_ref[kj].T.astype(BF16)

    return pl.pallas_call(
        body, name="attn_bwd",
        grid=(N_HEAD_BLOCKS, nq),
        in_specs=[pl.BlockSpec((1, tq, LANES), lambda p, i: (U_Q, i, p)),
                  pl.BlockSpec((1, t, LANES), lambda p, i: (U_K, 0, p)),
                  pl.BlockSpec((1, t, LANES), lambda p, i: (U_V, 0, p)),
                  pl.BlockSpec((tq, LANES), lambda p, i: (i, p)),
                  pl.BlockSpec((1, tq, LANES), lambda p, i: (p, i, 0)),
                  pl.BlockSpec((1, 1, 8, LANES), lambda p, i: (p, i, 0, 0))],
        out_specs=[pl.BlockSpec((tq, LANES), lambda p, i: (i, p)),
                   pl.BlockSpec((t, LANES), lambda p, i: (0, p)),
                   pl.BlockSpec((t, LANES), lambda p, i: (0, p))],
        out_shape=[S((t, UNIT), BF16)] * 3,
        scratch_shapes=[pltpu.VMEM((nq, LANES, tq), F32), pltpu.VMEM((nq, LANES, tq), F32)],
        compiler_params=_params(2),
    )(proj, proj, proj, d_o, bsum, kfirst)


N_MIX_VIEWS = 11
N_MIX_VIEWS_NO_CONV = 7


def _mix_in_specs(tb, conv_inputs=True):
    hb = tb // HALO

    def unit(u):
        return pl.BlockSpec((1, tb, UNIT), lambda i: (u, i, 0))

    def halo(u):
        return pl.BlockSpec((1, HALO, UNIT), lambda i: (u, jnp.maximum(i * hb - 1, 0), 0))

    conv = [unit(U_CA), halo(U_CA), unit(U_CB), halo(U_CB)] if conv_inputs else []
    return [unit(U_P), halo(U_P), unit(U_GP)] + conv + [
        unit(U_GC), unit(U_GA), pl.BlockSpec((3, tb, UNIT), lambda i: (3, i, 0)),
        pl.BlockSpec((3, tb, UNIT), lambda i: (4, i, 0))]


def _mix_weight_specs():
    return [_const_spec((4, GROUP, GROUP)), _const_spec((1, UNIT)), _const_spec((1, UNIT)),
            _const_spec((UNIT, D_MODEL)), _const_spec((CONV_K_PAD, UNIT)), _const_spec((1, UNIT)),
            _const_spec((1, UNIT)), _const_spec((1, UNIT)), _const_spec((UNIT, D_MODEL)),
            _const_spec((UNIT, D_MODEL)), _const_spec((D_MODEL, D_MODEL)), _const_spec((1, D_MODEL))]


def _shifted_reader(buf, rbuf, tb):
    if rbuf is None:
        return lambda s: buf[s:s + tb, :]
    length = tb + HALO - 8
    for b in range(1, 8):
        rbuf[b, :, :] = buf[b:b + length, :]

    def read(s):
        a, b = divmod(s, 8)
        return buf[s:s + tb, :] if b == 0 else rbuf[b, 8 * a:8 * a + tb, :]

    return read


def _mix_forward(i, tb, proj_refs, o_ref, w_refs, pbuf, ubuf=None, rbuf=None, cv_ref=None):
    if cv_ref is None:
        p_ref, ph_ref, gp_ref, ca_ref, cah_ref, cb_ref, cbh_ref, gc_ref, ga_ref, gm0_ref, gm1_ref = proj_refs
    else:
        p_ref, ph_ref, gp_ref, gc_ref, ga_ref, gm0_ref, gm1_ref = proj_refs
    (poolw_ref, poolb_ref, pscale_ref, wpo_ref, convw_ref, convb_ref, lng_ref, lnb_ref,
     wco_ref, wao_ref, wo_ref, gpost_ref) = w_refs
    first = i == 0
    r = {}

    pbuf[0:HALO, :] = jnp.where(first, 0.0, ph_ref[0])
    pbuf[HALO:HALO + tb, :] = p_ref[0]
    tpos = i * tb + lax.broadcasted_iota(jnp.int32, (tb, 1), 0)
    d_parts, y_parts = [], []
    for g, win in enumerate(POOL_WINDOWS):
        cs = slice(g * GROUP, (g + 1) * GROUP)
        cur = pbuf[HALO:HALO + tb, cs]
        s = cur
        for j in range(1, win):
            s = s + pbuf[HALO - j:HALO - j + tb, cs]
        cnt = jnp.minimum(tpos + 1, win).astype(F32)
        d_g = s / cnt - cur
        d_parts.append(d_g)
        y_parts.append(_dot(d_g.astype(BF16), poolw_ref[g]))
    r["d"] = d_parts
    y = jnp.concatenate(y_parts, axis=1) + poolb_ref[...]
    r["y"] = y
    mp = y * pscale_ref[...]
    gp = gp_ref[0]
    sgp = _sigmoid(gp)
    r["mp"], r["gp"], r["sgp"] = mp, gp, sgp
    ua = mp * (gp * sgp)

    if cv_ref is None:
        ah, bh = cah_ref[0], cbh_ref[0]
        ubuf[0:HALO, :] = jnp.where(first, 0.0, ah * _sigmoid(bh))
        ubuf[HALO:HALO + tb, :] = ca_ref[0] * _sigmoid(cb_ref[0])
        cv = jnp.zeros((tb, UNIT), F32) + convb_ref[...]
        off = HALO - (CONV_K - 1)
        u_at = _shifted_reader(ubuf, rbuf, tb)
        for k in range(CONV_K):
            cv = cv + convw_ref[k:k + 1, :] * u_at(off + k)
    else:
        cv = cv_ref[...]
    r["cv"] = cv
    mu = jnp.mean(cv, axis=-1, keepdims=True)
    cc = cv - mu
    rs = lax.rsqrt(jnp.mean(cc * cc, axis=-1, keepdims=True) + LN_EPS)
    nrm = cc * rs
    ln = nrm * lng_ref[...] + lnb_ref[...]
    sln = _sigmoid(ln)
    sc = ln * sln
    gc = gc_ref[0]
    sgc = _sigmoid(gc)
    r["rs"], r["nrm"], r["ln"], r["sln"], r["sc"], r["gc"], r["sgc"] = rs, nrm, ln, sln, sc, gc, sgc
    ub = sc * (gc * sgc)

    o = o_ref[...]
    ga = ga_ref[0]
    sga = _sigmoid(ga)
    r["o"], r["ga"], r["sga"] = o, ga, sga
    uc = o * (ga * sga)

    r["ua"], r["ub"], r["uc"] = ua.astype(BF16), ub.astype(BF16), uc.astype(BF16)
    ya = _dot(r["ua"], wpo_ref[...])
    yb = _dot(r["ub"], wco_ref[...])
    yc = _dot(r["uc"], wao_ref[...])
    g0 = _sigmoid(jnp.concatenate([gm0_ref[0], gm0_ref[1]], axis=1))
    g1 = _sigmoid(jnp.concatenate([gm0_ref[2], gm1_ref[0]], axis=1))
    g2 = _sigmoid(jnp.concatenate([gm1_ref[1], gm1_ref[2]], axis=1))
    r["ya"], r["yb"], r["yc"], r["g0"], r["g1"], r["g2"] = ya, yb, yc, g0, g1, g2
    m = (g0 * ya + g1 * yb + g2 * yc).astype(BF16)
    r["m"] = m
    out = _dot(m, wo_ref[...])
    r2 = lax.rsqrt(jnp.mean(out * out, axis=-1, keepdims=True) + RMS_EPS)
    r["n2"], r["r2"] = out * r2, r2
    return r


def _mix_fwd(proj, o, x, weights):
    t = x.shape[0]
    tb = min(256, t)
    nv = N_MIX_VIEWS

    def body(*refs):
        proj_refs, o_ref, x_ref = refs[0:nv], refs[nv], refs[nv + 1]
        w_refs = refs[nv + 2:nv + 14]
        xn_ref, cv_ref, pbuf, ubuf, rbuf = refs[nv + 14:nv + 19]
        r = _mix_forward(pl.program_id(0), tb, proj_refs, o_ref, w_refs, pbuf, ubuf, rbuf)
        xn_ref[...] = x_ref[...] + r["n2"] * w_refs[11][...]
        cv_ref[...] = r["cv"]

    return pl.pallas_call(
        body, name="mix_fwd",
        grid=(t // tb,),
        in_specs=_mix_in_specs(tb) + [pl.BlockSpec((tb, UNIT), lambda i: (i, 0)),
                                      pl.BlockSpec((tb, D_MODEL), lambda i: (i, 0))] + _mix_weight_specs(),
        out_specs=[pl.BlockSpec((tb, D_MODEL), lambda i: (i, 0)), pl.BlockSpec((tb, UNIT), lambda i: (i, 0))],
        out_shape=[S((t, D_MODEL), F32), S((t, UNIT), F32)],
        scratch_shapes=[pltpu.VMEM((HALO + tb, UNIT), F32), pltpu.VMEM((HALO + tb, UNIT), F32),
                        pltpu.VMEM((8, HALO + tb - 8, UNIT), F32)],
        compiler_params=_params(1),
    )(*([proj] * nv), o, x, *weights)


def _mix_bwd(proj, o, cv, x, dxn, weights):
    t = x.shape[0]
    tb = min(256, t)
    nt = t // tb

    nv = N_MIX_VIEWS_NO_CONV

    def body(*refs):
        proj_refs, o_ref, cv_ref, x_ref, dxn_ref = refs[0:nv], refs[nv], refs[nv + 1], refs[nv + 2], refs[nv + 3]
        w_refs = refs[nv + 4:nv + 16]
        (dg_ref, dd_ref, dcv_ref, do_ref, dwo_hbm, dwp_hbm, dwc_hbm, dwa_hbm, dpw_ref,
         dvec_ref) = refs[nv + 16:nv + 26]
        pbuf, dwo_ref, dwp_ref, dwc_ref, dwa_ref = refs[nv + 26:nv + 31]
        (poolw_ref, _, pscale_ref, wpo_ref, _, _, lng_ref, _, wco_ref, wao_ref, wo_ref, gpost_ref) = w_refs
        i = pl.program_id(0)

        @pl.when(i == 0)
        def _():
            for ref in (dwo_ref, dwp_ref, dwc_ref, dwa_ref, dpw_ref, dvec_ref):
                ref[...] = jnp.zeros_like(ref)

        r = _mix_forward(i, tb, proj_refs, o_ref, w_refs, pbuf, cv_ref=cv_ref)

        def colsum(v):
            return jnp.sum(v, axis=0, keepdims=True)

        dxn = dxn_ref[...]
        n2 = r["n2"]
        dvec_ref[0:1, :] += colsum(dxn * n2)
        dn2 = dxn * gpost_ref[...]
        dout = (r["r2"] * (dn2 - n2 * jnp.mean(dn2 * n2, axis=-1, keepdims=True))).astype(BF16)
        dm = _dot_nt(dout, wo_ref[...])
        dwo_ref[...] += _dot_tn(r["m"], dout)

        g0, g1, g2 = r["g0"], r["g1"], r["g2"]
        dgm = [dm * r["ya"] * g0 * (1.0 - g0), dm * r["yb"] * g1 * (1.0 - g1), dm * r["yc"] * g2 * (1.0 - g2)]
        for bidx in range(N_BRANCH):
            dg_ref[3 + 2 * bidx] = dgm[bidx][:, 0:UNIT].astype(BF16)
            dg_ref[4 + 2 * bidx] = dgm[bidx][:, UNIT:2 * UNIT].astype(BF16)
        dya = (dm * g0).astype(BF16)
        dyb = (dm * g1).astype(BF16)
        dyc = (dm * g2).astype(BF16)
        dua = _dot_nt(dya, wpo_ref[...])
        dub = _dot_nt(dyb, wco_ref[...])
        duc = _dot_nt(dyc, wao_ref[...])
        dwp_ref[...] += _dot_tn(r["ua"], dya)
        dwc_ref[...] += _dot_tn(r["ub"], dyb)
        dwa_ref[...] += _dot_tn(r["uc"], dyc)

        gp, sgp = r["gp"], r["sgp"]
        dmp = dua * (gp * sgp)
        dg_ref[0] = (dua * r["mp"] * _dsilu(gp, sgp)).astype(BF16)
        dvec_ref[2:3, 0:UNIT] += colsum(dmp * r["y"])
        dy = dmp * pscale_ref[...]
        dvec_ref[1:2, 0:UNIT] += colsum(dy)
        dd_parts = []
        for g in range(len(POOL_WINDOWS)):
            dy_g = dy[:, g * GROUP:(g + 1) * GROUP].astype(BF16)
            dd_parts.append(_dot_nt(dy_g, poolw_ref[g]))
            dpw_ref[g] += _dot_tn(r["d"][g].astype(BF16), dy_g)
        dd_ref[...] = jnp.concatenate(dd_parts, axis=1)

        gc, sgc = r["gc"], r["sgc"]
        dsc = dub * (gc * sgc)
        dg_ref[1] = (dub * r["sc"] * _dsilu(gc, sgc)).astype(BF16)
        dln = dsc * _dsilu(r["ln"], r["sln"])
        nrm = r["nrm"]
        dvec_ref[4:5, 0:UNIT] += colsum(dln * nrm)
        dvec_ref[5:6, 0:UNIT] += colsum(dln)
        dnrm = dln * lng_ref[...]
        dcv = r["rs"] * (dnrm - jnp.mean(dnrm, axis=-1, keepdims=True)
                         - nrm * jnp.mean(dnrm * nrm, axis=-1, keepdims=True))
        dvec_ref[3:4, 0:UNIT] += colsum(dcv)
        dcv_ref[...] = dcv

        ga, sga = r["ga"], r["sga"]
        do_ref[...] = (duc * (ga * sga)).astype(BF16)
        dg_ref[2] = (duc * r["o"] * _dsilu(ga, sga)).astype(BF16)

        @pl.when(i == nt - 1)
        def _():
            for acc, hbm in ((dwo_ref, dwo_hbm), (dwp_ref, dwp_hbm), (dwc_ref, dwc_hbm), (dwa_ref, dwa_hbm)):
                pltpu.sync_copy(acc, hbm)

    def acc_spec(shape):
        n = len(shape)
        return pl.BlockSpec(shape, lambda i: (0,) * n)

    tok = lambda w: pl.BlockSpec((tb, w), lambda i: (i, 0))
    any_spec = pl.BlockSpec(memory_space=pl.ANY)
    w_shapes = [(D_MODEL, D_MODEL), (UNIT, D_MODEL), (UNIT, D_MODEL), (UNIT, D_MODEL)]
    return pl.pallas_call(
        body, name="mix_bwd",
        grid=(nt,),
        in_specs=_mix_in_specs(tb, conv_inputs=False) + [tok(UNIT), tok(UNIT), tok(D_MODEL), tok(D_MODEL)]
        + _mix_weight_specs(),
        out_specs=[pl.BlockSpec((9, tb, UNIT), lambda i: (0, i, 0)), tok(UNIT), tok(UNIT), tok(UNIT)]
        + [any_spec] * 4 + [acc_spec((4, GROUP, GROUP)), acc_spec((8, D_MODEL))],
        out_shape=[S((9, t, UNIT), BF16), S((t, UNIT), F32), S((t, UNIT), F32), S((t, UNIT), BF16)]
        + [S(s, F32) for s in w_shapes] + [S((4, GROUP, GROUP), F32), S((8, D_MODEL), F32)],
        scratch_shapes=[pltpu.VMEM((HALO + tb, UNIT), F32)] + [pltpu.VMEM(s, F32) for s in w_shapes],
        compiler_params=_params(1),
    )(*([proj] * nv), o, cv, x, dxn, *weights)


def _halo_bwd(proj, dcv, dd, conv_w):
    t = dcv.shape[0]
    tb = min(256, t)
    hb = tb // HALO
    n_halo_blocks = t // HALO
    nt = t // tb

    def body(ca_ref, cah_ref, cb_ref, cbh_ref, dcv_ref, dcvh_ref, dd_ref, ddh_ref, cw_ref,
             dpre_ref, dcw_ref, ubuf, gbuf, nbuf, ru_buf, rg_buf, acc_ref):
        i = pl.program_id(0)
        first = i == 0
        last = i == nt - 1

        @pl.when(first)
        def _():
            acc_ref[...] = jnp.zeros_like(acc_ref)

        a, b = ca_ref[0], cb_ref[0]
        sb = _sigmoid(b)
        ubuf[0:HALO, :] = jnp.where(first, 0.0, cah_ref[0] * _sigmoid(cbh_ref[0]))
        ubuf[HALO:HALO + tb, :] = a * sb
        dcv_main = dcv_ref[...]
        gbuf[0:tb, :] = dcv_main
        gbuf[tb:tb + HALO, :] = jnp.where(last, 0.0, dcvh_ref[...])

        off = HALO - (CONV_K - 1)
        u_at = _shifted_reader(ubuf, ru_buf, tb)
        g_at = _shifted_reader(gbuf, rg_buf, tb)
        du = jnp.zeros((tb, UNIT), F32)
        for k in range(CONV_K):
            du = du + cw_ref[k:k + 1, :] * g_at(CONV_K - 1 - k)
            acc_ref[k] += jnp.sum((dcv_main * u_at(off + k)).reshape(tb // 8, 8, UNIT), axis=0)
        dpre_ref[1] = (du * sb).astype(BF16)
        dpre_ref[2] = (du * a * sb * (1.0 - sb)).astype(BF16)

        @pl.when(last)
        def _():
            dcw_ref[...] = jnp.sum(acc_ref[...], axis=1)

        tpos = i * tb + lax.broadcasted_iota(jnp.int32, (tb + HALO, 1), 0)
        dd_main = dd_ref[...]
        dd_ext = jnp.concatenate([dd_main, jnp.where(last, 0.0, ddh_ref[...])], axis=0)
        dp_parts = []
        for g, win in enumerate(POOL_WINDOWS):
            cs = slice(g * GROUP, (g + 1) * GROUP)
            cnt = jnp.minimum(tpos + 1, win).astype(F32)
            nbuf[:, cs] = dd_ext[:, cs] / cnt
        for g, win in enumerate(POOL_WINDOWS):
            cs = slice(g * GROUP, (g + 1) * GROUP)
            s = nbuf[0:tb, cs]
            for j in range(1, win):
                s = s + nbuf[j:j + tb, cs]
            dp_parts.append(s - dd_main[:, cs])
        dpre_ref[0] = jnp.concatenate(dp_parts, axis=1).astype(BF16)

    def unit(u):
        return pl.BlockSpec((1, tb, UNIT), lambda i: (u, i, 0))

    def past(u):
        return pl.BlockSpec((1, HALO, UNIT), lambda i: (u, jnp.maximum(i * hb - 1, 0), 0))

    tok = pl.BlockSpec((tb, UNIT), lambda i: (i, 0))
    future = pl.BlockSpec((HALO, UNIT), lambda i: (jnp.minimum((i + 1) * hb, n_halo_blocks - 1), 0))
    return pl.pallas_call(
        body, name="halo_bwd",
        grid=(nt,),
        in_specs=[unit(U_CA), past(U_CA), unit(U_CB), past(U_CB), tok, future, tok, future,
                  _const_spec((CONV_K_PAD, UNIT))],
        out_specs=[pl.BlockSpec((3, tb, UNIT), lambda i: (0, i, 0)),
                   pl.BlockSpec((CONV_K_PAD, UNIT), lambda i: (0, 0))],
        out_shape=[S((3, t, UNIT), BF16), S((CONV_K_PAD, UNIT), F32)],
        scratch_shapes=[pltpu.VMEM((HALO + tb, UNIT), F32)] * 3
        + [pltpu.VMEM((8, HALO + tb - 8, UNIT), F32)] * 2 + [pltpu.VMEM((CONV_K_PAD, 8, UNIT), F32)],
        compiler_params=_params(1),
    )(proj, proj, proj, proj, dcv, dcv, dd, dd, conv_w)


def _loss_head(y, target):
    t = y.shape[0]
    tb = min(512, t)

    def body(y_ref, t_ref, loss_ref, dy_ref):
        @pl.when(pl.program_id(0) == 0)
        def _():
            loss_ref[...] = jnp.zeros_like(loss_ref)

        err = y_ref[...] - t_ref[...]
        dy_ref[...] = err * (1.0 / D_MODEL)
        part = 0.5 * jnp.sum(jnp.mean(err * err, axis=-1, keepdims=True), axis=0, keepdims=True)
        r8 = lax.broadcasted_iota(jnp.int32, (8, LANES), 0)
        c8 = lax.broadcasted_iota(jnp.int32, (8, LANES), 1)
        loss_ref[...] += jnp.where((r8 == 0) & (c8 == 0), part, 0.0)

    return pl.pallas_call(
        body, name="loss_head",
        grid=(t // tb,),
        in_specs=[pl.BlockSpec((tb, D_MODEL), lambda i: (i, 0))] * 2,
        out_specs=[pl.BlockSpec((8, LANES), lambda i: (0, 0)), pl.BlockSpec((tb, D_MODEL), lambda i: (i, 0))],
        out_shape=[S((8, LANES), F32), S((t, D_MODEL), F32)],
        compiler_params=_params(1),
    )(y, target)


def _adamw(name, parts, w, m, v):
    n, rows, cols = parts.shape
    rb = rows
    while rb * cols * 4 * (n + 7) * 2 > 24 * 1024 * 1024 and rb % 16 == 0:
        rb //= 2

    def body(p_ref, w_ref, m_ref, v_ref, g_ref, d_ref, nm_ref, nv_ref):
        g = p_ref[0]
        for k in range(1, n):
            g = g + p_ref[k]
        nm = ADAM_B1 * m_ref[...] + (1.0 - ADAM_B1) * g
        nv = ADAM_B2 * v_ref[...] + (1.0 - ADAM_B2) * (g * g)
        m_hat = nm / (1.0 - ADAM_B1 ** ADAM_STEP)
        v_hat = nv / (1.0 - ADAM_B2 ** ADAM_STEP)
        g_ref[...] = g
        d_ref[...] = -ADAM_LR * (m_hat / (jnp.sqrt(v_hat) + ADAM_EPS) + ADAM_WD * w_ref[...])
        nm_ref[...] = nm
        nv_ref[...] = nv

    blk = pl.BlockSpec((rb, cols), lambda i: (i, 0))
    return pl.pallas_call(
        body, name=name,
        grid=(rows // rb,),
        in_specs=[pl.BlockSpec((n, rb, cols), lambda i: (0, i, 0)), blk, blk, blk],
        out_specs=[blk] * 4,
        out_shape=[S((rows, cols), F32)] * 4,
        compiler_params=_params(1),
    )(parts, w, m, v)


def _axis_slice(ref, axis, block, size):
    idx = [slice(None)] * len(ref.shape)
    idx[axis] = pl.ds(pl.multiple_of(block * size, size), size)
    return ref.at[tuple(idx)]


def _axis_index(ref, axis, index):
    return ref.at[tuple([slice(None)] * axis + [index])]


COPY_PIECE_BYTES = 2 * 1024 * 1024


def _row_pieces(shape):
    rows = shape[-2]
    size = 4 * rows * shape[-1]
    for d in shape[:-2]:
        size *= d
    n = 1
    while size // n > COPY_PIECE_BYTES and rows % (2 * n * 8) == 0:
        n *= 2
    return [(k * (rows // n), rows // n) for k in range(n)]


def _rows(ref, r0, rows):
    idx = [slice(None)] * len(ref.shape)
    idx[-2] = pl.ds(r0, rows)
    return ref.at[tuple(idx)]


def _staged_local_copies(pairs, bufs, in_sems, out_sems):
    ins = [pltpu.make_async_copy(src, buf, in_sems.at[k]) for k, ((src, _), buf) in enumerate(zip(pairs, bufs))]
    for cp in ins:
        cp.start()
    for cp in ins:
        cp.wait()
    outs = [pltpu.make_async_copy(buf, dst, out_sems.at[k]) for k, ((_, dst), buf) in enumerate(zip(pairs, bufs))]
    for cp in outs:
        cp.start()
    return outs


def _run_copies(local, sends, recvs):
    for cp in local + sends:
        cp.start()
    for cp in recvs:
        cp.wait_recv()
    for cp in sends:
        cp.wait_send()
    for cp in local:
        cp.wait()


def _gather_weights(shards, axes):
    n = len(shards)
    fulls = [S(tuple(4 * d if k == ax else d for k, d in enumerate(s.shape)), s.dtype) for s, ax in zip(shards, axes)]

    def body(*refs):
        src, dst = refs[0:n], refs[n:2 * n]
        send_sems, recv_sems, pass_send_sems, pass_recv_sems, in_sems, out_sems = refs[2 * n:2 * n + 6]
        bufs = refs[2 * n + 6:3 * n + 6]
        x, y, c = lax.axis_index("x"), lax.axis_index("y"), lax.axis_index("c")
        chips = [(1 - x, y), (x, 1 - y), (1 - x, 1 - y)]

        def place(a, chip, layer=None):
            ref = _axis_slice(dst[a], axes[a], 2 * chip[0] + chip[1], src[a].shape[axes[a]])
            return ref if layer is None else ref.at[pl.ds(layer, 1)]

        def over_ici(k, chip_of_block, to):
            a, j, r0, rows = pieces[k]
            return pltpu.make_async_remote_copy(
                src_ref=_rows(src[a].at[pl.ds(c, 1)], r0, rows), dst_ref=_rows(place(a, chip_of_block, c), r0, rows),
                send_sem=send_sems.at[k], recv_sem=recv_sems.at[k], device_id=(to[0], to[1], c), device_id_type=MESH)

        def to_sibling(k, chip_of_block, layer):
            a, j, r0, rows = pieces[k]
            block = _rows(place(a, chip_of_block, layer), r0, rows)
            return pltpu.make_async_remote_copy(
                src_ref=block, dst_ref=block, send_sem=pass_send_sems.at[k], recv_sem=pass_recv_sems.at[k],
                device_id=(x, y, 1 - c), device_id_type=MESH)

        first = [over_ici(k, (x, y), chips[j]) for k, (a, j, r0, rows) in enumerate(pieces)]
        for cp in first:
            cp.start()
        local = _staged_local_copies([(src[a], place(a, (x, y))) for a in range(n)], bufs, in_sems, out_sems)
        passed = []
        for k, (a, j, r0, rows) in enumerate(pieces):
            over_ici(k, chips[j], chips[j]).wait_recv()
            passed.append(to_sibling(k, chips[j], c))
            passed[-1].start()
        for k, (a, j, r0, rows) in enumerate(pieces):
            to_sibling(k, chips[j], 1 - c).wait_recv()
        for cp in first + passed:
            cp.wait_send()
        for cp in local:
            cp.wait()

    per_array = [_row_pieces((1,) + shards[a].shape[1:]) for a in range(n)]
    pieces = [(a, j, r0, rows) for a in range(n) for r0, rows in per_array[a] for j in range(3)]
    any_spec = pl.BlockSpec(memory_space=pl.ANY)
    return pl.pallas_call(
        body, name="gather_weights",
        in_specs=[any_spec] * n, out_specs=[any_spec] * n, out_shape=fulls,
        scratch_shapes=[pltpu.SemaphoreType.DMA((len(pieces),))] * 4 + [pltpu.SemaphoreType.DMA((n,))] * 2
        + [pltpu.VMEM(s.shape, s.dtype) for s in shards],
        compiler_params=_params(0),
    )(*shards)


def _swap_layers(layers):
    n = len(layers)
    pieces = [(a, r0, rows) for a in range(n) for r0, rows in _row_pieces(layers[a][0].shape)]

    def body(*refs):
        src = [refs[2 * a:2 * a + 2] for a in range(n)]
        dst = refs[2 * n:3 * n]
        send_sems, recv_sems = refs[3 * n:3 * n + 2]
        x, y, c = lax.axis_index("x"), lax.axis_index("y"), lax.axis_index("c")

        def remote(k, layer):
            a, r0, rows = pieces[k]
            return pltpu.make_async_remote_copy(
                src_ref=_rows(src[a][layer], r0, rows), dst_ref=_rows(dst[a], r0, rows),
                send_sem=send_sems.at[k], recv_sem=recv_sems.at[k], device_id=(x, y, 1 - c), device_id_type=MESH)

        for layer in range(2):
            @pl.when(c == 1 - layer)
            def _(layer=layer):
                for k in range(len(pieces)):
                    remote(k, layer).start()

        for k in range(len(pieces)):
            remote(k, 0).wait_recv()
        for k in range(len(pieces)):
            remote(k, 0).wait_send()

    any_spec = pl.BlockSpec(memory_space=pl.ANY)
    return pl.pallas_call(
        body, name="swap_layers",
        in_specs=[any_spec] * (2 * n), out_specs=[any_spec] * n,
        out_shape=[S(layers[a][0].shape, layers[a][0].dtype) for a in range(n)],
        scratch_shapes=[pltpu.SemaphoreType.DMA((len(pieces),))] * 2,
    )(*[arr for pair in layers for arr in pair])


def _add_own_layer(name, mine, theirs, layer):
    rows, cols = theirs.shape
    rb = rows
    while rb * cols * 4 * 4 * 2 > 24 * 1024 * 1024 and rb % 32 == 0:
        rb //= 2

    def body(layer_ref, m0_ref, m1_ref, t_ref, o_ref):
        own = jnp.where(layer_ref[0] == 0, m0_ref[...], m1_ref[...])
        o_ref[0] = (own + t_ref[...]).astype(BF16)

    def own_spec(which):
        return pl.BlockSpec((rb, cols), lambda i, l: (jnp.where(l[0] == which, i, 0), 0))

    return pl.pallas_call(
        body, name=name,
        grid_spec=pltpu.PrefetchScalarGridSpec(
            num_scalar_prefetch=1, grid=(rows // rb,),
            in_specs=[own_spec(0), own_spec(1), pl.BlockSpec((rb, cols), lambda i, l: (i, 0))],
            out_specs=pl.BlockSpec((1, rb, cols), lambda i, l: (0, i, 0))),
        out_shape=S((1, rows, cols), BF16),
        compiler_params=_params(1),
    )(layer, *mine, theirs)


def _send_to_owners(chip_halves, shard_axes, packed):
    n = len(chip_halves)

    def owned_shape(a):
        return tuple(d // 4 if k == shard_axes[a] else d for k, d in enumerate(chip_halves[a].shape))

    flips = [(dx, dy, dc) for dx in (0, 1) for dy in (0, 1) for dc in (0, 1)][1:]

    def body(*refs):
        src, psrc, dst, pdst = refs[0:n], refs[n], refs[n + 1:2 * n + 1], refs[2 * n + 1]
        send_sems, recv_sems, local_sems, psend_sems, precv_sems = refs[2 * n + 2:2 * n + 7]
        x, y, c = lax.axis_index("x"), lax.axis_index("y"), lax.axis_index("c")
        my_chip = 2 * x + y
        chips = [(1 - x, y), (x, 1 - y), (1 - x, 1 - y)]

        def owned(a, chip):
            return _axis_slice(src[a], shard_axes[a], 2 * chip[0] + chip[1], src[a].shape[shard_axes[a]] // 4)

        def remote(a, j, to, from_chip):
            return pltpu.make_async_remote_copy(
                src_ref=owned(a, to), dst_ref=dst[a].at[from_chip], send_sem=send_sems.at[a, j],
                recv_sem=recv_sems.at[a, j], device_id=(to[0], to[1], c), device_id_type=MESH)

        def peer(f):
            return (1 - x if f[0] else x, 1 - y if f[1] else y, 1 - c if f[2] else c)

        def premote(j, to, from_dev):
            return pltpu.make_async_remote_copy(
                src_ref=psrc, dst_ref=pdst.at[from_dev], send_sem=psend_sems.at[j], recv_sem=precv_sems.at[j],
                device_id=to, device_id_type=MESH)

        local = [pltpu.make_async_copy(_rows(owned(a, (x, y)), r0, rows), _rows(dst[a].at[my_chip], r0, rows),
                                       local_sems.at[k]) for k, (a, r0, rows) in enumerate(own_pieces)]
        local.append(pltpu.make_async_copy(psrc, pdst.at[4 * x + 2 * y + c], local_sems.at[len(own_pieces)]))
        sends = [remote(a, j, chip, my_chip) for a in range(n) for j, chip in enumerate(chips)]
        sends += [premote(j, peer(f), 4 * x + 2 * y + c) for j, f in enumerate(flips)]
        recvs = [remote(a, j, (x, y), 2 * chip[0] + chip[1]) for a in range(n) for j, chip in enumerate(chips)]
        for j, f in enumerate(flips):
            px, py, pc = peer(f)
            recvs.append(premote(j, (x, y, c), 4 * px + 2 * py + pc))
        _run_copies(local, sends, recvs)

    own_pieces = [(a, r0, rows) for a in range(n) for r0, rows in _row_pieces(owned_shape(a))]
    any_spec = pl.BlockSpec(memory_space=pl.ANY)
    return pl.pallas_call(
        body, name="send_to_owners",
        in_specs=[any_spec] * (n + 1), out_specs=[any_spec] * (n + 1),
        out_shape=[S((4,) + owned_shape(a), chip_halves[a].dtype) for a in range(n)]
        + [S((N_DEV,) + packed.shape, packed.dtype)],
        scratch_shapes=[pltpu.SemaphoreType.DMA((n, 3)), pltpu.SemaphoreType.DMA((n, 3)),
                        pltpu.SemaphoreType.DMA((len(own_pieces) + 1,)), pltpu.SemaphoreType.DMA((7,)),
                        pltpu.SemaphoreType.DMA((7,))],
    )(*chip_halves, packed)


def _join_halves(reduced, half_axes):
    n = len(reduced)

    def joined_shape(a):
        s, ax = reduced[a].shape, half_axes[a]
        return s[:ax] + (2,) + s[ax:]

    pieces = [(a, r0, rows) for a in range(n) for r0, rows in _row_pieces(reduced[a].shape)]

    def body(*refs):
        src, dst = refs[0:n], refs[n:2 * n]
        send_sems, recv_sems, in_sems, out_sems = refs[2 * n:2 * n + 4]
        bufs = refs[2 * n + 4:3 * n + 4]
        x, y, c = lax.axis_index("x"), lax.axis_index("y"), lax.axis_index("c")

        def remote(k, h):
            a, r0, rows = pieces[k]
            return pltpu.make_async_remote_copy(
                src_ref=_rows(src[a], r0, rows), dst_ref=_rows(_axis_index(dst[a], half_axes[a], h), r0, rows),
                send_sem=send_sems.at[k], recv_sem=recv_sems.at[k], device_id=(x, y, 1 - c), device_id_type=MESH)

        sends = [remote(k, c) for k in range(len(pieces))]
        for cp in sends:
            cp.start()
        own = [(src[a], _axis_index(dst[a], half_axes[a], c)) for a in range(n)]
        local = _staged_local_copies(own, bufs, in_sems, out_sems)
        for k in range(len(pieces)):
            remote(k, 1 - c).wait_recv()
        for cp in sends:
            cp.wait_send()
        for cp in local:
            cp.wait()

    any_spec = pl.BlockSpec(memory_space=pl.ANY)
    return pl.pallas_call(
        body, name="join_halves",
        in_specs=[any_spec] * n, out_specs=[any_spec] * n,
        out_shape=[S(joined_shape(a), reduced[a].dtype) for a in range(n)],
        scratch_shapes=[pltpu.SemaphoreType.DMA((len(pieces),))] * 2 + [pltpu.SemaphoreType.DMA((n,))] * 2
        + [pltpu.VMEM(r.shape, r.dtype) for r in reduced],
        compiler_params=_params(0),
    )(*reduced)


def _sum_slots(name, parts):
    n = parts.shape[0]
    shape = parts.shape[1:]
    flat = parts.reshape((n, -1, shape[-1]))
    rows, cols = flat.shape[1:]
    rb = rows
    while rb * cols * 4 * (n + 1) * 2 > 24 * 1024 * 1024 and rb % 32 == 0:
        rb //= 2

    def body(p_ref, o_ref):
        acc = p_ref[0].astype(F32)
        for k in range(1, n):
            acc = acc + p_ref[k].astype(F32)
        o_ref[...] = acc

    out = pl.pallas_call(
        body, name=name,
        grid=(rows // rb,),
        in_specs=[pl.BlockSpec((n, rb, cols), lambda i: (0, i, 0))],
        out_specs=pl.BlockSpec((rb, cols), lambda i: (i, 0)),
        out_shape=S((rows, cols), F32),
        compiler_params=_params(1),
    )(flat)
    return out.reshape(shape)


SMALL_NAMES = ("norm_pre", "pool_w", "pool_b", "pool_scale", "conv_b", "conv_ln_g", "conv_ln_b", "norm_post")
BIG_NAMES = ("w_in", "w_pool_out", "conv_w", "w_conv_out", "w_attn_out", "w_o")
SHARD_AXIS = {"w_in": 2, "w_pool_out": 2, "conv_w": 2, "w_conv_out": 2, "w_attn_out": 2, "w_o": 1}
WEIGHT_ORDER = ("norm_pre", "w_in", "pool_w", "pool_b", "pool_scale", "w_pool_out", "conv_w", "conv_b",
                "conv_ln_g", "conv_ln_b", "w_conv_out", "w_attn_out", "w_o", "norm_post")


def _pack_small(parts):
    return jnp.concatenate([parts[n].reshape(-1, LANES) for n in SMALL_NAMES], axis=0)


def _unpack_small(packed, shapes):
    out, r0 = {}, 0
    for n in SMALL_NAMES:
        size = 1
        for d in shapes[n]:
            size *= d
        rows = size // LANES
        out[n] = packed[r0:r0 + rows].reshape(shapes[n])
        r0 += rows
    return out


def _layer_weights(full, small, l):
    row = lambda a: a[l][None, :]
    return (small["pool_w"][l].astype(BF16), small["pool_b"][l].reshape(1, UNIT), row(small["pool_scale"]),
            full["w_pool_out"][l], full["conv_w"][l], row(small["conv_b"]), row(small["conv_ln_g"]),
            row(small["conv_ln_b"]), full["w_conv_out"][l], full["w_attn_out"][l], full["w_o"][l],
            row(small["norm_post"]))


def _forward_backward(x, target, full, small):
    depth = full["w_in"].shape[0]
    acts = []
    for l in range(depth):
        g_pre = small["norm_pre"][l][None, :]
        proj = _inproj_fwd(x, g_pre, full["w_in"][l])
        o, bsum, kfirst = _attn_fwd(proj)
        weights = _layer_weights(full, small, l)
        x_next, cv = _mix_fwd(proj, o, x, weights)
        acts.append((x, proj, o, cv, bsum, kfirst, weights, g_pre))
        x = x_next
    loss_tile, dx = _loss_head(x, target)

    big = {n: [None] * depth for n in BIG_NAMES}
    sm = {n: [None] * depth for n in SMALL_NAMES}
    for l in reversed(range(depth)):
        x_in, proj, o, cv, bsum, kfirst, weights, g_pre = acts[l]
        dgates, dd, dcv, d_o, dwo, dwp, dwc, dwa, dpw, dvec = _mix_bwd(proj, o, cv, x_in, dx, weights)
        dpre, dcw = _halo_bwd(proj, dcv, dd, full["conv_w"][l])
        dq, dk, dv = _attn_bwd(proj, d_o, bsum, kfirst)
        dproj_parts = (dpre, dgates, dq, dk, dv)
        dx, dg_pre = _inproj_bwd_x(x_in, g_pre, full["w_in"][l], dproj_parts, dx)
        big["w_in"][l] = _inproj_bwd_w(x_in, g_pre, dproj_parts)
        big["w_pool_out"][l], big["w_conv_out"][l], big["w_attn_out"][l], big["w_o"][l] = dwp, dwc, dwa, dwo
        big["conv_w"][l] = dcw
        sm["norm_pre"][l] = dg_pre[0]
        sm["pool_w"][l] = dpw
        sm["norm_post"][l] = dvec[0]
        sm["pool_b"][l] = dvec[1, 0:UNIT].reshape(4, GROUP)
        sm["pool_scale"][l] = dvec[2, 0:UNIT]
        sm["conv_b"][l] = dvec[3, 0:UNIT]
        sm["conv_ln_g"][l] = dvec[4, 0:UNIT]
        sm["conv_ln_b"][l] = dvec[5, 0:UNIT]
    sm = {n: jnp.stack(v) for n, v in sm.items()}
    return loss_tile, dx, big, sm


def kernel(x, norm_pre, w_in, pool_w, pool_b, pool_scale, w_pool_out, conv_w, conv_b, conv_ln_g, conv_ln_b, w_conv_out, w_attn_out, w_o, norm_post, loss_target, m_norm_pre, m_w_in, m_pool_w, m_pool_b, m_pool_scale, m_w_pool_out, m_conv_w, m_conv_b, m_conv_ln_g, m_conv_ln_b, m_w_conv_out, m_w_attn_out, m_w_o, m_norm_post, v_norm_pre, v_w_in, v_pool_w, v_pool_b, v_pool_scale, v_w_pool_out, v_conv_w, v_conv_b, v_conv_ln_g, v_conv_ln_b, v_w_conv_out, v_w_attn_out, v_w_o, v_norm_post):
    w = dict(norm_pre=norm_pre, w_in=w_in, pool_w=pool_w, pool_b=pool_b, pool_scale=pool_scale,
             w_pool_out=w_pool_out, conv_w=conv_w, conv_b=conv_b, conv_ln_g=conv_ln_g, conv_ln_b=conv_ln_b,
             w_conv_out=w_conv_out, w_attn_out=w_attn_out, w_o=w_o, norm_post=norm_post)
    m = dict(norm_pre=m_norm_pre, w_in=m_w_in, pool_w=m_pool_w, pool_b=m_pool_b, pool_scale=m_pool_scale,
             w_pool_out=m_w_pool_out, conv_w=m_conv_w, conv_b=m_conv_b, conv_ln_g=m_conv_ln_g,
             conv_ln_b=m_conv_ln_b, w_conv_out=m_w_conv_out, w_attn_out=m_w_attn_out, w_o=m_w_o,
             norm_post=m_norm_post)
    v = dict(norm_pre=v_norm_pre, w_in=v_w_in, pool_w=v_pool_w, pool_b=v_pool_b, pool_scale=v_pool_scale,
             w_pool_out=v_w_pool_out, conv_w=v_conv_w, conv_b=v_conv_b, conv_ln_g=v_conv_ln_g,
             conv_ln_b=v_conv_ln_b, w_conv_out=v_w_conv_out, w_attn_out=v_w_attn_out, w_o=v_w_o,
             norm_post=v_norm_post)
    pad_taps = lambda a: jnp.pad(a, ((0, 0), (0, CONV_K_PAD - CONV_K), (0, 0)))

    shards = [pad_taps(w[n]) if n == "conv_w" else w[n].astype(BF16) for n in BIG_NAMES]
    full = dict(zip(BIG_NAMES, _gather_weights(shards, [SHARD_AXIS[n] for n in BIG_NAMES])))
    small = {n: w[n] for n in SMALL_NAMES}

    loss_tile, dx, big, sm = _forward_backward(x[0], loss_target[0], full, small)

    packed = jnp.concatenate([_pack_small(sm), loss_tile], axis=0)
    layer_axis = [0] * len(BIG_NAMES)
    theirs = _swap_layers([big[n] for n in BIG_NAMES])
    my_layer = lax.axis_index("c").astype(jnp.int32).reshape(1)
    chip_sums = [_add_own_layer("sum_cores_" + n, big[n], t, my_layer) for n, t in zip(BIG_NAMES, theirs)]
    exchanged = _send_to_owners(chip_sums, [SHARD_AXIS[n] for n in BIG_NAMES], packed)
    reduced = [_sum_slots("sum_chips_" + n, p) for n, p in zip(BIG_NAMES, exchanged[:-1])]
    joined = _join_halves(reduced, layer_axis)
    out = {}
    for n, g in zip(BIG_NAMES, joined):
        wn, mn, vn = (pad_taps(a[n]) if n == "conv_w" else a[n] for a in (w, m, v))
        shape = wn.shape
        flat = lambda a: a.reshape((-1, shape[-1]))
        res = _adamw("adamw_" + n, g.reshape((1, -1, shape[-1])), flat(wn), flat(mn), flat(vn))
        res = [r.reshape(shape) for r in res]
        out[n] = [r[:, :CONV_K] for r in res] if n == "conv_w" else res
    parts = exchanged[-1]
    n_small_rows = parts.shape[1] - 8
    zeros_tile = jnp.zeros((8, LANES), F32)
    packs = [jnp.concatenate([_pack_small(a), zeros_tile], axis=0) for a in (w, m, v)]
    res = _adamw("adamw_small", parts, *packs)
    loss = res[0][n_small_rows, 0]
    shapes = {n: w[n].shape for n in SMALL_NAMES}
    unpacked = [_unpack_small(r[:n_small_rows], shapes) for r in res]
    for n in SMALL_NAMES:
        out[n] = [u[n] for u in unpacked]

    grads = [out[n][0] for n in WEIGHT_ORDER]
    deltas = [out[n][1] for n in WEIGHT_ORDER]
    new_m = [out[n][2] for n in WEIGHT_ORDER]
    new_v = [out[n][3] for n in WEIGHT_ORDER]
    return (loss, dx[None], *grads, *deltas, *new_m, *new_v)
```

```python
import jax
import jax.numpy as jnp
from jax import lax
from jax.experimental import pallas as pl
from jax.experimental.pallas import tpu as pltpu

F32 = jnp.float32
BF16 = jnp.bfloat16

D_MODEL = 1024
UNIT = 512
N_UNITS = 15
IN_WIDTH = UNIT * N_UNITS
N_HEADS = 8
HEAD_DIM = 64
HEADS_PER_BLOCK = 2
N_HEAD_BLOCKS = N_HEADS // HEADS_PER_BLOCK
LANES = 128
CONV_K = 31
CONV_K_PAD = 32
HALO = 32
POOL_WINDOWS = (2, 4, 8, 16)
GROUP = 128
N_BRANCH = 3
RMS_EPS = 1e-6
LN_EPS = 1e-5
ATTN_SCALE = 0.125
EXP_ZERO_BELOW = -104.0

ADAM_LR = 0.001
ADAM_B1 = 0.9
ADAM_B2 = 0.999
ADAM_EPS = 1e-08
ADAM_WD = 0.01
ADAM_STEP = 10

U_P, U_GP, U_CA, U_CB, U_GC, U_Q, U_K, U_V, U_GA, U_GM = 0, 1, 2, 3, 4, 5, 6, 7, 8, 9

V7X_VMEM_LIMIT = 62 * 1024 * 1024
N_DEV = 8
MESH = pl.DeviceIdType.MESH

S = jax.ShapeDtypeStruct


def _params(n_grid):
    return pltpu.CompilerParams(dimension_semantics=("arbitrary",) * n_grid, vmem_limit_bytes=V7X_VMEM_LIMIT)


def _sigmoid(x):
    return 0.5 * jnp.tanh(0.5 * x) + 0.5


def _dsilu(x, s):
    return s * (1.0 + x * (1.0 - s))


def _dot(a, b):
    return jnp.dot(a, b, preferred_element_type=F32)


def _dot_nt(a, b):
    return lax.dot_general(a, b, (((1,), (1,)), ((), ())), preferred_element_type=F32)


def _dot_tn(a, b):
    return lax.dot_general(a, b, (((0,), (0,)), ((), ())), preferred_element_type=F32)


def _split_bf16(x):
    hi = x.astype(BF16)
    lo = (x - hi.astype(F32)).astype(BF16)
    return hi, lo


def _const_spec(shape):
    n = len(shape)
    return pl.BlockSpec(shape, lambda *_: (0,) * n, pipeline_mode=pl.Buffered(1))


def _inproj_fwd(x, g_pre, w_in):
    t = x.shape[0]
    tb = min(1024, t)
    ug = 3
    nb = ug * UNIT

    def body(x_ref, g_ref, w_ref, o_ref, h_ref):
        @pl.when(pl.program_id(1) == 0)
        def _():
            xf = x_ref[...]
            r = lax.rsqrt(jnp.mean(xf * xf, axis=-1, keepdims=True) + RMS_EPS)
            h_ref[...] = (xf * r * g_ref[...]).astype(BF16)

        acc = _dot(h_ref[...], w_ref[...])
        for u in range(ug):
            o_ref[u] = acc[:, u * UNIT:(u + 1) * UNIT]

    return pl.pallas_call(
        body, name="inproj_fwd",
        grid=(t // tb, N_UNITS // ug),
        in_specs=[pl.BlockSpec((tb, D_MODEL), lambda i, j: (i, 0)),
                  pl.BlockSpec((1, D_MODEL), lambda i, j: (0, 0)),
                  pl.BlockSpec((D_MODEL, nb), lambda i, j: (0, j))],
        out_specs=pl.BlockSpec((ug, tb, UNIT), lambda i, j: (j, i, 0)),
        out_shape=S((N_UNITS, t, UNIT), F32),
        scratch_shapes=[pltpu.VMEM((tb, D_MODEL), BF16)],
        compiler_params=_params(2),
    )(x, g_pre, w_in)


def _dproj_specs(tb):
    tok = pl.BlockSpec((tb, UNIT), lambda i: (i, 0))
    return [pl.BlockSpec((3, tb, UNIT), lambda i: (0, i, 0)), pl.BlockSpec((9, tb, UNIT), lambda i: (0, i, 0)),
            tok, tok, tok]


def _dproj_unit(u, dpre_ref, dgates_ref, dq_ref, dk_ref, dv_ref):
    pre = {U_P: 0, U_CA: 1, U_CB: 2}
    gate = {U_GP: 0, U_GC: 1, U_GA: 2}
    if u in pre:
        val = dpre_ref[pre[u]]
    elif u in gate:
        val = dgates_ref[gate[u]]
    elif u >= U_GM:
        val = dgates_ref[3 + u - U_GM]
    else:
        val = {U_Q: dq_ref, U_K: dk_ref, U_V: dv_ref}[u][...]
    return val.astype(BF16)


def _inproj_bwd_x(x, g_pre, w_in, dproj_parts, dxn):
    t = x.shape[0]
    tb = min(256, t)

    def body(x_ref, g_ref, w_ref, dpre_ref, dgates_ref, dq_ref, dk_ref, dv_ref, dxn_ref, dx_ref, dg_ref):
        @pl.when(pl.program_id(0) == 0)
        def _():
            dg_ref[...] = jnp.zeros_like(dg_ref)

        dh = jnp.zeros((tb, D_MODEL), F32)
        for u in range(N_UNITS):
            dh = dh + _dot_nt(_dproj_unit(u, dpre_ref, dgates_ref, dq_ref, dk_ref, dv_ref),
                              w_ref[:, u * UNIT:(u + 1) * UNIT])
        xf = x_ref[...]
        r = lax.rsqrt(jnp.mean(xf * xf, axis=-1, keepdims=True) + RMS_EPS)
        xhat = xf * r
        dg_ref[...] += jnp.sum(dh * xhat, axis=0, keepdims=True)
        dxhat = dh * g_ref[...]
        dx_ref[...] = dxn_ref[...] + r * (dxhat - xhat * jnp.mean(dxhat * xhat, axis=-1, keepdims=True))

    tokd = pl.BlockSpec((tb, D_MODEL), lambda i: (i, 0))
    return pl.pallas_call(
        body, name="inproj_bwd_x",
        grid=(t // tb,),
        in_specs=[tokd, _const_spec((1, D_MODEL)), _const_spec((D_MODEL, IN_WIDTH))] + _dproj_specs(tb) + [tokd],
        out_specs=[tokd, pl.BlockSpec((1, D_MODEL), lambda i: (0, 0))],
        out_shape=[S((t, D_MODEL), F32), S((1, D_MODEL), F32)],
        compiler_params=_params(1),
    )(x, g_pre, w_in, *dproj_parts, dxn)


def _inproj_bwd_w(x, g_pre, dproj_parts):
    t = x.shape[0]
    tb = min(256, t)
    nt = t // tb

    def body(x_ref, g_ref, dpre_ref, dgates_ref, dq_ref, dk_ref, dv_ref, dw_hbm, acc_ref):
        i = pl.program_id(0)
        xf = x_ref[...]
        r = lax.rsqrt(jnp.mean(xf * xf, axis=-1, keepdims=True) + RMS_EPS)
        ht = (xf * r * g_ref[...]).T.astype(BF16)

        @pl.when(i == 0)
        def _():
            acc_ref[...] = jnp.zeros_like(acc_ref)

        for u in range(N_UNITS):
            acc_ref[:, u * UNIT:(u + 1) * UNIT] += _dot(
                ht, _dproj_unit(u, dpre_ref, dgates_ref, dq_ref, dk_ref, dv_ref))

        @pl.when(i == nt - 1)
        def _():
            pltpu.sync_copy(acc_ref, dw_hbm)

    return pl.pallas_call(
        body, name="inproj_bwd_w",
        grid=(nt,),
        in_specs=[pl.BlockSpec((tb, D_MODEL), lambda i: (i, 0)), _const_spec((1, D_MODEL))] + _dproj_specs(tb),
        out_specs=pl.BlockSpec(memory_space=pl.ANY),
        out_shape=S((D_MODEL, IN_WIDTH), F32),
        scratch_shapes=[pltpu.VMEM((D_MODEL, IN_WIDTH), F32)],
        compiler_params=_params(1),
    )(x, g_pre, *dproj_parts)


def _attn_tile_size(t):
    return min(256, t)


def _softplus_parts(z):
    sp = jnp.maximum(z, 0.0) + jnp.log(1.0 + jnp.exp(-jnp.abs(z)))
    return -sp, z - sp


def _attn_fwd(proj):
    t = proj.shape[1]
    tq = _attn_tile_size(t)

    def body(q_ref, k_ref, v_ref, o_ref, bs_ref, kf_ref):
        i = pl.program_id(1)
        row = lax.broadcasted_iota(jnp.int32, (tq, tq), 0)
        col = lax.broadcasted_iota(jnp.int32, (tq, tq), 1)
        tri = (row > col).astype(BF16)
        tri2 = jnp.concatenate([tri, tri], axis=0)
        causal = jnp.concatenate([col < row] * HEADS_PER_BLOCK, axis=0)
        lane = lax.broadcasted_iota(jnp.int32, (1, LANES), 1)
        q_all = q_ref[0] * ATTN_SCALE
        in_heads = [(lane >= hh * HEAD_DIM) & (lane < (hh + 1) * HEAD_DIM) for hh in range(HEADS_PER_BLOCK)]
        q_stack = jnp.concatenate([jnp.where(in_head, q_all, 0.0) for in_head in in_heads], axis=0).astype(BF16)
        rows = HEADS_PER_BLOCK * tq

        def tiles(kj, carry, acc, masked):
            ks = pl.multiple_of(kj * tq, tq)
            kh = k_ref[0, pl.ds(ks, tq), :].astype(BF16)
            vh = v_ref[0, pl.ds(ks, tq), :].astype(BF16)
            z = _dot_nt(q_stack, kh)
            lm, lb = _softplus_parts(z)
            if masked:
                lm = jnp.where(causal, lm, 0.0)
            between = _dot(jnp.concatenate(_split_bf16(lm), axis=1), tri2) + carry
            w = jnp.exp(lb + between)
            if masked:
                w = jnp.where(causal, w, 0.0)
            return carry + jnp.sum(lm, axis=1, keepdims=True), acc + _dot(w.astype(BF16), vh)

        carry, acc = tiles(i, jnp.zeros((rows, 1), F32), jnp.zeros((rows, LANES), F32), True)

        def more(c):
            return jnp.logical_and(c[0] >= 0, jnp.max(c[1]) >= EXP_ZERO_BELOW)

        kj_end, carry, acc = lax.while_loop(
            more, lambda c: (c[0] - 1,) + tiles(c[0], c[1], c[2], False), (i - 1, carry, acc))
        o_out, b_out = acc[0:tq], jnp.broadcast_to(carry[0:tq], (tq, LANES))
        for hh in range(1, HEADS_PER_BLOCK):
            o_out = jnp.where(in_heads[hh], acc[hh * tq:(hh + 1) * tq], o_out)
            b_out = jnp.where(in_heads[hh], carry[hh * tq:(hh + 1) * tq], b_out)
        o_ref[...] = o_out
        bs_ref[0] = b_out
        kf_ref[0, 0] = jnp.zeros((8, LANES), F32) + (kj_end + 1).astype(F32)

    nq = t // tq
    return pl.pallas_call(
        body, name="attn_fwd",
        grid=(N_HEAD_BLOCKS, nq),
        in_specs=[pl.BlockSpec((1, tq, LANES), lambda p, i: (U_Q, i, p)),
                  pl.BlockSpec((1, t, LANES), lambda p, i: (U_K, 0, p)),
                  pl.BlockSpec((1, t, LANES), lambda p, i: (U_V, 0, p))],
        out_specs=[pl.BlockSpec((tq, LANES), lambda p, i: (i, p)),
                   pl.BlockSpec((1, tq, LANES), lambda p, i: (p, i, 0)),
                   pl.BlockSpec((1, 1, 8, LANES), lambda p, i: (p, i, 0, 0))],
        out_shape=[S((t, UNIT), F32), S((N_HEAD_BLOCKS, t, LANES), F32), S((N_HEAD_BLOCKS, nq, 8, LANES), F32)],
        compiler_params=_params(2),
    )(proj, proj, proj)


def _attn_bwd(proj, d_o, bsum, kfirst):
    t = proj.shape[1]
    tq = _attn_tile_size(t)
    nq = t // tq

    def body(q_ref, k_ref, v_ref, do_ref, bs_ref, kf_ref, dq_ref, dk_ref, dv_ref, dkt_ref, dvt_ref):
        i = pl.program_id(1)

        @pl.when(i == 0)
        def _():
            dkt_ref[...] = jnp.zeros_like(dkt_ref)
            dvt_ref[...] = jnp.zeros_like(dvt_ref)

        row = lax.broadcasted_iota(jnp.int32, (tq, tq), 0)
        col = lax.broadcasted_iota(jnp.int32, (tq, tq), 1)
        upto = (row <= col).astype(BF16)
        before = (row < col).astype(BF16)
        upto2 = jnp.concatenate([upto, upto], axis=0)
        before2 = jnp.concatenate([before, before], axis=0)
        causal = jnp.concatenate([col < row] * HEADS_PER_BLOCK, axis=0)
        lane = lax.broadcasted_iota(jnp.int32, (1, LANES), 1)
        q_all = q_ref[0] * ATTN_SCALE
        do_all = do_ref[...].astype(F32)
        bs_all = bs_ref[0]
        k_first = jnp.clip(jnp.max(kf_ref[0, 0]).astype(jnp.int32), 0, i)

        in_heads = [(lane >= hh * HEAD_DIM) & (lane < (hh + 1) * HEAD_DIM) for hh in range(HEADS_PER_BLOCK)]
        q_f32 = jnp.concatenate([jnp.where(in_head, q_all, 0.0) for in_head in in_heads], axis=0)
        do_f32 = jnp.concatenate([jnp.where(in_head, do_all, 0.0) for in_head in in_heads], axis=0)
        q_stack, do_stack = q_f32.astype(BF16), do_f32.astype(BF16)
        q_t, do_t = q_f32.T.astype(BF16), do_f32.T.astype(BF16)
        btot = jnp.concatenate([jnp.max(jnp.where(in_head, bs_all, -jnp.inf), axis=1, keepdims=True)
                                for in_head in in_heads], axis=0)
        rows = HEADS_PER_BLOCK * tq

        def tiles(kj, c_b, c_p, dq, masked):
            ks = pl.multiple_of(kj * tq, tq)
            kh = k_ref[0, pl.ds(ks, tq), :].astype(BF16)
            vh = v_ref[0, pl.ds(ks, tq), :].astype(BF16)
            z = _dot_nt(q_stack, kh)
            lm, lb = _softplus_parts(z)
            if masked:
                lm = jnp.where(causal, lm, 0.0)
            between = btot - (c_b + _dot(jnp.concatenate(_split_bf16(lm), axis=1), upto2))
            w = jnp.exp(lb + between)
            if masked:
                w = jnp.where(causal, w, 0.0)
            e = w * _dot_nt(do_stack, vh)
            p_sum = c_p + _dot(jnp.concatenate(_split_bf16(e), axis=1), before2)
            beta = jnp.exp(lb)
            dz = e * (1.0 - beta) - p_sum * beta
            if masked:
                dz = jnp.where(causal, dz, 0.0)
            dzb = dz.astype(BF16)
            dkt_ref[kj] += _dot(q_t, dzb)
            dvt_ref[kj] += _dot(do_t, w.astype(BF16))
            return (c_b + jnp.sum(lm, axis=1, keepdims=True), c_p + jnp.sum(e, axis=1, keepdims=True),
                    dq + _dot(dzb, kh))

        zero = (jnp.zeros((rows, 1), F32), jnp.zeros((rows, 1), F32), jnp.zeros((rows, LANES), F32))
        state = lax.fori_loop(k_first, i, lambda kj, st: tiles(kj, *st, False), zero)
        dq = tiles(i, *state, True)[2]
        dq_out = dq[0:tq]
        for hh in range(1, HEADS_PER_BLOCK):
            dq_out = jnp.where(in_heads[hh], dq[hh * tq:(hh + 1) * tq], dq_out)
        dq_ref[...] = (dq_out * ATTN_SCALE).astype(BF16)

        @pl.when(i == nq - 1)
        def _():
            for kj in range(nq):
                dk_ref[kj * tq:(kj + 1) * tq, :] = dkt_ref[kj].T.astype(BF16)
                dv_ref[kj * tq:(kj + 1) * tq, :] = dvt_ref[kj].T.astype(BF16)

    return pl.pallas_call(
        body, name="attn_bwd",
        grid=(N_HEAD_BLOCKS, nq),
        in_specs=[pl.BlockSpec((1, tq, LANES), lambda p, i: (U_Q, i, p)),
                  pl.BlockSpec((1, t, LANES), lambda p, i: (U_K, 0, p)),
                  pl.BlockSpec((1, t, LANES), lambda p, i: (U_V, 0, p)),
                  pl.BlockSpec((tq, LANES), lambda p, i: (i, p)),
                  pl.BlockSpec((1, tq, LANES), lambda p, i: (p, i, 0)),
                  pl.BlockSpec((1, 1, 8, LANES), lambda p, i: (p, i, 0, 0))],
        out_specs=[pl.BlockSpec((tq, LANES), lambda p, i: (i, p)),
                   pl.BlockSpec((t, LANES), lambda p, i: (0, p)),
                   pl.BlockSpec((t, LANES), lambda p, i: (0, p))],
        out_shape=[S((t, UNIT), BF16)] * 3,
        scratch_shapes=[pltpu.VMEM((nq, LANES, tq), F32), pltpu.VMEM((nq, LANES, tq), F32)],
        compiler_params=_params(2),
    )(proj, proj, proj, d_o, bsum, kfirst)


N_MIX_VIEWS = 11
N_MIX_VIEWS_NO_CONV = 7


def _mix_in_specs(tb, conv_inputs=True):
    hb = tb // HALO

    def unit(u):
        return pl.BlockSpec((1, tb, UNIT), lambda i: (u, i, 0))

    def halo(u):
        return pl.BlockSpec((1, HALO, UNIT), lambda i: (u, jnp.maximum(i * hb - 1, 0), 0))

    conv = [unit(U_CA), halo(U_CA), unit(U_CB), halo(U_CB)] if conv_inputs else []
    return [unit(U_P), halo(U_P), unit(U_GP)] + conv + [
        unit(U_GC), unit(U_GA), pl.BlockSpec((3, tb, UNIT), lambda i: (3, i, 0)),
        pl.BlockSpec((3, tb, UNIT), lambda i: (4, i, 0))]


def _mix_weight_specs():
    return [_const_spec((4, GROUP, GROUP)), _const_spec((1, UNIT)), _const_spec((1, UNIT)),
            _const_spec((UNIT, D_MODEL)), _const_spec((CONV_K_PAD, UNIT)), _const_spec((1, UNIT)),
            _const_spec((1, UNIT)), _const_spec((1, UNIT)), _const_spec((UNIT, D_MODEL)),
            _const_spec((UNIT, D_MODEL)), _const_spec((D_MODEL, D_MODEL)), _const_spec((1, D_MODEL))]


def _shifted_reader(buf, rbuf, tb):
    if rbuf is None:
        return lambda s: buf[s:s + tb, :]
    length = tb + HALO - 8
    for b in range(1, 8):
        rbuf[b, :, :] = buf[b:b + length, :]

    def read(s):
        a, b = divmod(s, 8)
        return buf[s:s + tb, :] if b == 0 else rbuf[b, 8 * a:8 * a + tb, :]

    return read


def _mix_forward(i, tb, proj_refs, o_ref, w_refs, pbuf, ubuf=None, rbuf=None, cv_ref=None):
    if cv_ref is None:
        p_ref, ph_ref, gp_ref, ca_ref, cah_ref, cb_ref, cbh_ref, gc_ref, ga_ref, gm0_ref, gm1_ref = proj_refs
    else:
        p_ref, ph_ref, gp_ref, gc_ref, ga_ref, gm0_ref, gm1_ref = proj_refs
    (poolw_ref, poolb_ref, pscale_ref, wpo_ref, convw_ref, convb_ref, lng_ref, lnb_ref,
     wco_ref, wao_ref, wo_ref, gpost_ref) = w_refs
    first = i == 0
    r = {}

    pbuf[0:HALO, :] = jnp.where(first, 0.0, ph_ref[0])
    pbuf[HALO:HALO + tb, :] = p_ref[0]
    tpos = i * tb + lax.broadcasted_iota(jnp.int32, (tb, 1), 0)
    d_parts, y_parts = [], []
    for g, win in enumerate(POOL_WINDOWS):
        cs = slice(g * GROUP, (g + 1) * GROUP)
        cur = pbuf[HALO:HALO + tb, cs]
        s = cur
        for j in range(1, win):
            s = s + pbuf[HALO - j:HALO - j + tb, cs]
        cnt = jnp.minimum(tpos + 1, win).astype(F32)
        d_g = s / cnt - cur
        d_parts.append(d_g)
        y_parts.append(_dot(d_g.astype(BF16), poolw_ref[g]))
    r["d"] = d_parts
    y = jnp.concatenate(y_parts, axis=1) + poolb_ref[...]
    r["y"] = y
    mp = y * pscale_ref[...]
    gp = gp_ref[0]
    sgp = _sigmoid(gp)
    r["mp"], r["gp"], r["sgp"] = mp, gp, sgp
    ua = mp * (gp * sgp)

    if cv_ref is None:
        ah, bh = cah_ref[0], cbh_ref[0]
        ubuf[0:HALO, :] = jnp.where(first, 0.0, ah * _sigmoid(bh))
        ubuf[HALO:HALO + tb, :] = ca_ref[0] * _sigmoid(cb_ref[0])
        cv = jnp.zeros((tb, UNIT), F32) + convb_ref[...]
        off = HALO - (CONV_K - 1)
        u_at = _shifted_reader(ubuf, rbuf, tb)
        for k in range(CONV_K):
            cv = cv + convw_ref[k:k + 1, :] * u_at(off + k)
    else:
        cv = cv_ref[...]
    r["cv"] = cv
    mu = jnp.mean(cv, axis=-1, keepdims=True)
    cc = cv - mu
    rs = lax.rsqrt(jnp.mean(cc * cc, axis=-1, keepdims=True) + LN_EPS)
    nrm = cc * rs
    ln = nrm * lng_ref[...] + lnb_ref[...]
    sln = _sigmoid(ln)
    sc = ln * sln
    gc = gc_ref[0]
    sgc = _sigmoid(gc)
    r["rs"], r["nrm"], r["ln"], r["sln"], r["sc"], r["gc"], r["sgc"] = rs, nrm, ln, sln, sc, gc, sgc
    ub = sc * (gc * sgc)

    o = o_ref[...]
    ga = ga_ref[0]
    sga = _sigmoid(ga)
    r["o"], r["ga"], r["sga"] = o, ga, sga
    uc = o * (ga * sga)

    r["ua"], r["ub"], r["uc"] = ua.astype(BF16), ub.astype(BF16), uc.astype(BF16)
    ya = _dot(r["ua"], wpo_ref[...])
    yb = _dot(r["ub"], wco_ref[...])
    yc = _dot(r["uc"], wao_ref[...])
    g0 = _sigmoid(jnp.concatenate([gm0_ref[0], gm0_ref[1]], axis=1))
    g1 = _sigmoid(jnp.concatenate([gm0_ref[2], gm1_ref[0]], axis=1))
    g2 = _sigmoid(jnp.concatenate([gm1_ref[1], gm1_ref[2]], axis=1))
    r["ya"], r["yb"], r["yc"], r["g0"], r["g1"], r["g2"] = ya, yb, yc, g0, g1, g2
    m = (g0 * ya + g1 * yb + g2 * yc).astype(BF16)
    r["m"] = m
    out = _dot(m, wo_ref[...])
    r2 = lax.rsqrt(jnp.mean(out * out, axis=-1, keepdims=True) + RMS_EPS)
    r["n2"], r["r2"] = out * r2, r2
    return r


def _mix_fwd(proj, o, x, weights):
    t = x.shape[0]
    tb = min(256, t)
    nv = N_MIX_VIEWS

    def body(*refs):
        proj_refs, o_ref, x_ref = refs[0:nv], refs[nv], refs[nv + 1]
        w_refs = refs[nv + 2:nv + 14]
        xn_ref, cv_ref, pbuf, ubuf, rbuf = refs[nv + 14:nv + 19]
        r = _mix_forward(pl.program_id(0), tb, proj_refs, o_ref, w_refs, pbuf, ubuf, rbuf)
        xn_ref[...] = x_ref[...] + r["n2"] * w_refs[11][...]
        cv_ref[...] = r["cv"]

    return pl.pallas_call(
        body, name="mix_fwd",
        grid=(t // tb,),
        in_specs=_mix_in_specs(tb) + [pl.BlockSpec((tb, UNIT), lambda i: (i, 0)),
                                      pl.BlockSpec((tb, D_MODEL), lambda i: (i, 0))] + _mix_weight_specs(),
        out_specs=[pl.BlockSpec((tb, D_MODEL), lambda i: (i, 0)), pl.BlockSpec((tb, UNIT), lambda i: (i, 0))],
        out_shape=[S((t, D_MODEL), F32), S((t, UNIT), F32)],
        scratch_shapes=[pltpu.VMEM((HALO + tb, UNIT), F32), pltpu.VMEM((HALO + tb, UNIT), F32),
                        pltpu.VMEM((8, HALO + tb - 8, UNIT), F32)],
        compiler_params=_params(1),
    )(*([proj] * nv), o, x, *weights)


def _mix_bwd(proj, o, cv, x, dxn, weights):
    t = x.shape[0]
    tb = min(256, t)
    nt = t // tb

    nv = N_MIX_VIEWS_NO_CONV

    def body(*refs):
        proj_refs, o_ref, cv_ref, x_ref, dxn_ref = refs[0:nv], refs[nv], refs[nv + 1], refs[nv + 2], refs[nv + 3]
        w_refs = refs[nv + 4:nv + 16]
        (dg_ref, dd_ref, dcv_ref, do_ref, dwo_hbm, dwp_hbm, dwc_hbm, dwa_hbm, dpw_ref,
         dvec_ref) = refs[nv + 16:nv + 26]
        pbuf, dwo_ref, dwp_ref, dwc_ref, dwa_ref = refs[nv + 26:nv + 31]
        (poolw_ref, _, pscale_ref, wpo_ref, _, _, lng_ref, _, wco_ref, wao_ref, wo_ref, gpost_ref) = w_refs
        i = pl.program_id(0)

        @pl.when(i == 0)
        def _():
            for ref in (dwo_ref, dwp_ref, dwc_ref, dwa_ref, dpw_ref, dvec_ref):
                ref[...] = jnp.zeros_like(ref)

        r = _mix_forward(i, tb, proj_refs, o_ref, w_refs, pbuf, cv_ref=cv_ref)

        def colsum(v):
            return jnp.sum(v, axis=0, keepdims=True)

        dxn = dxn_ref[...]
        n2 = r["n2"]
        dvec_ref[0:1, :] += colsum(dxn * n2)
        dn2 = dxn * gpost_ref[...]
        dout = (r["r2"] * (dn2 - n2 * jnp.mean(dn2 * n2, axis=-1, keepdims=True))).astype(BF16)
        dm = _dot_nt(dout, wo_ref[...])
        dwo_ref[...] += _dot_tn(r["m"], dout)

        g0, g1, g2 = r["g0"], r["g1"], r["g2"]
        dgm = [dm * r["ya"] * g0 * (1.0 - g0), dm * r["yb"] * g1 * (1.0 - g1), dm * r["yc"] * g2 * (1.0 - g2)]
        for bidx in range(N_BRANCH):
            dg_ref[3 + 2 * bidx] = dgm[bidx][:, 0:UNIT].astype(BF16)
            dg_ref[4 + 2 * bidx] = dgm[bidx][:, UNIT:2 * UNIT].astype(BF16)
        dya = (dm * g0).astype(BF16)
        dyb = (dm * g1).astype(BF16)
        dyc = (dm * g2).astype(BF16)
        dua = _dot_nt(dya, wpo_ref[...])
        dub = _dot_nt(dyb, wco_ref[...])
        duc = _dot_nt(dyc, wao_ref[...])
        dwp_ref[...] += _dot_tn(r["ua"], dya)
        dwc_ref[...] += _dot_tn(r["ub"], dyb)
        dwa_ref[...] += _dot_tn(r["uc"], dyc)

        gp, sgp = r["gp"], r["sgp"]
        dmp = dua * (gp * sgp)
        dg_ref[0] = (dua * r["mp"] * _dsilu(gp, sgp)).astype(BF16)
        dvec_ref[2:3, 0:UNIT] += colsum(dmp * r["y"])
        dy = dmp * pscale_ref[...]
        dvec_ref[1:2, 0:UNIT] += colsum(dy)
        dd_parts = []
        for g in range(len(POOL_WINDOWS)):
            dy_g = dy[:, g * GROUP:(g + 1) * GROUP].astype(BF16)
            dd_parts.append(_dot_nt(dy_g, poolw_ref[g]))
            dpw_ref[g] += _dot_tn(r["d"][g].astype(BF16), dy_g)
        dd_ref[...] = jnp.concatenate(dd_parts, axis=1)

        gc, sgc = r["gc"], r["sgc"]
        dsc = dub * (gc * sgc)
        dg_ref[1] = (dub * r["sc"] * _dsilu(gc, sgc)).astype(BF16)
        dln = dsc * _dsilu(r["ln"], r["sln"])
        nrm = r["nrm"]
        dvec_ref[4:5, 0:UNIT] += colsum(dln * nrm)
        dvec_ref[5:6, 0:UNIT] += colsum(dln)
        dnrm = dln * lng_ref[...]
        dcv = r["rs"] * (dnrm - jnp.mean(dnrm, axis=-1, keepdims=True)
                         - nrm * jnp.mean(dnrm * nrm, axis=-1, keepdims=True))
        dvec_ref[3:4, 0:UNIT] += colsum(dcv)
        dcv_ref[...] = dcv

        ga, sga = r["ga"], r["sga"]
        do_ref[...] = (duc * (ga * sga)).astype(BF16)
        dg_ref[2] = (duc * r["o"] * _dsilu(ga, sga)).astype(BF16)

        @pl.when(i == nt - 1)
        def _():
            for acc, hbm in ((dwo_ref, dwo_hbm), (dwp_ref, dwp_hbm), (dwc_ref, dwc_hbm), (dwa_ref, dwa_hbm)):
                pltpu.sync_copy(acc, hbm)

    def acc_spec(shape):
        n = len(shape)
        return pl.BlockSpec(shape, lambda i: (0,) * n)

    tok = lambda w: pl.BlockSpec((tb, w), lambda i: (i, 0))
    any_spec = pl.BlockSpec(memory_space=pl.ANY)
    w_shapes = [(D_MODEL, D_MODEL), (UNIT, D_MODEL), (UNIT, D_MODEL), (UNIT, D_MODEL)]
    return pl.pallas_call(
        body, name="mix_bwd",
        grid=(nt,),
        in_specs=_mix_in_specs(tb, conv_inputs=False) + [tok(UNIT), tok(UNIT), tok(D_MODEL), tok(D_MODEL)]
        + _mix_weight_specs(),
        out_specs=[pl.BlockSpec((9, tb, UNIT), lambda i: (0, i, 0)), tok(UNIT), tok(UNIT), tok(UNIT)]
        + [any_spec] * 4 + [acc_spec((4, GROUP, GROUP)), acc_spec((8, D_MODEL))],
        out_shape=[S((9, t, UNIT), BF16), S((t, UNIT), F32), S((t, UNIT), F32), S((t, UNIT), BF16)]
        + [S(s, F32) for s in w_shapes] + [S((4, GROUP, GROUP), F32), S((8, D_MODEL), F32)],
        scratch_shapes=[pltpu.VMEM((HALO + tb, UNIT), F32)] + [pltpu.VMEM(s, F32) for s in w_shapes],
        compiler_params=_params(1),
    )(*([proj] * nv), o, cv, x, dxn, *weights)


def _halo_bwd(proj, dcv, dd, conv_w):
    t = dcv.shape[0]
    tb = min(256, t)
    hb = tb // HALO
    n_halo_blocks = t // HALO
    nt = t // tb

    def body(ca_ref, cah_ref, cb_ref, cbh_ref, dcv_ref, dcvh_ref, dd_ref, ddh_ref, cw_ref,
             dpre_ref, dcw_ref, ubuf, gbuf, nbuf, ru_buf, rg_buf, acc_ref):
        i = pl.program_id(0)
        first = i == 0
        last = i == nt - 1

        @pl.when(first)
        def _():
            acc_ref[...] = jnp.zeros_like(acc_ref)

        a, b = ca_ref[0], cb_ref[0]
        sb = _sigmoid(b)
        ubuf[0:HALO, :] = jnp.where(first, 0.0, cah_ref[0] * _sigmoid(cbh_ref[0]))
        ubuf[HALO:HALO + tb, :] = a * sb
        dcv_main = dcv_ref[...]
        gbuf[0:tb, :] = dcv_main
        gbuf[tb:tb + HALO, :] = jnp.where(last, 0.0, dcvh_ref[...])

        off = HALO - (CONV_K - 1)
        u_at = _shifted_reader(ubuf, ru_buf, tb)
        g_at = _shifted_reader(gbuf, rg_buf, tb)
        du = jnp.zeros((tb, UNIT), F32)
        for k in range(CONV_K):
            du = du + cw_ref[k:k + 1, :] * g_at(CONV_K - 1 - k)
            acc_ref[k] += jnp.sum((dcv_main * u_at(off + k)).reshape(tb // 8, 8, UNIT), axis=0)
        dpre_ref[1] = (du * sb).astype(BF16)
        dpre_ref[2] = (du * a * sb * (1.0 - sb)).astype(BF16)

        @pl.when(last)
        def _():
            dcw_ref[...] = jnp.sum(acc_ref[...], axis=1)

        tpos = i * tb + lax.broadcasted_iota(jnp.int32, (tb + HALO, 1), 0)
        dd_main = dd_ref[...]
        dd_ext = jnp.concatenate([dd_main, jnp.where(last, 0.0, ddh_ref[...])], axis=0)
        dp_parts = []
        for g, win in enumerate(POOL_WINDOWS):
            cs = slice(g * GROUP, (g + 1) * GROUP)
            cnt = jnp.minimum(tpos + 1, win).astype(F32)
            nbuf[:, cs] = dd_ext[:, cs] / cnt
        for g, win in enumerate(POOL_WINDOWS):
            cs = slice(g * GROUP, (g + 1) * GROUP)
            s = nbuf[0:tb, cs]
            for j in range(1, win):
                s = s + nbuf[j:j + tb, cs]
            dp_parts.append(s - dd_main[:, cs])
        dpre_ref[0] = jnp.concatenate(dp_parts, axis=1).astype(BF16)

    def unit(u):
        return pl.BlockSpec((1, tb, UNIT), lambda i: (u, i, 0))

    def past(u):
        return pl.BlockSpec((1, HALO, UNIT), lambda i: (u, jnp.maximum(i * hb - 1, 0), 0))

    tok = pl.BlockSpec((tb, UNIT), lambda i: (i, 0))
    future = pl.BlockSpec((HALO, UNIT), lambda i: (jnp.minimum((i + 1) * hb, n_halo_blocks - 1), 0))
    return pl.pallas_call(
        body, name="halo_bwd",
        grid=(nt,),
        in_specs=[unit(U_CA), past(U_CA), unit(U_CB), past(U_CB), tok, future, tok, future,
                  _const_spec((CONV_K_PAD, UNIT))],
        out_specs=[pl.BlockSpec((3, tb, UNIT), lambda i: (0, i, 0)),
                   pl.BlockSpec((CONV_K_PAD, UNIT), lambda i: (0, 0))],
        out_shape=[S((3, t, UNIT), BF16), S((CONV_K_PAD, UNIT), F32)],
        scratch_shapes=[pltpu.VMEM((HALO + tb, UNIT), F32)] * 3
        + [pltpu.VMEM((8, HALO + tb - 8, UNIT), F32)] * 2 + [pltpu.VMEM((CONV_K_PAD, 8, UNIT), F32)],
        compiler_params=_params(1),
    )(proj, proj, proj, proj, dcv, dcv, dd, dd, conv_w)


def _loss_head(y, target):
    t = y.shape[0]
    tb = min(512, t)

    def body(y_ref, t_ref, loss_ref, dy_ref):
        @pl.when(pl.program_id(0) == 0)
        def _():
            loss_ref[...] = jnp.zeros_like(loss_ref)

        err = y_ref[...] - t_ref[...]
        dy_ref[...] = err * (1.0 / D_MODEL)
        part = 0.5 * jnp.sum(jnp.mean(err * err, axis=-1, keepdims=True), axis=0, keepdims=True)
        r8 = lax.broadcasted_iota(jnp.int32, (8, LANES), 0)
        c8 = lax.broadcasted_iota(jnp.int32, (8, LANES), 1)
        loss_ref[...] += jnp.where((r8 == 0) & (c8 == 0), part, 0.0)

    return pl.pallas_call(
        body, name="loss_head",
        grid=(t // tb,),
        in_specs=[pl.BlockSpec((tb, D_MODEL), lambda i: (i, 0))] * 2,
        out_specs=[pl.BlockSpec((8, LANES), lambda i: (0, 0)), pl.BlockSpec((tb, D_MODEL), lambda i: (i, 0))],
        out_shape=[S((8, LANES), F32), S((t, D_MODEL), F32)],
        compiler_params=_params(1),
    )(y, target)


def _adamw(name, parts, w, m, v):
    n, rows, cols = parts.shape
    rb = rows
    while rb * cols * 4 * (n + 7) * 2 > 24 * 1024 * 1024 and rb % 16 == 0:
        rb //= 2

    def body(p_ref, w_ref, m_ref, v_ref, g_ref, d_ref, nm_ref, nv_ref):
        g = p_ref[0]
        for k in range(1, n):
            g = g + p_ref[k]
        nm = ADAM_B1 * m_ref[...] + (1.0 - ADAM_B1) * g
        nv = ADAM_B2 * v_ref[...] + (1.0 - ADAM_B2) * (g * g)
        m_hat = nm / (1.0 - ADAM_B1 ** ADAM_STEP)
        v_hat = nv / (1.0 - ADAM_B2 ** ADAM_STEP)
        g_ref[...] = g
        d_ref[...] = -ADAM_LR * (m_hat / (jnp.sqrt(v_hat) + ADAM_EPS) + ADAM_WD * w_ref[...])
        nm_ref[...] = nm
        nv_ref[...] = nv

    blk = pl.BlockSpec((rb, cols), lambda i: (i, 0))
    return pl.pallas_call(
        body, name=name,
        grid=(rows // rb,),
        in_specs=[pl.BlockSpec((n, rb, cols), lambda i: (0, i, 0)), blk, blk, blk],
        out_specs=[blk] * 4,
        out_shape=[S((rows, cols), F32)] * 4,
        compiler_params=_params(1),
    )(parts, w, m, v)


def _axis_slice(ref, axis, block, size):
    idx = [slice(None)] * len(ref.shape)
    idx[axis] = pl.ds(pl.multiple_of(block * size, size), size)
    return ref.at[tuple(idx)]


def _axis_index(ref, axis, index):
    return ref.at[tuple([slice(None)] * axis + [index])]


COPY_PIECE_BYTES = 2 * 1024 * 1024


def _row_pieces(shape):
    rows = shape[-2]
    size = 4 * rows * shape[-1]
    for d in shape[:-2]:
        size *= d
    n = 1
    while size // n > COPY_PIECE_BYTES and rows % (2 * n * 8) == 0:
        n *= 2
    return [(k * (rows // n), rows // n) for k in range(n)]


def _rows(ref, r0, rows):
    idx = [slice(None)] * len(ref.shape)
    idx[-2] = pl.ds(r0, rows)
    return ref.at[tuple(idx)]


def _staged_local_copies(pairs, bufs, in_sems, out_sems):
    ins = [pltpu.make_async_copy(src, buf, in_sems.at[k]) for k, ((src, _), buf) in enumerate(zip(pairs, bufs))]
    for cp in ins:
        cp.start()
    for cp in ins:
        cp.wait()
    outs = [pltpu.make_async_copy(buf, dst, out_sems.at[k]) for k, ((_, dst), buf) in enumerate(zip(pairs, bufs))]
    for cp in outs:
        cp.start()
    return outs


def _run_copies(local, sends, recvs):
    for cp in local + sends:
        cp.start()
    for cp in recvs:
        cp.wait_recv()
    for cp in sends:
        cp.wait_send()
    for cp in local:
        cp.wait()


def _gather_weights(shards, axes):
    n = len(shards)
    fulls = [S(tuple(4 * d if k == ax else d for k, d in enumerate(s.shape)), s.dtype) for s, ax in zip(shards, axes)]

    def body(*refs):
        src, dst = refs[0:n], refs[n:2 * n]
        send_sems, recv_sems, pass_send_sems, pass_recv_sems, in_sems, out_sems = refs[2 * n:2 * n + 6]
        bufs = refs[2 * n + 6:3 * n + 6]
        x, y, c = lax.axis_index("x"), lax.axis_index("y"), lax.axis_index("c")
        chips = [(1 - x, y), (x, 1 - y), (1 - x, 1 - y)]

        def place(a, chip, layer=None):
            ref = _axis_slice(dst[a], axes[a], 2 * chip[0] + chip[1], src[a].shape[axes[a]])
            return ref if layer is None else ref.at[pl.ds(layer, 1)]

        def over_ici(k, chip_of_block, to):
            a, j, r0, rows = pieces[k]
            return pltpu.make_async_remote_copy(
                src_ref=_rows(src[a].at[pl.ds(c, 1)], r0, rows), dst_ref=_rows(place(a, chip_of_block, c), r0, rows),
                send_sem=send_sems.at[k], recv_sem=recv_sems.at[k], device_id=(to[0], to[1], c), device_id_type=MESH)

        def to_sibling(k, chip_of_block, layer):
            a, j, r0, rows = pieces[k]
            block = _rows(place(a, chip_of_block, layer), r0, rows)
            return pltpu.make_async_remote_copy(
                src_ref=block, dst_ref=block, send_sem=pass_send_sems.at[k], recv_sem=pass_recv_sems.at[k],
                device_id=(x, y, 1 - c), device_id_type=MESH)

        first = [over_ici(k, (x, y), chips[j]) for k, (a, j, r0, rows) in enumerate(pieces)]
        for cp in first:
            cp.start()
        local = _staged_local_copies([(src[a], place(a, (x, y))) for a in range(n)], bufs, in_sems, out_sems)
        passed = []
        for k, (a, j, r0, rows) in enumerate(pieces):
            over_ici(k, chips[j], chips[j]).wait_recv()
            passed.append(to_sibling(k, chips[j], c))
            passed[-1].start()
        for k, (a, j, r0, rows) in enumerate(pieces):
            to_sibling(k, chips[j], 1 - c).wait_recv()
        for cp in first + passed:
            cp.wait_send()
        for cp in local:
            cp.wait()

    per_array = [_row_pieces((1,) + shards[a].shape[1:]) for a in range(n)]
    pieces = [(a, j, r0, rows) for a in range(n) for r0, rows in per_array[a] for j in range(3)]
    any_spec = pl.BlockSpec(memory_space=pl.ANY)
    return pl.pallas_call(
        body, name="gather_weights",
        in_specs=[any_spec] * n, out_specs=[any_spec] * n, out_shape=fulls,
        scratch_shapes=[pltpu.SemaphoreType.DMA((len(pieces),))] * 4 + [pltpu.SemaphoreType.DMA((n,))] * 2
        + [pltpu.VMEM(s.shape, s.dtype) for s in shards],
        compiler_params=_params(0),
    )(*shards)


def _swap_layers(layers):
    n = len(layers)
    pieces = [(a, r0, rows) for a in range(n) for r0, rows in _row_pieces(layers[a][0].shape)]

    def body(*refs):
        src = [refs[2 * a:2 * a + 2] for a in range(n)]
        dst = refs[2 * n:3 * n]
        send_sems, recv_sems = refs[3 * n:3 * n + 2]
        x, y, c = lax.axis_index("x"), lax.axis_index("y"), lax.axis_index("c")

        def remote(k, layer):
            a, r0, rows = pieces[k]
            return pltpu.make_async_remote_copy(
                src_ref=_rows(src[a][layer], r0, rows), dst_ref=_rows(dst[a], r0, rows),
                send_sem=send_sems.at[k], recv_sem=recv_sems.at[k], device_id=(x, y, 1 - c), device_id_type=MESH)

        for layer in range(2):
            @pl.when(c == 1 - layer)
            def _(layer=layer):
                for k in range(len(pieces)):
                    remote(k, layer).start()

        for k in range(len(pieces)):
            remote(k, 0).wait_recv()
        for k in range(len(pieces)):
            remote(k, 0).wait_send()

    any_spec = pl.BlockSpec(memory_space=pl.ANY)
    return pl.pallas_call(
        body, name="swap_layers",
        in_specs=[any_spec] * (2 * n), out_specs=[any_spec] * n,
        out_shape=[S(layers[a][0].shape, layers[a][0].dtype) for a in range(n)],
        scratch_shapes=[pltpu.SemaphoreType.DMA((len(pieces),))] * 2,
    )(*[arr for pair in layers for arr in pair])


def _add_own_layer(name, mine, theirs, layer):
    rows, cols = theirs.shape
    rb = rows
    while rb * cols * 4 * 4 * 2 > 24 * 1024 * 1024 and rb % 32 == 0:
        rb //= 2

    def body(layer_ref, m0_ref, m1_ref, t_ref, o_ref):
        own = jnp.where(layer_ref[0] == 0, m0_ref[...], m1_ref[...])
        o_ref[0] = (own + t_ref[...]).astype(BF16)

    def own_spec(which):
        return pl.BlockSpec((rb, cols), lambda i, l: (jnp.where(l[0] == which, i, 0), 0))

    return pl.pallas_call(
        body, name=name,
        grid_spec=pltpu.PrefetchScalarGridSpec(
            num_scalar_prefetch=1, grid=(rows // rb,),
            in_specs=[own_spec(0), own_spec(1), pl.BlockSpec((rb, cols), lambda i, l: (i, 0))],
            out_specs=pl.BlockSpec((1, rb, cols), lambda i, l: (0, i, 0))),
        out_shape=S((1, rows, cols), BF16),
        compiler_params=_params(1),
    )(layer, *mine, theirs)


def _send_to_owners(chip_halves, shard_axes, packed):
    n = len(chip_halves)

    def owned_shape(a):
        return tuple(d // 4 if k == shard_axes[a] else d for k, d in enumerate(chip_halves[a].shape))

    flips = [(dx, dy, dc) for dx in (0, 1) for dy in (0, 1) for dc in (0, 1)][1:]

    def body(*refs):
        src, psrc, dst, pdst = refs[0:n], refs[n], refs[n + 1:2 * n + 1], refs[2 * n + 1]
        send_sems, recv_sems, local_sems, psend_sems, precv_sems = refs[2 * n + 2:2 * n + 7]
        x, y, c = lax.axis_index("x"), lax.axis_index("y"), lax.axis_index("c")
        my_chip = 2 * x + y
        chips = [(1 - x, y), (x, 1 - y), (1 - x, 1 - y)]

        def owned(a, chip):
            return _axis_slice(src[a], shard_axes[a], 2 * chip[0] + chip[1], src[a].shape[shard_axes[a]] // 4)

        def remote(a, j, to, from_chip):
            return pltpu.make_async_remote_copy(
                src_ref=owned(a, to), dst_ref=dst[a].at[from_chip], send_sem=send_sems.at[a, j],
                recv_sem=recv_sems.at[a, j], device_id=(to[0], to[1], c), device_id_type=MESH)

        def peer(f):
            return (1 - x if f[0] else x, 1 - y if f[1] else y, 1 - c if f[2] else c)

        def premote(j, to, from_dev):
            return pltpu.make_async_remote_copy(
                src_ref=psrc, dst_ref=pdst.at[from_dev], send_sem=psend_sems.at[j], recv_sem=precv_sems.at[j],
                device_id=to, device_id_type=MESH)

        local = [pltpu.make_async_copy(_rows(owned(a, (x, y)), r0, rows), _rows(dst[a].at[my_chip], r0, rows),
                                       local_sems.at[k]) for k, (a, r0, rows) in enumerate(own_pieces)]
        local.append(pltpu.make_async_copy(psrc, pdst.at[4 * x + 2 * y + c], local_sems.at[len(own_pieces)]))
        sends = [remote(a, j, chip, my_chip) for a in range(n) for j, chip in enumerate(chips)]
        sends += [premote(j, peer(f), 4 * x + 2 * y + c) for j, f in enumerate(flips)]
        recvs = [remote(a, j, (x, y), 2 * chip[0] + chip[1]) for a in range(n) for j, chip in enumerate(chips)]
        for j, f in enumerate(flips):
            px, py, pc = peer(f)
            recvs.append(premote(j, (x, y, c), 4 * px + 2 * py + pc))
        _run_copies(local, sends, recvs)

    own_pieces = [(a, r0, rows) for a in range(n) for r0, rows in _row_pieces(owned_shape(a))]
    any_spec = pl.BlockSpec(memory_space=pl.ANY)
    return pl.pallas_call(
        body, name="send_to_owners",
        in_specs=[any_spec] * (n + 1), out_specs=[any_spec] * (n + 1),
        out_shape=[S((4,) + owned_shape(a), chip_halves[a].dtype) for a in range(n)]
        + [S((N_DEV,) + packed.shape, packed.dtype)],
        scratch_shapes=[pltpu.SemaphoreType.DMA((n, 3)), pltpu.SemaphoreType.DMA((n, 3)),
                        pltpu.SemaphoreType.DMA((len(own_pieces) + 1,)), pltpu.SemaphoreType.DMA((7,)),
                        pltpu.SemaphoreType.DMA((7,))],
    )(*chip_halves, packed)


def _join_halves(reduced, half_axes):
    n = len(reduced)

    def joined_shape(a):
        s, ax = reduced[a].shape, half_axes[a]
        return s[:ax] + (2,) + s[ax:]

    pieces = [(a, r0, rows) for a in range(n) for r0, rows in _row_pieces(reduced[a].shape)]

    def body(*refs):
        src, dst = refs[0:n], refs[n:2 * n]
        send_sems, recv_sems, in_sems, out_sems = refs[2 * n:2 * n + 4]
        bufs = refs[2 * n + 4:3 * n + 4]
        x, y, c = lax.axis_index("x"), lax.axis_index("y"), lax.axis_index("c")

        def remote(k, h):
            a, r0, rows = pieces[k]
            return pltpu.make_async_remote_copy(
                src_ref=_rows(src[a], r0, rows), dst_ref=_rows(_axis_index(dst[a], half_axes[a], h), r0, rows),
                send_sem=send_sems.at[k], recv_sem=recv_sems.at[k], device_id=(x, y, 1 - c), device_id_type=MESH)

        sends = [remote(k, c) for k in range(len(pieces))]
        for cp in sends:
            cp.start()
        own = [(src[a], _axis_index(dst[a], half_axes[a], c)) for a in range(n)]
        local = _staged_local_copies(own, bufs, in_sems, out_sems)
        for k in range(len(pieces)):
            remote(k, 1 - c).wait_recv()
        for cp in sends:
            cp.wait_send()
        for cp in local:
            cp.wait()

    any_spec = pl.BlockSpec(memory_space=pl.ANY)
    return pl.pallas_call(
        body, name="join_halves",
        in_specs=[any_spec] * n, out_specs=[any_spec] * n,
        out_shape=[S(joined_shape(a), reduced[a].dtype) for a in range(n)],
        scratch_shapes=[pltpu.SemaphoreType.DMA((len(pieces),))] * 2 + [pltpu.SemaphoreType.DMA((n,))] * 2
        + [pltpu.VMEM(r.shape, r.dtype) for r in reduced],
        compiler_params=_params(0),
    )(*reduced)


def _sum_slots(name, parts):
    n = parts.shape[0]
    shape = parts.shape[1:]
    flat = parts.reshape((n, -1, shape[-1]))
    rows, cols = flat.shape[1:]
    rb = rows
    while rb * cols * 4 * (n + 1) * 2 > 24 * 1024 * 1024 and rb % 32 == 0:
        rb //= 2

    def body(p_ref, o_ref):
        acc = p_ref[0].astype(F32)
        for k in range(1, n):
            acc = acc + p_ref[k].astype(F32)
        o_ref[...] = acc

    out = pl.pallas_call(
        body, name=name,
        grid=(rows // rb,),
        in_specs=[pl.BlockSpec((n, rb, cols), lambda i: (0, i, 0))],
        out_specs=pl.BlockSpec((rb, cols), lambda i: (i, 0)),
        out_shape=S((rows, cols), F32),
        compiler_params=_params(1),
    )(flat)
    return out.reshape(shape)


SMALL_NAMES = ("norm_pre", "pool_w", "pool_b", "pool_scale", "conv_b", "conv_ln_g", "conv_ln_b", "norm_post")
BIG_NAMES = ("w_in", "w_pool_out", "conv_w", "w_conv_out", "w_attn_out", "w_o")
SHARD_AXIS = {"w_in": 2, "w_pool_out": 2, "conv_w": 2, "w_conv_out": 2, "w_attn_out": 2, "w_o": 1}
WEIGHT_ORDER = ("norm_pre", "w_in", "pool_w", "pool_b", "pool_scale", "w_pool_out", "conv_w", "conv_b",
                "conv_ln_g", "conv_ln_b", "w_conv_out", "w_attn_out", "w_o", "norm_post")


def _pack_small(parts):
    return jnp.concatenate([parts[n].reshape(-1, LANES) for n in SMALL_NAMES], axis=0)


def _unpack_small(packed, shapes):
    out, r0 = {}, 0
    for n in SMALL_NAMES:
        size = 1
        for d in shapes[n]:
            size *= d
        rows = size // LANES
        out[n] = packed[r0:r0 + rows].reshape(shapes[n])
        r0 += rows
    return out


def _layer_weights(full, small, l):
    row = lambda a: a[l][None, :]
    return (small["pool_w"][l].astype(BF16), small["pool_b"][l].reshape(1, UNIT), row(small["pool_scale"]),
            full["w_pool_out"][l], full["conv_w"][l], row(small["conv_b"]), row(small["conv_ln_g"]),
            row(small["conv_ln_b"]), full["w_conv_out"][l], full["w_attn_out"][l], full["w_o"][l],
            row(small["norm_post"]))


def _forward_backward(x, target, full, small):
    depth = full["w_in"].shape[0]
    acts = []
    for l in range(depth):
        g_pre = small["norm_pre"][l][None, :]
        proj = _inproj_fwd(x, g_pre, full["w_in"][l])
        o, bsum, kfirst = _attn_fwd(proj)
        weights = _layer_weights(full, small, l)
        x_next, cv = _mix_fwd(proj, o, x, weights)
        acts.append((x, proj, o, cv, bsum, kfirst, weights, g_pre))
        x = x_next
    loss_tile, dx = _loss_head(x, target)

    big = {n: [None] * depth for n in BIG_NAMES}
    sm = {n: [None] * depth for n in SMALL_NAMES}
    for l in reversed(range(depth)):
        x_in, proj, o, cv, bsum, kfirst, weights, g_pre = acts[l]
        dgates, dd, dcv, d_o, dwo, dwp, dwc, dwa, dpw, dvec = _mix_bwd(proj, o, cv, x_in, dx, weights)
        dpre, dcw = _halo_bwd(proj, dcv, dd, full["conv_w"][l])
        dq, dk, dv = _attn_bwd(proj, d_o, bsum, kfirst)
        dproj_parts = (dpre, dgates, dq, dk, dv)
        dx, dg_pre = _inproj_bwd_x(x_in, g_pre, full["w_in"][l], dproj_parts, dx)
        big["w_in"][l] = _inproj_bwd_w(x_in, g_pre, dproj_parts)
        big["w_pool_out"][l], big["w_conv_out"][l], big["w_attn_out"][l], big["w_o"][l] = dwp, dwc, dwa, dwo
        big["conv_w"][l] = dcw
        sm["norm_pre"][l] = dg_pre[0]
        sm["pool_w"][l] = dpw
        sm["norm_post"][l] = dvec[0]
        sm["pool_b"][l] = dvec[1, 0:UNIT].reshape(4, GROUP)
        sm["pool_scale"][l] = dvec[2, 0:UNIT]
        sm["conv_b"][l] = dvec[3, 0:UNIT]
        sm["conv_ln_g"][l] = dvec[4, 0:UNIT]
        sm["conv_ln_b"][l] = dvec[5, 0:UNIT]
    sm = {n: jnp.stack(v) for n, v in sm.items()}
    return loss_tile, dx, big, sm


def kernel(x, norm_pre, w_in, pool_w, pool_b, pool_scale, w_pool_out, conv_w, conv_b, conv_ln_g, conv_ln_b, w_conv_out, w_attn_out, w_o, norm_post, loss_target, m_norm_pre, m_w_in, m_pool_w, m_pool_b, m_pool_scale, m_w_pool_out, m_conv_w, m_conv_b, m_conv_ln_g, m_conv_ln_b, m_w_conv_out, m_w_attn_out, m_w_o, m_norm_post, v_norm_pre, v_w_in, v_pool_w, v_pool_b, v_pool_scale, v_w_pool_out, v_conv_w, v_conv_b, v_conv_ln_g, v_conv_ln_b, v_w_conv_out, v_w_attn_out, v_w_o, v_norm_post):
    w = dict(norm_pre=norm_pre, w_in=w_in, pool_w=pool_w, pool_b=pool_b, pool_scale=pool_scale,
             w_pool_out=w_pool_out, conv_w=conv_w, conv_b=conv_b, conv_ln_g=conv_ln_g, conv_ln_b=conv_ln_b,
             w_conv_out=w_conv_out, w_attn_out=w_attn_out, w_o=w_o, norm_post=norm_post)
    m = dict(norm_pre=m_norm_pre, w_in=m_w_in, pool_w=m_pool_w, pool_b=m_pool_b, pool_scale=m_pool_scale,
             w_pool_out=m_w_pool_out, conv_w=m_conv_w, conv_b=m_conv_b, conv_ln_g=m_conv_ln_g,
             conv_ln_b=m_conv_ln_b, w_conv_out=m_w_conv_out, w_attn_out=m_w_attn_out, w_o=m_w_o,
             norm_post=m_norm_post)
    v = dict(norm_pre=v_norm_pre, w_in=v_w_in, pool_w=v_pool_w, pool_b=v_pool_b, pool_scale=v_pool_scale,
             w_pool_out=v_w_pool_out, conv_w=v_conv_w, conv_b=v_conv_b, conv_ln_g=v_conv_ln_g,
             conv_ln_b=v_conv_ln_b, w_conv_out=v_w_conv_out, w_attn_out=v_w_attn_out, w_o=v_w_o,
             norm_post=v_norm_post)
    pad_taps = lambda a: jnp.pad(a, ((0, 0), (0, CONV_K_PAD - CONV_K), (0, 0)))

    shards = [pad_taps(w[n]) if n == "conv_w" else w[n].astype(BF16) for n in BIG_NAMES]
    full = dict(zip(BIG_NAMES, _gather_weights(shards, [SHARD_AXIS[n] for n in BIG_NAMES])))
    small = {n: w[n] for n in SMALL_NAMES}

    loss_tile, dx, big, sm = _forward_backward(x[0], loss_target[0], full, small)

    packed = jnp.concatenate([_pack_small(sm), loss_tile], axis=0)
    layer_axis = [0] * len(BIG_NAMES)
    theirs = _swap_layers([big[n] for n in BIG_NAMES])
    my_layer = lax.axis_index("c").astype(jnp.int32).reshape(1)
    chip_sums = [_add_own_layer("sum_cores_" + n, big[n], t, my_layer) for n, t in zip(BIG_NAMES, theirs)]
    exchanged = _send_to_owners(chip_sums, [SHARD_AXIS[n] for n in BIG_NAMES], packed)
    reduced = [_sum_slots("sum_chips_" + n, p) for n, p in zip(BIG_NAMES, exchanged[:-1])]
    joined = _join_halves(reduced, layer_axis)
    out = {}
    for n, g in zip(BIG_NAMES, joined):
        wn, mn, vn = (pad_taps(a[n]) if n == "conv_w" else a[n] for a in (w, m, v))
        shape = wn.shape
        flat = lambda a: a.reshape((-1, shape[-1]))
        res = _adamw("adamw_" + n, g.reshape((1, -1, shape[-1])), flat(wn), flat(mn), flat(vn))
        res = [r.reshape(shape) for r in res]
        out[n] = [r[:, :CONV_K] for r in res] if n == "conv_w" else res
    parts = exchanged[-1]
    n_small_rows = parts.shape[1] - 8
    zeros_tile = jnp.zeros((8, LANES), F32)
    packs = [jnp.concatenate([_pack_small(a), zeros_tile], axis=0) for a in (w, m, v)]
    res = _adamw("adamw_small", parts, *packs)
    loss = res[0][n_small_rows, 0]
    shapes = {n: w[n].shape for n in SMALL_NAMES}
    unpacked = [_unpack_small(r[:n_small_rows], shapes) for r in res]
    for n in SMALL_NAMES:
        out[n] = [u[n] for u in unpacked]

    grads = [out[n][0] for n in WEIGHT_ORDER]
    deltas = [out[n][1] for n in WEIGHT_ORDER]
    new_m = [out[n][2] for n in WEIGHT_ORDER]
    new_v = [out[n][3] for n in WEIGHT_ORDER]
    return (loss, dx[None], *grads, *deltas, *new_m, *new_v)
```

```python
import jax
import jax.numpy as jnp
from jax import lax
from jax.experimental import pallas as pl
from jax.experimental.pallas import tpu as pltpu

F32 = jnp.float32
BF16 = jnp.bfloat16

D_MODEL = 1024
UNIT = 512
N_UNITS = 15
IN_WIDTH = UNIT * N_UNITS
N_HEADS = 8
HEAD_DIM = 64
HEADS_PER_BLOCK = 2
N_HEAD_BLOCKS = N_HEADS // HEADS_PER_BLOCK
LANES = 128
CONV_K = 31
CONV_K_PAD = 32
HALO = 32
POOL_WINDOWS = (2, 4, 8, 16)
GROUP = 128
N_BRANCH = 3
RMS_EPS = 1e-6
LN_EPS = 1e-5
ATTN_SCALE = 0.125
EXP_ZERO_BELOW = -104.0

ADAM_LR = 0.001
ADAM_B1 = 0.9
ADAM_B2 = 0.999
ADAM_EPS = 1e-08
ADAM_WD = 0.01
ADAM_STEP = 10

U_P, U_GP, U_CA, U_CB, U_GC, U_Q, U_K, U_V, U_GA, U_GM = 0, 1, 2, 3, 4, 5, 6, 7, 8, 9

V7X_VMEM_LIMIT = 62 * 1024 * 1024
N_DEV = 8
MESH = pl.DeviceIdType.MESH

S = jax.ShapeDtypeStruct


def _params(n_grid):
    return pltpu.CompilerParams(dimension_semantics=("arbitrary",) * n_grid, vmem_limit_bytes=V7X_VMEM_LIMIT)


def _sigmoid(x):
    return 0.5 * jnp.tanh(0.5 * x) + 0.5


def _dsilu(x, s):
    return s * (1.0 + x * (1.0 - s))


def _dot(a, b):
    return jnp.dot(a, b, preferred_element_type=F32)


def _dot_nt(a, b):
    return lax.dot_general(a, b, (((1,), (1,)), ((), ())), preferred_element_type=F32)


def _dot_tn(a, b):
    return lax.dot_general(a, b, (((0,), (0,)), ((), ())), preferred_element_type=F32)


def _split_bf16(x):
    hi = x.astype(BF16)
    lo = (x - hi.astype(F32)).astype(BF16)
    return hi, lo


def _const_spec(shape):
    n = len(shape)
    return pl.BlockSpec(shape, lambda *_: (0,) * n, pipeline_mode=pl.Buffered(1))


def _inproj_fwd(x, g_pre, w_in, layer):
    t = x.shape[0]
    tb = min(1024, t)
    ug = 3
    nb = ug * UNIT

    def body(x_ref, g_ref, w_ref, o_ref, h_ref):
        @pl.when(pl.program_id(1) == 0)
        def _():
            xf = x_ref[...]
            r = lax.rsqrt(jnp.mean(xf * xf, axis=-1, keepdims=True) + RMS_EPS)
            h_ref[...] = (xf * r * g_ref[...]).astype(BF16)

        acc = _dot(h_ref[...], w_ref[0])
        for u in range(ug):
            o_ref[u] = acc[:, u * UNIT:(u + 1) * UNIT]

    return pl.pallas_call(
        body, name="inproj_fwd",
        grid=(t // tb, N_UNITS // ug),
        in_specs=[pl.BlockSpec((tb, D_MODEL), lambda i, j: (i, 0)),
                  pl.BlockSpec((1, D_MODEL), lambda i, j: (0, 0)),
                  pl.BlockSpec((1, D_MODEL, nb), lambda i, j: (layer, 0, j))],
        out_specs=pl.BlockSpec((ug, tb, UNIT), lambda i, j: (j, i, 0)),
        out_shape=S((N_UNITS, t, UNIT), F32),
        scratch_shapes=[pltpu.VMEM((tb, D_MODEL), BF16)],
        compiler_params=_params(2),
    )(x, g_pre, w_in)


def _dproj_specs(tb):
    tok = pl.BlockSpec((tb, UNIT), lambda i: (i, 0))
    return [pl.BlockSpec((3, tb, UNIT), lambda i: (0, i, 0)), pl.BlockSpec((9, tb, UNIT), lambda i: (0, i, 0)),
            tok, tok, tok]


def _dproj_unit(u, dpre_ref, dgates_ref, dq_ref, dk_ref, dv_ref):
    pre = {U_P: 0, U_CA: 1, U_CB: 2}
    gate = {U_GP: 0, U_GC: 1, U_GA: 2}
    if u in pre:
        val = dpre_ref[pre[u]]
    elif u in gate:
        val = dgates_ref[gate[u]]
    elif u >= U_GM:
        val = dgates_ref[3 + u - U_GM]
    else:
        val = {U_Q: dq_ref, U_K: dk_ref, U_V: dv_ref}[u][...]
    return val.astype(BF16)


def _inproj_bwd_x(x, g_pre, w_in, layer, dproj_parts, dxn):
    t = x.shape[0]
    tb = min(256, t)

    def body(x_ref, g_ref, w_ref, dpre_ref, dgates_ref, dq_ref, dk_ref, dv_ref, dxn_ref, dx_ref, dg_ref):
        @pl.when(pl.program_id(0) == 0)
        def _():
            dg_ref[...] = jnp.zeros_like(dg_ref)

        dh = jnp.zeros((tb, D_MODEL), F32)
        for u in range(N_UNITS):
            dh = dh + _dot_nt(_dproj_unit(u, dpre_ref, dgates_ref, dq_ref, dk_ref, dv_ref),
                              w_ref[0, :, u * UNIT:(u + 1) * UNIT])
        xf = x_ref[...]
        r = lax.rsqrt(jnp.mean(xf * xf, axis=-1, keepdims=True) + RMS_EPS)
        xhat = xf * r
        dg_ref[...] += jnp.sum(dh * xhat, axis=0, keepdims=True)
        dxhat = dh * g_ref[...]
        dx_ref[...] = dxn_ref[...] + r * (dxhat - xhat * jnp.mean(dxhat * xhat, axis=-1, keepdims=True))

    tokd = pl.BlockSpec((tb, D_MODEL), lambda i: (i, 0))
    return pl.pallas_call(
        body, name="inproj_bwd_x",
        grid=(t // tb,),
        in_specs=[tokd, _const_spec((1, D_MODEL)),
                  pl.BlockSpec((1, D_MODEL, IN_WIDTH), lambda i: (layer, 0, 0), pipeline_mode=pl.Buffered(1))]
        + _dproj_specs(tb) + [tokd],
        out_specs=[tokd, pl.BlockSpec((1, D_MODEL), lambda i: (0, 0))],
        out_shape=[S((t, D_MODEL), F32), S((1, D_MODEL), F32)],
        compiler_params=_params(1),
    )(x, g_pre, w_in, *dproj_parts, dxn)


def _inproj_bwd_w(x, g_pre, dproj_parts):
    t = x.shape[0]
    tb = min(256, t)
    nt = t // tb

    def body(x_ref, g_ref, dpre_ref, dgates_ref, dq_ref, dk_ref, dv_ref, dw_hbm, acc_ref):
        i = pl.program_id(0)
        xf = x_ref[...]
        r = lax.rsqrt(jnp.mean(xf * xf, axis=-1, keepdims=True) + RMS_EPS)
        ht = (xf * r * g_ref[...]).T.astype(BF16)

        @pl.when(i == 0)
        def _():
            acc_ref[...] = jnp.zeros_like(acc_ref)

        for u in range(N_UNITS):
            acc_ref[:, u * UNIT:(u + 1) * UNIT] += _dot(
                ht, _dproj_unit(u, dpre_ref, dgates_ref, dq_ref, dk_ref, dv_ref))

        @pl.when(i == nt - 1)
        def _():
            pltpu.sync_copy(acc_ref, dw_hbm)

    return pl.pallas_call(
        body, name="inproj_bwd_w",
        grid=(nt,),
        in_specs=[pl.BlockSpec((tb, D_MODEL), lambda i: (i, 0)), _const_spec((1, D_MODEL))] + _dproj_specs(tb),
        out_specs=pl.BlockSpec(memory_space=pl.ANY),
        out_shape=S((D_MODEL, IN_WIDTH), F32),
        scratch_shapes=[pltpu.VMEM((D_MODEL, IN_WIDTH), F32)],
        compiler_params=_params(1),
    )(x, g_pre, *dproj_parts)


def _attn_tile_size(t):
    return min(256, t)


def _softplus_parts(z):
    sp = jnp.maximum(z, 0.0) + jnp.log(1.0 + jnp.exp(-jnp.abs(z)))
    return -sp, z - sp


def _attn_fwd(proj):
    t = proj.shape[1]
    tq = _attn_tile_size(t)

    def body(q_ref, k_ref, v_ref, o_ref, bs_ref, kf_ref):
        i = pl.program_id(1)
        row = lax.broadcasted_iota(jnp.int32, (tq, tq), 0)
        col = lax.broadcasted_iota(jnp.int32, (tq, tq), 1)
        tri = (row > col).astype(BF16)
        tri2 = jnp.concatenate([tri, tri], axis=0)
        causal = jnp.concatenate([col < row] * HEADS_PER_BLOCK, axis=0)
        lane = lax.broadcasted_iota(jnp.int32, (1, LANES), 1)
        q_all = q_ref[0] * ATTN_SCALE
        in_heads = [(lane >= hh * HEAD_DIM) & (lane < (hh + 1) * HEAD_DIM) for hh in range(HEADS_PER_BLOCK)]
        q_stack = jnp.concatenate([jnp.where(in_head, q_all, 0.0) for in_head in in_heads], axis=0).astype(BF16)
        rows = HEADS_PER_BLOCK * tq

        def tiles(kj, carry, acc, masked):
            ks = pl.multiple_of(kj * tq, tq)
            kh = k_ref[0, pl.ds(ks, tq), :].astype(BF16)
            vh = v_ref[0, pl.ds(ks, tq), :].astype(BF16)
            z = _dot_nt(q_stack, kh)
            lm, lb = _softplus_parts(z)
            if masked:
                lm = jnp.where(causal, lm, 0.0)
            between = _dot(jnp.concatenate(_split_bf16(lm), axis=1), tri2) + carry
            w = jnp.exp(lb + between)
            if masked:
                w = jnp.where(causal, w, 0.0)
            return carry + jnp.sum(lm, axis=1, keepdims=True), acc + _dot(w.astype(BF16), vh)

        carry, acc = tiles(i, jnp.zeros((rows, 1), F32), jnp.zeros((rows, LANES), F32), True)

        def more(c):
            return jnp.logical_and(c[0] >= 0, jnp.max(c[1]) >= EXP_ZERO_BELOW)

        kj_end, carry, acc = lax.while_loop(
            more, lambda c: (c[0] - 1,) + tiles(c[0], c[1], c[2], False), (i - 1, carry, acc))
        o_out, b_out = acc[0:tq], jnp.broadcast_to(carry[0:tq], (tq, LANES))
        for hh in range(1, HEADS_PER_BLOCK):
            o_out = jnp.where(in_heads[hh], acc[hh * tq:(hh + 1) * tq], o_out)
            b_out = jnp.where(in_heads[hh], carry[hh * tq:(hh + 1) * tq], b_out)
        o_ref[...] = o_out
        bs_ref[0] = b_out
        kf_ref[0, 0] = jnp.zeros((8, LANES), F32) + (kj_end + 1).astype(F32)

    nq = t // tq
    return pl.pallas_call(
        body, name="attn_fwd",
        grid=(N_HEAD_BLOCKS, nq),
        in_specs=[pl.BlockSpec((1, tq, LANES), lambda p, i: (U_Q, i, p)),
                  pl.BlockSpec((1, t, LANES), lambda p, i: (U_K, 0, p)),
                  pl.BlockSpec((1, t, LANES), lambda p, i: (U_V, 0, p))],
        out_specs=[pl.BlockSpec((tq, LANES), lambda p, i: (i, p)),
                   pl.BlockSpec((1, tq, LANES), lambda p, i: (p, i, 0)),
                   pl.BlockSpec((1, 1, 8, LANES), lambda p, i: (p, i, 0, 0))],
        out_shape=[S((t, UNIT), F32), S((N_HEAD_BLOCKS, t, LANES), F32), S((N_HEAD_BLOCKS, nq, 8, LANES), F32)],
        compiler_params=_params(2),
    )(proj, proj, proj)


def _attn_bwd(proj, d_o, bsum, kfirst):
    t = proj.shape[1]
    tq = _attn_tile_size(t)
    nq = t // tq

    def body(q_ref, k_ref, v_ref, do_ref, bs_ref, kf_ref, dq_ref, dk_ref, dv_ref, dkt_ref, dvt_ref):
        i = pl.program_id(1)

        @pl.when(i == 0)
        def _():
            dkt_ref[...] = jnp.zeros_like(dkt_ref)
            dvt_ref[...] = jnp.zeros_like(dvt_ref)

        row = lax.broadcasted_iota(jnp.int32, (tq, tq), 0)
        col = lax.broadcasted_iota(jnp.int32, (tq, tq), 1)
        upto = (row <= col).astype(BF16)
        before = (row < col).astype(BF16)
        upto2 = jnp.concatenate([upto, upto], axis=0)
        before2 = jnp.concatenate([before, before], axis=0)
        causal = jnp.concatenate([col < row] * HEADS_PER_BLOCK, axis=0)
        lane = lax.broadcasted_iota(jnp.int32, (1, LANES), 1)
        q_all = q_ref[0] * ATTN_SCALE
        do_all = do_ref[...].astype(F32)
        bs_all = bs_ref[0]
        k_first = jnp.clip(jnp.max(kf_ref[0, 0]).astype(jnp.int32), 0, i)

        in_heads = [(lane >= hh * HEAD_DIM) & (lane < (hh + 1) * HEAD_DIM) for hh in range(HEADS_PER_BLOCK)]
        q_f32 = jnp.concatenate([jnp.where(in_head, q_all, 0.0) for in_head in in_heads], axis=0)
        do_f32 = jnp.concatenate([jnp.where(in_head, do_all, 0.0) for in_head in in_heads], axis=0)
        q_stack, do_stack = q_f32.astype(BF16), do_f32.astype(BF16)
        q_t, do_t = q_f32.T.astype(BF16), do_f32.T.astype(BF16)
        btot = jnp.concatenate([jnp.max(jnp.where(in_head, bs_all, -jnp.inf), axis=1, keepdims=True)
                                for in_head in in_heads], axis=0)
        rows = HEADS_PER_BLOCK * tq

        def tiles(kj, c_b, c_p, dq, masked):
            ks = pl.multiple_of(kj * tq, tq)
            kh = k_ref[0, pl.ds(ks, tq), :].astype(BF16)
            vh = v_ref[0, pl.ds(ks, tq), :].astype(BF16)
            z = _dot_nt(q_stack, kh)
            lm, lb = _softplus_parts(z)
            if masked:
                lm = jnp.where(causal, lm, 0.0)
            between = btot - (c_b + _dot(jnp.concatenate(_split_bf16(lm), axis=1), upto2))
            w = jnp.exp(lb + between)
            if masked:
                w = jnp.where(causal, w, 0.0)
            e = w * _dot_nt(do_stack, vh)
            p_sum = c_p + _dot(jnp.concatenate(_split_bf16(e), axis=1), before2)
            beta = jnp.exp(lb)
            dz = e * (1.0 - beta) - p_sum * beta
            if masked:
                dz = jnp.where(causal, dz, 0.0)
            dzb = dz.astype(BF16)
            dkt_ref[kj] += _dot(q_t, dzb)
            dvt_ref[kj] += _dot(do_t, w.astype(BF16))
            return (c_b + jnp.sum(lm, axis=1, keepdims=True), c_p + jnp.sum(e, axis=1, keepdims=True),
                    dq + _dot(dzb, kh))

        zero = (jnp.zeros((rows, 1), F32), jnp.zeros((rows, 1), F32), jnp.zeros((rows, LANES), F32))
        state = lax.fori_loop(k_first, i, lambda kj, st: tiles(kj, *st, False), zero)
        dq = tiles(i, *state, True)[2]
        dq_out = dq[0:tq]
        for hh in range(1, HEADS_PER_BLOCK):
            dq_out = jnp.where(in_heads[hh], dq[hh * tq:(hh + 1) * tq], dq_out)
        dq_ref[...] = (dq_out * ATTN_SCALE).astype(BF16)

        @pl.when(i == nq - 1)
        def _():
            for kj in range(nq):
                dk_ref[kj * tq:(kj + 1) * tq, :] = dkt_ref[kj].T.astype(BF16)
                dv_ref[kj * tq:(kj + 1) * tq, :] = dvt_ref[kj].T.astype(BF16)

    return pl.pallas_call(
        body, name="attn_bwd",
        grid=(N_HEAD_BLOCKS, nq),
        in_specs=[pl.BlockSpec((1, tq, LANES), lambda p, i: (U_Q, i, p)),
                  pl.BlockSpec((1, t, LANES), lambda p, i: (U_K, 0, p)),
                  pl.BlockSpec((1, t, LANES), lambda p, i: (U_V, 0, p)),
                  pl.BlockSpec((tq, LANES), lambda p, i: (i, p)),
                  pl.BlockSpec((1, tq, LANES), lambda p, i: (p, i, 0)),
                  pl.BlockSpec((1, 1, 8, LANES), lambda p, i: (p, i, 0, 0))],
        out_specs=[pl.BlockSpec((tq, LANES), lambda p, i: (i, p)),
                   pl.BlockSpec((t, LANES), lambda p, i: (0, p)),
                   pl.BlockSpec((t, LANES), lambda p, i: (0, p))],
        out_shape=[S((t, UNIT), BF16)] * 3,
        scratch_shapes=[pltpu.VMEM((nq, LANES, tq), F32), pltpu.VMEM((nq, LANES, tq), F32)],
        compiler_params=_params(2),
    )(proj, proj, proj, d_o, bsum, kfirst)


N_MIX_VIEWS = 11
N_MIX_VIEWS_NO_CONV = 7


def _mix_in_specs(tb, conv_inputs=True):
    hb = tb // HALO

    def unit(u):
        return pl.BlockSpec((1, tb, UNIT), lambda i: (u, i, 0))

    def halo(u):
        return pl.BlockSpec((1, HALO, UNIT), lambda i: (u, jnp.maximum(i * hb - 1, 0), 0))

    conv = [unit(U_CA), halo(U_CA), unit(U_CB), halo(U_CB)] if conv_inputs else []
    return [unit(U_P), halo(U_P), unit(U_GP)] + conv + [
        unit(U_GC), unit(U_GA), pl.BlockSpec((3, tb, UNIT), lambda i: (3, i, 0)),
        pl.BlockSpec((3, tb, UNIT), lambda i: (4, i, 0))]


def _mix_weight_specs():
    return [_const_spec((4, GROUP, GROUP)), _const_spec((1, UNIT)), _const_spec((1, UNIT)),
            _const_spec((UNIT, D_MODEL)), _const_spec((CONV_K_PAD, UNIT)), _const_spec((1, UNIT)),
            _const_spec((1, UNIT)), _const_spec((1, UNIT)), _const_spec((UNIT, D_MODEL)),
            _const_spec((UNIT, D_MODEL)), _const_spec((D_MODEL, D_MODEL)), _const_spec((1, D_MODEL))]


def _shifted_reader(buf, rbuf, tb):
    if rbuf is None:
        return lambda s: buf[s:s + tb, :]
    length = tb + HALO - 8
    for b in range(1, 8):
        rbuf[b, :, :] = buf[b:b + length, :]

    def read(s):
        a, b = divmod(s, 8)
        return buf[s:s + tb, :] if b == 0 else rbuf[b, 8 * a:8 * a + tb, :]

    return read


def _mix_forward(i, tb, proj_refs, o_ref, w_refs, pbuf, ubuf=None, rbuf=None, cv_ref=None):
    if cv_ref is None:
        p_ref, ph_ref, gp_ref, ca_ref, cah_ref, cb_ref, cbh_ref, gc_ref, ga_ref, gm0_ref, gm1_ref = proj_refs
    else:
        p_ref, ph_ref, gp_ref, gc_ref, ga_ref, gm0_ref, gm1_ref = proj_refs
    (poolw_ref, poolb_ref, pscale_ref, wpo_ref, convw_ref, convb_ref, lng_ref, lnb_ref,
     wco_ref, wao_ref, wo_ref, gpost_ref) = w_refs
    first = i == 0
    r = {}

    pbuf[0:HALO, :] = jnp.where(first, 0.0, ph_ref[0])
    pbuf[HALO:HALO + tb, :] = p_ref[0]
    tpos = i * tb + lax.broadcasted_iota(jnp.int32, (tb, 1), 0)
    d_parts, y_parts = [], []
    for g, win in enumerate(POOL_WINDOWS):
        cs = slice(g * GROUP, (g + 1) * GROUP)
        cur = pbuf[HALO:HALO + tb, cs]
        s = cur
        for j in range(1, win):
            s = s + pbuf[HALO - j:HALO - j + tb, cs]
        cnt = jnp.minimum(tpos + 1, win).astype(F32)
        d_g = s / cnt - cur
        d_parts.append(d_g)
        y_parts.append(_dot(d_g.astype(BF16), poolw_ref[g]))
    r["d"] = d_parts
    y = jnp.concatenate(y_parts, axis=1) + poolb_ref[...]
    r["y"] = y
    mp = y * pscale_ref[...]
    gp = gp_ref[0]
    sgp = _sigmoid(gp)
    r["mp"], r["gp"], r["sgp"] = mp, gp, sgp
    ua = mp * (gp * sgp)

    if cv_ref is None:
        ah, bh = cah_ref[0], cbh_ref[0]
        ubuf[0:HALO, :] = jnp.where(first, 0.0, ah * _sigmoid(bh))
        ubuf[HALO:HALO + tb, :] = ca_ref[0] * _sigmoid(cb_ref[0])
        cv = jnp.zeros((tb, UNIT), F32) + convb_ref[...]
        off = HALO - (CONV_K - 1)
        u_at = _shifted_reader(ubuf, rbuf, tb)
        for k in range(CONV_K):
            cv = cv + convw_ref[k:k + 1, :] * u_at(off + k)
    else:
        cv = cv_ref[...]
    r["cv"] = cv
    mu = jnp.mean(cv, axis=-1, keepdims=True)
    cc = cv - mu
    rs = lax.rsqrt(jnp.mean(cc * cc, axis=-1, keepdims=True) + LN_EPS)
    nrm = cc * rs
    ln = nrm * lng_ref[...] + lnb_ref[...]
    sln = _sigmoid(ln)
    sc = ln * sln
    gc = gc_ref[0]
    sgc = _sigmoid(gc)
    r["rs"], r["nrm"], r["ln"], r["sln"], r["sc"], r["gc"], r["sgc"] = rs, nrm, ln, sln, sc, gc, sgc
    ub = sc * (gc * sgc)

    o = o_ref[...]
    ga = ga_ref[0]
    sga = _sigmoid(ga)
    r["o"], r["ga"], r["sga"] = o, ga, sga
    uc = o * (ga * sga)

    r["ua"], r["ub"], r["uc"] = ua.astype(BF16), ub.astype(BF16), uc.astype(BF16)
    ya = _dot(r["ua"], wpo_ref[...])
    yb = _dot(r["ub"], wco_ref[...])
    yc = _dot(r["uc"], wao_ref[...])
    g0 = _sigmoid(jnp.concatenate([gm0_ref[0], gm0_ref[1]], axis=1))
    g1 = _sigmoid(jnp.concatenate([gm0_ref[2], gm1_ref[0]], axis=1))
    g2 = _sigmoid(jnp.concatenate([gm1_ref[1], gm1_ref[2]], axis=1))
    r["ya"], r["yb"], r["yc"], r["g0"], r["g1"], r["g2"] = ya, yb, yc, g0, g1, g2
    m = (g0 * ya + g1 * yb + g2 * yc).astype(BF16)
    r["m"] = m
    out = _dot(m, wo_ref[...])
    r2 = lax.rsqrt(jnp.mean(out * out, axis=-1, keepdims=True) + RMS_EPS)
    r["n2"], r["r2"] = out * r2, r2
    return r


def _mix_fwd(proj, o, x, weights):
    t = x.shape[0]
    tb = min(256, t)
    nv = N_MIX_VIEWS

    def body(*refs):
        proj_refs, o_ref, x_ref = refs[0:nv], refs[nv], refs[nv + 1]
        w_refs = refs[nv + 2:nv + 14]
        xn_ref, cv_ref, pbuf, ubuf, rbuf = refs[nv + 14:nv + 19]
        r = _mix_forward(pl.program_id(0), tb, proj_refs, o_ref, w_refs, pbuf, ubuf, rbuf)
        xn_ref[...] = x_ref[...] + r["n2"] * w_refs[11][...]
        cv_ref[...] = r["cv"]

    return pl.pallas_call(
        body, name="mix_fwd",
        grid=(t // tb,),
        in_specs=_mix_in_specs(tb) + [pl.BlockSpec((tb, UNIT), lambda i: (i, 0)),
                                      pl.BlockSpec((tb, D_MODEL), lambda i: (i, 0))] + _mix_weight_specs(),
        out_specs=[pl.BlockSpec((tb, D_MODEL), lambda i: (i, 0)), pl.BlockSpec((tb, UNIT), lambda i: (i, 0))],
        out_shape=[S((t, D_MODEL), F32), S((t, UNIT), F32)],
        scratch_shapes=[pltpu.VMEM((HALO + tb, UNIT), F32), pltpu.VMEM((HALO + tb, UNIT), F32),
                        pltpu.VMEM((8, HALO + tb - 8, UNIT), F32)],
        compiler_params=_params(1),
    )(*([proj] * nv), o, x, *weights)


def _mix_bwd(proj, o, cv, x, dxn, weights):
    t = x.shape[0]
    tb = min(256, t)
    nt = t // tb

    nv = N_MIX_VIEWS_NO_CONV

    def body(*refs):
        proj_refs, o_ref, cv_ref, x_ref, dxn_ref = refs[0:nv], refs[nv], refs[nv + 1], refs[nv + 2], refs[nv + 3]
        w_refs = refs[nv + 4:nv + 16]
        (dg_ref, dd_ref, dcv_ref, do_ref, dwo_hbm, dwp_hbm, dwc_hbm, dwa_hbm, dpw_ref,
         dvec_ref) = refs[nv + 16:nv + 26]
        pbuf, dwo_ref, dwp_ref, dwc_ref, dwa_ref = refs[nv + 26:nv + 31]
        (poolw_ref, _, pscale_ref, wpo_ref, _, _, lng_ref, _, wco_ref, wao_ref, wo_ref, gpost_ref) = w_refs
        i = pl.program_id(0)

        @pl.when(i == 0)
        def _():
            for ref in (dwo_ref, dwp_ref, dwc_ref, dwa_ref, dpw_ref, dvec_ref):
                ref[...] = jnp.zeros_like(ref)

        r = _mix_forward(i, tb, proj_refs, o_ref, w_refs, pbuf, cv_ref=cv_ref)

        def colsum(v):
            return jnp.sum(v, axis=0, keepdims=True)

        dxn = dxn_ref[...]
        n2 = r["n2"]
        dvec_ref[0:1, :] += colsum(dxn * n2)
        dn2 = dxn * gpost_ref[...]
        dout = (r["r2"] * (dn2 - n2 * jnp.mean(dn2 * n2, axis=-1, keepdims=True))).astype(BF16)
        dm = _dot_nt(dout, wo_ref[...])
        dwo_ref[...] += _dot_tn(r["m"], dout)

        g0, g1, g2 = r["g0"], r["g1"], r["g2"]
        dgm = [dm * r["ya"] * g0 * (1.0 - g0), dm * r["yb"] * g1 * (1.0 - g1), dm * r["yc"] * g2 * (1.0 - g2)]
        for bidx in range(N_BRANCH):
            dg_ref[3 + 2 * bidx] = dgm[bidx][:, 0:UNIT].astype(BF16)
            dg_ref[4 + 2 * bidx] = dgm[bidx][:, UNIT:2 * UNIT].astype(BF16)
        dya = (dm * g0).astype(BF16)
        dyb = (dm * g1).astype(BF16)
        dyc = (dm * g2).astype(BF16)
        dua = _dot_nt(dya, wpo_ref[...])
        dub = _dot_nt(dyb, wco_ref[...])
        duc = _dot_nt(dyc, wao_ref[...])
        dwp_ref[...] += _dot_tn(r["ua"], dya)
        dwc_ref[...] += _dot_tn(r["ub"], dyb)
        dwa_ref[...] += _dot_tn(r["uc"], dyc)

        gp, sgp = r["gp"], r["sgp"]
        dmp = dua * (gp * sgp)
        dg_ref[0] = (dua * r["mp"] * _dsilu(gp, sgp)).astype(BF16)
        dvec_ref[2:3, 0:UNIT] += colsum(dmp * r["y"])
        dy = dmp * pscale_ref[...]
        dvec_ref[1:2, 0:UNIT] += colsum(dy)
        dd_parts = []
        for g in range(len(POOL_WINDOWS)):
            dy_g = dy[:, g * GROUP:(g + 1) * GROUP].astype(BF16)
            dd_parts.append(_dot_nt(dy_g, poolw_ref[g]))
            dpw_ref[g] += _dot_tn(r["d"][g].astype(BF16), dy_g)
        dd_ref[...] = jnp.concatenate(dd_parts, axis=1)

        gc, sgc = r["gc"], r["sgc"]
        dsc = dub * (gc * sgc)
        dg_ref[1] = (dub * r["sc"] * _dsilu(gc, sgc)).astype(BF16)
        dln = dsc * _dsilu(r["ln"], r["sln"])
        nrm = r["nrm"]
        dvec_ref[4:5, 0:UNIT] += colsum(dln * nrm)
        dvec_ref[5:6, 0:UNIT] += colsum(dln)
        dnrm = dln * lng_ref[...]
        dcv = r["rs"] * (dnrm - jnp.mean(dnrm, axis=-1, keepdims=True)
                         - nrm * jnp.mean(dnrm * nrm, axis=-1, keepdims=True))
        dvec_ref[3:4, 0:UNIT] += colsum(dcv)
        dcv_ref[...] = dcv

        ga, sga = r["ga"], r["sga"]
        do_ref[...] = (duc * (ga * sga)).astype(BF16)
        dg_ref[2] = (duc * r["o"] * _dsilu(ga, sga)).astype(BF16)

        @pl.when(i == nt - 1)
        def _():
            for acc, hbm in ((dwo_ref, dwo_hbm), (dwp_ref, dwp_hbm), (dwc_ref, dwc_hbm), (dwa_ref, dwa_hbm)):
                pltpu.sync_copy(acc, hbm)

    def acc_spec(shape):
        n = len(shape)
        return pl.BlockSpec(shape, lambda i: (0,) * n)

    tok = lambda w: pl.BlockSpec((tb, w), lambda i: (i, 0))
    any_spec = pl.BlockSpec(memory_space=pl.ANY)
    w_shapes = [(D_MODEL, D_MODEL), (UNIT, D_MODEL), (UNIT, D_MODEL), (UNIT, D_MODEL)]
    return pl.pallas_call(
        body, name="mix_bwd",
        grid=(nt,),
        in_specs=_mix_in_specs(tb, conv_inputs=False) + [tok(UNIT), tok(UNIT), tok(D_MODEL), tok(D_MODEL)]
        + _mix_weight_specs(),
        out_specs=[pl.BlockSpec((9, tb, UNIT), lambda i: (0, i, 0)), tok(UNIT), tok(UNIT), tok(UNIT)]
        + [any_spec] * 4 + [acc_spec((4, GROUP, GROUP)), acc_spec((8, D_MODEL))],
        out_shape=[S((9, t, UNIT), BF16), S((t, UNIT), F32), S((t, UNIT), F32), S((t, UNIT), BF16)]
        + [S(s, F32) for s in w_shapes] + [S((4, GROUP, GROUP), F32), S((8, D_MODEL), F32)],
        scratch_shapes=[pltpu.VMEM((HALO + tb, UNIT), F32)] + [pltpu.VMEM(s, F32) for s in w_shapes],
        compiler_params=_params(1),
    )(*([proj] * nv), o, cv, x, dxn, *weights)


def _halo_bwd(proj, dcv, dd, conv_w):
    t = dcv.shape[0]
    tb = min(256, t)
    hb = tb // HALO
    n_halo_blocks = t // HALO
    nt = t // tb

    def body(ca_ref, cah_ref, cb_ref, cbh_ref, dcv_ref, dcvh_ref, dd_ref, ddh_ref, cw_ref,
             dpre_ref, dcw_ref, ubuf, gbuf, nbuf, ru_buf, rg_buf, acc_ref):
        i = pl.program_id(0)
        first = i == 0
        last = i == nt - 1

        @pl.when(first)
        def _():
            acc_ref[...] = jnp.zeros_like(acc_ref)

        a, b = ca_ref[0], cb_ref[0]
        sb = _sigmoid(b)
        ubuf[0:HALO, :] = jnp.where(first, 0.0, cah_ref[0] * _sigmoid(cbh_ref[0]))
        ubuf[HALO:HALO + tb, :] = a * sb
        dcv_main = dcv_ref[...]
        gbuf[0:tb, :] = dcv_main
        gbuf[tb:tb + HALO, :] = jnp.where(last, 0.0, dcvh_ref[...])

        off = HALO - (CONV_K - 1)
        u_at = _shifted_reader(ubuf, ru_buf, tb)
        g_at = _shifted_reader(gbuf, rg_buf, tb)
        du = jnp.zeros((tb, UNIT), F32)
        for k in range(CONV_K):
            du = du + cw_ref[k:k + 1, :] * g_at(CONV_K - 1 - k)
            acc_ref[k] += jnp.sum((dcv_main * u_at(off + k)).reshape(tb // 8, 8, UNIT), axis=0)
        dpre_ref[1] = (du * sb).astype(BF16)
        dpre_ref[2] = (du * a * sb * (1.0 - sb)).astype(BF16)

        @pl.when(last)
        def _():
            dcw_ref[...] = jnp.sum(acc_ref[...], axis=1)

        tpos = i * tb + lax.broadcasted_iota(jnp.int32, (tb + HALO, 1), 0)
        dd_main = dd_ref[...]
        dd_ext = jnp.concatenate([dd_main, jnp.where(last, 0.0, ddh_ref[...])], axis=0)
        dp_parts = []
        for g, win in enumerate(POOL_WINDOWS):
            cs = slice(g * GROUP, (g + 1) * GROUP)
            cnt = jnp.minimum(tpos + 1, win).astype(F32)
            nbuf[:, cs] = dd_ext[:, cs] / cnt
        for g, win in enumerate(POOL_WINDOWS):
            cs = slice(g * GROUP, (g + 1) * GROUP)
            s = nbuf[0:tb, cs]
            for j in range(1, win):
                s = s + nbuf[j:j + tb, cs]
            dp_parts.append(s - dd_main[:, cs])
        dpre_ref[0] = jnp.concatenate(dp_parts, axis=1).astype(BF16)

    def unit(u):
        return pl.BlockSpec((1, tb, UNIT), lambda i: (u, i, 0))

    def past(u):
        return pl.BlockSpec((1, HALO, UNIT), lambda i: (u, jnp.maximum(i * hb - 1, 0), 0))

    tok = pl.BlockSpec((tb, UNIT), lambda i: (i, 0))
    future = pl.BlockSpec((HALO, UNIT), lambda i: (jnp.minimum((i + 1) * hb, n_halo_blocks - 1), 0))
    return pl.pallas_call(
        body, name="halo_bwd",
        grid=(nt,),
        in_specs=[unit(U_CA), past(U_CA), unit(U_CB), past(U_CB), tok, future, tok, future,
                  _const_spec((CONV_K_PAD, UNIT))],
        out_specs=[pl.BlockSpec((3, tb, UNIT), lambda i: (0, i, 0)),
                   pl.BlockSpec((CONV_K_PAD, UNIT), lambda i: (0, 0))],
        out_shape=[S((3, t, UNIT), BF16), S((CONV_K_PAD, UNIT), F32)],
        scratch_shapes=[pltpu.VMEM((HALO + tb, UNIT), F32)] * 3
        + [pltpu.VMEM((8, HALO + tb - 8, UNIT), F32)] * 2 + [pltpu.VMEM((CONV_K_PAD, 8, UNIT), F32)],
        compiler_params=_params(1),
    )(proj, proj, proj, proj, dcv, dcv, dd, dd, conv_w)


def _loss_head(y, target):
    t = y.shape[0]
    tb = min(512, t)

    def body(y_ref, t_ref, loss_ref, dy_ref):
        @pl.when(pl.program_id(0) == 0)
        def _():
            loss_ref[...] = jnp.zeros_like(loss_ref)

        err = y_ref[...] - t_ref[...]
        dy_ref[...] = err * (1.0 / D_MODEL)
        part = 0.5 * jnp.sum(jnp.mean(err * err, axis=-1, keepdims=True), axis=0, keepdims=True)
        r8 = lax.broadcasted_iota(jnp.int32, (8, LANES), 0)
        c8 = lax.broadcasted_iota(jnp.int32, (8, LANES), 1)
        loss_ref[...] += jnp.where((r8 == 0) & (c8 == 0), part, 0.0)

    return pl.pallas_call(
        body, name="loss_head",
        grid=(t // tb,),
        in_specs=[pl.BlockSpec((tb, D_MODEL), lambda i: (i, 0))] * 2,
        out_specs=[pl.BlockSpec((8, LANES), lambda i: (0, 0)), pl.BlockSpec((tb, D_MODEL), lambda i: (i, 0))],
        out_shape=[S((8, LANES), F32), S((t, D_MODEL), F32)],
        compiler_params=_params(1),
    )(y, target)


def _adamw(name, parts, w, m, v):
    n, rows, cols = parts.shape
    rb = rows
    while rb * cols * 4 * (n + 7) * 2 > 24 * 1024 * 1024 and rb % 16 == 0:
        rb //= 2

    def body(p_ref, w_ref, m_ref, v_ref, g_ref, d_ref, nm_ref, nv_ref):
        g = p_ref[0]
        for k in range(1, n):
            g = g + p_ref[k]
        nm = ADAM_B1 * m_ref[...] + (1.0 - ADAM_B1) * g
        nv = ADAM_B2 * v_ref[...] + (1.0 - ADAM_B2) * (g * g)
        m_hat = nm / (1.0 - ADAM_B1 ** ADAM_STEP)
        v_hat = nv / (1.0 - ADAM_B2 ** ADAM_STEP)
        g_ref[...] = g
        d_ref[...] = -ADAM_LR * (m_hat / (jnp.sqrt(v_hat) + ADAM_EPS) + ADAM_WD * w_ref[...])
        nm_ref[...] = nm
        nv_ref[...] = nv

    blk = pl.BlockSpec((rb, cols), lambda i: (i, 0))
    return pl.pallas_call(
        body, name=name,
        grid=(rows // rb,),
        in_specs=[pl.BlockSpec((n, rb, cols), lambda i: (0, i, 0)), blk, blk, blk],
        out_specs=[blk] * 4,
        out_shape=[S((rows, cols), F32)] * 4,
        compiler_params=_params(1),
    )(parts, w, m, v)


def _axis_slice(ref, axis, block, size):
    idx = [slice(None)] * len(ref.shape)
    idx[axis] = pl.ds(pl.multiple_of(block * size, size), size)
    return ref.at[tuple(idx)]


def _axis_index(ref, axis, index):
    return ref.at[tuple([slice(None)] * axis + [index])]


COPY_PIECE_BYTES = 2 * 1024 * 1024


def _row_pieces(shape):
    rows = shape[-2]
    size = 4 * rows * shape[-1]
    for d in shape[:-2]:
        size *= d
    n = 1
    while size // n > COPY_PIECE_BYTES and rows % (2 * n * 8) == 0:
        n *= 2
    return [(k * (rows // n), rows // n) for k in range(n)]


def _rows(ref, r0, rows):
    idx = [slice(None)] * len(ref.shape)
    idx[-2] = pl.ds(r0, rows)
    return ref.at[tuple(idx)]


def _staged_local_copies(pairs, bufs, in_sems, out_sems):
    ins = [pltpu.make_async_copy(src, buf, in_sems.at[k]) for k, ((src, _), buf) in enumerate(zip(pairs, bufs))]
    for cp in ins:
        cp.start()
    for cp in ins:
        cp.wait()
    outs = [pltpu.make_async_copy(buf, dst, out_sems.at[k]) for k, ((_, dst), buf) in enumerate(zip(pairs, bufs))]
    for cp in outs:
        cp.start()
    return outs


def _run_copies(local, sends, recvs):
    for cp in local + sends:
        cp.start()
    for cp in recvs:
        cp.wait_recv()
    for cp in sends:
        cp.wait_send()
    for cp in local:
        cp.wait()


def _gather_weights(shards, axes):
    n = len(shards)
    fulls = [S(tuple(4 * d if k == ax else d for k, d in enumerate(s.shape)), s.dtype) for s, ax in zip(shards, axes)]

    def body(*refs):
        src, dst = refs[0:n], refs[n:2 * n]
        send_sems, recv_sems, pass_send_sems, pass_recv_sems, in_sems, out_sems = refs[2 * n:2 * n + 6]
        bufs = refs[2 * n + 6:3 * n + 6]
        x, y, c = lax.axis_index("x"), lax.axis_index("y"), lax.axis_index("c")
        chips = [(1 - x, y), (x, 1 - y), (1 - x, 1 - y)]

        def place(a, chip, layer=None):
            ref = _axis_slice(dst[a], axes[a], 2 * chip[0] + chip[1], src[a].shape[axes[a]])
            return ref if layer is None else ref.at[pl.ds(layer, 1)]

        def over_ici(k, chip_of_block, to):
            a, j, r0, rows = pieces[k]
            return pltpu.make_async_remote_copy(
                src_ref=_rows(src[a].at[pl.ds(c, 1)], r0, rows), dst_ref=_rows(place(a, chip_of_block, c), r0, rows),
                send_sem=send_sems.at[k], recv_sem=recv_sems.at[k], device_id=(to[0], to[1], c), device_id_type=MESH)

        def to_sibling(k, chip_of_block, layer):
            a, j, r0, rows = pieces[k]
            block = _rows(place(a, chip_of_block, layer), r0, rows)
            return pltpu.make_async_remote_copy(
                src_ref=block, dst_ref=block, send_sem=pass_send_sems.at[k], recv_sem=pass_recv_sems.at[k],
                device_id=(x, y, 1 - c), device_id_type=MESH)

        first = [over_ici(k, (x, y), chips[j]) for k, (a, j, r0, rows) in enumerate(pieces)]
        for cp in first:
            cp.start()
        local = _staged_local_copies([(src[a], place(a, (x, y))) for a in range(n)], bufs, in_sems, out_sems)
        passed = []
        for k, (a, j, r0, rows) in enumerate(pieces):
            over_ici(k, chips[j], chips[j]).wait_recv()
            passed.append(to_sibling(k, chips[j], c))
            passed[-1].start()
        for k, (a, j, r0, rows) in enumerate(pieces):
            to_sibling(k, chips[j], 1 - c).wait_recv()
        for cp in first + passed:
            cp.wait_send()
        for cp in local:
            cp.wait()

    per_array = [_row_pieces((1,) + shards[a].shape[1:]) for a in range(n)]
    pieces = [(a, j, r0, rows) for a in range(n) for r0, rows in per_array[a] for j in range(3)]
    any_spec = pl.BlockSpec(memory_space=pl.ANY)
    return pl.pallas_call(
        body, name="gather_weights",
        in_specs=[any_spec] * n, out_specs=[any_spec] * n, out_shape=fulls,
        scratch_shapes=[pltpu.SemaphoreType.DMA((len(pieces),))] * 4 + [pltpu.SemaphoreType.DMA((n,))] * 2
        + [pltpu.VMEM(s.shape, s.dtype) for s in shards],
        compiler_params=_params(0),
    )(*shards)


def _swap_layers(layers):
    n = len(layers)
    pieces = [(a, r0, rows) for a in range(n) for r0, rows in _row_pieces(layers[a][0].shape)]

    def body(*refs):
        src = [refs[2 * a:2 * a + 2] for a in range(n)]
        dst = refs[2 * n:3 * n]
        send_sems, recv_sems = refs[3 * n:3 * n + 2]
        x, y, c = lax.axis_index("x"), lax.axis_index("y"), lax.axis_index("c")

        def remote(k, layer):
            a, r0, rows = pieces[k]
            return pltpu.make_async_remote_copy(
                src_ref=_rows(src[a][layer], r0, rows), dst_ref=_rows(dst[a], r0, rows),
                send_sem=send_sems.at[k], recv_sem=recv_sems.at[k], device_id=(x, y, 1 - c), device_id_type=MESH)

        for layer in range(2):
            @pl.when(c == 1 - layer)
            def _(layer=layer):
                for k in range(len(pieces)):
                    remote(k, layer).start()

        for k in range(len(pieces)):
            remote(k, 0).wait_recv()
        for k in range(len(pieces)):
            remote(k, 0).wait_send()

    any_spec = pl.BlockSpec(memory_space=pl.ANY)
    return pl.pallas_call(
        body, name="swap_layers",
        in_specs=[any_spec] * (2 * n), out_specs=[any_spec] * n,
        out_shape=[S(layers[a][0].shape, layers[a][0].dtype) for a in range(n)],
        scratch_shapes=[pltpu.SemaphoreType.DMA((len(pieces),))] * 2,
    )(*[arr for pair in layers for arr in pair])


def _add_own_layer(name, mine, theirs, layer):
    rows, cols = theirs.shape
    rb = rows
    while rb * cols * 4 * 4 * 2 > 24 * 1024 * 1024 and rb % 32 == 0:
        rb //= 2

    def body(layer_ref, m0_ref, m1_ref, t_ref, o_ref):
        own = jnp.where(layer_ref[0] == 0, m0_ref[...], m1_ref[...])
        o_ref[0] = (own + t_ref[...]).astype(BF16)

    def own_spec(which):
        return pl.BlockSpec((rb, cols), lambda i, l: (jnp.where(l[0] == which, i, 0), 0))

    return pl.pallas_call(
        body, name=name,
        grid_spec=pltpu.PrefetchScalarGridSpec(
            num_scalar_prefetch=1, grid=(rows // rb,),
            in_specs=[own_spec(0), own_spec(1), pl.BlockSpec((rb, cols), lambda i, l: (i, 0))],
            out_specs=pl.BlockSpec((1, rb, cols), lambda i, l: (0, i, 0))),
        out_shape=S((1, rows, cols), BF16),
        compiler_params=_params(1),
    )(layer, *mine, theirs)


def _send_to_owners(chip_halves, shard_axes, packed):
    n = len(chip_halves)

    def owned_shape(a):
        return tuple(d // 4 if k == shard_axes[a] else d for k, d in enumerate(chip_halves[a].shape))

    flips = [(dx, dy, dc) for dx in (0, 1) for dy in (0, 1) for dc in (0, 1)][1:]

    def body(*refs):
        src, psrc, dst, pdst = refs[0:n], refs[n], refs[n + 1:2 * n + 1], refs[2 * n + 1]
        send_sems, recv_sems, local_sems, psend_sems, precv_sems = refs[2 * n + 2:2 * n + 7]
        x, y, c = lax.axis_index("x"), lax.axis_index("y"), lax.axis_index("c")
        my_chip = 2 * x + y
        chips = [(1 - x, y), (x, 1 - y), (1 - x, 1 - y)]

        def owned(a, chip):
            return _axis_slice(src[a], shard_axes[a], 2 * chip[0] + chip[1], src[a].shape[shard_axes[a]] // 4)

        def remote(a, j, to, from_chip):
            return pltpu.make_async_remote_copy(
                src_ref=owned(a, to), dst_ref=dst[a].at[from_chip], send_sem=send_sems.at[a, j],
                recv_sem=recv_sems.at[a, j], device_id=(to[0], to[1], c), device_id_type=MESH)

        def peer(f):
            return (1 - x if f[0] else x, 1 - y if f[1] else y, 1 - c if f[2] else c)

        def premote(j, to, from_dev):
            return pltpu.make_async_remote_copy(
                src_ref=psrc, dst_ref=pdst.at[from_dev], send_sem=psend_sems.at[j], recv_sem=precv_sems.at[j],
                device_id=to, device_id_type=MESH)

        local = [pltpu.make_async_copy(_rows(owned(a, (x, y)), r0, rows), _rows(dst[a].at[my_chip], r0, rows),
                                       local_sems.at[k]) for k, (a, r0, rows) in enumerate(own_pieces)]
        local.append(pltpu.make_async_copy(psrc, pdst.at[4 * x + 2 * y + c], local_sems.at[len(own_pieces)]))
        sends = [remote(a, j, chip, my_chip) for a in range(n) for j, chip in enumerate(chips)]
        sends += [premote(j, peer(f), 4 * x + 2 * y + c) for j, f in enumerate(flips)]
        recvs = [remote(a, j, (x, y), 2 * chip[0] + chip[1]) for a in range(n) for j, chip in enumerate(chips)]
        for j, f in enumerate(flips):
            px, py, pc = peer(f)
            recvs.append(premote(j, (x, y, c), 4 * px + 2 * py + pc))
        _run_copies(local, sends, recvs)

    own_pieces = [(a, r0, rows) for a in range(n) for r0, rows in _row_pieces(owned_shape(a))]
    any_spec = pl.BlockSpec(memory_space=pl.ANY)
    return pl.pallas_call(
        body, name="send_to_owners",
        in_specs=[any_spec] * (n + 1), out_specs=[any_spec] * (n + 1),
        out_shape=[S((4,) + owned_shape(a), chip_halves[a].dtype) for a in range(n)]
        + [S((N_DEV,) + packed.shape, packed.dtype)],
        scratch_shapes=[pltpu.SemaphoreType.DMA((n, 3)), pltpu.SemaphoreType.DMA((n, 3)),
                        pltpu.SemaphoreType.DMA((len(own_pieces) + 1,)), pltpu.SemaphoreType.DMA((7,)),
                        pltpu.SemaphoreType.DMA((7,))],
    )(*chip_halves, packed)


def _join_halves(reduced, half_axes):
    n = len(reduced)

    def joined_shape(a):
        s, ax = reduced[a].shape, half_axes[a]
        return s[:ax] + (2,) + s[ax:]

    pieces = [(a, r0, rows) for a in range(n) for r0, rows in _row_pieces(reduced[a].shape)]

    def body(*refs):
        src, dst = refs[0:n], refs[n:2 * n]
        send_sems, recv_sems, in_sems, out_sems = refs[2 * n:2 * n + 4]
        bufs = refs[2 * n + 4:3 * n + 4]
        x, y, c = lax.axis_index("x"), lax.axis_index("y"), lax.axis_index("c")

        def remote(k, h):
            a, r0, rows = pieces[k]
            return pltpu.make_async_remote_copy(
                src_ref=_rows(src[a], r0, rows), dst_ref=_rows(_axis_index(dst[a], half_axes[a], h), r0, rows),
                send_sem=send_sems.at[k], recv_sem=recv_sems.at[k], device_id=(x, y, 1 - c), device_id_type=MESH)

        sends = [remote(k, c) for k in range(len(pieces))]
        for cp in sends:
            cp.start()
        own = [(src[a], _axis_index(dst[a], half_axes[a], c)) for a in range(n)]
        local = _staged_local_copies(own, bufs, in_sems, out_sems)
        for k in range(len(pieces)):
            remote(k, 1 - c).wait_recv()
        for cp in sends:
            cp.wait_send()
        for cp in local:
            cp.wait()

    any_spec = pl.BlockSpec(memory_space=pl.ANY)
    return pl.pallas_call(
        body, name="join_halves",
        in_specs=[any_spec] * n, out_specs=[any_spec] * n,
        out_shape=[S(joined_shape(a), reduced[a].dtype) for a in range(n)],
        scratch_shapes=[pltpu.SemaphoreType.DMA((len(pieces),))] * 2 + [pltpu.SemaphoreType.DMA((n,))] * 2
        + [pltpu.VMEM(r.shape, r.dtype) for r in reduced],
        compiler_params=_params(0),
    )(*reduced)


def _sum_slots(name, parts):
    n = parts.shape[0]
    shape = parts.shape[1:]
    flat = parts.reshape((n, -1, shape[-1]))
    rows, cols = flat.shape[1:]
    rb = rows
    while rb * cols * 4 * (n + 1) * 2 > 24 * 1024 * 1024 and rb % 32 == 0:
        rb //= 2

    def body(p_ref, o_ref):
        acc = p_ref[0].astype(F32)
        for k in range(1, n):
            acc = acc + p_ref[k].astype(F32)
        o_ref[...] = acc

    out = pl.pallas_call(
        body, name=name,
        grid=(rows // rb,),
        in_specs=[pl.BlockSpec((n, rb, cols), lambda i: (0, i, 0))],
        out_specs=pl.BlockSpec((rb, cols), lambda i: (i, 0)),
        out_shape=S((rows, cols), F32),
        compiler_params=_params(1),
    )(flat)
    return out.reshape(shape)


SMALL_NAMES = ("norm_pre", "pool_w", "pool_b", "pool_scale", "conv_b", "conv_ln_g", "conv_ln_b", "norm_post")
BIG_NAMES = ("w_in", "w_pool_out", "conv_w", "w_conv_out", "w_attn_out", "w_o")
SHARD_AXIS = {"w_in": 2, "w_pool_out": 2, "conv_w": 2, "w_conv_out": 2, "w_attn_out": 2, "w_o": 1}
WEIGHT_ORDER = ("norm_pre", "w_in", "pool_w", "pool_b", "pool_scale", "w_pool_out", "conv_w", "conv_b",
                "conv_ln_g", "conv_ln_b", "w_conv_out", "w_attn_out", "w_o", "norm_post")


def _pack_small(parts):
    return jnp.concatenate([parts[n].reshape(-1, LANES) for n in SMALL_NAMES], axis=0)


def _unpack_small(packed, shapes):
    out, r0 = {}, 0
    for n in SMALL_NAMES:
        size = 1
        for d in shapes[n]:
            size *= d
        rows = size // LANES
        out[n] = packed[r0:r0 + rows].reshape(shapes[n])
        r0 += rows
    return out


def _layer_weights(full, small, l):
    row = lambda a: a[l][None, :]
    return (small["pool_w"][l].astype(BF16), small["pool_b"][l].reshape(1, UNIT), row(small["pool_scale"]),
            full["w_pool_out"][l], full["conv_w"][l], row(small["conv_b"]), row(small["conv_ln_g"]),
            row(small["conv_ln_b"]), full["w_conv_out"][l], full["w_attn_out"][l], full["w_o"][l],
            row(small["norm_post"]))


def _forward_backward(x, target, full, small):
    depth = full["w_in"].shape[0]
    acts = []
    for l in range(depth):
        g_pre = small["norm_pre"][l][None, :]
        proj = _inproj_fwd(x, g_pre, full["w_in"], l)
        o, bsum, kfirst = _attn_fwd(proj)
        weights = _layer_weights(full, small, l)
        x_next, cv = _mix_fwd(proj, o, x, weights)
        acts.append((x, proj, o, cv, bsum, kfirst, weights, g_pre))
        x = x_next
    loss_tile, dx = _loss_head(x, target)

    big = {n: [None] * depth for n in BIG_NAMES}
    sm = {n: [None] * depth for n in SMALL_NAMES}
    for l in reversed(range(depth)):
        x_in, proj, o, cv, bsum, kfirst, weights, g_pre = acts[l]
        dgates, dd, dcv, d_o, dwo, dwp, dwc, dwa, dpw, dvec = _mix_bwd(proj, o, cv, x_in, dx, weights)
        dpre, dcw = _halo_bwd(proj, dcv, dd, full["conv_w"][l])
        dq, dk, dv = _attn_bwd(proj, d_o, bsum, kfirst)
        dproj_parts = (dpre, dgates, dq, dk, dv)
        dx, dg_pre = _inproj_bwd_x(x_in, g_pre, full["w_in"], l, dproj_parts, dx)
        big["w_in"][l] = _inproj_bwd_w(x_in, g_pre, dproj_parts)
        big["w_pool_out"][l], big["w_conv_out"][l], big["w_attn_out"][l], big["w_o"][l] = dwp, dwc, dwa, dwo
        big["conv_w"][l] = dcw
        sm["norm_pre"][l] = dg_pre[0]
        sm["pool_w"][l] = dpw
        sm["norm_post"][l] = dvec[0]
        sm["pool_b"][l] = dvec[1, 0:UNIT].reshape(4, GROUP)
        sm["pool_scale"][l] = dvec[2, 0:UNIT]
        sm["conv_b"][l] = dvec[3, 0:UNIT]
        sm["conv_ln_g"][l] = dvec[4, 0:UNIT]
        sm["conv_ln_b"][l] = dvec[5, 0:UNIT]
    sm = {n: jnp.stack(v) for n, v in sm.items()}
    return loss_tile, dx, big, sm


def kernel(x, norm_pre, w_in, pool_w, pool_b, pool_scale, w_pool_out, conv_w, conv_b, conv_ln_g, conv_ln_b, w_conv_out, w_attn_out, w_o, norm_post, loss_target, m_norm_pre, m_w_in, m_pool_w, m_pool_b, m_pool_scale, m_w_pool_out, m_conv_w, m_conv_b, m_conv_ln_g, m_conv_ln_b, m_w_conv_out, m_w_attn_out, m_w_o, m_norm_post, v_norm_pre, v_w_in, v_pool_w, v_pool_b, v_pool_scale, v_w_pool_out, v_conv_w, v_conv_b, v_conv_ln_g, v_conv_ln_b, v_w_conv_out, v_w_attn_out, v_w_o, v_norm_post):
    w = dict(norm_pre=norm_pre, w_in=w_in, pool_w=pool_w, pool_b=pool_b, pool_scale=pool_scale,
             w_pool_out=w_pool_out, conv_w=conv_w, conv_b=conv_b, conv_ln_g=conv_ln_g, conv_ln_b=conv_ln_b,
             w_conv_out=w_conv_out, w_attn_out=w_attn_out, w_o=w_o, norm_post=norm_post)
    m = dict(norm_pre=m_norm_pre, w_in=m_w_in, pool_w=m_pool_w, pool_b=m_pool_b, pool_scale=m_pool_scale,
             w_pool_out=m_w_pool_out, conv_w=m_conv_w, conv_b=m_conv_b, conv_ln_g=m_conv_ln_g,
             conv_ln_b=m_conv_ln_b, w_conv_out=m_w_conv_out, w_attn_out=m_w_attn_out, w_o=m_w_o,
             norm_post=m_norm_post)
    v = dict(norm_pre=v_norm_pre, w_in=v_w_in, pool_w=v_pool_w, pool_b=v_pool_b, pool_scale=v_pool_scale,
             w_pool_out=v_w_pool_out, conv_w=v_conv_w, conv_b=v_conv_b, conv_ln_g=v_conv_ln_g,
             conv_ln_b=v_conv_ln_b, w_conv_out=v_w_conv_out, w_attn_out=v_w_attn_out, w_o=v_w_o,
             norm_post=v_norm_post)
    pad_taps = lambda a: jnp.pad(a, ((0, 0), (0, CONV_K_PAD - CONV_K), (0, 0)))

    shards = [pad_taps(w[n]) if n == "conv_w" else w[n].astype(BF16) for n in BIG_NAMES]
    full = dict(zip(BIG_NAMES, _gather_weights(shards, [SHARD_AXIS[n] for n in BIG_NAMES])))
    small = {n: w[n] for n in SMALL_NAMES}

    loss_tile, dx, big, sm = _forward_backward(x[0], loss_target[0], full, small)

    packed = jnp.concatenate([_pack_small(sm), loss_tile], axis=0)
    layer_axis = [0] * len(BIG_NAMES)
    theirs = _swap_layers([big[n] for n in BIG_NAMES])
    my_layer = lax.axis_index("c").astype(jnp.int32).reshape(1)
    chip_sums = [_add_own_layer("sum_cores_" + n, big[n], t, my_layer) for n, t in zip(BIG_NAMES, theirs)]
    exchanged = _send_to_owners(chip_sums, [SHARD_AXIS[n] for n in BIG_NAMES], packed)
    reduced = [_sum_slots("sum_chips_" + n, p) for n, p in zip(BIG_NAMES, exchanged[:-1])]
    joined = _join_halves(reduced, layer_axis)
    out = {}
    for n, g in zip(BIG_NAMES, joined):
        wn, mn, vn = (pad_taps(a[n]) if n == "conv_w" else a[n] for a in (w, m, v))
        shape = wn.shape
        flat = lambda a: a.reshape((-1, shape[-1]))
        res = _adamw("adamw_" + n, g.reshape((1, -1, shape[-1])), flat(wn), flat(mn), flat(vn))
        res = [r.reshape(shape) for r in res]
        out[n] = [r[:, :CONV_K] for r in res] if n == "conv_w" else res
    parts = exchanged[-1]
    n_small_rows = parts.shape[1] - 8
    zeros_tile = jnp.zeros((8, LANES), F32)
    packs = [jnp.concatenate([_pack_small(a), zeros_tile], axis=0) for a in (w, m, v)]
    res = _adamw("adamw_small", parts, *packs)
    loss = res[0][n_small_rows, 0]
    shapes = {n: w[n].shape for n in SMALL_NAMES}
    unpacked = [_unpack_small(r[:n_small_rows], shapes) for r in res]
    for n in SMALL_NAMES:
        out[n] = [u[n] for u in unpacked]

    grads = [out[n][0] for n in WEIGHT_ORDER]
    deltas = [out[n][1] for n in WEIGHT_ORDER]
    new_m = [out[n][2] for n in WEIGHT_ORDER]
    new_v = [out[n][3] for n in WEIGHT_ORDER]
    return (loss, dx[None], *grads, *deltas, *new_m, *new_v)
```

```python
import jax
import jax.numpy as jnp
from jax import lax
from jax.experimental import pallas as pl
from jax.experimental.pallas import tpu as pltpu

F32 = jnp.float32
BF16 = jnp.bfloat16

D_MODEL = 1024
UNIT = 512
N_UNITS = 15
IN_WIDTH = UNIT * N_UNITS
N_HEADS = 8
HEAD_DIM = 64
HEADS_PER_BLOCK = 2
N_HEAD_BLOCKS = N_HEADS // HEADS_PER_BLOCK
LANES = 128
CONV_K = 31
CONV_K_PAD = 32
HALO = 32
POOL_WINDOWS = (2, 4, 8, 16)
GROUP = 128
N_BRANCH = 3
RMS_EPS = 1e-6
LN_EPS = 1e-5
ATTN_SCALE = 0.125
EXP_ZERO_BELOW = -104.0

ADAM_LR = 0.001
ADAM_B1 = 0.9
ADAM_B2 = 0.999
ADAM_EPS = 1e-08
ADAM_WD = 0.01
ADAM_STEP = 10

U_P, U_GP, U_CA, U_CB, U_GC, U_Q, U_K, U_V, U_GA, U_GM = 0, 1, 2, 3, 4, 5, 6, 7, 8, 9

V7X_VMEM_LIMIT = 62 * 1024 * 1024
N_DEV = 8
MESH = pl.DeviceIdType.MESH

S = jax.ShapeDtypeStruct


def _params(n_grid):
    return pltpu.CompilerParams(dimension_semantics=("arbitrary",) * n_grid, vmem_limit_bytes=V7X_VMEM_LIMIT)


def _sigmoid(x):
    return 0.5 * jnp.tanh(0.5 * x) + 0.5


def _dsilu(x, s):
    return s * (1.0 + x * (1.0 - s))


def _dot(a, b):
    return jnp.dot(a, b, preferred_element_type=F32)


def _dot_nt(a, b):
    return lax.dot_general(a, b, (((1,), (1,)), ((), ())), preferred_element_type=F32)


def _dot_tn(a, b):
    return lax.dot_general(a, b, (((0,), (0,)), ((), ())), preferred_element_type=F32)


def _split_bf16(x):
    hi = x.astype(BF16)
    lo = (x - hi.astype(F32)).astype(BF16)
    return hi, lo


def _const_spec(shape):
    n = len(shape)
    return pl.BlockSpec(shape, lambda *_: (0,) * n, pipeline_mode=pl.Buffered(1))


def _inproj_fwd(x, g_pre, w_in, layer):
    t = x.shape[0]
    tb = min(1024, t)
    ug = 3
    nb = ug * UNIT

    def body(x_ref, g_ref, w_ref, o_ref, h_ref):
        @pl.when(pl.program_id(1) == 0)
        def _():
            xf = x_ref[...]
            r = lax.rsqrt(jnp.mean(xf * xf, axis=-1, keepdims=True) + RMS_EPS)
            h_ref[...] = (xf * r * g_ref[...]).astype(BF16)

        acc = _dot(h_ref[...], w_ref[0])
        for u in range(ug):
            o_ref[u] = acc[:, u * UNIT:(u + 1) * UNIT]

    return pl.pallas_call(
        body, name="inproj_fwd",
        grid=(t // tb, N_UNITS // ug),
        in_specs=[pl.BlockSpec((tb, D_MODEL), lambda i, j: (i, 0)),
                  pl.BlockSpec((1, D_MODEL), lambda i, j: (0, 0)),
                  pl.BlockSpec((1, D_MODEL, nb), lambda i, j: (layer, 0, j))],
        out_specs=pl.BlockSpec((ug, tb, UNIT), lambda i, j: (j, i, 0)),
        out_shape=S((N_UNITS, t, UNIT), F32),
        scratch_shapes=[pltpu.VMEM((tb, D_MODEL), BF16)],
        compiler_params=_params(2),
    )(x, g_pre, w_in)


def _dproj_specs(tb):
    tok = pl.BlockSpec((tb, UNIT), lambda i: (i, 0))
    return [pl.BlockSpec((3, tb, UNIT), lambda i: (0, i, 0)), pl.BlockSpec((9, tb, UNIT), lambda i: (0, i, 0)),
            tok, tok, tok]


def _dproj_unit(u, dpre_ref, dgates_ref, dq_ref, dk_ref, dv_ref):
    pre = {U_P: 0, U_CA: 1, U_CB: 2}
    gate = {U_GP: 0, U_GC: 1, U_GA: 2}
    if u in pre:
        val = dpre_ref[pre[u]]
    elif u in gate:
        val = dgates_ref[gate[u]]
    elif u >= U_GM:
        val = dgates_ref[3 + u - U_GM]
    else:
        val = {U_Q: dq_ref, U_K: dk_ref, U_V: dv_ref}[u][...]
    return val.astype(BF16)


def _inproj_bwd_x(x, g_pre, w_in, layer, dproj_parts, dxn):
    t = x.shape[0]
    tb = min(256, t)

    def body(x_ref, g_ref, w_ref, dpre_ref, dgates_ref, dq_ref, dk_ref, dv_ref, dxn_ref, dx_ref, dg_ref):
        @pl.when(pl.program_id(0) == 0)
        def _():
            dg_ref[...] = jnp.zeros_like(dg_ref)

        dh = jnp.zeros((tb, D_MODEL), F32)
        for u in range(N_UNITS):
            dh = dh + _dot_nt(_dproj_unit(u, dpre_ref, dgates_ref, dq_ref, dk_ref, dv_ref),
                              w_ref[0, :, u * UNIT:(u + 1) * UNIT])
        xf = x_ref[...]
        r = lax.rsqrt(jnp.mean(xf * xf, axis=-1, keepdims=True) + RMS_EPS)
        xhat = xf * r
        dg_ref[...] += jnp.sum(dh * xhat, axis=0, keepdims=True)
        dxhat = dh * g_ref[...]
        dx_ref[...] = dxn_ref[...] + r * (dxhat - xhat * jnp.mean(dxhat * xhat, axis=-1, keepdims=True))

    tokd = pl.BlockSpec((tb, D_MODEL), lambda i: (i, 0))
    return pl.pallas_call(
        body, name="inproj_bwd_x",
        grid=(t // tb,),
        in_specs=[tokd, _const_spec((1, D_MODEL)),
                  pl.BlockSpec((1, D_MODEL, IN_WIDTH), lambda i: (layer, 0, 0), pipeline_mode=pl.Buffered(1))]
        + _dproj_specs(tb) + [tokd],
        out_specs=[tokd, pl.BlockSpec((1, D_MODEL), lambda i: (0, 0))],
        out_shape=[S((t, D_MODEL), F32), S((1, D_MODEL), F32)],
        compiler_params=_params(1),
    )(x, g_pre, w_in, *dproj_parts, dxn)


def _inproj_bwd_w(x, g_pre, dproj_parts):
    t = x.shape[0]
    tb = min(256, t)
    nt = t // tb

    def body(x_ref, g_ref, dpre_ref, dgates_ref, dq_ref, dk_ref, dv_ref, dw_hbm, acc_ref):
        i = pl.program_id(0)
        xf = x_ref[...]
        r = lax.rsqrt(jnp.mean(xf * xf, axis=-1, keepdims=True) + RMS_EPS)
        ht = (xf * r * g_ref[...]).T.astype(BF16)

        @pl.when(i == 0)
        def _():
            acc_ref[...] = jnp.zeros_like(acc_ref)

        for u in range(N_UNITS):
            acc_ref[:, u * UNIT:(u + 1) * UNIT] += _dot(
                ht, _dproj_unit(u, dpre_ref, dgates_ref, dq_ref, dk_ref, dv_ref))

        @pl.when(i == nt - 1)
        def _():
            pltpu.sync_copy(acc_ref, dw_hbm)

    return pl.pallas_call(
        body, name="inproj_bwd_w",
        grid=(nt,),
        in_specs=[pl.BlockSpec((tb, D_MODEL), lambda i: (i, 0)), _const_spec((1, D_MODEL))] + _dproj_specs(tb),
        out_specs=pl.BlockSpec(memory_space=pl.ANY),
        out_shape=S((D_MODEL, IN_WIDTH), F32),
        scratch_shapes=[pltpu.VMEM((D_MODEL, IN_WIDTH), F32)],
        compiler_params=_params(1),
    )(x, g_pre, *dproj_parts)


def _attn_tile_size(t):
    return min(256, t)


def _softplus_parts(z):
    sp = jnp.maximum(z, 0.0) + jnp.log(1.0 + jnp.exp(-jnp.abs(z)))
    return -sp, z - sp


def _attn_fwd(proj):
    t = proj.shape[1]
    tq = _attn_tile_size(t)

    def body(q_ref, k_ref, v_ref, o_ref, bs_ref, kf_ref):
        i = pl.program_id(1)
        row = lax.broadcasted_iota(jnp.int32, (tq, tq), 0)
        col = lax.broadcasted_iota(jnp.int32, (tq, tq), 1)
        tri = (row > col).astype(BF16)
        tri2 = jnp.concatenate([tri, tri], axis=0)
        causal = jnp.concatenate([col < row] * HEADS_PER_BLOCK, axis=0)
        lane = lax.broadcasted_iota(jnp.int32, (1, LANES), 1)
        q_all = q_ref[0] * ATTN_SCALE
        in_heads = [(lane >= hh * HEAD_DIM) & (lane < (hh + 1) * HEAD_DIM) for hh in range(HEADS_PER_BLOCK)]
        q_stack = jnp.concatenate([jnp.where(in_head, q_all, 0.0) for in_head in in_heads], axis=0).astype(BF16)
        rows = HEADS_PER_BLOCK * tq

        def tiles(kj, carry, acc, masked):
            ks = pl.multiple_of(kj * tq, tq)
            kh = k_ref[0, pl.ds(ks, tq), :].astype(BF16)
            vh = v_ref[0, pl.ds(ks, tq), :].astype(BF16)
            z = _dot_nt(q_stack, kh)
            lm, lb = _softplus_parts(z)
            if masked:
                lm = jnp.where(causal, lm, 0.0)
            between = _dot(jnp.concatenate(_split_bf16(lm), axis=1), tri2) + carry
            w = jnp.exp(lb + between)
            if masked:
                w = jnp.where(causal, w, 0.0)
            return carry + jnp.sum(lm, axis=1, keepdims=True), acc + _dot(w.astype(BF16), vh)

        carry, acc = tiles(i, jnp.zeros((rows, 1), F32), jnp.zeros((rows, LANES), F32), True)

        def more(c):
            return jnp.logical_and(c[0] >= 0, jnp.max(c[1]) >= EXP_ZERO_BELOW)

        kj_end, carry, acc = lax.while_loop(
            more, lambda c: (c[0] - 1,) + tiles(c[0], c[1], c[2], False), (i - 1, carry, acc))
        o_out, b_out = acc[0:tq], jnp.broadcast_to(carry[0:tq], (tq, LANES))
        for hh in range(1, HEADS_PER_BLOCK):
            o_out = jnp.where(in_heads[hh], acc[hh * tq:(hh + 1) * tq], o_out)
            b_out = jnp.where(in_heads[hh], carry[hh * tq:(hh + 1) * tq], b_out)
        o_ref[...] = o_out
        bs_ref[0] = b_out
        kf_ref[0, 0] = jnp.zeros((8, LANES), F32) + (kj_end + 1).astype(F32)

    nq = t // tq
    return pl.pallas_call(
        body, name="attn_fwd",
        grid=(N_HEAD_BLOCKS, nq),
        in_specs=[pl.BlockSpec((1, tq, LANES), lambda p, i: (U_Q, i, p)),
                  pl.BlockSpec((1, t, LANES), lambda p, i: (U_K, 0, p)),
                  pl.BlockSpec((1, t, LANES), lambda p, i: (U_V, 0, p))],
        out_specs=[pl.BlockSpec((tq, LANES), lambda p, i: (i, p)),
                   pl.BlockSpec((1, tq, LANES), lambda p, i: (p, i, 0)),
                   pl.BlockSpec((1, 1, 8, LANES), lambda p, i: (p, i, 0, 0))],
        out_shape=[S((t, UNIT), F32), S((N_HEAD_BLOCKS, t, LANES), F32), S((N_HEAD_BLOCKS, nq, 8, LANES), F32)],
        compiler_params=_params(2),
    )(proj, proj, proj)


def _attn_bwd(proj, d_o, bsum, kfirst):
    t = proj.shape[1]
    tq = _attn_tile_size(t)
    nq = t // tq

    def body(q_ref, k_ref, v_ref, do_ref, bs_ref, kf_ref, dq_ref, dk_ref, dv_ref, dkt_ref, dvt_ref):
        i = pl.program_id(1)

        @pl.when(i == 0)
        def _():
            dkt_ref[...] = jnp.zeros_like(dkt_ref)
            dvt_ref[...] = jnp.zeros_like(dvt_ref)

        row = lax.broadcasted_iota(jnp.int32, (tq, tq), 0)
        col = lax.broadcasted_iota(jnp.int32, (tq, tq), 1)
        upto = (row <= col).astype(BF16)
        before = (row < col).astype(BF16)
        upto2 = jnp.concatenate([upto, upto], axis=0)
        before2 = jnp.concatenate([before, before], axis=0)
        causal = jnp.concatenate([col < row] * HEADS_PER_BLOCK, axis=0)
        lane = lax.broadcasted_iota(jnp.int32, (1, LANES), 1)
        q_all = q_ref[0] * ATTN_SCALE
        do_all = do_ref[...].astype(F32)
        bs_all = bs_ref[0]
        k_first = jnp.clip(jnp.max(kf_ref[0, 0]).astype(jnp.int32), 0, i)

        in_heads = [(lane >= hh * HEAD_DIM) & (lane < (hh + 1) * HEAD_DIM) for hh in range(HEADS_PER_BLOCK)]
        q_f32 = jnp.concatenate([jnp.where(in_head, q_all, 0.0) for in_head in in_heads], axis=0)
        do_f32 = jnp.concatenate([jnp.where(in_head, do_all, 0.0) for in_head in in_heads], axis=0)
        q_stack, do_stack = q_f32.astype(BF16), do_f32.astype(BF16)
        q_t, do_t = q_f32.T.astype(BF16), do_f32.T.astype(BF16)
        btot = jnp.concatenate([jnp.max(jnp.where(in_head, bs_all, -jnp.inf), axis=1, keepdims=True)
                                for in_head in in_heads], axis=0)
        rows = HEADS_PER_BLOCK * tq

        def tiles(kj, c_b, c_p, dq, masked):
            ks = pl.multiple_of(kj * tq, tq)
            kh = k_ref[0, pl.ds(ks, tq), :].astype(BF16)
            vh = v_ref[0, pl.ds(ks, tq), :].astype(BF16)
            z = _dot_nt(q_stack, kh)
            lm, lb = _softplus_parts(z)
            if masked:
                lm = jnp.where(causal, lm, 0.0)
            between = btot - (c_b + _dot(jnp.concatenate(_split_bf16(lm), axis=1), upto2))
            w = jnp.exp(lb + between)
            if masked:
                w = jnp.where(causal, w, 0.0)
            e = w * _dot_nt(do_stack, vh)
            p_sum = c_p + _dot(jnp.concatenate(_split_bf16(e), axis=1), before2)
            beta = jnp.exp(lb)
            dz = e * (1.0 - beta) - p_sum * beta
            if masked:
                dz = jnp.where(causal, dz, 0.0)
            dzb = dz.astype(BF16)
            dkt_ref[kj] += _dot(q_t, dzb)
            dvt_ref[kj] += _dot(do_t, w.astype(BF16))
            return (c_b + jnp.sum(lm, axis=1, keepdims=True), c_p + jnp.sum(e, axis=1, keepdims=True),
                    dq + _dot(dzb, kh))

        zero = (jnp.zeros((rows, 1), F32), jnp.zeros((rows, 1), F32), jnp.zeros((rows, LANES), F32))
        state = lax.fori_loop(k_first, i, lambda kj, st: tiles(kj, *st, False), zero)
        dq = tiles(i, *state, True)[2]
        dq_out = dq[0:tq]
        for hh in range(1, HEADS_PER_BLOCK):
            dq_out = jnp.where(in_heads[hh], dq[hh * tq:(hh + 1) * tq], dq_out)
        dq_ref[...] = (dq_out * ATTN_SCALE).astype(BF16)

        @pl.when(i == nq - 1)
        def _():
            for kj in range(nq):
                dk_ref[kj * tq:(kj + 1) * tq, :] = dkt_ref[kj].T.astype(BF16)
                dv_ref[kj * tq:(kj + 1) * tq, :] = dvt_ref[kj].T.astype(BF16)

    return pl.pallas_call(
        body, name="attn_bwd",
        grid=(N_HEAD_BLOCKS, nq),
        in_specs=[pl.BlockSpec((1, tq, LANES), lambda p, i: (U_Q, i, p)),
                  pl.BlockSpec((1, t, LANES), lambda p, i: (U_K, 0, p)),
                  pl.BlockSpec((1, t, LANES), lambda p, i: (U_V, 0, p)),
                  pl.BlockSpec((tq, LANES), lambda p, i: (i, p)),
                  pl.BlockSpec((1, tq, LANES), lambda p, i: (p, i, 0)),
                  pl.BlockSpec((1, 1, 8, LANES), lambda p, i: (p, i, 0, 0))],
        out_specs=[pl.BlockSpec((tq, LANES), lambda p, i: (i, p)),
                   pl.BlockSpec((t, LANES), lambda p, i: (0, p)),
                   pl.BlockSpec((t, LANES), lambda p, i: (0, p))],
        out_shape=[S((t, UNIT), BF16)] * 3,
        scratch_shapes=[pltpu.VMEM((nq, LANES, tq), F32), pltpu.VMEM((nq, LANES, tq), F32)],
        compiler_params=_params(2),
    )(proj, proj, proj, d_o, bsum, kfirst)


N_MIX_VIEWS = 11
N_MIX_VIEWS_NO_CONV = 7


def _mix_in_specs(tb, conv_inputs=True):
    hb = tb // HALO

    def unit(u):
        return pl.BlockSpec((1, tb, UNIT), lambda i: (u, i, 0))

    def halo(u):
        return pl.BlockSpec((1, HALO, UNIT), lambda i: (u, jnp.maximum(i * hb - 1, 0), 0))

    conv = [unit(U_CA), halo(U_CA), unit(U_CB), halo(U_CB)] if conv_inputs else []
    return [unit(U_P), halo(U_P), unit(U_GP)] + conv + [
        unit(U_GC), unit(U_GA), pl.BlockSpec((3, tb, UNIT), lambda i: (3, i, 0)),
        pl.BlockSpec((3, tb, UNIT), lambda i: (4, i, 0))]


def _mix_weight_specs():
    return [_const_spec((4, GROUP, GROUP)), _const_spec((1, UNIT)), _const_spec((1, UNIT)),
            _const_spec((UNIT, D_MODEL)), _const_spec((CONV_K_PAD, UNIT)), _const_spec((1, UNIT)),
            _const_spec((1, UNIT)), _const_spec((1, UNIT)), _const_spec((UNIT, D_MODEL)),
            _const_spec((UNIT, D_MODEL)), _const_spec((D_MODEL, D_MODEL)), _const_spec((1, D_MODEL))]


def _shifted_reader(buf, rbuf, tb):
    if rbuf is None:
        return lambda s: buf[s:s + tb, :]
    length = tb + HALO - 8
    for b in range(1, 8):
        rbuf[b, :, :] = buf[b:b + length, :]

    def read(s):
        a, b = divmod(s, 8)
        return buf[s:s + tb, :] if b == 0 else rbuf[b, 8 * a:8 * a + tb, :]

    return read


def _mix_forward(i, tb, proj_refs, o_ref, w_refs, pbuf, ubuf=None, rbuf=None, cv_ref=None):
    if cv_ref is None:
        p_ref, ph_ref, gp_ref, ca_ref, cah_ref, cb_ref, cbh_ref, gc_ref, ga_ref, gm0_ref, gm1_ref = proj_refs
    else:
        p_ref, ph_ref, gp_ref, gc_ref, ga_ref, gm0_ref, gm1_ref = proj_refs
    (poolw_ref, poolb_ref, pscale_ref, wpo_ref, convw_ref, convb_ref, lng_ref, lnb_ref,
     wco_ref, wao_ref, wo_ref, gpost_ref) = w_refs
    first = i == 0
    r = {}

    pbuf[0:HALO, :] = jnp.where(first, 0.0, ph_ref[0])
    pbuf[HALO:HALO + tb, :] = p_ref[0]
    tpos = i * tb + lax.broadcasted_iota(jnp.int32, (tb, 1), 0)
    d_parts, y_parts = [], []
    for g, win in enumerate(POOL_WINDOWS):
        cs = slice(g * GROUP, (g + 1) * GROUP)
        cur = pbuf[HALO:HALO + tb, cs]
        s = cur
        for j in range(1, win):
            s = s + pbuf[HALO - j:HALO - j + tb, cs]
        cnt = jnp.minimum(tpos + 1, win).astype(F32)
        d_g = s / cnt - cur
        d_parts.append(d_g)
        y_parts.append(_dot(d_g.astype(BF16), poolw_ref[g]))
    r["d"] = d_parts
    y = jnp.concatenate(y_parts, axis=1) + poolb_ref[...]
    r["y"] = y
    mp = y * pscale_ref[...]
    gp = gp_ref[0]
    sgp = _sigmoid(gp)
    r["mp"], r["gp"], r["sgp"] = mp, gp, sgp
    ua = mp * (gp * sgp)

    if cv_ref is None:
        ah, bh = cah_ref[0], cbh_ref[0]
        ubuf[0:HALO, :] = jnp.where(first, 0.0, ah * _sigmoid(bh))
        ubuf[HALO:HALO + tb, :] = ca_ref[0] * _sigmoid(cb_ref[0])
        cv = jnp.zeros((tb, UNIT), F32) + convb_ref[...]
        off = HALO - (CONV_K - 1)
        u_at = _shifted_reader(ubuf, rbuf, tb)
        for k in range(CONV_K):
            cv = cv + convw_ref[k:k + 1, :] * u_at(off + k)
    else:
        cv = cv_ref[...]
    r["cv"] = cv
    mu = jnp.mean(cv, axis=-1, keepdims=True)
    cc = cv - mu
    rs = lax.rsqrt(jnp.mean(cc * cc, axis=-1, keepdims=True) + LN_EPS)
    nrm = cc * rs
    ln = nrm * lng_ref[...] + lnb_ref[...]
    sln = _sigmoid(ln)
    sc = ln * sln
    gc = gc_ref[0]
    sgc = _sigmoid(gc)
    r["rs"], r["nrm"], r["ln"], r["sln"], r["sc"], r["gc"], r["sgc"] = rs, nrm, ln, sln, sc, gc, sgc
    ub = sc * (gc * sgc)

    o = o_ref[...]
    ga = ga_ref[0]
    sga = _sigmoid(ga)
    r["o"], r["ga"], r["sga"] = o, ga, sga
    uc = o * (ga * sga)

    r["ua"], r["ub"], r["uc"] = ua.astype(BF16), ub.astype(BF16), uc.astype(BF16)
    ya = _dot(r["ua"], wpo_ref[...])
    yb = _dot(r["ub"], wco_ref[...])
    yc = _dot(r["uc"], wao_ref[...])
    g0 = _sigmoid(jnp.concatenate([gm0_ref[0], gm0_ref[1]], axis=1))
    g1 = _sigmoid(jnp.concatenate([gm0_ref[2], gm1_ref[0]], axis=1))
    g2 = _sigmoid(jnp.concatenate([gm1_ref[1], gm1_ref[2]], axis=1))
    r["ya"], r["yb"], r["yc"], r["g0"], r["g1"], r["g2"] = ya, yb, yc, g0, g1, g2
    m = (g0 * ya + g1 * yb + g2 * yc).astype(BF16)
    r["m"] = m
    out = _dot(m, wo_ref[...])
    r2 = lax.rsqrt(jnp.mean(out * out, axis=-1, keepdims=True) + RMS_EPS)
    r["n2"], r["r2"] = out * r2, r2
    return r


def _mix_fwd(proj, o, x, weights):
    t = x.shape[0]
    tb = min(256, t)
    nv = N_MIX_VIEWS

    def body(*refs):
        proj_refs, o_ref, x_ref = refs[0:nv], refs[nv], refs[nv + 1]
        w_refs = refs[nv + 2:nv + 14]
        xn_ref, cv_ref, pbuf, ubuf, rbuf = refs[nv + 14:nv + 19]
        r = _mix_forward(pl.program_id(0), tb, proj_refs, o_ref, w_refs, pbuf, ubuf, rbuf)
        xn_ref[...] = x_ref[...] + r["n2"] * w_refs[11][...]
        cv_ref[...] = r["cv"]

    return pl.pallas_call(
        body, name="mix_fwd",
        grid=(t // tb,),
        in_specs=_mix_in_specs(tb) + [pl.BlockSpec((tb, UNIT), lambda i: (i, 0)),
                                      pl.BlockSpec((tb, D_MODEL), lambda i: (i, 0))] + _mix_weight_specs(),
        out_specs=[pl.BlockSpec((tb, D_MODEL), lambda i: (i, 0)), pl.BlockSpec((tb, UNIT), lambda i: (i, 0))],
        out_shape=[S((t, D_MODEL), F32), S((t, UNIT), F32)],
        scratch_shapes=[pltpu.VMEM((HALO + tb, UNIT), F32), pltpu.VMEM((HALO + tb, UNIT), F32),
                        pltpu.VMEM((8, HALO + tb - 8, UNIT), F32)],
        compiler_params=_params(1),
    )(*([proj] * nv), o, x, *weights)


def _mix_bwd(proj, o, cv, x, dxn, weights):
    t = x.shape[0]
    tb = min(256, t)
    nt = t // tb

    nv = N_MIX_VIEWS_NO_CONV

    def body(*refs):
        proj_refs, o_ref, cv_ref, x_ref, dxn_ref = refs[0:nv], refs[nv], refs[nv + 1], refs[nv + 2], refs[nv + 3]
        w_refs = refs[nv + 4:nv + 16]
        (dg_ref, dd_ref, dcv_ref, do_ref, dwo_hbm, dwp_hbm, dwc_hbm, dwa_hbm, dpw_ref,
         dvec_ref) = refs[nv + 16:nv + 26]
        pbuf, dwo_ref, dwp_ref, dwc_ref, dwa_ref = refs[nv + 26:nv + 31]
        (poolw_ref, _, pscale_ref, wpo_ref, _, _, lng_ref, _, wco_ref, wao_ref, wo_ref, gpost_ref) = w_refs
        i = pl.program_id(0)

        @pl.when(i == 0)
        def _():
            for ref in (dwo_ref, dwp_ref, dwc_ref, dwa_ref, dpw_ref, dvec_ref):
                ref[...] = jnp.zeros_like(ref)

        r = _mix_forward(i, tb, proj_refs, o_ref, w_refs, pbuf, cv_ref=cv_ref)

        def colsum(v):
            return jnp.sum(v, axis=0, keepdims=True)

        dxn = dxn_ref[...]
        n2 = r["n2"]
        dvec_ref[0:1, :] += colsum(dxn * n2)
        dn2 = dxn * gpost_ref[...]
        dout = (r["r2"] * (dn2 - n2 * jnp.mean(dn2 * n2, axis=-1, keepdims=True))).astype(BF16)
        dm = _dot_nt(dout, wo_ref[...])
        dwo_ref[...] += _dot_tn(r["m"], dout)

        g0, g1, g2 = r["g0"], r["g1"], r["g2"]
        dgm = [dm * r["ya"] * g0 * (1.0 - g0), dm * r["yb"] * g1 * (1.0 - g1), dm * r["yc"] * g2 * (1.0 - g2)]
        for bidx in range(N_BRANCH):
            dg_ref[3 + 2 * bidx] = dgm[bidx][:, 0:UNIT].astype(BF16)
            dg_ref[4 + 2 * bidx] = dgm[bidx][:, UNIT:2 * UNIT].astype(BF16)
        dya = (dm * g0).astype(BF16)
        dyb = (dm * g1).astype(BF16)
        dyc = (dm * g2).astype(BF16)
        dua = _dot_nt(dya, wpo_ref[...])
        dub = _dot_nt(dyb, wco_ref[...])
        duc = _dot_nt(dyc, wao_ref[...])
        dwp_ref[...] += _dot_tn(r["ua"], dya)
        dwc_ref[...] += _dot_tn(r["ub"], dyb)
        dwa_ref[...] += _dot_tn(r["uc"], dyc)

        gp, sgp = r["gp"], r["sgp"]
        dmp = dua * (gp * sgp)
        dg_ref[0] = (dua * r["mp"] * _dsilu(gp, sgp)).astype(BF16)
        dvec_ref[2:3, 0:UNIT] += colsum(dmp * r["y"])
        dy = dmp * pscale_ref[...]
        dvec_ref[1:2, 0:UNIT] += colsum(dy)
        dd_parts = []
        for g in range(len(POOL_WINDOWS)):
            dy_g = dy[:, g * GROUP:(g + 1) * GROUP].astype(BF16)
            dd_parts.append(_dot_nt(dy_g, poolw_ref[g]))
            dpw_ref[g] += _dot_tn(r["d"][g].astype(BF16), dy_g)
        dd_ref[...] = jnp.concatenate(dd_parts, axis=1)

        gc, sgc = r["gc"], r["sgc"]
        dsc = dub * (gc * sgc)
        dg_ref[1] = (dub * r["sc"] * _dsilu(gc, sgc)).astype(BF16)
        dln = dsc * _dsilu(r["ln"], r["sln"])
        nrm = r["nrm"]
        dvec_ref[4:5, 0:UNIT] += colsum(dln * nrm)
        dvec_ref[5:6, 0:UNIT] += colsum(dln)
        dnrm = dln * lng_ref[...]
        dcv = r["rs"] * (dnrm - jnp.mean(dnrm, axis=-1, keepdims=True)
                         - nrm * jnp.mean(dnrm * nrm, axis=-1, keepdims=True))
        dvec_ref[3:4, 0:UNIT] += colsum(dcv)
        dcv_ref[...] = dcv

        ga, sga = r["ga"], r["sga"]
        do_ref[...] = (duc * (ga * sga)).astype(BF16)
        dg_ref[2] = (duc * r["o"] * _dsilu(ga, sga)).astype(BF16)

        @pl.when(i == nt - 1)
        def _():
            for acc, hbm in ((dwo_ref, dwo_hbm), (dwp_ref, dwp_hbm), (dwc_ref, dwc_hbm), (dwa_ref, dwa_hbm)):
                pltpu.sync_copy(acc, hbm)

    def acc_spec(shape):
        n = len(shape)
        return pl.BlockSpec(shape, lambda i: (0,) * n)

    tok = lambda w: pl.BlockSpec((tb, w), lambda i: (i, 0))
    any_spec = pl.BlockSpec(memory_space=pl.ANY)
    w_shapes = [(D_MODEL, D_MODEL), (UNIT, D_MODEL), (UNIT, D_MODEL), (UNIT, D_MODEL)]
    return pl.pallas_call(
        body, name="mix_bwd",
        grid=(nt,),
        in_specs=_mix_in_specs(tb, conv_inputs=False) + [tok(UNIT), tok(UNIT), tok(D_MODEL), tok(D_MODEL)]
        + _mix_weight_specs(),
        out_specs=[pl.BlockSpec((9, tb, UNIT), lambda i: (0, i, 0)), tok(UNIT), tok(UNIT), tok(UNIT)]
        + [any_spec] * 4 + [acc_spec((4, GROUP, GROUP)), acc_spec((8, D_MODEL))],
        out_shape=[S((9, t, UNIT), BF16), S((t, UNIT), F32), S((t, UNIT), F32), S((t, UNIT), BF16)]
        + [S(s, F32) for s in w_shapes] + [S((4, GROUP, GROUP), F32), S((8, D_MODEL), F32)],
        scratch_shapes=[pltpu.VMEM((HALO + tb, UNIT), F32)] + [pltpu.VMEM(s, F32) for s in w_shapes],
        compiler_params=_params(1),
    )(*([proj] * nv), o, cv, x, dxn, *weights)


def _halo_bwd(proj, dcv, dd, conv_w):
    t = dcv.shape[0]
    tb = min(256, t)
    hb = tb // HALO
    n_halo_blocks = t // HALO
    nt = t // tb

    def body(ca_ref, cb_ref, dcv_ref, dcvh_ref, dd_ref, ddh_ref, cw_ref,
             dpre_ref, dcw_ref, gbuf, nbuf, rg_buf, acc_ref):
        i = pl.program_id(0)
        last = i == nt - 1

        @pl.when(i == 0)
        def _():
            acc_ref[...] = jnp.zeros_like(acc_ref)

        a, b = ca_ref[0], cb_ref[0]
        sb = _sigmoid(b)
        u = a * sb
        gbuf[0:tb, :] = dcv_ref[...]
        gbuf[tb:tb + HALO, :] = jnp.where(last, 0.0, dcvh_ref[...])

        g_at = _shifted_reader(gbuf, rg_buf, tb)
        du = jnp.zeros((tb, UNIT), F32)
        for k in range(CONV_K):
            ahead = g_at(CONV_K - 1 - k)
            du = du + cw_ref[k:k + 1, :] * ahead
            acc_ref[k] += jnp.sum((ahead * u).reshape(tb // 8, 8, UNIT), axis=0)
        dpre_ref[1] = (du * sb).astype(BF16)
        dpre_ref[2] = (du * a * sb * (1.0 - sb)).astype(BF16)

        @pl.when(last)
        def _():
            dcw_ref[...] = jnp.sum(acc_ref[...], axis=1)

        tpos = i * tb + lax.broadcasted_iota(jnp.int32, (tb + HALO, 1), 0)
        dd_main = dd_ref[...]
        dd_ext = jnp.concatenate([dd_main, jnp.where(last, 0.0, ddh_ref[...])], axis=0)
        dp_parts = []
        for g, win in enumerate(POOL_WINDOWS):
            cs = slice(g * GROUP, (g + 1) * GROUP)
            cnt = jnp.minimum(tpos + 1, win).astype(F32)
            nbuf[:, cs] = dd_ext[:, cs] / cnt
        for g, win in enumerate(POOL_WINDOWS):
            cs = slice(g * GROUP, (g + 1) * GROUP)
            s = nbuf[0:tb, cs]
            for j in range(1, win):
                s = s + nbuf[j:j + tb, cs]
            dp_parts.append(s - dd_main[:, cs])
        dpre_ref[0] = jnp.concatenate(dp_parts, axis=1).astype(BF16)

    def unit(u):
        return pl.BlockSpec((1, tb, UNIT), lambda i: (u, i, 0))

    tok = pl.BlockSpec((tb, UNIT), lambda i: (i, 0))
    future = pl.BlockSpec((HALO, UNIT), lambda i: (jnp.minimum((i + 1) * hb, n_halo_blocks - 1), 0))
    return pl.pallas_call(
        body, name="halo_bwd",
        grid=(nt,),
        in_specs=[unit(U_CA), unit(U_CB), tok, future, tok, future, _const_spec((CONV_K_PAD, UNIT))],
        out_specs=[pl.BlockSpec((3, tb, UNIT), lambda i: (0, i, 0)),
                   pl.BlockSpec((CONV_K_PAD, UNIT), lambda i: (0, 0))],
        out_shape=[S((3, t, UNIT), BF16), S((CONV_K_PAD, UNIT), F32)],
        scratch_shapes=[pltpu.VMEM((HALO + tb, UNIT), F32)] * 2
        + [pltpu.VMEM((8, HALO + tb - 8, UNIT), F32), pltpu.VMEM((CONV_K_PAD, 8, UNIT), F32)],
        compiler_params=_params(1),
    )(proj, proj, dcv, dcv, dd, dd, conv_w)


def _loss_head(y, target):
    t = y.shape[0]
    tb = min(512, t)

    def body(y_ref, t_ref, loss_ref, dy_ref):
        @pl.when(pl.program_id(0) == 0)
        def _():
            loss_ref[...] = jnp.zeros_like(loss_ref)

        err = y_ref[...] - t_ref[...]
        dy_ref[...] = err * (1.0 / D_MODEL)
        part = 0.5 * jnp.sum(jnp.mean(err * err, axis=-1, keepdims=True), axis=0, keepdims=True)
        r8 = lax.broadcasted_iota(jnp.int32, (8, LANES), 0)
        c8 = lax.broadcasted_iota(jnp.int32, (8, LANES), 1)
        loss_ref[...] += jnp.where((r8 == 0) & (c8 == 0), part, 0.0)

    return pl.pallas_call(
        body, name="loss_head",
        grid=(t // tb,),
        in_specs=[pl.BlockSpec((tb, D_MODEL), lambda i: (i, 0))] * 2,
        out_specs=[pl.BlockSpec((8, LANES), lambda i: (0, 0)), pl.BlockSpec((tb, D_MODEL), lambda i: (i, 0))],
        out_shape=[S((8, LANES), F32), S((t, D_MODEL), F32)],
        compiler_params=_params(1),
    )(y, target)


def _adamw(name, parts, w, m, v):
    n, rows, cols = parts.shape
    rb = rows
    while rb * cols * 4 * (n + 7) * 2 > 24 * 1024 * 1024 and rb % 16 == 0:
        rb //= 2

    def body(p_ref, w_ref, m_ref, v_ref, g_ref, d_ref, nm_ref, nv_ref):
        g = p_ref[0]
        for k in range(1, n):
            g = g + p_ref[k]
        nm = ADAM_B1 * m_ref[...] + (1.0 - ADAM_B1) * g
        nv = ADAM_B2 * v_ref[...] + (1.0 - ADAM_B2) * (g * g)
        m_hat = nm / (1.0 - ADAM_B1 ** ADAM_STEP)
        v_hat = nv / (1.0 - ADAM_B2 ** ADAM_STEP)
        g_ref[...] = g
        d_ref[...] = -ADAM_LR * (m_hat / (jnp.sqrt(v_hat) + ADAM_EPS) + ADAM_WD * w_ref[...])
        nm_ref[...] = nm
        nv_ref[...] = nv

    blk = pl.BlockSpec((rb, cols), lambda i: (i, 0))
    return pl.pallas_call(
        body, name=name,
        grid=(rows // rb,),
        in_specs=[pl.BlockSpec((n, rb, cols), lambda i: (0, i, 0)), blk, blk, blk],
        out_specs=[blk] * 4,
        out_shape=[S((rows, cols), F32)] * 4,
        compiler_params=_params(1),
    )(parts, w, m, v)


def _axis_slice(ref, axis, block, size):
    idx = [slice(None)] * len(ref.shape)
    idx[axis] = pl.ds(pl.multiple_of(block * size, size), size)
    return ref.at[tuple(idx)]


def _axis_index(ref, axis, index):
    return ref.at[tuple([slice(None)] * axis + [index])]


COPY_PIECE_BYTES = 2 * 1024 * 1024


def _row_pieces(shape):
    rows = shape[-2]
    size = 4 * rows * shape[-1]
    for d in shape[:-2]:
        size *= d
    n = 1
    while size // n > COPY_PIECE_BYTES and rows % (2 * n * 8) == 0:
        n *= 2
    return [(k * (rows // n), rows // n) for k in range(n)]


def _rows(ref, r0, rows):
    idx = [slice(None)] * len(ref.shape)
    idx[-2] = pl.ds(r0, rows)
    return ref.at[tuple(idx)]


def _staged_local_copies(pairs, bufs, in_sems, out_sems):
    ins = [pltpu.make_async_copy(src, buf, in_sems.at[k]) for k, ((src, _), buf) in enumerate(zip(pairs, bufs))]
    for cp in ins:
        cp.start()
    for cp in ins:
        cp.wait()
    outs = [pltpu.make_async_copy(buf, dst, out_sems.at[k]) for k, ((_, dst), buf) in enumerate(zip(pairs, bufs))]
    for cp in outs:
        cp.start()
    return outs


def _run_copies(local, sends, recvs):
    for cp in local + sends:
        cp.start()
    for cp in recvs:
        cp.wait_recv()
    for cp in sends:
        cp.wait_send()
    for cp in local:
        cp.wait()


def _gather_weights(shards, axes):
    n = len(shards)
    fulls = [S(tuple(4 * d if k == ax else d for k, d in enumerate(s.shape)), s.dtype) for s, ax in zip(shards, axes)]

    def body(*refs):
        src, dst = refs[0:n], refs[n:2 * n]
        send_sems, recv_sems, pass_send_sems, pass_recv_sems, in_sems, out_sems = refs[2 * n:2 * n + 6]
        bufs = refs[2 * n + 6:3 * n + 6]
        x, y, c = lax.axis_index("x"), lax.axis_index("y"), lax.axis_index("c")
        chips = [(1 - x, y), (x, 1 - y), (1 - x, 1 - y)]

        def place(a, chip, layer=None):
            ref = _axis_slice(dst[a], axes[a], 2 * chip[0] + chip[1], src[a].shape[axes[a]])
            return ref if layer is None else ref.at[pl.ds(layer, 1)]

        def over_ici(k, chip_of_block, to):
            a, j, r0, rows = pieces[k]
            return pltpu.make_async_remote_copy(
                src_ref=_rows(src[a].at[pl.ds(c, 1)], r0, rows), dst_ref=_rows(place(a, chip_of_block, c), r0, rows),
                send_sem=send_sems.at[k], recv_sem=recv_sems.at[k], device_id=(to[0], to[1], c), device_id_type=MESH)

        def to_sibling(k, chip_of_block, layer):
            a, j, r0, rows = pieces[k]
            block = _rows(place(a, chip_of_block, layer), r0, rows)
            return pltpu.make_async_remote_copy(
                src_ref=block, dst_ref=block, send_sem=pass_send_sems.at[k], recv_sem=pass_recv_sems.at[k],
                device_id=(x, y, 1 - c), device_id_type=MESH)

        first = [over_ici(k, (x, y), chips[j]) for k, (a, j, r0, rows) in enumerate(pieces)]
        for cp in first:
            cp.start()
        local = _staged_local_copies([(src[a], place(a, (x, y))) for a in range(n)], bufs, in_sems, out_sems)
        passed = []
        for k, (a, j, r0, rows) in enumerate(pieces):
            over_ici(k, chips[j], chips[j]).wait_recv()
            passed.append(to_sibling(k, chips[j], c))
            passed[-1].start()
        for k, (a, j, r0, rows) in enumerate(pieces):
            to_sibling(k, chips[j], 1 - c).wait_recv()
        for cp in first + passed:
            cp.wait_send()
        for cp in local:
            cp.wait()

    per_array = [_row_pieces((1,) + shards[a].shape[1:]) for a in range(n)]
    pieces = [(a, j, r0, rows) for a in range(n) for r0, rows in per_array[a] for j in range(3)]
    any_spec = pl.BlockSpec(memory_space=pl.ANY)
    return pl.pallas_call(
        body, name="gather_weights",
        in_specs=[any_spec] * n, out_specs=[any_spec] * n, out_shape=fulls,
        scratch_shapes=[pltpu.SemaphoreType.DMA((len(pieces),))] * 4 + [pltpu.SemaphoreType.DMA((n,))] * 2
        + [pltpu.VMEM(s.shape, s.dtype) for s in shards],
        compiler_params=_params(0),
    )(*shards)


def _swap_layers(layers):
    n = len(layers)
    pieces = [(a, r0, rows) for a in range(n) for r0, rows in _row_pieces(layers[a][0].shape)]

    def body(*refs):
        src = [refs[2 * a:2 * a + 2] for a in range(n)]
        dst = refs[2 * n:3 * n]
        send_sems, recv_sems = refs[3 * n:3 * n + 2]
        x, y, c = lax.axis_index("x"), lax.axis_index("y"), lax.axis_index("c")

        def remote(k, layer):
            a, r0, rows = pieces[k]
            return pltpu.make_async_remote_copy(
                src_ref=_rows(src[a][layer], r0, rows), dst_ref=_rows(dst[a], r0, rows),
                send_sem=send_sems.at[k], recv_sem=recv_sems.at[k], device_id=(x, y, 1 - c), device_id_type=MESH)

        for layer in range(2):
            @pl.when(c == 1 - layer)
            def _(layer=layer):
                for k in range(len(pieces)):
                    remote(k, layer).start()

        for k in range(len(pieces)):
            remote(k, 0).wait_recv()
        for k in range(len(pieces)):
            remote(k, 0).wait_send()

    any_spec = pl.BlockSpec(memory_space=pl.ANY)
    return pl.pallas_call(
        body, name="swap_layers",
        in_specs=[any_spec] * (2 * n), out_specs=[any_spec] * n,
        out_shape=[S(layers[a][0].shape, layers[a][0].dtype) for a in range(n)],
        scratch_shapes=[pltpu.SemaphoreType.DMA((len(pieces),))] * 2,
    )(*[arr for pair in layers for arr in pair])


def _add_own_layer(name, mine, theirs, layer):
    rows, cols = theirs.shape
    rb = rows
    while rb * cols * 4 * 4 * 2 > 24 * 1024 * 1024 and rb % 32 == 0:
        rb //= 2

    def body(layer_ref, m0_ref, m1_ref, t_ref, o_ref):
        own = jnp.where(layer_ref[0] == 0, m0_ref[...], m1_ref[...])
        o_ref[0] = (own + t_ref[...]).astype(BF16)

    def own_spec(which):
        return pl.BlockSpec((rb, cols), lambda i, l: (jnp.where(l[0] == which, i, 0), 0))

    return pl.pallas_call(
        body, name=name,
        grid_spec=pltpu.PrefetchScalarGridSpec(
            num_scalar_prefetch=1, grid=(rows // rb,),
            in_specs=[own_spec(0), own_spec(1), pl.BlockSpec((rb, cols), lambda i, l: (i, 0))],
            out_specs=pl.BlockSpec((1, rb, cols), lambda i, l: (0, i, 0))),
        out_shape=S((1, rows, cols), BF16),
        compiler_params=_params(1),
    )(layer, *mine, theirs)


def _send_to_owners(chip_halves, shard_axes, packed):
    n = len(chip_halves)

    def owned_shape(a):
        return tuple(d // 4 if k == shard_axes[a] else d for k, d in enumerate(chip_halves[a].shape))

    flips = [(dx, dy, dc) for dx in (0, 1) for dy in (0, 1) for dc in (0, 1)][1:]

    def body(*refs):
        src, psrc, dst, pdst = refs[0:n], refs[n], refs[n + 1:2 * n + 1], refs[2 * n + 1]
        send_sems, recv_sems, local_sems, psend_sems, precv_sems = refs[2 * n + 2:2 * n + 7]
        x, y, c = lax.axis_index("x"), lax.axis_index("y"), lax.axis_index("c")
        my_chip = 2 * x + y
        chips = [(1 - x, y), (x, 1 - y), (1 - x, 1 - y)]

        def owned(a, chip):
            return _axis_slice(src[a], shard_axes[a], 2 * chip[0] + chip[1], src[a].shape[shard_axes[a]] // 4)

        def remote(a, j, to, from_chip):
            return pltpu.make_async_remote_copy(
                src_ref=owned(a, to), dst_ref=dst[a].at[from_chip], send_sem=send_sems.at[a, j],
                recv_sem=recv_sems.at[a, j], device_id=(to[0], to[1], c), device_id_type=MESH)

        def peer(f):
            return (1 - x if f[0] else x, 1 - y if f[1] else y, 1 - c if f[2] else c)

        def premote(j, to, from_dev):
            return pltpu.make_async_remote_copy(
                src_ref=psrc, dst_ref=pdst.at[from_dev], send_sem=psend_sems.at[j], recv_sem=precv_sems.at[j],
                device_id=to, device_id_type=MESH)

        local = [pltpu.make_async_copy(_rows(owned(a, (x, y)), r0, rows), _rows(dst[a].at[my_chip], r0, rows),
                                       local_sems.at[k]) for k, (a, r0, rows) in enumerate(own_pieces)]
        local.append(pltpu.make_async_copy(psrc, pdst.at[4 * x + 2 * y + c], local_sems.at[len(own_pieces)]))
        sends = [remote(a, j, chip, my_chip) for a in range(n) for j, chip in enumerate(chips)]
        sends += [premote(j, peer(f), 4 * x + 2 * y + c) for j, f in enumerate(flips)]
        recvs = [remote(a, j, (x, y), 2 * chip[0] + chip[1]) for a in range(n) for j, chip in enumerate(chips)]
        for j, f in enumerate(flips):
            px, py, pc = peer(f)
            recvs.append(premote(j, (x, y, c), 4 * px + 2 * py + pc))
        _run_copies(local, sends, recvs)

    own_pieces = [(a, r0, rows) for a in range(n) for r0, rows in _row_pieces(owned_shape(a))]
    any_spec = pl.BlockSpec(memory_space=pl.ANY)
    return pl.pallas_call(
        body, name="send_to_owners",
        in_specs=[any_spec] * (n + 1), out_specs=[any_spec] * (n + 1),
        out_shape=[S((4,) + owned_shape(a), chip_halves[a].dtype) for a in range(n)]
        + [S((N_DEV,) + packed.shape, packed.dtype)],
        scratch_shapes=[pltpu.SemaphoreType.DMA((n, 3)), pltpu.SemaphoreType.DMA((n, 3)),
                        pltpu.SemaphoreType.DMA((len(own_pieces) + 1,)), pltpu.SemaphoreType.DMA((7,)),
                        pltpu.SemaphoreType.DMA((7,))],
    )(*chip_halves, packed)


def _join_halves(reduced, half_axes):
    n = len(reduced)

    def joined_shape(a):
        s, ax = reduced[a].shape, half_axes[a]
        return s[:ax] + (2,) + s[ax:]

    pieces = [(a, r0, rows) for a in range(n) for r0, rows in _row_pieces(reduced[a].shape)]

    def body(*refs):
        src, dst = refs[0:n], refs[n:2 * n]
        send_sems, recv_sems, in_sems, out_sems = refs[2 * n:2 * n + 4]
        bufs = refs[2 * n + 4:3 * n + 4]
        x, y, c = lax.axis_index("x"), lax.axis_index("y"), lax.axis_index("c")

        def remote(k, h):
            a, r0, rows = pieces[k]
            return pltpu.make_async_remote_copy(
                src_ref=_rows(src[a], r0, rows), dst_ref=_rows(_axis_index(dst[a], half_axes[a], h), r0, rows),
                send_sem=send_sems.at[k], recv_sem=recv_sems.at[k], device_id=(x, y, 1 - c), device_id_type=MESH)

        sends = [remote(k, c) for k in range(len(pieces))]
        for cp in sends:
            cp.start()
        own = [(src[a], _axis_index(dst[a], half_axes[a], c)) for a in range(n)]
        local = _staged_local_copies(own, bufs, in_sems, out_sems)
        for k in range(len(pieces)):
            remote(k, 1 - c).wait_recv()
        for cp in sends:
            cp.wait_send()
        for cp in local:
            cp.wait()

    any_spec = pl.BlockSpec(memory_space=pl.ANY)
    return pl.pallas_call(
        body, name="join_halves",
        in_specs=[any_spec] * n, out_specs=[any_spec] * n,
        out_shape=[S(joined_shape(a), reduced[a].dtype) for a in range(n)],
        scratch_shapes=[pltpu.SemaphoreType.DMA((len(pieces),))] * 2 + [pltpu.SemaphoreType.DMA((n,))] * 2
        + [pltpu.VMEM(r.shape, r.dtype) for r in reduced],
        compiler_params=_params(0),
    )(*reduced)


def _sum_slots(name, parts):
    n = parts.shape[0]
    shape = parts.shape[1:]
    flat = parts.reshape((n, -1, shape[-1]))
    rows, cols = flat.shape[1:]
    rb = rows
    while rb * cols * 4 * (n + 1) * 2 > 24 * 1024 * 1024 and rb % 32 == 0:
        rb //= 2

    def body(p_ref, o_ref):
        acc = p_ref[0].astype(F32)
        for k in range(1, n):
            acc = acc + p_ref[k].astype(F32)
        o_ref[...] = acc

    out = pl.pallas_call(
        body, name=name,
        grid=(rows // rb,),
        in_specs=[pl.BlockSpec((n, rb, cols), lambda i: (0, i, 0))],
        out_specs=pl.BlockSpec((rb, cols), lambda i: (i, 0)),
        out_shape=S((rows, cols), F32),
        compiler_params=_params(1),
    )(flat)
    return out.reshape(shape)


SMALL_NAMES = ("norm_pre", "pool_w", "pool_b", "pool_scale", "conv_b", "conv_ln_g", "conv_ln_b", "norm_post")
BIG_NAMES = ("w_in", "w_pool_out", "conv_w", "w_conv_out", "w_attn_out", "w_o")
SHARD_AXIS = {"w_in": 2, "w_pool_out": 2, "conv_w": 2, "w_conv_out": 2, "w_attn_out": 2, "w_o": 1}
WEIGHT_ORDER = ("norm_pre", "w_in", "pool_w", "pool_b", "pool_scale", "w_pool_out", "conv_w", "conv_b",
                "conv_ln_g", "conv_ln_b", "w_conv_out", "w_attn_out", "w_o", "norm_post")


def _pack_small(parts):
    return jnp.concatenate([parts[n].reshape(-1, LANES) for n in SMALL_NAMES], axis=0)


def _unpack_small(packed, shapes):
    out, r0 = {}, 0
    for n in SMALL_NAMES:
        size = 1
        for d in shapes[n]:
            size *= d
        rows = size // LANES
        out[n] = packed[r0:r0 + rows].reshape(shapes[n])
        r0 += rows
    return out


def _layer_weights(full, small, l):
    row = lambda a: a[l][None, :]
    return (small["pool_w"][l].astype(BF16), small["pool_b"][l].reshape(1, UNIT), row(small["pool_scale"]),
            full["w_pool_out"][l], full["conv_w"][l], row(small["conv_b"]), row(small["conv_ln_g"]),
            row(small["conv_ln_b"]), full["w_conv_out"][l], full["w_attn_out"][l], full["w_o"][l],
            row(small["norm_post"]))


def _forward_backward(x, target, full, small):
    depth = full["w_in"].shape[0]
    acts = []
    for l in range(depth):
        g_pre = small["norm_pre"][l][None, :]
        proj = _inproj_fwd(x, g_pre, full["w_in"], l)
        o, bsum, kfirst = _attn_fwd(proj)
        weights = _layer_weights(full, small, l)
        x_next, cv = _mix_fwd(proj, o, x, weights)
        acts.append((x, proj, o, cv, bsum, kfirst, weights, g_pre))
        x = x_next
    loss_tile, dx = _loss_head(x, target)

    big = {n: [None] * depth for n in BIG_NAMES}
    sm = {n: [None] * depth for n in SMALL_NAMES}
    for l in reversed(range(depth)):
        x_in, proj, o, cv, bsum, kfirst, weights, g_pre = acts[l]
        dgates, dd, dcv, d_o, dwo, dwp, dwc, dwa, dpw, dvec = _mix_bwd(proj, o, cv, x_in, dx, weights)
        dpre, dcw = _halo_bwd(proj, dcv, dd, full["conv_w"][l])
        dq, dk, dv = _attn_bwd(proj, d_o, bsum, kfirst)
        dproj_parts = (dpre, dgates, dq, dk, dv)
        dx, dg_pre = _inproj_bwd_x(x_in, g_pre, full["w_in"], l, dproj_parts, dx)
        big["w_in"][l] = _inproj_bwd_w(x_in, g_pre, dproj_parts)
        big["w_pool_out"][l], big["w_conv_out"][l], big["w_attn_out"][l], big["w_o"][l] = dwp, dwc, dwa, dwo
        big["conv_w"][l] = dcw
        sm["norm_pre"][l] = dg_pre[0]
        sm["pool_w"][l] = dpw
        sm["norm_post"][l] = dvec[0]
        sm["pool_b"][l] = dvec[1, 0:UNIT].reshape(4, GROUP)
        sm["pool_scale"][l] = dvec[2, 0:UNIT]
        sm["conv_b"][l] = dvec[3, 0:UNIT]
        sm["conv_ln_g"][l] = dvec[4, 0:UNIT]
        sm["conv_ln_b"][l] = dvec[5, 0:UNIT]
    sm = {n: jnp.stack(v) for n, v in sm.items()}
    return loss_tile, dx, big, sm


def kernel(x, norm_pre, w_in, pool_w, pool_b, pool_scale, w_pool_out, conv_w, conv_b, conv_ln_g, conv_ln_b, w_conv_out, w_attn_out, w_o, norm_post, loss_target, m_norm_pre, m_w_in, m_pool_w, m_pool_b, m_pool_scale, m_w_pool_out, m_conv_w, m_conv_b, m_conv_ln_g, m_conv_ln_b, m_w_conv_out, m_w_attn_out, m_w_o, m_norm_post, v_norm_pre, v_w_in, v_pool_w, v_pool_b, v_pool_scale, v_w_pool_out, v_conv_w, v_conv_b, v_conv_ln_g, v_conv_ln_b, v_w_conv_out, v_w_attn_out, v_w_o, v_norm_post):
    w = dict(norm_pre=norm_pre, w_in=w_in, pool_w=pool_w, pool_b=pool_b, pool_scale=pool_scale,
             w_pool_out=w_pool_out, conv_w=conv_w, conv_b=conv_b, conv_ln_g=conv_ln_g, conv_ln_b=conv_ln_b,
             w_conv_out=w_conv_out, w_attn_out=w_attn_out, w_o=w_o, norm_post=norm_post)
    m = dict(norm_pre=m_norm_pre, w_in=m_w_in, pool_w=m_pool_w, pool_b=m_pool_b, pool_scale=m_pool_scale,
             w_pool_out=m_w_pool_out, conv_w=m_conv_w, conv_b=m_conv_b, conv_ln_g=m_conv_ln_g,
             conv_ln_b=m_conv_ln_b, w_conv_out=m_w_conv_out, w_attn_out=m_w_attn_out, w_o=m_w_o,
             norm_post=m_norm_post)
    v = dict(norm_pre=v_norm_pre, w_in=v_w_in, pool_w=v_pool_w, pool_b=v_pool_b, pool_scale=v_pool_scale,
             w_pool_out=v_w_pool_out, conv_w=v_conv_w, conv_b=v_conv_b, conv_ln_g=v_conv_ln_g,
             conv_ln_b=v_conv_ln_b, w_conv_out=v_w_conv_out, w_attn_out=v_w_attn_out, w_o=v_w_o,
             norm_post=v_norm_post)
    pad_taps = lambda a: jnp.pad(a, ((0, 0), (0, CONV_K_PAD - CONV_K), (0, 0)))

    shards = [pad_taps(w[n]) if n == "conv_w" else w[n].astype(BF16) for n in BIG_NAMES]
    full = dict(zip(BIG_NAMES, _gather_weights(shards, [SHARD_AXIS[n] for n in BIG_NAMES])))
    small = {n: w[n] for n in SMALL_NAMES}

    loss_tile, dx, big, sm = _forward_backward(x[0], loss_target[0], full, small)

    packed = jnp.concatenate([_pack_small(sm), loss_tile], axis=0)
    layer_axis = [0] * len(BIG_NAMES)
    theirs = _swap_layers([big[n] for n in BIG_NAMES])
    my_layer = lax.axis_index("c").astype(jnp.int32).reshape(1)
    chip_sums = [_add_own_layer("sum_cores_" + n, big[n], t, my_layer) for n, t in zip(BIG_NAMES, theirs)]
    exchanged = _send_to_owners(chip_sums, [SHARD_AXIS[n] for n in BIG_NAMES], packed)
    reduced = [_sum_slots("sum_chips_" + n, p) for n, p in zip(BIG_NAMES, exchanged[:-1])]
    joined = _join_halves(reduced, layer_axis)
    out = {}
    for n, g in zip(BIG_NAMES, joined):
        wn, mn, vn = (pad_taps(a[n]) if n == "conv_w" else a[n] for a in (w, m, v))
        shape = wn.shape
        flat = lambda a: a.reshape((-1, shape[-1]))
        res = _adamw("adamw_" + n, g.reshape((1, -1, shape[-1])), flat(wn), flat(mn), flat(vn))
        res = [r.reshape(shape) for r in res]
        out[n] = [r[:, :CONV_K] for r in res] if n == "conv_w" else res
    parts = exchanged[-1]
    n_small_rows = parts.shape[1] - 8
    zeros_tile = jnp.zeros((8, LANES), F32)
    packs = [jnp.concatenate([_pack_small(a), zeros_tile], axis=0) for a in (w, m, v)]
    res = _adamw("adamw_small", parts, *packs)
    loss = res[0][n_small_rows, 0]
    shapes = {n: w[n].shape for n in SMALL_NAMES}
    unpacked = [_unpack_small(r[:n_small_rows], shapes) for r in res]
    for n in SMALL_NAMES:
        out[n] = [u[n] for u in unpacked]

    grads = [out[n][0] for n in WEIGHT_ORDER]
    deltas = [out[n][1] for n in WEIGHT_ORDER]
    new_m = [out[n][2] for n in WEIGHT_ORDER]
    new_v = [out[n][3] for n in WEIGHT_ORDER]
    return (loss, dx[None], *grads, *deltas, *new_m, *new_v)
```

```python
import jax
import jax.numpy as jnp
from jax import lax
from jax.experimental import pallas as pl
from jax.experimental.pallas import tpu as pltpu

F32 = jnp.float32
BF16 = jnp.bfloat16

D_MODEL = 1024
UNIT = 512
N_UNITS = 15
IN_WIDTH = UNIT * N_UNITS
N_HEADS = 8
HEAD_DIM = 64
HEADS_PER_BLOCK = 2
N_HEAD_BLOCKS = N_HEADS // HEADS_PER_BLOCK
LANES = 128
CONV_K = 31
CONV_K_PAD = 32
HALO = 32
POOL_WINDOWS = (2, 4, 8, 16)
GROUP = 128
N_BRANCH = 3
RMS_EPS = 1e-6
LN_EPS = 1e-5
ATTN_SCALE = 0.125
EXP_ZERO_BELOW = -104.0

ADAM_LR = 0.001
ADAM_B1 = 0.9
ADAM_B2 = 0.999
ADAM_EPS = 1e-08
ADAM_WD = 0.01
ADAM_STEP = 10

U_P, U_GP, U_CA, U_CB, U_GC, U_Q, U_K, U_V, U_GA, U_GM = 0, 1, 2, 3, 4, 5, 6, 7, 8, 9

V7X_VMEM_LIMIT = 62 * 1024 * 1024
N_DEV = 8
MESH = pl.DeviceIdType.MESH

S = jax.ShapeDtypeStruct


def _params(n_grid):
    return pltpu.CompilerParams(dimension_semantics=("arbitrary",) * n_grid, vmem_limit_bytes=V7X_VMEM_LIMIT)


def _sigmoid(x):
    return 0.5 * jnp.tanh(0.5 * x) + 0.5


def _dsilu(x, s):
    return s * (1.0 + x * (1.0 - s))


def _dot(a, b):
    return jnp.dot(a, b, preferred_element_type=F32)


def _dot_nt(a, b):
    return lax.dot_general(a, b, (((1,), (1,)), ((), ())), preferred_element_type=F32)


def _dot_tn(a, b):
    return lax.dot_general(a, b, (((0,), (0,)), ((), ())), preferred_element_type=F32)


def _split_bf16(x):
    hi = x.astype(BF16)
    lo = (x - hi.astype(F32)).astype(BF16)
    return hi, lo


def _const_spec(shape):
    n = len(shape)
    return pl.BlockSpec(shape, lambda *_: (0,) * n, pipeline_mode=pl.Buffered(1))


def _inproj_fwd(x, g_pre, w_in, layer):
    t = x.shape[0]
    tb = min(512, t)

    def body(x_ref, g_ref, w_ref, o_ref):
        xf = x_ref[...]
        r = lax.rsqrt(jnp.mean(xf * xf, axis=-1, keepdims=True) + RMS_EPS)
        h = (xf * r * g_ref[...]).astype(BF16)
        for u in range(N_UNITS):
            o_ref[u] = _dot(h, w_ref[0, :, u * UNIT:(u + 1) * UNIT])

    return pl.pallas_call(
        body, name="inproj_fwd",
        grid=(t // tb,),
        in_specs=[pl.BlockSpec((tb, D_MODEL), lambda i: (i, 0)), _const_spec((1, D_MODEL)),
                  pl.BlockSpec((1, D_MODEL, IN_WIDTH), lambda i: (layer, 0, 0), pipeline_mode=pl.Buffered(1))],
        out_specs=pl.BlockSpec((N_UNITS, tb, UNIT), lambda i: (0, i, 0)),
        out_shape=S((N_UNITS, t, UNIT), F32),
        compiler_params=_params(1),
    )(x, g_pre, w_in)


def _dproj_specs(tb):
    tok = pl.BlockSpec((tb, UNIT), lambda i: (i, 0))
    return [pl.BlockSpec((3, tb, UNIT), lambda i: (0, i, 0)), pl.BlockSpec((9, tb, UNIT), lambda i: (0, i, 0)),
            tok, tok, tok]


def _dproj_unit(u, dpre_ref, dgates_ref, dq_ref, dk_ref, dv_ref):
    pre = {U_P: 0, U_CA: 1, U_CB: 2}
    gate = {U_GP: 0, U_GC: 1, U_GA: 2}
    if u in pre:
        val = dpre_ref[pre[u]]
    elif u in gate:
        val = dgates_ref[gate[u]]
    elif u >= U_GM:
        val = dgates_ref[3 + u - U_GM]
    else:
        val = {U_Q: dq_ref, U_K: dk_ref, U_V: dv_ref}[u][...]
    return val.astype(BF16)


def _inproj_bwd_x(x, g_pre, w_in, layer, dproj_parts, dxn):
    t = x.shape[0]
    tb = min(256, t)

    def body(x_ref, g_ref, w_ref, dpre_ref, dgates_ref, dq_ref, dk_ref, dv_ref, dxn_ref, dx_ref, dg_ref):
        @pl.when(pl.program_id(0) == 0)
        def _():
            dg_ref[...] = jnp.zeros_like(dg_ref)

        dh = jnp.zeros((tb, D_MODEL), F32)
        for u in range(N_UNITS):
            dh = dh + _dot_nt(_dproj_unit(u, dpre_ref, dgates_ref, dq_ref, dk_ref, dv_ref),
                              w_ref[0, :, u * UNIT:(u + 1) * UNIT])
        xf = x_ref[...]
        r = lax.rsqrt(jnp.mean(xf * xf, axis=-1, keepdims=True) + RMS_EPS)
        xhat = xf * r
        dg_ref[...] += jnp.sum(dh * xhat, axis=0, keepdims=True)
        dxhat = dh * g_ref[...]
        dx_ref[...] = dxn_ref[...] + r * (dxhat - xhat * jnp.mean(dxhat * xhat, axis=-1, keepdims=True))

    tokd = pl.BlockSpec((tb, D_MODEL), lambda i: (i, 0))
    return pl.pallas_call(
        body, name="inproj_bwd_x",
        grid=(t // tb,),
        in_specs=[tokd, _const_spec((1, D_MODEL)),
                  pl.BlockSpec((1, D_MODEL, IN_WIDTH), lambda i: (layer, 0, 0), pipeline_mode=pl.Buffered(1))]
        + _dproj_specs(tb) + [tokd],
        out_specs=[tokd, pl.BlockSpec((1, D_MODEL), lambda i: (0, 0))],
        out_shape=[S((t, D_MODEL), F32), S((1, D_MODEL), F32)],
        compiler_params=_params(1),
    )(x, g_pre, w_in, *dproj_parts, dxn)


def _inproj_bwd_w(x, g_pre, dproj_parts):
    t = x.shape[0]
    tb = min(256, t)
    nt = t // tb

    def body(x_ref, g_ref, dpre_ref, dgates_ref, dq_ref, dk_ref, dv_ref, dw_hbm, acc_ref):
        i = pl.program_id(0)
        xf = x_ref[...]
        r = lax.rsqrt(jnp.mean(xf * xf, axis=-1, keepdims=True) + RMS_EPS)
        ht = (xf * r * g_ref[...]).T.astype(BF16)

        @pl.when(i == 0)
        def _():
            acc_ref[...] = jnp.zeros_like(acc_ref)

        for u in range(N_UNITS):
            acc_ref[:, u * UNIT:(u + 1) * UNIT] += _dot(
                ht, _dproj_unit(u, dpre_ref, dgates_ref, dq_ref, dk_ref, dv_ref))

        @pl.when(i == nt - 1)
        def _():
            pltpu.sync_copy(acc_ref, dw_hbm)

    return pl.pallas_call(
        body, name="inproj_bwd_w",
        grid=(nt,),
        in_specs=[pl.BlockSpec((tb, D_MODEL), lambda i: (i, 0)), _const_spec((1, D_MODEL))] + _dproj_specs(tb),
        out_specs=pl.BlockSpec(memory_space=pl.ANY),
        out_shape=S((D_MODEL, IN_WIDTH), F32),
        scratch_shapes=[pltpu.VMEM((D_MODEL, IN_WIDTH), F32)],
        compiler_params=_params(1),
    )(x, g_pre, *dproj_parts)


def _attn_tile_size(t):
    return min(256, t)


def _softplus_parts(z):
    sp = jnp.maximum(z, 0.0) + jnp.log(1.0 + jnp.exp(-jnp.abs(z)))
    return -sp, z - sp


def _attn_fwd(proj):
    t = proj.shape[1]
    tq = _attn_tile_size(t)

    def body(q_ref, k_ref, v_ref, o_ref, bs_ref, kf_ref):
        i = pl.program_id(1)
        row = lax.broadcasted_iota(jnp.int32, (tq, tq), 0)
        col = lax.broadcasted_iota(jnp.int32, (tq, tq), 1)
        tri = (row > col).astype(BF16)
        tri2 = jnp.concatenate([tri, tri], axis=0)
        causal = jnp.concatenate([col < row] * HEADS_PER_BLOCK, axis=0)
        lane = lax.broadcasted_iota(jnp.int32, (1, LANES), 1)
        q_all = q_ref[0] * ATTN_SCALE
        in_heads = [(lane >= hh * HEAD_DIM) & (lane < (hh + 1) * HEAD_DIM) for hh in range(HEADS_PER_BLOCK)]
        q_stack = jnp.concatenate([jnp.where(in_head, q_all, 0.0) for in_head in in_heads], axis=0).astype(BF16)
        rows = HEADS_PER_BLOCK * tq

        def tiles(kj, carry, acc, masked):
            ks = pl.multiple_of(kj * tq, tq)
            kh = k_ref[0, pl.ds(ks, tq), :].astype(BF16)
            vh = v_ref[0, pl.ds(ks, tq), :].astype(BF16)
            z = _dot_nt(q_stack, kh)
            lm, lb = _softplus_parts(z)
            if masked:
                lm = jnp.where(causal, lm, 0.0)
            between = _dot(jnp.concatenate(_split_bf16(lm), axis=1), tri2) + carry
            w = jnp.exp(lb + between)
            if masked:
                w = jnp.where(causal, w, 0.0)
            return carry + jnp.sum(lm, axis=1, keepdims=True), acc + _dot(w.astype(BF16), vh)

        carry, acc = tiles(i, jnp.zeros((rows, 1), F32), jnp.zeros((rows, LANES), F32), True)

        def more(c):
            return jnp.logical_and(c[0] >= 0, jnp.max(c[1]) >= EXP_ZERO_BELOW)

        kj_end, carry, acc = lax.while_loop(
            more, lambda c: (c[0] - 1,) + tiles(c[0], c[1], c[2], False), (i - 1, carry, acc))
        o_out, b_out = acc[0:tq], jnp.broadcast_to(carry[0:tq], (tq, LANES))
        for hh in range(1, HEADS_PER_BLOCK):
            o_out = jnp.where(in_heads[hh], acc[hh * tq:(hh + 1) * tq], o_out)
            b_out = jnp.where(in_heads[hh], carry[hh * tq:(hh + 1) * tq], b_out)
        o_ref[...] = o_out
        bs_ref[0] = b_out
        kf_ref[0, 0] = jnp.zeros((8, LANES), F32) + (kj_end + 1).astype(F32)

    nq = t // tq
    return pl.pallas_call(
        body, name="attn_fwd",
        grid=(N_HEAD_BLOCKS, nq),
        in_specs=[pl.BlockSpec((1, tq, LANES), lambda p, i: (U_Q, i, p)),
                  pl.BlockSpec((1, t, LANES), lambda p, i: (U_K, 0, p)),
                  pl.BlockSpec((1, t, LANES), lambda p, i: (U_V, 0, p))],
        out_specs=[pl.BlockSpec((tq, LANES), lambda p, i: (i, p)),
                   pl.BlockSpec((1, tq, LANES), lambda p, i: (p, i, 0)),
                   pl.BlockSpec((1, 1, 8, LANES), lambda p, i: (p, i, 0, 0))],
        out_shape=[S((t, UNIT), F32), S((N_HEAD_BLOCKS, t, LANES), F32), S((N_HEAD_BLOCKS, nq, 8, LANES), F32)],
        compiler_params=_params(2),
    )(proj, proj, proj)


def _attn_bwd(proj, d_o, bsum, kfirst):
    t = proj.shape[1]
    tq = _attn_tile_size(t)
    nq = t // tq

    def body(q_ref, k_ref, v_ref, do_ref, bs_ref, kf_ref, dq_ref, dk_ref, dv_ref, dkt_ref, dvt_ref):
        i = pl.program_id(1)

        @pl.when(i == 0)
        def _():
            dkt_ref[...] = jnp.zeros_like(dkt_ref)
            dvt_ref[...] = jnp.zeros_like(dvt_ref)

        row = lax.broadcasted_iota(jnp.int32, (tq, tq), 0)
        col = lax.broadcasted_iota(jnp.int32, (tq, tq), 1)
        upto = (row <= col).astype(BF16)
        before = (row < col).astype(BF16)
        upto2 = jnp.concatenate([upto, upto], axis=0)
        before2 = jnp.concatenate([before, before], axis=0)
        causal = jnp.concatenate([col < row] * HEADS_PER_BLOCK, axis=0)
        lane = lax.broadcasted_iota(jnp.int32, (1, LANES), 1)
        q_all = q_ref[0] * ATTN_SCALE
        do_all = do_ref[...].astype(F32)
        bs_all = bs_ref[0]
        k_first = jnp.clip(jnp.max(kf_ref[0, 0]).astype(jnp.int32), 0, i)

        in_heads = [(lane >= hh * HEAD_DIM) & (lane < (hh + 1) * HEAD_DIM) for hh in range(HEADS_PER_BLOCK)]
        q_f32 = jnp.concatenate([jnp.where(in_head, q_all, 0.0) for in_head in in_heads], axis=0)
        do_f32 = jnp.concatenate([jnp.where(in_head, do_all, 0.0) for in_head in in_heads], axis=0)
        q_stack, do_stack = q_f32.astype(BF16), do_f32.astype(BF16)
        q_t, do_t = q_f32.T.astype(BF16), do_f32.T.astype(BF16)
        btot = jnp.concatenate([jnp.max(jnp.where(in_head, bs_all, -jnp.inf), axis=1, keepdims=True)
                                for in_head in in_heads], axis=0)
        rows = HEADS_PER_BLOCK * tq

        def tiles(kj, c_b, c_p, dq, masked):
            ks = pl.multiple_of(kj * tq, tq)
            kh = k_ref[0, pl.ds(ks, tq), :].astype(BF16)
            vh = v_ref[0, pl.ds(ks, tq), :].astype(BF16)
            z = _dot_nt(q_stack, kh)
            lm, lb = _softplus_parts(z)
            if masked:
                lm = jnp.where(causal, lm, 0.0)
            between = btot - (c_b + _dot(jnp.concatenate(_split_bf16(lm), axis=1), upto2))
            w = jnp.exp(lb + between)
            if masked:
                w = jnp.where(causal, w, 0.0)
            e = w * _dot_nt(do_stack, vh)
            p_sum = c_p + _dot(jnp.concatenate(_split_bf16(e), axis=1), before2)
            beta = jnp.exp(lb)
            dz = e * (1.0 - beta) - p_sum * beta
            if masked:
                dz = jnp.where(causal, dz, 0.0)
            dzb = dz.astype(BF16)
            dkt_ref[kj] += _dot(q_t, dzb)
            dvt_ref[kj] += _dot(do_t, w.astype(BF16))
            return (c_b + jnp.sum(lm, axis=1, keepdims=True), c_p + jnp.sum(e, axis=1, keepdims=True),
                    dq + _dot(dzb, kh))

        zero = (jnp.zeros((rows, 1), F32), jnp.zeros((rows, 1), F32), jnp.zeros((rows, LANES), F32))
        state = lax.fori_loop(k_first, i, lambda kj, st: tiles(kj, *st, False), zero)
        dq = tiles(i, *state, True)[2]
        dq_out = dq[0:tq]
        for hh in range(1, HEADS_PER_BLOCK):
            dq_out = jnp.where(in_heads[hh], dq[hh * tq:(hh + 1) * tq], dq_out)
        dq_ref[...] = (dq_out * ATTN_SCALE).astype(BF16)

        @pl.when(i == nq - 1)
        def _():
            for kj in range(nq):
                dk_ref[kj * tq:(kj + 1) * tq, :] = dkt_ref[kj].T.astype(BF16)
                dv_ref[kj * tq:(kj + 1) * tq, :] = dvt_ref[kj].T.astype(BF16)

    return pl.pallas_call(
        body, name="attn_bwd",
        grid=(N_HEAD_BLOCKS, nq),
        in_specs=[pl.BlockSpec((1, tq, LANES), lambda p, i: (U_Q, i, p)),
                  pl.BlockSpec((1, t, LANES), lambda p, i: (U_K, 0, p)),
                  pl.BlockSpec((1, t, LANES), lambda p, i: (U_V, 0, p)),
                  pl.BlockSpec((tq, LANES), lambda p, i: (i, p)),
                  pl.BlockSpec((1, tq, LANES), lambda p, i: (p, i, 0)),
                  pl.BlockSpec((1, 1, 8, LANES), lambda p, i: (p, i, 0, 0))],
        out_specs=[pl.BlockSpec((tq, LANES), lambda p, i: (i, p)),
                   pl.BlockSpec((t, LANES), lambda p, i: (0, p)),
                   pl.BlockSpec((t, LANES), lambda p, i: (0, p))],
        out_shape=[S((t, UNIT), BF16)] * 3,
        scratch_shapes=[pltpu.VMEM((nq, LANES, tq), F32), pltpu.VMEM((nq, LANES, tq), F32)],
        compiler_params=_params(2),
    )(proj, proj, proj, d_o, bsum, kfirst)


N_MIX_VIEWS = 11
N_MIX_VIEWS_NO_CONV = 7


def _mix_in_specs(tb, conv_inputs=True):
    hb = tb // HALO

    def unit(u):
        return pl.BlockSpec((1, tb, UNIT), lambda i: (u, i, 0))

    def halo(u):
        return pl.BlockSpec((1, HALO, UNIT), lambda i: (u, jnp.maximum(i * hb - 1, 0), 0))

    conv = [unit(U_CA), halo(U_CA), unit(U_CB), halo(U_CB)] if conv_inputs else []
    return [unit(U_P), halo(U_P), unit(U_GP)] + conv + [
        unit(U_GC), unit(U_GA), pl.BlockSpec((3, tb, UNIT), lambda i: (3, i, 0)),
        pl.BlockSpec((3, tb, UNIT), lambda i: (4, i, 0))]


def _mix_weight_specs():
    return [_const_spec((4, GROUP, GROUP)), _const_spec((1, UNIT)), _const_spec((1, UNIT)),
            _const_spec((UNIT, D_MODEL)), _const_spec((CONV_K_PAD, UNIT)), _const_spec((1, UNIT)),
            _const_spec((1, UNIT)), _const_spec((1, UNIT)), _const_spec((UNIT, D_MODEL)),
            _const_spec((UNIT, D_MODEL)), _const_spec((D_MODEL, D_MODEL)), _const_spec((1, D_MODEL))]


def _shifted_reader(buf, rbuf, tb):
    if rbuf is None:
        return lambda s: buf[s:s + tb, :]
    length = tb + HALO - 8
    for b in range(1, 8):
        rbuf[b, :, :] = buf[b:b + length, :]

    def read(s):
        a, b = divmod(s, 8)
        return buf[s:s + tb, :] if b == 0 else rbuf[b, 8 * a:8 * a + tb, :]

    return read


def _mix_forward(i, tb, proj_refs, o_ref, w_refs, pbuf, ubuf=None, rbuf=None, cv_ref=None):
    if cv_ref is None:
        p_ref, ph_ref, gp_ref, ca_ref, cah_ref, cb_ref, cbh_ref, gc_ref, ga_ref, gm0_ref, gm1_ref = proj_refs
    else:
        p_ref, ph_ref, gp_ref, gc_ref, ga_ref, gm0_ref, gm1_ref = proj_refs
    (poolw_ref, poolb_ref, pscale_ref, wpo_ref, convw_ref, convb_ref, lng_ref, lnb_ref,
     wco_ref, wao_ref, wo_ref, gpost_ref) = w_refs
    first = i == 0
    r = {}

    pbuf[0:HALO, :] = jnp.where(first, 0.0, ph_ref[0])
    pbuf[HALO:HALO + tb, :] = p_ref[0]
    tpos = i * tb + lax.broadcasted_iota(jnp.int32, (tb, 1), 0)
    d_parts, y_parts = [], []
    for g, win in enumerate(POOL_WINDOWS):
        cs = slice(g * GROUP, (g + 1) * GROUP)
        cur = pbuf[HALO:HALO + tb, cs]
        s = cur
        for j in range(1, win):
            s = s + pbuf[HALO - j:HALO - j + tb, cs]
        cnt = jnp.minimum(tpos + 1, win).astype(F32)
        d_g = s / cnt - cur
        d_parts.append(d_g)
        y_parts.append(_dot(d_g.astype(BF16), poolw_ref[g]))
    r["d"] = d_parts
    y = jnp.concatenate(y_parts, axis=1) + poolb_ref[...]
    r["y"] = y
    mp = y * pscale_ref[...]
    gp = gp_ref[0]
    sgp = _sigmoid(gp)
    r["mp"], r["gp"], r["sgp"] = mp, gp, sgp
    ua = mp * (gp * sgp)

    if cv_ref is None:
        ah, bh = cah_ref[0], cbh_ref[0]
        ubuf[0:HALO, :] = jnp.where(first, 0.0, ah * _sigmoid(bh))
        ubuf[HALO:HALO + tb, :] = ca_ref[0] * _sigmoid(cb_ref[0])
        cv = jnp.zeros((tb, UNIT), F32) + convb_ref[...]
        off = HALO - (CONV_K - 1)
        u_at = _shifted_reader(ubuf, rbuf, tb)
        for k in range(CONV_K):
            cv = cv + convw_ref[k:k + 1, :] * u_at(off + k)
    else:
        cv = cv_ref[...]
    r["cv"] = cv
    mu = jnp.mean(cv, axis=-1, keepdims=True)
    cc = cv - mu
    rs = lax.rsqrt(jnp.mean(cc * cc, axis=-1, keepdims=True) + LN_EPS)
    nrm = cc * rs
    ln = nrm * lng_ref[...] + lnb_ref[...]
    sln = _sigmoid(ln)
    sc = ln * sln
    gc = gc_ref[0]
    sgc = _sigmoid(gc)
    r["rs"], r["nrm"], r["ln"], r["sln"], r["sc"], r["gc"], r["sgc"] = rs, nrm, ln, sln, sc, gc, sgc
    ub = sc * (gc * sgc)

    o = o_ref[...]
    ga = ga_ref[0]
    sga = _sigmoid(ga)
    r["o"], r["ga"], r["sga"] = o, ga, sga
    uc = o * (ga * sga)

    r["ua"], r["ub"], r["uc"] = ua.astype(BF16), ub.astype(BF16), uc.astype(BF16)
    ya = _dot(r["ua"], wpo_ref[...])
    yb = _dot(r["ub"], wco_ref[...])
    yc = _dot(r["uc"], wao_ref[...])
    g0 = _sigmoid(jnp.concatenate([gm0_ref[0], gm0_ref[1]], axis=1))
    g1 = _sigmoid(jnp.concatenate([gm0_ref[2], gm1_ref[0]], axis=1))
    g2 = _sigmoid(jnp.concatenate([gm1_ref[1], gm1_ref[2]], axis=1))
    r["ya"], r["yb"], r["yc"], r["g0"], r["g1"], r["g2"] = ya, yb, yc, g0, g1, g2
    m = (g0 * ya + g1 * yb + g2 * yc).astype(BF16)
    r["m"] = m
    out = _dot(m, wo_ref[...])
    r2 = lax.rsqrt(jnp.mean(out * out, axis=-1, keepdims=True) + RMS_EPS)
    r["n2"], r["r2"] = out * r2, r2
    return r


def _mix_fwd(proj, o, x, weights):
    t = x.shape[0]
    tb = min(256, t)
    nv = N_MIX_VIEWS

    def body(*refs):
        proj_refs, o_ref, x_ref = refs[0:nv], refs[nv], refs[nv + 1]
        w_refs = refs[nv + 2:nv + 14]
        xn_ref, cv_ref, pbuf, ubuf, rbuf = refs[nv + 14:nv + 19]
        r = _mix_forward(pl.program_id(0), tb, proj_refs, o_ref, w_refs, pbuf, ubuf, rbuf)
        xn_ref[...] = x_ref[...] + r["n2"] * w_refs[11][...]
        cv_ref[...] = r["cv"]

    return pl.pallas_call(
        body, name="mix_fwd",
        grid=(t // tb,),
        in_specs=_mix_in_specs(tb) + [pl.BlockSpec((tb, UNIT), lambda i: (i, 0)),
                                      pl.BlockSpec((tb, D_MODEL), lambda i: (i, 0))] + _mix_weight_specs(),
        out_specs=[pl.BlockSpec((tb, D_MODEL), lambda i: (i, 0)), pl.BlockSpec((tb, UNIT), lambda i: (i, 0))],
        out_shape=[S((t, D_MODEL), F32), S((t, UNIT), F32)],
        scratch_shapes=[pltpu.VMEM((HALO + tb, UNIT), F32), pltpu.VMEM((HALO + tb, UNIT), F32),
                        pltpu.VMEM((8, HALO + tb - 8, UNIT), F32)],
        compiler_params=_params(1),
    )(*([proj] * nv), o, x, *weights)


def _mix_bwd(proj, o, cv, x, dxn, weights):
    t = x.shape[0]
    tb = min(256, t)
    nt = t // tb

    nv = N_MIX_VIEWS_NO_CONV

    def body(*refs):
        proj_refs, o_ref, cv_ref, x_ref, dxn_ref = refs[0:nv], refs[nv], refs[nv + 1], refs[nv + 2], refs[nv + 3]
        w_refs = refs[nv + 4:nv + 16]
        (dg_ref, dd_ref, dcv_ref, do_ref, dwo_hbm, dwp_hbm, dwc_hbm, dwa_hbm, dpw_ref,
         dvec_ref) = refs[nv + 16:nv + 26]
        pbuf, dwo_ref, dwp_ref, dwc_ref, dwa_ref = refs[nv + 26:nv + 31]
        (poolw_ref, _, pscale_ref, wpo_ref, _, _, lng_ref, _, wco_ref, wao_ref, wo_ref, gpost_ref) = w_refs
        i = pl.program_id(0)

        @pl.when(i == 0)
        def _():
            for ref in (dwo_ref, dwp_ref, dwc_ref, dwa_ref, dpw_ref, dvec_ref):
                ref[...] = jnp.zeros_like(ref)

        r = _mix_forward(i, tb, proj_refs, o_ref, w_refs, pbuf, cv_ref=cv_ref)

        def colsum(v):
            return jnp.sum(v, axis=0, keepdims=True)

        dxn = dxn_ref[...]
        n2 = r["n2"]
        dvec_ref[0:1, :] += colsum(dxn * n2)
        dn2 = dxn * gpost_ref[...]
        dout = (r["r2"] * (dn2 - n2 * jnp.mean(dn2 * n2, axis=-1, keepdims=True))).astype(BF16)
        dm = _dot_nt(dout, wo_ref[...])
        dwo_ref[...] += _dot_tn(r["m"], dout)

        g0, g1, g2 = r["g0"], r["g1"], r["g2"]
        dgm = [dm * r["ya"] * g0 * (1.0 - g0), dm * r["yb"] * g1 * (1.0 - g1), dm * r["yc"] * g2 * (1.0 - g2)]
        for bidx in range(N_BRANCH):
            dg_ref[3 + 2 * bidx] = dgm[bidx][:, 0:UNIT].astype(BF16)
            dg_ref[4 + 2 * bidx] = dgm[bidx][:, UNIT:2 * UNIT].astype(BF16)
        dya = (dm * g0).astype(BF16)
        dyb = (dm * g1).astype(BF16)
        dyc = (dm * g2).astype(BF16)
        dua = _dot_nt(dya, wpo_ref[...])
        dub = _dot_nt(dyb, wco_ref[...])
        duc = _dot_nt(dyc, wao_ref[...])
        dwp_ref[...] += _dot_tn(r["ua"], dya)
        dwc_ref[...] += _dot_tn(r["ub"], dyb)
        dwa_ref[...] += _dot_tn(r["uc"], dyc)

        gp, sgp = r["gp"], r["sgp"]
        dmp = dua * (gp * sgp)
        dg_ref[0] = (dua * r["mp"] * _dsilu(gp, sgp)).astype(BF16)
        dvec_ref[2:3, 0:UNIT] += colsum(dmp * r["y"])
        dy = dmp * pscale_ref[...]
        dvec_ref[1:2, 0:UNIT] += colsum(dy)
        dd_parts = []
        for g in range(len(POOL_WINDOWS)):
            dy_g = dy[:, g * GROUP:(g + 1) * GROUP].astype(BF16)
            dd_parts.append(_dot_nt(dy_g, poolw_ref[g]))
            dpw_ref[g] += _dot_tn(r["d"][g].astype(BF16), dy_g)
        dd_ref[...] = jnp.concatenate(dd_parts, axis=1)

        gc, sgc = r["gc"], r["sgc"]
        dsc = dub * (gc * sgc)
        dg_ref[1] = (dub * r["sc"] * _dsilu(gc, sgc)).astype(BF16)
        dln = dsc * _dsilu(r["ln"], r["sln"])
        nrm = r["nrm"]
        dvec_ref[4:5, 0:UNIT] += colsum(dln * nrm)
        dvec_ref[5:6, 0:UNIT] += colsum(dln)
        dnrm = dln * lng_ref[...]
        dcv = r["rs"] * (dnrm - jnp.mean(dnrm, axis=-1, keepdims=True)
                         - nrm * jnp.mean(dnrm * nrm, axis=-1, keepdims=True))
        dvec_ref[3:4, 0:UNIT] += colsum(dcv)
        dcv_ref[...] = dcv

        ga, sga = r["ga"], r["sga"]
        do_ref[...] = (duc * (ga * sga)).astype(BF16)
        dg_ref[2] = (duc * r["o"] * _dsilu(ga, sga)).astype(BF16)

        @pl.when(i == nt - 1)
        def _():
            for acc, hbm in ((dwo_ref, dwo_hbm), (dwp_ref, dwp_hbm), (dwc_ref, dwc_hbm), (dwa_ref, dwa_hbm)):
                pltpu.sync_copy(acc, hbm)

    def acc_spec(shape):
        n = len(shape)
        return pl.BlockSpec(shape, lambda i: (0,) * n)

    tok = lambda w: pl.BlockSpec((tb, w), lambda i: (i, 0))
    any_spec = pl.BlockSpec(memory_space=pl.ANY)
    w_shapes = [(D_MODEL, D_MODEL), (UNIT, D_MODEL), (UNIT, D_MODEL), (UNIT, D_MODEL)]
    return pl.pallas_call(
        body, name="mix_bwd",
        grid=(nt,),
        in_specs=_mix_in_specs(tb, conv_inputs=False) + [tok(UNIT), tok(UNIT), tok(D_MODEL), tok(D_MODEL)]
        + _mix_weight_specs(),
        out_specs=[pl.BlockSpec((9, tb, UNIT), lambda i: (0, i, 0)), tok(UNIT), tok(UNIT), tok(UNIT)]
        + [any_spec] * 4 + [acc_spec((4, GROUP, GROUP)), acc_spec((8, D_MODEL))],
        out_shape=[S((9, t, UNIT), BF16), S((t, UNIT), F32), S((t, UNIT), F32), S((t, UNIT), BF16)]
        + [S(s, F32) for s in w_shapes] + [S((4, GROUP, GROUP), F32), S((8, D_MODEL), F32)],
        scratch_shapes=[pltpu.VMEM((HALO + tb, UNIT), F32)] + [pltpu.VMEM(s, F32) for s in w_shapes],
        compiler_params=_params(1),
    )(*([proj] * nv), o, cv, x, dxn, *weights)


def _halo_bwd(proj, dcv, dd, conv_w):
    t = dcv.shape[0]
    tb = min(256, t)
    hb = tb // HALO
    n_halo_blocks = t // HALO
    nt = t // tb

    def body(ca_ref, cb_ref, dcv_ref, dcvh_ref, dd_ref, ddh_ref, cw_ref,
             dpre_ref, dcw_ref, gbuf, nbuf, rg_buf, acc_ref):
        i = pl.program_id(0)
        last = i == nt - 1

        @pl.when(i == 0)
        def _():
            acc_ref[...] = jnp.zeros_like(acc_ref)

        a, b = ca_ref[0], cb_ref[0]
        sb = _sigmoid(b)
        u = a * sb
        gbuf[0:tb, :] = dcv_ref[...]
        gbuf[tb:tb + HALO, :] = jnp.where(last, 0.0, dcvh_ref[...])

        g_at = _shifted_reader(gbuf, rg_buf, tb)
        du = jnp.zeros((tb, UNIT), F32)
        for k in range(CONV_K):
            ahead = g_at(CONV_K - 1 - k)
            du = du + cw_ref[k:k + 1, :] * ahead
            acc_ref[k] += jnp.sum((ahead * u).reshape(tb // 8, 8, UNIT), axis=0)
        dpre_ref[1] = (du * sb).astype(BF16)
        dpre_ref[2] = (du * a * sb * (1.0 - sb)).astype(BF16)

        @pl.when(last)
        def _():
            dcw_ref[...] = jnp.sum(acc_ref[...], axis=1)

        tpos = i * tb + lax.broadcasted_iota(jnp.int32, (tb + HALO, 1), 0)
        dd_main = dd_ref[...]
        dd_ext = jnp.concatenate([dd_main, jnp.where(last, 0.0, ddh_ref[...])], axis=0)
        dp_parts = []
        for g, win in enumerate(POOL_WINDOWS):
            cs = slice(g * GROUP, (g + 1) * GROUP)
            cnt = jnp.minimum(tpos + 1, win).astype(F32)
            nbuf[:, cs] = dd_ext[:, cs] / cnt
        for g, win in enumerate(POOL_WINDOWS):
            cs = slice(g * GROUP, (g + 1) * GROUP)
            s = nbuf[0:tb, cs]
            for j in range(1, win):
                s = s + nbuf[j:j + tb, cs]
            dp_parts.append(s - dd_main[:, cs])
        dpre_ref[0] = jnp.concatenate(dp_parts, axis=1).astype(BF16)

    def unit(u):
        return pl.BlockSpec((1, tb, UNIT), lambda i: (u, i, 0))

    tok = pl.BlockSpec((tb, UNIT), lambda i: (i, 0))
    future = pl.BlockSpec((HALO, UNIT), lambda i: (jnp.minimum((i + 1) * hb, n_halo_blocks - 1), 0))
    return pl.pallas_call(
        body, name="halo_bwd",
        grid=(nt,),
        in_specs=[unit(U_CA), unit(U_CB), tok, future, tok, future, _const_spec((CONV_K_PAD, UNIT))],
        out_specs=[pl.BlockSpec((3, tb, UNIT), lambda i: (0, i, 0)),
                   pl.BlockSpec((CONV_K_PAD, UNIT), lambda i: (0, 0))],
        out_shape=[S((3, t, UNIT), BF16), S((CONV_K_PAD, UNIT), F32)],
        scratch_shapes=[pltpu.VMEM((HALO + tb, UNIT), F32)] * 2
        + [pltpu.VMEM((8, HALO + tb - 8, UNIT), F32), pltpu.VMEM((CONV_K_PAD, 8, UNIT), F32)],
        compiler_params=_params(1),
    )(proj, proj, dcv, dcv, dd, dd, conv_w)


def _loss_head(y, target):
    t = y.shape[0]
    tb = min(512, t)

    def body(y_ref, t_ref, loss_ref, dy_ref):
        @pl.when(pl.program_id(0) == 0)
        def _():
            loss_ref[...] = jnp.zeros_like(loss_ref)

        err = y_ref[...] - t_ref[...]
        dy_ref[...] = err * (1.0 / D_MODEL)
        part = 0.5 * jnp.sum(jnp.mean(err * err, axis=-1, keepdims=True), axis=0, keepdims=True)
        r8 = lax.broadcasted_iota(jnp.int32, (8, LANES), 0)
        c8 = lax.broadcasted_iota(jnp.int32, (8, LANES), 1)
        loss_ref[...] += jnp.where((r8 == 0) & (c8 == 0), part, 0.0)

    return pl.pallas_call(
        body, name="loss_head",
        grid=(t // tb,),
        in_specs=[pl.BlockSpec((tb, D_MODEL), lambda i: (i, 0))] * 2,
        out_specs=[pl.BlockSpec((8, LANES), lambda i: (0, 0)), pl.BlockSpec((tb, D_MODEL), lambda i: (i, 0))],
        out_shape=[S((8, LANES), F32), S((t, D_MODEL), F32)],
        compiler_params=_params(1),
    )(y, target)


def _adamw(name, parts, w, m, v):
    n, rows, cols = parts.shape
    rb = rows
    while rb * cols * 4 * (n + 7) * 2 > 24 * 1024 * 1024 and rb % 16 == 0:
        rb //= 2

    def body(p_ref, w_ref, m_ref, v_ref, g_ref, d_ref, nm_ref, nv_ref):
        g = p_ref[0]
        for k in range(1, n):
            g = g + p_ref[k]
        nm = ADAM_B1 * m_ref[...] + (1.0 - ADAM_B1) * g
        nv = ADAM_B2 * v_ref[...] + (1.0 - ADAM_B2) * (g * g)
        m_hat = nm / (1.0 - ADAM_B1 ** ADAM_STEP)
        v_hat = nv / (1.0 - ADAM_B2 ** ADAM_STEP)
        g_ref[...] = g
        d_ref[...] = -ADAM_LR * (m_hat / (jnp.sqrt(v_hat) + ADAM_EPS) + ADAM_WD * w_ref[...])
        nm_ref[...] = nm
        nv_ref[...] = nv

    blk = pl.BlockSpec((rb, cols), lambda i: (i, 0))
    return pl.pallas_call(
        body, name=name,
        grid=(rows // rb,),
        in_specs=[pl.BlockSpec((n, rb, cols), lambda i: (0, i, 0)), blk, blk, blk],
        out_specs=[blk] * 4,
        out_shape=[S((rows, cols), F32)] * 4,
        compiler_params=_params(1),
    )(parts, w, m, v)


def _axis_slice(ref, axis, block, size):
    idx = [slice(None)] * len(ref.shape)
    idx[axis] = pl.ds(pl.multiple_of(block * size, size), size)
    return ref.at[tuple(idx)]


def _axis_index(ref, axis, index):
    return ref.at[tuple([slice(None)] * axis + [index])]


COPY_PIECE_BYTES = 2 * 1024 * 1024


def _row_pieces(shape):
    rows = shape[-2]
    size = 4 * rows * shape[-1]
    for d in shape[:-2]:
        size *= d
    n = 1
    while size // n > COPY_PIECE_BYTES and rows % (2 * n * 8) == 0:
        n *= 2
    return [(k * (rows // n), rows // n) for k in range(n)]


def _rows(ref, r0, rows):
    idx = [slice(None)] * len(ref.shape)
    idx[-2] = pl.ds(r0, rows)
    return ref.at[tuple(idx)]


def _staged_local_copies(pairs, bufs, in_sems, out_sems):
    ins = [pltpu.make_async_copy(src, buf, in_sems.at[k]) for k, ((src, _), buf) in enumerate(zip(pairs, bufs))]
    for cp in ins:
        cp.start()
    for cp in ins:
        cp.wait()
    outs = [pltpu.make_async_copy(buf, dst, out_sems.at[k]) for k, ((_, dst), buf) in enumerate(zip(pairs, bufs))]
    for cp in outs:
        cp.start()
    return outs


def _run_copies(local, sends, recvs):
    for cp in local + sends:
        cp.start()
    for cp in recvs:
        cp.wait_recv()
    for cp in sends:
        cp.wait_send()
    for cp in local:
        cp.wait()


def _gather_weights(shards, axes):
    n = len(shards)
    fulls = [S(tuple(4 * d if k == ax else d for k, d in enumerate(s.shape)), s.dtype) for s, ax in zip(shards, axes)]

    def body(*refs):
        src, dst = refs[0:n], refs[n:2 * n]
        send_sems, recv_sems, pass_send_sems, pass_recv_sems, in_sems, out_sems = refs[2 * n:2 * n + 6]
        bufs = refs[2 * n + 6:3 * n + 6]
        x, y, c = lax.axis_index("x"), lax.axis_index("y"), lax.axis_index("c")
        chips = [(1 - x, y), (x, 1 - y), (1 - x, 1 - y)]

        def place(a, chip, layer=None):
            ref = _axis_slice(dst[a], axes[a], 2 * chip[0] + chip[1], src[a].shape[axes[a]])
            return ref if layer is None else ref.at[pl.ds(layer, 1)]

        def over_ici(k, chip_of_block, to):
            a, j, r0, rows = pieces[k]
            return pltpu.make_async_remote_copy(
                src_ref=_rows(src[a].at[pl.ds(c, 1)], r0, rows), dst_ref=_rows(place(a, chip_of_block, c), r0, rows),
                send_sem=send_sems.at[k], recv_sem=recv_sems.at[k], device_id=(to[0], to[1], c), device_id_type=MESH)

        def to_sibling(k, chip_of_block, layer):
            a, j, r0, rows = pieces[k]
            block = _rows(place(a, chip_of_block, layer), r0, rows)
            return pltpu.make_async_remote_copy(
                src_ref=block, dst_ref=block, send_sem=pass_send_sems.at[k], recv_sem=pass_recv_sems.at[k],
                device_id=(x, y, 1 - c), device_id_type=MESH)

        first = [over_ici(k, (x, y), chips[j]) for k, (a, j, r0, rows) in enumerate(pieces)]
        for cp in first:
            cp.start()
        local = _staged_local_copies([(src[a], place(a, (x, y))) for a in range(n)], bufs, in_sems, out_sems)
        passed = []
        for k, (a, j, r0, rows) in enumerate(pieces):
            over_ici(k, chips[j], chips[j]).wait_recv()
            passed.append(to_sibling(k, chips[j], c))
            passed[-1].start()
        for k, (a, j, r0, rows) in enumerate(pieces):
            to_sibling(k, chips[j], 1 - c).wait_recv()
        for cp in first + passed:
            cp.wait_send()
        for cp in local:
            cp.wait()

    per_array = [_row_pieces((1,) + shards[a].shape[1:]) for a in range(n)]
    pieces = [(a, j, r0, rows) for a in range(n) for r0, rows in per_array[a] for j in range(3)]
    any_spec = pl.BlockSpec(memory_space=pl.ANY)
    return pl.pallas_call(
        body, name="gather_weights",
        in_specs=[any_spec] * n, out_specs=[any_spec] * n, out_shape=fulls,
        scratch_shapes=[pltpu.SemaphoreType.DMA((len(pieces),))] * 4 + [pltpu.SemaphoreType.DMA((n,))] * 2
        + [pltpu.VMEM(s.shape, s.dtype) for s in shards],
        compiler_params=_params(0),
    )(*shards)


def _swap_layers(layers):
    n = len(layers)
    pieces = [(a, r0, rows) for a in range(n) for r0, rows in _row_pieces(layers[a][0].shape)]

    def body(*refs):
        src = [refs[2 * a:2 * a + 2] for a in range(n)]
        dst = refs[2 * n:3 * n]
        send_sems, recv_sems = refs[3 * n:3 * n + 2]
        x, y, c = lax.axis_index("x"), lax.axis_index("y"), lax.axis_index("c")

        def remote(k, layer):
            a, r0, rows = pieces[k]
            return pltpu.make_async_remote_copy(
                src_ref=_rows(src[a][layer], r0, rows), dst_ref=_rows(dst[a], r0, rows),
                send_sem=send_sems.at[k], recv_sem=recv_sems.at[k], device_id=(x, y, 1 - c), device_id_type=MESH)

        for layer in range(2):
            @pl.when(c == 1 - layer)
            def _(layer=layer):
                for k in range(len(pieces)):
                    remote(k, layer).start()

        for k in range(len(pieces)):
            remote(k, 0).wait_recv()
        for k in range(len(pieces)):
            remote(k, 0).wait_send()

    any_spec = pl.BlockSpec(memory_space=pl.ANY)
    return pl.pallas_call(
        body, name="swap_layers",
        in_specs=[any_spec] * (2 * n), out_specs=[any_spec] * n,
        out_shape=[S(layers[a][0].shape, layers[a][0].dtype) for a in range(n)],
        scratch_shapes=[pltpu.SemaphoreType.DMA((len(pieces),))] * 2,
    )(*[arr for pair in layers for arr in pair])


def _add_own_layer(name, mine, theirs, layer):
    rows, cols = theirs.shape
    rb = rows
    while rb * cols * 4 * 4 * 2 > 24 * 1024 * 1024 and rb % 32 == 0:
        rb //= 2

    def body(layer_ref, m0_ref, m1_ref, t_ref, o_ref):
        own = jnp.where(layer_ref[0] == 0, m0_ref[...], m1_ref[...])
        o_ref[0] = (own + t_ref[...]).astype(BF16)

    def own_spec(which):
        return pl.BlockSpec((rb, cols), lambda i, l: (jnp.where(l[0] == which, i, 0), 0))

    return pl.pallas_call(
        body, name=name,
        grid_spec=pltpu.PrefetchScalarGridSpec(
            num_scalar_prefetch=1, grid=(rows // rb,),
            in_specs=[own_spec(0), own_spec(1), pl.BlockSpec((rb, cols), lambda i, l: (i, 0))],
            out_specs=pl.BlockSpec((1, rb, cols), lambda i, l: (0, i, 0))),
        out_shape=S((1, rows, cols), BF16),
        compiler_params=_params(1),
    )(layer, *mine, theirs)


def _send_to_owners(chip_halves, shard_axes, packed):
    n = len(chip_halves)

    def owned_shape(a):
        return tuple(d // 4 if k == shard_axes[a] else d for k, d in enumerate(chip_halves[a].shape))

    flips = [(dx, dy, dc) for dx in (0, 1) for dy in (0, 1) for dc in (0, 1)][1:]

    def body(*refs):
        src, psrc, dst, pdst = refs[0:n], refs[n], refs[n + 1:2 * n + 1], refs[2 * n + 1]
        send_sems, recv_sems, local_sems, psend_sems, precv_sems = refs[2 * n + 2:2 * n + 7]
        x, y, c = lax.axis_index("x"), lax.axis_index("y"), lax.axis_index("c")
        my_chip = 2 * x + y
        chips = [(1 - x, y), (x, 1 - y), (1 - x, 1 - y)]

        def owned(a, chip):
            return _axis_slice(src[a], shard_axes[a], 2 * chip[0] + chip[1], src[a].shape[shard_axes[a]] // 4)

        def remote(a, j, to, from_chip):
            return pltpu.make_async_remote_copy(
                src_ref=owned(a, to), dst_ref=dst[a].at[from_chip], send_sem=send_sems.at[a, j],
                recv_sem=recv_sems.at[a, j], device_id=(to[0], to[1], c), device_id_type=MESH)

        def peer(f):
            return (1 - x if f[0] else x, 1 - y if f[1] else y, 1 - c if f[2] else c)

        def premote(j, to, from_dev):
            return pltpu.make_async_remote_copy(
                src_ref=psrc, dst_ref=pdst.at[from_dev], send_sem=psend_sems.at[j], recv_sem=precv_sems.at[j],
                device_id=to, device_id_type=MESH)

        local = [pltpu.make_async_copy(_rows(owned(a, (x, y)), r0, rows), _rows(dst[a].at[my_chip], r0, rows),
                                       local_sems.at[k]) for k, (a, r0, rows) in enumerate(own_pieces)]
        local.append(pltpu.make_async_copy(psrc, pdst.at[4 * x + 2 * y + c], local_sems.at[len(own_pieces)]))
        sends = [remote(a, j, chip, my_chip) for a in range(n) for j, chip in enumerate(chips)]
        sends += [premote(j, peer(f), 4 * x + 2 * y + c) for j, f in enumerate(flips)]
        recvs = [remote(a, j, (x, y), 2 * chip[0] + chip[1]) for a in range(n) for j, chip in enumerate(chips)]
        for j, f in enumerate(flips):
            px, py, pc = peer(f)
            recvs.append(premote(j, (x, y, c), 4 * px + 2 * py + pc))
        _run_copies(local, sends, recvs)

    own_pieces = [(a, r0, rows) for a in range(n) for r0, rows in _row_pieces(owned_shape(a))]
    any_spec = pl.BlockSpec(memory_space=pl.ANY)
    return pl.pallas_call(
        body, name="send_to_owners",
        in_specs=[any_spec] * (n + 1), out_specs=[any_spec] * (n + 1),
        out_shape=[S((4,) + owned_shape(a), chip_halves[a].dtype) for a in range(n)]
        + [S((N_DEV,) + packed.shape, packed.dtype)],
        scratch_shapes=[pltpu.SemaphoreType.DMA((n, 3)), pltpu.SemaphoreType.DMA((n, 3)),
                        pltpu.SemaphoreType.DMA((len(own_pieces) + 1,)), pltpu.SemaphoreType.DMA((7,)),
                        pltpu.SemaphoreType.DMA((7,))],
    )(*chip_halves, packed)


def _join_halves(reduced, half_axes):
    n = len(reduced)

    def joined_shape(a):
        s, ax = reduced[a].shape, half_axes[a]
        return s[:ax] + (2,) + s[ax:]

    pieces = [(a, r0, rows) for a in range(n) for r0, rows in _row_pieces(reduced[a].shape)]

    def body(*refs):
        src, dst = refs[0:n], refs[n:2 * n]
        send_sems, recv_sems, in_sems, out_sems = refs[2 * n:2 * n + 4]
        bufs = refs[2 * n + 4:3 * n + 4]
        x, y, c = lax.axis_index("x"), lax.axis_index("y"), lax.axis_index("c")

        def remote(k, h):
            a, r0, rows = pieces[k]
            return pltpu.make_async_remote_copy(
                src_ref=_rows(src[a], r0, rows), dst_ref=_rows(_axis_index(dst[a], half_axes[a], h), r0, rows),
                send_sem=send_sems.at[k], recv_sem=recv_sems.at[k], device_id=(x, y, 1 - c), device_id_type=MESH)

        sends = [remote(k, c) for k in range(len(pieces))]
        for cp in sends:
            cp.start()
        own = [(src[a], _axis_index(dst[a], half_axes[a], c)) for a in range(n)]
        local = _staged_local_copies(own, bufs, in_sems, out_sems)
        for k in range(len(pieces)):
            remote(k, 1 - c).wait_recv()
        for cp in sends:
            cp.wait_send()
        for cp in local:
            cp.wait()

    any_spec = pl.BlockSpec(memory_space=pl.ANY)
    return pl.pallas_call(
        body, name="join_halves",
        in_specs=[any_spec] * n, out_specs=[any_spec] * n,
        out_shape=[S(joined_shape(a), reduced[a].dtype) for a in range(n)],
        scratch_shapes=[pltpu.SemaphoreType.DMA((len(pieces),))] * 2 + [pltpu.SemaphoreType.DMA((n,))] * 2
        + [pltpu.VMEM(r.shape, r.dtype) for r in reduced],
        compiler_params=_params(0),
    )(*reduced)


def _sum_slots(name, parts):
    n = parts.shape[0]
    shape = parts.shape[1:]
    flat = parts.reshape((n, -1, shape[-1]))
    rows, cols = flat.shape[1:]
    rb = rows
    while rb * cols * 4 * (n + 1) * 2 > 24 * 1024 * 1024 and rb % 32 == 0:
        rb //= 2

    def body(p_ref, o_ref):
        acc = p_ref[0].astype(F32)
        for k in range(1, n):
            acc = acc + p_ref[k].astype(F32)
        o_ref[...] = acc

    out = pl.pallas_call(
        body, name=name,
        grid=(rows // rb,),
        in_specs=[pl.BlockSpec((n, rb, cols), lambda i: (0, i, 0))],
        out_specs=pl.BlockSpec((rb, cols), lambda i: (i, 0)),
        out_shape=S((rows, cols), F32),
        compiler_params=_params(1),
    )(flat)
    return out.reshape(shape)


SMALL_NAMES = ("norm_pre", "pool_w", "pool_b", "pool_scale", "conv_b", "conv_ln_g", "conv_ln_b", "norm_post")
BIG_NAMES = ("w_in", "w_pool_out", "conv_w", "w_conv_out", "w_attn_out", "w_o")
SHARD_AXIS = {"w_in": 2, "w_pool_out": 2, "conv_w": 2, "w_conv_out": 2, "w_attn_out": 2, "w_o": 1}
WEIGHT_ORDER = ("norm_pre", "w_in", "pool_w", "pool_b", "pool_scale", "w_pool_out", "conv_w", "conv_b",
                "conv_ln_g", "conv_ln_b", "w_conv_out", "w_attn_out", "w_o", "norm_post")


def _pack_small(parts):
    return jnp.concatenate([parts[n].reshape(-1, LANES) for n in SMALL_NAMES], axis=0)


def _unpack_small(packed, shapes):
    out, r0 = {}, 0
    for n in SMALL_NAMES:
        size = 1
        for d in shapes[n]:
            size *= d
        rows = size // LANES
        out[n] = packed[r0:r0 + rows].reshape(shapes[n])
        r0 += rows
    return out


def _layer_weights(full, small, l):
    row = lambda a: a[l][None, :]
    return (small["pool_w"][l].astype(BF16), small["pool_b"][l].reshape(1, UNIT), row(small["pool_scale"]),
            full["w_pool_out"][l], full["conv_w"][l], row(small["conv_b"]), row(small["conv_ln_g"]),
            row(small["conv_ln_b"]), full["w_conv_out"][l], full["w_attn_out"][l], full["w_o"][l],
            row(small["norm_post"]))


def _forward_backward(x, target, full, small):
    depth = full["w_in"].shape[0]
    acts = []
    for l in range(depth):
        g_pre = small["norm_pre"][l][None, :]
        proj = _inproj_fwd(x, g_pre, full["w_in"], l)
        o, bsum, kfirst = _attn_fwd(proj)
        weights = _layer_weights(full, small, l)
        x_next, cv = _mix_fwd(proj, o, x, weights)
        acts.append((x, proj, o, cv, bsum, kfirst, weights, g_pre))
        x = x_next
    loss_tile, dx = _loss_head(x, target)

    big = {n: [None] * depth for n in BIG_NAMES}
    sm = {n: [None] * depth for n in SMALL_NAMES}
    for l in reversed(range(depth)):
        x_in, proj, o, cv, bsum, kfirst, weights, g_pre = acts[l]
        dgates, dd, dcv, d_o, dwo, dwp, dwc, dwa, dpw, dvec = _mix_bwd(proj, o, cv, x_in, dx, weights)
        dpre, dcw = _halo_bwd(proj, dcv, dd, full["conv_w"][l])
        dq, dk, dv = _attn_bwd(proj, d_o, bsum, kfirst)
        dproj_parts = (dpre, dgates, dq, dk, dv)
        dx, dg_pre = _inproj_bwd_x(x_in, g_pre, full["w_in"], l, dproj_parts, dx)
        big["w_in"][l] = _inproj_bwd_w(x_in, g_pre, dproj_parts)
        big["w_pool_out"][l], big["w_conv_out"][l], big["w_attn_out"][l], big["w_o"][l] = dwp, dwc, dwa, dwo
        big["conv_w"][l] = dcw
        sm["norm_pre"][l] = dg_pre[0]
        sm["pool_w"][l] = dpw
        sm["norm_post"][l] = dvec[0]
        sm["pool_b"][l] = dvec[1, 0:UNIT].reshape(4, GROUP)
        sm["pool_scale"][l] = dvec[2, 0:UNIT]
        sm["conv_b"][l] = dvec[3, 0:UNIT]
        sm["conv_ln_g"][l] = dvec[4, 0:UNIT]
        sm["conv_ln_b"][l] = dvec[5, 0:UNIT]
    sm = {n: jnp.stack(v) for n, v in sm.items()}
    return loss_tile, dx, big, sm


def kernel(x, norm_pre, w_in, pool_w, pool_b, pool_scale, w_pool_out, conv_w, conv_b, conv_ln_g, conv_ln_b, w_conv_out, w_attn_out, w_o, norm_post, loss_target, m_norm_pre, m_w_in, m_pool_w, m_pool_b, m_pool_scale, m_w_pool_out, m_conv_w, m_conv_b, m_conv_ln_g, m_conv_ln_b, m_w_conv_out, m_w_attn_out, m_w_o, m_norm_post, v_norm_pre, v_w_in, v_pool_w, v_pool_b, v_pool_scale, v_w_pool_out, v_conv_w, v_conv_b, v_conv_ln_g, v_conv_ln_b, v_w_conv_out, v_w_attn_out, v_w_o, v_norm_post):
    w = dict(norm_pre=norm_pre, w_in=w_in, pool_w=pool_w, pool_b=pool_b, pool_scale=pool_scale,
             w_pool_out=w_pool_out, conv_w=conv_w, conv_b=conv_b, conv_ln_g=conv_ln_g, conv_ln_b=conv_ln_b,
             w_conv_out=w_conv_out, w_attn_out=w_attn_out, w_o=w_o, norm_post=norm_post)
    m = dict(norm_pre=m_norm_pre, w_in=m_w_in, pool_w=m_pool_w, pool_b=m_pool_b, pool_scale=m_pool_scale,
             w_pool_out=m_w_pool_out, conv_w=m_conv_w, conv_b=m_conv_b, conv_ln_g=m_conv_ln_g,
             conv_ln_b=m_conv_ln_b, w_conv_out=m_w_conv_out, w_attn_out=m_w_attn_out, w_o=m_w_o,
             norm_post=m_norm_post)
    v = dict(norm_pre=v_norm_pre, w_in=v_w_in, pool_w=v_pool_w, pool_b=v_pool_b, pool_scale=v_pool_scale,
             w_pool_out=v_w_pool_out, conv_w=v_conv_w, conv_b=v_conv_b, conv_ln_g=v_conv_ln_g,
             conv_ln_b=v_conv_ln_b, w_conv_out=v_w_conv_out, w_attn_out=v_w_attn_out, w_o=v_w_o,
             norm_post=v_norm_post)
    pad_taps = lambda a: jnp.pad(a, ((0, 0), (0, CONV_K_PAD - CONV_K), (0, 0)))

    shards = [pad_taps(w[n]) if n == "conv_w" else w[n].astype(BF16) for n in BIG_NAMES]
    full = dict(zip(BIG_NAMES, _gather_weights(shards, [SHARD_AXIS[n] for n in BIG_NAMES])))
    small = {n: w[n] for n in SMALL_NAMES}

    loss_tile, dx, big, sm = _forward_backward(x[0], loss_target[0], full, small)

    packed = jnp.concatenate([_pack_small(sm), loss_tile], axis=0)
    layer_axis = [0] * len(BIG_NAMES)
    theirs = _swap_layers([big[n] for n in BIG_NAMES])
    my_layer = lax.axis_index("c").astype(jnp.int32).reshape(1)
    chip_sums = [_add_own_layer("sum_cores_" + n, big[n], t, my_layer) for n, t in zip(BIG_NAMES, theirs)]
    exchanged = _send_to_owners(chip_sums, [SHARD_AXIS[n] for n in BIG_NAMES], packed)
    reduced = [_sum_slots("sum_chips_" + n, p) for n, p in zip(BIG_NAMES, exchanged[:-1])]
    joined = _join_halves(reduced, layer_axis)
    out = {}
    for n, g in zip(BIG_NAMES, joined):
        wn, mn, vn = (pad_taps(a[n]) if n == "conv_w" else a[n] for a in (w, m, v))
        shape = wn.shape
        flat = lambda a: a.reshape((-1, shape[-1]))
        res = _adamw("adamw_" + n, g.reshape((1, -1, shape[-1])), flat(wn), flat(mn), flat(vn))
        res = [r.reshape(shape) for r in res]
        out[n] = [r[:, :CONV_K] for r in res] if n == "conv_w" else res
    parts = exchanged[-1]
    n_small_rows = parts.shape[1] - 8
    zeros_tile = jnp.zeros((8, LANES), F32)
    packs = [jnp.concatenate([_pack_small(a), zeros_tile], axis=0) for a in (w, m, v)]
    res = _adamw("adamw_small", parts, *packs)
    loss = res[0][n_small_rows, 0]
    shapes = {n: w[n].shape for n in SMALL_NAMES}
    unpacked = [_unpack_small(r[:n_small_rows], shapes) for r in res]
    for n in SMALL_NAMES:
        out[n] = [u[n] for u in unpacked]

    grads = [out[n][0] for n in WEIGHT_ORDER]
    deltas = [out[n][1] for n in WEIGHT_ORDER]
    new_m = [out[n][2] for n in WEIGHT_ORDER]
    new_v = [out[n][3] for n in WEIGHT_ORDER]
    return (loss, dx[None], *grads, *deltas, *new_m, *new_v)
```

```python
import jax
import jax.numpy as jnp
from jax import lax
from jax.experimental import pallas as pl
from jax.experimental.pallas import tpu as pltpu

F32 = jnp.float32
BF16 = jnp.bfloat16

D_MODEL = 1024
UNIT = 512
N_UNITS = 15
IN_WIDTH = UNIT * N_UNITS
N_HEADS = 8
HEAD_DIM = 64
HEADS_PER_BLOCK = 2
N_HEAD_BLOCKS = N_HEADS // HEADS_PER_BLOCK
LANES = 128
CONV_K = 31
CONV_K_PAD = 32
HALO = 32
POOL_WINDOWS = (2, 4, 8, 16)
GROUP = 128
N_BRANCH = 3
RMS_EPS = 1e-6
LN_EPS = 1e-5
ATTN_SCALE = 0.125
EXP_ZERO_BELOW = -104.0

ADAM_LR = 0.001
ADAM_B1 = 0.9
ADAM_B2 = 0.999
ADAM_EPS = 1e-08
ADAM_WD = 0.01
ADAM_STEP = 10

U_P, U_GP, U_CA, U_CB, U_GC, U_Q, U_K, U_V, U_GA, U_GM = 0, 1, 2, 3, 4, 5, 6, 7, 8, 9

V7X_VMEM_LIMIT = 62 * 1024 * 1024
N_DEV = 8
MESH = pl.DeviceIdType.MESH

S = jax.ShapeDtypeStruct


def _params(n_grid):
    return pltpu.CompilerParams(dimension_semantics=("arbitrary",) * n_grid, vmem_limit_bytes=V7X_VMEM_LIMIT)


def _sigmoid(x):
    return 0.5 * jnp.tanh(0.5 * x) + 0.5


def _dsilu(x, s):
    return s * (1.0 + x * (1.0 - s))


def _dot(a, b):
    return jnp.dot(a, b, preferred_element_type=F32)


def _dot_nt(a, b):
    return lax.dot_general(a, b, (((1,), (1,)), ((), ())), preferred_element_type=F32)


def _dot_tn(a, b):
    return lax.dot_general(a, b, (((0,), (0,)), ((), ())), preferred_element_type=F32)


def _split_bf16(x):
    hi = x.astype(BF16)
    lo = (x - hi.astype(F32)).astype(BF16)
    return hi, lo


def _const_spec(shape):
    n = len(shape)
    return pl.BlockSpec(shape, lambda *_: (0,) * n, pipeline_mode=pl.Buffered(1))


def _inproj_fwd(x, g_pre, w_in, layer):
    t = x.shape[0]
    tb = min(512, t)

    def body(x_ref, g_ref, w_ref, o_ref):
        xf = x_ref[...]
        r = lax.rsqrt(jnp.mean(xf * xf, axis=-1, keepdims=True) + RMS_EPS)
        h = (xf * r * g_ref[...]).astype(BF16)
        for u in range(N_UNITS):
            o_ref[u] = _dot(h, w_ref[0, :, u * UNIT:(u + 1) * UNIT])

    return pl.pallas_call(
        body, name="inproj_fwd",
        grid=(t // tb,),
        in_specs=[pl.BlockSpec((tb, D_MODEL), lambda i: (i, 0)), _const_spec((1, D_MODEL)),
                  pl.BlockSpec((1, D_MODEL, IN_WIDTH), lambda i: (layer, 0, 0), pipeline_mode=pl.Buffered(1))],
        out_specs=pl.BlockSpec((N_UNITS, tb, UNIT), lambda i: (0, i, 0)),
        out_shape=S((N_UNITS, t, UNIT), F32),
        compiler_params=_params(1),
    )(x, g_pre, w_in)


def _dproj_specs(tb):
    tok = pl.BlockSpec((tb, UNIT), lambda i: (i, 0))
    return [pl.BlockSpec((3, tb, UNIT), lambda i: (0, i, 0)), pl.BlockSpec((9, tb, UNIT), lambda i: (0, i, 0)),
            tok, tok, tok]


def _dproj_unit(u, dpre_ref, dgates_ref, dq_ref, dk_ref, dv_ref):
    pre = {U_P: 0, U_CA: 1, U_CB: 2}
    gate = {U_GP: 0, U_GC: 1, U_GA: 2}
    if u in pre:
        val = dpre_ref[pre[u]]
    elif u in gate:
        val = dgates_ref[gate[u]]
    elif u >= U_GM:
        val = dgates_ref[3 + u - U_GM]
    else:
        val = {U_Q: dq_ref, U_K: dk_ref, U_V: dv_ref}[u][...]
    return val.astype(BF16)


def _inproj_bwd_x(x, g_pre, w_in, layer, dproj_parts, dxn):
    t = x.shape[0]
    tb = min(512, t)

    def body(x_ref, g_ref, w_ref, dpre_ref, dgates_ref, dq_ref, dk_ref, dv_ref, dxn_ref, dx_ref, dg_ref):
        @pl.when(pl.program_id(0) == 0)
        def _():
            dg_ref[...] = jnp.zeros_like(dg_ref)

        dh = jnp.zeros((tb, D_MODEL), F32)
        for u in range(N_UNITS):
            dh = dh + _dot_nt(_dproj_unit(u, dpre_ref, dgates_ref, dq_ref, dk_ref, dv_ref),
                              w_ref[0, :, u * UNIT:(u + 1) * UNIT])
        xf = x_ref[...]
        r = lax.rsqrt(jnp.mean(xf * xf, axis=-1, keepdims=True) + RMS_EPS)
        xhat = xf * r
        dg_ref[...] += jnp.sum(dh * xhat, axis=0, keepdims=True)
        dxhat = dh * g_ref[...]
        dx_ref[...] = dxn_ref[...] + r * (dxhat - xhat * jnp.mean(dxhat * xhat, axis=-1, keepdims=True))

    tokd = pl.BlockSpec((tb, D_MODEL), lambda i: (i, 0))
    return pl.pallas_call(
        body, name="inproj_bwd_x",
        grid=(t // tb,),
        in_specs=[tokd, _const_spec((1, D_MODEL)),
                  pl.BlockSpec((1, D_MODEL, IN_WIDTH), lambda i: (layer, 0, 0), pipeline_mode=pl.Buffered(1))]
        + _dproj_specs(tb) + [tokd],
        out_specs=[tokd, pl.BlockSpec((1, D_MODEL), lambda i: (0, 0))],
        out_shape=[S((t, D_MODEL), F32), S((1, D_MODEL), F32)],
        compiler_params=_params(1),
    )(x, g_pre, w_in, *dproj_parts, dxn)


def _inproj_bwd_w(x, g_pre, dproj_parts):
    t = x.shape[0]
    tb = min(512, t)
    nt = t // tb

    def body(x_ref, g_ref, dpre_ref, dgates_ref, dq_ref, dk_ref, dv_ref, dw_hbm, acc_ref):
        i = pl.program_id(0)
        xf = x_ref[...]
        r = lax.rsqrt(jnp.mean(xf * xf, axis=-1, keepdims=True) + RMS_EPS)
        ht = (xf * r * g_ref[...]).T.astype(BF16)

        @pl.when(i == 0)
        def _():
            acc_ref[...] = jnp.zeros_like(acc_ref)

        for u in range(N_UNITS):
            acc_ref[:, u * UNIT:(u + 1) * UNIT] += _dot(
                ht, _dproj_unit(u, dpre_ref, dgates_ref, dq_ref, dk_ref, dv_ref))

        @pl.when(i == nt - 1)
        def _():
            pltpu.sync_copy(acc_ref, dw_hbm)

    return pl.pallas_call(
        body, name="inproj_bwd_w",
        grid=(nt,),
        in_specs=[pl.BlockSpec((tb, D_MODEL), lambda i: (i, 0)), _const_spec((1, D_MODEL))] + _dproj_specs(tb),
        out_specs=pl.BlockSpec(memory_space=pl.ANY),
        out_shape=S((D_MODEL, IN_WIDTH), F32),
        scratch_shapes=[pltpu.VMEM((D_MODEL, IN_WIDTH), F32)],
        compiler_params=_params(1),
    )(x, g_pre, *dproj_parts)


def _attn_tile_size(t):
    return min(256, t)


def _softplus_parts(z):
    sp = jnp.maximum(z, 0.0) + jnp.log(1.0 + jnp.exp(-jnp.abs(z)))
    return -sp, z - sp


def _attn_fwd(proj):
    t = proj.shape[1]
    tq = _attn_tile_size(t)

    def body(q_ref, k_ref, v_ref, o_ref, bs_ref, kf_ref):
        i = pl.program_id(1)
        row = lax.broadcasted_iota(jnp.int32, (tq, tq), 0)
        col = lax.broadcasted_iota(jnp.int32, (tq, tq), 1)
        tri = (row > col).astype(BF16)
        tri2 = jnp.concatenate([tri, tri], axis=0)
        causal = jnp.concatenate([col < row] * HEADS_PER_BLOCK, axis=0)
        lane = lax.broadcasted_iota(jnp.int32, (1, LANES), 1)
        q_all = q_ref[0] * ATTN_SCALE
        in_heads = [(lane >= hh * HEAD_DIM) & (lane < (hh + 1) * HEAD_DIM) for hh in range(HEADS_PER_BLOCK)]
        q_stack = jnp.concatenate([jnp.where(in_head, q_all, 0.0) for in_head in in_heads], axis=0).astype(BF16)
        rows = HEADS_PER_BLOCK * tq

        def tiles(kj, carry, acc, masked):
            ks = pl.multiple_of(kj * tq, tq)
            kh = k_ref[0, pl.ds(ks, tq), :].astype(BF16)
            vh = v_ref[0, pl.ds(ks, tq), :].astype(BF16)
            z = _dot_nt(q_stack, kh)
            lm, lb = _softplus_parts(z)
            if masked:
                lm = jnp.where(causal, lm, 0.0)
            between = _dot(jnp.concatenate(_split_bf16(lm), axis=1), tri2) + carry
            w = jnp.exp(lb + between)
            if masked:
                w = jnp.where(causal, w, 0.0)
            return carry + jnp.sum(lm, axis=1, keepdims=True), acc + _dot(w.astype(BF16), vh)

        carry, acc = tiles(i, jnp.zeros((rows, 1), F32), jnp.zeros((rows, LANES), F32), True)

        def more(c):
            return jnp.logical_and(c[0] >= 0, jnp.max(c[1]) >= EXP_ZERO_BELOW)

        kj_end, carry, acc = lax.while_loop(
            more, lambda c: (c[0] - 1,) + tiles(c[0], c[1], c[2], False), (i - 1, carry, acc))
        o_out, b_out = acc[0:tq], jnp.broadcast_to(carry[0:tq], (tq, LANES))
        for hh in range(1, HEADS_PER_BLOCK):
            o_out = jnp.where(in_heads[hh], acc[hh * tq:(hh + 1) * tq], o_out)
            b_out = jnp.where(in_heads[hh], carry[hh * tq:(hh + 1) * tq], b_out)
        o_ref[...] = o_out
        bs_ref[0] = b_out
        kf_ref[0, 0] = jnp.zeros((8, LANES), F32) + (kj_end + 1).astype(F32)

    nq = t // tq
    return pl.pallas_call(
        body, name="attn_fwd",
        grid=(N_HEAD_BLOCKS, nq),
        in_specs=[pl.BlockSpec((1, tq, LANES), lambda p, i: (U_Q, i, p)),
                  pl.BlockSpec((1, t, LANES), lambda p, i: (U_K, 0, p)),
                  pl.BlockSpec((1, t, LANES), lambda p, i: (U_V, 0, p))],
        out_specs=[pl.BlockSpec((tq, LANES), lambda p, i: (i, p)),
                   pl.BlockSpec((1, tq, LANES), lambda p, i: (p, i, 0)),
                   pl.BlockSpec((1, 1, 8, LANES), lambda p, i: (p, i, 0, 0))],
        out_shape=[S((t, UNIT), F32), S((N_HEAD_BLOCKS, t, LANES), F32), S((N_HEAD_BLOCKS, nq, 8, LANES), F32)],
        compiler_params=_params(2),
    )(proj, proj, proj)


def _attn_bwd(proj, d_o, bsum, kfirst):
    t = proj.shape[1]
    tq = _attn_tile_size(t)
    nq = t // tq

    def body(q_ref, k_ref, v_ref, do_ref, bs_ref, kf_ref, dq_ref, dk_ref, dv_ref, dkt_ref, dvt_ref):
        i = pl.program_id(1)

        @pl.when(i == 0)
        def _():
            dkt_ref[...] = jnp.zeros_like(dkt_ref)
            dvt_ref[...] = jnp.zeros_like(dvt_ref)

        row = lax.broadcasted_iota(jnp.int32, (tq, tq), 0)
        col = lax.broadcasted_iota(jnp.int32, (tq, tq), 1)
        upto = (row <= col).astype(BF16)
        before = (row < col).astype(BF16)
        upto2 = jnp.concatenate([upto, upto], axis=0)
        before2 = jnp.concatenate([before, before], axis=0)
        causal = jnp.concatenate([col < row] * HEADS_PER_BLOCK, axis=0)
        lane = lax.broadcasted_iota(jnp.int32, (1, LANES), 1)
        q_all = q_ref[0] * ATTN_SCALE
        do_all = do_ref[...].astype(F32)
        bs_all = bs_ref[0]
        k_first = jnp.clip(jnp.max(kf_ref[0, 0]).astype(jnp.int32), 0, i)

        in_heads = [(lane >= hh * HEAD_DIM) & (lane < (hh + 1) * HEAD_DIM) for hh in range(HEADS_PER_BLOCK)]
        q_f32 = jnp.concatenate([jnp.where(in_head, q_all, 0.0) for in_head in in_heads], axis=0)
        do_f32 = jnp.concatenate([jnp.where(in_head, do_all, 0.0) for in_head in in_heads], axis=0)
        q_stack, do_stack = q_f32.astype(BF16), do_f32.astype(BF16)
        q_t, do_t = q_f32.T.astype(BF16), do_f32.T.astype(BF16)
        btot = jnp.concatenate([jnp.max(jnp.where(in_head, bs_all, -jnp.inf), axis=1, keepdims=True)
                                for in_head in in_heads], axis=0)
        rows = HEADS_PER_BLOCK * tq

        def tiles(kj, c_b, c_p, dq, masked):
            ks = pl.multiple_of(kj * tq, tq)
            kh = k_ref[0, pl.ds(ks, tq), :].astype(BF16)
            vh = v_ref[0, pl.ds(ks, tq), :].astype(BF16)
            z = _dot_nt(q_stack, kh)
            lm, lb = _softplus_parts(z)
            if masked:
                lm = jnp.where(causal, lm, 0.0)
            between = btot - (c_b + _dot(jnp.concatenate(_split_bf16(lm), axis=1), upto2))
            w = jnp.exp(lb + between)
            if masked:
                w = jnp.where(causal, w, 0.0)
            e = w * _dot_nt(do_stack, vh)
            p_sum = c_p + _dot(jnp.concatenate(_split_bf16(e), axis=1), before2)
            beta = jnp.exp(lb)
            dz = e * (1.0 - beta) - p_sum * beta
            if masked:
                dz = jnp.where(causal, dz, 0.0)
            dzb = dz.astype(BF16)
            dkt_ref[kj] += _dot(q_t, dzb)
            dvt_ref[kj] += _dot(do_t, w.astype(BF16))
            return (c_b + jnp.sum(lm, axis=1, keepdims=True), c_p + jnp.sum(e, axis=1, keepdims=True),
                    dq + _dot(dzb, kh))

        zero = (jnp.zeros((rows, 1), F32), jnp.zeros((rows, 1), F32), jnp.zeros((rows, LANES), F32))
        state = lax.fori_loop(k_first, i, lambda kj, st: tiles(kj, *st, False), zero)
        dq = tiles(i, *state, True)[2]
        dq_out = dq[0:tq]
        for hh in range(1, HEADS_PER_BLOCK):
            dq_out = jnp.where(in_heads[hh], dq[hh * tq:(hh + 1) * tq], dq_out)
        dq_ref[...] = (dq_out * ATTN_SCALE).astype(BF16)

        @pl.when(i == nq - 1)
        def _():
            for kj in range(nq):
                dk_ref[kj * tq:(kj + 1) * tq, :] = dkt_ref[kj].T.astype(BF16)
                dv_ref[kj * tq:(kj + 1) * tq, :] = dvt_ref[kj].T.astype(BF16)

    return pl.pallas_call(
        body, name="attn_bwd",
        grid=(N_HEAD_BLOCKS, nq),
        in_specs=[pl.BlockSpec((1, tq, LANES), lambda p, i: (U_Q, i, p)),
                  pl.BlockSpec((1, t, LANES), lambda p, i: (U_K, 0, p)),
                  pl.BlockSpec((1, t, LANES), lambda p, i: (U_V, 0, p)),
                  pl.BlockSpec((tq, LANES), lambda p, i: (i, p)),
                  pl.BlockSpec((1, tq, LANES), lambda p, i: (p, i, 0)),
                  pl.BlockSpec((1, 1, 8, LANES), lambda p, i: (p, i, 0, 0))],
        out_specs=[pl.BlockSpec((tq, LANES), lambda p, i: (i, p)),
                   pl.BlockSpec((t, LANES), lambda p, i: (0, p)),
                   pl.BlockSpec((t, LANES), lambda p, i: (0, p))],
        out_shape=[S((t, UNIT), BF16)] * 3,
        scratch_shapes=[pltpu.VMEM((nq, LANES, tq), F32), pltpu.VMEM((nq, LANES, tq), F32)],
        compiler_params=_params(2),
    )(proj, proj, proj, d_o, bsum, kfirst)


N_MIX_VIEWS = 11
N_MIX_VIEWS_NO_CONV = 7


def _mix_in_specs(tb, conv_inputs=True):
    hb = tb // HALO

    def unit(u):
        return pl.BlockSpec((1, tb, UNIT), lambda i: (u, i, 0))

    def halo(u):
        return pl.BlockSpec((1, HALO, UNIT), lambda i: (u, jnp.maximum(i * hb - 1, 0), 0))

    conv = [unit(U_CA), halo(U_CA), unit(U_CB), halo(U_CB)] if conv_inputs else []
    return [unit(U_P), halo(U_P), unit(U_GP)] + conv + [
        unit(U_GC), unit(U_GA), pl.BlockSpec((3, tb, UNIT), lambda i: (3, i, 0)),
        pl.BlockSpec((3, tb, UNIT), lambda i: (4, i, 0))]


def _mix_weight_specs():
    return [_const_spec((4, GROUP, GROUP)), _const_spec((1, UNIT)), _const_spec((1, UNIT)),
            _const_spec((UNIT, D_MODEL)), _const_spec((CONV_K_PAD, UNIT)), _const_spec((1, UNIT)),
            _const_spec((1, UNIT)), _const_spec((1, UNIT)), _const_spec((UNIT, D_MODEL)),
            _const_spec((UNIT, D_MODEL)), _const_spec((D_MODEL, D_MODEL)), _const_spec((1, D_MODEL))]


def _shifted_reader(buf, rbuf, tb):
    if rbuf is None:
        return lambda s: buf[s:s + tb, :]
    length = tb + HALO - 8
    for b in range(1, 8):
        rbuf[b, :, :] = buf[b:b + length, :]

    def read(s):
        a, b = divmod(s, 8)
        return buf[s:s + tb, :] if b == 0 else rbuf[b, 8 * a:8 * a + tb, :]

    return read


def _mix_forward(i, tb, proj_refs, o_ref, w_refs, pbuf, ubuf=None, rbuf=None, cv_ref=None):
    if cv_ref is None:
        p_ref, ph_ref, gp_ref, ca_ref, cah_ref, cb_ref, cbh_ref, gc_ref, ga_ref, gm0_ref, gm1_ref = proj_refs
    else:
        p_ref, ph_ref, gp_ref, gc_ref, ga_ref, gm0_ref, gm1_ref = proj_refs
    (poolw_ref, poolb_ref, pscale_ref, wpo_ref, convw_ref, convb_ref, lng_ref, lnb_ref,
     wco_ref, wao_ref, wo_ref, gpost_ref) = w_refs
    first = i == 0
    r = {}

    pbuf[0:HALO, :] = jnp.where(first, 0.0, ph_ref[0])
    pbuf[HALO:HALO + tb, :] = p_ref[0]
    tpos = i * tb + lax.broadcasted_iota(jnp.int32, (tb, 1), 0)
    d_parts, y_parts = [], []
    for g, win in enumerate(POOL_WINDOWS):
        cs = slice(g * GROUP, (g + 1) * GROUP)
        cur = pbuf[HALO:HALO + tb, cs]
        s = cur
        for j in range(1, win):
            s = s + pbuf[HALO - j:HALO - j + tb, cs]
        cnt = jnp.minimum(tpos + 1, win).astype(F32)
        d_g = s / cnt - cur
        d_parts.append(d_g)
        y_parts.append(_dot(d_g.astype(BF16), poolw_ref[g]))
    r["d"] = d_parts
    y = jnp.concatenate(y_parts, axis=1) + poolb_ref[...]
    r["y"] = y
    mp = y * pscale_ref[...]
    gp = gp_ref[0]
    sgp = _sigmoid(gp)
    r["mp"], r["gp"], r["sgp"] = mp, gp, sgp
    ua = mp * (gp * sgp)

    if cv_ref is None:
        ah, bh = cah_ref[0], cbh_ref[0]
        ubuf[0:HALO, :] = jnp.where(first, 0.0, ah * _sigmoid(bh))
        ubuf[HALO:HALO + tb, :] = ca_ref[0] * _sigmoid(cb_ref[0])
        cv = jnp.zeros((tb, UNIT), F32) + convb_ref[...]
        off = HALO - (CONV_K - 1)
        u_at = _shifted_reader(ubuf, rbuf, tb)
        for k in range(CONV_K):
            cv = cv + convw_ref[k:k + 1, :] * u_at(off + k)
    else:
        cv = cv_ref[...]
    r["cv"] = cv
    mu = jnp.mean(cv, axis=-1, keepdims=True)
    cc = cv - mu
    rs = lax.rsqrt(jnp.mean(cc * cc, axis=-1, keepdims=True) + LN_EPS)
    nrm = cc * rs
    ln = nrm * lng_ref[...] + lnb_ref[...]
    sln = _sigmoid(ln)
    sc = ln * sln
    gc = gc_ref[0]
    sgc = _sigmoid(gc)
    r["rs"], r["nrm"], r["ln"], r["sln"], r["sc"], r["gc"], r["sgc"] = rs, nrm, ln, sln, sc, gc, sgc
    ub = sc * (gc * sgc)

    o = o_ref[...]
    ga = ga_ref[0]
    sga = _sigmoid(ga)
    r["o"], r["ga"], r["sga"] = o, ga, sga
    uc = o * (ga * sga)

    r["ua"], r["ub"], r["uc"] = ua.astype(BF16), ub.astype(BF16), uc.astype(BF16)
    ya = _dot(r["ua"], wpo_ref[...])
    yb = _dot(r["ub"], wco_ref[...])
    yc = _dot(r["uc"], wao_ref[...])
    g0 = _sigmoid(jnp.concatenate([gm0_ref[0], gm0_ref[1]], axis=1))
    g1 = _sigmoid(jnp.concatenate([gm0_ref[2], gm1_ref[0]], axis=1))
    g2 = _sigmoid(jnp.concatenate([gm1_ref[1], gm1_ref[2]], axis=1))
    r["ya"], r["yb"], r["yc"], r["g0"], r["g1"], r["g2"] = ya, yb, yc, g0, g1, g2
    m = (g0 * ya + g1 * yb + g2 * yc).astype(BF16)
    r["m"] = m
    out = _dot(m, wo_ref[...])
    r2 = lax.rsqrt(jnp.mean(out * out, axis=-1, keepdims=True) + RMS_EPS)
    r["n2"], r["r2"] = out * r2, r2
    return r


def _mix_fwd(proj, o, x, weights):
    t = x.shape[0]
    tb = min(256, t)
    nv = N_MIX_VIEWS

    def body(*refs):
        proj_refs, o_ref, x_ref = refs[0:nv], refs[nv], refs[nv + 1]
        w_refs = refs[nv + 2:nv + 14]
        xn_ref, cv_ref, pbuf, ubuf, rbuf = refs[nv + 14:nv + 19]
        r = _mix_forward(pl.program_id(0), tb, proj_refs, o_ref, w_refs, pbuf, ubuf, rbuf)
        xn_ref[...] = x_ref[...] + r["n2"] * w_refs[11][...]
        cv_ref[...] = r["cv"]

    return pl.pallas_call(
        body, name="mix_fwd",
        grid=(t // tb,),
        in_specs=_mix_in_specs(tb) + [pl.BlockSpec((tb, UNIT), lambda i: (i, 0)),
                                      pl.BlockSpec((tb, D_MODEL), lambda i: (i, 0))] + _mix_weight_specs(),
        out_specs=[pl.BlockSpec((tb, D_MODEL), lambda i: (i, 0)), pl.BlockSpec((tb, UNIT), lambda i: (i, 0))],
        out_shape=[S((t, D_MODEL), F32), S((t, UNIT), F32)],
        scratch_shapes=[pltpu.VMEM((HALO + tb, UNIT), F32), pltpu.VMEM((HALO + tb, UNIT), F32),
                        pltpu.VMEM((8, HALO + tb - 8, UNIT), F32)],
        compiler_params=_params(1),
    )(*([proj] * nv), o, x, *weights)


def _mix_bwd(proj, o, cv, x, dxn, weights):
    t = x.shape[0]
    tb = min(256, t)
    nt = t // tb

    nv = N_MIX_VIEWS_NO_CONV

    def body(*refs):
        proj_refs, o_ref, cv_ref, x_ref, dxn_ref = refs[0:nv], refs[nv], refs[nv + 1], refs[nv + 2], refs[nv + 3]
        w_refs = refs[nv + 4:nv + 16]
        (dg_ref, dd_ref, dcv_ref, do_ref, dwo_hbm, dwp_hbm, dwc_hbm, dwa_hbm, dpw_ref,
         dvec_ref) = refs[nv + 16:nv + 26]
        pbuf, dwo_ref, dwp_ref, dwc_ref, dwa_ref = refs[nv + 26:nv + 31]
        (poolw_ref, _, pscale_ref, wpo_ref, _, _, lng_ref, _, wco_ref, wao_ref, wo_ref, gpost_ref) = w_refs
        i = pl.program_id(0)

        @pl.when(i == 0)
        def _():
            for ref in (dwo_ref, dwp_ref, dwc_ref, dwa_ref, dpw_ref, dvec_ref):
                ref[...] = jnp.zeros_like(ref)

        r = _mix_forward(i, tb, proj_refs, o_ref, w_refs, pbuf, cv_ref=cv_ref)

        def colsum(v):
            return jnp.sum(v, axis=0, keepdims=True)

        dxn = dxn_ref[...]
        n2 = r["n2"]
        dvec_ref[0:1, :] += colsum(dxn * n2)
        dn2 = dxn * gpost_ref[...]
        dout = (r["r2"] * (dn2 - n2 * jnp.mean(dn2 * n2, axis=-1, keepdims=True))).astype(BF16)
        dm = _dot_nt(dout, wo_ref[...])
        dwo_ref[...] += _dot_tn(r["m"], dout)

        g0, g1, g2 = r["g0"], r["g1"], r["g2"]
        dgm = [dm * r["ya"] * g0 * (1.0 - g0), dm * r["yb"] * g1 * (1.0 - g1), dm * r["yc"] * g2 * (1.0 - g2)]
        for bidx in range(N_BRANCH):
            dg_ref[3 + 2 * bidx] = dgm[bidx][:, 0:UNIT].astype(BF16)
            dg_ref[4 + 2 * bidx] = dgm[bidx][:, UNIT:2 * UNIT].astype(BF16)
        dya = (dm * g0).astype(BF16)
        dyb = (dm * g1).astype(BF16)
        dyc = (dm * g2).astype(BF16)
        dua = _dot_nt(dya, wpo_ref[...])
        dub = _dot_nt(dyb, wco_ref[...])
        duc = _dot_nt(dyc, wao_ref[...])
        dwp_ref[...] += _dot_tn(r["ua"], dya)
        dwc_ref[...] += _dot_tn(r["ub"], dyb)
        dwa_ref[...] += _dot_tn(r["uc"], dyc)

        gp, sgp = r["gp"], r["sgp"]
        dmp = dua * (gp * sgp)
        dg_ref[0] = (dua * r["mp"] * _dsilu(gp, sgp)).astype(BF16)
        dvec_ref[2:3, 0:UNIT] += colsum(dmp * r["y"])
        dy = dmp * pscale_ref[...]
        dvec_ref[1:2, 0:UNIT] += colsum(dy)
        dd_parts = []
        for g in range(len(POOL_WINDOWS)):
            dy_g = dy[:, g * GROUP:(g + 1) * GROUP].astype(BF16)
            dd_parts.append(_dot_nt(dy_g, poolw_ref[g]))
            dpw_ref[g] += _dot_tn(r["d"][g].astype(BF16), dy_g)
        dd_ref[...] = jnp.concatenate(dd_parts, axis=1)

        gc, sgc = r["gc"], r["sgc"]
        dsc = dub * (gc * sgc)
        dg_ref[1] = (dub * r["sc"] * _dsilu(gc, sgc)).astype(BF16)
        dln = dsc * _dsilu(r["ln"], r["sln"])
        nrm = r["nrm"]
        dvec_ref[4:5, 0:UNIT] += colsum(dln * nrm)
        dvec_ref[5:6, 0:UNIT] += colsum(dln)
        dnrm = dln * lng_ref[...]
        dcv = r["rs"] * (dnrm - jnp.mean(dnrm, axis=-1, keepdims=True)
                         - nrm * jnp.mean(dnrm * nrm, axis=-1, keepdims=True))
        dvec_ref[3:4, 0:UNIT] += colsum(dcv)
        dcv_ref[...] = dcv

        ga, sga = r["ga"], r["sga"]
        do_ref[...] = (duc * (ga * sga)).astype(BF16)
        dg_ref[2] = (duc * r["o"] * _dsilu(ga, sga)).astype(BF16)

        @pl.when(i == nt - 1)
        def _():
            for acc, hbm in ((dwo_ref, dwo_hbm), (dwp_ref, dwp_hbm), (dwc_ref, dwc_hbm), (dwa_ref, dwa_hbm)):
                pltpu.sync_copy(acc, hbm)

    def acc_spec(shape):
        n = len(shape)
        return pl.BlockSpec(shape, lambda i: (0,) * n)

    tok = lambda w: pl.BlockSpec((tb, w), lambda i: (i, 0))
    any_spec = pl.BlockSpec(memory_space=pl.ANY)
    w_shapes = [(D_MODEL, D_MODEL), (UNIT, D_MODEL), (UNIT, D_MODEL), (UNIT, D_MODEL)]
    return pl.pallas_call(
        body, name="mix_bwd",
        grid=(nt,),
        in_specs=_mix_in_specs(tb, conv_inputs=False) + [tok(UNIT), tok(UNIT), tok(D_MODEL), tok(D_MODEL)]
        + _mix_weight_specs(),
        out_specs=[pl.BlockSpec((9, tb, UNIT), lambda i: (0, i, 0)), tok(UNIT), tok(UNIT), tok(UNIT)]
        + [any_spec] * 4 + [acc_spec((4, GROUP, GROUP)), acc_spec((8, D_MODEL))],
        out_shape=[S((9, t, UNIT), BF16), S((t, UNIT), F32), S((t, UNIT), F32), S((t, UNIT), BF16)]
        + [S(s, F32) for s in w_shapes] + [S((4, GROUP, GROUP), F32), S((8, D_MODEL), F32)],
        scratch_shapes=[pltpu.VMEM((HALO + tb, UNIT), F32)] + [pltpu.VMEM(s, F32) for s in w_shapes],
        compiler_params=_params(1),
    )(*([proj] * nv), o, cv, x, dxn, *weights)


def _halo_bwd(proj, dcv, dd, conv_w):
    t = dcv.shape[0]
    tb = min(256, t)
    hb = tb // HALO
    n_halo_blocks = t // HALO
    nt = t // tb

    def body(ca_ref, cb_ref, dcv_ref, dcvh_ref, dd_ref, ddh_ref, cw_ref,
             dpre_ref, dcw_ref, gbuf, nbuf, rg_buf, acc_ref):
        i = pl.program_id(0)
        last = i == nt - 1

        @pl.when(i == 0)
        def _():
            acc_ref[...] = jnp.zeros_like(acc_ref)

        a, b = ca_ref[0], cb_ref[0]
        sb = _sigmoid(b)
        u = a * sb
        gbuf[0:tb, :] = dcv_ref[...]
        gbuf[tb:tb + HALO, :] = jnp.where(last, 0.0, dcvh_ref[...])

        g_at = _shifted_reader(gbuf, rg_buf, tb)
        du = jnp.zeros((tb, UNIT), F32)
        for k in range(CONV_K):
            ahead = g_at(CONV_K - 1 - k)
            du = du + cw_ref[k:k + 1, :] * ahead
            acc_ref[k] += jnp.sum((ahead * u).reshape(tb // 8, 8, UNIT), axis=0)
        dpre_ref[1] = (du * sb).astype(BF16)
        dpre_ref[2] = (du * a * sb * (1.0 - sb)).astype(BF16)

        @pl.when(last)
        def _():
            dcw_ref[...] = jnp.sum(acc_ref[...], axis=1)

        tpos = i * tb + lax.broadcasted_iota(jnp.int32, (tb + HALO, 1), 0)
        dd_main = dd_ref[...]
        dd_ext = jnp.concatenate([dd_main, jnp.where(last, 0.0, ddh_ref[...])], axis=0)
        dp_parts = []
        for g, win in enumerate(POOL_WINDOWS):
            cs = slice(g * GROUP, (g + 1) * GROUP)
            cnt = jnp.minimum(tpos + 1, win).astype(F32)
            nbuf[:, cs] = dd_ext[:, cs] / cnt
        for g, win in enumerate(POOL_WINDOWS):
            cs = slice(g * GROUP, (g + 1) * GROUP)
            s = nbuf[0:tb, cs]
            for j in range(1, win):
                s = s + nbuf[j:j + tb, cs]
            dp_parts.append(s - dd_main[:, cs])
        dpre_ref[0] = jnp.concatenate(dp_parts, axis=1).astype(BF16)

    def unit(u):
        return pl.BlockSpec((1, tb, UNIT), lambda i: (u, i, 0))

    tok = pl.BlockSpec((tb, UNIT), lambda i: (i, 0))
    future = pl.BlockSpec((HALO, UNIT), lambda i: (jnp.minimum((i + 1) * hb, n_halo_blocks - 1), 0))
    return pl.pallas_call(
        body, name="halo_bwd",
        grid=(nt,),
        in_specs=[unit(U_CA), unit(U_CB), tok, future, tok, future, _const_spec((CONV_K_PAD, UNIT))],
        out_specs=[pl.BlockSpec((3, tb, UNIT), lambda i: (0, i, 0)),
                   pl.BlockSpec((CONV_K_PAD, UNIT), lambda i: (0, 0))],
        out_shape=[S((3, t, UNIT), BF16), S((CONV_K_PAD, UNIT), F32)],
        scratch_shapes=[pltpu.VMEM((HALO + tb, UNIT), F32)] * 2
        + [pltpu.VMEM((8, HALO + tb - 8, UNIT), F32), pltpu.VMEM((CONV_K_PAD, 8, UNIT), F32)],
        compiler_params=_params(1),
    )(proj, proj, dcv, dcv, dd, dd, conv_w)


def _loss_head(y, target):
    t = y.shape[0]
    tb = min(512, t)

    def body(y_ref, t_ref, loss_ref, dy_ref):
        @pl.when(pl.program_id(0) == 0)
        def _():
            loss_ref[...] = jnp.zeros_like(loss_ref)

        err = y_ref[...] - t_ref[...]
        dy_ref[...] = err * (1.0 / D_MODEL)
        part = 0.5 * jnp.sum(jnp.mean(err * err, axis=-1, keepdims=True), axis=0, keepdims=True)
        r8 = lax.broadcasted_iota(jnp.int32, (8, LANES), 0)
        c8 = lax.broadcasted_iota(jnp.int32, (8, LANES), 1)
        loss_ref[...] += jnp.where((r8 == 0) & (c8 == 0), part, 0.0)

    return pl.pallas_call(
        body, name="loss_head",
        grid=(t // tb,),
        in_specs=[pl.BlockSpec((tb, D_MODEL), lambda i: (i, 0))] * 2,
        out_specs=[pl.BlockSpec((8, LANES), lambda i: (0, 0)), pl.BlockSpec((tb, D_MODEL), lambda i: (i, 0))],
        out_shape=[S((8, LANES), F32), S((t, D_MODEL), F32)],
        compiler_params=_params(1),
    )(y, target)


def _adamw(name, parts, w, m, v):
    n, rows, cols = parts.shape
    rb = rows
    while rb * cols * 4 * (n + 7) * 2 > 24 * 1024 * 1024 and rb % 16 == 0:
        rb //= 2

    def body(p_ref, w_ref, m_ref, v_ref, g_ref, d_ref, nm_ref, nv_ref):
        g = p_ref[0]
        for k in range(1, n):
            g = g + p_ref[k]
        nm = ADAM_B1 * m_ref[...] + (1.0 - ADAM_B1) * g
        nv = ADAM_B2 * v_ref[...] + (1.0 - ADAM_B2) * (g * g)
        m_hat = nm / (1.0 - ADAM_B1 ** ADAM_STEP)
        v_hat = nv / (1.0 - ADAM_B2 ** ADAM_STEP)
        g_ref[...] = g
        d_ref[...] = -ADAM_LR * (m_hat / (jnp.sqrt(v_hat) + ADAM_EPS) + ADAM_WD * w_ref[...])
        nm_ref[...] = nm
        nv_ref[...] = nv

    blk = pl.BlockSpec((rb, cols), lambda i: (i, 0))
    return pl.pallas_call(
        body, name=name,
        grid=(rows // rb,),
        in_specs=[pl.BlockSpec((n, rb, cols), lambda i: (0, i, 0)), blk, blk, blk],
        out_specs=[blk] * 4,
        out_shape=[S((rows, cols), F32)] * 4,
        compiler_params=_params(1),
    )(parts, w, m, v)


def _axis_slice(ref, axis, block, size):
    idx = [slice(None)] * len(ref.shape)
    idx[axis] = pl.ds(pl.multiple_of(block * size, size), size)
    return ref.at[tuple(idx)]


def _axis_index(ref, axis, index):
    return ref.at[tuple([slice(None)] * axis + [index])]


COPY_PIECE_BYTES = 2 * 1024 * 1024


def _row_pieces(shape):
    rows = shape[-2]
    size = 4 * rows * shape[-1]
    for d in shape[:-2]:
        size *= d
    n = 1
    while size // n > COPY_PIECE_BYTES and rows % (2 * n * 8) == 0:
        n *= 2
    return [(k * (rows // n), rows // n) for k in range(n)]


def _rows(ref, r0, rows):
    idx = [slice(None)] * len(ref.shape)
    idx[-2] = pl.ds(r0, rows)
    return ref.at[tuple(idx)]


def _staged_local_copies(pairs, bufs, in_sems, out_sems):
    ins = [pltpu.make_async_copy(src, buf, in_sems.at[k]) for k, ((src, _), buf) in enumerate(zip(pairs, bufs))]
    for cp in ins:
        cp.start()
    for cp in ins:
        cp.wait()
    outs = [pltpu.make_async_copy(buf, dst, out_sems.at[k]) for k, ((_, dst), buf) in enumerate(zip(pairs, bufs))]
    for cp in outs:
        cp.start()
    return outs


def _run_copies(local, sends, recvs):
    for cp in local + sends:
        cp.start()
    for cp in recvs:
        cp.wait_recv()
    for cp in sends:
        cp.wait_send()
    for cp in local:
        cp.wait()


def _gather_weights(shards, axes):
    n = len(shards)
    fulls = [S(tuple(4 * d if k == ax else d for k, d in enumerate(s.shape)), s.dtype) for s, ax in zip(shards, axes)]

    def body(*refs):
        src, dst = refs[0:n], refs[n:2 * n]
        send_sems, recv_sems, pass_send_sems, pass_recv_sems, in_sems, out_sems = refs[2 * n:2 * n + 6]
        bufs = refs[2 * n + 6:3 * n + 6]
        x, y, c = lax.axis_index("x"), lax.axis_index("y"), lax.axis_index("c")
        chips = [(1 - x, y), (x, 1 - y), (1 - x, 1 - y)]

        def place(a, chip, layer=None):
            ref = _axis_slice(dst[a], axes[a], 2 * chip[0] + chip[1], src[a].shape[axes[a]])
            return ref if layer is None else ref.at[pl.ds(layer, 1)]

        def over_ici(k, chip_of_block, to):
            a, j, r0, rows = pieces[k]
            return pltpu.make_async_remote_copy(
                src_ref=_rows(src[a].at[pl.ds(c, 1)], r0, rows), dst_ref=_rows(place(a, chip_of_block, c), r0, rows),
                send_sem=send_sems.at[k], recv_sem=recv_sems.at[k], device_id=(to[0], to[1], c), device_id_type=MESH)

        def to_sibling(k, chip_of_block, layer):
            a, j, r0, rows = pieces[k]
            block = _rows(place(a, chip_of_block, layer), r0, rows)
            return pltpu.make_async_remote_copy(
                src_ref=block, dst_ref=block, send_sem=pass_send_sems.at[k], recv_sem=pass_recv_sems.at[k],
                device_id=(x, y, 1 - c), device_id_type=MESH)

        first = [over_ici(k, (x, y), chips[j]) for k, (a, j, r0, rows) in enumerate(pieces)]
        for cp in first:
            cp.start()
        local = _staged_local_copies([(src[a], place(a, (x, y))) for a in range(n)], bufs, in_sems, out_sems)
        passed = []
        for k, (a, j, r0, rows) in enumerate(pieces):
            over_ici(k, chips[j], chips[j]).wait_recv()
            passed.append(to_sibling(k, chips[j], c))
            passed[-1].start()
        for k, (a, j, r0, rows) in enumerate(pieces):
            to_sibling(k, chips[j], 1 - c).wait_recv()
        for cp in first + passed:
            cp.wait_send()
        for cp in local:
            cp.wait()

    per_array = [_row_pieces((1,) + shards[a].shape[1:]) for a in range(n)]
    pieces = [(a, j, r0, rows) for a in range(n) for r0, rows in per_array[a] for j in range(3)]
    any_spec = pl.BlockSpec(memory_space=pl.ANY)
    return pl.pallas_call(
        body, name="gather_weights",
        in_specs=[any_spec] * n, out_specs=[any_spec] * n, out_shape=fulls,
        scratch_shapes=[pltpu.SemaphoreType.DMA((len(pieces),))] * 4 + [pltpu.SemaphoreType.DMA((n,))] * 2
        + [pltpu.VMEM(s.shape, s.dtype) for s in shards],
        compiler_params=_params(0),
    )(*shards)


def _swap_layers(layers):
    n = len(layers)
    pieces = [(a, r0, rows) for a in range(n) for r0, rows in _row_pieces(layers[a][0].shape)]

    def body(*refs):
        src = [refs[2 * a:2 * a + 2] for a in range(n)]
        dst = refs[2 * n:3 * n]
        send_sems, recv_sems = refs[3 * n:3 * n + 2]
        x, y, c = lax.axis_index("x"), lax.axis_index("y"), lax.axis_index("c")

        def remote(k, layer):
            a, r0, rows = pieces[k]
            return pltpu.make_async_remote_copy(
                src_ref=_rows(src[a][layer], r0, rows), dst_ref=_rows(dst[a], r0, rows),
                send_sem=send_sems.at[k], recv_sem=recv_sems.at[k], device_id=(x, y, 1 - c), device_id_type=MESH)

        for layer in range(2):
            @pl.when(c == 1 - layer)
            def _(layer=layer):
                for k in range(len(pieces)):
                    remote(k, layer).start()

        for k in range(len(pieces)):
            remote(k, 0).wait_recv()
        for k in range(len(pieces)):
            remote(k, 0).wait_send()

    any_spec = pl.BlockSpec(memory_space=pl.ANY)
    return pl.pallas_call(
        body, name="swap_layers",
        in_specs=[any_spec] * (2 * n), out_specs=[any_spec] * n,
        out_shape=[S(layers[a][0].shape, layers[a][0].dtype) for a in range(n)],
        scratch_shapes=[pltpu.SemaphoreType.DMA((len(pieces),))] * 2,
    )(*[arr for pair in layers for arr in pair])


def _add_own_layer(name, mine, theirs, layer):
    rows, cols = theirs.shape
    rb = rows
    while rb * cols * 4 * 4 * 2 > 24 * 1024 * 1024 and rb % 32 == 0:
        rb //= 2

    def body(layer_ref, m0_ref, m1_ref, t_ref, o_ref):
        own = jnp.where(layer_ref[0] == 0, m0_ref[...], m1_ref[...])
        o_ref[0] = (own + t_ref[...]).astype(BF16)

    def own_spec(which):
        return pl.BlockSpec((rb, cols), lambda i, l: (jnp.where(l[0] == which, i, 0), 0))

    return pl.pallas_call(
        body, name=name,
        grid_spec=pltpu.PrefetchScalarGridSpec(
            num_scalar_prefetch=1, grid=(rows // rb,),
            in_specs=[own_spec(0), own_spec(1), pl.BlockSpec((rb, cols), lambda i, l: (i, 0))],
            out_specs=pl.BlockSpec((1, rb, cols), lambda i, l: (0, i, 0))),
        out_shape=S((1, rows, cols), BF16),
        compiler_params=_params(1),
    )(layer, *mine, theirs)


def _send_to_owners(chip_halves, shard_axes, packed):
    n = len(chip_halves)

    def owned_shape(a):
        return tuple(d // 4 if k == shard_axes[a] else d for k, d in enumerate(chip_halves[a].shape))

    flips = [(dx, dy, dc) for dx in (0, 1) for dy in (0, 1) for dc in (0, 1)][1:]

    def body(*refs):
        src, psrc, dst, pdst = refs[0:n], refs[n], refs[n + 1:2 * n + 1], refs[2 * n + 1]
        send_sems, recv_sems, local_sems, psend_sems, precv_sems = refs[2 * n + 2:2 * n + 7]
        x, y, c = lax.axis_index("x"), lax.axis_index("y"), lax.axis_index("c")
        my_chip = 2 * x + y
        chips = [(1 - x, y), (x, 1 - y), (1 - x, 1 - y)]

        def owned(a, chip):
            return _axis_slice(src[a], shard_axes[a], 2 * chip[0] + chip[1], src[a].shape[shard_axes[a]] // 4)

        def remote(a, j, to, from_chip):
            return pltpu.make_async_remote_copy(
                src_ref=owned(a, to), dst_ref=dst[a].at[from_chip], send_sem=send_sems.at[a, j],
                recv_sem=recv_sems.at[a, j], device_id=(to[0], to[1], c), device_id_type=MESH)

        def peer(f):
            return (1 - x if f[0] else x, 1 - y if f[1] else y, 1 - c if f[2] else c)

        def premote(j, to, from_dev):
            return pltpu.make_async_remote_copy(
                src_ref=psrc, dst_ref=pdst.at[from_dev], send_sem=psend_sems.at[j], recv_sem=precv_sems.at[j],
                device_id=to, device_id_type=MESH)

        local = [pltpu.make_async_copy(_rows(owned(a, (x, y)), r0, rows), _rows(dst[a].at[my_chip], r0, rows),
                                       local_sems.at[k]) for k, (a, r0, rows) in enumerate(own_pieces)]
        local.append(pltpu.make_async_copy(psrc, pdst.at[4 * x + 2 * y + c], local_sems.at[len(own_pieces)]))
        sends = [remote(a, j, chip, my_chip) for a in range(n) for j, chip in enumerate(chips)]
        sends += [premote(j, peer(f), 4 * x + 2 * y + c) for j, f in enumerate(flips)]
        recvs = [remote(a, j, (x, y), 2 * chip[0] + chip[1]) for a in range(n) for j, chip in enumerate(chips)]
        for j, f in enumerate(flips):
            px, py, pc = peer(f)
            recvs.append(premote(j, (x, y, c), 4 * px + 2 * py + pc))
        _run_copies(local, sends, recvs)

    own_pieces = [(a, r0, rows) for a in range(n) for r0, rows in _row_pieces(owned_shape(a))]
    any_spec = pl.BlockSpec(memory_space=pl.ANY)
    return pl.pallas_call(
        body, name="send_to_owners",
        in_specs=[any_spec] * (n + 1), out_specs=[any_spec] * (n + 1),
        out_shape=[S((4,) + owned_shape(a), chip_halves[a].dtype) for a in range(n)]
        + [S((N_DEV,) + packed.shape, packed.dtype)],
        scratch_shapes=[pltpu.SemaphoreType.DMA((n, 3)), pltpu.SemaphoreType.DMA((n, 3)),
                        pltpu.SemaphoreType.DMA((len(own_pieces) + 1,)), pltpu.SemaphoreType.DMA((7,)),
                        pltpu.SemaphoreType.DMA((7,))],
    )(*chip_halves, packed)


def _join_halves(reduced, half_axes):
    n = len(reduced)

    def joined_shape(a):
        s, ax = reduced[a].shape, half_axes[a]
        return s[:ax] + (2,) + s[ax:]

    pieces = [(a, r0, rows) for a in range(n) for r0, rows in _row_pieces(reduced[a].shape)]

    def body(*refs):
        src, dst = refs[0:n], refs[n:2 * n]
        send_sems, recv_sems, in_sems, out_sems = refs[2 * n:2 * n + 4]
        bufs = refs[2 * n + 4:3 * n + 4]
        x, y, c = lax.axis_index("x"), lax.axis_index("y"), lax.axis_index("c")

        def remote(k, h):
            a, r0, rows = pieces[k]
            return pltpu.make_async_remote_copy(
                src_ref=_rows(src[a], r0, rows), dst_ref=_rows(_axis_index(dst[a], half_axes[a], h), r0, rows),
                send_sem=send_sems.at[k], recv_sem=recv_sems.at[k], device_id=(x, y, 1 - c), device_id_type=MESH)

        sends = [remote(k, c) for k in range(len(pieces))]
        for cp in sends:
            cp.start()
        own = [(src[a], _axis_index(dst[a], half_axes[a], c)) for a in range(n)]
        local = _staged_local_copies(own, bufs, in_sems, out_sems)
        for k in range(len(pieces)):
            remote(k, 1 - c).wait_recv()
        for cp in sends:
            cp.wait_send()
        for cp in local:
            cp.wait()

    any_spec = pl.BlockSpec(memory_space=pl.ANY)
    return pl.pallas_call(
        body, name="join_halves",
        in_specs=[any_spec] * n, out_specs=[any_spec] * n,
        out_shape=[S(joined_shape(a), reduced[a].dtype) for a in range(n)],
        scratch_shapes=[pltpu.SemaphoreType.DMA((len(pieces),))] * 2 + [pltpu.SemaphoreType.DMA((n,))] * 2
        + [pltpu.VMEM(r.shape, r.dtype) for r in reduced],
        compiler_params=_params(0),
    )(*reduced)


def _sum_slots(name, parts):
    n = parts.shape[0]
    shape = parts.shape[1:]
    flat = parts.reshape((n, -1, shape[-1]))
    rows, cols = flat.shape[1:]
    rb = rows
    while rb * cols * 4 * (n + 1) * 2 > 24 * 1024 * 1024 and rb % 32 == 0:
        rb //= 2

    def body(p_ref, o_ref):
        acc = p_ref[0].astype(F32)
        for k in range(1, n):
            acc = acc + p_ref[k].astype(F32)
        o_ref[...] = acc

    out = pl.pallas_call(
        body, name=name,
        grid=(rows // rb,),
        in_specs=[pl.BlockSpec((n, rb, cols), lambda i: (0, i, 0))],
        out_specs=pl.BlockSpec((rb, cols), lambda i: (i, 0)),
        out_shape=S((rows, cols), F32),
        compiler_params=_params(1),
    )(flat)
    return out.reshape(shape)


SMALL_NAMES = ("norm_pre", "pool_w", "pool_b", "pool_scale", "conv_b", "conv_ln_g", "conv_ln_b", "norm_post")
BIG_NAMES = ("w_in", "w_pool_out", "conv_w", "w_conv_out", "w_attn_out", "w_o")
SHARD_AXIS = {"w_in": 2, "w_pool_out": 2, "conv_w": 2, "w_conv_out": 2, "w_attn_out": 2, "w_o": 1}
WEIGHT_ORDER = ("norm_pre", "w_in", "pool_w", "pool_b", "pool_scale", "w_pool_out", "conv_w", "conv_b",
                "conv_ln_g", "conv_ln_b", "w_conv_out", "w_attn_out", "w_o", "norm_post")


def _pack_small(parts):
    return jnp.concatenate([parts[n].reshape(-1, LANES) for n in SMALL_NAMES], axis=0)


def _unpack_small(packed, shapes):
    out, r0 = {}, 0
    for n in SMALL_NAMES:
        size = 1
        for d in shapes[n]:
            size *= d
        rows = size // LANES
        out[n] = packed[r0:r0 + rows].reshape(shapes[n])
        r0 += rows
    return out


def _layer_weights(full, small, l):
    row = lambda a: a[l][None, :]
    return (small["pool_w"][l].astype(BF16), small["pool_b"][l].reshape(1, UNIT), row(small["pool_scale"]),
            full["w_pool_out"][l], full["conv_w"][l], row(small["conv_b"]), row(small["conv_ln_g"]),
            row(small["conv_ln_b"]), full["w_conv_out"][l], full["w_attn_out"][l], full["w_o"][l],
            row(small["norm_post"]))


def _forward_backward(x, target, full, small):
    depth = full["w_in"].shape[0]
    acts = []
    for l in range(depth):
        g_pre = small["norm_pre"][l][None, :]
        proj = _inproj_fwd(x, g_pre, full["w_in"], l)
        o, bsum, kfirst = _attn_fwd(proj)
        weights = _layer_weights(full, small, l)
        x_next, cv = _mix_fwd(proj, o, x, weights)
        acts.append((x, proj, o, cv, bsum, kfirst, weights, g_pre))
        x = x_next
    loss_tile, dx = _loss_head(x, target)

    big = {n: [None] * depth for n in BIG_NAMES}
    sm = {n: [None] * depth for n in SMALL_NAMES}
    for l in reversed(range(depth)):
        x_in, proj, o, cv, bsum, kfirst, weights, g_pre = acts[l]
        dgates, dd, dcv, d_o, dwo, dwp, dwc, dwa, dpw, dvec = _mix_bwd(proj, o, cv, x_in, dx, weights)
        dpre, dcw = _halo_bwd(proj, dcv, dd, full["conv_w"][l])
        dq, dk, dv = _attn_bwd(proj, d_o, bsum, kfirst)
        dproj_parts = (dpre, dgates, dq, dk, dv)
        dx, dg_pre = _inproj_bwd_x(x_in, g_pre, full["w_in"], l, dproj_parts, dx)
        big["w_in"][l] = _inproj_bwd_w(x_in, g_pre, dproj_parts)
        big["w_pool_out"][l], big["w_conv_out"][l], big["w_attn_out"][l], big["w_o"][l] = dwp, dwc, dwa, dwo
        big["conv_w"][l] = dcw
        sm["norm_pre"][l] = dg_pre[0]
        sm["pool_w"][l] = dpw
        sm["norm_post"][l] = dvec[0]
        sm["pool_b"][l] = dvec[1, 0:UNIT].reshape(4, GROUP)
        sm["pool_scale"][l] = dvec[2, 0:UNIT]
        sm["conv_b"][l] = dvec[3, 0:UNIT]
        sm["conv_ln_g"][l] = dvec[4, 0:UNIT]
        sm["conv_ln_b"][l] = dvec[5, 0:UNIT]
    sm = {n: jnp.stack(v) for n, v in sm.items()}
    return loss_tile, dx, big, sm


def kernel(x, norm_pre, w_in, pool_w, pool_b, pool_scale, w_pool_out, conv_w, conv_b, conv_ln_g, conv_ln_b, w_conv_out, w_attn_out, w_o, norm_post, loss_target, m_norm_pre, m_w_in, m_pool_w, m_pool_b, m_pool_scale, m_w_pool_out, m_conv_w, m_conv_b, m_conv_ln_g, m_conv_ln_b, m_w_conv_out, m_w_attn_out, m_w_o, m_norm_post, v_norm_pre, v_w_in, v_pool_w, v_pool_b, v_pool_scale, v_w_pool_out, v_conv_w, v_conv_b, v_conv_ln_g, v_conv_ln_b, v_w_conv_out, v_w_attn_out, v_w_o, v_norm_post):
    w = dict(norm_pre=norm_pre, w_in=w_in, pool_w=pool_w, pool_b=pool_b, pool_scale=pool_scale,
             w_pool_out=w_pool_out, conv_w=conv_w, conv_b=conv_b, conv_ln_g=conv_ln_g, conv_ln_b=conv_ln_b,
             w_conv_out=w_conv_out, w_attn_out=w_attn_out, w_o=w_o, norm_post=norm_post)
    m = dict(norm_pre=m_norm_pre, w_in=m_w_in, pool_w=m_pool_w, pool_b=m_pool_b, pool_scale=m_pool_scale,
             w_pool_out=m_w_pool_out, conv_w=m_conv_w, conv_b=m_conv_b, conv_ln_g=m_conv_ln_g,
             conv_ln_b=m_conv_ln_b, w_conv_out=m_w_conv_out, w_attn_out=m_w_attn_out, w_o=m_w_o,
             norm_post=m_norm_post)
    v = dict(norm_pre=v_norm_pre, w_in=v_w_in, pool_w=v_pool_w, pool_b=v_pool_b, pool_scale=v_pool_scale,
             w_pool_out=v_w_pool_out, conv_w=v_conv_w, conv_b=v_conv_b, conv_ln_g=v_conv_ln_g,
             conv_ln_b=v_conv_ln_b, w_conv_out=v_w_conv_out, w_attn_out=v_w_attn_out, w_o=v_w_o,
             norm_post=v_norm_post)
    pad_taps = lambda a: jnp.pad(a, ((0, 0), (0, CONV_K_PAD - CONV_K), (0, 0)))

    shards = [pad_taps(w[n]) if n == "conv_w" else w[n].astype(BF16) for n in BIG_NAMES]
    full = dict(zip(BIG_NAMES, _gather_weights(shards, [SHARD_AXIS[n] for n in BIG_NAMES])))
    small = {n: w[n] for n in SMALL_NAMES}

    loss_tile, dx, big, sm = _forward_backward(x[0], loss_target[0], full, small)

    packed = jnp.concatenate([_pack_small(sm), loss_tile], axis=0)
    layer_axis = [0] * len(BIG_NAMES)
    theirs = _swap_layers([big[n] for n in BIG_NAMES])
    my_layer = lax.axis_index("c").astype(jnp.int32).reshape(1)
    chip_sums = [_add_own_layer("sum_cores_" + n, big[n], t, my_layer) for n, t in zip(BIG_NAMES, theirs)]
    exchanged = _send_to_owners(chip_sums, [SHARD_AXIS[n] for n in BIG_NAMES], packed)
    reduced = [_sum_slots("sum_chips_" + n, p) for n, p in zip(BIG_NAMES, exchanged[:-1])]
    joined = _join_halves(reduced, layer_axis)
    out = {}
    for n, g in zip(BIG_NAMES, joined):
        wn, mn, vn = (pad_taps(a[n]) if n == "conv_w" else a[n] for a in (w, m, v))
        shape = wn.shape
        flat = lambda a: a.reshape((-1, shape[-1]))
        res = _adamw("adamw_" + n, g.reshape((1, -1, shape[-1])), flat(wn), flat(mn), flat(vn))
        res = [r.reshape(shape) for r in res]
        out[n] = [r[:, :CONV_K] for r in res] if n == "conv_w" else res
    parts = exchanged[-1]
    n_small_rows = parts.shape[1] - 8
    zeros_tile = jnp.zeros((8, LANES), F32)
    packs = [jnp.concatenate([_pack_small(a), zeros_tile], axis=0) for a in (w, m, v)]
    res = _adamw("adamw_small", parts, *packs)
    loss = res[0][n_small_rows, 0]
    shapes = {n: w[n].shape for n in SMALL_NAMES}
    unpacked = [_unpack_small(r[:n_small_rows], shapes) for r in res]
    for n in SMALL_NAMES:
        out[n] = [u[n] for u in unpacked]

    grads = [out[n][0] for n in WEIGHT_ORDER]
    deltas = [out[n][1] for n in WEIGHT_ORDER]
    new_m = [out[n][2] for n in WEIGHT_ORDER]
    new_v = [out[n][3] for n in WEIGHT_ORDER]
    return (loss, dx[None], *grads, *deltas, *new_m, *new_v)
```

```python
import jax
import jax.numpy as jnp
from jax import lax
from jax.experimental import pallas as pl
from jax.experimental.pallas import tpu as pltpu

F32 = jnp.float32
BF16 = jnp.bfloat16

D_MODEL = 1024
UNIT = 512
N_UNITS = 15
IN_WIDTH = UNIT * N_UNITS
N_HEADS = 8
HEAD_DIM = 64
HEADS_PER_BLOCK = 2
N_HEAD_BLOCKS = N_HEADS // HEADS_PER_BLOCK
LANES = 128
CONV_K = 31
CONV_K_PAD = 32
HALO = 32
POOL_WINDOWS = (2, 4, 8, 16)
GROUP = 128
N_BRANCH = 3
RMS_EPS = 1e-6
LN_EPS = 1e-5
ATTN_SCALE = 0.125
EXP_ZERO_BELOW = -104.0

ADAM_LR = 0.001
ADAM_B1 = 0.9
ADAM_B2 = 0.999
ADAM_EPS = 1e-08
ADAM_WD = 0.01
ADAM_STEP = 10

U_P, U_GP, U_CA, U_CB, U_GC, U_Q, U_K, U_V, U_GA, U_GM = 0, 1, 2, 3, 4, 5, 6, 7, 8, 9

V7X_VMEM_LIMIT = 62 * 1024 * 1024
N_DEV = 8
MESH = pl.DeviceIdType.MESH

S = jax.ShapeDtypeStruct


def _params(n_grid):
    return pltpu.CompilerParams(dimension_semantics=("arbitrary",) * n_grid, vmem_limit_bytes=V7X_VMEM_LIMIT)


def _sigmoid(x):
    return 0.5 * jnp.tanh(0.5 * x) + 0.5


def _dsilu(x, s):
    return s * (1.0 + x * (1.0 - s))


def _dot(a, b):
    return jnp.dot(a, b, preferred_element_type=F32)


def _dot_nt(a, b):
    return lax.dot_general(a, b, (((1,), (1,)), ((), ())), preferred_element_type=F32)


def _dot_tn(a, b):
    return lax.dot_general(a, b, (((0,), (0,)), ((), ())), preferred_element_type=F32)


def _split_bf16(x):
    hi = x.astype(BF16)
    lo = (x - hi.astype(F32)).astype(BF16)
    return hi, lo


def _const_spec(shape):
    n = len(shape)
    return pl.BlockSpec(shape, lambda *_: (0,) * n, pipeline_mode=pl.Buffered(1))


def _inproj_fwd(x, g_pre, w_in, layer):
    t = x.shape[0]
    tb = min(512, t)

    def body(x_ref, g_ref, w_ref, o_ref, qkv_ref):
        xf = x_ref[...]
        r = lax.rsqrt(jnp.mean(xf * xf, axis=-1, keepdims=True) + RMS_EPS)
        h = (xf * r * g_ref[...]).astype(BF16)
        for u in range(N_UNITS):
            res = _dot(h, w_ref[0, :, u * UNIT:(u + 1) * UNIT])
            o_ref[u] = res
            if U_Q <= u <= U_V:
                qkv_ref[u - U_Q] = res.astype(BF16)

    return pl.pallas_call(
        body, name="inproj_fwd",
        grid=(t // tb,),
        in_specs=[pl.BlockSpec((tb, D_MODEL), lambda i: (i, 0)), _const_spec((1, D_MODEL)),
                  pl.BlockSpec((1, D_MODEL, IN_WIDTH), lambda i: (layer, 0, 0), pipeline_mode=pl.Buffered(1))],
        out_specs=[pl.BlockSpec((N_UNITS, tb, UNIT), lambda i: (0, i, 0)),
                   pl.BlockSpec((3, tb, UNIT), lambda i: (0, i, 0))],
        out_shape=[S((N_UNITS, t, UNIT), F32), S((3, t, UNIT), BF16)],
        compiler_params=_params(1),
    )(x, g_pre, w_in)


def _dproj_specs(tb):
    tok = pl.BlockSpec((tb, UNIT), lambda i: (i, 0))
    return [pl.BlockSpec((3, tb, UNIT), lambda i: (0, i, 0)), pl.BlockSpec((9, tb, UNIT), lambda i: (0, i, 0)),
            tok, tok, tok]


def _dproj_unit(u, dpre_ref, dgates_ref, dq_ref, dk_ref, dv_ref):
    pre = {U_P: 0, U_CA: 1, U_CB: 2}
    gate = {U_GP: 0, U_GC: 1, U_GA: 2}
    if u in pre:
        val = dpre_ref[pre[u]]
    elif u in gate:
        val = dgates_ref[gate[u]]
    elif u >= U_GM:
        val = dgates_ref[3 + u - U_GM]
    else:
        val = {U_Q: dq_ref, U_K: dk_ref, U_V: dv_ref}[u][...]
    return val.astype(BF16)


def _inproj_bwd_x(x, g_pre, w_in, layer, dproj_parts, dxn):
    t = x.shape[0]
    tb = min(512, t)

    def body(x_ref, g_ref, w_ref, dpre_ref, dgates_ref, dq_ref, dk_ref, dv_ref, dxn_ref, dx_ref, dg_ref):
        @pl.when(pl.program_id(0) == 0)
        def _():
            dg_ref[...] = jnp.zeros_like(dg_ref)

        dh = jnp.zeros((tb, D_MODEL), F32)
        for u in range(N_UNITS):
            dh = dh + _dot_nt(_dproj_unit(u, dpre_ref, dgates_ref, dq_ref, dk_ref, dv_ref),
                              w_ref[0, :, u * UNIT:(u + 1) * UNIT])
        xf = x_ref[...]
        r = lax.rsqrt(jnp.mean(xf * xf, axis=-1, keepdims=True) + RMS_EPS)
        xhat = xf * r
        dg_ref[...] += jnp.sum(dh * xhat, axis=0, keepdims=True)
        dxhat = dh * g_ref[...]
        dx_ref[...] = dxn_ref[...] + r * (dxhat - xhat * jnp.mean(dxhat * xhat, axis=-1, keepdims=True))

    tokd = pl.BlockSpec((tb, D_MODEL), lambda i: (i, 0))
    return pl.pallas_call(
        body, name="inproj_bwd_x",
        grid=(t // tb,),
        in_specs=[tokd, _const_spec((1, D_MODEL)),
                  pl.BlockSpec((1, D_MODEL, IN_WIDTH), lambda i: (layer, 0, 0), pipeline_mode=pl.Buffered(1))]
        + _dproj_specs(tb) + [tokd],
        out_specs=[tokd, pl.BlockSpec((1, D_MODEL), lambda i: (0, 0))],
        out_shape=[S((t, D_MODEL), F32), S((1, D_MODEL), F32)],
        compiler_params=_params(1),
    )(x, g_pre, w_in, *dproj_parts, dxn)


def _inproj_bwd_w(x, g_pre, dproj_parts):
    t = x.shape[0]
    tb = min(512, t)
    nt = t // tb

    def body(x_ref, g_ref, dpre_ref, dgates_ref, dq_ref, dk_ref, dv_ref, dw_hbm, acc_ref):
        i = pl.program_id(0)
        xf = x_ref[...]
        r = lax.rsqrt(jnp.mean(xf * xf, axis=-1, keepdims=True) + RMS_EPS)
        ht = (xf * r * g_ref[...]).T.astype(BF16)

        @pl.when(i == 0)
        def _():
            acc_ref[...] = jnp.zeros_like(acc_ref)

        for u in range(N_UNITS):
            acc_ref[:, u * UNIT:(u + 1) * UNIT] += _dot(
                ht, _dproj_unit(u, dpre_ref, dgates_ref, dq_ref, dk_ref, dv_ref))

        @pl.when(i == nt - 1)
        def _():
            pltpu.sync_copy(acc_ref, dw_hbm)

    return pl.pallas_call(
        body, name="inproj_bwd_w",
        grid=(nt,),
        in_specs=[pl.BlockSpec((tb, D_MODEL), lambda i: (i, 0)), _const_spec((1, D_MODEL))] + _dproj_specs(tb),
        out_specs=pl.BlockSpec(memory_space=pl.ANY),
        out_shape=S((D_MODEL, IN_WIDTH), F32),
        scratch_shapes=[pltpu.VMEM((D_MODEL, IN_WIDTH), F32)],
        compiler_params=_params(1),
    )(x, g_pre, *dproj_parts)


def _attn_tile_size(t):
    return min(256, t)


def _softplus_parts(z):
    sp = jnp.maximum(z, 0.0) + jnp.log(1.0 + jnp.exp(-jnp.abs(z)))
    return -sp, z - sp


def _attn_fwd(proj):
    t = proj.shape[1]
    tq = _attn_tile_size(t)

    def body(q_ref, k_ref, v_ref, o_ref, bs_ref, kf_ref):
        i = pl.program_id(1)
        row = lax.broadcasted_iota(jnp.int32, (tq, tq), 0)
        col = lax.broadcasted_iota(jnp.int32, (tq, tq), 1)
        tri = (row > col).astype(BF16)
        tri2 = jnp.concatenate([tri, tri], axis=0)
        causal = jnp.concatenate([col < row] * HEADS_PER_BLOCK, axis=0)
        lane = lax.broadcasted_iota(jnp.int32, (1, LANES), 1)
        q_all = q_ref[0].astype(F32) * ATTN_SCALE
        in_heads = [(lane >= hh * HEAD_DIM) & (lane < (hh + 1) * HEAD_DIM) for hh in range(HEADS_PER_BLOCK)]
        q_stack = jnp.concatenate([jnp.where(in_head, q_all, 0.0) for in_head in in_heads], axis=0).astype(BF16)
        rows = HEADS_PER_BLOCK * tq

        def tiles(kj, carry, acc, masked):
            ks = pl.multiple_of(kj * tq, tq)
            kh = k_ref[0, pl.ds(ks, tq), :].astype(BF16)
            vh = v_ref[0, pl.ds(ks, tq), :].astype(BF16)
            z = _dot_nt(q_stack, kh)
            lm, lb = _softplus_parts(z)
            if masked:
                lm = jnp.where(causal, lm, 0.0)
            between = _dot(jnp.concatenate(_split_bf16(lm), axis=1), tri2) + carry
            w = jnp.exp(lb + between)
            if masked:
                w = jnp.where(causal, w, 0.0)
            return carry + jnp.sum(lm, axis=1, keepdims=True), acc + _dot(w.astype(BF16), vh)

        carry, acc = tiles(i, jnp.zeros((rows, 1), F32), jnp.zeros((rows, LANES), F32), True)

        def more(c):
            return jnp.logical_and(c[0] >= 0, jnp.max(c[1]) >= EXP_ZERO_BELOW)

        kj_end, carry, acc = lax.while_loop(
            more, lambda c: (c[0] - 1,) + tiles(c[0], c[1], c[2], False), (i - 1, carry, acc))
        o_out, b_out = acc[0:tq], jnp.broadcast_to(carry[0:tq], (tq, LANES))
        for hh in range(1, HEADS_PER_BLOCK):
            o_out = jnp.where(in_heads[hh], acc[hh * tq:(hh + 1) * tq], o_out)
            b_out = jnp.where(in_heads[hh], carry[hh * tq:(hh + 1) * tq], b_out)
        o_ref[...] = o_out
        bs_ref[0] = b_out
        kf_ref[0, 0] = jnp.zeros((8, LANES), F32) + (kj_end + 1).astype(F32)

    nq = t // tq
    return pl.pallas_call(
        body, name="attn_fwd",
        grid=(N_HEAD_BLOCKS, nq),
        in_specs=[pl.BlockSpec((1, tq, LANES), lambda p, i: (0, i, p)),
                  pl.BlockSpec((1, t, LANES), lambda p, i: (1, 0, p)),
                  pl.BlockSpec((1, t, LANES), lambda p, i: (2, 0, p))],
        out_specs=[pl.BlockSpec((tq, LANES), lambda p, i: (i, p)),
                   pl.BlockSpec((1, tq, LANES), lambda p, i: (p, i, 0)),
                   pl.BlockSpec((1, 1, 8, LANES), lambda p, i: (p, i, 0, 0))],
        out_shape=[S((t, UNIT), F32), S((N_HEAD_BLOCKS, t, LANES), F32), S((N_HEAD_BLOCKS, nq, 8, LANES), F32)],
        compiler_params=_params(2),
    )(proj, proj, proj)


def _attn_bwd(proj, d_o, bsum, kfirst):
    t = proj.shape[1]
    tq = _attn_tile_size(t)
    nq = t // tq

    def body(q_ref, k_ref, v_ref, do_ref, bs_ref, kf_ref, dq_ref, dk_ref, dv_ref, dkt_ref, dvt_ref):
        i = pl.program_id(1)

        @pl.when(i == 0)
        def _():
            dkt_ref[...] = jnp.zeros_like(dkt_ref)
            dvt_ref[...] = jnp.zeros_like(dvt_ref)

        row = lax.broadcasted_iota(jnp.int32, (tq, tq), 0)
        col = lax.broadcasted_iota(jnp.int32, (tq, tq), 1)
        upto = (row <= col).astype(BF16)
        before = (row < col).astype(BF16)
        upto2 = jnp.concatenate([upto, upto], axis=0)
        before2 = jnp.concatenate([before, before], axis=0)
        causal = jnp.concatenate([col < row] * HEADS_PER_BLOCK, axis=0)
        lane = lax.broadcasted_iota(jnp.int32, (1, LANES), 1)
        q_all = q_ref[0].astype(F32) * ATTN_SCALE
        do_all = do_ref[...].astype(F32)
        bs_all = bs_ref[0]
        k_first = jnp.clip(jnp.max(kf_ref[0, 0]).astype(jnp.int32), 0, i)

        in_heads = [(lane >= hh * HEAD_DIM) & (lane < (hh + 1) * HEAD_DIM) for hh in range(HEADS_PER_BLOCK)]
        q_f32 = jnp.concatenate([jnp.where(in_head, q_all, 0.0) for in_head in in_heads], axis=0)
        do_f32 = jnp.concatenate([jnp.where(in_head, do_all, 0.0) for in_head in in_heads], axis=0)
        q_stack, do_stack = q_f32.astype(BF16), do_f32.astype(BF16)
        q_t, do_t = q_f32.T.astype(BF16), do_f32.T.astype(BF16)
        btot = jnp.concatenate([jnp.max(jnp.where(in_head, bs_all, -jnp.inf), axis=1, keepdims=True)
                                for in_head in in_heads], axis=0)
        rows = HEADS_PER_BLOCK * tq

        def tiles(kj, c_b, c_p, dq, masked):
            ks = pl.multiple_of(kj * tq, tq)
            kh = k_ref[0, pl.ds(ks, tq), :].astype(BF16)
            vh = v_ref[0, pl.ds(ks, tq), :].astype(BF16)
            z = _dot_nt(q_stack, kh)
            lm, lb = _softplus_parts(z)
            if masked:
                lm = jnp.where(causal, lm, 0.0)
            between = btot - (c_b + _dot(jnp.concatenate(_split_bf16(lm), axis=1), upto2))
            w = jnp.exp(lb + between)
            if masked:
                w = jnp.where(causal, w, 0.0)
            e = w * _dot_nt(do_stack, vh)
            p_sum = c_p + _dot(jnp.concatenate(_split_bf16(e), axis=1), before2)
            beta = jnp.exp(lb)
            dz = e * (1.0 - beta) - p_sum * beta
            if masked:
                dz = jnp.where(causal, dz, 0.0)
            dzb = dz.astype(BF16)
            dkt_ref[kj] += _dot(q_t, dzb)
            dvt_ref[kj] += _dot(do_t, w.astype(BF16))
            return (c_b + jnp.sum(lm, axis=1, keepdims=True), c_p + jnp.sum(e, axis=1, keepdims=True),
                    dq + _dot(dzb, kh))

        zero = (jnp.zeros((rows, 1), F32), jnp.zeros((rows, 1), F32), jnp.zeros((rows, LANES), F32))
        state = lax.fori_loop(k_first, i, lambda kj, st: tiles(kj, *st, False), zero)
        dq = tiles(i, *state, True)[2]
        dq_out = dq[0:tq]
        for hh in range(1, HEADS_PER_BLOCK):
            dq_out = jnp.where(in_heads[hh], dq[hh * tq:(hh + 1) * tq], dq_out)
        dq_ref[...] = (dq_out * ATTN_SCALE).astype(BF16)

        @pl.when(i == nq - 1)
        def _():
            for kj in range(nq):
                dk_ref[kj * tq:(kj + 1) * tq, :] = dkt_ref[kj].T.astype(BF16)
                dv_ref[kj * tq:(kj + 1) * tq, :] = dvt_ref[kj].T.astype(BF16)

    return pl.pallas_call(
        body, name="attn_bwd",
        grid=(N_HEAD_BLOCKS, nq),
        in_specs=[pl.BlockSpec((1, tq, LANES), lambda p, i: (0, i, p)),
                  pl.BlockSpec((1, t, LANES), lambda p, i: (1, 0, p)),
                  pl.BlockSpec((1, t, LANES), lambda p, i: (2, 0, p)),
                  pl.BlockSpec((tq, LANES), lambda p, i: (i, p)),
                  pl.BlockSpec((1, tq, LANES), lambda p, i: (p, i, 0)),
                  pl.BlockSpec((1, 1, 8, LANES), lambda p, i: (p, i, 0, 0))],
        out_specs=[pl.BlockSpec((tq, LANES), lambda p, i: (i, p)),
                   pl.BlockSpec((t, LANES), lambda p, i: (0, p)),
                   pl.BlockSpec((t, LANES), lambda p, i: (0, p))],
        out_shape=[S((t, UNIT), BF16)] * 3,
        scratch_shapes=[pltpu.VMEM((nq, LANES, tq), F32), pltpu.VMEM((nq, LANES, tq), F32)],
        compiler_params=_params(2),
    )(proj, proj, proj, d_o, bsum, kfirst)


N_MIX_VIEWS = 11
N_MIX_VIEWS_NO_CONV = 7


def _mix_in_specs(tb, conv_inputs=True):
    hb = tb // HALO

    def unit(u):
        return pl.BlockSpec((1, tb, UNIT), lambda i: (u, i, 0))

    def halo(u):
        return pl.BlockSpec((1, HALO, UNIT), lambda i: (u, jnp.maximum(i * hb - 1, 0), 0))

    conv = [unit(U_CA), halo(U_CA), unit(U_CB), halo(U_CB)] if conv_inputs else []
    return [unit(U_P), halo(U_P), unit(U_GP)] + conv + [
        unit(U_GC), unit(U_GA), pl.BlockSpec((3, tb, UNIT), lambda i: (3, i, 0)),
        pl.BlockSpec((3, tb, UNIT), lambda i: (4, i, 0))]


def _mix_weight_specs():
    return [_const_spec((4, GROUP, GROUP)), _const_spec((1, UNIT)), _const_spec((1, UNIT)),
            _const_spec((UNIT, D_MODEL)), _const_spec((CONV_K_PAD, UNIT)), _const_spec((1, UNIT)),
            _const_spec((1, UNIT)), _const_spec((1, UNIT)), _const_spec((UNIT, D_MODEL)),
            _const_spec((UNIT, D_MODEL)), _const_spec((D_MODEL, D_MODEL)), _const_spec((1, D_MODEL))]


def _shifted_reader(buf, rbuf, tb):
    if rbuf is None:
        return lambda s: buf[s:s + tb, :]
    length = tb + HALO - 8
    for b in range(1, 8):
        rbuf[b, :, :] = buf[b:b + length, :]

    def read(s):
        a, b = divmod(s, 8)
        return buf[s:s + tb, :] if b == 0 else rbuf[b, 8 * a:8 * a + tb, :]

    return read


def _mix_forward(i, tb, proj_refs, o_ref, w_refs, pbuf, ubuf=None, rbuf=None, cv_ref=None):
    if cv_ref is None:
        p_ref, ph_ref, gp_ref, ca_ref, cah_ref, cb_ref, cbh_ref, gc_ref, ga_ref, gm0_ref, gm1_ref = proj_refs
    else:
        p_ref, ph_ref, gp_ref, gc_ref, ga_ref, gm0_ref, gm1_ref = proj_refs
    (poolw_ref, poolb_ref, pscale_ref, wpo_ref, convw_ref, convb_ref, lng_ref, lnb_ref,
     wco_ref, wao_ref, wo_ref, gpost_ref) = w_refs
    first = i == 0
    r = {}

    pbuf[0:HALO, :] = jnp.where(first, 0.0, ph_ref[0])
    pbuf[HALO:HALO + tb, :] = p_ref[0]
    tpos = i * tb + lax.broadcasted_iota(jnp.int32, (tb, 1), 0)
    d_parts, y_parts = [], []
    for g, win in enumerate(POOL_WINDOWS):
        cs = slice(g * GROUP, (g + 1) * GROUP)
        cur = pbuf[HALO:HALO + tb, cs]
        s = cur
        for j in range(1, win):
            s = s + pbuf[HALO - j:HALO - j + tb, cs]
        cnt = jnp.minimum(tpos + 1, win).astype(F32)
        d_g = s / cnt - cur
        d_parts.append(d_g)
        y_parts.append(_dot(d_g.astype(BF16), poolw_ref[g]))
    r["d"] = d_parts
    y = jnp.concatenate(y_parts, axis=1) + poolb_ref[...]
    r["y"] = y
    mp = y * pscale_ref[...]
    gp = gp_ref[0]
    sgp = _sigmoid(gp)
    r["mp"], r["gp"], r["sgp"] = mp, gp, sgp
    ua = mp * (gp * sgp)

    if cv_ref is None:
        ah, bh = cah_ref[0], cbh_ref[0]
        ubuf[0:HALO, :] = jnp.where(first, 0.0, ah * _sigmoid(bh))
        ubuf[HALO:HALO + tb, :] = ca_ref[0] * _sigmoid(cb_ref[0])
        cv = jnp.zeros((tb, UNIT), F32) + convb_ref[...]
        off = HALO - (CONV_K - 1)
        u_at = _shifted_reader(ubuf, rbuf, tb)
        for k in range(CONV_K):
            cv = cv + convw_ref[k:k + 1, :] * u_at(off + k)
    else:
        cv = cv_ref[...]
    r["cv"] = cv
    mu = jnp.mean(cv, axis=-1, keepdims=True)
    cc = cv - mu
    rs = lax.rsqrt(jnp.mean(cc * cc, axis=-1, keepdims=True) + LN_EPS)
    nrm = cc * rs
    ln = nrm * lng_ref[...] + lnb_ref[...]
    sln = _sigmoid(ln)
    sc = ln * sln
    gc = gc_ref[0]
    sgc = _sigmoid(gc)
    r["rs"], r["nrm"], r["ln"], r["sln"], r["sc"], r["gc"], r["sgc"] = rs, nrm, ln, sln, sc, gc, sgc
    ub = sc * (gc * sgc)

    o = o_ref[...]
    ga = ga_ref[0]
    sga = _sigmoid(ga)
    r["o"], r["ga"], r["sga"] = o, ga, sga
    uc = o * (ga * sga)

    r["ua"], r["ub"], r["uc"] = ua.astype(BF16), ub.astype(BF16), uc.astype(BF16)
    ya = _dot(r["ua"], wpo_ref[...])
    yb = _dot(r["ub"], wco_ref[...])
    yc = _dot(r["uc"], wao_ref[...])
    g0 = _sigmoid(jnp.concatenate([gm0_ref[0], gm0_ref[1]], axis=1))
    g1 = _sigmoid(jnp.concatenate([gm0_ref[2], gm1_ref[0]], axis=1))
    g2 = _sigmoid(jnp.concatenate([gm1_ref[1], gm1_ref[2]], axis=1))
    r["ya"], r["yb"], r["yc"], r["g0"], r["g1"], r["g2"] = ya, yb, yc, g0, g1, g2
    m = (g0 * ya + g1 * yb + g2 * yc).astype(BF16)
    r["m"] = m
    out = _dot(m, wo_ref[...])
    r2 = lax.rsqrt(jnp.mean(out * out, axis=-1, keepdims=True) + RMS_EPS)
    r["n2"], r["r2"] = out * r2, r2
    return r


def _mix_fwd(proj, o, x, weights):
    t = x.shape[0]
    tb = min(256, t)
    nv = N_MIX_VIEWS

    def body(*refs):
        proj_refs, o_ref, x_ref = refs[0:nv], refs[nv], refs[nv + 1]
        w_refs = refs[nv + 2:nv + 14]
        xn_ref, cv_ref, pbuf, ubuf, rbuf = refs[nv + 14:nv + 19]
        r = _mix_forward(pl.program_id(0), tb, proj_refs, o_ref, w_refs, pbuf, ubuf, rbuf)
        xn_ref[...] = x_ref[...] + r["n2"] * w_refs[11][...]
        cv_ref[...] = r["cv"]

    return pl.pallas_call(
        body, name="mix_fwd",
        grid=(t // tb,),
        in_specs=_mix_in_specs(tb) + [pl.BlockSpec((tb, UNIT), lambda i: (i, 0)),
                                      pl.BlockSpec((tb, D_MODEL), lambda i: (i, 0))] + _mix_weight_specs(),
        out_specs=[pl.BlockSpec((tb, D_MODEL), lambda i: (i, 0)), pl.BlockSpec((tb, UNIT), lambda i: (i, 0))],
        out_shape=[S((t, D_MODEL), F32), S((t, UNIT), F32)],
        scratch_shapes=[pltpu.VMEM((HALO + tb, UNIT), F32), pltpu.VMEM((HALO + tb, UNIT), F32),
                        pltpu.VMEM((8, HALO + tb - 8, UNIT), F32)],
        compiler_params=_params(1),
    )(*([proj] * nv), o, x, *weights)


def _mix_bwd(proj, o, cv, x, dxn, weights):
    t = x.shape[0]
    tb = min(256, t)
    nt = t // tb

    nv = N_MIX_VIEWS_NO_CONV

    def body(*refs):
        proj_refs, o_ref, cv_ref, x_ref, dxn_ref = refs[0:nv], refs[nv], refs[nv + 1], refs[nv + 2], refs[nv + 3]
        w_refs = refs[nv + 4:nv + 16]
        (dg_ref, dd_ref, dcv_ref, do_ref, dwo_hbm, dwp_hbm, dwc_hbm, dwa_hbm, dpw_ref,
         dvec_ref) = refs[nv + 16:nv + 26]
        pbuf, dwo_ref, dwp_ref, dwc_ref, dwa_ref = refs[nv + 26:nv + 31]
        (poolw_ref, _, pscale_ref, wpo_ref, _, _, lng_ref, _, wco_ref, wao_ref, wo_ref, gpost_ref) = w_refs
        i = pl.program_id(0)

        @pl.when(i == 0)
        def _():
            for ref in (dwo_ref, dwp_ref, dwc_ref, dwa_ref, dpw_ref, dvec_ref):
                ref[...] = jnp.zeros_like(ref)

        r = _mix_forward(i, tb, proj_refs, o_ref, w_refs, pbuf, cv_ref=cv_ref)

        def colsum(v):
            return jnp.sum(v, axis=0, keepdims=True)

        dxn = dxn_ref[...]
        n2 = r["n2"]
        dvec_ref[0:1, :] += colsum(dxn * n2)
        dn2 = dxn * gpost_ref[...]
        dout = (r["r2"] * (dn2 - n2 * jnp.mean(dn2 * n2, axis=-1, keepdims=True))).astype(BF16)
        dm = _dot_nt(dout, wo_ref[...])
        dwo_ref[...] += _dot_tn(r["m"], dout)

        g0, g1, g2 = r["g0"], r["g1"], r["g2"]
        dgm = [dm * r["ya"] * g0 * (1.0 - g0), dm * r["yb"] * g1 * (1.0 - g1), dm * r["yc"] * g2 * (1.0 - g2)]
        for bidx in range(N_BRANCH):
            dg_ref[3 + 2 * bidx] = dgm[bidx][:, 0:UNIT].astype(BF16)
            dg_ref[4 + 2 * bidx] = dgm[bidx][:, UNIT:2 * UNIT].astype(BF16)
        dya = (dm * g0).astype(BF16)
        dyb = (dm * g1).astype(BF16)
        dyc = (dm * g2).astype(BF16)
        dua = _dot_nt(dya, wpo_ref[...])
        dub = _dot_nt(dyb, wco_ref[...])
        duc = _dot_nt(dyc, wao_ref[...])
        dwp_ref[...] += _dot_tn(r["ua"], dya)
        dwc_ref[...] += _dot_tn(r["ub"], dyb)
        dwa_ref[...] += _dot_tn(r["uc"], dyc)

        gp, sgp = r["gp"], r["sgp"]
        dmp = dua * (gp * sgp)
        dg_ref[0] = (dua * r["mp"] * _dsilu(gp, sgp)).astype(BF16)
        dvec_ref[2:3, 0:UNIT] += colsum(dmp * r["y"])
        dy = dmp * pscale_ref[...]
        dvec_ref[1:2, 0:UNIT] += colsum(dy)
        dd_parts = []
        for g in range(len(POOL_WINDOWS)):
            dy_g = dy[:, g * GROUP:(g + 1) * GROUP].astype(BF16)
            dd_parts.append(_dot_nt(dy_g, poolw_ref[g]))
            dpw_ref[g] += _dot_tn(r["d"][g].astype(BF16), dy_g)
        dd_ref[...] = jnp.concatenate(dd_parts, axis=1)

        gc, sgc = r["gc"], r["sgc"]
        dsc = dub * (gc * sgc)
        dg_ref[1] = (dub * r["sc"] * _dsilu(gc, sgc)).astype(BF16)
        dln = dsc * _dsilu(r["ln"], r["sln"])
        nrm = r["nrm"]
        dvec_ref[4:5, 0:UNIT] += colsum(dln * nrm)
        dvec_ref[5:6, 0:UNIT] += colsum(dln)
        dnrm = dln * lng_ref[...]
        dcv = r["rs"] * (dnrm - jnp.mean(dnrm, axis=-1, keepdims=True)
                         - nrm * jnp.mean(dnrm * nrm, axis=-1, keepdims=True))
        dvec_ref[3:4, 0:UNIT] += colsum(dcv)
        dcv_ref[...] = dcv

        ga, sga = r["ga"], r["sga"]
        do_ref[...] = (duc * (ga * sga)).astype(BF16)
        dg_ref[2] = (duc * r["o"] * _dsilu(ga, sga)).astype(BF16)

        @pl.when(i == nt - 1)
        def _():
            for acc, hbm in ((dwo_ref, dwo_hbm), (dwp_ref, dwp_hbm), (dwc_ref, dwc_hbm), (dwa_ref, dwa_hbm)):
                pltpu.sync_copy(acc, hbm)

    def acc_spec(shape):
        n = len(shape)
        return pl.BlockSpec(shape, lambda i: (0,) * n)

    tok = lambda w: pl.BlockSpec((tb, w), lambda i: (i, 0))
    any_spec = pl.BlockSpec(memory_space=pl.ANY)
    w_shapes = [(D_MODEL, D_MODEL), (UNIT, D_MODEL), (UNIT, D_MODEL), (UNIT, D_MODEL)]
    return pl.pallas_call(
        body, name="mix_bwd",
        grid=(nt,),
        in_specs=_mix_in_specs(tb, conv_inputs=False) + [tok(UNIT), tok(UNIT), tok(D_MODEL), tok(D_MODEL)]
        + _mix_weight_specs(),
        out_specs=[pl.BlockSpec((9, tb, UNIT), lambda i: (0, i, 0)), tok(UNIT), tok(UNIT), tok(UNIT)]
        + [any_spec] * 4 + [acc_spec((4, GROUP, GROUP)), acc_spec((8, D_MODEL))],
        out_shape=[S((9, t, UNIT), BF16), S((t, UNIT), F32), S((t, UNIT), F32), S((t, UNIT), BF16)]
        + [S(s, F32) for s in w_shapes] + [S((4, GROUP, GROUP), F32), S((8, D_MODEL), F32)],
        scratch_shapes=[pltpu.VMEM((HALO + tb, UNIT), F32)] + [pltpu.VMEM(s, F32) for s in w_shapes],
        compiler_params=_params(1),
    )(*([proj] * nv), o, cv, x, dxn, *weights)


def _halo_bwd(proj, dcv, dd, conv_w):
    t = dcv.shape[0]
    tb = min(256, t)
    hb = tb // HALO
    n_halo_blocks = t // HALO
    nt = t // tb

    def body(ca_ref, cb_ref, dcv_ref, dcvh_ref, dd_ref, ddh_ref, cw_ref,
             dpre_ref, dcw_ref, gbuf, nbuf, rg_buf, acc_ref):
        i = pl.program_id(0)
        last = i == nt - 1

        @pl.when(i == 0)
        def _():
            acc_ref[...] = jnp.zeros_like(acc_ref)

        a, b = ca_ref[0], cb_ref[0]
        sb = _sigmoid(b)
        u = a * sb
        gbuf[0:tb, :] = dcv_ref[...]
        gbuf[tb:tb + HALO, :] = jnp.where(last, 0.0, dcvh_ref[...])

        g_at = _shifted_reader(gbuf, rg_buf, tb)
        du = jnp.zeros((tb, UNIT), F32)
        for k in range(CONV_K):
            ahead = g_at(CONV_K - 1 - k)
            du = du + cw_ref[k:k + 1, :] * ahead
            acc_ref[k] += jnp.sum((ahead * u).reshape(tb // 8, 8, UNIT), axis=0)
        dpre_ref[1] = (du * sb).astype(BF16)
        dpre_ref[2] = (du * a * sb * (1.0 - sb)).astype(BF16)

        @pl.when(last)
        def _():
            dcw_ref[...] = jnp.sum(acc_ref[...], axis=1)

        tpos = i * tb + lax.broadcasted_iota(jnp.int32, (tb + HALO, 1), 0)
        dd_main = dd_ref[...]
        dd_ext = jnp.concatenate([dd_main, jnp.where(last, 0.0, ddh_ref[...])], axis=0)
        dp_parts = []
        for g, win in enumerate(POOL_WINDOWS):
            cs = slice(g * GROUP, (g + 1) * GROUP)
            cnt = jnp.minimum(tpos + 1, win).astype(F32)
            nbuf[:, cs] = dd_ext[:, cs] / cnt
        for g, win in enumerate(POOL_WINDOWS):
            cs = slice(g * GROUP, (g + 1) * GROUP)
            s = nbuf[0:tb, cs]
            for j in range(1, win):
                s = s + nbuf[j:j + tb, cs]
            dp_parts.append(s - dd_main[:, cs])
        dpre_ref[0] = jnp.concatenate(dp_parts, axis=1).astype(BF16)

    def unit(u):
        return pl.BlockSpec((1, tb, UNIT), lambda i: (u, i, 0))

    tok = pl.BlockSpec((tb, UNIT), lambda i: (i, 0))
    future = pl.BlockSpec((HALO, UNIT), lambda i: (jnp.minimum((i + 1) * hb, n_halo_blocks - 1), 0))
    return pl.pallas_call(
        body, name="halo_bwd",
        grid=(nt,),
        in_specs=[unit(U_CA), unit(U_CB), tok, future, tok, future, _const_spec((CONV_K_PAD, UNIT))],
        out_specs=[pl.BlockSpec((3, tb, UNIT), lambda i: (0, i, 0)),
                   pl.BlockSpec((CONV_K_PAD, UNIT), lambda i: (0, 0))],
        out_shape=[S((3, t, UNIT), BF16), S((CONV_K_PAD, UNIT), F32)],
        scratch_shapes=[pltpu.VMEM((HALO + tb, UNIT), F32)] * 2
        + [pltpu.VMEM((8, HALO + tb - 8, UNIT), F32), pltpu.VMEM((CONV_K_PAD, 8, UNIT), F32)],
        compiler_params=_params(1),
    )(proj, proj, dcv, dcv, dd, dd, conv_w)


def _loss_head(y, target):
    t = y.shape[0]
    tb = min(512, t)

    def body(y_ref, t_ref, loss_ref, dy_ref):
        @pl.when(pl.program_id(0) == 0)
        def _():
            loss_ref[...] = jnp.zeros_like(loss_ref)

        err = y_ref[...] - t_ref[...]
        dy_ref[...] = err * (1.0 / D_MODEL)
        part = 0.5 * jnp.sum(jnp.mean(err * err, axis=-1, keepdims=True), axis=0, keepdims=True)
        r8 = lax.broadcasted_iota(jnp.int32, (8, LANES), 0)
        c8 = lax.broadcasted_iota(jnp.int32, (8, LANES), 1)
        loss_ref[...] += jnp.where((r8 == 0) & (c8 == 0), part, 0.0)

    return pl.pallas_call(
        body, name="loss_head",
        grid=(t // tb,),
        in_specs=[pl.BlockSpec((tb, D_MODEL), lambda i: (i, 0))] * 2,
        out_specs=[pl.BlockSpec((8, LANES), lambda i: (0, 0)), pl.BlockSpec((tb, D_MODEL), lambda i: (i, 0))],
        out_shape=[S((8, LANES), F32), S((t, D_MODEL), F32)],
        compiler_params=_params(1),
    )(y, target)


def _adamw(name, parts, w, m, v):
    n, rows, cols = parts.shape
    rb = rows
    while rb * cols * 4 * (n + 7) * 2 > 24 * 1024 * 1024 and rb % 16 == 0:
        rb //= 2

    def body(p_ref, w_ref, m_ref, v_ref, g_ref, d_ref, nm_ref, nv_ref):
        g = p_ref[0]
        for k in range(1, n):
            g = g + p_ref[k]
        nm = ADAM_B1 * m_ref[...] + (1.0 - ADAM_B1) * g
        nv = ADAM_B2 * v_ref[...] + (1.0 - ADAM_B2) * (g * g)
        m_hat = nm / (1.0 - ADAM_B1 ** ADAM_STEP)
        v_hat = nv / (1.0 - ADAM_B2 ** ADAM_STEP)
        g_ref[...] = g
        d_ref[...] = -ADAM_LR * (m_hat / (jnp.sqrt(v_hat) + ADAM_EPS) + ADAM_WD * w_ref[...])
        nm_ref[...] = nm
        nv_ref[...] = nv

    blk = pl.BlockSpec((rb, cols), lambda i: (i, 0))
    return pl.pallas_call(
        body, name=name,
        grid=(rows // rb,),
        in_specs=[pl.BlockSpec((n, rb, cols), lambda i: (0, i, 0)), blk, blk, blk],
        out_specs=[blk] * 4,
        out_shape=[S((rows, cols), F32)] * 4,
        compiler_params=_params(1),
    )(parts, w, m, v)


def _axis_slice(ref, axis, block, size):
    idx = [slice(None)] * len(ref.shape)
    idx[axis] = pl.ds(pl.multiple_of(block * size, size), size)
    return ref.at[tuple(idx)]


def _axis_index(ref, axis, index):
    return ref.at[tuple([slice(None)] * axis + [index])]


COPY_PIECE_BYTES = 2 * 1024 * 1024


def _row_pieces(shape):
    rows = shape[-2]
    size = 4 * rows * shape[-1]
    for d in shape[:-2]:
        size *= d
    n = 1
    while size // n > COPY_PIECE_BYTES and rows % (2 * n * 8) == 0:
        n *= 2
    return [(k * (rows // n), rows // n) for k in range(n)]


def _rows(ref, r0, rows):
    idx = [slice(None)] * len(ref.shape)
    idx[-2] = pl.ds(r0, rows)
    return ref.at[tuple(idx)]


def _staged_local_copies(pairs, bufs, in_sems, out_sems):
    ins = [pltpu.make_async_copy(src, buf, in_sems.at[k]) for k, ((src, _), buf) in enumerate(zip(pairs, bufs))]
    for cp in ins:
        cp.start()
    for cp in ins:
        cp.wait()
    outs = [pltpu.make_async_copy(buf, dst, out_sems.at[k]) for k, ((_, dst), buf) in enumerate(zip(pairs, bufs))]
    for cp in outs:
        cp.start()
    return outs


def _run_copies(local, sends, recvs):
    for cp in local + sends:
        cp.start()
    for cp in recvs:
        cp.wait_recv()
    for cp in sends:
        cp.wait_send()
    for cp in local:
        cp.wait()


def _gather_weights(shards, axes):
    n = len(shards)
    fulls = [S(tuple(4 * d if k == ax else d for k, d in enumerate(s.shape)), s.dtype) for s, ax in zip(shards, axes)]

    def body(*refs):
        src, dst = refs[0:n], refs[n:2 * n]
        send_sems, recv_sems, pass_send_sems, pass_recv_sems, in_sems, out_sems = refs[2 * n:2 * n + 6]
        bufs = refs[2 * n + 6:3 * n + 6]
        x, y, c = lax.axis_index("x"), lax.axis_index("y"), lax.axis_index("c")
        chips = [(1 - x, y), (x, 1 - y), (1 - x, 1 - y)]

        def place(a, chip, layer=None):
            ref = _axis_slice(dst[a], axes[a], 2 * chip[0] + chip[1], src[a].shape[axes[a]])
            return ref if layer is None else ref.at[pl.ds(layer, 1)]

        def over_ici(k, chip_of_block, to):
            a, j, r0, rows = pieces[k]
            return pltpu.make_async_remote_copy(
                src_ref=_rows(src[a].at[pl.ds(c, 1)], r0, rows), dst_ref=_rows(place(a, chip_of_block, c), r0, rows),
                send_sem=send_sems.at[k], recv_sem=recv_sems.at[k], device_id=(to[0], to[1], c), device_id_type=MESH)

        def to_sibling(k, chip_of_block, layer):
            a, j, r0, rows = pieces[k]
            block = _rows(place(a, chip_of_block, layer), r0, rows)
            return pltpu.make_async_remote_copy(
                src_ref=block, dst_ref=block, send_sem=pass_send_sems.at[k], recv_sem=pass_recv_sems.at[k],
                device_id=(x, y, 1 - c), device_id_type=MESH)

        first = [over_ici(k, (x, y), chips[j]) for k, (a, j, r0, rows) in enumerate(pieces)]
        for cp in first:
            cp.start()
        local = _staged_local_copies([(src[a], place(a, (x, y))) for a in range(n)], bufs, in_sems, out_sems)
        passed = []
        for k, (a, j, r0, rows) in enumerate(pieces):
            over_ici(k, chips[j], chips[j]).wait_recv()
            passed.append(to_sibling(k, chips[j], c))
            passed[-1].start()
        for k, (a, j, r0, rows) in enumerate(pieces):
            to_sibling(k, chips[j], 1 - c).wait_recv()
        for cp in first + passed:
            cp.wait_send()
        for cp in local:
            cp.wait()

    per_array = [_row_pieces((1,) + shards[a].shape[1:]) for a in range(n)]
    pieces = [(a, j, r0, rows) for a in range(n) for r0, rows in per_array[a] for j in range(3)]
    any_spec = pl.BlockSpec(memory_space=pl.ANY)
    return pl.pallas_call(
        body, name="gather_weights",
        in_specs=[any_spec] * n, out_specs=[any_spec] * n, out_shape=fulls,
        scratch_shapes=[pltpu.SemaphoreType.DMA((len(pieces),))] * 4 + [pltpu.SemaphoreType.DMA((n,))] * 2
        + [pltpu.VMEM(s.shape, s.dtype) for s in shards],
        compiler_params=_params(0),
    )(*shards)


def _swap_layers(layers):
    n = len(layers)
    pieces = [(a, r0, rows) for a in range(n) for r0, rows in _row_pieces(layers[a][0].shape)]

    def body(*refs):
        src = [refs[2 * a:2 * a + 2] for a in range(n)]
        dst = refs[2 * n:3 * n]
        send_sems, recv_sems = refs[3 * n:3 * n + 2]
        x, y, c = lax.axis_index("x"), lax.axis_index("y"), lax.axis_index("c")

        def remote(k, layer):
            a, r0, rows = pieces[k]
            return pltpu.make_async_remote_copy(
                src_ref=_rows(src[a][layer], r0, rows), dst_ref=_rows(dst[a], r0, rows),
                send_sem=send_sems.at[k], recv_sem=recv_sems.at[k], device_id=(x, y, 1 - c), device_id_type=MESH)

        for layer in range(2):
            @pl.when(c == 1 - layer)
            def _(layer=layer):
                for k in range(len(pieces)):
                    remote(k, layer).start()

        for k in range(len(pieces)):
            remote(k, 0).wait_recv()
        for k in range(len(pieces)):
            remote(k, 0).wait_send()

    any_spec = pl.BlockSpec(memory_space=pl.ANY)
    return pl.pallas_call(
        body, name="swap_layers",
        in_specs=[any_spec] * (2 * n), out_specs=[any_spec] * n,
        out_shape=[S(layers[a][0].shape, layers[a][0].dtype) for a in range(n)],
        scratch_shapes=[pltpu.SemaphoreType.DMA((len(pieces),))] * 2,
    )(*[arr for pair in layers for arr in pair])


def _add_own_layer(name, mine, theirs, layer):
    rows, cols = theirs.shape
    rb = rows
    while rb * cols * 4 * 4 * 2 > 24 * 1024 * 1024 and rb % 32 == 0:
        rb //= 2

    def body(layer_ref, m0_ref, m1_ref, t_ref, o_ref):
        own = jnp.where(layer_ref[0] == 0, m0_ref[...], m1_ref[...])
        o_ref[0] = (own + t_ref[...]).astype(BF16)

    def own_spec(which):
        return pl.BlockSpec((rb, cols), lambda i, l: (jnp.where(l[0] == which, i, 0), 0))

    return pl.pallas_call(
        body, name=name,
        grid_spec=pltpu.PrefetchScalarGridSpec(
            num_scalar_prefetch=1, grid=(rows // rb,),
            in_specs=[own_spec(0), own_spec(1), pl.BlockSpec((rb, cols), lambda i, l: (i, 0))],
            out_specs=pl.BlockSpec((1, rb, cols), lambda i, l: (0, i, 0))),
        out_shape=S((1, rows, cols), BF16),
        compiler_params=_params(1),
    )(layer, *mine, theirs)


def _send_to_owners(chip_halves, shard_axes, packed):
    n = len(chip_halves)

    def owned_shape(a):
        return tuple(d // 4 if k == shard_axes[a] else d for k, d in enumerate(chip_halves[a].shape))

    flips = [(dx, dy, dc) for dx in (0, 1) for dy in (0, 1) for dc in (0, 1)][1:]

    def body(*refs):
        src, psrc, dst, pdst = refs[0:n], refs[n], refs[n + 1:2 * n + 1], refs[2 * n + 1]
        send_sems, recv_sems, local_sems, psend_sems, precv_sems = refs[2 * n + 2:2 * n + 7]
        x, y, c = lax.axis_index("x"), lax.axis_index("y"), lax.axis_index("c")
        my_chip = 2 * x + y
        chips = [(1 - x, y), (x, 1 - y), (1 - x, 1 - y)]

        def owned(a, chip):
            return _axis_slice(src[a], shard_axes[a], 2 * chip[0] + chip[1], src[a].shape[shard_axes[a]] // 4)

        def remote(a, j, to, from_chip):
            return pltpu.make_async_remote_copy(
                src_ref=owned(a, to), dst_ref=dst[a].at[from_chip], send_sem=send_sems.at[a, j],
                recv_sem=recv_sems.at[a, j], device_id=(to[0], to[1], c), device_id_type=MESH)

        def peer(f):
            return (1 - x if f[0] else x, 1 - y if f[1] else y, 1 - c if f[2] else c)

        def premote(j, to, from_dev):
            return pltpu.make_async_remote_copy(
                src_ref=psrc, dst_ref=pdst.at[from_dev], send_sem=psend_sems.at[j], recv_sem=precv_sems.at[j],
                device_id=to, device_id_type=MESH)

        local = [pltpu.make_async_copy(_rows(owned(a, (x, y)), r0, rows), _rows(dst[a].at[my_chip], r0, rows),
                                       local_sems.at[k]) for k, (a, r0, rows) in enumerate(own_pieces)]
        local.append(pltpu.make_async_copy(psrc, pdst.at[4 * x + 2 * y + c], local_sems.at[len(own_pieces)]))
        sends = [remote(a, j, chip, my_chip) for a in range(n) for j, chip in enumerate(chips)]
        sends += [premote(j, peer(f), 4 * x + 2 * y + c) for j, f in enumerate(flips)]
        recvs = [remote(a, j, (x, y), 2 * chip[0] + chip[1]) for a in range(n) for j, chip in enumerate(chips)]
        for j, f in enumerate(flips):
            px, py, pc = peer(f)
            recvs.append(premote(j, (x, y, c), 4 * px + 2 * py + pc))
        _run_copies(local, sends, recvs)

    own_pieces = [(a, r0, rows) for a in range(n) for r0, rows in _row_pieces(owned_shape(a))]
    any_spec = pl.BlockSpec(memory_space=pl.ANY)
    return pl.pallas_call(
        body, name="send_to_owners",
        in_specs=[any_spec] * (n + 1), out_specs=[any_spec] * (n + 1),
        out_shape=[S((4,) + owned_shape(a), chip_halves[a].dtype) for a in range(n)]
        + [S((N_DEV,) + packed.shape, packed.dtype)],
        scratch_shapes=[pltpu.SemaphoreType.DMA((n, 3)), pltpu.SemaphoreType.DMA((n, 3)),
                        pltpu.SemaphoreType.DMA((len(own_pieces) + 1,)), pltpu.SemaphoreType.DMA((7,)),
                        pltpu.SemaphoreType.DMA((7,))],
    )(*chip_halves, packed)


def _join_halves(reduced, half_axes):
    n = len(reduced)

    def joined_shape(a):
        s, ax = reduced[a].shape, half_axes[a]
        return s[:ax] + (2,) + s[ax:]

    pieces = [(a, r0, rows) for a in range(n) for r0, rows in _row_pieces(reduced[a].shape)]

    def body(*refs):
        src, dst = refs[0:n], refs[n:2 * n]
        send_sems, recv_sems, in_sems, out_sems = refs[2 * n:2 * n + 4]
        bufs = refs[2 * n + 4:3 * n + 4]
        x, y, c = lax.axis_index("x"), lax.axis_index("y"), lax.axis_index("c")

        def remote(k, h):
            a, r0, rows = pieces[k]
            return pltpu.make_async_remote_copy(
                src_ref=_rows(src[a], r0, rows), dst_ref=_rows(_axis_index(dst[a], half_axes[a], h), r0, rows),
                send_sem=send_sems.at[k], recv_sem=recv_sems.at[k], device_id=(x, y, 1 - c), device_id_type=MESH)

        sends = [remote(k, c) for k in range(len(pieces))]
        for cp in sends:
            cp.start()
        own = [(src[a], _axis_index(dst[a], half_axes[a], c)) for a in range(n)]
        local = _staged_local_copies(own, bufs, in_sems, out_sems)
        for k in range(len(pieces)):
            remote(k, 1 - c).wait_recv()
        for cp in sends:
            cp.wait_send()
        for cp in local:
            cp.wait()

    any_spec = pl.BlockSpec(memory_space=pl.ANY)
    return pl.pallas_call(
        body, name="join_halves",
        in_specs=[any_spec] * n, out_specs=[any_spec] * n,
        out_shape=[S(joined_shape(a), reduced[a].dtype) for a in range(n)],
        scratch_shapes=[pltpu.SemaphoreType.DMA((len(pieces),))] * 2 + [pltpu.SemaphoreType.DMA((n,))] * 2
        + [pltpu.VMEM(r.shape, r.dtype) for r in reduced],
        compiler_params=_params(0),
    )(*reduced)


def _sum_slots(name, parts):
    n = parts.shape[0]
    shape = parts.shape[1:]
    flat = parts.reshape((n, -1, shape[-1]))
    rows, cols = flat.shape[1:]
    rb = rows
    while rb * cols * 4 * (n + 1) * 2 > 24 * 1024 * 1024 and rb % 32 == 0:
        rb //= 2

    def body(p_ref, o_ref):
        acc = p_ref[0].astype(F32)
        for k in range(1, n):
            acc = acc + p_ref[k].astype(F32)
        o_ref[...] = acc

    out = pl.pallas_call(
        body, name=name,
        grid=(rows // rb,),
        in_specs=[pl.BlockSpec((n, rb, cols), lambda i: (0, i, 0))],
        out_specs=pl.BlockSpec((rb, cols), lambda i: (i, 0)),
        out_shape=S((rows, cols), F32),
        compiler_params=_params(1),
    )(flat)
    return out.reshape(shape)


SMALL_NAMES = ("norm_pre", "pool_w", "pool_b", "pool_scale", "conv_b", "conv_ln_g", "conv_ln_b", "norm_post")
BIG_NAMES = ("w_in", "w_pool_out", "conv_w", "w_conv_out", "w_attn_out", "w_o")
SHARD_AXIS = {"w_in": 2, "w_pool_out": 2, "conv_w": 2, "w_conv_out": 2, "w_attn_out": 2, "w_o": 1}
WEIGHT_ORDER = ("norm_pre", "w_in", "pool_w", "pool_b", "pool_scale", "w_pool_out", "conv_w", "conv_b",
                "conv_ln_g", "conv_ln_b", "w_conv_out", "w_attn_out", "w_o", "norm_post")


def _pack_small(parts):
    return jnp.concatenate([parts[n].reshape(-1, LANES) for n in SMALL_NAMES], axis=0)


def _unpack_small(packed, shapes):
    out, r0 = {}, 0
    for n in SMALL_NAMES:
        size = 1
        for d in shapes[n]:
            size *= d
        rows = size // LANES
        out[n] = packed[r0:r0 + rows].reshape(shapes[n])
        r0 += rows
    return out


def _layer_weights(full, small, l):
    row = lambda a: a[l][None, :]
    return (small["pool_w"][l].astype(BF16), small["pool_b"][l].reshape(1, UNIT), row(small["pool_scale"]),
            full["w_pool_out"][l], full["conv_w"][l], row(small["conv_b"]), row(small["conv_ln_g"]),
            row(small["conv_ln_b"]), full["w_conv_out"][l], full["w_attn_out"][l], full["w_o"][l],
            row(small["norm_post"]))


def _forward_backward(x, target, full, small):
    depth = full["w_in"].shape[0]
    acts = []
    for l in range(depth):
        g_pre = small["norm_pre"][l][None, :]
        proj, qkv = _inproj_fwd(x, g_pre, full["w_in"], l)
        o, bsum, kfirst = _attn_fwd(qkv)
        weights = _layer_weights(full, small, l)
        x_next, cv = _mix_fwd(proj, o, x, weights)
        acts.append((x, proj, qkv, o, cv, bsum, kfirst, weights, g_pre))
        x = x_next
    loss_tile, dx = _loss_head(x, target)

    big = {n: [None] * depth for n in BIG_NAMES}
    sm = {n: [None] * depth for n in SMALL_NAMES}
    for l in reversed(range(depth)):
        x_in, proj, qkv, o, cv, bsum, kfirst, weights, g_pre = acts[l]
        dgates, dd, dcv, d_o, dwo, dwp, dwc, dwa, dpw, dvec = _mix_bwd(proj, o, cv, x_in, dx, weights)
        dpre, dcw = _halo_bwd(proj, dcv, dd, full["conv_w"][l])
        dq, dk, dv = _attn_bwd(qkv, d_o, bsum, kfirst)
        dproj_parts = (dpre, dgates, dq, dk, dv)
        dx, dg_pre = _inproj_bwd_x(x_in, g_pre, full["w_in"], l, dproj_parts, dx)
        big["w_in"][l] = _inproj_bwd_w(x_in, g_pre, dproj_parts)
        big["w_pool_out"][l], big["w_conv_out"][l], big["w_attn_out"][l], big["w_o"][l] = dwp, dwc, dwa, dwo
        big["conv_w"][l] = dcw
        sm["norm_pre"][l] = dg_pre[0]
        sm["pool_w"][l] = dpw
        sm["norm_post"][l] = dvec[0]
        sm["pool_b"][l] = dvec[1, 0:UNIT].reshape(4, GROUP)
        sm["pool_scale"][l] = dvec[2, 0:UNIT]
        sm["conv_b"][l] = dvec[3, 0:UNIT]
        sm["conv_ln_g"][l] = dvec[4, 0:UNIT]
        sm["conv_ln_b"][l] = dvec[5, 0:UNIT]
    sm = {n: jnp.stack(v) for n, v in sm.items()}
    return loss_tile, dx, big, sm


def kernel(x, norm_pre, w_in, pool_w, pool_b, pool_scale, w_pool_out, conv_w, conv_b, conv_ln_g, conv_ln_b, w_conv_out, w_attn_out, w_o, norm_post, loss_target, m_norm_pre, m_w_in, m_pool_w, m_pool_b, m_pool_scale, m_w_pool_out, m_conv_w, m_conv_b, m_conv_ln_g, m_conv_ln_b, m_w_conv_out, m_w_attn_out, m_w_o, m_norm_post, v_norm_pre, v_w_in, v_pool_w, v_pool_b, v_pool_scale, v_w_pool_out, v_conv_w, v_conv_b, v_conv_ln_g, v_conv_ln_b, v_w_conv_out, v_w_attn_out, v_w_o, v_norm_post):
    w = dict(norm_pre=norm_pre, w_in=w_in, pool_w=pool_w, pool_b=pool_b, pool_scale=pool_scale,
             w_pool_out=w_pool_out, conv_w=conv_w, conv_b=conv_b, conv_ln_g=conv_ln_g, conv_ln_b=conv_ln_b,
             w_conv_out=w_conv_out, w_attn_out=w_attn_out, w_o=w_o, norm_post=norm_post)
    m = dict(norm_pre=m_norm_pre, w_in=m_w_in, pool_w=m_pool_w, pool_b=m_pool_b, pool_scale=m_pool_scale,
             w_pool_out=m_w_pool_out, conv_w=m_conv_w, conv_b=m_conv_b, conv_ln_g=m_conv_ln_g,
             conv_ln_b=m_conv_ln_b, w_conv_out=m_w_conv_out, w_attn_out=m_w_attn_out, w_o=m_w_o,
             norm_post=m_norm_post)
    v = dict(norm_pre=v_norm_pre, w_in=v_w_in, pool_w=v_pool_w, pool_b=v_pool_b, pool_scale=v_pool_scale,
             w_pool_out=v_w_pool_out, conv_w=v_conv_w, conv_b=v_conv_b, conv_ln_g=v_conv_ln_g,
             conv_ln_b=v_conv_ln_b, w_conv_out=v_w_conv_out, w_attn_out=v_w_attn_out, w_o=v_w_o,
             norm_post=v_norm_post)
    pad_taps = lambda a: jnp.pad(a, ((0, 0), (0, CONV_K_PAD - CONV_K), (0, 0)))

    shards = [pad_taps(w[n]) if n == "conv_w" else w[n].astype(BF16) for n in BIG_NAMES]
    full = dict(zip(BIG_NAMES, _gather_weights(shards, [SHARD_AXIS[n] for n in BIG_NAMES])))
    small = {n: w[n] for n in SMALL_NAMES}

    loss_tile, dx, big, sm = _forward_backward(x[0], loss_target[0], full, small)

    packed = jnp.concatenate([_pack_small(sm), loss_tile], axis=0)
    layer_axis = [0] * len(BIG_NAMES)
    theirs = _swap_layers([big[n] for n in BIG_NAMES])
    my_layer = lax.axis_index("c").astype(jnp.int32).reshape(1)
    chip_sums = [_add_own_layer("sum_cores_" + n, big[n], t, my_layer) for n, t in zip(BIG_NAMES, theirs)]
    exchanged = _send_to_owners(chip_sums, [SHARD_AXIS[n] for n in BIG_NAMES], packed)
    reduced = [_sum_slots("sum_chips_" + n, p) for n, p in zip(BIG_NAMES, exchanged[:-1])]
    joined = _join_halves(reduced, layer_axis)
    out = {}
    for n, g in zip(BIG_NAMES, joined):
        wn, mn, vn = (pad_taps(a[n]) if n == "conv_w" else a[n] for a in (w, m, v))
        shape = wn.shape
        flat = lambda a: a.reshape((-1, shape[-1]))
        res = _adamw("adamw_" + n, g.reshape((1, -1, shape[-1])), flat(wn), flat(mn), flat(vn))
        res = [r.reshape(shape) for r in res]
        out[n] = [r[:, :CONV_K] for r in res] if n == "conv_w" else res
    parts = exchanged[-1]
    n_small_rows = parts.shape[1] - 8
    zeros_tile = jnp.zeros((8, LANES), F32)
    packs = [jnp.concatenate([_pack_small(a), zeros_tile], axis=0) for a in (w, m, v)]
    res = _adamw("adamw_small", parts, *packs)
    loss = res[0][n_small_rows, 0]
    shapes = {n: w[n].shape for n in SMALL_NAMES}
    unpacked = [_unpack_small(r[:n_small_rows], shapes) for r in res]
    for n in SMALL_NAMES:
        out[n] = [u[n] for u in unpacked]

    grads = [out[n][0] for n in WEIGHT_ORDER]
    deltas = [out[n][1] for n in WEIGHT_ORDER]
    new_m = [out[n][2] for n in WEIGHT_ORDER]
    new_v = [out[n][3] for n in WEIGHT_ORDER]
    return (loss, dx[None], *grads, *deltas, *new_m, *new_v)
```
